```python
import jax, jax.numpy as jnp
from jax import lax
import numpy as np

D_MODEL = 2048
BATCH = 8
SEQ = 4096
DEPTH = 1

GRID_W = 64
CTX_LEN = 256
HEAD_DIM = 128
N_Q_HEADS = 16
N_KV_HEADS = 4
GQA_GROUP = N_Q_HEADS // N_KV_HEADS
ATTN_W = N_Q_HEADS * HEAD_DIM
KV_W = N_KV_HEADS * HEAD_DIM
LRU_W = D_MODEL
LRU_BLOCKS = 16
LRU_BLOCK_DIM = LRU_W // LRU_BLOCKS
LRU_C = 8.0
CONV_W = 4
CONV_LEFT = 2
D_FF = 5632
Q_BLOCK = 128
ROPE_THETA = 10000.0
EPS = 1e-6
N_MOD = 9
FFN_RES = 0.5
OFF_Q = 0
OFF_K = OFF_Q + ATTN_W
OFF_V = OFF_K + KV_W
OFF_LX = OFF_V + KV_W
OFF_LG = OFF_LX + LRU_W
OFF_GA = OFF_LG + LRU_W
OFF_GL = OFF_GA + D_MODEL
IN_W = OFF_GL + D_MODEL

kernel_name = 'hybrid_gqa_rglru_macaron_dit_layer'


def rms_norm(t, g):
    tf = t.astype(jnp.float32)
    y = tf * lax.rsqrt(jnp.mean(tf * tf, axis=-1, keepdims=True) + EPS)
    return (y * g.astype(jnp.float32)).astype(t.dtype)


def modulate(h, shift, scale):
    return h * (1.0 + scale) + shift


def swiglu(h, wg, wu, wd):
    return (jax.nn.silu(h @ wg) * (h @ wu)) @ wd


def axial_rope_tables(n_tok):
    rows = n_tok // GRID_W
    row = jnp.repeat(jnp.arange(rows, dtype=jnp.float32), GRID_W)
    col = jnp.tile(jnp.arange(GRID_W, dtype=jnp.float32), rows)
    axis_dims = HEAD_DIM // 2
    freqs = ROPE_THETA ** (-jnp.arange(0, axis_dims, 2, dtype=jnp.float32) / axis_dims)
    ang = jnp.concatenate([row[:, None] * freqs, col[:, None] * freqs], axis=-1)
    return jnp.cos(ang), jnp.sin(ang)


def apply_rope(t, cos, sin):
    tf = t.astype(jnp.float32).reshape(t.shape[:-1] + (HEAD_DIM // 2, 2))
    t1, t2 = tf[..., 0], tf[..., 1]
    out = jnp.stack([t1 * cos - t2 * sin, t1 * sin + t2 * cos], axis=-1)
    return out.reshape(t.shape).astype(t.dtype)


def to_heads(t, n_heads):
    b, n, _ = t.shape
    return t.reshape(b, n, n_heads, HEAD_DIM).transpose(0, 2, 1, 3)


def group_queries(q):
    b, _, n, _ = q.shape
    return q.reshape(b, N_KV_HEADS, GQA_GROUP, n, HEAD_DIM)


def latent_attention(q, k_lat, v_lat, k_ctx, v_ctx):
    b, _, _, n, _ = q.shape
    k_all = jnp.concatenate([k_ctx, k_lat], axis=2)
    v_all = jnp.concatenate([v_ctx, v_lat], axis=2)
    n_blk = n // Q_BLOCK
    qb = jnp.moveaxis(q.reshape(b, N_KV_HEADS, GQA_GROUP, n_blk, Q_BLOCK, HEAD_DIM), 3, 0)
    scale = HEAD_DIM ** -0.5

    def one_block(q_blk):
        s = jnp.einsum('bkgqd,bksd->bkgqs', q_blk, k_all, preferred_element_type=jnp.float32) * scale
        p = jax.nn.softmax(s, axis=-1)
        return jnp.einsum('bkgqs,bksd->bkgqd', p.astype(v_all.dtype), v_all)

    ob = lax.map(one_block, qb)
    return ob.transpose(1, 0, 4, 2, 3, 5).reshape(b, n, ATTN_W)


def context_attention(q, k, v):
    b, _, _, n, _ = q.shape
    s = jnp.einsum('bkgqd,bksd->bkgqs', q, k, preferred_element_type=jnp.float32) * (HEAD_DIM ** -0.5)
    p = jax.nn.softmax(s, axis=-1)
    o = jnp.einsum('bkgqs,bksd->bkgqd', p.astype(v.dtype), v)
    return o.transpose(0, 3, 1, 2, 4).reshape(b, n, ATTN_W)


def centred_dwconv(t, w, b):
    n = t.shape[1]
    tp = jnp.pad(t, ((0, 0), (CONV_LEFT, CONV_W - 1 - CONV_LEFT), (0, 0)))
    out = b
    for k in range(CONV_W):
        out = out + tp[:, k:k + n] * w[k]
    return out


def block_diag(t, w, b):
    tb = t.reshape(t.shape[:-1] + (LRU_BLOCKS, LRU_BLOCK_DIM))
    return jnp.einsum('btnd,nde->btne', tb, w).reshape(t.shape) + b


def rglru_coeffs(xc, w_a, b_a, w_x, b_x, lam):
    xf = xc.astype(jnp.float32)
    r = jax.nn.sigmoid(block_diag(xf, w_a, b_a).astype(jnp.float32))
    i = jax.nn.sigmoid(block_diag(xf, w_x, b_x).astype(jnp.float32))
    log_a = -LRU_C * r * jax.nn.softplus(-lam.astype(jnp.float32))
    a = jnp.exp(log_a)
    u = jnp.sqrt(-jnp.expm1(2.0 * log_a)) * (i * xf)
    return a, u


def linear_scan(a, u, h0, reverse):
    def combine(e1, e2):
        a1, b1 = e1
        a2, b2 = e2
        return a1 * a2, a2 * b1 + b2
    a_cum, b_cum = lax.associative_scan(combine, (a, u), axis=1, reverse=reverse)
    return a_cum * h0[:, None, :] + b_cum


def rglru_bidir(xc, h0_f, h0_b, wa, ba, wx, bx, lam):
    a_f, u_f = rglru_coeffs(xc, wa[0], ba[0], wx[0], bx[0], lam[0])
    a_b, u_b = rglru_coeffs(xc, wa[1], ba[1], wx[1], bx[1], lam[1])
    return linear_scan(a_f, u_f, h0_f, False), linear_scan(a_b, u_b, h0_b, True)


def gated_lru_out(h_f, h_b, gate, dtype):
    return ((h_f + h_b) * jax.nn.gelu(gate.astype(jnp.float32))).astype(dtype)


def merge_branches(attn, lru, ga, gl, w_out):
    return (jax.nn.sigmoid(ga) * attn + jax.nn.sigmoid(gl) * lru) @ w_out


def hybrid_layer(x, ctx, c, c_ctx, w_mod, b_mod, norm_g, ffn_wg, ffn_wu, ffn_wd, w_in, w_out,
                 q_norm_g, k_norm_g, conv_w, conv_b, lru_wa, lru_ba, lru_wx, lru_bx, lru_lambda,
                 cos, sin, update_ctx):
    b = x.shape[0]
    mod_x = (jax.nn.silu(c) @ w_mod + b_mod).reshape(b, N_MOD, 1, D_MODEL)
    mod_c = (jax.nn.silu(c_ctx) @ w_mod + b_mod).reshape(N_MOD, D_MODEL)
    sh1, sc1, g1, sh2, sc2, g2, sh3, sc3, g3 = [mod_x[:, i] for i in range(N_MOD)]
    csh1, csc1, cg1, csh2, csc2, cg2, csh3, csc3, cg3 = [mod_c[i] for i in range(N_MOD)]

    x = x + FFN_RES * g1 * swiglu(modulate(rms_norm(x, norm_g[0]), sh1, sc1), ffn_wg[0], ffn_wu[0], ffn_wd[0])
    ctx = ctx + FFN_RES * cg1 * swiglu(modulate(rms_norm(ctx, norm_g[0]), csh1, csc1), ffn_wg[0], ffn_wu[0], ffn_wd[0])

    hx = modulate(rms_norm(x, norm_g[1]), sh2, sc2)
    hc = modulate(rms_norm(ctx, norm_g[1]), csh2, csc2)

    pc = hc @ w_in[:, OFF_K:OFF_LG]
    k_c = rms_norm(to_heads(pc[..., :KV_W], N_KV_HEADS), k_norm_g)
    v_c = to_heads(pc[..., KV_W:2 * KV_W], N_KV_HEADS)
    xc_c = centred_dwconv(pc[..., 2 * KV_W:], conv_w, conv_b)
    zeros = jnp.zeros((b, LRU_W), jnp.float32)
    hf_c, hb_c = rglru_bidir(xc_c, zeros, zeros, lru_wa, lru_ba, lru_wx, lru_bx, lru_lambda)

    p = hx @ w_in
    q = apply_rope(rms_norm(to_heads(p[..., OFF_Q:OFF_K], N_Q_HEADS), q_norm_g), cos, sin)
    k = apply_rope(rms_norm(to_heads(p[..., OFF_K:OFF_V], N_KV_HEADS), k_norm_g), cos, sin)
    v = to_heads(p[..., OFF_V:OFF_LX], N_KV_HEADS)
    attn = latent_attention(group_queries(q), k, v, k_c, v_c)
    xc = centred_dwconv(p[..., OFF_LX:OFF_LG], conv_w, conv_b)
    hf, hb = rglru_bidir(xc, hf_c[:, -1], hb_c[:, 0], lru_wa, lru_ba, lru_wx, lru_bx, lru_lambda)
    lru = gated_lru_out(hf, hb, p[..., OFF_LG:OFF_GA], x.dtype)
    x = x + g2 * merge_branches(attn, lru, p[..., OFF_GA:OFF_GL], p[..., OFF_GL:], w_out)

    if update_ctx:
        pq_c = hc @ w_in[:, OFF_Q:OFF_K]
        pg_c = hc @ w_in[:, OFF_LG:]
        q_c = rms_norm(to_heads(pq_c, N_Q_HEADS), q_norm_g)
        attn_c = context_attention(group_queries(q_c), k_c, v_c)
        lru_c = gated_lru_out(hf_c, hb_c, pg_c[..., :LRU_W], ctx.dtype)
        ctx = ctx + cg2 * merge_branches(attn_c, lru_c, pg_c[..., LRU_W:LRU_W + D_MODEL],
                                         pg_c[..., LRU_W + D_MODEL:], w_out)

    x = x + FFN_RES * g3 * swiglu(modulate(rms_norm(x, norm_g[2]), sh3, sc3), ffn_wg[1], ffn_wu[1], ffn_wd[1])
    if update_ctx:
        ctx = ctx + FFN_RES * cg3 * swiglu(modulate(rms_norm(ctx, norm_g[2]), csh3, csc3), ffn_wg[1], ffn_wu[1], ffn_wd[1])
    return x, ctx


def _fwd_setup_inputs(seed: int = 0) -> dict:
    key = jax.random.key(seed)
    ks = jax.random.split(key, 24)
    f32 = jnp.float32

    def nrm(k, shape, scale):
        return jax.random.normal(k, shape, f32) * scale

    u = jax.random.uniform(ks[20], (DEPTH, 2, LRU_W), f32, 0.9, 0.999)
    base = u ** (1.0 / LRU_C)
    return {
        'x': nrm(ks[0], (BATCH, SEQ, D_MODEL), 1.0),
        'c': nrm(ks[1], (BATCH, D_MODEL), 1.0),
        'ctx': nrm(ks[2], (BATCH, CTX_LEN, D_MODEL), 1.0),
        'c_ctx': nrm(ks[3], (D_MODEL,), 1.0),
        'w_mod': nrm(ks[4], (DEPTH, D_MODEL, N_MOD * D_MODEL), 0.5 * D_MODEL ** -0.5),
        'b_mod': nrm(ks[5], (DEPTH, N_MOD * D_MODEL), 0.01),
        'norm_g': 1.0 + nrm(ks[6], (DEPTH, 3, D_MODEL), 0.02),
        'ffn_wg': nrm(ks[7], (DEPTH, 2, D_MODEL, D_FF), D_MODEL ** -0.5),
        'ffn_wu': nrm(ks[8], (DEPTH, 2, D_MODEL, D_FF), D_MODEL ** -0.5),
        'ffn_wd': nrm(ks[9], (DEPTH, 2, D_FF, D_MODEL), D_FF ** -0.5),
        'w_in': nrm(ks[10], (DEPTH, D_MODEL, IN_W), D_MODEL ** -0.5),
        'w_out': nrm(ks[11], (DEPTH, D_MODEL, D_MODEL), D_MODEL ** -0.5),
        'q_norm_g': 1.0 + nrm(ks[12], (DEPTH, HEAD_DIM), 0.02),
        'k_norm_g': 1.0 + nrm(ks[13], (DEPTH, HEAD_DIM), 0.02),
        'conv_w': nrm(ks[14], (DEPTH, CONV_W, LRU_W), CONV_W ** -0.5),
        'conv_b': nrm(ks[15], (DEPTH, LRU_W), 0.01),
        'lru_wa': nrm(ks[16], (DEPTH, 2, LRU_BLOCKS, LRU_BLOCK_DIM, LRU_BLOCK_DIM), LRU_BLOCK_DIM ** -0.5),
        'lru_ba': nrm(ks[17], (DEPTH, 2, LRU_W), 0.01),
        'lru_wx': nrm(ks[18], (DEPTH, 2, LRU_BLOCKS, LRU_BLOCK_DIM, LRU_BLOCK_DIM), LRU_BLOCK_DIM ** -0.5),
        'lru_bx': nrm(ks[19], (DEPTH, 2, LRU_W), 0.01),
        'lru_lambda': jnp.log(base) - jnp.log1p(-base),
        'final_norm_g': 1.0 + nrm(ks[21], (D_MODEL,), 0.02),
    }


def _fwd_reference(x, c, ctx, c_ctx, w_mod, b_mod, norm_g, ffn_wg, ffn_wu, ffn_wd, w_in, w_out,
              q_norm_g, k_norm_g, conv_w, conv_b, lru_wa, lru_ba, lru_wx, lru_bx, lru_lambda,
              final_norm_g):
    cos, sin = axial_rope_tables(x.shape[1])
    for l in range(DEPTH):
        x, ctx = hybrid_layer(x, ctx, c, c_ctx, w_mod[l], b_mod[l], norm_g[l], ffn_wg[l], ffn_wu[l], ffn_wd[l],
                              w_in[l], w_out[l], q_norm_g[l], k_norm_g[l], conv_w[l], conv_b[l],
                              lru_wa[l], lru_ba[l], lru_wx[l], lru_bx[l], lru_lambda[l],
                              cos, sin, l < DEPTH - 1)
    return rms_norm(x, final_norm_g)


import jax as _jax
import jax.numpy as _jnp

TWIN_FORMAT = 'train_step'
FWD_PARAMS = ['x', 'c', 'ctx', 'c_ctx', 'w_mod', 'b_mod', 'norm_g', 'ffn_wg', 'ffn_wu', 'ffn_wd', 'w_in', 'w_out', 'q_norm_g', 'k_norm_g', 'conv_w', 'conv_b', 'lru_wa', 'lru_ba', 'lru_wx', 'lru_bx', 'lru_lambda', 'final_norm_g']
TWIN_WEIGHTS = ['c_ctx', 'w_mod', 'b_mod', 'norm_g', 'ffn_wg', 'ffn_wu', 'ffn_wd', 'w_in', 'w_out', 'q_norm_g', 'k_norm_g', 'conv_w', 'conv_b', 'lru_wa', 'lru_ba', 'lru_wx', 'lru_bx', 'lru_lambda', 'final_norm_g']
TWIN_DIFF_INPUT = 'x'
TWIN_INPUTS = ['x', 'c', 'ctx', 'c_ctx', 'w_mod', 'b_mod', 'norm_g', 'ffn_wg', 'ffn_wu', 'ffn_wd', 'w_in', 'w_out', 'q_norm_g', 'k_norm_g', 'conv_w', 'conv_b', 'lru_wa', 'lru_ba', 'lru_wx', 'lru_bx', 'lru_lambda', 'final_norm_g', 'loss_target', 'm_c_ctx', 'm_w_mod', 'm_b_mod', 'm_norm_g', 'm_ffn_wg', 'm_ffn_wu', 'm_ffn_wd', 'm_w_in', 'm_w_out', 'm_q_norm_g', 'm_k_norm_g', 'm_conv_w', 'm_conv_b', 'm_lru_wa', 'm_lru_ba', 'm_lru_wx', 'm_lru_bx', 'm_lru_lambda', 'm_final_norm_g', 'v_c_ctx', 'v_w_mod', 'v_b_mod', 'v_norm_g', 'v_ffn_wg', 'v_ffn_wu', 'v_ffn_wd', 'v_w_in', 'v_w_out', 'v_q_norm_g', 'v_k_norm_g', 'v_conv_w', 'v_conv_b', 'v_lru_wa', 'v_lru_ba', 'v_lru_wx', 'v_lru_bx', 'v_lru_lambda', 'v_final_norm_g']
TWIN_OUTPUTS = ['loss', 'grad_x', 'grad_c_ctx', 'grad_w_mod', 'grad_b_mod', 'grad_norm_g', 'grad_ffn_wg', 'grad_ffn_wu', 'grad_ffn_wd', 'grad_w_in', 'grad_w_out', 'grad_q_norm_g', 'grad_k_norm_g', 'grad_conv_w', 'grad_conv_b', 'grad_lru_wa', 'grad_lru_ba', 'grad_lru_wx', 'grad_lru_bx', 'grad_lru_lambda', 'grad_final_norm_g', 'delta_c_ctx', 'delta_w_mod', 'delta_b_mod', 'delta_norm_g', 'delta_ffn_wg', 'delta_ffn_wu', 'delta_ffn_wd', 'delta_w_in', 'delta_w_out', 'delta_q_norm_g', 'delta_k_norm_g', 'delta_conv_w', 'delta_conv_b', 'delta_lru_wa', 'delta_lru_ba', 'delta_lru_wx', 'delta_lru_bx', 'delta_lru_lambda', 'delta_final_norm_g', 'new_m_c_ctx', 'new_m_w_mod', 'new_m_b_mod', 'new_m_norm_g', 'new_m_ffn_wg', 'new_m_ffn_wu', 'new_m_ffn_wd', 'new_m_w_in', 'new_m_w_out', 'new_m_q_norm_g', 'new_m_k_norm_g', 'new_m_conv_w', 'new_m_conv_b', 'new_m_lru_wa', 'new_m_lru_ba', 'new_m_lru_wx', 'new_m_lru_bx', 'new_m_lru_lambda', 'new_m_final_norm_g', 'new_v_c_ctx', 'new_v_w_mod', 'new_v_b_mod', 'new_v_norm_g', 'new_v_ffn_wg', 'new_v_ffn_wu', 'new_v_ffn_wd', 'new_v_w_in', 'new_v_w_out', 'new_v_q_norm_g', 'new_v_k_norm_g', 'new_v_conv_w', 'new_v_conv_b', 'new_v_lru_wa', 'new_v_lru_ba', 'new_v_lru_wx', 'new_v_lru_bx', 'new_v_lru_lambda', 'new_v_final_norm_g']
TWIN_LEAF_KINDS = {'loss': 'loss', 'grad_x': 'grad_x', 'grad_c_ctx': 'grad_w', 'grad_w_mod': 'grad_w', 'grad_b_mod': 'grad_w', 'grad_norm_g': 'grad_w', 'grad_ffn_wg': 'grad_w', 'grad_ffn_wu': 'grad_w', 'grad_ffn_wd': 'grad_w', 'grad_w_in': 'grad_w', 'grad_w_out': 'grad_w', 'grad_q_norm_g': 'grad_w', 'grad_k_norm_g': 'grad_w', 'grad_conv_w': 'grad_w', 'grad_conv_b': 'grad_w', 'grad_lru_wa': 'grad_w', 'grad_lru_ba': 'grad_w', 'grad_lru_wx': 'grad_w', 'grad_lru_bx': 'grad_w', 'grad_lru_lambda': 'grad_w', 'grad_final_norm_g': 'grad_w', 'delta_c_ctx': 'delta_w', 'delta_w_mod': 'delta_w', 'delta_b_mod': 'delta_w', 'delta_norm_g': 'delta_w', 'delta_ffn_wg': 'delta_w', 'delta_ffn_wu': 'delta_w', 'delta_ffn_wd': 'delta_w', 'delta_w_in': 'delta_w', 'delta_w_out': 'delta_w', 'delta_q_norm_g': 'delta_w', 'delta_k_norm_g': 'delta_w', 'delta_conv_w': 'delta_w', 'delta_conv_b': 'delta_w', 'delta_lru_wa': 'delta_w', 'delta_lru_ba': 'delta_w', 'delta_lru_wx': 'delta_w', 'delta_lru_bx': 'delta_w', 'delta_lru_lambda': 'delta_w', 'delta_final_norm_g': 'delta_w', 'new_m_c_ctx': 'new_m', 'new_m_w_mod': 'new_m', 'new_m_b_mod': 'new_m', 'new_m_norm_g': 'new_m', 'new_m_ffn_wg': 'new_m', 'new_m_ffn_wu': 'new_m', 'new_m_ffn_wd': 'new_m', 'new_m_w_in': 'new_m', 'new_m_w_out': 'new_m', 'new_m_q_norm_g': 'new_m', 'new_m_k_norm_g': 'new_m', 'new_m_conv_w': 'new_m', 'new_m_conv_b': 'new_m', 'new_m_lru_wa': 'new_m', 'new_m_lru_ba': 'new_m', 'new_m_lru_wx': 'new_m', 'new_m_lru_bx': 'new_m', 'new_m_lru_lambda': 'new_m', 'new_m_final_norm_g': 'new_m', 'new_v_c_ctx': 'new_v', 'new_v_w_mod': 'new_v', 'new_v_b_mod': 'new_v', 'new_v_norm_g': 'new_v', 'new_v_ffn_wg': 'new_v', 'new_v_ffn_wu': 'new_v', 'new_v_ffn_wd': 'new_v', 'new_v_w_in': 'new_v', 'new_v_w_out': 'new_v', 'new_v_q_norm_g': 'new_v', 'new_v_k_norm_g': 'new_v', 'new_v_conv_w': 'new_v', 'new_v_conv_b': 'new_v', 'new_v_lru_wa': 'new_v', 'new_v_lru_ba': 'new_v', 'new_v_lru_wx': 'new_v', 'new_v_lru_bx': 'new_v', 'new_v_lru_lambda': 'new_v', 'new_v_final_norm_g': 'new_v'}


def _forward(args):
    return _fwd_reference(*[args[k] for k in FWD_PARAMS])


def _output_shape():
    def fwd():
        inp = _fwd_setup_inputs(0)
        return _fwd_reference(*[inp[k] for k in FWD_PARAMS])
    out = _jax.eval_shape(fwd)
    return out.shape, out.dtype

N_MICROBATCH = 1
ADAM_LR = 0.001
ADAM_B1 = 0.9
ADAM_B2 = 0.999
ADAM_EPS = 1e-08
ADAM_WD = 0.01
ADAM_STEP = 10
PER_EXAMPLE_BATCH_AXIS = {'x': 0, 'c': 0, 'ctx': 0, 'loss_target': 0}
SHARED_INPUTS = []
_WEIGHT_DTYPES = {'c_ctx': _jnp.float32, 'w_mod': _jnp.float32, 'b_mod': _jnp.float32, 'norm_g': _jnp.float32, 'ffn_wg': _jnp.float32, 'ffn_wu': _jnp.float32, 'ffn_wd': _jnp.float32, 'w_in': _jnp.float32, 'w_out': _jnp.float32, 'q_norm_g': _jnp.float32, 'k_norm_g': _jnp.float32, 'conv_w': _jnp.float32, 'conv_b': _jnp.float32, 'lru_wa': _jnp.float32, 'lru_ba': _jnp.float32, 'lru_wx': _jnp.float32, 'lru_bx': _jnp.float32, 'lru_lambda': _jnp.float32, 'final_norm_g': _jnp.float32}
MOMENT_SCALE = {'c_ctx': 1.059136e-02, 'w_mod': 4.169110e-02, 'b_mod': 7.173636e-02, 'norm_g': 2.430433e-02, 'ffn_wg': 5.768822e-03, 'ffn_wu': 5.596310e-03, 'ffn_wd': 9.285078e-03, 'w_in': 2.387036e-02, 'w_out': 3.962539e-02, 'q_norm_g': 4.161553e-03, 'k_norm_g': 4.020885e-03, 'conv_w': 3.733524e-02, 'conv_b': 1.164171e-01, 'lru_wa': 2.336386e-03, 'lru_ba': 3.259509e-03, 'lru_wx': 4.603851e-03, 'lru_bx': 7.518506e-03, 'lru_lambda': 8.067700e-03, 'final_norm_g': 1.604736e+01}


def _to_microbatches(a, axis):
    t = _jnp.moveaxis(a, axis, 0)
    t = t.reshape((N_MICROBATCH, t.shape[0] // N_MICROBATCH) + t.shape[1:])
    return _jnp.moveaxis(t, 1, axis + 1)


def setup_inputs(seed: int = 0) -> dict:
    inp = _fwd_setup_inputs(seed)
    key = _jax.random.fold_in(_jax.random.key(seed), 7919)
    shape, _ = _output_shape()
    out = dict(inp)
    out["loss_target"] = _jax.random.normal(_jax.random.fold_in(key, 0), shape, _jnp.float32)
    for i, name in enumerate(TWIN_WEIGHTS):
        w = inp[name].astype(_jnp.float32)
        if MOMENT_SCALE is None:
            s = _jnp.sqrt(_jnp.mean(_jnp.square(w)) + 1e-30)
        else:
            s = MOMENT_SCALE[name]
        km, kv = _jax.random.split(_jax.random.fold_in(key, i + 1))
        out[name] = w
        out["m_" + name] = s * _jax.random.normal(km, w.shape, _jnp.float32)
        out["v_" + name] = (s * s) * _jax.random.uniform(kv, w.shape, _jnp.float32, 0.5, 1.5)
    if N_MICROBATCH > 1:
        for name, axis in PER_EXAMPLE_BATCH_AXIS.items():
            out[name] = _to_microbatches(out[name], axis)
    return {'x': out['x'], 'c': out['c'], 'ctx': out['ctx'], 'c_ctx': out['c_ctx'], 'w_mod': out['w_mod'], 'b_mod': out['b_mod'], 'norm_g': out['norm_g'], 'ffn_wg': out['ffn_wg'], 'ffn_wu': out['ffn_wu'], 'ffn_wd': out['ffn_wd'], 'w_in': out['w_in'], 'w_out': out['w_out'], 'q_norm_g': out['q_norm_g'], 'k_norm_g': out['k_norm_g'], 'conv_w': out['conv_w'], 'conv_b': out['conv_b'], 'lru_wa': out['lru_wa'], 'lru_ba': out['lru_ba'], 'lru_wx': out['lru_wx'], 'lru_bx': out['lru_bx'], 'lru_lambda': out['lru_lambda'], 'final_norm_g': out['final_norm_g'], 'loss_target': out['loss_target'], 'm_c_ctx': out['m_c_ctx'], 'm_w_mod': out['m_w_mod'], 'm_b_mod': out['m_b_mod'], 'm_norm_g': out['m_norm_g'], 'm_ffn_wg': out['m_ffn_wg'], 'm_ffn_wu': out['m_ffn_wu'], 'm_ffn_wd': out['m_ffn_wd'], 'm_w_in': out['m_w_in'], 'm_w_out': out['m_w_out'], 'm_q_norm_g': out['m_q_norm_g'], 'm_k_norm_g': out['m_k_norm_g'], 'm_conv_w': out['m_conv_w'], 'm_conv_b': out['m_conv_b'], 'm_lru_wa': out['m_lru_wa'], 'm_lru_ba': out['m_lru_ba'], 'm_lru_wx': out['m_lru_wx'], 'm_lru_bx': out['m_lru_bx'], 'm_lru_lambda': out['m_lru_lambda'], 'm_final_norm_g': out['m_final_norm_g'], 'v_c_ctx': out['v_c_ctx'], 'v_w_mod': out['v_w_mod'], 'v_b_mod': out['v_b_mod'], 'v_norm_g': out['v_norm_g'], 'v_ffn_wg': out['v_ffn_wg'], 'v_ffn_wu': out['v_ffn_wu'], 'v_ffn_wd': out['v_ffn_wd'], 'v_w_in': out['v_w_in'], 'v_w_out': out['v_w_out'], 'v_q_norm_g': out['v_q_norm_g'], 'v_k_norm_g': out['v_k_norm_g'], 'v_conv_w': out['v_conv_w'], 'v_conv_b': out['v_conv_b'], 'v_lru_wa': out['v_lru_wa'], 'v_lru_ba': out['v_lru_ba'], 'v_lru_wx': out['v_lru_wx'], 'v_lru_bx': out['v_lru_bx'], 'v_lru_lambda': out['v_lru_lambda'], 'v_final_norm_g': out['v_final_norm_g']}


def _loss(weights, diff, rest, loss_target):
    with _jax.named_scope("forward"):
        args = {**rest, TWIN_DIFF_INPUT: diff, **{k: w.astype(_WEIGHT_DTYPES[k]) for k, w in weights.items()}}
        y = _forward(args)
    with _jax.named_scope("loss_head"):
        err = _jnp.square(y.astype(_jnp.float32) - loss_target)
        return 0.5 * _jnp.sum(_jnp.mean(err, axis=-1)) if err.ndim else 0.5 * err


def _adamw(w, g, m, v):
    m = ADAM_B1 * m + (1.0 - ADAM_B1) * g
    v = ADAM_B2 * v + (1.0 - ADAM_B2) * _jnp.square(g)
    m_hat = m / (1.0 - ADAM_B1 ** ADAM_STEP)
    v_hat = v / (1.0 - ADAM_B2 ** ADAM_STEP)
    delta = -ADAM_LR * (m_hat / (_jnp.sqrt(v_hat) + ADAM_EPS) + ADAM_WD * w)
    return delta, m, v


def reference(x, c, ctx, c_ctx, w_mod, b_mod, norm_g, ffn_wg, ffn_wu, ffn_wd, w_in, w_out, q_norm_g, k_norm_g, conv_w, conv_b, lru_wa, lru_ba, lru_wx, lru_bx, lru_lambda, final_norm_g, loss_target, m_c_ctx, m_w_mod, m_b_mod, m_norm_g, m_ffn_wg, m_ffn_wu, m_ffn_wd, m_w_in, m_w_out, m_q_norm_g, m_k_norm_g, m_conv_w, m_conv_b, m_lru_wa, m_lru_ba, m_lru_wx, m_lru_bx, m_lru_lambda, m_final_norm_g, v_c_ctx, v_w_mod, v_b_mod, v_norm_g, v_ffn_wg, v_ffn_wu, v_ffn_wd, v_w_in, v_w_out, v_q_norm_g, v_k_norm_g, v_conv_w, v_conv_b, v_lru_wa, v_lru_ba, v_lru_wx, v_lru_bx, v_lru_lambda, v_final_norm_g):
    given = dict(x=x, c=c, ctx=ctx, c_ctx=c_ctx, w_mod=w_mod, b_mod=b_mod, norm_g=norm_g, ffn_wg=ffn_wg, ffn_wu=ffn_wu, ffn_wd=ffn_wd, w_in=w_in, w_out=w_out, q_norm_g=q_norm_g, k_norm_g=k_norm_g, conv_w=conv_w, conv_b=conv_b, lru_wa=lru_wa, lru_ba=lru_ba, lru_wx=lru_wx, lru_bx=lru_bx, lru_lambda=lru_lambda, final_norm_g=final_norm_g, loss_target=loss_target, m_c_ctx=m_c_ctx, m_w_mod=m_w_mod, m_b_mod=m_b_mod, m_norm_g=m_norm_g, m_ffn_wg=m_ffn_wg, m_ffn_wu=m_ffn_wu, m_ffn_wd=m_ffn_wd, m_w_in=m_w_in, m_w_out=m_w_out, m_q_norm_g=m_q_norm_g, m_k_norm_g=m_k_norm_g, m_conv_w=m_conv_w, m_conv_b=m_conv_b, m_lru_wa=m_lru_wa, m_lru_ba=m_lru_ba, m_lru_wx=m_lru_wx, m_lru_bx=m_lru_bx, m_lru_lambda=m_lru_lambda, m_final_norm_g=m_final_norm_g, v_c_ctx=v_c_ctx, v_w_mod=v_w_mod, v_b_mod=v_b_mod, v_norm_g=v_norm_g, v_ffn_wg=v_ffn_wg, v_ffn_wu=v_ffn_wu, v_ffn_wd=v_ffn_wd, v_w_in=v_w_in, v_w_out=v_w_out, v_q_norm_g=v_q_norm_g, v_k_norm_g=v_k_norm_g, v_conv_w=v_conv_w, v_conv_b=v_conv_b, v_lru_wa=v_lru_wa, v_lru_ba=v_lru_ba, v_lru_wx=v_lru_wx, v_lru_bx=v_lru_bx, v_lru_lambda=v_lru_lambda, v_final_norm_g=v_final_norm_g)
    weights = {n: given[n] for n in TWIN_WEIGHTS}
    shared = {n: given[n] for n in SHARED_INPUTS}
    per_example = {n: given[n] for n in ['x', 'c', 'ctx']}
    grad_fn = _jax.value_and_grad(_loss, argnums=(0, 1))

    def one_microbatch(ex, loss_target):
        ex = dict(ex)
        diff = ex.pop(TWIN_DIFF_INPUT)
        return grad_fn(weights, diff, {**shared, **ex}, loss_target)

    if N_MICROBATCH == 1:
        loss, (grad_w, grad_x) = one_microbatch(per_example, given["loss_target"])
    else:
        def body(carry, xs):
            loss_sum, grad_sum = carry
            l_k, (gw_k, gx_k) = one_microbatch(xs[0], xs[1])
            with _jax.named_scope("update"):
                return (loss_sum + l_k, _jax.tree.map(_jnp.add, grad_sum, gw_k)), gx_k

        init = (_jnp.zeros((), _jnp.float32), _jax.tree.map(_jnp.zeros_like, weights))
        (loss, grad_w), grad_x = _jax.lax.scan(body, init, (per_example, given["loss_target"]))
    with _jax.named_scope("update"):
        delta_w, new_m, new_v = {}, {}, {}
        for n in TWIN_WEIGHTS:
            delta_w[n], new_m[n], new_v[n] = _adamw(weights[n], grad_w[n], given["m_" + n], given["v_" + n])
    return (loss, grad_x, *[grad_w[n] for n in TWIN_WEIGHTS], *[delta_w[n] for n in TWIN_WEIGHTS],
            *[new_m[n] for n in TWIN_WEIGHTS], *[new_v[n] for n in TWIN_WEIGHTS])
```

```python
import functools
import math

import jax
import jax.numpy as jnp
from jax import lax
from jax.experimental import pallas as pl
from jax.experimental.pallas import tpu as pltpu

F32 = jnp.float32
_MXU = jnp.bfloat16
ND = 8
TR = 256
HD = 128
EPS = 1e-6
GRID_W = 64
ROPE_THETA = 10000.0
LRU_C = 8.0
VMEM_LIMIT = 56 * 1024 * 1024
SCAN_W = 512
ADAM_LR, ADAM_B1, ADAM_B2, ADAM_EPS, ADAM_WD, ADAM_STEP = 0.001, 0.9, 0.999, 1e-08, 0.01, 10
MESH = pl.DeviceIdType.MESH


def _call(body, *, name, grid, in_specs, out_specs, out_shape, scratch=(), aliases=None):
    return pl.pallas_call(
        body, name=name, grid=grid, in_specs=in_specs, out_specs=out_specs, out_shape=out_shape,
        scratch_shapes=scratch, input_output_aliases=aliases or {},
        compiler_params=pltpu.CompilerParams(dimension_semantics=("arbitrary",) * len(grid),
                                             vmem_limit_bytes=VMEM_LIMIT))


def _sds(shape, dtype=F32):
    return jax.ShapeDtypeStruct(tuple(shape), dtype)


def _dot(a, b):
    return jnp.dot(a.astype(_MXU), b.astype(_MXU), preferred_element_type=F32)


def _dot_nt(a, b):
    return lax.dot_general(a.astype(_MXU), b.astype(_MXU), (((1,), (1,)), ((), ())), preferred_element_type=F32)


def _dot_tn(a, b):
    return lax.dot_general(a.astype(_MXU), b.astype(_MXU), (((0,), (0,)), ((), ())), preferred_element_type=F32)


def _sigmoid(x):
    return 1.0 / (1.0 + jnp.exp(-x))


_GELU_C = math.sqrt(2.0 / math.pi)


def _gelu_and_grad(x):
    x2 = x * x
    t = jnp.tanh(_GELU_C * (x + 0.044715 * x * x2))
    ge = 0.5 * x * (1.0 + t)
    dge = 0.5 * (1.0 + t) + 0.5 * x * (1.0 - t * t) * (_GELU_C * (1.0 + 3.0 * 0.044715 * x2))
    return ge, dge


def _row_spec(width, nmax=None):
    if nmax is None:
        return pl.BlockSpec((TR, width), lambda i: (i, 0))
    return pl.BlockSpec((TR, width), lambda i: (jnp.minimum(i, nmax), 0))


def _full_spec(shape):
    n = len(shape)
    return pl.BlockSpec(tuple(shape), lambda *_: (0,) * n)


def _mod_spec(D, nx):
    return pl.BlockSpec((None, 9, D), lambda i: (i // nx, 0, 0))


def _norm_mod_fwd(X, ng, mod2, k, rows, nx, name):
    D = X.shape[1]

    def body(x_ref, g_ref, mod_ref, h_ref):
        x = x_ref[...]
        r = lax.rsqrt(jnp.mean(x * x, axis=-1, keepdims=True) + EPS)
        n = (x * r) * g_ref[...]
        h_ref[...] = (n * (1.0 + mod_ref[3 * k + 1:3 * k + 2, :]) + mod_ref[3 * k:3 * k + 1, :]).astype(h_ref.dtype)

    return _call(body, name=name, grid=(rows // TR,),
                 in_specs=[_row_spec(D), _full_spec((1, D)), _mod_spec(D, nx)],
                 out_specs=_row_spec(D), out_shape=_sds((rows, D), _MXU))(X, ng, mod2)


def _norm_mod_bwd(X, dH, dXres, ng, mod2, k, rows, nx, res_tiles, name):
    D = X.shape[1]
    ngroups = -(-(rows // TR) // nx)

    def body(x_ref, dh_ref, dres_ref, g_ref, mod_ref, dx_ref, dsh_ref, dsc_ref, dng_ref):
        i = pl.program_id(0)
        x = x_ref[...]
        dh = dh_ref[...]
        g = g_ref[...]
        r = lax.rsqrt(jnp.mean(x * x, axis=-1, keepdims=True) + EPS)
        xh = x * r
        n = xh * g
        dn_mod = dh * (1.0 + mod_ref[3 * k + 1:3 * k + 2, :])

        @pl.when(i % nx == 0)
        def _():
            dsh_ref[...] = jnp.zeros_like(dsh_ref)
            dsc_ref[...] = jnp.zeros_like(dsc_ref)

        @pl.when(i == 0)
        def _():
            dng_ref[...] = jnp.zeros_like(dng_ref)

        dsh_ref[...] += jnp.sum(dh, axis=0, keepdims=True)
        dsc_ref[...] += jnp.sum(dh * n, axis=0, keepdims=True)
        dng_ref[...] += jnp.sum(dn_mod * xh, axis=0, keepdims=True)
        dn = dn_mod * g
        dres = jnp.where(i < res_tiles, dres_ref[...], 0.0)
        dx_ref[...] = r * (dn - xh * jnp.mean(dn * xh, axis=-1, keepdims=True)) + dres

    grp = pl.BlockSpec((None, 1, D), lambda i: (i // nx, 0, 0))
    return _call(body, name=name, grid=(rows // TR,),
                 in_specs=[_row_spec(D), _row_spec(D), _row_spec(D, res_tiles - 1), _full_spec((1, D)), _mod_spec(D, nx)],
                 out_specs=[_row_spec(D), grp, grp, _full_spec((1, D))],
                 out_shape=[_sds((rows, D)), _sds((ngroups, 1, D)), _sds((ngroups, 1, D)), _sds((1, D))],
                 )(X, dH, dXres, ng, mod2)


def _res_bwd(dX, Y, mod2, k, coef, rows, nx, name):
    D = dX.shape[1]
    ngroups = -(-(rows // TR) // nx)

    def body(dx_ref, y_ref, mod_ref, dy_ref, dg_ref):
        i = pl.program_id(0)
        dx = dx_ref[...]

        @pl.when(i % nx == 0)
        def _():
            dg_ref[...] = jnp.zeros_like(dg_ref)

        dy_ref[...] = ((coef * mod_ref[3 * k + 2:3 * k + 3, :]) * dx).astype(dy_ref.dtype)
        dg_ref[...] += jnp.sum(coef * dx * y_ref[...], axis=0, keepdims=True)

    return _call(body, name=name, grid=(rows // TR,),
                 in_specs=[_row_spec(D), _row_spec(D), _mod_spec(D, nx)],
                 out_specs=[_row_spec(D), pl.BlockSpec((None, 1, D), lambda i: (i // nx, 0, 0))],
                 out_shape=[_sds((rows, D), _MXU), _sds((ngroups, 1, D))])(dX, Y, mod2)


def _final_loss(X3, fg, target, name):
    S, D = X3.shape

    def body(x_ref, g_ref, t_ref, loss_ref, dx_ref, dg_ref):
        i = pl.program_id(0)
        x = x_ref[...]
        g = g_ref[...]
        r = lax.rsqrt(jnp.mean(x * x, axis=-1, keepdims=True) + EPS)
        n = x * r
        err = n * g - t_ref[...]

        @pl.when(i == 0)
        def _():
            loss_ref[...] = jnp.zeros_like(loss_ref)
            dg_ref[...] = jnp.zeros_like(dg_ref)

        loss_ref[...] += 0.5 * jnp.sum(jnp.mean(err * err, axis=-1, keepdims=True), axis=0, keepdims=True)
        dy = err * (1.0 / D)
        dg_ref[...] += jnp.sum(dy * n, axis=0, keepdims=True)
        dn = dy * g
        dx_ref[...] = r * (dn - n * jnp.mean(dn * n, axis=-1, keepdims=True))

    return _call(body, name=name, grid=(S // TR,),
                 in_specs=[_row_spec(D), _full_spec((1, D)), _row_spec(D)],
                 out_specs=[_full_spec((1, 1)), _row_spec(D), _full_spec((1, D))],
                 out_shape=[_sds((1, 1)), _sds((S, D)), _sds((1, D))])(X3, fg, target)


def _ffn_up(H, WGU, layer, rows, name):
    D = H.shape[1]
    Fb = WGU.shape[-1]

    def body(h_ref, w_ref, g_ref, u_ref, a_ref):
        h = h_ref[...]
        g = _dot(h, w_ref[0])
        u = _dot(h, w_ref[1])
        g_ref[...] = g
        u_ref[...] = u
        a_ref[...] = ((g * _sigmoid(g)) * u).astype(a_ref.dtype)

    blk = pl.BlockSpec((None, TR, Fb), lambda d, m: (d, m, 0))
    return _call(body, name=name, grid=(ND, rows // TR),
                 in_specs=[pl.BlockSpec((TR, D), lambda d, m: (m, 0)),
                           pl.BlockSpec((None, None, 2, D, Fb), lambda d, m: (d, layer, 0, 0, 0))],
                 out_specs=[blk, blk, blk],
                 out_shape=[_sds((ND, rows, Fb)), _sds((ND, rows, Fb)), _sds((ND, rows, Fb), _MXU)])(H, WGU)


def _ffn_down(A, WD, layer, X, mod2, k, rows, nx, name):
    Fb, D = WD.shape[-2:]

    def body(a_ref, w_ref, x_ref, mod_ref, y_ref, xn_ref, acc_ref):
        d = pl.program_id(1)

        @pl.when(d == 0)
        def _():
            acc_ref[...] = jnp.zeros_like(acc_ref)

        acc_ref[...] += _dot(a_ref[...], w_ref[...])

        @pl.when(d == ND - 1)
        def _():
            y = acc_ref[...]
            y_ref[...] = y
            xn_ref[...] = x_ref[...] + (0.5 * mod_ref[3 * k + 2:3 * k + 3, :]) * y

    row = pl.BlockSpec((TR, D), lambda m, d: (m, 0))
    return _call(body, name=name, grid=(rows // TR, ND),
                 in_specs=[pl.BlockSpec((None, TR, Fb), lambda m, d: (d, m, 0)),
                           pl.BlockSpec((None, None, Fb, D), lambda m, d: (d, layer, 0, 0)),
                           row, pl.BlockSpec((None, 9, D), lambda m, d: (m // nx, 0, 0))],
                 out_specs=[row, row], out_shape=[_sds((rows, D)), _sds((rows, D))],
                 scratch=[pltpu.VMEM((TR, D), F32)])(A, WD, X, mod2)


def _ffn_dact(dYb, WD, layer, G, U, rows, name):
    Fb, D = WD.shape[-2:]

    def body(dy_ref, w_ref, g_ref, u_ref, dg_ref, du_ref):
        da = _dot_nt(dy_ref[...], w_ref[...])
        g = g_ref[...]
        sg = _sigmoid(g)
        dg_ref[...] = (da * u_ref[...] * (sg * (1.0 + g * (1.0 - sg)))).astype(dg_ref.dtype)
        du_ref[...] = (da * (g * sg)).astype(du_ref.dtype)

    blk = pl.BlockSpec((None, TR, Fb), lambda d, m: (d, m, 0))
    return _call(body, name=name, grid=(ND, rows // TR),
                 in_specs=[pl.BlockSpec((TR, D), lambda d, m: (m, 0)),
                           pl.BlockSpec((None, None, Fb, D), lambda d, m: (d, layer, 0, 0)), blk, blk],
                 out_specs=[blk, blk],
                 out_shape=[_sds((ND, rows, Fb), _MXU), _sds((ND, rows, Fb), _MXU)])(dYb, WD, G, U)


def _ffn_dh(dG, dU, WGU, layer, rows, name):
    D, Fb = WGU.shape[-2:]

    def body(dg_ref, du_ref, w_ref, dh_ref, acc_ref):
        d = pl.program_id(1)

        @pl.when(d == 0)
        def _():
            acc_ref[...] = jnp.zeros_like(acc_ref)

        acc_ref[...] += _dot_nt(dg_ref[...], w_ref[0]) + _dot_nt(du_ref[...], w_ref[1])

        @pl.when(d == ND - 1)
        def _():
            dh_ref[...] = acc_ref[...]

    blk = pl.BlockSpec((None, TR, Fb), lambda m, d: (d, m, 0))
    return _call(body, name=name, grid=(rows // TR, ND),
                 in_specs=[blk, blk, pl.BlockSpec((None, None, 2, D, Fb), lambda m, d: (d, layer, 0, 0, 0))],
                 out_specs=pl.BlockSpec((TR, D), lambda m, d: (m, 0)), out_shape=_sds((rows, D)),
                 scratch=[pltpu.VMEM((TR, D), F32)])(dG, dU, WGU)


def _mm_tn(A, a_spec, B, b_spec, out_shape, out_spec, rows, name, prev=None):
    def body(*refs):
        a_ref, b_ref, o_ref = refs[0], refs[1], refs[-1]

        @pl.when(pl.program_id(1) == 0)
        def _():
            o_ref[...] = jnp.zeros_like(o_ref)

        o_ref[...] += _dot_tn(a_ref[...], b_ref[...])

    in_specs = [a_spec, b_spec]
    args = [A, B]
    aliases = None
    if prev is not None:
        in_specs.append(pl.BlockSpec(memory_space=pl.ANY))
        args.append(prev)
        aliases = {2: 0}
    return _call(body, name=name, grid=(ND, rows // TR), in_specs=in_specs, out_specs=out_spec,
                 out_shape=_sds(out_shape), aliases=aliases)(*args)


def _proj_in(H2, WIN, name):
    R, D = H2.shape
    Nb = WIN.shape[-1]

    def body(h_ref, w_ref, p_ref):
        p_ref[...] = _dot(h_ref[...], w_ref[...])

    return _call(body, name=name, grid=(ND, R // TR),
                 in_specs=[pl.BlockSpec((TR, D), lambda d, m: (m, 0)), pl.BlockSpec((None, D, Nb), lambda d, m: (d, 0, 0))],
                 out_specs=pl.BlockSpec((TR, Nb), lambda d, m: (m, d)), out_shape=_sds((R, ND * Nb)))(H2, WIN)


def _dproj_in(dP, WIN, name):
    R = dP.shape[0]
    D, Nb = WIN.shape[-2:]

    def body(dp_ref, w_ref, dh_ref, acc_ref):
        d = pl.program_id(1)

        @pl.when(d == 0)
        def _():
            acc_ref[...] = jnp.zeros_like(acc_ref)

        acc_ref[...] += _dot_nt(dp_ref[...], w_ref[...])

        @pl.when(d == ND - 1)
        def _():
            dh_ref[...] = acc_ref[...]

    return _call(body, name=name, grid=(R // TR, ND),
                 in_specs=[pl.BlockSpec((TR, Nb), lambda m, d: (m, d)), pl.BlockSpec((None, D, Nb), lambda m, d: (d, 0, 0))],
                 out_specs=pl.BlockSpec((TR, D), lambda m, d: (m, 0)), out_shape=_sds((R, D)),
                 scratch=[pltpu.VMEM((TR, D), F32)])(dP, WIN)


def _proj_out(mixb, WOUT, X1, mod2, S, name):
    D = WOUT.shape[0]

    def body(m_ref, w_ref, x_ref, mod_ref, z_ref, xn_ref):
        z = _dot(m_ref[...], w_ref[...])
        z_ref[...] = z
        xn_ref[...] = x_ref[...] + mod_ref[5:6, :] * z

    return _call(body, name=name, grid=(S // TR,),
                 in_specs=[_row_spec(D), _full_spec((D, D)), _row_spec(D), pl.BlockSpec((None, 9, D), lambda i: (0, 0, 0))],
                 out_specs=[_row_spec(D), _row_spec(D)], out_shape=[_sds((S, D)), _sds((S, D))])(mixb, WOUT, X1, mod2)


def _dproj_out(dZb, WOUT, name):
    S, D = dZb.shape

    def body(dz_ref, w_ref, dm_ref):
        dm_ref[...] = _dot_nt(dz_ref[...], w_ref[...])

    return _call(body, name=name, grid=(S // TR,), in_specs=[_row_spec(D), _full_spec((D, D))],
                 out_specs=_row_spec(D), out_shape=_sds((S, D)))(dZb, WOUT)


def _pair_swap(t):
    lane = lax.broadcasted_iota(jnp.int32, t.shape, 1)
    return jnp.where(lane % 2 == 0, pltpu.roll(t, HD - 1, 1), pltpu.roll(t, 1, 1))


def _qkv_prep(P, qg, kg, COS, SIN, D, KVW, name):
    R = P.shape[0]
    W = D + 2 * KVW
    nq, nk = D // HD, KVW // HD

    def body(p_ref, qg_ref, kg_ref, cos_ref, sin_ref, q_ref, k_ref, v_ref):
        cos, sin = cos_ref[...], sin_ref[...]

        def head(t, g):
            y = (t * lax.rsqrt(jnp.mean(t * t, axis=-1, keepdims=True) + EPS)) * g
            return y * cos + _pair_swap(y) * sin

        for h in range(nq):
            q_ref[:, h * HD:(h + 1) * HD] = head(p_ref[:, h * HD:(h + 1) * HD], qg_ref[...]).astype(q_ref.dtype)
        for h in range(nk):
            k_ref[:, h * HD:(h + 1) * HD] = head(p_ref[:, D + h * HD:D + (h + 1) * HD], kg_ref[...]).astype(k_ref.dtype)
        v_ref[...] = p_ref[:, D + KVW:W].astype(v_ref.dtype)

    return _call(body, name=name, grid=(R // TR,),
                 in_specs=[_row_spec(W), _full_spec((1, HD)), _full_spec((1, HD)), _row_spec(HD), _row_spec(HD)],
                 out_specs=[_row_spec(D), _row_spec(KVW), _row_spec(KVW)],
                 out_shape=[_sds((R, D), _MXU), _sds((R, KVW), _MXU), _sds((R, KVW), _MXU)])(P, qg, kg, COS, SIN)


def _qkv_bwd(P, dq, dk, dv, qg, kg, COS, SIN, dP, D, KVW, nx, name):
    R, INW = P.shape
    W = D + 2 * KVW
    nq, nk = D // HD, KVW // HD

    def body(p_ref, dq_ref, dk_ref, dv_ref, qg_ref, kg_ref, cos_ref, sin_ref, dp_in, dp_ref, dqg_ref, dkg_ref):
        i = pl.program_id(0)
        cos, sin = cos_ref[...], sin_ref[...]

        @pl.when(i == 0)
        def _():
            dqg_ref[...] = jnp.zeros_like(dqg_ref)
            dkg_ref[...] = jnp.zeros_like(dkg_ref)

        def head_bwd(t, g, dout):
            r = lax.rsqrt(jnp.mean(t * t, axis=-1, keepdims=True) + EPS)
            n = t * r
            dy = dout * cos + _pair_swap(dout * sin)
            dn = dy * g
            return r * (dn - n * jnp.mean(dn * n, axis=-1, keepdims=True)), jnp.sum(dy * n, axis=0, keepdims=True)

        dqg = jnp.zeros((1, HD), F32)
        for h in range(nq):
            sl = slice(h * HD, (h + 1) * HD)
            dt, dg = head_bwd(p_ref[:, sl], qg_ref[...], jnp.where(i < nx, dq_ref[:, sl], 0.0))
            dp_ref[:, sl] = dt.astype(dp_ref.dtype)
            dqg += dg
        dkg = jnp.zeros((1, HD), F32)
        for h in range(nk):
            sl = slice(h * HD, (h + 1) * HD)
            dt, dg = head_bwd(p_ref[:, D + h * HD:D + (h + 1) * HD], kg_ref[...], dk_ref[:, sl])
            dp_ref[:, D + h * HD:D + (h + 1) * HD] = dt.astype(dp_ref.dtype)
            dkg += dg
        dqg_ref[...] += dqg
        dkg_ref[...] += dkg
        dp_ref[:, D + KVW:W] = dv_ref[...].astype(dp_ref.dtype)

    return _call(body, name=name, grid=(R // TR,),
                 in_specs=[_row_spec(W), _row_spec(D, nx - 1), _row_spec(KVW), _row_spec(KVW), _full_spec((1, HD)),
                           _full_spec((1, HD)), _row_spec(HD), _row_spec(HD), pl.BlockSpec(memory_space=pl.ANY)],
                 out_specs=[_row_spec(W), _full_spec((1, HD)), _full_spec((1, HD))],
                 out_shape=[_sds((R, INW), _MXU), _sds((1, HD)), _sds((1, HD))],
                 aliases={8: 0})(P, dq, dk, dv, qg, kg, COS, SIN, dP)


def _stack_heads(ref, G, dtype=None):
    parts = [ref[:, g * HD:(g + 1) * HD] for g in range(G)]
    out = jnp.concatenate(parts, axis=0)
    return out if dtype is None else out.astype(dtype)


def _attn_fwd(q, k, v, S, G, name):
    R, KVW = k.shape
    D = q.shape[1]
    Kh = KVW // HD
    tq = 128
    scale = HD ** -0.5

    def body(q_ref, k_ref, v_ref, o_ref, lse_ref):
        qs = _stack_heads(q_ref, G)
        s = _dot_nt(qs, k_ref[...]) * scale
        m = jnp.max(s, axis=-1, keepdims=True)
        p = jnp.exp(s - m)
        l = jnp.sum(p, axis=-1, keepdims=True)
        o = _dot(p, v_ref[...]) / l
        lse = m + jnp.log(l)
        for g in range(G):
            o_ref[:, g * HD:(g + 1) * HD] = o[g * tq:(g + 1) * tq, :]
            lse_ref[g] = jnp.broadcast_to(lse[g * tq:(g + 1) * tq, :], (tq, HD))

    return _call(body, name=name, grid=(Kh, S // tq),
                 in_specs=[pl.BlockSpec((tq, G * HD), lambda h, i: (i, h)), pl.BlockSpec((R, HD), lambda h, i: (0, h)),
                           pl.BlockSpec((R, HD), lambda h, i: (0, h))],
                 out_specs=[pl.BlockSpec((tq, G * HD), lambda h, i: (i, h)),
                            pl.BlockSpec((None, G, tq, HD), lambda h, i: (h, 0, i, 0))],
                 out_shape=[_sds((S, D)), _sds((Kh, G, S, HD))])(q, k, v)


def _attn_bwd(q, k, v, O, LSE, dOb, S, G, name):
    R, KVW = k.shape
    D = q.shape[1]
    Kh = KVW // HD
    tq = 64
    scale = HD ** -0.5

    def body(q_ref, k_ref, v_ref, o_ref, lse_ref, do_ref, dq_ref, dk_ref, dv_ref):
        @pl.when(pl.program_id(1) == 0)
        def _():
            dk_ref[...] = jnp.zeros_like(dk_ref)
            dv_ref[...] = jnp.zeros_like(dv_ref)

        qs = _stack_heads(q_ref, G)
        do = _stack_heads(do_ref, G)
        o = _stack_heads(o_ref, G)
        lse = jnp.concatenate([lse_ref[g][:, 0:1] for g in range(G)], axis=0)
        delta = jnp.sum(do.astype(F32) * o, axis=-1, keepdims=True)
        kk, vv = k_ref[...], v_ref[...]
        p = jnp.exp(_dot_nt(qs, kk) * scale - lse)
        dp = _dot_nt(do, vv)
        ds = (p * (dp - delta) * scale).astype(_MXU)
        dq = _dot(ds, kk)
        for g in range(G):
            dq_ref[:, g * HD:(g + 1) * HD] = dq[g * tq:(g + 1) * tq, :]
        dk_ref[...] += _dot_tn(ds, qs)
        dv_ref[...] += _dot_tn(p, do)

    qspec = pl.BlockSpec((tq, G * HD), lambda h, i: (i, h))
    kspec = pl.BlockSpec((R, HD), lambda h, i: (0, h))
    return _call(body, name=name, grid=(Kh, S // tq),
                 in_specs=[qspec, kspec, kspec, qspec, pl.BlockSpec((None, G, tq, HD), lambda h, i: (h, 0, i, 0)), qspec],
                 out_specs=[qspec, kspec, kspec],
                 out_shape=[_sds((S, D)), _sds((R, KVW)), _sds((R, KVW))])(q, k, v, O, LSE, dOb)


def _halo_specs(R, CB, col0):
    nt8 = TR // 8
    return [pl.BlockSpec((8, CB), lambda h, i: (jnp.maximum(i * nt8 - 1, 0), col0 + h)),
            pl.BlockSpec((TR, CB), lambda h, i: (i, col0 + h)),
            pl.BlockSpec((8, CB), lambda h, i: (jnp.minimum((i + 1) * nt8, R // 8 - 1), col0 + h))]


def _seq_pos(i, S, R, CB):
    t = i * TR - 8 + lax.broadcasted_iota(jnp.int32, (TR + 16, CB), 0)
    start = jnp.where(t >= S, S, 0)
    end = jnp.where(t >= S, R, S)
    return t - start, end - t


def _shift(cat, by):
    return pltpu.roll(cat, by % cat.shape[0], 0)


def _gate_mats(xcb, w_ref, dirn, nb):
    return jnp.concatenate([_dot(xcb[:, b * HD:(b + 1) * HD], w_ref[dirn, b]) for b in range(nb)], axis=1)


def _lru_gates_fwd(P, conv_w, conv_b, WA, WX, ba, bx, lam, S, D, col0, name):
    R = P.shape[0]
    CB = D // 2
    nb = CB // HD

    def body(xp_ref, x_ref, xn_ref, cw_ref, cb_ref, wa_ref, wx_ref, ba_ref, bx_ref, lam_ref,
             xc_ref, af_ref, uf_ref, ab_ref, ub_ref):
        i = pl.program_id(1)
        cat = jnp.concatenate([xp_ref[...], x_ref[...], xn_ref[...]], axis=0)
        from_start, to_end = _seq_pos(i, S, R, CB)
        conv = (cb_ref[...] + cw_ref[2:3, :] * cat
                + cw_ref[0:1, :] * jnp.where(from_start >= 2, _shift(cat, 2), 0.0)
                + cw_ref[1:2, :] * jnp.where(from_start >= 1, _shift(cat, 1), 0.0)
                + cw_ref[3:4, :] * jnp.where(to_end >= 2, _shift(cat, -1), 0.0))
        xc = conv[8:8 + TR, :]
        xc_ref[...] = xc
        xcb = xc.astype(_MXU)
        for dirn, (a_ref, u_ref) in enumerate(((af_ref, uf_ref), (ab_ref, ub_ref))):
            ra = _sigmoid(_gate_mats(xcb, wa_ref, dirn, nb) + ba_ref[dirn:dirn + 1, :])
            ia = _sigmoid(_gate_mats(xcb, wx_ref, dirn, nb) + bx_ref[dirn:dirn + 1, :])
            nl = -lam_ref[dirn:dirn + 1, :]
            sp = jnp.maximum(nl, 0.0) + jnp.log(1.0 + jnp.exp(-jnp.abs(nl)))
            la = (-LRU_C) * ra * sp
            a_ref[...] = jnp.exp(la)
            u_ref[...] = jnp.sqrt(1.0 - jnp.exp(2.0 * la)) * (ia * xc)

    def par(r):
        return pl.BlockSpec((r, CB), lambda h, i: (0, h))

    wspec = pl.BlockSpec((2, nb, HD, HD), lambda h, i: (0, h, 0, 0))
    out = pl.BlockSpec((TR, CB), lambda h, i: (i, h))
    return _call(body, name=name, grid=(2, R // TR),
                 in_specs=_halo_specs(R, CB, col0) + [par(4), par(1), wspec, wspec, par(2), par(2), par(2)],
                 out_specs=[out] * 5, out_shape=[_sds((R, D))] * 5,
                 )(P, P, P, conv_w, conv_b, WA, WX, ba, bx, lam)


def _lru_gates_bwd(xc, hfp, hbp, gf, gb, WA, WX, ba, bx, lam, name):
    R, D = xc.shape
    CB = D // 2
    nb = CB // HD

    def body(xc_ref, hfp_ref, hbp_ref, gf_ref, gb_ref, wa_ref, wx_ref, ba_ref, bx_ref, lam_ref,
             dxc_ref, dwa_ref, dwx_ref, dba_ref, dbx_ref, dlam_ref):
        @pl.when(pl.program_id(1) == 0)
        def _():
            for r in (dwa_ref, dwx_ref, dba_ref, dbx_ref, dlam_ref):
                r[...] = jnp.zeros_like(r)

        xc = xc_ref[...]
        xcb = xc.astype(_MXU)
        dxc = jnp.zeros_like(xc)
        for dirn, (hp_ref, g_ref) in enumerate(((hfp_ref, gf_ref), (hbp_ref, gb_ref))):
            ra = _sigmoid(_gate_mats(xcb, wa_ref, dirn, nb) + ba_ref[dirn:dirn + 1, :])
            ia = _sigmoid(_gate_mats(xcb, wx_ref, dirn, nb) + bx_ref[dirn:dirn + 1, :])
            nl = -lam_ref[dirn:dirn + 1, :]
            sp = jnp.maximum(nl, 0.0) + jnp.log(1.0 + jnp.exp(-jnp.abs(nl)))
            la = (-LRU_C) * ra * sp
            a = jnp.exp(la)
            e2 = jnp.exp(2.0 * la)
            s = jnp.sqrt(1.0 - e2)
            du = g_ref[...]
            dla = du * hp_ref[...] * a - du * (ia * xc) * (e2 / s)
            dxc += du * s * ia
            dza = (dla * (-LRU_C) * sp) * ra * (1.0 - ra)
            dzx = (du * s * xc) * ia * (1.0 - ia)
            dlam_ref[dirn:dirn + 1, :] += jnp.sum(dla * (LRU_C * ra) * _sigmoid(nl), axis=0, keepdims=True)
            dba_ref[dirn:dirn + 1, :] += jnp.sum(dza, axis=0, keepdims=True)
            dbx_ref[dirn:dirn + 1, :] += jnp.sum(dzx, axis=0, keepdims=True)
            dzab, dzxb = dza.astype(_MXU), dzx.astype(_MXU)
            parts = []
            for b in range(nb):
                sl = slice(b * HD, (b + 1) * HD)
                dwa_ref[dirn, b] += _dot_tn(xcb[:, sl], dzab[:, sl])
                dwx_ref[dirn, b] += _dot_tn(xcb[:, sl], dzxb[:, sl])
                parts.append(_dot_nt(dzab[:, sl], wa_ref[dirn, b]) + _dot_nt(dzxb[:, sl], wx_ref[dirn, b]))
            dxc += jnp.concatenate(parts, axis=1)
        dxc_ref[...] = dxc

    def par(r):
        return pl.BlockSpec((r, CB), lambda h, i: (0, h))

    wspec = pl.BlockSpec((2, nb, HD, HD), lambda h, i: (0, h, 0, 0))
    tile = pl.BlockSpec((TR, CB), lambda h, i: (i, h))
    nbt = D // HD
    return _call(body, name=name, grid=(2, R // TR),
                 in_specs=[tile] * 5 + [wspec, wspec, par(2), par(2), par(2)],
                 out_specs=[tile, wspec, wspec, par(2), par(2), par(2)],
                 out_shape=[_sds((R, D)), _sds((2, nbt, HD, HD)), _sds((2, nbt, HD, HD)), _sds((2, D)), _sds((2, D)),
                            _sds((2, D))])(xc, hfp, hbp, gf, gb, WA, WX, ba, bx, lam)


def _scan_rows(n_groups, step, init):
    return lax.fori_loop(0, n_groups, lambda gi, c: step(pl.multiple_of(gi * 8, 8), c), init)


def _lru_scan_fwd(af, uf, ab, ub, S, name):
    R, D = af.shape
    W = min(SCAN_W, D)
    nm, nx = R // TR, S // TR
    nc = nm - nx
    ng = TR // 8

    def body(af_ref, uf_ref, ab_ref, ub_ref, hf_ref, hfp_ref, hb_ref, hbp_ref, cf_ref, cb_ref):
        @pl.when(pl.program_id(1) == 0)
        def _():
            cf_ref[...] = jnp.zeros_like(cf_ref)
            cb_ref[...] = jnp.zeros_like(cb_ref)

        def step(base, carry):
            hf, hb = carry
            baseb = pl.multiple_of(TR - 8 - base, 8)
            for r in range(8):
                tf, tb = base + r, baseb + 7 - r
                hfp_ref[pl.ds(tf, 1), :] = hf
                hf = af_ref[pl.ds(tf, 1), :] * hf + uf_ref[pl.ds(tf, 1), :]
                hf_ref[pl.ds(tf, 1), :] = hf
                hbp_ref[pl.ds(tb, 1), :] = hb
                hb = ab_ref[pl.ds(tb, 1), :] * hb + ub_ref[pl.ds(tb, 1), :]
                hb_ref[pl.ds(tb, 1), :] = hb
            return hf, hb

        hf, hb = _scan_rows(ng, step, (cf_ref[0:1, :], cb_ref[0:1, :]))
        cf_ref[0:1, :] = hf
        cb_ref[0:1, :] = hb

    fmap = lambda j, s: (jnp.where(s < nc, nx + s, s - nc), j)
    bmap = lambda j, s: (nm - 1 - s, j)
    fs, bs = pl.BlockSpec((TR, W), fmap), pl.BlockSpec((TR, W), bmap)
    return _call(body, name=name, grid=(D // W, nm), in_specs=[fs, fs, bs, bs], out_specs=[fs, fs, bs, bs],
                 out_shape=[_sds((R, D))] * 4, scratch=[pltpu.VMEM((8, W), F32), pltpu.VMEM((8, W), F32)])(af, uf, ab, ub)


def _lru_scan_bwd(af, ab, dhs, S, name):
    R, D = af.shape
    W = min(SCAN_W, D)
    nm, nx = R // TR, S // TR
    ng = TR // 8

    def body(af_ref, dhf_ref, ab_ref, dhb_ref, gf_ref, gb_ref, cf_ref, cb_ref):
        @pl.when(pl.program_id(1) == 0)
        def _():
            cf_ref[...] = jnp.zeros_like(cf_ref)
            cb_ref[...] = jnp.zeros_like(cb_ref)

        def step(base, carry):
            cf, cb = carry
            based = pl.multiple_of(TR - 8 - base, 8)
            for r in range(8):
                tf, tb = based + 7 - r, base + r
                g = dhf_ref[pl.ds(tf, 1), :] + cf
                gf_ref[pl.ds(tf, 1), :] = g
                cf = af_ref[pl.ds(tf, 1), :] * g
                g = dhb_ref[pl.ds(tb, 1), :] + cb
                gb_ref[pl.ds(tb, 1), :] = g
                cb = ab_ref[pl.ds(tb, 1), :] * g
            return cf, cb

        cf, cb = _scan_rows(ng, step, (cf_ref[0:1, :], cb_ref[0:1, :]))
        cf_ref[0:1, :] = cf
        cb_ref[0:1, :] = cb

    fmap = lambda j, s: (jnp.where(s < nx, nx - 1 - s, nm - 1 - (s - nx)), j)
    bmap = lambda j, s: (s, j)
    fs, bs = pl.BlockSpec((TR, W), fmap), pl.BlockSpec((TR, W), bmap)
    return _call(body, name=name, grid=(D // W, nm), in_specs=[fs, fs, bs, bs], out_specs=[fs, bs],
                 out_shape=[_sds((R, D))] * 2, scratch=[pltpu.VMEM((8, W), F32), pltpu.VMEM((8, W), F32)])(af, dhs, ab, dhs)


def _merge_fwd(P, hf, hb, O, S, D, col_lg, name):
    R = P.shape[0]
    CB = D // 2
    nx = S // TR

    def body(lg_ref, ga_ref, gl_ref, hf_ref, hb_ref, o_ref, mix_ref):
        ge, _ = _gelu_and_grad(lg_ref[...])
        lru = (hf_ref[...] + hb_ref[...]) * ge
        mix_ref[...] = (_sigmoid(ga_ref[...]) * o_ref[...] + _sigmoid(gl_ref[...]) * lru).astype(mix_ref.dtype)

    def col(c0):
        return pl.BlockSpec((TR, CB), lambda h, i: (i, c0 + h))

    return _call(body, name=name, grid=(2, R // TR),
                 in_specs=[col(col_lg), col(col_lg + 2), col(col_lg + 4), col(0), col(0),
                           pl.BlockSpec((TR, CB), lambda h, i: (jnp.minimum(i, nx - 1), h))],
                 out_specs=col(0), out_shape=_sds((R, D), _MXU))(P, P, P, hf, hb, O)


def _merge_bwd(dmix, P, hf, hb, O, S, D, col_lg, name):
    R = P.shape[0]
    CB = D // 2
    nx = S // TR

    def body(dm_ref, lg_ref, ga_ref, gl_ref, hf_ref, hb_ref, o_ref, do_ref, dhs_ref, dp_ref, stash):
        i, sec = pl.program_id(1), pl.program_id(2)

        @pl.when(sec == 0)
        def _():
            dm = jnp.where(i < nx, dm_ref[...], 0.0)
            sa, sl = _sigmoid(ga_ref[...]), _sigmoid(gl_ref[...])
            ge, dge = _gelu_and_grad(lg_ref[...])
            hs = hf_ref[...] + hb_ref[...]
            o = o_ref[...]
            dl = dm * sl
            do_ref[...] = (dm * sa).astype(do_ref.dtype)
            dhs_ref[...] = dl * ge
            stash[0] = (dl * hs * dge).astype(stash.dtype)
            stash[1] = (dm * o * sa * (1.0 - sa)).astype(stash.dtype)
            stash[2] = (dm * (hs * ge) * sl * (1.0 - sl)).astype(stash.dtype)

        dp_ref[...] = stash[sec]

    def col(c0):
        return pl.BlockSpec((TR, CB), lambda h, i, s: (i, c0 + h))

    xrow = pl.BlockSpec((TR, CB), lambda h, i, s: (jnp.minimum(i, nx - 1), h))
    return _call(body, name=name, grid=(2, R // TR, 3),
                 in_specs=[xrow, col(col_lg), col(col_lg + 2), col(col_lg + 4), col(0), col(0), xrow],
                 out_specs=[col(0), col(0), pl.BlockSpec((TR, CB), lambda h, i, s: (i, col_lg + 2 * s + h))],
                 out_shape=[_sds((R, D), _MXU), _sds((R, D)), _sds(P.shape, _MXU)],
                 scratch=[pltpu.VMEM((3, TR, CB), _MXU)])(dmix, P, P, P, hf, hb, O)


def _conv_bwd(dxc, P, conv_w, dP, S, D, col0, name):
    R = P.shape[0]
    CB = D // 2

    def body(dp_, d_ref, dn_, xp_ref, x_ref, xn_ref, cw_ref, dp_in, dpo_ref, dcw_ref, dcb_ref):
        i = pl.program_id(1)

        @pl.when(i == 0)
        def _():
            dcw_ref[...] = jnp.zeros_like(dcw_ref)
            dcb_ref[...] = jnp.zeros_like(dcb_ref)

        d = d_ref[...]
        catd = jnp.concatenate([dp_[...], d, dn_[...]], axis=0)
        catx = jnp.concatenate([xp_ref[...], x_ref[...], xn_ref[...]], axis=0)
        from_start, to_end = _seq_pos(i, S, R, CB)
        dxl = (cw_ref[2:3, :] * catd
               + cw_ref[0:1, :] * jnp.where(to_end >= 3, _shift(catd, -2), 0.0)
               + cw_ref[1:2, :] * jnp.where(to_end >= 2, _shift(catd, -1), 0.0)
               + cw_ref[3:4, :] * jnp.where(from_start >= 1, _shift(catd, 1), 0.0))
        dpo_ref[...] = dxl[8:8 + TR, :].astype(dpo_ref.dtype)
        taps = (jnp.where(from_start >= 2, _shift(catx, 2), 0.0), jnp.where(from_start >= 1, _shift(catx, 1), 0.0),
                catx, jnp.where(to_end >= 2, _shift(catx, -1), 0.0))
        for kk in range(4):
            dcw_ref[kk:kk + 1, :] += jnp.sum(d * taps[kk][8:8 + TR, :], axis=0, keepdims=True)
        dcb_ref[...] += jnp.sum(d, axis=0, keepdims=True)

    return _call(body, name=name, grid=(2, R // TR),
                 in_specs=_halo_specs(R, CB, 0) + _halo_specs(R, CB, col0)
                 + [pl.BlockSpec((4, CB), lambda h, i: (0, h)), pl.BlockSpec(memory_space=pl.ANY)],
                 out_specs=[pl.BlockSpec((TR, CB), lambda h, i: (i, col0 + h)), pl.BlockSpec((4, CB), lambda h, i: (0, h)),
                            pl.BlockSpec((1, CB), lambda h, i: (0, h))],
                 out_shape=[_sds(dP.shape, dP.dtype), _sds((4, D)), _sds((1, D))],
                 aliases={7: 0})(dxc, dxc, dxc, P, P, P, conv_w, dP)


def _rope_tables(S, C):
    t = jnp.arange(S, dtype=jnp.int32)
    row = (t // GRID_W).astype(F32)
    col = (t % GRID_W).astype(F32)
    axis_dims = HD // 2
    freqs = ROPE_THETA ** (-jnp.arange(0, axis_dims, 2, dtype=F32) / axis_dims)
    ang = jnp.concatenate([row[:, None] * freqs, col[:, None] * freqs], axis=-1)
    cos = jnp.repeat(jnp.cos(ang), 2, axis=-1)
    sin = jnp.repeat(jnp.sin(ang), 2, axis=-1) * jnp.tile(jnp.array([-1.0, 1.0], F32), HD // 2)
    return (jnp.concatenate([cos, jnp.ones((C, HD), F32)], axis=0),
            jnp.concatenate([sin, jnp.zeros((C, HD), F32)], axis=0))


def _ffn_fwd(X, ng, mod2, k, layer, WGU, WD, rows, nx, tag):
    H = _norm_mod_fwd(X, ng, mod2, k, rows, nx, f"{tag}_norm")
    G, U, A = _ffn_up(H, WGU, layer, rows, f"{tag}_up")
    Y, Xn = _ffn_down(A, WD, layer, X, mod2, k, rows, nx, f"{tag}_down")
    return Xn, (H, G, U, A, Y)


def _ffn_bwd(dXn, X, saved, ng, mod2, k, layer, WGU, WD, rows, nx, prev, tag):
    H, G, U, A, Y = saved
    D = X.shape[1]
    Fb = WGU.shape[-1]
    dYb, dgate = _res_bwd(dXn, Y, mod2, k, 0.5, rows, nx, f"{tag}_dres")
    dG, dU = _ffn_dact(dYb, WD, layer, G, U, rows, f"{tag}_dact")
    blk = pl.BlockSpec((None, TR, Fb), lambda d, r: (d, r, 0))
    row = pl.BlockSpec((TR, D), lambda d, r: (r, 0))
    pwg, pwu, pwd = prev if prev is not None else (None, None, None)
    dWD = _mm_tn(A, blk, dYb, row, (ND, 2, Fb, D), pl.BlockSpec((None, None, Fb, D), lambda d, r: (d, layer, 0, 0)),
                 rows, f"{tag}_dwd", pwd)
    wspec = pl.BlockSpec((None, None, D, Fb), lambda d, r: (d, layer, 0, 0))
    dWG = _mm_tn(H, row, dG, blk, (ND, 2, D, Fb), wspec, rows, f"{tag}_dwg", pwg)
    dWU = _mm_tn(H, row, dU, blk, (ND, 2, D, Fb), wspec, rows, f"{tag}_dwu", pwu)
    dH = _ffn_dh(dG, dU, WGU, layer, rows, f"{tag}_dh")
    dX, dsh, dsc, dng = _norm_mod_bwd(X, dH, dXn, ng, mod2, k, rows, nx, rows // TR, f"{tag}_dnorm")
    return dX, (dsh, dsc, dgate), dng, (dWG, dWU, dWD)


def _local_step(x, ctx, target, modx, modc, ng, WGU, WD, WIN, WOUT, qg, kg, conv_w, conv_b, WA, WX, ba, bx, lam, fg):
    S, D = x.shape
    C = ctx.shape[0]
    R = S + C
    nx = S // TR
    Nb = WIN.shape[-1]
    KVW = (ND * Nb - 5 * D) // 2
    G = D // KVW
    CB = D // 2
    col_lx = (D + 2 * KVW) // CB
    assert S % TR == 0 and C % TR == 0 and (D + 2 * KVW) % CB == 0 and CB % HD == 0
    mod2 = jnp.stack([modx, modc])
    X0 = jnp.concatenate([x, ctx], axis=0)
    COS, SIN = _rope_tables(S, C)
    ng0, ng1, ng2 = ng[0:1], ng[1:2], ng[2:3]

    X1, ffn1 = _ffn_fwd(X0, ng0, mod2, 0, 0, WGU, WD, R, nx, "ffn1")
    H2 = _norm_mod_fwd(X1, ng1, mod2, 1, R, nx, "mix_norm")
    P = _proj_in(H2, WIN, "proj_in")
    q, k, v = _qkv_prep(P, qg, kg, COS, SIN, D, KVW, "qkv_prep")
    O, LSE = _attn_fwd(q, k, v, S, G, "attn_fwd")
    xc, af, uf, ab, ub = _lru_gates_fwd(P, conv_w, conv_b, WA, WX, ba, bx, lam, S, D, col_lx, "lru_gates")
    hf, hfp, hb, hbp = _lru_scan_fwd(af, uf, ab, ub, S, "lru_scan")
    mixb = _merge_fwd(P, hf, hb, O, S, D, col_lx + 2, "merge")
    Z, X2 = _proj_out(mixb, WOUT, X1, mod2, S, "proj_out")
    X3, ffn2 = _ffn_fwd(X2, ng2, mod2, 2, 1, WGU, WD, S, nx, "ffn2")
    loss, dX3, dfg = _final_loss(X3, fg, target, "loss_head")

    dX2, (dsh3, dsc3, dg3), dng2, dffn = _ffn_bwd(dX3, X2, ffn2, ng2, mod2, 2, 1, WGU, WD, S, nx, None, "ffn2")
    dZb, dg2 = _res_bwd(dX2, Z, mod2, 1, 1.0, S, nx, "mix_dres")
    dmix = _dproj_out(dZb, WOUT, "dproj_out")
    dWOUT = _mm_tn(mixb, pl.BlockSpec((TR, D // ND), lambda d, r: (r, d)), dZb, pl.BlockSpec((TR, D), lambda d, r: (r, 0)),
                   (ND, D // ND, D), pl.BlockSpec((None, D // ND, D), lambda d, r: (d, 0, 0)), S, "dw_out")
    dOb, dhs, dP = _merge_bwd(dmix, P, hf, hb, O, S, D, col_lx + 2, "merge_bwd")
    gf, gb = _lru_scan_bwd(af, ab, dhs, S, "lru_scan_bwd")
    dxc, dWA, dWX, dba, dbx, dlam = _lru_gates_bwd(xc, hfp, hbp, gf, gb, WA, WX, ba, bx, lam, "lru_gates_bwd")
    dP, dconv_w, dconv_b = _conv_bwd(dxc, P, conv_w, dP, S, D, col_lx, "conv_bwd")
    dq, dk, dv = _attn_bwd(q, k, v, O, LSE, dOb, S, G, "attn_bwd")
    dP, dqg, dkg = _qkv_bwd(P, dq, dk, dv, qg, kg, COS, SIN, dP, D, KVW, nx, "qkv_bwd")
    dH2 = _dproj_in(dP, WIN, "dproj_in")
    dWIN = _mm_tn(H2, pl.BlockSpec((TR, D), lambda d, r: (r, 0)), dP, pl.BlockSpec((TR, Nb), lambda d, r: (r, d)),
                  (ND, D, Nb), pl.BlockSpec((None, D, Nb), lambda d, r: (d, 0, 0)), R, "dw_in")
    dX1, dsh2, dsc2, dng1 = _norm_mod_bwd(X1, dH2, dX2, ng1, mod2, 1, R, nx, nx, "mix_dnorm")
    dX0, (dsh1, dsc1, dg1), dng0, (dWG, dWU, dWD) = _ffn_bwd(dX1, X0, ffn1, ng0, mod2, 0, 0, WGU, WD, R, nx, dffn, "ffn1")

    zero = jnp.zeros((1, D), F32)
    dmodx = jnp.concatenate([dsh1[0], dsc1[0], dg1[0], dsh2[0], dsc2[0], dg2[0], dsh3[0], dsc3[0], dg3[0]], axis=0)
    dmodc = jnp.concatenate([dsh1[1], dsc1[1], dg1[1], dsh2[1], dsc2[1], zero, zero, zero, zero], axis=0)
    return dict(loss=loss, grad_x=dX0[:S], dmodx=dmodx, dmodc=dmodc, norm_g=jnp.concatenate([dng0, dng1, dng2], axis=0),
                ffn_wg=dWG, ffn_wu=dWU, ffn_wd=dWD, w_in=dWIN, w_out=dWOUT, q_norm_g=dqg, k_norm_g=dkg,
                conv_w=dconv_w, conv_b=dconv_b, lru_wa=dWA, lru_ba=dba, lru_wx=dWX, lru_bx=dbx, lru_lambda=dlam,
                final_norm_g=dfg)


def _mesh_pos():
    return lax.axis_index("x"), lax.axis_index("y"), lax.axis_index("c")


def _all_gather(xb, name, in_vmem=False):
    space = pltpu.VMEM if in_vmem else pl.ANY

    def body(x_ref, out_ref, send_sems, recv_sems, local_sem):
        x, y, c = _mesh_pos()
        me, sibling = (x, y, c), (x, y, 1 - c)
        chips = [(1 - x, y), (x, 1 - y), (1 - x, 1 - y)]

        def slot(px, py, pc):
            return out_ref.at[4 * px + 2 * py + pc]

        def copy(k, block, to, src=None):
            return pltpu.make_async_remote_copy(
                src_ref=slot(*block) if src is None else src, dst_ref=slot(*block),
                send_sem=send_sems.at[k], recv_sem=recv_sems.at[k], device_id=to, device_id_type=MESH)

        mine = pltpu.make_async_copy(x_ref, slot(*me), local_sem)
        mine.start()
        first = [copy(0, me, sibling, src=x_ref)]
        first += [copy(1 + j, me, (*chip, c), src=x_ref) for j, chip in enumerate(chips)]
        for cp in first:
            cp.start()
        passed = [copy(4 + j, (*chip, c), sibling) for j, chip in enumerate(chips)]
        for j, chip in enumerate(chips):
            copy(1 + j, (*chip, c), me).wait_recv()
            passed[j].start()
        copy(0, sibling, me).wait_recv()
        for j, chip in enumerate(chips):
            copy(4 + j, (*chip, 1 - c), me).wait_recv()
        for cp in first + passed:
            cp.wait_send()
        mine.wait()

    return pl.pallas_call(
        body, name=name, out_shape=_sds((ND,) + xb.shape, xb.dtype),
        in_specs=[pl.BlockSpec(memory_space=space)], out_specs=pl.BlockSpec(memory_space=space),
        scratch_shapes=[pltpu.SemaphoreType.DMA((7,)), pltpu.SemaphoreType.DMA((7,)), pltpu.SemaphoreType.DMA(())])(xb)


def _rs_sibling(Gp, name):
    def body(g_ref, buf_ref, send_sems, recv_sems):
        x, y, c = _mesh_pos()
        copies = []
        for k in range(4):
            copies.append(pltpu.make_async_remote_copy(
                src_ref=g_ref.at[2 * k + (1 - c)], dst_ref=buf_ref.at[k], send_sem=send_sems.at[k],
                recv_sem=recv_sems.at[k], device_id=(x, y, 1 - c), device_id_type=MESH))
        for cp in copies:
            cp.start()
        for cp in copies:
            cp.wait_recv()
        for cp in copies:
            cp.wait_send()

    return pl.pallas_call(
        body, name=name, out_shape=_sds((4,) + Gp.shape[1:], Gp.dtype),
        in_specs=[pl.BlockSpec(memory_space=pl.ANY)], out_specs=pl.BlockSpec(memory_space=pl.ANY),
        scratch_shapes=[pltpu.SemaphoreType.DMA((4,)), pltpu.SemaphoreType.DMA((4,))])(Gp)


def _rs_chips(T, name):
    def body(t_ref, buf_ref, send_sems, recv_sems):
        x, y, c = _mesh_pos()
        chips = [(1 - x, y), (x, 1 - y), (1 - x, 1 - y)]
        copies = []
        for j, (px, py) in enumerate(chips):
            copies.append(pltpu.make_async_remote_copy(
                src_ref=t_ref.at[2 * px + py], dst_ref=buf_ref.at[j], send_sem=send_sems.at[j],
                recv_sem=recv_sems.at[j], device_id=(px, py, c), device_id_type=MESH))
        for cp in copies:
            cp.start()
        for cp in copies:
            cp.wait_recv()
        for cp in copies:
            cp.wait_send()

    return pl.pallas_call(
        body, name=name, out_shape=_sds((3,) + T.shape[1:], T.dtype),
        in_specs=[pl.BlockSpec(memory_space=pl.ANY)], out_specs=pl.BlockSpec(memory_space=pl.ANY),
        scratch_shapes=[pltpu.SemaphoreType.DMA((3,)), pltpu.SemaphoreType.DMA((3,))])(T)


def _tile_rows(rows, cols):
    best = None
    for t in range(8, rows + 1, 8):
        if rows % t == 0 and t * cols * 4 <= (1 << 20):
            best = t
    return best if best is not None else rows


def _prefetch_call(body, *, name, grid, in_specs, out_specs, out_shape):
    return pl.pallas_call(
        body, name=name, out_shape=out_shape,
        grid_spec=pltpu.PrefetchScalarGridSpec(num_scalar_prefetch=1, grid=grid, in_specs=in_specs, out_specs=out_specs),
        compiler_params=pltpu.CompilerParams(dimension_semantics=("arbitrary",) * len(grid), vmem_limit_bytes=VMEM_LIMIT))


def _rs_add(Gp, bufA, cidx, name):
    rows, cols = Gp.shape[1:]
    tr = _tile_rows(rows, cols)

    def body(c_ref, g_ref, a_ref, t_ref):
        t_ref[...] = g_ref[...] + a_ref[...]

    return _prefetch_call(
        body, name=name, grid=(4, rows // tr),
        in_specs=[pl.BlockSpec((None, tr, cols), lambda k, r, c_ref: (2 * k + c_ref[0], r, 0)),
                  pl.BlockSpec((None, tr, cols), lambda k, r, c_ref: (k, r, 0))],
        out_specs=pl.BlockSpec((None, tr, cols), lambda k, r, c_ref: (k, r, 0)),
        out_shape=_sds((4, rows, cols)))(cidx, Gp, bufA)


def _adam(w, g, m, v):
    m = ADAM_B1 * m + (1.0 - ADAM_B1) * g
    v = ADAM_B2 * v + (1.0 - ADAM_B2) * (g * g)
    m_hat = m / (1.0 - ADAM_B1 ** ADAM_STEP)
    v_hat = v / (1.0 - ADAM_B2 ** ADAM_STEP)
    return -ADAM_LR * (m_hat / (jnp.sqrt(v_hat) + ADAM_EPS) + ADAM_WD * w), m, v


def _rs_finish(T, bufB, own, name, wmv=None):
    rows, cols = T.shape[1:]
    tr = _tile_rows(rows, cols)
    n_in = 4 + (3 if wmv is not None else 0)

    def body(o_ref, *refs):
        ins, outs = refs[:n_in], refs[n_in:]
        g = ((ins[0][...] + ins[1][...]) + ins[2][...]) + ins[3][...]
        outs[0][...] = g
        if wmv is not None:
            d, m, v = _adam(ins[4][...], g, ins[5][...], ins[6][...])
            outs[1][...] = d
            outs[2][...] = m
            outs[3][...] = v

    plain = pl.BlockSpec((tr, cols), lambda r, o_ref: (r, 0))
    in_specs = [pl.BlockSpec((None, tr, cols), lambda r, o_ref: (o_ref[0], r, 0))]
    in_specs += [pl.BlockSpec((None, tr, cols), (lambda j: lambda r, o_ref: (j, r, 0))(j)) for j in range(3)]
    args = [T, bufB, bufB, bufB]
    n_out = 1
    if wmv is not None:
        in_specs += [plain] * 3
        args += list(wmv)
        n_out = 4
    return _prefetch_call(body, name=name, grid=(rows // tr,), in_specs=in_specs, out_specs=[plain] * n_out,
                          out_shape=[_sds((rows, cols))] * n_out)(own, *args)


def _reduce_scatter(Gp, cidx, own, tag, wmv=None):
    bufA = _rs_sibling(Gp, f"{tag}_rs_sibling")
    T = _rs_add(Gp, bufA, cidx, f"{tag}_rs_add")
    bufB = _rs_chips(T, f"{tag}_rs_chips")
    return _rs_finish(T, bufB, own, f"{tag}_rs_finish", wmv)


def _adamw_plain(w, g, m, v, name):
    rows, cols = w.shape
    tr = _tile_rows(rows, cols)

    def body(w_ref, g_ref, m_ref, v_ref, d_ref, mo_ref, vo_ref):
        d, m_, v_ = _adam(w_ref[...], g_ref[...], m_ref[...], v_ref[...])
        d_ref[...] = d
        mo_ref[...] = m_
        vo_ref[...] = v_

    spec = pl.BlockSpec((tr, cols), lambda r: (r, 0))
    return _call(body, name=name, grid=(rows // tr,), in_specs=[spec] * 4, out_specs=[spec] * 3,
                 out_shape=[_sds((rows, cols))] * 3)(w, g, m, v)


_MOD_TK = 512


def _mod_fwd(cc16, w_loc, b_loc, name):
    D, cols = w_loc.shape
    tk = min(_MOD_TK, D)
    nk = D // tk

    def body(c_ref, w_ref, b_ref, o_ref):
        kk = pl.program_id(0)

        @pl.when(kk == 0)
        def _():
            o_ref[...] = jnp.zeros_like(o_ref)

        cc = c_ref[...]
        o_ref[...] += _dot(cc * _sigmoid(cc), w_ref[...])

        @pl.when(kk == nk - 1)
        def _():
            o_ref[...] += b_ref[...]

    return _call(body, name=name, grid=(nk,),
                 in_specs=[pl.BlockSpec((16, tk), lambda kk: (0, kk)), pl.BlockSpec((tk, cols), lambda kk: (kk, 0)),
                           _full_spec((1, cols))],
                 out_specs=_full_spec((16, cols)), out_shape=_sds((16, cols)))(cc16, w_loc, b_loc)


def _mod_bwd(dm_loc, cc16, w_loc, name):
    D, cols = w_loc.shape

    def body(dm_ref, c_ref, w_ref, gw_ref, ds_ref):
        rows = [dm_ref[b, 0:1, :] for b in range(ND)]
        ctx = dm_ref[0, 1:2, :]
        for b in range(1, ND):
            ctx = ctx + dm_ref[b, 1:2, :]
        dm16 = jnp.concatenate(rows + [ctx, jnp.zeros((7, cols), F32)], axis=0)
        cc = c_ref[...]
        gw_ref[...] = _dot_tn(cc * _sigmoid(cc), dm16)
        ds_ref[...] = _dot_nt(dm16, w_ref[...])

    tk = min(_MOD_TK, D)
    return _call(body, name=name, grid=(D // tk,),
                 in_specs=[_full_spec((ND, 8, cols)), pl.BlockSpec((16, tk), lambda kk: (0, kk)),
                           pl.BlockSpec((tk, cols), lambda kk: (kk, 0))],
                 out_specs=[pl.BlockSpec((tk, cols), lambda kk: (kk, 0)), pl.BlockSpec((16, tk), lambda kk: (0, kk))],
                 out_shape=[_sds((D, cols)), _sds((16, D))])(dm_loc, cc16, w_loc)


def _bmod_grad(dm_all, name):
    n = dm_all.shape[-1]

    def body(dm_ref, o_ref):
        acc = dm_ref[0, 0:1, :] + dm_ref[0, 1:2, :]
        for b in range(1, ND):
            acc = (acc + dm_ref[b, 0:1, :]) + dm_ref[b, 1:2, :]
        o_ref[...] = acc

    return _call(body, name=name, grid=(1,), in_specs=[_full_spec((ND, 8, n))], out_specs=_full_spec((1, n)),
                 out_shape=_sds((1, n)))(dm_all)


_SMALL_ROWS = 24
_ROW_CCTX = 15


def _small_finish(parts, c_ctx, name):
    D = parts.shape[-1]

    def body(p_ref, c_ref, o_ref):
        acc = p_ref[0]
        for b in range(1, ND):
            acc = acc + p_ref[b]
        cc = c_ref[...]
        sg = _sigmoid(cc)
        dsilu = sg * (1.0 + cc * (1.0 - sg))
        row = lax.broadcasted_iota(jnp.int32, acc.shape, 0)
        o_ref[...] = jnp.where(row == _ROW_CCTX, acc * dsilu, acc)

    return _call(body, name=name, grid=(1,), in_specs=[_full_spec(parts.shape), _full_spec((1, D))],
                 out_specs=_full_spec((_SMALL_ROWS, D)), out_shape=_sds((_SMALL_ROWS, D)))(parts, c_ctx)


_WEIGHTS = ['c_ctx', 'w_mod', 'b_mod', 'norm_g', 'ffn_wg', 'ffn_wu', 'ffn_wd', 'w_in', 'w_out', 'q_norm_g', 'k_norm_g',
            'conv_w', 'conv_b', 'lru_wa', 'lru_ba', 'lru_wx', 'lru_bx', 'lru_lambda', 'final_norm_g']
_SMALL = ['c_ctx', 'b_mod', 'norm_g', 'q_norm_g', 'k_norm_g', 'conv_w', 'conv_b', 'lru_ba', 'lru_bx', 'lru_lambda',
          'final_norm_g']


def _pad_rows(a, rows):
    return jnp.concatenate([a, jnp.zeros((rows - a.shape[0],) + a.shape[1:], a.dtype)], axis=0)


def _step(w, m, v, x, c, ctx, loss_target):
    xi, yi, ci = _mesh_pos()
    me = 4 * xi + 2 * yi + ci
    cidx = jnp.reshape(ci, (1,)).astype(jnp.int32)
    own = jnp.reshape(2 * xi + yi, (1,)).astype(jnp.int32)
    S, D = x.shape[1:]
    Ds = D // ND
    cols = w['w_mod'].shape[-1]

    sp = jnp.concatenate([w['norm_g'][0], w['conv_w'][0], w['lru_ba'][0], w['lru_bx'][0], w['lru_lambda'][0]], axis=0)
    spg = _all_gather(_pad_rows(sp, 16), "ag_small_params", in_vmem=True)
    spf = jnp.transpose(spg, (1, 0, 2)).reshape(16, D)
    ng, conv_w, ba, bx, lam = spf[0:3], spf[3:7], spf[7:9], spf[9:11], spf[11:13]

    cg = _all_gather(_pad_rows(c, 8), "ag_cond", in_vmem=True)
    cc16 = _pad_rows(jnp.concatenate([cg[:, 0, :], w['c_ctx'][None, :]], axis=0), 16)
    b_loc = lax.dynamic_slice_in_dim(w['b_mod'], me * cols, cols, axis=1)
    mod_loc = _mod_fwd(cc16, w['w_mod'][0], b_loc, "mod_fwd")
    modg = _all_gather(mod_loc, "ag_mod", in_vmem=True)
    mod16 = jnp.transpose(modg, (1, 0, 2)).reshape(16, ND * cols)
    modx = lax.dynamic_index_in_dim(mod16, me, axis=0, keepdims=False).reshape(9, D)
    modc = mod16[8].reshape(9, D)

    WGU = _all_gather(jnp.stack([w['ffn_wg'][0], w['ffn_wu'][0]], axis=1).astype(_MXU), "ag_ffn_up")
    WD = _all_gather(w['ffn_wd'][0].astype(_MXU), "ag_ffn_down")
    WIN = _all_gather(w['w_in'][0].astype(_MXU), "ag_w_in")
    WOUT = _all_gather(w['w_out'][0].astype(_MXU), "ag_w_out").reshape(D, D)
    WA, WX = w['lru_wa'][0].astype(_MXU), w['lru_wx'][0].astype(_MXU)

    g = _local_step(x[0], ctx[0], loss_target[0], modx, modc, ng, WGU, WD, WIN, WOUT, w['q_norm_g'], w['k_norm_g'],
                    conv_w, w['conv_b'], WA, WX, ba, bx, lam, w['final_norm_g'][None, :])

    grad, delta, new_m, new_v = {}, {}, {}, {}

    for n in ('ffn_wg', 'ffn_wu', 'ffn_wd', 'w_in', 'w_out'):
        shard = w[n].shape
        c2 = shard[-1]
        view = lambda a: a.reshape(-1, c2)
        outs = _reduce_scatter(g[n].reshape(ND, -1, c2), cidx, own, n, (view(w[n]), view(m[n]), view(v[n])))
        grad[n], delta[n], new_m[n], new_v[n] = [o.reshape(shard) for o in outs]

    lw = jnp.stack([g['lru_wa'], g['lru_wx']])
    (lsum,) = _reduce_scatter(lw.reshape(ND, -1, HD), cidx, own, "lru_w")
    lfull = _all_gather(lsum, "ag_lru_w").reshape(lw.shape)
    for i, n in enumerate(('lru_wa', 'lru_wx')):
        shard = w[n].shape
        view = lambda a: a.reshape(-1, HD)
        grad[n] = lfull[i].reshape(shard)
        outs = _adamw_plain(view(w[n]), view(lfull[i]), view(m[n]), view(v[n]), f"adamw_{n}")
        delta[n], new_m[n], new_v[n] = [o.reshape(shard) for o in outs]

    dm = _pad_rows(jnp.stack([g['dmodx'].reshape(-1), g['dmodc'].reshape(-1)]), 8)
    dm_all = _all_gather(dm, "ag_dmod", in_vmem=True)
    dm_loc = lax.dynamic_slice_in_dim(dm_all, me * cols, cols, axis=2)
    gw_mod, dsil = _mod_bwd(dm_loc, cc16, w['w_mod'][0], "mod_bwd")
    grad['w_mod'] = gw_mod[None]
    outs = _adamw_plain(w['w_mod'][0], gw_mod, m['w_mod'][0], v['w_mod'][0], "adamw_w_mod")
    delta['w_mod'], new_m['w_mod'], new_v['w_mod'] = [o[None] for o in outs]
    grad['b_mod'] = _bmod_grad(dm_all, "bmod_grad")

    pad_d = lambda a: jnp.concatenate([a, jnp.zeros((1, D - a.shape[1]), F32)], axis=1)
    small = jnp.concatenate([g['norm_g'], g['conv_w'], g['conv_b'], g['lru_ba'], g['lru_bx'], g['lru_lambda'],
                             g['final_norm_g'], dsil[8:9], pad_d(g['q_norm_g']), pad_d(g['k_norm_g'])], axis=0)
    parts = _all_gather(_pad_rows(small, _SMALL_ROWS), "ag_small_grads", in_vmem=True)
    ssum = _small_finish(parts, w['c_ctx'][None, :], "small_finish")
    mine = lambda rows: lax.dynamic_slice_in_dim(rows, me * Ds, Ds, axis=1)
    grad['norm_g'] = mine(ssum[0:3])[None]
    grad['conv_w'] = mine(ssum[3:7])[None]
    grad['conv_b'] = ssum[7:8]
    grad['lru_ba'] = mine(ssum[8:10])[None]
    grad['lru_bx'] = mine(ssum[10:12])[None]
    grad['lru_lambda'] = mine(ssum[12:14])[None]
    grad['final_norm_g'] = ssum[14]
    grad['c_ctx'] = ssum[_ROW_CCTX]
    grad['q_norm_g'] = ssum[16:17, :HD]
    grad['k_norm_g'] = ssum[17:18, :HD]

    def pack(d):
        flat = jnp.concatenate([d[n].reshape(-1) for n in _SMALL])
        padded = -(-flat.shape[0] // 1024) * 1024
        return jnp.concatenate([flat, jnp.zeros((padded - flat.shape[0],), F32)]).reshape(-1, HD)

    outs = _adamw_plain(pack(w), pack(grad), pack(m), pack(v), "adamw_small")
    off = 0
    for n in _SMALL:
        size = math.prod(w[n].shape)
        for dst, o in zip((delta, new_m, new_v), outs):
            dst[n] = o.reshape(-1)[off:off + size].reshape(w[n].shape)
        off += size

    loss = lax.psum(g['loss'][0, 0], ("x", "y", "c"))
    return (loss, g['grad_x'][None], *[grad[n] for n in _WEIGHTS], *[delta[n] for n in _WEIGHTS],
            *[new_m[n] for n in _WEIGHTS], *[new_v[n] for n in _WEIGHTS])


def kernel(x, c, ctx, c_ctx, w_mod, b_mod, norm_g, ffn_wg, ffn_wu, ffn_wd, w_in, w_out, q_norm_g, k_norm_g, conv_w, conv_b, lru_wa, lru_ba, lru_wx, lru_bx, lru_lambda, final_norm_g, loss_target, m_c_ctx, m_w_mod, m_b_mod, m_norm_g, m_ffn_wg, m_ffn_wu, m_ffn_wd, m_w_in, m_w_out, m_q_norm_g, m_k_norm_g, m_conv_w, m_conv_b, m_lru_wa, m_lru_ba, m_lru_wx, m_lru_bx, m_lru_lambda, m_final_norm_g, v_c_ctx, v_w_mod, v_b_mod, v_norm_g, v_ffn_wg, v_ffn_wu, v_ffn_wd, v_w_in, v_w_out, v_q_norm_g, v_k_norm_g, v_conv_w, v_conv_b, v_lru_wa, v_lru_ba, v_lru_wx, v_lru_bx, v_lru_lambda, v_final_norm_g):
    given = dict(locals())
    w = {n: given[n] for n in _WEIGHTS}
    m = {n: given["m_" + n] for n in _WEIGHTS}
    v = {n: given["v_" + n] for n in _WEIGHTS}
    return _step(w, m, v, x, c, ctx, loss_target)
```

```python
import functools
import math

import jax
import jax.numpy as jnp
from jax import lax
from jax.experimental import pallas as pl
from jax.experimental.pallas import tpu as pltpu

F32 = jnp.float32
_MXU = jnp.bfloat16
ND = 8
TR = 256
HD = 128
EPS = 1e-6
GRID_W = 64
ROPE_THETA = 10000.0
LRU_C = 8.0
VMEM_LIMIT = 56 * 1024 * 1024
SCAN_W = 512
ADAM_LR, ADAM_B1, ADAM_B2, ADAM_EPS, ADAM_WD, ADAM_STEP = 0.001, 0.9, 0.999, 1e-08, 0.01, 10
MESH = pl.DeviceIdType.MESH


def _call(body, *, name, grid, in_specs, out_specs, out_shape, scratch=(), aliases=None):
    return pl.pallas_call(
        body, name=name, grid=grid, in_specs=in_specs, out_specs=out_specs, out_shape=out_shape,
        scratch_shapes=scratch, input_output_aliases=aliases or {},
        compiler_params=pltpu.CompilerParams(dimension_semantics=("arbitrary",) * len(grid),
                                             vmem_limit_bytes=VMEM_LIMIT))


def _sds(shape, dtype=F32):
    return jax.ShapeDtypeStruct(tuple(shape), dtype)


def _dot(a, b):
    return jnp.dot(a.astype(_MXU), b.astype(_MXU), preferred_element_type=F32)


def _dot_nt(a, b):
    return lax.dot_general(a.astype(_MXU), b.astype(_MXU), (((1,), (1,)), ((), ())), preferred_element_type=F32)


def _dot_tn(a, b):
    return lax.dot_general(a.astype(_MXU), b.astype(_MXU), (((0,), (0,)), ((), ())), preferred_element_type=F32)


def _sigmoid(x):
    return 1.0 / (1.0 + jnp.exp(-x))


_GELU_C = math.sqrt(2.0 / math.pi)


def _gelu_and_grad(x):
    x2 = x * x
    t = jnp.tanh(_GELU_C * (x + 0.044715 * x * x2))
    ge = 0.5 * x * (1.0 + t)
    dge = 0.5 * (1.0 + t) + 0.5 * x * (1.0 - t * t) * (_GELU_C * (1.0 + 3.0 * 0.044715 * x2))
    return ge, dge


def _tm(rows):
    assert rows % 4 == 0 and (rows // 4) % 16 == 0
    return rows // 4


def _row_spec(width, nmax=None):
    if nmax is None:
        return pl.BlockSpec((TR, width), lambda i: (i, 0))
    return pl.BlockSpec((TR, width), lambda i: (jnp.minimum(i, nmax), 0))


def _full_spec(shape):
    n = len(shape)
    return pl.BlockSpec(tuple(shape), lambda *_: (0,) * n)


def _mod_spec(D, nx):
    return pl.BlockSpec((None, 9, D), lambda i: (i // nx, 0, 0))


def _norm_mod_fwd(X, ng, mod2, k, rows, nx, name):
    D = X.shape[1]

    def body(x_ref, g_ref, mod_ref, h_ref):
        x = x_ref[...]
        r = lax.rsqrt(jnp.mean(x * x, axis=-1, keepdims=True) + EPS)
        n = (x * r) * g_ref[...]
        h_ref[...] = (n * (1.0 + mod_ref[3 * k + 1:3 * k + 2, :]) + mod_ref[3 * k:3 * k + 1, :]).astype(h_ref.dtype)

    return _call(body, name=name, grid=(rows // TR,),
                 in_specs=[_row_spec(D), _full_spec((1, D)), _mod_spec(D, nx)],
                 out_specs=_row_spec(D), out_shape=_sds((rows, D), _MXU))(X, ng, mod2)


def _norm_mod_bwd(X, dH, dXres, ng, mod2, k, rows, nx, res_tiles, name):
    D = X.shape[1]
    ngroups = -(-(rows // TR) // nx)

    def body(x_ref, dh_ref, dres_ref, g_ref, mod_ref, dx_ref, dsh_ref, dsc_ref, dng_ref):
        i = pl.program_id(0)
        x = x_ref[...]
        dh = dh_ref[...]
        g = g_ref[...]
        r = lax.rsqrt(jnp.mean(x * x, axis=-1, keepdims=True) + EPS)
        xh = x * r
        n = xh * g
        dn_mod = dh * (1.0 + mod_ref[3 * k + 1:3 * k + 2, :])

        @pl.when(i % nx == 0)
        def _():
            dsh_ref[...] = jnp.zeros_like(dsh_ref)
            dsc_ref[...] = jnp.zeros_like(dsc_ref)

        @pl.when(i == 0)
        def _():
            dng_ref[...] = jnp.zeros_like(dng_ref)

        dsh_ref[...] += jnp.sum(dh, axis=0, keepdims=True)
        dsc_ref[...] += jnp.sum(dh * n, axis=0, keepdims=True)
        dng_ref[...] += jnp.sum(dn_mod * xh, axis=0, keepdims=True)
        dn = dn_mod * g
        dres = jnp.where(i < res_tiles, dres_ref[...], 0.0)
        dx_ref[...] = r * (dn - xh * jnp.mean(dn * xh, axis=-1, keepdims=True)) + dres

    grp = pl.BlockSpec((None, 1, D), lambda i: (i // nx, 0, 0))
    return _call(body, name=name, grid=(rows // TR,),
                 in_specs=[_row_spec(D), _row_spec(D), _row_spec(D, res_tiles - 1), _full_spec((1, D)), _mod_spec(D, nx)],
                 out_specs=[_row_spec(D), grp, grp, _full_spec((1, D))],
                 out_shape=[_sds((rows, D)), _sds((ngroups, 1, D)), _sds((ngroups, 1, D)), _sds((1, D))],
                 )(X, dH, dXres, ng, mod2)


def _res_bwd(dX, Y, mod2, k, coef, rows, nx, name):
    D = dX.shape[1]
    ngroups = -(-(rows // TR) // nx)

    def body(dx_ref, y_ref, mod_ref, dy_ref, dg_ref):
        i = pl.program_id(0)
        dx = dx_ref[...]

        @pl.when(i % nx == 0)
        def _():
            dg_ref[...] = jnp.zeros_like(dg_ref)

        dy_ref[...] = ((coef * mod_ref[3 * k + 2:3 * k + 3, :]) * dx).astype(dy_ref.dtype)
        dg_ref[...] += jnp.sum(coef * dx * y_ref[...], axis=0, keepdims=True)

    return _call(body, name=name, grid=(rows // TR,),
                 in_specs=[_row_spec(D), _row_spec(D), _mod_spec(D, nx)],
                 out_specs=[_row_spec(D), pl.BlockSpec((None, 1, D), lambda i: (i // nx, 0, 0))],
                 out_shape=[_sds((rows, D), _MXU), _sds((ngroups, 1, D))])(dX, Y, mod2)


def _final_loss(X3, fg, target, name):
    S, D = X3.shape

    def body(x_ref, g_ref, t_ref, loss_ref, dx_ref, dg_ref):
        i = pl.program_id(0)
        x = x_ref[...]
        g = g_ref[...]
        r = lax.rsqrt(jnp.mean(x * x, axis=-1, keepdims=True) + EPS)
        n = x * r
        err = n * g - t_ref[...]

        @pl.when(i == 0)
        def _():
            loss_ref[...] = jnp.zeros_like(loss_ref)
            dg_ref[...] = jnp.zeros_like(dg_ref)

        loss_ref[...] += 0.5 * jnp.sum(jnp.mean(err * err, axis=-1, keepdims=True), axis=0, keepdims=True)
        dy = err * (1.0 / D)
        dg_ref[...] += jnp.sum(dy * n, axis=0, keepdims=True)
        dn = dy * g
        dx_ref[...] = r * (dn - n * jnp.mean(dn * n, axis=-1, keepdims=True))

    return _call(body, name=name, grid=(S // TR,),
                 in_specs=[_row_spec(D), _full_spec((1, D)), _row_spec(D)],
                 out_specs=[_full_spec((1, 1)), _row_spec(D), _full_spec((1, D))],
                 out_shape=[_sds((1, 1)), _sds((S, D)), _sds((1, D))])(X3, fg, target)


def _ffn_up(H, WGU, layer, rows, name):
    D = H.shape[1]
    Fb = WGU.shape[-1]

    def body(h_ref, w_ref, g_ref, u_ref, a_ref):
        h = h_ref[...]
        g = _dot(h, w_ref[0])
        u = _dot(h, w_ref[1])
        g_ref[...] = g
        u_ref[...] = u
        a_ref[...] = ((g * _sigmoid(g)) * u).astype(a_ref.dtype)

    tm = _tm(rows)
    blk = pl.BlockSpec((None, tm, Fb), lambda d, m: (d, m, 0))
    return _call(body, name=name, grid=(ND, rows // tm),
                 in_specs=[pl.BlockSpec((tm, D), lambda d, m: (m, 0)),
                           pl.BlockSpec((None, None, 2, D, Fb), lambda d, m: (d, layer, 0, 0, 0))],
                 out_specs=[blk, blk, blk],
                 out_shape=[_sds((ND, rows, Fb)), _sds((ND, rows, Fb)), _sds((ND, rows, Fb), _MXU)])(H, WGU)


def _ffn_down(A, WD, layer, X, mod2, k, rows, S, name):
    Fb, D = WD.shape[-2:]
    tm = _tm(rows) // 2

    def body(a_ref, w_ref, x_ref, mod_ref, y_ref, xn_ref, acc_ref):
        d = pl.program_id(1)

        @pl.when(d == 0)
        def _():
            acc_ref[...] = jnp.zeros_like(acc_ref)

        acc_ref[...] += _dot(a_ref[...], w_ref[...])

        @pl.when(d == ND - 1)
        def _():
            y = acc_ref[...]
            y_ref[...] = y
            t = pl.program_id(0) * tm + lax.broadcasted_iota(jnp.int32, (tm, 1), 0)
            gate = jnp.where(t >= S, mod_ref[1, 3 * k + 2:3 * k + 3, :], mod_ref[0, 3 * k + 2:3 * k + 3, :])
            xn_ref[...] = x_ref[...] + (0.5 * gate) * y

    row = pl.BlockSpec((tm, D), lambda m, d: (m, 0))
    return _call(body, name=name, grid=(rows // tm, ND),
                 in_specs=[pl.BlockSpec((None, tm, Fb), lambda m, d: (d, m, 0)),
                           pl.BlockSpec((None, None, Fb, D), lambda m, d: (d, layer, 0, 0)),
                           row, pl.BlockSpec((2, 9, D), lambda m, d: (0, 0, 0))],
                 out_specs=[row, row], out_shape=[_sds((rows, D)), _sds((rows, D))],
                 scratch=[pltpu.VMEM((tm, D), F32)])(A, WD, X, mod2)


def _ffn_dact(dYb, WD, layer, G, U, rows, name):
    Fb, D = WD.shape[-2:]

    def body(dy_ref, w_ref, g_ref, u_ref, dg_ref, du_ref):
        da = _dot_nt(dy_ref[...], w_ref[...])
        g = g_ref[...]
        sg = _sigmoid(g)
        dg_ref[...] = (da * u_ref[...] * (sg * (1.0 + g * (1.0 - sg)))).astype(dg_ref.dtype)
        du_ref[...] = (da * (g * sg)).astype(du_ref.dtype)

    tm = _tm(rows)
    blk = pl.BlockSpec((None, tm, Fb), lambda d, m: (d, m, 0))
    return _call(body, name=name, grid=(ND, rows // tm),
                 in_specs=[pl.BlockSpec((tm, D), lambda d, m: (m, 0)),
                           pl.BlockSpec((None, None, Fb, D), lambda d, m: (d, layer, 0, 0)), blk, blk],
                 out_specs=[blk, blk],
                 out_shape=[_sds((ND, rows, Fb), _MXU), _sds((ND, rows, Fb), _MXU)])(dYb, WD, G, U)


def _ffn_dh(dG, dU, WGU, layer, rows, name):
    D, Fb = WGU.shape[-2:]

    def body(dg_ref, du_ref, w_ref, dh_ref, acc_ref):
        d = pl.program_id(1)

        @pl.when(d == 0)
        def _():
            acc_ref[...] = jnp.zeros_like(acc_ref)

        acc_ref[...] += _dot_nt(dg_ref[...], w_ref[0]) + _dot_nt(du_ref[...], w_ref[1])

        @pl.when(d == ND - 1)
        def _():
            dh_ref[...] = acc_ref[...]

    tm = _tm(rows)
    blk = pl.BlockSpec((None, tm, Fb), lambda m, d: (d, m, 0))
    return _call(body, name=name, grid=(rows // tm, ND),
                 in_specs=[blk, blk, pl.BlockSpec((None, None, 2, D, Fb), lambda m, d: (d, layer, 0, 0, 0))],
                 out_specs=pl.BlockSpec((tm, D), lambda m, d: (m, 0)), out_shape=_sds((rows, D)),
                 scratch=[pltpu.VMEM((tm, D), F32)])(dG, dU, WGU)


def _mm_tn(A, a_spec, B, b_spec, out_shape, out_spec, rows, name, prev=None):
    def body(*refs):
        a_ref, b_ref, o_ref = refs[0], refs[1], refs[-1]

        @pl.when(pl.program_id(1) == 0)
        def _():
            o_ref[...] = jnp.zeros_like(o_ref)

        o_ref[...] += _dot_tn(a_ref[...], b_ref[...])

    in_specs = [a_spec, b_spec]
    args = [A, B]
    aliases = None
    if prev is not None:
        in_specs.append(pl.BlockSpec(memory_space=pl.ANY))
        args.append(prev)
        aliases = {2: 0}
    return _call(body, name=name, grid=(ND, rows // _tm(rows)), in_specs=in_specs, out_specs=out_spec,
                 out_shape=_sds(out_shape), aliases=aliases)(*args)


def _proj_in(H2, WIN, name):
    R, D = H2.shape
    Nb = WIN.shape[-1]

    def body(h_ref, w_ref, p_ref):
        p_ref[...] = _dot(h_ref[...], w_ref[...])

    tm = _tm(R)
    return _call(body, name=name, grid=(ND, R // tm),
                 in_specs=[pl.BlockSpec((tm, D), lambda d, m: (m, 0)), pl.BlockSpec((None, D, Nb), lambda d, m: (d, 0, 0))],
                 out_specs=pl.BlockSpec((tm, Nb), lambda d, m: (m, d)), out_shape=_sds((R, ND * Nb)))(H2, WIN)


def _dproj_in(dP, WIN, name):
    R = dP.shape[0]
    D, Nb = WIN.shape[-2:]

    def body(dp_ref, w_ref, dh_ref, acc_ref):
        d = pl.program_id(1)

        @pl.when(d == 0)
        def _():
            acc_ref[...] = jnp.zeros_like(acc_ref)

        acc_ref[...] += _dot_nt(dp_ref[...], w_ref[...])

        @pl.when(d == ND - 1)
        def _():
            dh_ref[...] = acc_ref[...]

    tm = _tm(R)
    return _call(body, name=name, grid=(R // tm, ND),
                 in_specs=[pl.BlockSpec((tm, Nb), lambda m, d: (m, d)), pl.BlockSpec((None, D, Nb), lambda m, d: (d, 0, 0))],
                 out_specs=pl.BlockSpec((tm, D), lambda m, d: (m, 0)), out_shape=_sds((R, D)),
                 scratch=[pltpu.VMEM((tm, D), F32)])(dP, WIN)


def _proj_out(mixb, WOUT, X1, mod2, S, name):
    D = WOUT.shape[0]

    def body(m_ref, w_ref, x_ref, mod_ref, z_ref, xn_ref):
        z = _dot(m_ref[...], w_ref[...])
        z_ref[...] = z
        xn_ref[...] = x_ref[...] + mod_ref[5:6, :] * z

    return _call(body, name=name, grid=(S // TR,),
                 in_specs=[_row_spec(D), _full_spec((D, D)), _row_spec(D), pl.BlockSpec((None, 9, D), lambda i: (0, 0, 0))],
                 out_specs=[_row_spec(D), _row_spec(D)], out_shape=[_sds((S, D)), _sds((S, D))])(mixb, WOUT, X1, mod2)


def _dproj_out(dZb, WOUT, name):
    S, D = dZb.shape

    def body(dz_ref, w_ref, dm_ref):
        dm_ref[...] = _dot_nt(dz_ref[...], w_ref[...])

    return _call(body, name=name, grid=(S // TR,), in_specs=[_row_spec(D), _full_spec((D, D))],
                 out_specs=_row_spec(D), out_shape=_sds((S, D)))(dZb, WOUT)


def _pair_swap(t):
    lane = lax.broadcasted_iota(jnp.int32, t.shape, 1)
    return jnp.where(lane % 2 == 0, pltpu.roll(t, HD - 1, 1), pltpu.roll(t, 1, 1))


def _qkv_prep(P, qg, kg, COS, SIN, D, KVW, name):
    R = P.shape[0]
    W = D + 2 * KVW
    nq, nk = D // HD, KVW // HD

    def body(p_ref, qg_ref, kg_ref, cos_ref, sin_ref, q_ref, k_ref, v_ref):
        cos, sin = cos_ref[...], sin_ref[...]

        def head(t, g):
            y = (t * lax.rsqrt(jnp.mean(t * t, axis=-1, keepdims=True) + EPS)) * g
            return y * cos + _pair_swap(y) * sin

        for h in range(nq):
            q_ref[:, h * HD:(h + 1) * HD] = head(p_ref[:, h * HD:(h + 1) * HD], qg_ref[...]).astype(q_ref.dtype)
        for h in range(nk):
            k_ref[:, h * HD:(h + 1) * HD] = head(p_ref[:, D + h * HD:D + (h + 1) * HD], kg_ref[...]).astype(k_ref.dtype)
        v_ref[...] = p_ref[:, D + KVW:W].astype(v_ref.dtype)

    return _call(body, name=name, grid=(R // TR,),
                 in_specs=[_row_spec(W), _full_spec((1, HD)), _full_spec((1, HD)), _row_spec(HD), _row_spec(HD)],
                 out_specs=[_row_spec(D), _row_spec(KVW), _row_spec(KVW)],
                 out_shape=[_sds((R, D), _MXU), _sds((R, KVW), _MXU), _sds((R, KVW), _MXU)])(P, qg, kg, COS, SIN)


def _qkv_bwd(P, dq, dk, dv, qg, kg, COS, SIN, dP, D, KVW, nx, name):
    R, INW = P.shape
    W = D + 2 * KVW
    nq, nk = D // HD, KVW // HD

    def body(p_ref, dq_ref, dk_ref, dv_ref, qg_ref, kg_ref, cos_ref, sin_ref, dp_in, dp_ref, dqg_ref, dkg_ref):
        i = pl.program_id(0)
        cos, sin = cos_ref[...], sin_ref[...]

        @pl.when(i == 0)
        def _():
            dqg_ref[...] = jnp.zeros_like(dqg_ref)
            dkg_ref[...] = jnp.zeros_like(dkg_ref)

        def head_bwd(t, g, dout):
            r = lax.rsqrt(jnp.mean(t * t, axis=-1, keepdims=True) + EPS)
            n = t * r
            dy = dout * cos + _pair_swap(dout * sin)
            dn = dy * g
            return r * (dn - n * jnp.mean(dn * n, axis=-1, keepdims=True)), jnp.sum(dy * n, axis=0, keepdims=True)

        dqg = jnp.zeros((1, HD), F32)
        for h in range(nq):
            sl = slice(h * HD, (h + 1) * HD)
            dt, dg = head_bwd(p_ref[:, sl], qg_ref[...], jnp.where(i < nx, dq_ref[:, sl], 0.0))
            dp_ref[:, sl] = dt.astype(dp_ref.dtype)
            dqg += dg
        dkg = jnp.zeros((1, HD), F32)
        for h in range(nk):
            sl = slice(h * HD, (h + 1) * HD)
            dt, dg = head_bwd(p_ref[:, D + h * HD:D + (h + 1) * HD], kg_ref[...], dk_ref[:, sl])
            dp_ref[:, D + h * HD:D + (h + 1) * HD] = dt.astype(dp_ref.dtype)
            dkg += dg
        dqg_ref[...] += dqg
        dkg_ref[...] += dkg
        dp_ref[:, D + KVW:W] = dv_ref[...].astype(dp_ref.dtype)

    return _call(body, name=name, grid=(R // TR,),
                 in_specs=[_row_spec(W), _row_spec(D, nx - 1), _row_spec(KVW), _row_spec(KVW), _full_spec((1, HD)),
                           _full_spec((1, HD)), _row_spec(HD), _row_spec(HD), pl.BlockSpec(memory_space=pl.ANY)],
                 out_specs=[_row_spec(W), _full_spec((1, HD)), _full_spec((1, HD))],
                 out_shape=[_sds((R, INW), _MXU), _sds((1, HD)), _sds((1, HD))],
                 aliases={8: 0})(P, dq, dk, dv, qg, kg, COS, SIN, dP)


def _stack_heads(ref, G, dtype=None):
    parts = [ref[:, g * HD:(g + 1) * HD] for g in range(G)]
    out = jnp.concatenate(parts, axis=0)
    return out if dtype is None else out.astype(dtype)


def _attn_fwd(q, k, v, S, G, name):
    R, KVW = k.shape
    D = q.shape[1]
    Kh = KVW // HD
    tq = 128
    scale = HD ** -0.5

    def body(q_ref, k_ref, v_ref, o_ref, lse_ref):
        qs = _stack_heads(q_ref, G)
        s = _dot_nt(qs, k_ref[...]) * scale
        m = jnp.max(s, axis=-1, keepdims=True)
        p = jnp.exp(s - m)
        l = jnp.sum(p, axis=-1, keepdims=True)
        o = _dot(p, v_ref[...]) / l
        lse = m + jnp.log(l)
        for g in range(G):
            o_ref[:, g * HD:(g + 1) * HD] = o[g * tq:(g + 1) * tq, :]
            lse_ref[g] = jnp.broadcast_to(lse[g * tq:(g + 1) * tq, :], (tq, HD))

    return _call(body, name=name, grid=(Kh, S // tq),
                 in_specs=[pl.BlockSpec((tq, G * HD), lambda h, i: (i, h)), pl.BlockSpec((R, HD), lambda h, i: (0, h)),
                           pl.BlockSpec((R, HD), lambda h, i: (0, h))],
                 out_specs=[pl.BlockSpec((tq, G * HD), lambda h, i: (i, h)),
                            pl.BlockSpec((None, G, tq, HD), lambda h, i: (h, 0, i, 0))],
                 out_shape=[_sds((S, D)), _sds((Kh, G, S, HD))])(q, k, v)


def _attn_bwd(q, k, v, O, LSE, dOb, S, G, name):
    R, KVW = k.shape
    D = q.shape[1]
    Kh = KVW // HD
    tq = 64
    scale = HD ** -0.5

    def body(q_ref, k_ref, v_ref, o_ref, lse_ref, do_ref, dq_ref, dk_ref, dv_ref):
        @pl.when(pl.program_id(1) == 0)
        def _():
            dk_ref[...] = jnp.zeros_like(dk_ref)
            dv_ref[...] = jnp.zeros_like(dv_ref)

        qs = _stack_heads(q_ref, G)
        do = _stack_heads(do_ref, G)
        o = _stack_heads(o_ref, G)
        lse = jnp.concatenate([lse_ref[g][:, 0:1] for g in range(G)], axis=0)
        delta = jnp.sum(do.astype(F32) * o, axis=-1, keepdims=True)
        kk, vv = k_ref[...], v_ref[...]
        p = jnp.exp(_dot_nt(qs, kk) * scale - lse)
        dp = _dot_nt(do, vv)
        ds = (p * (dp - delta) * scale).astype(_MXU)
        dq = _dot(ds, kk)
        for g in range(G):
            dq_ref[:, g * HD:(g + 1) * HD] = dq[g * tq:(g + 1) * tq, :]
        dk_ref[...] += _dot_tn(ds, qs)
        dv_ref[...] += _dot_tn(p, do)

    qspec = pl.BlockSpec((tq, G * HD), lambda h, i: (i, h))
    kspec = pl.BlockSpec((R, HD), lambda h, i: (0, h))
    return _call(body, name=name, grid=(Kh, S // tq),
                 in_specs=[qspec, kspec, kspec, qspec, pl.BlockSpec((None, G, tq, HD), lambda h, i: (h, 0, i, 0)), qspec],
                 out_specs=[qspec, kspec, kspec],
                 out_shape=[_sds((S, D)), _sds((R, KVW)), _sds((R, KVW))])(q, k, v, O, LSE, dOb)


def _halo_specs(R, CB, col0):
    nt8 = TR // 8
    return [pl.BlockSpec((8, CB), lambda h, i: (jnp.maximum(i * nt8 - 1, 0), col0 + h)),
            pl.BlockSpec((TR, CB), lambda h, i: (i, col0 + h)),
            pl.BlockSpec((8, CB), lambda h, i: (jnp.minimum((i + 1) * nt8, R // 8 - 1), col0 + h))]


def _seq_pos(i, S, R, CB):
    t = i * TR - 8 + lax.broadcasted_iota(jnp.int32, (TR + 16, CB), 0)
    start = jnp.where(t >= S, S, 0)
    end = jnp.where(t >= S, R, S)
    return t - start, end - t


def _shift(cat, by):
    return pltpu.roll(cat, by % cat.shape[0], 0)


def _gate_mats(xcb, w_ref, dirn, nb):
    return jnp.concatenate([_dot(xcb[:, b * HD:(b + 1) * HD], w_ref[dirn, b]) for b in range(nb)], axis=1)


def _lru_gates_fwd(P, conv_w, conv_b, WA, WX, ba, bx, lam, S, D, col0, name):
    R = P.shape[0]
    CB = D // 2
    nb = CB // HD

    def body(xp_ref, x_ref, xn_ref, cw_ref, cb_ref, wa_ref, wx_ref, ba_ref, bx_ref, lam_ref,
             xc_ref, af_ref, uf_ref, ab_ref, ub_ref):
        i = pl.program_id(1)
        cat = jnp.concatenate([xp_ref[...], x_ref[...], xn_ref[...]], axis=0)
        from_start, to_end = _seq_pos(i, S, R, CB)
        conv = (cb_ref[...] + cw_ref[2:3, :] * cat
                + cw_ref[0:1, :] * jnp.where(from_start >= 2, _shift(cat, 2), 0.0)
                + cw_ref[1:2, :] * jnp.where(from_start >= 1, _shift(cat, 1), 0.0)
                + cw_ref[3:4, :] * jnp.where(to_end >= 2, _shift(cat, -1), 0.0))
        xc = conv[8:8 + TR, :]
        xc_ref[...] = xc
        xcb = xc.astype(_MXU)
        for dirn, (a_ref, u_ref) in enumerate(((af_ref, uf_ref), (ab_ref, ub_ref))):
            ra = _sigmoid(_gate_mats(xcb, wa_ref, dirn, nb) + ba_ref[dirn:dirn + 1, :])
            ia = _sigmoid(_gate_mats(xcb, wx_ref, dirn, nb) + bx_ref[dirn:dirn + 1, :])
            nl = -lam_ref[dirn:dirn + 1, :]
            sp = jnp.maximum(nl, 0.0) + jnp.log(1.0 + jnp.exp(-jnp.abs(nl)))
            la = (-LRU_C) * ra * sp
            a_ref[...] = jnp.exp(la)
            u_ref[...] = jnp.sqrt(1.0 - jnp.exp(2.0 * la)) * (ia * xc)

    def par(r):
        return pl.BlockSpec((r, CB), lambda h, i: (0, h))

    wspec = pl.BlockSpec((2, nb, HD, HD), lambda h, i: (0, h, 0, 0))
    out = pl.BlockSpec((TR, CB), lambda h, i: (i, h))
    return _call(body, name=name, grid=(2, R // TR),
                 in_specs=_halo_specs(R, CB, col0) + [par(4), par(1), wspec, wspec, par(2), par(2), par(2)],
                 out_specs=[out] * 5, out_shape=[_sds((R, D))] * 5,
                 )(P, P, P, conv_w, conv_b, WA, WX, ba, bx, lam)


def _lru_gates_bwd(xc, hfp, hbp, gf, gb, WA, WX, ba, bx, lam, name):
    R, D = xc.shape
    CB = D // 2
    nb = CB // HD

    def body(xc_ref, hfp_ref, hbp_ref, gf_ref, gb_ref, wa_ref, wx_ref, ba_ref, bx_ref, lam_ref,
             dxc_ref, dwa_ref, dwx_ref, dba_ref, dbx_ref, dlam_ref):
        @pl.when(pl.program_id(1) == 0)
        def _():
            for r in (dwa_ref, dwx_ref, dba_ref, dbx_ref, dlam_ref):
                r[...] = jnp.zeros_like(r)

        xc = xc_ref[...]
        xcb = xc.astype(_MXU)
        dxc = jnp.zeros_like(xc)
        for dirn, (hp_ref, g_ref) in enumerate(((hfp_ref, gf_ref), (hbp_ref, gb_ref))):
            ra = _sigmoid(_gate_mats(xcb, wa_ref, dirn, nb) + ba_ref[dirn:dirn + 1, :])
            ia = _sigmoid(_gate_mats(xcb, wx_ref, dirn, nb) + bx_ref[dirn:dirn + 1, :])
            nl = -lam_ref[dirn:dirn + 1, :]
            sp = jnp.maximum(nl, 0.0) + jnp.log(1.0 + jnp.exp(-jnp.abs(nl)))
            la = (-LRU_C) * ra * sp
            a = jnp.exp(la)
            e2 = jnp.exp(2.0 * la)
            s = jnp.sqrt(1.0 - e2)
            du = g_ref[...]
            dla = du * hp_ref[...] * a - du * (ia * xc) * (e2 / s)
            dxc += du * s * ia
            dza = (dla * (-LRU_C) * sp) * ra * (1.0 - ra)
            dzx = (du * s * xc) * ia * (1.0 - ia)
            dlam_ref[dirn:dirn + 1, :] += jnp.sum(dla * (LRU_C * ra) * _sigmoid(nl), axis=0, keepdims=True)
            dba_ref[dirn:dirn + 1, :] += jnp.sum(dza, axis=0, keepdims=True)
            dbx_ref[dirn:dirn + 1, :] += jnp.sum(dzx, axis=0, keepdims=True)
            dzab, dzxb = dza.astype(_MXU), dzx.astype(_MXU)
            parts = []
            for b in range(nb):
                sl = slice(b * HD, (b + 1) * HD)
                dwa_ref[dirn, b] += _dot_tn(xcb[:, sl], dzab[:, sl])
                dwx_ref[dirn, b] += _dot_tn(xcb[:, sl], dzxb[:, sl])
                parts.append(_dot_nt(dzab[:, sl], wa_ref[dirn, b]) + _dot_nt(dzxb[:, sl], wx_ref[dirn, b]))
            dxc += jnp.concatenate(parts, axis=1)
        dxc_ref[...] = dxc

    def par(r):
        return pl.BlockSpec((r, CB), lambda h, i: (0, h))

    wspec = pl.BlockSpec((2, nb, HD, HD), lambda h, i: (0, h, 0, 0))
    tile = pl.BlockSpec((TR, CB), lambda h, i: (i, h))
    nbt = D // HD
    return _call(body, name=name, grid=(2, R // TR),
                 in_specs=[tile] * 5 + [wspec, wspec, par(2), par(2), par(2)],
                 out_specs=[tile, wspec, wspec, par(2), par(2), par(2)],
                 out_shape=[_sds((R, D)), _sds((2, nbt, HD, HD)), _sds((2, nbt, HD, HD)), _sds((2, D)), _sds((2, D)),
                            _sds((2, D))])(xc, hfp, hbp, gf, gb, WA, WX, ba, bx, lam)


def _scan_rows(n_groups, step, init):
    return lax.fori_loop(0, n_groups, lambda gi, c: step(pl.multiple_of(gi * 8, 8), c), init)


def _lru_scan_fwd(af, uf, ab, ub, S, name):
    R, D = af.shape
    W = min(SCAN_W, D)
    nm, nx = R // TR, S // TR
    nc = nm - nx
    ng = TR // 8

    def body(af_ref, uf_ref, ab_ref, ub_ref, hf_ref, hfp_ref, hb_ref, hbp_ref, cf_ref, cb_ref):
        @pl.when(pl.program_id(1) == 0)
        def _():
            cf_ref[...] = jnp.zeros_like(cf_ref)
            cb_ref[...] = jnp.zeros_like(cb_ref)

        def step(base, carry):
            hf, hb = carry
            baseb = pl.multiple_of(TR - 8 - base, 8)
            for r in range(8):
                tf, tb = base + r, baseb + 7 - r
                hfp_ref[pl.ds(tf, 1), :] = hf
                hf = af_ref[pl.ds(tf, 1), :] * hf + uf_ref[pl.ds(tf, 1), :]
                hf_ref[pl.ds(tf, 1), :] = hf
                hbp_ref[pl.ds(tb, 1), :] = hb
                hb = ab_ref[pl.ds(tb, 1), :] * hb + ub_ref[pl.ds(tb, 1), :]
                hb_ref[pl.ds(tb, 1), :] = hb
            return hf, hb

        hf, hb = _scan_rows(ng, step, (cf_ref[0:1, :], cb_ref[0:1, :]))
        cf_ref[0:1, :] = hf
        cb_ref[0:1, :] = hb

    fmap = lambda j, s: (jnp.where(s < nc, nx + s, s - nc), j)
    bmap = lambda j, s: (nm - 1 - s, j)
    fs, bs = pl.BlockSpec((TR, W), fmap), pl.BlockSpec((TR, W), bmap)
    return _call(body, name=name, grid=(D // W, nm), in_specs=[fs, fs, bs, bs], out_specs=[fs, fs, bs, bs],
                 out_shape=[_sds((R, D))] * 4, scratch=[pltpu.VMEM((8, W), F32), pltpu.VMEM((8, W), F32)])(af, uf, ab, ub)


def _lru_scan_bwd(af, ab, dhs, S, name):
    R, D = af.shape
    W = min(SCAN_W, D)
    nm, nx = R // TR, S // TR
    ng = TR // 8

    def body(af_ref, dhf_ref, ab_ref, dhb_ref, gf_ref, gb_ref, cf_ref, cb_ref):
        @pl.when(pl.program_id(1) == 0)
        def _():
            cf_ref[...] = jnp.zeros_like(cf_ref)
            cb_ref[...] = jnp.zeros_like(cb_ref)

        def step(base, carry):
            cf, cb = carry
            based = pl.multiple_of(TR - 8 - base, 8)
            for r in range(8):
                tf, tb = based + 7 - r, base + r
                g = dhf_ref[pl.ds(tf, 1), :] + cf
                gf_ref[pl.ds(tf, 1), :] = g
                cf = af_ref[pl.ds(tf, 1), :] * g
                g = dhb_ref[pl.ds(tb, 1), :] + cb
                gb_ref[pl.ds(tb, 1), :] = g
                cb = ab_ref[pl.ds(tb, 1), :] * g
            return cf, cb

        cf, cb = _scan_rows(ng, step, (cf_ref[0:1, :], cb_ref[0:1, :]))
        cf_ref[0:1, :] = cf
        cb_ref[0:1, :] = cb

    fmap = lambda j, s: (jnp.where(s < nx, nx - 1 - s, nm - 1 - (s - nx)), j)
    bmap = lambda j, s: (s, j)
    fs, bs = pl.BlockSpec((TR, W), fmap), pl.BlockSpec((TR, W), bmap)
    return _call(body, name=name, grid=(D // W, nm), in_specs=[fs, fs, bs, bs], out_specs=[fs, bs],
                 out_shape=[_sds((R, D))] * 2, scratch=[pltpu.VMEM((8, W), F32), pltpu.VMEM((8, W), F32)])(af, dhs, ab, dhs)


def _merge_fwd(P, hf, hb, O, S, D, col_lg, name):
    R = P.shape[0]
    CB = D // 2
    nx = S // TR

    def body(lg_ref, ga_ref, gl_ref, hf_ref, hb_ref, o_ref, mix_ref):
        ge, _ = _gelu_and_grad(lg_ref[...])
        lru = (hf_ref[...] + hb_ref[...]) * ge
        mix_ref[...] = (_sigmoid(ga_ref[...]) * o_ref[...] + _sigmoid(gl_ref[...]) * lru).astype(mix_ref.dtype)

    def col(c0):
        return pl.BlockSpec((TR, CB), lambda h, i: (i, c0 + h))

    return _call(body, name=name, grid=(2, R // TR),
                 in_specs=[col(col_lg), col(col_lg + 2), col(col_lg + 4), col(0), col(0),
                           pl.BlockSpec((TR, CB), lambda h, i: (jnp.minimum(i, nx - 1), h))],
                 out_specs=col(0), out_shape=_sds((R, D), _MXU))(P, P, P, hf, hb, O)


def _merge_bwd(dmix, P, hf, hb, O, S, D, col_lg, name):
    R = P.shape[0]
    CB = D // 2
    nx = S // TR

    def body(dm_ref, lg_ref, ga_ref, gl_ref, hf_ref, hb_ref, o_ref, do_ref, dhs_ref, dp_ref, stash):
        i, sec = pl.program_id(1), pl.program_id(2)

        @pl.when(sec == 0)
        def _():
            dm = jnp.where(i < nx, dm_ref[...], 0.0)
            sa, sl = _sigmoid(ga_ref[...]), _sigmoid(gl_ref[...])
            ge, dge = _gelu_and_grad(lg_ref[...])
            hs = hf_ref[...] + hb_ref[...]
            o = o_ref[...]
            dl = dm * sl
            do_ref[...] = (dm * sa).astype(do_ref.dtype)
            dhs_ref[...] = dl * ge
            stash[0] = (dl * hs * dge).astype(stash.dtype)
            stash[1] = (dm * o * sa * (1.0 - sa)).astype(stash.dtype)
            stash[2] = (dm * (hs * ge) * sl * (1.0 - sl)).astype(stash.dtype)

        dp_ref[...] = stash[sec]

    def col(c0):
        return pl.BlockSpec((TR, CB), lambda h, i, s: (i, c0 + h))

    xrow = pl.BlockSpec((TR, CB), lambda h, i, s: (jnp.minimum(i, nx - 1), h))
    return _call(body, name=name, grid=(2, R // TR, 3),
                 in_specs=[xrow, col(col_lg), col(col_lg + 2), col(col_lg + 4), col(0), col(0), xrow],
                 out_specs=[col(0), col(0), pl.BlockSpec((TR, CB), lambda h, i, s: (i, col_lg + 2 * s + h))],
                 out_shape=[_sds((R, D), _MXU), _sds((R, D)), _sds(P.shape, _MXU)],
                 scratch=[pltpu.VMEM((3, TR, CB), _MXU)])(dmix, P, P, P, hf, hb, O)


def _conv_bwd(dxc, P, conv_w, dP, S, D, col0, name):
    R = P.shape[0]
    CB = D // 2

    def body(dp_, d_ref, dn_, xp_ref, x_ref, xn_ref, cw_ref, dp_in, dpo_ref, dcw_ref, dcb_ref):
        i = pl.program_id(1)

        @pl.when(i == 0)
        def _():
            dcw_ref[...] = jnp.zeros_like(dcw_ref)
            dcb_ref[...] = jnp.zeros_like(dcb_ref)

        d = d_ref[...]
        catd = jnp.concatenate([dp_[...], d, dn_[...]], axis=0)
        catx = jnp.concatenate([xp_ref[...], x_ref[...], xn_ref[...]], axis=0)
        from_start, to_end = _seq_pos(i, S, R, CB)
        dxl = (cw_ref[2:3, :] * catd
               + cw_ref[0:1, :] * jnp.where(to_end >= 3, _shift(catd, -2), 0.0)
               + cw_ref[1:2, :] * jnp.where(to_end >= 2, _shift(catd, -1), 0.0)
               + cw_ref[3:4, :] * jnp.where(from_start >= 1, _shift(catd, 1), 0.0))
        dpo_ref[...] = dxl[8:8 + TR, :].astype(dpo_ref.dtype)
        taps = (jnp.where(from_start >= 2, _shift(catx, 2), 0.0), jnp.where(from_start >= 1, _shift(catx, 1), 0.0),
                catx, jnp.where(to_end >= 2, _shift(catx, -1), 0.0))
        for kk in range(4):
            dcw_ref[kk:kk + 1, :] += jnp.sum(d * taps[kk][8:8 + TR, :], axis=0, keepdims=True)
        dcb_ref[...] += jnp.sum(d, axis=0, keepdims=True)

    return _call(body, name=name, grid=(2, R // TR),
                 in_specs=_halo_specs(R, CB, 0) + _halo_specs(R, CB, col0)
                 + [pl.BlockSpec((4, CB), lambda h, i: (0, h)), pl.BlockSpec(memory_space=pl.ANY)],
                 out_specs=[pl.BlockSpec((TR, CB), lambda h, i: (i, col0 + h)), pl.BlockSpec((4, CB), lambda h, i: (0, h)),
                            pl.BlockSpec((1, CB), lambda h, i: (0, h))],
                 out_shape=[_sds(dP.shape, dP.dtype), _sds((4, D)), _sds((1, D))],
                 aliases={7: 0})(dxc, dxc, dxc, P, P, P, conv_w, dP)


def _rope_tables(S, C):
    t = jnp.arange(S, dtype=jnp.int32)
    row = (t // GRID_W).astype(F32)
    col = (t % GRID_W).astype(F32)
    axis_dims = HD // 2
    freqs = ROPE_THETA ** (-jnp.arange(0, axis_dims, 2, dtype=F32) / axis_dims)
    ang = jnp.concatenate([row[:, None] * freqs, col[:, None] * freqs], axis=-1)
    cos = jnp.repeat(jnp.cos(ang), 2, axis=-1)
    sin = jnp.repeat(jnp.sin(ang), 2, axis=-1) * jnp.tile(jnp.array([-1.0, 1.0], F32), HD // 2)
    return (jnp.concatenate([cos, jnp.ones((C, HD), F32)], axis=0),
            jnp.concatenate([sin, jnp.zeros((C, HD), F32)], axis=0))


def _ffn_fwd(X, ng, mod2, k, layer, WGU, WD, rows, nx, tag):
    H = _norm_mod_fwd(X, ng, mod2, k, rows, nx, f"{tag}_norm")
    G, U, A = _ffn_up(H, WGU, layer, rows, f"{tag}_up")
    Y, Xn = _ffn_down(A, WD, layer, X, mod2, k, rows, nx * TR, f"{tag}_down")
    return Xn, (H, G, U, A, Y)


def _ffn_bwd(dXn, X, saved, ng, mod2, k, layer, WGU, WD, rows, nx, prev, tag):
    H, G, U, A, Y = saved
    D = X.shape[1]
    Fb = WGU.shape[-1]
    dYb, dgate = _res_bwd(dXn, Y, mod2, k, 0.5, rows, nx, f"{tag}_dres")
    dG, dU = _ffn_dact(dYb, WD, layer, G, U, rows, f"{tag}_dact")
    tm = _tm(rows)
    blk = pl.BlockSpec((None, tm, Fb), lambda d, r: (d, r, 0))
    row = pl.BlockSpec((tm, D), lambda d, r: (r, 0))
    pwg, pwu, pwd = prev if prev is not None else (None, None, None)
    dWD = _mm_tn(A, blk, dYb, row, (ND, 2, Fb, D), pl.BlockSpec((None, None, Fb, D), lambda d, r: (d, layer, 0, 0)),
                 rows, f"{tag}_dwd", pwd)
    wspec = pl.BlockSpec((None, None, D, Fb), lambda d, r: (d, layer, 0, 0))
    dWG = _mm_tn(H, row, dG, blk, (ND, 2, D, Fb), wspec, rows, f"{tag}_dwg", pwg)
    dWU = _mm_tn(H, row, dU, blk, (ND, 2, D, Fb), wspec, rows, f"{tag}_dwu", pwu)
    dH = _ffn_dh(dG, dU, WGU, layer, rows, f"{tag}_dh")
    dX, dsh, dsc, dng = _norm_mod_bwd(X, dH, dXn, ng, mod2, k, rows, nx, rows // TR, f"{tag}_dnorm")
    return dX, (dsh, dsc, dgate), dng, (dWG, dWU, dWD)


def _local_step(x, ctx, target, modx, modc, ng, WGU, WD, WIN, WOUT, qg, kg, conv_w, conv_b, WA, WX, ba, bx, lam, fg):
    S, D = x.shape
    C = ctx.shape[0]
    R = S + C
    nx = S // TR
    Nb = WIN.shape[-1]
    KVW = (ND * Nb - 5 * D) // 2
    G = D // KVW
    CB = D // 2
    col_lx = (D + 2 * KVW) // CB
    assert S % TR == 0 and C % TR == 0 and (D + 2 * KVW) % CB == 0 and CB % HD == 0
    mod2 = jnp.stack([modx, modc])
    X0 = jnp.concatenate([x, ctx], axis=0)
    COS, SIN = _rope_tables(S, C)
    ng0, ng1, ng2 = ng[0:1], ng[1:2], ng[2:3]

    X1, ffn1 = _ffn_fwd(X0, ng0, mod2, 0, 0, WGU, WD, R, nx, "ffn1")
    H2 = _norm_mod_fwd(X1, ng1, mod2, 1, R, nx, "mix_norm")
    P = _proj_in(H2, WIN, "proj_in")
    q, k, v = _qkv_prep(P, qg, kg, COS, SIN, D, KVW, "qkv_prep")
    O, LSE = _attn_fwd(q, k, v, S, G, "attn_fwd")
    xc, af, uf, ab, ub = _lru_gates_fwd(P, conv_w, conv_b, WA, WX, ba, bx, lam, S, D, col_lx, "lru_gates")
    hf, hfp, hb, hbp = _lru_scan_fwd(af, uf, ab, ub, S, "lru_scan")
    mixb = _merge_fwd(P, hf, hb, O, S, D, col_lx + 2, "merge")
    Z, X2 = _proj_out(mixb, WOUT, X1, mod2, S, "proj_out")
    X3, ffn2 = _ffn_fwd(X2, ng2, mod2, 2, 1, WGU, WD, S, nx, "ffn2")
    loss, dX3, dfg = _final_loss(X3, fg, target, "loss_head")

    dX2, (dsh3, dsc3, dg3), dng2, dffn = _ffn_bwd(dX3, X2, ffn2, ng2, mod2, 2, 1, WGU, WD, S, nx, None, "ffn2")
    dZb, dg2 = _res_bwd(dX2, Z, mod2, 1, 1.0, S, nx, "mix_dres")
    dmix = _dproj_out(dZb, WOUT, "dproj_out")
    dWOUT = _mm_tn(mixb, pl.BlockSpec((_tm(S), D // ND), lambda d, r: (r, d)), dZb, pl.BlockSpec((_tm(S), D), lambda d, r: (r, 0)),
                   (ND, D // ND, D), pl.BlockSpec((None, D // ND, D), lambda d, r: (d, 0, 0)), S, "dw_out")
    dOb, dhs, dP = _merge_bwd(dmix, P, hf, hb, O, S, D, col_lx + 2, "merge_bwd")
    gf, gb = _lru_scan_bwd(af, ab, dhs, S, "lru_scan_bwd")
    dxc, dWA, dWX, dba, dbx, dlam = _lru_gates_bwd(xc, hfp, hbp, gf, gb, WA, WX, ba, bx, lam, "lru_gates_bwd")
    dP, dconv_w, dconv_b = _conv_bwd(dxc, P, conv_w, dP, S, D, col_lx, "conv_bwd")
    dq, dk, dv = _attn_bwd(q, k, v, O, LSE, dOb, S, G, "attn_bwd")
    dP, dqg, dkg = _qkv_bwd(P, dq, dk, dv, qg, kg, COS, SIN, dP, D, KVW, nx, "qkv_bwd")
    dH2 = _dproj_in(dP, WIN, "dproj_in")
    dWIN = _mm_tn(H2, pl.BlockSpec((_tm(R), D), lambda d, r: (r, 0)), dP, pl.BlockSpec((_tm(R), Nb), lambda d, r: (r, d)),
                  (ND, D, Nb), pl.BlockSpec((None, D, Nb), lambda d, r: (d, 0, 0)), R, "dw_in")
    dX1, dsh2, dsc2, dng1 = _norm_mod_bwd(X1, dH2, dX2, ng1, mod2, 1, R, nx, nx, "mix_dnorm")
    dX0, (dsh1, dsc1, dg1), dng0, (dWG, dWU, dWD) = _ffn_bwd(dX1, X0, ffn1, ng0, mod2, 0, 0, WGU, WD, R, nx, dffn, "ffn1")

    zero = jnp.zeros((1, D), F32)
    dmodx = jnp.concatenate([dsh1[0], dsc1[0], dg1[0], dsh2[0], dsc2[0], dg2[0], dsh3[0], dsc3[0], dg3[0]], axis=0)
    dmodc = jnp.concatenate([dsh1[1], dsc1[1], dg1[1], dsh2[1], dsc2[1], zero, zero, zero, zero], axis=0)
    return dict(loss=loss, grad_x=dX0[:S], dmodx=dmodx, dmodc=dmodc, norm_g=jnp.concatenate([dng0, dng1, dng2], axis=0),
                ffn_wg=dWG, ffn_wu=dWU, ffn_wd=dWD, w_in=dWIN, w_out=dWOUT, q_norm_g=dqg, k_norm_g=dkg,
                conv_w=dconv_w, conv_b=dconv_b, lru_wa=dWA, lru_ba=dba, lru_wx=dWX, lru_bx=dbx, lru_lambda=dlam,
                final_norm_g=dfg)


def _mesh_pos():
    return lax.axis_index("x"), lax.axis_index("y"), lax.axis_index("c")


def _all_gather(xb, name, in_vmem=False):
    space = pltpu.VMEM if in_vmem else pl.ANY

    def body(x_ref, out_ref, send_sems, recv_sems, local_sem):
        x, y, c = _mesh_pos()
        me, sibling = (x, y, c), (x, y, 1 - c)
        chips = [(1 - x, y), (x, 1 - y), (1 - x, 1 - y)]

        def slot(px, py, pc):
            return out_ref.at[4 * px + 2 * py + pc]

        def copy(k, block, to, src=None):
            return pltpu.make_async_remote_copy(
                src_ref=slot(*block) if src is None else src, dst_ref=slot(*block),
                send_sem=send_sems.at[k], recv_sem=recv_sems.at[k], device_id=to, device_id_type=MESH)

        mine = pltpu.make_async_copy(x_ref, slot(*me), local_sem)
        mine.start()
        first = [copy(0, me, sibling, src=x_ref)]
        first += [copy(1 + j, me, (*chip, c), src=x_ref) for j, chip in enumerate(chips)]
        for cp in first:
            cp.start()
        passed = [copy(4 + j, (*chip, c), sibling) for j, chip in enumerate(chips)]
        for j, chip in enumerate(chips):
            copy(1 + j, (*chip, c), me).wait_recv()
            passed[j].start()
        copy(0, sibling, me).wait_recv()
        for j, chip in enumerate(chips):
            copy(4 + j, (*chip, 1 - c), me).wait_recv()
        for cp in first + passed:
            cp.wait_send()
        mine.wait()

    return pl.pallas_call(
        body, name=name, out_shape=_sds((ND,) + xb.shape, xb.dtype),
        in_specs=[pl.BlockSpec(memory_space=space)], out_specs=pl.BlockSpec(memory_space=space),
        scratch_shapes=[pltpu.SemaphoreType.DMA((7,)), pltpu.SemaphoreType.DMA((7,)), pltpu.SemaphoreType.DMA(())])(xb)


def _rs_sibling(Gp, name):
    def body(g_ref, buf_ref, send_sems, recv_sems):
        x, y, c = _mesh_pos()
        copies = []
        for k in range(4):
            copies.append(pltpu.make_async_remote_copy(
                src_ref=g_ref.at[2 * k + (1 - c)], dst_ref=buf_ref.at[k], send_sem=send_sems.at[k],
                recv_sem=recv_sems.at[k], device_id=(x, y, 1 - c), device_id_type=MESH))
        for cp in copies:
            cp.start()
        for cp in copies:
            cp.wait_recv()
        for cp in copies:
            cp.wait_send()

    return pl.pallas_call(
        body, name=name, out_shape=_sds((4,) + Gp.shape[1:], Gp.dtype),
        in_specs=[pl.BlockSpec(memory_space=pl.ANY)], out_specs=pl.BlockSpec(memory_space=pl.ANY),
        scratch_shapes=[pltpu.SemaphoreType.DMA((4,)), pltpu.SemaphoreType.DMA((4,))])(Gp)


def _rs_chips(T, name):
    def body(t_ref, buf_ref, send_sems, recv_sems):
        x, y, c = _mesh_pos()
        chips = [(1 - x, y), (x, 1 - y), (1 - x, 1 - y)]
        copies = []
        for j, (px, py) in enumerate(chips):
            copies.append(pltpu.make_async_remote_copy(
                src_ref=t_ref.at[2 * px + py], dst_ref=buf_ref.at[j], send_sem=send_sems.at[j],
                recv_sem=recv_sems.at[j], device_id=(px, py, c), device_id_type=MESH))
        for cp in copies:
            cp.start()
        for cp in copies:
            cp.wait_recv()
        for cp in copies:
            cp.wait_send()

    return pl.pallas_call(
        body, name=name, out_shape=_sds((3,) + T.shape[1:], T.dtype),
        in_specs=[pl.BlockSpec(memory_space=pl.ANY)], out_specs=pl.BlockSpec(memory_space=pl.ANY),
        scratch_shapes=[pltpu.SemaphoreType.DMA((3,)), pltpu.SemaphoreType.DMA((3,))])(T)


def _tile_rows(rows, cols):
    best = None
    for t in range(16, rows + 1, 16):
        if rows % t == 0 and t * cols * 4 <= (1 << 20):
            best = t
    return best if best is not None else rows


def _prefetch_call(body, *, name, grid, in_specs, out_specs, out_shape):
    return pl.pallas_call(
        body, name=name, out_shape=out_shape,
        grid_spec=pltpu.PrefetchScalarGridSpec(num_scalar_prefetch=1, grid=grid, in_specs=in_specs, out_specs=out_specs),
        compiler_params=pltpu.CompilerParams(dimension_semantics=("arbitrary",) * len(grid), vmem_limit_bytes=VMEM_LIMIT))


def _rs_add(Gp, bufA, idx, name):
    rows, cols = Gp.shape[1:]
    tr = _tile_rows(rows, cols)

    def body(i_ref, g_ref, a_ref, t_ref, own_ref):
        t = g_ref[...] + a_ref[...]
        t_ref[...] = t.astype(t_ref.dtype)

        @pl.when(pl.program_id(1) == i_ref[1])
        def _():
            own_ref[...] = t

    return _prefetch_call(
        body, name=name, grid=(rows // tr, 4),
        in_specs=[pl.BlockSpec((None, tr, cols), lambda r, k, i_ref: (2 * k + i_ref[0], r, 0)),
                  pl.BlockSpec((None, tr, cols), lambda r, k, i_ref: (k, r, 0))],
        out_specs=[pl.BlockSpec((None, tr, cols), lambda r, k, i_ref: (k, r, 0)),
                   pl.BlockSpec((tr, cols), lambda r, k, i_ref: (r, 0))],
        out_shape=[_sds((4, rows, cols), jnp.bfloat16), _sds((rows, cols))])(idx, Gp, bufA)


def _adam(w, g, m, v):
    m = ADAM_B1 * m + (1.0 - ADAM_B1) * g
    v = ADAM_B2 * v + (1.0 - ADAM_B2) * (g * g)
    m_hat = m / (1.0 - ADAM_B1 ** ADAM_STEP)
    v_hat = v / (1.0 - ADAM_B2 ** ADAM_STEP)
    return -ADAM_LR * (m_hat / (jnp.sqrt(v_hat) + ADAM_EPS) + ADAM_WD * w), m, v


def _rs_finish(Town, bufB, name, wmv=None):
    rows, cols = Town.shape
    tr = _tile_rows(rows, cols)
    n_in = 4 + (3 if wmv is not None else 0)

    def body(*refs):
        ins, outs = refs[:n_in], refs[n_in:]
        g = ((ins[0][...] + ins[1][...].astype(F32)) + ins[2][...].astype(F32)) + ins[3][...].astype(F32)
        outs[0][...] = g
        if wmv is not None:
            d, m, v = _adam(ins[4][...], g, ins[5][...], ins[6][...])
            outs[1][...] = d
            outs[2][...] = m
            outs[3][...] = v

    plain = pl.BlockSpec((tr, cols), lambda r: (r, 0))
    in_specs = [plain] + [pl.BlockSpec((None, tr, cols), (lambda j: lambda r: (j, r, 0))(j)) for j in range(3)]
    args = [Town, bufB, bufB, bufB]
    n_out = 1
    if wmv is not None:
        in_specs += [plain] * 3
        args += list(wmv)
        n_out = 4
    return _call(body, name=name, grid=(rows // tr,), in_specs=in_specs, out_specs=[plain] * n_out,
                 out_shape=[_sds((rows, cols))] * n_out)(*args)


def _reduce_scatter(Gp, idx, tag, wmv=None):
    bufA = _rs_sibling(Gp, f"{tag}_rs_sibling")
    T, Town = _rs_add(Gp, bufA, idx, f"{tag}_rs_add")
    bufB = _rs_chips(T, f"{tag}_rs_chips")
    return _rs_finish(Town, bufB, f"{tag}_rs_finish", wmv)


def _adamw_plain(w, g, m, v, name):
    rows, cols = w.shape
    tr = _tile_rows(rows, cols)

    def body(w_ref, g_ref, m_ref, v_ref, d_ref, mo_ref, vo_ref):
        d, m_, v_ = _adam(w_ref[...], g_ref[...], m_ref[...], v_ref[...])
        d_ref[...] = d
        mo_ref[...] = m_
        vo_ref[...] = v_

    spec = pl.BlockSpec((tr, cols), lambda r: (r, 0))
    return _call(body, name=name, grid=(rows // tr,), in_specs=[spec] * 4, out_specs=[spec] * 3,
                 out_shape=[_sds((rows, cols))] * 3)(w, g, m, v)


_MOD_TK = 512


def _mod_fwd(cc16, w_loc, b_loc, name):
    D, cols = w_loc.shape
    tk = min(_MOD_TK, D)
    nk = D // tk

    def body(c_ref, w_ref, b_ref, o_ref):
        kk = pl.program_id(0)

        @pl.when(kk == 0)
        def _():
            o_ref[...] = jnp.zeros_like(o_ref)

        cc = c_ref[...]
        o_ref[...] += _dot(cc * _sigmoid(cc), w_ref[...])

        @pl.when(kk == nk - 1)
        def _():
            o_ref[...] += b_ref[...]

    return _call(body, name=name, grid=(nk,),
                 in_specs=[pl.BlockSpec((16, tk), lambda kk: (0, kk)), pl.BlockSpec((tk, cols), lambda kk: (kk, 0)),
                           _full_spec((1, cols))],
                 out_specs=_full_spec((16, cols)), out_shape=_sds((16, cols)))(cc16, w_loc, b_loc)


def _mod_bwd(dm_loc, cc16, w_loc, name):
    D, cols = w_loc.shape

    def body(dm_ref, c_ref, w_ref, gw_ref, ds_ref):
        rows = [dm_ref[b, 0:1, :] for b in range(ND)]
        ctx = dm_ref[0, 1:2, :]
        for b in range(1, ND):
            ctx = ctx + dm_ref[b, 1:2, :]
        dm16 = jnp.concatenate(rows + [ctx, jnp.zeros((7, cols), F32)], axis=0)
        cc = c_ref[...]
        gw_ref[...] = _dot_tn(cc * _sigmoid(cc), dm16)
        ds_ref[...] = _dot_nt(dm16, w_ref[...])

    tk = min(_MOD_TK, D)
    return _call(body, name=name, grid=(D // tk,),
                 in_specs=[_full_spec((ND, 8, cols)), pl.BlockSpec((16, tk), lambda kk: (0, kk)),
                           pl.BlockSpec((tk, cols), lambda kk: (kk, 0))],
                 out_specs=[pl.BlockSpec((tk, cols), lambda kk: (kk, 0)), pl.BlockSpec((16, tk), lambda kk: (0, kk))],
                 out_shape=[_sds((D, cols)), _sds((16, D))])(dm_loc, cc16, w_loc)


def _bmod_grad(dm_all, name):
    n = dm_all.shape[-1]

    def body(dm_ref, o_ref):
        acc = dm_ref[0, 0:1, :] + dm_ref[0, 1:2, :]
        for b in range(1, ND):
            acc = (acc + dm_ref[b, 0:1, :]) + dm_ref[b, 1:2, :]
        o_ref[...] = acc

    return _call(body, name=name, grid=(1,), in_specs=[_full_spec((ND, 8, n))], out_specs=_full_spec((1, n)),
                 out_shape=_sds((1, n)))(dm_all)


_SMALL_ROWS = 24
_ROW_CCTX = 15


def _small_finish(parts, c_ctx, name):
    D = parts.shape[-1]

    def body(p_ref, c_ref, o_ref):
        acc = p_ref[0]
        for b in range(1, ND):
            acc = acc + p_ref[b]
        cc = c_ref[...]
        sg = _sigmoid(cc)
        dsilu = sg * (1.0 + cc * (1.0 - sg))
        row = lax.broadcasted_iota(jnp.int32, acc.shape, 0)
        o_ref[...] = jnp.where(row == _ROW_CCTX, acc * dsilu, acc)

    return _call(body, name=name, grid=(1,), in_specs=[_full_spec(parts.shape), _full_spec((1, D))],
                 out_specs=_full_spec((_SMALL_ROWS, D)), out_shape=_sds((_SMALL_ROWS, D)))(parts, c_ctx)


_WEIGHTS = ['c_ctx', 'w_mod', 'b_mod', 'norm_g', 'ffn_wg', 'ffn_wu', 'ffn_wd', 'w_in', 'w_out', 'q_norm_g', 'k_norm_g',
            'conv_w', 'conv_b', 'lru_wa', 'lru_ba', 'lru_wx', 'lru_bx', 'lru_lambda', 'final_norm_g']
_SMALL = ['c_ctx', 'b_mod', 'norm_g', 'q_norm_g', 'k_norm_g', 'conv_w', 'conv_b', 'lru_ba', 'lru_bx', 'lru_lambda',
          'final_norm_g']


def _pad_rows(a, rows):
    return jnp.pad(a, ((0, rows - a.shape[0]),) + ((0, 0),) * (a.ndim - 1))


def _step(w, m, v, x, c, ctx, loss_target):
    xi, yi, ci = _mesh_pos()
    me = 4 * xi + 2 * yi + ci
    idx = jnp.stack([ci, 2 * xi + yi]).astype(jnp.int32)
    S, D = x.shape[1:]
    Ds = D // ND
    cols = w['w_mod'].shape[-1]

    sp = jnp.concatenate([w['norm_g'][0], w['conv_w'][0], w['lru_ba'][0], w['lru_bx'][0], w['lru_lambda'][0]], axis=0)
    spg = _all_gather(_pad_rows(sp, 16), "ag_small_params", in_vmem=True)
    spf = jnp.transpose(spg, (1, 0, 2)).reshape(16, D)
    ng, conv_w, ba, bx, lam = spf[0:3], spf[3:7], spf[7:9], spf[9:11], spf[11:13]

    cg = _all_gather(_pad_rows(c, 8), "ag_cond", in_vmem=True)
    cc16 = _pad_rows(jnp.concatenate([cg[:, 0, :], w['c_ctx'][None, :]], axis=0), 16)
    b_loc = lax.dynamic_slice_in_dim(w['b_mod'], me * cols, cols, axis=1)
    mod_loc = _mod_fwd(cc16, w['w_mod'][0], b_loc, "mod_fwd")
    modg = _all_gather(mod_loc, "ag_mod", in_vmem=True)
    mod16 = jnp.transpose(modg, (1, 0, 2)).reshape(16, ND * cols)
    modx = lax.dynamic_index_in_dim(mod16, me, axis=0, keepdims=False).reshape(9, D)
    modc = mod16[8].reshape(9, D)

    WGU = _all_gather(jnp.stack([w['ffn_wg'][0], w['ffn_wu'][0]], axis=1).astype(_MXU), "ag_ffn_up")
    WD = _all_gather(w['ffn_wd'][0].astype(_MXU), "ag_ffn_down")
    WIN = _all_gather(w['w_in'][0].astype(_MXU), "ag_w_in")
    WOUT = _all_gather(w['w_out'][0].astype(_MXU), "ag_w_out").reshape(D, D)
    WA, WX = w['lru_wa'][0].astype(_MXU), w['lru_wx'][0].astype(_MXU)

    g = _local_step(x[0], ctx[0], loss_target[0], modx, modc, ng, WGU, WD, WIN, WOUT, w['q_norm_g'], w['k_norm_g'],
                    conv_w, w['conv_b'], WA, WX, ba, bx, lam, w['final_norm_g'][None, :])

    grad, delta, new_m, new_v = {}, {}, {}, {}

    for n in ('ffn_wg', 'ffn_wu', 'ffn_wd', 'w_in', 'w_out'):
        shard = w[n].shape
        c2 = shard[-1]
        view = lambda a: a.reshape(-1, c2)
        outs = _reduce_scatter(g[n].reshape(ND, -1, c2), idx, n, (view(w[n]), view(m[n]), view(v[n])))
        grad[n], delta[n], new_m[n], new_v[n] = [o.reshape(shard) for o in outs]

    lw = jnp.stack([g['lru_wa'], g['lru_wx']])
    (lsum,) = _reduce_scatter(lw.reshape(ND, -1, HD), idx, "lru_w")
    lfull = _all_gather(lsum, "ag_lru_w").reshape(lw.shape)
    for i, n in enumerate(('lru_wa', 'lru_wx')):
        shard = w[n].shape
        view = lambda a: a.reshape(-1, HD)
        grad[n] = lfull[i].reshape(shard)
        outs = _adamw_plain(view(w[n]), view(lfull[i]), view(m[n]), view(v[n]), f"adamw_{n}")
        delta[n], new_m[n], new_v[n] = [o.reshape(shard) for o in outs]

    dm = _pad_rows(jnp.stack([g['dmodx'].reshape(-1), g['dmodc'].reshape(-1)]), 8)
    dm_all = _all_gather(dm, "ag_dmod", in_vmem=True)
    dm_loc = lax.dynamic_slice_in_dim(dm_all, me * cols, cols, axis=2)
    gw_mod, dsil = _mod_bwd(dm_loc, cc16, w['w_mod'][0], "mod_bwd")
    grad['w_mod'] = gw_mod[None]
    outs = _adamw_plain(w['w_mod'][0], gw_mod, m['w_mod'][0], v['w_mod'][0], "adamw_w_mod")
    delta['w_mod'], new_m['w_mod'], new_v['w_mod'] = [o[None] for o in outs]
    grad['b_mod'] = _bmod_grad(dm_all, "bmod_grad")

    pad_d = lambda a: jnp.concatenate([a, jnp.zeros((1, D - a.shape[1]), F32)], axis=1)
    small = jnp.concatenate([g['norm_g'], g['conv_w'], g['conv_b'], g['lru_ba'], g['lru_bx'], g['lru_lambda'],
                             g['final_norm_g'], dsil[8:9], pad_d(g['q_norm_g']), pad_d(g['k_norm_g'])], axis=0)
    parts = _all_gather(_pad_rows(small, _SMALL_ROWS), "ag_small_grads", in_vmem=True)
    ssum = _small_finish(parts, w['c_ctx'][None, :], "small_finish")
    mine = lambda rows: lax.dynamic_slice_in_dim(rows, me * Ds, Ds, axis=1)
    grad['norm_g'] = mine(ssum[0:3])[None]
    grad['conv_w'] = mine(ssum[3:7])[None]
    grad['conv_b'] = ssum[7:8]
    grad['lru_ba'] = mine(ssum[8:10])[None]
    grad['lru_bx'] = mine(ssum[10:12])[None]
    grad['lru_lambda'] = mine(ssum[12:14])[None]
    grad['final_norm_g'] = ssum[14]
    grad['c_ctx'] = ssum[_ROW_CCTX]
    grad['q_norm_g'] = ssum[16:17, :HD]
    grad['k_norm_g'] = ssum[17:18, :HD]

    def pack(d):
        flat = jnp.concatenate([d[n].reshape(-1) for n in _SMALL])
        padded = -(-flat.shape[0] // 1024) * 1024
        return jnp.concatenate([flat, jnp.zeros((padded - flat.shape[0],), F32)]).reshape(-1, HD)

    outs = _adamw_plain(pack(w), pack(grad), pack(m), pack(v), "adamw_small")
    off = 0
    for n in _SMALL:
        size = math.prod(w[n].shape)
        for dst, o in zip((delta, new_m, new_v), outs):
            dst[n] = o.reshape(-1)[off:off + size].reshape(w[n].shape)
        off += size

    loss = lax.psum(g['loss'][0, 0], ("x", "y", "c"))
    return (loss, g['grad_x'][None], *[grad[n] for n in _WEIGHTS], *[delta[n] for n in _WEIGHTS],
            *[new_m[n] for n in _WEIGHTS], *[new_v[n] for n in _WEIGHTS])


def kernel(x, c, ctx, c_ctx, w_mod, b_mod, norm_g, ffn_wg, ffn_wu, ffn_wd, w_in, w_out, q_norm_g, k_norm_g, conv_w, conv_b, lru_wa, lru_ba, lru_wx, lru_bx, lru_lambda, final_norm_g, loss_target, m_c_ctx, m_w_mod, m_b_mod, m_norm_g, m_ffn_wg, m_ffn_wu, m_ffn_wd, m_w_in, m_w_out, m_q_norm_g, m_k_norm_g, m_conv_w, m_conv_b, m_lru_wa, m_lru_ba, m_lru_wx, m_lru_bx, m_lru_lambda, m_final_norm_g, v_c_ctx, v_w_mod, v_b_mod, v_norm_g, v_ffn_wg, v_ffn_wu, v_ffn_wd, v_w_in, v_w_out, v_q_norm_g, v_k_norm_g, v_conv_w, v_conv_b, v_lru_wa, v_lru_ba, v_lru_wx, v_lru_bx, v_lru_lambda, v_final_norm_g):
    given = dict(locals())
    w = {n: given[n] for n in _WEIGHTS}
    m = {n: given["m_" + n] for n in _WEIGHTS}
    v = {n: given["v_" + n] for n in _WEIGHTS}
    return _step(w, m, v, x, c, ctx, loss_target)
```

```python
import functools
import math

import jax
import jax.numpy as jnp
from jax import lax
from jax.experimental import pallas as pl
from jax.experimental.pallas import tpu as pltpu

F32 = jnp.float32
_MXU = jnp.bfloat16
ND = 8
TR = 256
HD = 128
EPS = 1e-6
GRID_W = 64
ROPE_THETA = 10000.0
LRU_C = 8.0
VMEM_LIMIT = 56 * 1024 * 1024
SCAN_W = 512
ADAM_LR, ADAM_B1, ADAM_B2, ADAM_EPS, ADAM_WD, ADAM_STEP = 0.001, 0.9, 0.999, 1e-08, 0.01, 10
MESH = pl.DeviceIdType.MESH


class _Job:
    def __init__(self, inputs, out_shapes, sems, start, finish):
        self.inputs, self.out_shapes, self.sems, self.start, self.finish = inputs, out_shapes, sems, start, finish


def _call(body, *, name, grid, in_specs, out_specs, out_shape, scratch=(), aliases=None, jobs=()):
    params = pltpu.CompilerParams(dimension_semantics=("arbitrary",) * len(grid), vmem_limit_bytes=VMEM_LIMIT)
    if not jobs:
        return pl.pallas_call(body, name=name, grid=grid, in_specs=in_specs, out_specs=out_specs, out_shape=out_shape,
                              scratch_shapes=scratch, input_output_aliases=aliases or {}, compiler_params=params)
    single = not isinstance(out_specs, (list, tuple))
    o_specs = [out_specs] if single else list(out_specs)
    o_shape = [out_shape] if single else list(out_shape)
    n_in, n_out, n_scr = len(in_specs), len(o_specs), len(scratch)
    j_in = [a for j in jobs for a in j.inputs]
    j_out = [s for j in jobs for s in j.out_shapes]
    j_sem = [s for j in jobs for s in j.sems]
    hbm = pl.BlockSpec(memory_space=pl.ANY)

    def wrapped(*refs):
        ins, rest = refs[:n_in], refs[n_in:]
        jin, rest = rest[:len(j_in)], rest[len(j_in):]
        outs, rest = rest[:n_out], rest[n_out:]
        jout, rest = rest[:len(j_out)], rest[len(j_out):]
        scr, jsem = rest[:n_scr], rest[n_scr:]
        first = functools.reduce(jnp.logical_and, [pl.program_id(a) == 0 for a in range(len(grid))])
        last = functools.reduce(jnp.logical_and, [pl.program_id(a) == grid[a] - 1 for a in range(len(grid))])

        def each(which):
            i = o = s = 0
            for j in jobs:
                ni, no, ns = len(j.inputs), len(j.out_shapes), len(j.sems)
                getattr(j, which)(jin[i:i + ni], jout[o:o + no], jsem[s:s + ns])
                i, o, s = i + ni, o + no, s + ns

        @pl.when(first)
        def _():
            each("start")

        body(*ins, *outs, *scr)

        @pl.when(last)
        def _():
            each("finish")

    call = pl.pallas_call(wrapped, name=name, grid=grid, in_specs=list(in_specs) + [hbm] * len(j_in),
                          out_specs=o_specs + [hbm] * len(j_out), out_shape=o_shape + j_out,
                          scratch_shapes=list(scratch) + j_sem, input_output_aliases=aliases or {}, compiler_params=params)

    def run(*args):
        res = call(*args, *j_in)
        comp = res[0] if single else list(res[:n_out])
        jres, o = [], n_out
        for j in jobs:
            jres.append(list(res[o:o + len(j.out_shapes)]))
            o += len(j.out_shapes)
        return comp, jres

    return run


def _comm_call(jobs, name):
    j_in = [a for j in jobs for a in j.inputs]
    j_out = [s for j in jobs for s in j.out_shapes]
    j_sem = [s for j in jobs for s in j.sems]
    hbm = pl.BlockSpec(memory_space=pl.ANY)

    def body(*refs):
        jin, jout, jsem = refs[:len(j_in)], refs[len(j_in):len(j_in) + len(j_out)], refs[len(j_in) + len(j_out):]
        for which in ("start", "finish"):
            i = o = s = 0
            for j in jobs:
                ni, no, ns = len(j.inputs), len(j.out_shapes), len(j.sems)
                getattr(j, which)(jin[i:i + ni], jout[o:o + no], jsem[s:s + ns])
                i, o, s = i + ni, o + no, s + ns

    res = pl.pallas_call(body, name=name, out_shape=j_out, in_specs=[hbm] * len(j_in), out_specs=[hbm] * len(j_out),
                         scratch_shapes=j_sem)(*j_in)
    jres, o = [], 0
    for j in jobs:
        jres.append(list(res[o:o + len(j.out_shapes)]))
        o += len(j.out_shapes)
    return jres


def _sds(shape, dtype=F32):
    return jax.ShapeDtypeStruct(tuple(shape), dtype)


def _dot(a, b):
    return jnp.dot(a.astype(_MXU), b.astype(_MXU), preferred_element_type=F32)


def _dot_nt(a, b):
    return lax.dot_general(a.astype(_MXU), b.astype(_MXU), (((1,), (1,)), ((), ())), preferred_element_type=F32)


def _dot_tn(a, b):
    return lax.dot_general(a.astype(_MXU), b.astype(_MXU), (((0,), (0,)), ((), ())), preferred_element_type=F32)


def _sigmoid(x):
    return 1.0 / (1.0 + jnp.exp(-x))


_GELU_C = math.sqrt(2.0 / math.pi)


def _gelu_and_grad(x):
    x2 = x * x
    t = jnp.tanh(_GELU_C * (x + 0.044715 * x * x2))
    ge = 0.5 * x * (1.0 + t)
    dge = 0.5 * (1.0 + t) + 0.5 * x * (1.0 - t * t) * (_GELU_C * (1.0 + 3.0 * 0.044715 * x2))
    return ge, dge


def _tm(rows):
    assert rows % 4 == 0 and (rows // 4) % 16 == 0
    return rows // 4


def _row_spec(width, nmax=None):
    if nmax is None:
        return pl.BlockSpec((TR, width), lambda i: (i, 0))
    return pl.BlockSpec((TR, width), lambda i: (jnp.minimum(i, nmax), 0))


def _full_spec(shape):
    n = len(shape)
    return pl.BlockSpec(tuple(shape), lambda *_: (0,) * n)


def _mod_spec(D, nx):
    return pl.BlockSpec((None, 9, D), lambda i: (i // nx, 0, 0))


def _norm_mod_fwd(X, ng, mod2, k, rows, nx, name):
    D = X.shape[1]

    def body(x_ref, g_ref, mod_ref, h_ref):
        x = x_ref[...]
        r = lax.rsqrt(jnp.mean(x * x, axis=-1, keepdims=True) + EPS)
        n = (x * r) * g_ref[...]
        h_ref[...] = (n * (1.0 + mod_ref[3 * k + 1:3 * k + 2, :]) + mod_ref[3 * k:3 * k + 1, :]).astype(h_ref.dtype)

    return _call(body, name=name, grid=(rows // TR,),
                 in_specs=[_row_spec(D), _full_spec((1, D)), _mod_spec(D, nx)],
                 out_specs=_row_spec(D), out_shape=_sds((rows, D), _MXU))(X, ng, mod2)


def _norm_mod_bwd(X, dH, dXres, ng, mod2, k, rows, nx, res_tiles, name, jobs=()):
    D = X.shape[1]
    ngroups = -(-(rows // TR) // nx)

    def body(x_ref, dh_ref, dres_ref, g_ref, mod_ref, dx_ref, dsh_ref, dsc_ref, dng_ref):
        i = pl.program_id(0)
        x = x_ref[...]
        dh = dh_ref[...]
        g = g_ref[...]
        r = lax.rsqrt(jnp.mean(x * x, axis=-1, keepdims=True) + EPS)
        xh = x * r
        n = xh * g
        dn_mod = dh * (1.0 + mod_ref[3 * k + 1:3 * k + 2, :])

        @pl.when(i % nx == 0)
        def _():
            dsh_ref[...] = jnp.zeros_like(dsh_ref)
            dsc_ref[...] = jnp.zeros_like(dsc_ref)

        @pl.when(i == 0)
        def _():
            dng_ref[...] = jnp.zeros_like(dng_ref)

        dsh_ref[...] += jnp.sum(dh, axis=0, keepdims=True)
        dsc_ref[...] += jnp.sum(dh * n, axis=0, keepdims=True)
        dng_ref[...] += jnp.sum(dn_mod * xh, axis=0, keepdims=True)
        dn = dn_mod * g
        dres = jnp.where(i < res_tiles, dres_ref[...], 0.0)
        dx_ref[...] = r * (dn - xh * jnp.mean(dn * xh, axis=-1, keepdims=True)) + dres

    grp = pl.BlockSpec((None, 1, D), lambda i: (i // nx, 0, 0))
    return _call(body, jobs=jobs, name=name, grid=(rows // TR,),
                 in_specs=[_row_spec(D), _row_spec(D), _row_spec(D, res_tiles - 1), _full_spec((1, D)), _mod_spec(D, nx)],
                 out_specs=[_row_spec(D), grp, grp, _full_spec((1, D))],
                 out_shape=[_sds((rows, D)), _sds((ngroups, 1, D)), _sds((ngroups, 1, D)), _sds((1, D))],
                 )(X, dH, dXres, ng, mod2)


def _res_bwd(dX, Y, mod2, k, coef, rows, nx, name):
    D = dX.shape[1]
    ngroups = -(-(rows // TR) // nx)

    def body(dx_ref, y_ref, mod_ref, dy_ref, dg_ref):
        i = pl.program_id(0)
        dx = dx_ref[...]

        @pl.when(i % nx == 0)
        def _():
            dg_ref[...] = jnp.zeros_like(dg_ref)

        dy_ref[...] = ((coef * mod_ref[3 * k + 2:3 * k + 3, :]) * dx).astype(dy_ref.dtype)
        dg_ref[...] += jnp.sum(coef * dx * y_ref[...], axis=0, keepdims=True)

    return _call(body, name=name, grid=(rows // TR,),
                 in_specs=[_row_spec(D), _row_spec(D), _mod_spec(D, nx)],
                 out_specs=[_row_spec(D), pl.BlockSpec((None, 1, D), lambda i: (i // nx, 0, 0))],
                 out_shape=[_sds((rows, D), _MXU), _sds((ngroups, 1, D))])(dX, Y, mod2)


def _final_loss(X3, fg, target, name):
    S, D = X3.shape

    def body(x_ref, g_ref, t_ref, loss_ref, dx_ref, dg_ref):
        i = pl.program_id(0)
        x = x_ref[...]
        g = g_ref[...]
        r = lax.rsqrt(jnp.mean(x * x, axis=-1, keepdims=True) + EPS)
        n = x * r
        err = n * g - t_ref[...]

        @pl.when(i == 0)
        def _():
            loss_ref[...] = jnp.zeros_like(loss_ref)
            dg_ref[...] = jnp.zeros_like(dg_ref)

        loss_ref[...] += 0.5 * jnp.sum(jnp.mean(err * err, axis=-1, keepdims=True), axis=0, keepdims=True)
        dy = err * (1.0 / D)
        dg_ref[...] += jnp.sum(dy * n, axis=0, keepdims=True)
        dn = dy * g
        dx_ref[...] = r * (dn - n * jnp.mean(dn * n, axis=-1, keepdims=True))

    return _call(body, name=name, grid=(S // TR,),
                 in_specs=[_row_spec(D), _full_spec((1, D)), _row_spec(D)],
                 out_specs=[_full_spec((1, 1)), _row_spec(D), _full_spec((1, D))],
                 out_shape=[_sds((1, 1)), _sds((S, D)), _sds((1, D))])(X3, fg, target)


def _ffn_up(H, WGU, layer, rows, name, jobs=()):
    D = H.shape[1]
    Fb = WGU.shape[-1]

    def body(h_ref, w_ref, g_ref, u_ref, a_ref):
        h = h_ref[...]
        g = _dot(h, w_ref[0])
        u = _dot(h, w_ref[1])
        g_ref[...] = g
        u_ref[...] = u
        a_ref[...] = ((g * _sigmoid(g)) * u).astype(a_ref.dtype)

    tm = _tm(rows)
    blk = pl.BlockSpec((None, tm, Fb), lambda d, m: (d, m, 0))
    return _call(body, jobs=jobs, name=name, grid=(ND, rows // tm),
                 in_specs=[pl.BlockSpec((tm, D), lambda d, m: (m, 0)),
                           pl.BlockSpec((None, None, 2, D, Fb), lambda d, m: (d, layer, 0, 0, 0))],
                 out_specs=[blk, blk, blk],
                 out_shape=[_sds((ND, rows, Fb)), _sds((ND, rows, Fb)), _sds((ND, rows, Fb), _MXU)])(H, WGU)


def _ffn_down(A, WD, layer, X, mod2, k, rows, S, name, jobs=()):
    Fb, D = WD.shape[-2:]
    tm = _tm(rows) // 2

    def body(a_ref, w_ref, x_ref, mod_ref, y_ref, xn_ref, acc_ref):
        d = pl.program_id(1)

        @pl.when(d == 0)
        def _():
            acc_ref[...] = jnp.zeros_like(acc_ref)

        acc_ref[...] += _dot(a_ref[...], w_ref[...])

        @pl.when(d == ND - 1)
        def _():
            y = acc_ref[...]
            y_ref[...] = y
            t = pl.program_id(0) * tm + lax.broadcasted_iota(jnp.int32, (tm, 1), 0)
            gate = jnp.where(t >= S, mod_ref[1, 3 * k + 2:3 * k + 3, :], mod_ref[0, 3 * k + 2:3 * k + 3, :])
            xn_ref[...] = x_ref[...] + (0.5 * gate) * y

    row = pl.BlockSpec((tm, D), lambda m, d: (m, 0))
    return _call(body, jobs=jobs, name=name, grid=(rows // tm, ND),
                 in_specs=[pl.BlockSpec((None, tm, Fb), lambda m, d: (d, m, 0)),
                           pl.BlockSpec((None, None, Fb, D), lambda m, d: (d, layer, 0, 0)),
                           row, pl.BlockSpec((2, 9, D), lambda m, d: (0, 0, 0))],
                 out_specs=[row, row], out_shape=[_sds((rows, D)), _sds((rows, D))],
                 scratch=[pltpu.VMEM((tm, D), F32)])(A, WD, X, mod2)


def _ffn_dact(dYb, WD, layer, G, U, rows, name, jobs=()):
    Fb, D = WD.shape[-2:]

    def body(dy_ref, w_ref, g_ref, u_ref, dg_ref, du_ref):
        da = _dot_nt(dy_ref[...], w_ref[...])
        g = g_ref[...]
        sg = _sigmoid(g)
        dg_ref[...] = (da * u_ref[...] * (sg * (1.0 + g * (1.0 - sg)))).astype(dg_ref.dtype)
        du_ref[...] = (da * (g * sg)).astype(du_ref.dtype)

    tm = _tm(rows)
    blk = pl.BlockSpec((None, tm, Fb), lambda d, m: (d, m, 0))
    return _call(body, jobs=jobs, name=name, grid=(ND, rows // tm),
                 in_specs=[pl.BlockSpec((tm, D), lambda d, m: (m, 0)),
                           pl.BlockSpec((None, None, Fb, D), lambda d, m: (d, layer, 0, 0)), blk, blk],
                 out_specs=[blk, blk],
                 out_shape=[_sds((ND, rows, Fb), _MXU), _sds((ND, rows, Fb), _MXU)])(dYb, WD, G, U)


def _ffn_dh(dG, dU, WGU, layer, rows, name, jobs=()):
    D, Fb = WGU.shape[-2:]

    def body(dg_ref, du_ref, w_ref, dh_ref, acc_ref):
        d = pl.program_id(1)

        @pl.when(d == 0)
        def _():
            acc_ref[...] = jnp.zeros_like(acc_ref)

        acc_ref[...] += _dot_nt(dg_ref[...], w_ref[0]) + _dot_nt(du_ref[...], w_ref[1])

        @pl.when(d == ND - 1)
        def _():
            dh_ref[...] = acc_ref[...]

    tm = _tm(rows)
    blk = pl.BlockSpec((None, tm, Fb), lambda m, d: (d, m, 0))
    return _call(body, jobs=jobs, name=name, grid=(rows // tm, ND),
                 in_specs=[blk, blk, pl.BlockSpec((None, None, 2, D, Fb), lambda m, d: (d, layer, 0, 0, 0))],
                 out_specs=pl.BlockSpec((tm, D), lambda m, d: (m, 0)), out_shape=_sds((rows, D)),
                 scratch=[pltpu.VMEM((tm, D), F32)])(dG, dU, WGU)


def _mm_tn(A, a_spec, B, b_spec, out_shape, out_spec, rows, name, prev=None, jobs=()):
    def body(*refs):
        a_ref, b_ref, o_ref = refs[0], refs[1], refs[-1]

        @pl.when(pl.program_id(1) == 0)
        def _():
            o_ref[...] = jnp.zeros_like(o_ref)

        o_ref[...] += _dot_tn(a_ref[...], b_ref[...])

    in_specs = [a_spec, b_spec]
    args = [A, B]
    aliases = None
    if prev is not None:
        in_specs.append(pl.BlockSpec(memory_space=pl.ANY))
        args.append(prev)
        aliases = {2: 0}
    return _call(body, jobs=jobs, name=name, grid=(ND, rows // _tm(rows)), in_specs=in_specs, out_specs=out_spec,
                 out_shape=_sds(out_shape), aliases=aliases)(*args)


def _proj_in(H2, WIN, name):
    R, D = H2.shape
    Nb = WIN.shape[-1]

    def body(h_ref, w_ref, p_ref):
        p_ref[...] = _dot(h_ref[...], w_ref[...])

    tm = _tm(R)
    return _call(body, name=name, grid=(ND, R // tm),
                 in_specs=[pl.BlockSpec((tm, D), lambda d, m: (m, 0)), pl.BlockSpec((None, D, Nb), lambda d, m: (d, 0, 0))],
                 out_specs=pl.BlockSpec((tm, Nb), lambda d, m: (m, d)), out_shape=_sds((R, ND * Nb)))(H2, WIN)


def _dproj_in(dP, WIN, name):
    R = dP.shape[0]
    D, Nb = WIN.shape[-2:]

    def body(dp_ref, w_ref, dh_ref, acc_ref):
        d = pl.program_id(1)

        @pl.when(d == 0)
        def _():
            acc_ref[...] = jnp.zeros_like(acc_ref)

        acc_ref[...] += _dot_nt(dp_ref[...], w_ref[...])

        @pl.when(d == ND - 1)
        def _():
            dh_ref[...] = acc_ref[...]

    tm = _tm(R)
    return _call(body, name=name, grid=(R // tm, ND),
                 in_specs=[pl.BlockSpec((tm, Nb), lambda m, d: (m, d)), pl.BlockSpec((None, D, Nb), lambda m, d: (d, 0, 0))],
                 out_specs=pl.BlockSpec((tm, D), lambda m, d: (m, 0)), out_shape=_sds((R, D)),
                 scratch=[pltpu.VMEM((tm, D), F32)])(dP, WIN)


def _proj_out(mixb, WOUT, X1, mod2, S, name):
    D = WOUT.shape[0]

    def body(m_ref, w_ref, x_ref, mod_ref, z_ref, xn_ref):
        z = _dot(m_ref[...], w_ref[...])
        z_ref[...] = z
        xn_ref[...] = x_ref[...] + mod_ref[5:6, :] * z

    return _call(body, name=name, grid=(S // TR,),
                 in_specs=[_row_spec(D), _full_spec((D, D)), _row_spec(D), pl.BlockSpec((None, 9, D), lambda i: (0, 0, 0))],
                 out_specs=[_row_spec(D), _row_spec(D)], out_shape=[_sds((S, D)), _sds((S, D))])(mixb, WOUT, X1, mod2)


def _dproj_out(dZb, WOUT, name):
    S, D = dZb.shape

    def body(dz_ref, w_ref, dm_ref):
        dm_ref[...] = _dot_nt(dz_ref[...], w_ref[...])

    return _call(body, name=name, grid=(S // TR,), in_specs=[_row_spec(D), _full_spec((D, D))],
                 out_specs=_row_spec(D), out_shape=_sds((S, D)))(dZb, WOUT)


def _pair_swap(t):
    lane = lax.broadcasted_iota(jnp.int32, t.shape, 1)
    return jnp.where(lane % 2 == 0, pltpu.roll(t, HD - 1, 1), pltpu.roll(t, 1, 1))


SM_SCALE = HD ** -0.5


def _qkv_prep(P, qg, kg, COS, SIN, D, KVW, name):
    R = P.shape[0]
    W = D + 2 * KVW
    nq, nk = D // HD, KVW // HD

    def body(p_ref, qg_ref, kg_ref, cos_ref, sin_ref, q_ref, k_ref, v_ref):
        cos, sin = cos_ref[...], sin_ref[...]

        def head(t, g):
            y = (t * lax.rsqrt(jnp.mean(t * t, axis=-1, keepdims=True) + EPS)) * g
            return y * cos + _pair_swap(y) * sin

        for h in range(nq):
            q_ref[:, h * HD:(h + 1) * HD] = (head(p_ref[:, h * HD:(h + 1) * HD], qg_ref[...]) * SM_SCALE).astype(q_ref.dtype)
        for h in range(nk):
            k_ref[:, h * HD:(h + 1) * HD] = head(p_ref[:, D + h * HD:D + (h + 1) * HD], kg_ref[...]).astype(k_ref.dtype)
        v_ref[...] = p_ref[:, D + KVW:W].astype(v_ref.dtype)

    return _call(body, name=name, grid=(R // TR,),
                 in_specs=[_row_spec(W), _full_spec((1, HD)), _full_spec((1, HD)), _row_spec(HD), _row_spec(HD)],
                 out_specs=[_row_spec(D), _row_spec(KVW), _row_spec(KVW)],
                 out_shape=[_sds((R, D), _MXU), _sds((R, KVW), _MXU), _sds((R, KVW), _MXU)])(P, qg, kg, COS, SIN)


def _qkv_bwd(P, dq, dk, dv, qg, kg, COS, SIN, dP, D, KVW, nx, name):
    R, INW = P.shape
    W = D + 2 * KVW
    nq, nk = D // HD, KVW // HD

    def body(p_ref, dq_ref, dk_ref, dv_ref, qg_ref, kg_ref, cos_ref, sin_ref, dp_in, dp_ref, dqg_ref, dkg_ref):
        i = pl.program_id(0)
        cos, sin = cos_ref[...], sin_ref[...]

        @pl.when(i == 0)
        def _():
            dqg_ref[...] = jnp.zeros_like(dqg_ref)
            dkg_ref[...] = jnp.zeros_like(dkg_ref)

        def head_bwd(t, g, dout):
            r = lax.rsqrt(jnp.mean(t * t, axis=-1, keepdims=True) + EPS)
            n = t * r
            dy = dout * cos + _pair_swap(dout * sin)
            dn = dy * g
            return r * (dn - n * jnp.mean(dn * n, axis=-1, keepdims=True)), jnp.sum(dy * n, axis=0, keepdims=True)

        dqg = jnp.zeros((1, HD), F32)
        for h in range(nq):
            sl = slice(h * HD, (h + 1) * HD)
            dt, dg = head_bwd(p_ref[:, sl], qg_ref[...], jnp.where(i < nx, dq_ref[:, sl] * SM_SCALE, 0.0))
            dp_ref[:, sl] = dt.astype(dp_ref.dtype)
            dqg += dg
        dkg = jnp.zeros((1, HD), F32)
        for h in range(nk):
            sl = slice(h * HD, (h + 1) * HD)
            dt, dg = head_bwd(p_ref[:, D + h * HD:D + (h + 1) * HD], kg_ref[...], dk_ref[:, sl])
            dp_ref[:, D + h * HD:D + (h + 1) * HD] = dt.astype(dp_ref.dtype)
            dkg += dg
        dqg_ref[...] += dqg
        dkg_ref[...] += dkg
        dp_ref[:, D + KVW:W] = dv_ref[...].astype(dp_ref.dtype)

    return _call(body, name=name, grid=(R // TR,),
                 in_specs=[_row_spec(W), _row_spec(D, nx - 1), _row_spec(KVW), _row_spec(KVW), _full_spec((1, HD)),
                           _full_spec((1, HD)), _row_spec(HD), _row_spec(HD), pl.BlockSpec(memory_space=pl.ANY)],
                 out_specs=[_row_spec(W), _full_spec((1, HD)), _full_spec((1, HD))],
                 out_shape=[_sds((R, INW), _MXU), _sds((1, HD)), _sds((1, HD))],
                 aliases={8: 0})(P, dq, dk, dv, qg, kg, COS, SIN, dP)


def _stack_heads(ref, G, dtype=None):
    parts = [ref[:, g * HD:(g + 1) * HD] for g in range(G)]
    out = jnp.concatenate(parts, axis=0)
    return out if dtype is None else out.astype(dtype)


_KEY_CHUNKS = 4


def _attn_fwd(q, k, v, S, G, name, jobs=()):
    R, KVW = k.shape
    D = q.shape[1]
    Kh = KVW // HD
    tq = 128
    kc = R // _KEY_CHUNKS
    assert R % _KEY_CHUNKS == 0 and kc % 16 == 0

    def body(q_ref, k_ref, v_ref, o_ref, lse_ref):
        qs = _stack_heads(q_ref, G)
        m = l = acc = None
        for c in range(_KEY_CHUNKS):
            s = _dot_nt(qs, k_ref[c * kc:(c + 1) * kc, :])
            mc = jnp.max(s, axis=-1, keepdims=True)
            m_new = mc if c == 0 else jnp.maximum(m, mc)
            p = jnp.exp(s - m_new)
            ps = jnp.sum(p, axis=-1, keepdims=True)
            pv = _dot(p, v_ref[c * kc:(c + 1) * kc, :])
            if c == 0:
                l, acc = ps, pv
            else:
                alpha = jnp.exp(m - m_new)
                l = alpha * l + ps
                acc = alpha * acc + pv
            m = m_new
        o = acc / l
        lse = m + jnp.log(l)
        for g in range(G):
            o_ref[:, g * HD:(g + 1) * HD] = o[g * tq:(g + 1) * tq, :]
            lse_ref[g] = jnp.broadcast_to(lse[g * tq:(g + 1) * tq, :], (tq, HD))

    return _call(body, jobs=jobs, name=name, grid=(Kh, S // tq),
                 in_specs=[pl.BlockSpec((tq, G * HD), lambda h, i: (i, h)), pl.BlockSpec((R, HD), lambda h, i: (0, h)),
                           pl.BlockSpec((R, HD), lambda h, i: (0, h))],
                 out_specs=[pl.BlockSpec((tq, G * HD), lambda h, i: (i, h)),
                            pl.BlockSpec((None, G, tq, HD), lambda h, i: (h, 0, i, 0))],
                 out_shape=[_sds((S, D)), _sds((Kh, G, S, HD))])(q, k, v)


def _attn_bwd(q, k, v, O, LSE, dOb, S, G, name, jobs=()):
    R, KVW = k.shape
    D = q.shape[1]
    Kh = KVW // HD
    tq = 128
    kc = R // _KEY_CHUNKS

    def body(q_ref, k_ref, v_ref, o_ref, lse_ref, do_ref, dq_ref, dk_ref, dv_ref):
        @pl.when(pl.program_id(1) == 0)
        def _():
            dk_ref[...] = jnp.zeros_like(dk_ref)
            dv_ref[...] = jnp.zeros_like(dv_ref)

        qs = _stack_heads(q_ref, G)
        do = _stack_heads(do_ref, G)
        o = _stack_heads(o_ref, G)
        lse = jnp.concatenate([lse_ref[g][:, 0:1] for g in range(G)], axis=0)
        delta = jnp.sum(do.astype(F32) * o, axis=-1, keepdims=True)
        dq = None
        for c in range(_KEY_CHUNKS):
            rows = slice(c * kc, (c + 1) * kc)
            kk = k_ref[rows, :]
            p = jnp.exp(_dot_nt(qs, kk) - lse)
            dp = _dot_nt(do, v_ref[rows, :])
            ds = (p * (dp - delta)).astype(_MXU)
            dqc = _dot(ds, kk)
            dq = dqc if c == 0 else dq + dqc
            dk_ref[rows, :] += _dot_tn(ds, qs)
            dv_ref[rows, :] += _dot_tn(p, do)
        for g in range(G):
            dq_ref[:, g * HD:(g + 1) * HD] = dq[g * tq:(g + 1) * tq, :]

    qspec = pl.BlockSpec((tq, G * HD), lambda h, i: (i, h))
    kspec = pl.BlockSpec((R, HD), lambda h, i: (0, h))
    return _call(body, jobs=jobs, name=name, grid=(Kh, S // tq),
                 in_specs=[qspec, kspec, kspec, qspec, pl.BlockSpec((None, G, tq, HD), lambda h, i: (h, 0, i, 0)), qspec],
                 out_specs=[qspec, kspec, kspec],
                 out_shape=[_sds((S, D)), _sds((R, KVW)), _sds((R, KVW))])(q, k, v, O, LSE, dOb)


def _halo_specs(R, CB, col0):
    nt8 = TR // 8
    return [pl.BlockSpec((8, CB), lambda h, i: (jnp.maximum(i * nt8 - 1, 0), col0 + h)),
            pl.BlockSpec((TR, CB), lambda h, i: (i, col0 + h)),
            pl.BlockSpec((8, CB), lambda h, i: (jnp.minimum((i + 1) * nt8, R // 8 - 1), col0 + h))]


def _seq_pos(i, S, R, CB):
    t = i * TR - 8 + lax.broadcasted_iota(jnp.int32, (TR + 16, CB), 0)
    start = jnp.where(t >= S, S, 0)
    end = jnp.where(t >= S, R, S)
    return t - start, end - t


def _shift(cat, by):
    return pltpu.roll(cat, by % cat.shape[0], 0)


def _gate_mats(xcb, w_ref, dirn, nb):
    return jnp.concatenate([_dot(xcb[:, b * HD:(b + 1) * HD], w_ref[dirn, b]) for b in range(nb)], axis=1)


def _lru_gates_fwd(P, conv_w, conv_b, WA, WX, ba, bx, lam, S, D, col0, name):
    R = P.shape[0]
    CB = D // 2
    nb = CB // HD

    def body(xp_ref, x_ref, xn_ref, cw_ref, cb_ref, wa_ref, wx_ref, ba_ref, bx_ref, lam_ref,
             xc_ref, af_ref, uf_ref, ab_ref, ub_ref):
        i = pl.program_id(1)
        cat = jnp.concatenate([xp_ref[...], x_ref[...], xn_ref[...]], axis=0)
        from_start, to_end = _seq_pos(i, S, R, CB)
        conv = (cb_ref[...] + cw_ref[2:3, :] * cat
                + cw_ref[0:1, :] * jnp.where(from_start >= 2, _shift(cat, 2), 0.0)
                + cw_ref[1:2, :] * jnp.where(from_start >= 1, _shift(cat, 1), 0.0)
                + cw_ref[3:4, :] * jnp.where(to_end >= 2, _shift(cat, -1), 0.0))
        xc = conv[8:8 + TR, :]
        xc_ref[...] = xc
        xcb = xc.astype(_MXU)
        for dirn, (a_ref, u_ref) in enumerate(((af_ref, uf_ref), (ab_ref, ub_ref))):
            ra = _sigmoid(_gate_mats(xcb, wa_ref, dirn, nb) + ba_ref[dirn:dirn + 1, :])
            ia = _sigmoid(_gate_mats(xcb, wx_ref, dirn, nb) + bx_ref[dirn:dirn + 1, :])
            nl = -lam_ref[dirn:dirn + 1, :]
            sp = jnp.maximum(nl, 0.0) + jnp.log(1.0 + jnp.exp(-jnp.abs(nl)))
            la = (-LRU_C) * ra * sp
            a_ref[...] = jnp.exp(la)
            u_ref[...] = jnp.sqrt(1.0 - jnp.exp(2.0 * la)) * (ia * xc)

    def par(r):
        return pl.BlockSpec((r, CB), lambda h, i: (0, h))

    wspec = pl.BlockSpec((2, nb, HD, HD), lambda h, i: (0, h, 0, 0))
    out = pl.BlockSpec((TR, CB), lambda h, i: (i, h))
    return _call(body, name=name, grid=(2, R // TR),
                 in_specs=_halo_specs(R, CB, col0) + [par(4), par(1), wspec, wspec, par(2), par(2), par(2)],
                 out_specs=[out] * 5, out_shape=[_sds((R, D))] * 5,
                 )(P, P, P, conv_w, conv_b, WA, WX, ba, bx, lam)


def _lru_gates_bwd(xc, hfp, hbp, gf, gb, WA, WX, ba, bx, lam, name):
    R, D = xc.shape
    CB = D // 2
    nb = CB // HD

    def body(xc_ref, hfp_ref, hbp_ref, gf_ref, gb_ref, wa_ref, wx_ref, ba_ref, bx_ref, lam_ref,
             dxc_ref, dwa_ref, dwx_ref, dba_ref, dbx_ref, dlam_ref):
        @pl.when(pl.program_id(1) == 0)
        def _():
            for r in (dwa_ref, dwx_ref, dba_ref, dbx_ref, dlam_ref):
                r[...] = jnp.zeros_like(r)

        xc = xc_ref[...]
        xcb = xc.astype(_MXU)
        dxc = jnp.zeros_like(xc)
        for dirn, (hp_ref, g_ref) in enumerate(((hfp_ref, gf_ref), (hbp_ref, gb_ref))):
            ra = _sigmoid(_gate_mats(xcb, wa_ref, dirn, nb) + ba_ref[dirn:dirn + 1, :])
            ia = _sigmoid(_gate_mats(xcb, wx_ref, dirn, nb) + bx_ref[dirn:dirn + 1, :])
            nl = -lam_ref[dirn:dirn + 1, :]
            sp = jnp.maximum(nl, 0.0) + jnp.log(1.0 + jnp.exp(-jnp.abs(nl)))
            la = (-LRU_C) * ra * sp
            a = jnp.exp(la)
            e2 = jnp.exp(2.0 * la)
            s = jnp.sqrt(1.0 - e2)
            du = g_ref[...]
            dla = du * hp_ref[...] * a - du * (ia * xc) * (e2 / s)
            dxc += du * s * ia
            dza = (dla * (-LRU_C) * sp) * ra * (1.0 - ra)
            dzx = (du * s * xc) * ia * (1.0 - ia)
            dlam_ref[dirn:dirn + 1, :] += jnp.sum(dla * (LRU_C * ra) * _sigmoid(nl), axis=0, keepdims=True)
            dba_ref[dirn:dirn + 1, :] += jnp.sum(dza, axis=0, keepdims=True)
            dbx_ref[dirn:dirn + 1, :] += jnp.sum(dzx, axis=0, keepdims=True)
            dzab, dzxb = dza.astype(_MXU), dzx.astype(_MXU)
            parts = []
            for b in range(nb):
                sl = slice(b * HD, (b + 1) * HD)
                dwa_ref[dirn, b] += _dot_tn(xcb[:, sl], dzab[:, sl])
                dwx_ref[dirn, b] += _dot_tn(xcb[:, sl], dzxb[:, sl])
                parts.append(_dot_nt(dzab[:, sl], wa_ref[dirn, b]) + _dot_nt(dzxb[:, sl], wx_ref[dirn, b]))
            dxc += jnp.concatenate(parts, axis=1)
        dxc_ref[...] = dxc

    def par(r):
        return pl.BlockSpec((r, CB), lambda h, i: (0, h))

    wspec = pl.BlockSpec((2, nb, HD, HD), lambda h, i: (0, h, 0, 0))
    tile = pl.BlockSpec((TR, CB), lambda h, i: (i, h))
    nbt = D // HD
    return _call(body, name=name, grid=(2, R // TR),
                 in_specs=[tile] * 5 + [wspec, wspec, par(2), par(2), par(2)],
                 out_specs=[tile, wspec, wspec, par(2), par(2), par(2)],
                 out_shape=[_sds((R, D)), _sds((2, nbt, HD, HD)), _sds((2, nbt, HD, HD)), _sds((2, D)), _sds((2, D)),
                            _sds((2, D))])(xc, hfp, hbp, gf, gb, WA, WX, ba, bx, lam)


def _scan_rows(n_groups, step, init):
    return lax.fori_loop(0, n_groups, lambda gi, c: step(pl.multiple_of(gi * 8, 8), c), init)


def _lru_scan_fwd(af, uf, ab, ub, S, name):
    R, D = af.shape
    W = min(SCAN_W, D)
    nm, nx = R // TR, S // TR
    nc = nm - nx
    ng = TR // 8

    def body(af_ref, uf_ref, ab_ref, ub_ref, hf_ref, hfp_ref, hb_ref, hbp_ref, cf_ref, cb_ref):
        @pl.when(pl.program_id(1) == 0)
        def _():
            cf_ref[...] = jnp.zeros_like(cf_ref)
            cb_ref[...] = jnp.zeros_like(cb_ref)

        def step(base, carry):
            hf, hb = carry
            baseb = pl.multiple_of(TR - 8 - base, 8)
            for r in range(8):
                tf, tb = base + r, baseb + 7 - r
                hfp_ref[pl.ds(tf, 1), :] = hf
                hf = af_ref[pl.ds(tf, 1), :] * hf + uf_ref[pl.ds(tf, 1), :]
                hf_ref[pl.ds(tf, 1), :] = hf
                hbp_ref[pl.ds(tb, 1), :] = hb
                hb = ab_ref[pl.ds(tb, 1), :] * hb + ub_ref[pl.ds(tb, 1), :]
                hb_ref[pl.ds(tb, 1), :] = hb
            return hf, hb

        hf, hb = _scan_rows(ng, step, (cf_ref[0:1, :], cb_ref[0:1, :]))
        cf_ref[0:1, :] = hf
        cb_ref[0:1, :] = hb

    fmap = lambda j, s: (jnp.where(s < nc, nx + s, s - nc), j)
    bmap = lambda j, s: (nm - 1 - s, j)
    fs, bs = pl.BlockSpec((TR, W), fmap), pl.BlockSpec((TR, W), bmap)
    return _call(body, name=name, grid=(D // W, nm), in_specs=[fs, fs, bs, bs], out_specs=[fs, fs, bs, bs],
                 out_shape=[_sds((R, D))] * 4, scratch=[pltpu.VMEM((8, W), F32), pltpu.VMEM((8, W), F32)])(af, uf, ab, ub)


def _lru_scan_bwd(af, ab, dhs, S, name):
    R, D = af.shape
    W = min(SCAN_W, D)
    nm, nx = R // TR, S // TR
    ng = TR // 8

    def body(af_ref, dhf_ref, ab_ref, dhb_ref, gf_ref, gb_ref, cf_ref, cb_ref):
        @pl.when(pl.program_id(1) == 0)
        def _():
            cf_ref[...] = jnp.zeros_like(cf_ref)
            cb_ref[...] = jnp.zeros_like(cb_ref)

        def step(base, carry):
            cf, cb = carry
            based = pl.multiple_of(TR - 8 - base, 8)
            for r in range(8):
                tf, tb = based + 7 - r, base + r
                g = dhf_ref[pl.ds(tf, 1), :] + cf
                gf_ref[pl.ds(tf, 1), :] = g
                cf = af_ref[pl.ds(tf, 1), :] * g
                g = dhb_ref[pl.ds(tb, 1), :] + cb
                gb_ref[pl.ds(tb, 1), :] = g
                cb = ab_ref[pl.ds(tb, 1), :] * g
            return cf, cb

        cf, cb = _scan_rows(ng, step, (cf_ref[0:1, :], cb_ref[0:1, :]))
        cf_ref[0:1, :] = cf
        cb_ref[0:1, :] = cb

    fmap = lambda j, s: (jnp.where(s < nx, nx - 1 - s, nm - 1 - (s - nx)), j)
    bmap = lambda j, s: (s, j)
    fs, bs = pl.BlockSpec((TR, W), fmap), pl.BlockSpec((TR, W), bmap)
    return _call(body, name=name, grid=(D // W, nm), in_specs=[fs, fs, bs, bs], out_specs=[fs, bs],
                 out_shape=[_sds((R, D))] * 2, scratch=[pltpu.VMEM((8, W), F32), pltpu.VMEM((8, W), F32)])(af, dhs, ab, dhs)


def _merge_fwd(P, hf, hb, O, S, D, col_lg, name):
    R = P.shape[0]
    CB = D // 2
    nx = S // TR

    def body(lg_ref, ga_ref, gl_ref, hf_ref, hb_ref, o_ref, mix_ref):
        ge, _ = _gelu_and_grad(lg_ref[...])
        lru = (hf_ref[...] + hb_ref[...]) * ge
        mix_ref[...] = (_sigmoid(ga_ref[...]) * o_ref[...] + _sigmoid(gl_ref[...]) * lru).astype(mix_ref.dtype)

    def col(c0):
        return pl.BlockSpec((TR, CB), lambda h, i: (i, c0 + h))

    return _call(body, name=name, grid=(2, R // TR),
                 in_specs=[col(col_lg), col(col_lg + 2), col(col_lg + 4), col(0), col(0),
                           pl.BlockSpec((TR, CB), lambda h, i: (jnp.minimum(i, nx - 1), h))],
                 out_specs=col(0), out_shape=_sds((R, D), _MXU))(P, P, P, hf, hb, O)


def _merge_bwd(dmix, P, hf, hb, O, S, D, col_lg, name, jobs=()):
    R = P.shape[0]
    CB = D // 2
    nx = S // TR

    def body(dm_ref, lg_ref, ga_ref, gl_ref, hf_ref, hb_ref, o_ref, do_ref, dhs_ref, dp_ref, stash):
        i, sec = pl.program_id(1), pl.program_id(2)

        @pl.when(sec == 0)
        def _():
            dm = jnp.where(i < nx, dm_ref[...], 0.0)
            sa, sl = _sigmoid(ga_ref[...]), _sigmoid(gl_ref[...])
            ge, dge = _gelu_and_grad(lg_ref[...])
            hs = hf_ref[...] + hb_ref[...]
            o = o_ref[...]
            dl = dm * sl
            do_ref[...] = (dm * sa).astype(do_ref.dtype)
            dhs_ref[...] = dl * ge
            stash[0] = (dl * hs * dge).astype(stash.dtype)
            stash[1] = (dm * o * sa * (1.0 - sa)).astype(stash.dtype)
            stash[2] = (dm * (hs * ge) * sl * (1.0 - sl)).astype(stash.dtype)

        dp_ref[...] = stash[sec]

    def col(c0):
        return pl.BlockSpec((TR, CB), lambda h, i, s: (i, c0 + h))

    xrow = pl.BlockSpec((TR, CB), lambda h, i, s: (jnp.minimum(i, nx - 1), h))
    return _call(body, jobs=jobs, name=name, grid=(2, R // TR, 3),
                 in_specs=[xrow, col(col_lg), col(col_lg + 2), col(col_lg + 4), col(0), col(0), xrow],
                 out_specs=[col(0), col(0), pl.BlockSpec((TR, CB), lambda h, i, s: (i, col_lg + 2 * s + h))],
                 out_shape=[_sds((R, D), _MXU), _sds((R, D)), _sds(P.shape, _MXU)],
                 scratch=[pltpu.VMEM((3, TR, CB), _MXU)])(dmix, P, P, P, hf, hb, O)


def _conv_bwd(dxc, P, conv_w, dP, S, D, col0, name):
    R = P.shape[0]
    CB = D // 2

    def body(dp_, d_ref, dn_, xp_ref, x_ref, xn_ref, cw_ref, dp_in, dpo_ref, dcw_ref, dcb_ref):
        i = pl.program_id(1)

        @pl.when(i == 0)
        def _():
            dcw_ref[...] = jnp.zeros_like(dcw_ref)
            dcb_ref[...] = jnp.zeros_like(dcb_ref)

        d = d_ref[...]
        catd = jnp.concatenate([dp_[...], d, dn_[...]], axis=0)
        catx = jnp.concatenate([xp_ref[...], x_ref[...], xn_ref[...]], axis=0)
        from_start, to_end = _seq_pos(i, S, R, CB)
        dxl = (cw_ref[2:3, :] * catd
               + cw_ref[0:1, :] * jnp.where(to_end >= 3, _shift(catd, -2), 0.0)
               + cw_ref[1:2, :] * jnp.where(to_end >= 2, _shift(catd, -1), 0.0)
               + cw_ref[3:4, :] * jnp.where(from_start >= 1, _shift(catd, 1), 0.0))
        dpo_ref[...] = dxl[8:8 + TR, :].astype(dpo_ref.dtype)
        taps = (jnp.where(from_start >= 2, _shift(catx, 2), 0.0), jnp.where(from_start >= 1, _shift(catx, 1), 0.0),
                catx, jnp.where(to_end >= 2, _shift(catx, -1), 0.0))
        for kk in range(4):
            dcw_ref[kk:kk + 1, :] += jnp.sum(d * taps[kk][8:8 + TR, :], axis=0, keepdims=True)
        dcb_ref[...] += jnp.sum(d, axis=0, keepdims=True)

    return _call(body, name=name, grid=(2, R // TR),
                 in_specs=_halo_specs(R, CB, 0) + _halo_specs(R, CB, col0)
                 + [pl.BlockSpec((4, CB), lambda h, i: (0, h)), pl.BlockSpec(memory_space=pl.ANY)],
                 out_specs=[pl.BlockSpec((TR, CB), lambda h, i: (i, col0 + h)), pl.BlockSpec((4, CB), lambda h, i: (0, h)),
                            pl.BlockSpec((1, CB), lambda h, i: (0, h))],
                 out_shape=[_sds(dP.shape, dP.dtype), _sds((4, D)), _sds((1, D))],
                 aliases={7: 0})(dxc, dxc, dxc, P, P, P, conv_w, dP)


def _rope_tables(S, C):
    t = jnp.arange(S, dtype=jnp.int32)
    row = (t // GRID_W).astype(F32)
    col = (t % GRID_W).astype(F32)
    axis_dims = HD // 2
    freqs = ROPE_THETA ** (-jnp.arange(0, axis_dims, 2, dtype=F32) / axis_dims)
    ang = jnp.concatenate([row[:, None] * freqs, col[:, None] * freqs], axis=-1)
    cos = jnp.repeat(jnp.cos(ang), 2, axis=-1)
    sin = jnp.repeat(jnp.sin(ang), 2, axis=-1) * jnp.tile(jnp.array([-1.0, 1.0], F32), HD // 2)
    return (jnp.concatenate([cos, jnp.ones((C, HD), F32)], axis=0),
            jnp.concatenate([sin, jnp.zeros((C, HD), F32)], axis=0))


def _local_step(x, ctx, target, modx, modc, ng, shards, qg, kg, conv_w, conv_b, WA, WX, ba, bx, lam, fg, idx, opt):
    S, D = x.shape
    C = ctx.shape[0]
    R = S + C
    nx = S // TR
    Nb = shards['w_in'].shape[-1]
    Fb = shards['wd0'].shape[1]
    KVW = (ND * Nb - 5 * D) // 2
    G = D // KVW
    CB = D // 2
    col_lx = (D + 2 * KVW) // CB
    assert S % TR == 0 and C % TR == 0 and (D + 2 * KVW) % CB == 0 and CB % HD == 0
    mod2 = jnp.stack([modx, modc])
    X0 = jnp.concatenate([x, ctx], axis=0)
    COS, SIN = _rope_tables(S, C)
    ng0, ng1, ng2 = ng[0:1], ng[1:2], ng[2:3]
    ag = lambda n: _ag_job(shards[n])
    sib = lambda Gp: _rs_sibling_job(Gp.reshape(ND, -1, Gp.shape[-1]))
    add = lambda Gp, bufA, tag: _rs_add(Gp.reshape(ND, -1, Gp.shape[-1]), bufA, idx, f"{tag}_rs_add")
    out = {}

    def tn_specs(rows):
        tm = _tm(rows)
        return pl.BlockSpec((None, tm, Fb), lambda d, r: (d, r, 0)), pl.BlockSpec((tm, D), lambda d, r: (r, 0))

    wd_spec = pl.BlockSpec((None, None, Fb, D), lambda d, r: (d, 0, 0, 0))
    wg_spec = pl.BlockSpec((None, None, D, Fb), lambda d, r: (d, 0, 0, 0))

    ((WGU0,),) = _comm_call([ag('wgu0')], "ag_wgu0")
    H1 = _norm_mod_fwd(X0, ng0, mod2, 0, R, nx, "ffn1_norm")
    (G1, U1, A1), ((WD0,),) = _ffn_up(H1, WGU0, 0, R, "ffn1_up", jobs=[ag('wd0')])
    (Y1, X1), ((WIN,),) = _ffn_down(A1, WD0, 0, X0, mod2, 0, R, S, "ffn1_down", jobs=[ag('w_in')])
    H2 = _norm_mod_fwd(X1, ng1, mod2, 1, R, nx, "mix_norm")
    P = _proj_in(H2, WIN, "proj_in")
    q, k, v = _qkv_prep(P, qg, kg, COS, SIN, D, KVW, "qkv_prep")
    (O, LSE), ((WOUT,), (WGU1,), (WD1,)) = _attn_fwd(q, k, v, S, G, "attn_fwd", jobs=[ag('w_out'), ag('wgu1'), ag('wd1')])
    WOUT = WOUT.reshape(D, D)
    xc, af, uf, ab, ub = _lru_gates_fwd(P, conv_w, conv_b, WA, WX, ba, bx, lam, S, D, col_lx, "lru_gates")
    hf, hfp, hb, hbp = _lru_scan_fwd(af, uf, ab, ub, S, "lru_scan")
    mixb = _merge_fwd(P, hf, hb, O, S, D, col_lx + 2, "merge")
    Z, X2 = _proj_out(mixb, WOUT, X1, mod2, S, "proj_out")
    H3 = _norm_mod_fwd(X2, ng2, mod2, 2, S, nx, "ffn2_norm")
    G3, U3, A3 = _ffn_up(H3, WGU1, 0, S, "ffn2_up")
    Y3, X3 = _ffn_down(A3, WD1, 0, X2, mod2, 2, S, S, "ffn2_down")
    loss, dX3, dfg = _final_loss(X3, fg, target, "loss_head")

    dY3b, dg3 = _res_bwd(dX3, Y3, mod2, 2, 0.5, S, nx, "ffn2_dres")
    dG3, dU3 = _ffn_dact(dY3b, WD1, 0, G3, U3, S, "ffn2_dact")
    blk, row = tn_specs(S)
    dWD1 = _mm_tn(A3, blk, dY3b, row, (ND, 1, Fb, D), wd_spec, S, "ffn2_dwd")
    dWG1 = _mm_tn(H3, row, dG3, blk, (ND, 1, D, Fb), wg_spec, S, "ffn2_dwg")
    dWU1 = _mm_tn(H3, row, dU3, blk, (ND, 1, D, Fb), wg_spec, S, "ffn2_dwu")
    dH3, ((a_wd1,), (a_wg1,), (a_wu1,)) = _ffn_dh(dG3, dU3, WGU1, 0, S, "ffn2_dh", jobs=[sib(dWD1), sib(dWG1), sib(dWU1)])
    T_wd1, own_wd1 = add(dWD1, a_wd1, "wd1")
    T_wg1, own_wg1 = add(dWG1, a_wg1, "wg1")
    T_wu1, own_wu1 = add(dWU1, a_wu1, "wu1")
    dX2, dsh3, dsc3, dng2 = _norm_mod_bwd(X2, dH3, dX3, ng2, mod2, 2, S, nx, nx, "ffn2_dnorm")

    dZb, dg2 = _res_bwd(dX2, Z, mod2, 1, 1.0, S, nx, "mix_dres")
    dmix = _dproj_out(dZb, WOUT, "dproj_out")
    dWOUT = _mm_tn(mixb, pl.BlockSpec((_tm(S), D // ND), lambda d, r: (r, d)), dZb, pl.BlockSpec((_tm(S), D), lambda d, r: (r, 0)),
                   (ND, D // ND, D), pl.BlockSpec((None, D // ND, D), lambda d, r: (d, 0, 0)), S, "dw_out")
    (dOb, dhs, dP), ((a_wout,),) = _merge_bwd(dmix, P, hf, hb, O, S, D, col_lx + 2, "merge_bwd", jobs=[sib(dWOUT)])
    T_wout, own_wout = add(dWOUT, a_wout, "w_out")
    gf, gb = _lru_scan_bwd(af, ab, dhs, S, "lru_scan_bwd")
    dxc, dWA, dWX, dba, dbx, dlam = _lru_gates_bwd(xc, hfp, hbp, gf, gb, WA, WX, ba, bx, lam, "lru_gates_bwd")
    dP, dconv_w, dconv_b = _conv_bwd(dxc, P, conv_w, dP, S, D, col_lx, "conv_bwd")
    (dq, dk, dv), ((b_wd1,), (b_wg1,), (b_wu1,), (b_wout,)) = _attn_bwd(
        q, k, v, O, LSE, dOb, S, G, "attn_bwd",
        jobs=[_rs_chips_job(T_wd1), _rs_chips_job(T_wg1), _rs_chips_job(T_wu1), _rs_chips_job(T_wout)])
    fin_wd = _rs_finish(own_wd1, b_wd1, "wd1_rs_finish", opt['ffn_wd'], 1, 2)
    fin_wg = _rs_finish(own_wg1, b_wg1, "wg1_rs_finish", opt['ffn_wg'], 1, 2)
    fin_wu = _rs_finish(own_wu1, b_wu1, "wu1_rs_finish", opt['ffn_wu'], 1, 2)
    out['w_out'] = _rs_finish(own_wout, b_wout, "w_out_rs_finish", opt['w_out'])
    dP, dqg, dkg = _qkv_bwd(P, dq, dk, dv, qg, kg, COS, SIN, dP, D, KVW, nx, "qkv_bwd")
    dH2 = _dproj_in(dP, WIN, "dproj_in")
    dWIN = _mm_tn(H2, pl.BlockSpec((_tm(R), D), lambda d, r: (r, 0)), dP, pl.BlockSpec((_tm(R), Nb), lambda d, r: (r, d)),
                  (ND, D, Nb), pl.BlockSpec((None, D, Nb), lambda d, r: (d, 0, 0)), R, "dw_in")
    dX1, dsh2, dsc2, dng1 = _norm_mod_bwd(X1, dH2, dX2, ng1, mod2, 1, R, nx, nx, "mix_dnorm")

    dY1b, dg1 = _res_bwd(dX1, Y1, mod2, 0, 0.5, R, nx, "ffn1_dres")
    (dG1, dU1), ((a_win,),) = _ffn_dact(dY1b, WD0, 0, G1, U1, R, "ffn1_dact", jobs=[sib(dWIN)])
    T_win, own_win = add(dWIN, a_win, "w_in")
    dH1, ((b_win,),) = _ffn_dh(dG1, dU1, WGU0, 0, R, "ffn1_dh", jobs=[_rs_chips_job(T_win)])
    out['w_in'] = _rs_finish(own_win, b_win, "w_in_rs_finish", opt['w_in'])
    blk, row = tn_specs(R)
    dWD0 = _mm_tn(A1, blk, dY1b, row, (ND, 1, Fb, D), wd_spec, R, "ffn1_dwd")
    dWG0, ((a_wd0,),) = _mm_tn(H1, row, dG1, blk, (ND, 1, D, Fb), wg_spec, R, "ffn1_dwg", jobs=[sib(dWD0)])
    T_wd0, own_wd0 = add(dWD0, a_wd0, "wd0")
    dWU0, ((a_wg0,), (b_wd0,)) = _mm_tn(H1, row, dU1, blk, (ND, 1, D, Fb), wg_spec, R, "ffn1_dwu",
                                        jobs=[sib(dWG0), _rs_chips_job(T_wd0)])
    T_wg0, own_wg0 = add(dWG0, a_wg0, "wg0")
    out['ffn_wd'] = _rs_finish(own_wd0, b_wd0, "wd0_rs_finish", opt['ffn_wd'], 0, 2, fin_wd)
    (dX0, dsh1, dsc1, dng0), ((a_wu0,), (b_wg0,)) = _norm_mod_bwd(
        X0, dH1, dX1, ng0, mod2, 0, R, nx, R // TR, "ffn1_dnorm", jobs=[sib(dWU0), _rs_chips_job(T_wg0)])
    T_wu0, own_wu0 = add(dWU0, a_wu0, "wu0")
    out['ffn_wg'] = _rs_finish(own_wg0, b_wg0, "wg0_rs_finish", opt['ffn_wg'], 0, 2, fin_wg)
    ((b_wu0,),) = _comm_call([_rs_chips_job(T_wu0)], "wu0_rs_chips")
    out['ffn_wu'] = _rs_finish(own_wu0, b_wu0, "wu0_rs_finish", opt['ffn_wu'], 0, 2, fin_wu)

    zero = jnp.zeros((1, D), F32)
    dmodx = jnp.concatenate([dsh1[0], dsc1[0], dg1[0], dsh2[0], dsc2[0], dg2[0], dsh3[0], dsc3[0], dg3[0]], axis=0)
    dmodc = jnp.concatenate([dsh1[1], dsc1[1], dg1[1], dsh2[1], dsc2[1], zero, zero, zero, zero], axis=0)
    out.update(loss=loss, grad_x=dX0[:S], dmodx=dmodx, dmodc=dmodc, norm_g=jnp.concatenate([dng0, dng1, dng2], axis=0),
               q_norm_g=dqg, k_norm_g=dkg, conv_w=dconv_w, conv_b=dconv_b, lru_wa=dWA, lru_ba=dba, lru_wx=dWX, lru_bx=dbx,
               lru_lambda=dlam, final_norm_g=dfg)
    return out


def _mesh_pos():
    return lax.axis_index("x"), lax.axis_index("y"), lax.axis_index("c")


def _all_gather(xb, name, in_vmem=False):
    space = pltpu.VMEM if in_vmem else pl.ANY

    def body(x_ref, out_ref, send_sems, recv_sems, local_sem):
        x, y, c = _mesh_pos()
        me, sibling = (x, y, c), (x, y, 1 - c)
        chips = [(1 - x, y), (x, 1 - y), (1 - x, 1 - y)]

        def slot(px, py, pc):
            return out_ref.at[4 * px + 2 * py + pc]

        def copy(k, block, to, src=None):
            return pltpu.make_async_remote_copy(
                src_ref=slot(*block) if src is None else src, dst_ref=slot(*block),
                send_sem=send_sems.at[k], recv_sem=recv_sems.at[k], device_id=to, device_id_type=MESH)

        mine = pltpu.make_async_copy(x_ref, slot(*me), local_sem)
        mine.start()
        first = [copy(0, me, sibling, src=x_ref)]
        first += [copy(1 + j, me, (*chip, c), src=x_ref) for j, chip in enumerate(chips)]
        for cp in first:
            cp.start()
        passed = [copy(4 + j, (*chip, c), sibling) for j, chip in enumerate(chips)]
        for j, chip in enumerate(chips):
            copy(1 + j, (*chip, c), me).wait_recv()
            passed[j].start()
        copy(0, sibling, me).wait_recv()
        for j, chip in enumerate(chips):
            copy(4 + j, (*chip, 1 - c), me).wait_recv()
        for cp in first + passed:
            cp.wait_send()
        mine.wait()

    return pl.pallas_call(
        body, name=name, out_shape=_sds((ND,) + xb.shape, xb.dtype),
        in_specs=[pl.BlockSpec(memory_space=space)], out_specs=pl.BlockSpec(memory_space=space),
        scratch_shapes=[pltpu.SemaphoreType.DMA((7,)), pltpu.SemaphoreType.DMA((7,)), pltpu.SemaphoreType.DMA(())])(xb)


def _ag_job(xb):
    def parts(ins, outs, sems, starting=False):
        x_ref, out_ref = ins[0], outs[0]
        send_sems, recv_sems, local_sem = sems
        x, y, c = _mesh_pos()
        me, sibling = (x, y, c), (x, y, 1 - c)
        chips = [(1 - x, y), (x, 1 - y), (1 - x, 1 - y)]

        def slot(px, py, pc):
            return out_ref.at[4 * px + 2 * py + pc]

        def copy(k, block, to, src=None):
            return pltpu.make_async_remote_copy(
                src_ref=slot(*block) if src is None else src, dst_ref=slot(*block),
                send_sem=send_sems.at[k], recv_sem=recv_sems.at[k], device_id=to, device_id_type=MESH)

        mine = pltpu.make_async_copy(x_ref, slot(*me), local_sem)
        first = [copy(0, me, sibling, src=x_ref)] + [copy(1 + j, me, (*chip, c), src=x_ref) for j, chip in enumerate(chips)]
        if starting:
            return mine, first
        passed = [copy(4 + j, (*chip, c), sibling) for j, chip in enumerate(chips)]
        landed = [copy(1 + j, (*chip, c), me) for j, chip in enumerate(chips)]
        from_sibling = [copy(0, sibling, me)] + [copy(4 + j, (*chip, 1 - c), me) for j, chip in enumerate(chips)]
        return mine, first, passed, landed, from_sibling

    def start(ins, outs, sems):
        mine, first = parts(ins, outs, sems, starting=True)
        mine.start()
        for cp in first:
            cp.start()

    def finish(ins, outs, sems):
        mine, first, passed, landed, from_sibling = parts(ins, outs, sems)
        for j in range(3):
            landed[j].wait_recv()
            passed[j].start()
        for cp in from_sibling:
            cp.wait_recv()
        for cp in first + passed:
            cp.wait_send()
        mine.wait()

    return _Job([xb], [_sds((ND,) + xb.shape, xb.dtype)],
                [pltpu.SemaphoreType.DMA((7,)), pltpu.SemaphoreType.DMA((7,)), pltpu.SemaphoreType.DMA(())], start, finish)


def _rs_sibling_job(Gp):
    def copies(ins, outs, sems):
        x, y, c = _mesh_pos()
        return [pltpu.make_async_remote_copy(
            src_ref=ins[0].at[2 * k + (1 - c)], dst_ref=outs[0].at[k], send_sem=sems[0].at[k], recv_sem=sems[1].at[k],
            device_id=(x, y, 1 - c), device_id_type=MESH) for k in range(4)]

    def start(ins, outs, sems):
        for cp in copies(ins, outs, sems):
            cp.start()

    def finish(ins, outs, sems):
        cps = copies(ins, outs, sems)
        for cp in cps:
            cp.wait_recv()
        for cp in cps:
            cp.wait_send()

    return _Job([Gp], [_sds((4,) + Gp.shape[1:], Gp.dtype)],
                [pltpu.SemaphoreType.DMA((4,)), pltpu.SemaphoreType.DMA((4,))], start, finish)


def _rs_chips_job(T):
    def copies(ins, outs, sems):
        x, y, c = _mesh_pos()
        chips = [(1 - x, y), (x, 1 - y), (1 - x, 1 - y)]
        return [pltpu.make_async_remote_copy(
            src_ref=ins[0].at[2 * px + py], dst_ref=outs[0].at[j], send_sem=sems[0].at[j], recv_sem=sems[1].at[j],
            device_id=(px, py, c), device_id_type=MESH) for j, (px, py) in enumerate(chips)]

    def start(ins, outs, sems):
        for cp in copies(ins, outs, sems):
            cp.start()

    def finish(ins, outs, sems):
        cps = copies(ins, outs, sems)
        for cp in cps:
            cp.wait_recv()
        for cp in cps:
            cp.wait_send()

    return _Job([T], [_sds((3,) + T.shape[1:], T.dtype)],
                [pltpu.SemaphoreType.DMA((3,)), pltpu.SemaphoreType.DMA((3,))], start, finish)


def _tile_rows(rows, cols):
    best = None
    for t in range(16, rows + 1, 16):
        if rows % t == 0 and t * cols * 4 <= (1 << 20):
            best = t
    return best if best is not None else rows


def _prefetch_call(body, *, name, grid, in_specs, out_specs, out_shape):
    return pl.pallas_call(
        body, name=name, out_shape=out_shape,
        grid_spec=pltpu.PrefetchScalarGridSpec(num_scalar_prefetch=1, grid=grid, in_specs=in_specs, out_specs=out_specs),
        compiler_params=pltpu.CompilerParams(dimension_semantics=("arbitrary",) * len(grid), vmem_limit_bytes=VMEM_LIMIT))


def _rs_add(Gp, bufA, idx, name):
    rows, cols = Gp.shape[1:]
    tr = _tile_rows(rows, cols)

    def body(i_ref, g_ref, a_ref, t_ref, own_ref):
        t = g_ref[...] + a_ref[...]
        t_ref[...] = t.astype(t_ref.dtype)

        @pl.when(pl.program_id(1) == i_ref[1])
        def _():
            own_ref[...] = t

    return _prefetch_call(
        body, name=name, grid=(rows // tr, 4),
        in_specs=[pl.BlockSpec((None, tr, cols), lambda r, k, i_ref: (2 * k + i_ref[0], r, 0)),
                  pl.BlockSpec((None, tr, cols), lambda r, k, i_ref: (k, r, 0))],
        out_specs=[pl.BlockSpec((None, tr, cols), lambda r, k, i_ref: (k, r, 0)),
                   pl.BlockSpec((tr, cols), lambda r, k, i_ref: (r, 0))],
        out_shape=[_sds((4, rows, cols), jnp.bfloat16), _sds((rows, cols))])(idx, Gp, bufA)


def _adam(w, g, m, v):
    m = ADAM_B1 * m + (1.0 - ADAM_B1) * g
    v = ADAM_B2 * v + (1.0 - ADAM_B2) * (g * g)
    m_hat = m / (1.0 - ADAM_B1 ** ADAM_STEP)
    v_hat = v / (1.0 - ADAM_B2 ** ADAM_STEP)
    return -ADAM_LR * (m_hat / (jnp.sqrt(v_hat) + ADAM_EPS) + ADAM_WD * w), m, v


def _rs_finish(Town, bufB, name, wmv=None, slab=0, n_slabs=1, prev=None):
    rows, cols = Town.shape
    tr = _tile_rows(rows, cols)
    nr = rows // tr
    n_in = 4 + (3 if wmv is not None else 0)
    n_out = 4 if wmv is not None else 1

    def body(*refs):
        ins, outs = refs[:n_in], refs[len(refs) - n_out:]
        g = ((ins[0][...] + ins[1][...].astype(F32)) + ins[2][...].astype(F32)) + ins[3][...].astype(F32)
        outs[0][...] = g
        if wmv is not None:
            d, m, v = _adam(ins[4][...], g, ins[5][...], ins[6][...])
            outs[1][...] = d
            outs[2][...] = m
            outs[3][...] = v

    plain = pl.BlockSpec((tr, cols), lambda r: (r, 0))
    slabbed = pl.BlockSpec((tr, cols), lambda r: (slab * nr + r, 0))
    in_specs = [plain] + [pl.BlockSpec((None, tr, cols), (lambda j: lambda r: (j, r, 0))(j)) for j in range(3)]
    args = [Town, bufB, bufB, bufB]
    if wmv is not None:
        in_specs += [slabbed] * 3
        args += list(wmv)
    aliases = None
    if prev is not None:
        in_specs += [pl.BlockSpec(memory_space=pl.ANY)] * n_out
        aliases = {len(args) + i: i for i in range(n_out)}
        args += list(prev)
    return _call(body, name=name, grid=(nr,), in_specs=in_specs, out_specs=[slabbed] * n_out,
                 out_shape=[_sds((n_slabs * rows, cols))] * n_out, aliases=aliases)(*args)


def _reduce_scatter(Gp, idx, tag, wmv=None):
    ((bufA,),) = _comm_call([_rs_sibling_job(Gp)], f"{tag}_rs_sibling")
    T, Town = _rs_add(Gp, bufA, idx, f"{tag}_rs_add")
    ((bufB,),) = _comm_call([_rs_chips_job(T)], f"{tag}_rs_chips")
    return _rs_finish(Town, bufB, f"{tag}_rs_finish", wmv)


def _adamw_plain(w, g, m, v, name):
    rows, cols = w.shape
    tr = _tile_rows(rows, cols)

    def body(w_ref, g_ref, m_ref, v_ref, d_ref, mo_ref, vo_ref):
        d, m_, v_ = _adam(w_ref[...], g_ref[...], m_ref[...], v_ref[...])
        d_ref[...] = d
        mo_ref[...] = m_
        vo_ref[...] = v_

    spec = pl.BlockSpec((tr, cols), lambda r: (r, 0))
    return _call(body, name=name, grid=(rows // tr,), in_specs=[spec] * 4, out_specs=[spec] * 3,
                 out_shape=[_sds((rows, cols))] * 3)(w, g, m, v)


_MOD_TK = 512


def _mod_fwd(cc16, w_loc, b_loc, name):
    D, cols = w_loc.shape
    tk = min(_MOD_TK, D)
    nk = D // tk

    def body(c_ref, w_ref, b_ref, o_ref):
        kk = pl.program_id(0)

        @pl.when(kk == 0)
        def _():
            o_ref[...] = jnp.zeros_like(o_ref)

        cc = c_ref[...]
        o_ref[...] += _dot(cc * _sigmoid(cc), w_ref[...])

        @pl.when(kk == nk - 1)
        def _():
            o_ref[...] += b_ref[...]

    return _call(body, name=name, grid=(nk,),
                 in_specs=[pl.BlockSpec((16, tk), lambda kk: (0, kk)), pl.BlockSpec((tk, cols), lambda kk: (kk, 0)),
                           _full_spec((1, cols))],
                 out_specs=_full_spec((16, cols)), out_shape=_sds((16, cols)))(cc16, w_loc, b_loc)


def _mod_bwd(dm_loc, cc16, w_loc, name):
    D, cols = w_loc.shape

    def body(dm_ref, c_ref, w_ref, gw_ref, ds_ref):
        rows = [dm_ref[b, 0:1, :] for b in range(ND)]
        ctx = dm_ref[0, 1:2, :]
        for b in range(1, ND):
            ctx = ctx + dm_ref[b, 1:2, :]
        dm16 = jnp.concatenate(rows + [ctx, jnp.zeros((7, cols), F32)], axis=0)
        cc = c_ref[...]
        gw_ref[...] = _dot_tn(cc * _sigmoid(cc), dm16)
        ds_ref[...] = _dot_nt(dm16, w_ref[...])

    tk = min(_MOD_TK, D)
    return _call(body, name=name, grid=(D // tk,),
                 in_specs=[_full_spec((ND, 8, cols)), pl.BlockSpec((16, tk), lambda kk: (0, kk)),
                           pl.BlockSpec((tk, cols), lambda kk: (kk, 0))],
                 out_specs=[pl.BlockSpec((tk, cols), lambda kk: (kk, 0)), pl.BlockSpec((16, tk), lambda kk: (0, kk))],
                 out_shape=[_sds((D, cols)), _sds((16, D))])(dm_loc, cc16, w_loc)


def _bmod_grad(dm_all, name):
    n = dm_all.shape[-1]

    def body(dm_ref, o_ref):
        acc = dm_ref[0, 0:1, :] + dm_ref[0, 1:2, :]
        for b in range(1, ND):
            acc = (acc + dm_ref[b, 0:1, :]) + dm_ref[b, 1:2, :]
        o_ref[...] = acc

    return _call(body, name=name, grid=(1,), in_specs=[_full_spec((ND, 8, n))], out_specs=_full_spec((1, n)),
                 out_shape=_sds((1, n)))(dm_all)


_SMALL_ROWS = 24
_ROW_CCTX = 15


def _small_finish(parts, c_ctx, name):
    D = parts.shape[-1]

    def body(p_ref, c_ref, o_ref):
        acc = p_ref[0]
        for b in range(1, ND):
            acc = acc + p_ref[b]
        cc = c_ref[...]
        sg = _sigmoid(cc)
        dsilu = sg * (1.0 + cc * (1.0 - sg))
        row = lax.broadcasted_iota(jnp.int32, acc.shape, 0)
        o_ref[...] = jnp.where(row == _ROW_CCTX, acc * dsilu, acc)

    return _call(body, name=name, grid=(1,), in_specs=[_full_spec(parts.shape), _full_spec((1, D))],
                 out_specs=_full_spec((_SMALL_ROWS, D)), out_shape=_sds((_SMALL_ROWS, D)))(parts, c_ctx)


_WEIGHTS = ['c_ctx', 'w_mod', 'b_mod', 'norm_g', 'ffn_wg', 'ffn_wu', 'ffn_wd', 'w_in', 'w_out', 'q_norm_g', 'k_norm_g',
            'conv_w', 'conv_b', 'lru_wa', 'lru_ba', 'lru_wx', 'lru_bx', 'lru_lambda', 'final_norm_g']
_SMALL = ['c_ctx', 'b_mod', 'norm_g', 'q_norm_g', 'k_norm_g', 'conv_w', 'conv_b', 'lru_ba', 'lru_bx', 'lru_lambda',
          'final_norm_g']


def _pad_rows(a, rows):
    return jnp.pad(a, ((0, rows - a.shape[0]),) + ((0, 0),) * (a.ndim - 1))


def _step(w, m, v, x, c, ctx, loss_target):
    xi, yi, ci = _mesh_pos()
    me = 4 * xi + 2 * yi + ci
    idx = jnp.stack([ci, 2 * xi + yi]).astype(jnp.int32)
    S, D = x.shape[1:]
    Ds = D // ND
    cols = w['w_mod'].shape[-1]

    sp = jnp.concatenate([w['norm_g'][0], w['conv_w'][0], w['lru_ba'][0], w['lru_bx'][0], w['lru_lambda'][0]], axis=0)
    spg = _all_gather(_pad_rows(sp, 16), "ag_small_params", in_vmem=True)
    spf = jnp.transpose(spg, (1, 0, 2)).reshape(16, D)
    ng, conv_w, ba, bx, lam = spf[0:3], spf[3:7], spf[7:9], spf[9:11], spf[11:13]

    cg = _all_gather(_pad_rows(c, 8), "ag_cond", in_vmem=True)
    cc16 = _pad_rows(jnp.concatenate([cg[:, 0, :], w['c_ctx'][None, :]], axis=0), 16)
    b_loc = lax.dynamic_slice_in_dim(w['b_mod'], me * cols, cols, axis=1)
    mod_loc = _mod_fwd(cc16, w['w_mod'][0], b_loc, "mod_fwd")
    modg = _all_gather(mod_loc, "ag_mod", in_vmem=True)
    mod16 = jnp.transpose(modg, (1, 0, 2)).reshape(16, ND * cols)
    modx = lax.dynamic_index_in_dim(mod16, me, axis=0, keepdims=False).reshape(9, D)
    modc = mod16[8].reshape(9, D)

    shards = {'w_in': w['w_in'][0].astype(_MXU), 'w_out': w['w_out'][0].astype(_MXU)}
    for layer in range(2):
        shards[f'wgu{layer}'] = jnp.stack([w['ffn_wg'][0, layer], w['ffn_wu'][0, layer]]).astype(_MXU)[None]
        shards[f'wd{layer}'] = w['ffn_wd'][0, layer].astype(_MXU)[None]
    WA, WX = w['lru_wa'][0].astype(_MXU), w['lru_wx'][0].astype(_MXU)
    big = ('ffn_wg', 'ffn_wu', 'ffn_wd', 'w_in', 'w_out')
    opt = {n: tuple(a[n].reshape(-1, a[n].shape[-1]) for a in (w, m, v)) for n in big}

    g = _local_step(x[0], ctx[0], loss_target[0], modx, modc, ng, shards, w['q_norm_g'], w['k_norm_g'],
                    conv_w, w['conv_b'], WA, WX, ba, bx, lam, w['final_norm_g'][None, :], idx, opt)

    grad, delta, new_m, new_v = {}, {}, {}, {}
    for n in big:
        grad[n], delta[n], new_m[n], new_v[n] = [o.reshape(w[n].shape) for o in g[n]]

    lw = jnp.stack([g['lru_wa'], g['lru_wx']])
    (lsum,) = _reduce_scatter(lw.reshape(ND, -1, HD), idx, "lru_w")
    lfull = _all_gather(lsum, "ag_lru_w").reshape(lw.shape)
    for i, n in enumerate(('lru_wa', 'lru_wx')):
        shard = w[n].shape
        view = lambda a: a.reshape(-1, HD)
        grad[n] = lfull[i].reshape(shard)
        outs = _adamw_plain(view(w[n]), view(lfull[i]), view(m[n]), view(v[n]), f"adamw_{n}")
        delta[n], new_m[n], new_v[n] = [o.reshape(shard) for o in outs]

    dm = _pad_rows(jnp.stack([g['dmodx'].reshape(-1), g['dmodc'].reshape(-1)]), 8)
    dm_all = _all_gather(dm, "ag_dmod", in_vmem=True)
    dm_loc = lax.dynamic_slice_in_dim(dm_all, me * cols, cols, axis=2)
    gw_mod, dsil = _mod_bwd(dm_loc, cc16, w['w_mod'][0], "mod_bwd")
    grad['w_mod'] = gw_mod[None]
    outs = _adamw_plain(w['w_mod'][0], gw_mod, m['w_mod'][0], v['w_mod'][0], "adamw_w_mod")
    delta['w_mod'], new_m['w_mod'], new_v['w_mod'] = [o[None] for o in outs]
    grad['b_mod'] = _bmod_grad(dm_all, "bmod_grad")

    pad_d = lambda a: jnp.concatenate([a, jnp.zeros((1, D - a.shape[1]), F32)], axis=1)
    small = jnp.concatenate([g['norm_g'], g['conv_w'], g['conv_b'], g['lru_ba'], g['lru_bx'], g['lru_lambda'],
                             g['final_norm_g'], dsil[8:9], pad_d(g['q_norm_g']), pad_d(g['k_norm_g'])], axis=0)
    parts = _all_gather(_pad_rows(small, _SMALL_ROWS), "ag_small_grads", in_vmem=True)
    ssum = _small_finish(parts, w['c_ctx'][None, :], "small_finish")
    mine = lambda rows: lax.dynamic_slice_in_dim(rows, me * Ds, Ds, axis=1)
    grad['norm_g'] = mine(ssum[0:3])[None]
    grad['conv_w'] = mine(ssum[3:7])[None]
    grad['conv_b'] = ssum[7:8]
    grad['lru_ba'] = mine(ssum[8:10])[None]
    grad['lru_bx'] = mine(ssum[10:12])[None]
    grad['lru_lambda'] = mine(ssum[12:14])[None]
    grad['final_norm_g'] = ssum[14]
    grad['c_ctx'] = ssum[_ROW_CCTX]
    grad['q_norm_g'] = ssum[16:17, :HD]
    grad['k_norm_g'] = ssum[17:18, :HD]

    def pack(d):
        flat = jnp.concatenate([d[n].reshape(-1) for n in _SMALL])
        padded = -(-flat.shape[0] // 1024) * 1024
        return jnp.concatenate([flat, jnp.zeros((padded - flat.shape[0],), F32)]).reshape(-1, HD)

    outs = _adamw_plain(pack(w), pack(grad), pack(m), pack(v), "adamw_small")
    off = 0
    for n in _SMALL:
        size = math.prod(w[n].shape)
        for dst, o in zip((delta, new_m, new_v), outs):
            dst[n] = o.reshape(-1)[off:off + size].reshape(w[n].shape)
        off += size

    loss = lax.psum(g['loss'][0, 0], ("x", "y", "c"))
    return (loss, g['grad_x'][None], *[grad[n] for n in _WEIGHTS], *[delta[n] for n in _WEIGHTS],
            *[new_m[n] for n in _WEIGHTS], *[new_v[n] for n in _WEIGHTS])


def kernel(x, c, ctx, c_ctx, w_mod, b_mod, norm_g, ffn_wg, ffn_wu, ffn_wd, w_in, w_out, q_norm_g, k_norm_g, conv_w, conv_b, lru_wa, lru_ba, lru_wx, lru_bx, lru_lambda, final_norm_g, loss_target, m_c_ctx, m_w_mod, m_b_mod, m_norm_g, m_ffn_wg, m_ffn_wu, m_ffn_wd, m_w_in, m_w_out, m_q_norm_g, m_k_norm_g, m_conv_w, m_conv_b, m_lru_wa, m_lru_ba, m_lru_wx, m_lru_bx, m_lru_lambda, m_final_norm_g, v_c_ctx, v_w_mod, v_b_mod, v_norm_g, v_ffn_wg, v_ffn_wu, v_ffn_wd, v_w_in, v_w_out, v_q_norm_g, v_k_norm_g, v_conv_w, v_conv_b, v_lru_wa, v_lru_ba, v_lru_wx, v_lru_bx, v_lru_lambda, v_final_norm_g):
    given = dict(locals())
    w = {n: given[n] for n in _WEIGHTS}
    m = {n: given["m_" + n] for n in _WEIGHTS}
    v = {n: given["v_" + n] for n in _WEIGHTS}
    return _step(w, m, v, x, c, ctx, loss_target)
```

```python
import functools
import math

import jax
import jax.numpy as jnp
from jax import lax
from jax.experimental import pallas as pl
from jax.experimental.pallas import tpu as pltpu

F32 = jnp.float32
_MXU = jnp.bfloat16
ND = 8
TR = 256
HD = 128
EPS = 1e-6
GRID_W = 64
ROPE_THETA = 10000.0
LRU_C = 8.0
VMEM_LIMIT = 56 * 1024 * 1024
SCAN_W = 512
ADAM_LR, ADAM_B1, ADAM_B2, ADAM_EPS, ADAM_WD, ADAM_STEP = 0.001, 0.9, 0.999, 1e-08, 0.01, 10
MESH = pl.DeviceIdType.MESH


class _Job:
    def __init__(self, inputs, out_shapes, sems, start, finish):
        self.inputs, self.out_shapes, self.sems, self.start, self.finish = inputs, out_shapes, sems, start, finish


def _call(body, *, name, grid, in_specs, out_specs, out_shape, scratch=(), aliases=None, jobs=()):
    params = pltpu.CompilerParams(dimension_semantics=("arbitrary",) * len(grid), vmem_limit_bytes=VMEM_LIMIT)
    if not jobs:
        return pl.pallas_call(body, name=name, grid=grid, in_specs=in_specs, out_specs=out_specs, out_shape=out_shape,
                              scratch_shapes=scratch, input_output_aliases=aliases or {}, compiler_params=params)
    single = not isinstance(out_specs, (list, tuple))
    o_specs = [out_specs] if single else list(out_specs)
    o_shape = [out_shape] if single else list(out_shape)
    n_in, n_out, n_scr = len(in_specs), len(o_specs), len(scratch)
    j_in = [a for j in jobs for a in j.inputs]
    j_out = [s for j in jobs for s in j.out_shapes]
    j_sem = [s for j in jobs for s in j.sems]
    hbm = pl.BlockSpec(memory_space=pl.ANY)

    def wrapped(*refs):
        ins, rest = refs[:n_in], refs[n_in:]
        jin, rest = rest[:len(j_in)], rest[len(j_in):]
        outs, rest = rest[:n_out], rest[n_out:]
        jout, rest = rest[:len(j_out)], rest[len(j_out):]
        scr, jsem = rest[:n_scr], rest[n_scr:]
        first = functools.reduce(jnp.logical_and, [pl.program_id(a) == 0 for a in range(len(grid))])
        last = functools.reduce(jnp.logical_and, [pl.program_id(a) == grid[a] - 1 for a in range(len(grid))])

        def each(which):
            i = o = s = 0
            for j in jobs:
                ni, no, ns = len(j.inputs), len(j.out_shapes), len(j.sems)
                getattr(j, which)(jin[i:i + ni], jout[o:o + no], jsem[s:s + ns])
                i, o, s = i + ni, o + no, s + ns

        @pl.when(first)
        def _():
            each("start")

        body(*ins, *outs, *scr)

        @pl.when(last)
        def _():
            each("finish")

    call = pl.pallas_call(wrapped, name=name, grid=grid, in_specs=list(in_specs) + [hbm] * len(j_in),
                          out_specs=o_specs + [hbm] * len(j_out), out_shape=o_shape + j_out,
                          scratch_shapes=list(scratch) + j_sem, input_output_aliases=aliases or {}, compiler_params=params)

    def run(*args):
        res = call(*args, *j_in)
        comp = res[0] if single else list(res[:n_out])
        jres, o = [], n_out
        for j in jobs:
            jres.append(list(res[o:o + len(j.out_shapes)]))
            o += len(j.out_shapes)
        return comp, jres

    return run


def _comm_call(jobs, name):
    j_in = [a for j in jobs for a in j.inputs]
    j_out = [s for j in jobs for s in j.out_shapes]
    j_sem = [s for j in jobs for s in j.sems]
    hbm = pl.BlockSpec(memory_space=pl.ANY)

    def body(*refs):
        jin, jout, jsem = refs[:len(j_in)], refs[len(j_in):len(j_in) + len(j_out)], refs[len(j_in) + len(j_out):]
        for which in ("start", "finish"):
            i = o = s = 0
            for j in jobs:
                ni, no, ns = len(j.inputs), len(j.out_shapes), len(j.sems)
                getattr(j, which)(jin[i:i + ni], jout[o:o + no], jsem[s:s + ns])
                i, o, s = i + ni, o + no, s + ns

    res = pl.pallas_call(body, name=name, out_shape=j_out, in_specs=[hbm] * len(j_in), out_specs=[hbm] * len(j_out),
                         scratch_shapes=j_sem)(*j_in)
    jres, o = [], 0
    for j in jobs:
        jres.append(list(res[o:o + len(j.out_shapes)]))
        o += len(j.out_shapes)
    return jres


def _sds(shape, dtype=F32):
    return jax.ShapeDtypeStruct(tuple(shape), dtype)


def _dot(a, b):
    return jnp.dot(a.astype(_MXU), b.astype(_MXU), preferred_element_type=F32)


def _dot_nt(a, b):
    return lax.dot_general(a.astype(_MXU), b.astype(_MXU), (((1,), (1,)), ((), ())), preferred_element_type=F32)


def _dot_tn(a, b):
    return lax.dot_general(a.astype(_MXU), b.astype(_MXU), (((0,), (0,)), ((), ())), preferred_element_type=F32)


def _sigmoid(x):
    return 0.5 * jnp.tanh(0.5 * x) + 0.5


_GELU_C = math.sqrt(2.0 / math.pi)


def _gelu_and_grad(x):
    x2 = x * x
    t = jnp.tanh(_GELU_C * (x + 0.044715 * x * x2))
    ge = 0.5 * x * (1.0 + t)
    dge = 0.5 * (1.0 + t) + 0.5 * x * (1.0 - t * t) * (_GELU_C * (1.0 + 3.0 * 0.044715 * x2))
    return ge, dge


def _tm(rows):
    assert rows % 4 == 0 and (rows // 4) % 16 == 0
    return rows // 4


def _row_spec(width, nmax=None):
    if nmax is None:
        return pl.BlockSpec((TR, width), lambda i: (i, 0))
    return pl.BlockSpec((TR, width), lambda i: (jnp.minimum(i, nmax), 0))


def _full_spec(shape):
    n = len(shape)
    return pl.BlockSpec(tuple(shape), lambda *_: (0,) * n)


def _mod_spec(D, nx):
    return pl.BlockSpec((None, 9, D), lambda i: (i // nx, 0, 0))


def _norm_mod_fwd(X, ng, mod2, k, rows, nx, name):
    D = X.shape[1]

    def body(x_ref, g_ref, mod_ref, h_ref):
        x = x_ref[...]
        r = lax.rsqrt(jnp.mean(x * x, axis=-1, keepdims=True) + EPS)
        n = (x * r) * g_ref[...]
        h_ref[...] = (n * (1.0 + mod_ref[3 * k + 1:3 * k + 2, :]) + mod_ref[3 * k:3 * k + 1, :]).astype(h_ref.dtype)

    return _call(body, name=name, grid=(rows // TR,),
                 in_specs=[_row_spec(D), _full_spec((1, D)), _mod_spec(D, nx)],
                 out_specs=_row_spec(D), out_shape=_sds((rows, D), _MXU))(X, ng, mod2)


def _norm_mod_bwd(X, dH, dXres, ng, mod2, k, rows, nx, res_tiles, name, jobs=()):
    D = X.shape[1]
    ngroups = -(-(rows // TR) // nx)

    def body(x_ref, dh_ref, dres_ref, g_ref, mod_ref, dx_ref, dsh_ref, dsc_ref, dng_ref):
        i = pl.program_id(0)
        x = x_ref[...]
        dh = dh_ref[...]
        g = g_ref[...]
        r = lax.rsqrt(jnp.mean(x * x, axis=-1, keepdims=True) + EPS)
        xh = x * r
        n = xh * g
        dn_mod = dh * (1.0 + mod_ref[3 * k + 1:3 * k + 2, :])

        @pl.when(i % nx == 0)
        def _():
            dsh_ref[...] = jnp.zeros_like(dsh_ref)
            dsc_ref[...] = jnp.zeros_like(dsc_ref)

        @pl.when(i == 0)
        def _():
            dng_ref[...] = jnp.zeros_like(dng_ref)

        dsh_ref[...] += jnp.sum(dh, axis=0, keepdims=True)
        dsc_ref[...] += jnp.sum(dh * n, axis=0, keepdims=True)
        dng_ref[...] += jnp.sum(dn_mod * xh, axis=0, keepdims=True)
        dn = dn_mod * g
        dres = jnp.where(i < res_tiles, dres_ref[...], 0.0)
        dx_ref[...] = r * (dn - xh * jnp.mean(dn * xh, axis=-1, keepdims=True)) + dres

    grp = pl.BlockSpec((None, 1, D), lambda i: (i // nx, 0, 0))
    return _call(body, jobs=jobs, name=name, grid=(rows // TR,),
                 in_specs=[_row_spec(D), _row_spec(D), _row_spec(D, res_tiles - 1), _full_spec((1, D)), _mod_spec(D, nx)],
                 out_specs=[_row_spec(D), grp, grp, _full_spec((1, D))],
                 out_shape=[_sds((rows, D)), _sds((ngroups, 1, D)), _sds((ngroups, 1, D)), _sds((1, D))],
                 )(X, dH, dXres, ng, mod2)


def _res_bwd(dX, Y, mod2, k, coef, rows, nx, name):
    D = dX.shape[1]
    ngroups = -(-(rows // TR) // nx)

    def body(dx_ref, y_ref, mod_ref, dy_ref, dg_ref):
        i = pl.program_id(0)
        dx = dx_ref[...]

        @pl.when(i % nx == 0)
        def _():
            dg_ref[...] = jnp.zeros_like(dg_ref)

        dy_ref[...] = ((coef * mod_ref[3 * k + 2:3 * k + 3, :]) * dx).astype(dy_ref.dtype)
        dg_ref[...] += jnp.sum(coef * dx * y_ref[...], axis=0, keepdims=True)

    return _call(body, name=name, grid=(rows // TR,),
                 in_specs=[_row_spec(D), _row_spec(D), _mod_spec(D, nx)],
                 out_specs=[_row_spec(D), pl.BlockSpec((None, 1, D), lambda i: (i // nx, 0, 0))],
                 out_shape=[_sds((rows, D), _MXU), _sds((ngroups, 1, D))])(dX, Y, mod2)


def _final_loss(X3, fg, target, name):
    S, D = X3.shape

    def body(x_ref, g_ref, t_ref, loss_ref, dx_ref, dg_ref):
        i = pl.program_id(0)
        x = x_ref[...]
        g = g_ref[...]
        r = lax.rsqrt(jnp.mean(x * x, axis=-1, keepdims=True) + EPS)
        n = x * r
        err = n * g - t_ref[...]

        @pl.when(i == 0)
        def _():
            loss_ref[...] = jnp.zeros_like(loss_ref)
            dg_ref[...] = jnp.zeros_like(dg_ref)

        loss_ref[...] += 0.5 * jnp.sum(jnp.mean(err * err, axis=-1, keepdims=True), axis=0, keepdims=True)
        dy = err * (1.0 / D)
        dg_ref[...] += jnp.sum(dy * n, axis=0, keepdims=True)
        dn = dy * g
        dx_ref[...] = r * (dn - n * jnp.mean(dn * n, axis=-1, keepdims=True))

    return _call(body, name=name, grid=(S // TR,),
                 in_specs=[_row_spec(D), _full_spec((1, D)), _row_spec(D)],
                 out_specs=[_full_spec((1, 1)), _row_spec(D), _full_spec((1, D))],
                 out_shape=[_sds((1, 1)), _sds((S, D)), _sds((1, D))])(X3, fg, target)


def _ffn_gate(H, WG, rows, name, jobs=()):
    D = H.shape[1]
    Fb = WG.shape[-1]

    def body(h_ref, w_ref, g_ref):
        g_ref[...] = _dot(h_ref[...], w_ref[...])

    tm = _tm(rows)
    return _call(body, jobs=jobs, name=name, grid=(ND, rows // tm),
                 in_specs=[pl.BlockSpec((tm, D), lambda d, m: (m, 0)), pl.BlockSpec((None, None, D, Fb), lambda d, m: (d, 0, 0, 0))],
                 out_specs=pl.BlockSpec((None, tm, Fb), lambda d, m: (d, m, 0)), out_shape=_sds((ND, rows, Fb)))(H, WG)


def _ffn_up(H, WG, WU, rows, name, jobs=(), G=None):
    D = H.shape[1]
    Fb = WU.shape[-1]

    def body(h_ref, w_ref, x_ref, *outs):
        h = h_ref[...]
        g = x_ref[...] if G is not None else _dot(h, x_ref[...])
        u = _dot(h, w_ref[...])
        if G is None:
            outs[0][...] = g
        outs[-2][...] = u
        outs[-1][...] = ((g * _sigmoid(g)) * u).astype(outs[-1].dtype)

    tm = _tm(rows)
    blk = pl.BlockSpec((None, tm, Fb), lambda d, m: (d, m, 0))
    wspec = pl.BlockSpec((None, None, D, Fb), lambda d, m: (d, 0, 0, 0))
    f32o, bfo = _sds((ND, rows, Fb)), _sds((ND, rows, Fb), _MXU)
    return _call(body, jobs=jobs, name=name, grid=(ND, rows // tm),
                 in_specs=[pl.BlockSpec((tm, D), lambda d, m: (m, 0)), wspec, blk if G is not None else wspec],
                 out_specs=[blk, blk] if G is not None else [blk, blk, blk],
                 out_shape=[f32o, bfo] if G is not None else [f32o, f32o, bfo])(H, WU, G if G is not None else WG)


def _ffn_down(A, WD, layer, X, mod2, k, rows, S, name, jobs=()):
    Fb, D = WD.shape[-2:]
    tm = _tm(rows) // 2

    def body(a_ref, w_ref, x_ref, mod_ref, y_ref, xn_ref, acc_ref):
        d = pl.program_id(1)

        @pl.when(d == 0)
        def _():
            acc_ref[...] = jnp.zeros_like(acc_ref)

        acc_ref[...] += _dot(a_ref[...], w_ref[...])

        @pl.when(d == ND - 1)
        def _():
            y = acc_ref[...]
            y_ref[...] = y
            t = pl.program_id(0) * tm + lax.broadcasted_iota(jnp.int32, (tm, 1), 0)
            gate = jnp.where(t >= S, mod_ref[1, 3 * k + 2:3 * k + 3, :], mod_ref[0, 3 * k + 2:3 * k + 3, :])
            xn_ref[...] = x_ref[...] + (0.5 * gate) * y

    row = pl.BlockSpec((tm, D), lambda m, d: (m, 0))
    return _call(body, jobs=jobs, name=name, grid=(rows // tm, ND),
                 in_specs=[pl.BlockSpec((None, tm, Fb), lambda m, d: (d, m, 0)),
                           pl.BlockSpec((None, None, Fb, D), lambda m, d: (d, layer, 0, 0)),
                           row, pl.BlockSpec((2, 9, D), lambda m, d: (0, 0, 0))],
                 out_specs=[row, row], out_shape=[_sds((rows, D)), _sds((rows, D))],
                 scratch=[pltpu.VMEM((tm, D), F32)])(A, WD, X, mod2)


def _ffn_dact(dYb, WD, layer, G, U, rows, name, jobs=()):
    Fb, D = WD.shape[-2:]

    def body(dy_ref, w_ref, g_ref, u_ref, dg_ref, du_ref):
        da = _dot_nt(dy_ref[...], w_ref[...])
        g = g_ref[...]
        sg = _sigmoid(g)
        dg_ref[...] = (da * u_ref[...] * (sg * (1.0 + g * (1.0 - sg)))).astype(dg_ref.dtype)
        du_ref[...] = (da * (g * sg)).astype(du_ref.dtype)

    tm = _tm(rows)
    blk = pl.BlockSpec((None, tm, Fb), lambda d, m: (d, m, 0))
    return _call(body, jobs=jobs, name=name, grid=(ND, rows // tm),
                 in_specs=[pl.BlockSpec((tm, D), lambda d, m: (m, 0)),
                           pl.BlockSpec((None, None, Fb, D), lambda d, m: (d, layer, 0, 0)), blk, blk],
                 out_specs=[blk, blk],
                 out_shape=[_sds((ND, rows, Fb), _MXU), _sds((ND, rows, Fb), _MXU)])(dYb, WD, G, U)


def _ffn_dh(dG, dU, WG, WU, rows, name, jobs=()):
    D, Fb = WG.shape[-2:]

    def body(dg_ref, du_ref, wg_ref, wu_ref, dh_ref, acc_ref):
        d = pl.program_id(1)

        @pl.when(d == 0)
        def _():
            acc_ref[...] = jnp.zeros_like(acc_ref)

        acc_ref[...] += _dot_nt(dg_ref[...], wg_ref[...]) + _dot_nt(du_ref[...], wu_ref[...])

        @pl.when(d == ND - 1)
        def _():
            dh_ref[...] = acc_ref[...]

    tm = _tm(rows)
    blk = pl.BlockSpec((None, tm, Fb), lambda m, d: (d, m, 0))
    wspec = pl.BlockSpec((None, None, D, Fb), lambda m, d: (d, 0, 0, 0))
    return _call(body, jobs=jobs, name=name, grid=(rows // tm, ND), in_specs=[blk, blk, wspec, wspec],
                 out_specs=pl.BlockSpec((tm, D), lambda m, d: (m, 0)), out_shape=_sds((rows, D)),
                 scratch=[pltpu.VMEM((tm, D), F32)])(dG, dU, WG, WU)


def _mm_tn(A, a_spec, B, b_spec, out_shape, out_spec, rows, name, prev=None, jobs=()):
    def body(*refs):
        a_ref, b_ref, o_ref = refs[0], refs[1], refs[-1]

        @pl.when(pl.program_id(1) == 0)
        def _():
            o_ref[...] = jnp.zeros_like(o_ref)

        o_ref[...] += _dot_tn(a_ref[...], b_ref[...])

    in_specs = [a_spec, b_spec]
    args = [A, B]
    aliases = None
    if prev is not None:
        in_specs.append(pl.BlockSpec(memory_space=pl.ANY))
        args.append(prev)
        aliases = {2: 0}
    return _call(body, jobs=jobs, name=name, grid=(ND, rows // _tm(rows)), in_specs=in_specs, out_specs=out_spec,
                 out_shape=_sds(out_shape), aliases=aliases)(*args)


def _proj_in(H2, WIN, name):
    R, D = H2.shape
    Nb = WIN.shape[-1]

    def body(h_ref, w_ref, p_ref):
        p_ref[...] = _dot(h_ref[...], w_ref[...])

    tm = _tm(R)
    return _call(body, name=name, grid=(ND, R // tm),
                 in_specs=[pl.BlockSpec((tm, D), lambda d, m: (m, 0)), pl.BlockSpec((None, D, Nb), lambda d, m: (d, 0, 0))],
                 out_specs=pl.BlockSpec((tm, Nb), lambda d, m: (m, d)), out_shape=_sds((R, ND * Nb)))(H2, WIN)


def _dproj_in(dP, WIN, name, jobs=()):
    R = dP.shape[0]
    D, Nb = WIN.shape[-2:]

    def body(dp_ref, w_ref, dh_ref, acc_ref):
        d = pl.program_id(1)

        @pl.when(d == 0)
        def _():
            acc_ref[...] = jnp.zeros_like(acc_ref)

        acc_ref[...] += _dot_nt(dp_ref[...], w_ref[...])

        @pl.when(d == ND - 1)
        def _():
            dh_ref[...] = acc_ref[...]

    tm = _tm(R)
    return _call(body, jobs=jobs, name=name, grid=(R // tm, ND),
                 in_specs=[pl.BlockSpec((tm, Nb), lambda m, d: (m, d)), pl.BlockSpec((None, D, Nb), lambda m, d: (d, 0, 0))],
                 out_specs=pl.BlockSpec((tm, D), lambda m, d: (m, 0)), out_shape=_sds((R, D)),
                 scratch=[pltpu.VMEM((tm, D), F32)])(dP, WIN)


def _proj_out(mixb, WOUT, X1, mod2, S, name):
    D = WOUT.shape[0]

    def body(m_ref, w_ref, x_ref, mod_ref, z_ref, xn_ref):
        z = _dot(m_ref[...], w_ref[...])
        z_ref[...] = z
        xn_ref[...] = x_ref[...] + mod_ref[5:6, :] * z

    return _call(body, name=name, grid=(S // TR,),
                 in_specs=[_row_spec(D), _full_spec((D, D)), _row_spec(D), pl.BlockSpec((None, 9, D), lambda i: (0, 0, 0))],
                 out_specs=[_row_spec(D), _row_spec(D)], out_shape=[_sds((S, D)), _sds((S, D))])(mixb, WOUT, X1, mod2)


def _dproj_out(dZb, WOUT, name):
    S, D = dZb.shape

    def body(dz_ref, w_ref, dm_ref):
        dm_ref[...] = _dot_nt(dz_ref[...], w_ref[...])

    return _call(body, name=name, grid=(S // TR,), in_specs=[_row_spec(D), _full_spec((D, D))],
                 out_specs=_row_spec(D), out_shape=_sds((S, D)))(dZb, WOUT)


def _pair_swap(t):
    lane = lax.broadcasted_iota(jnp.int32, t.shape, 1)
    return jnp.where(lane % 2 == 0, pltpu.roll(t, HD - 1, 1), pltpu.roll(t, 1, 1))


SM_SCALE = HD ** -0.5


def _qkv_prep(P, qg, kg, COS, SIN, D, KVW, name):
    R = P.shape[0]
    W = D + 2 * KVW
    nq, nk = D // HD, KVW // HD

    def body(p_ref, qg_ref, kg_ref, cos_ref, sin_ref, q_ref, k_ref, v_ref):
        cos, sin = cos_ref[...], sin_ref[...]

        def head(t, g):
            y = (t * lax.rsqrt(jnp.mean(t * t, axis=-1, keepdims=True) + EPS)) * g
            return y * cos + _pair_swap(y) * sin

        for h in range(nq):
            q_ref[:, h * HD:(h + 1) * HD] = (head(p_ref[:, h * HD:(h + 1) * HD], qg_ref[...]) * SM_SCALE).astype(q_ref.dtype)
        for h in range(nk):
            k_ref[:, h * HD:(h + 1) * HD] = head(p_ref[:, D + h * HD:D + (h + 1) * HD], kg_ref[...]).astype(k_ref.dtype)
        v_ref[...] = p_ref[:, D + KVW:W].astype(v_ref.dtype)

    return _call(body, name=name, grid=(R // TR,),
                 in_specs=[_row_spec(W), _full_spec((1, HD)), _full_spec((1, HD)), _row_spec(HD), _row_spec(HD)],
                 out_specs=[_row_spec(D), _row_spec(KVW), _row_spec(KVW)],
                 out_shape=[_sds((R, D), _MXU), _sds((R, KVW), _MXU), _sds((R, KVW), _MXU)])(P, qg, kg, COS, SIN)


def _qkv_bwd(P, dq, dk, dv, qg, kg, COS, SIN, dP, D, KVW, nx, name):
    R, INW = P.shape
    W = D + 2 * KVW
    nq, nk = D // HD, KVW // HD

    def body(p_ref, dq_ref, dk_ref, dv_ref, qg_ref, kg_ref, cos_ref, sin_ref, dp_in, dp_ref, dqg_ref, dkg_ref):
        i = pl.program_id(0)
        cos, sin = cos_ref[...], sin_ref[...]

        @pl.when(i == 0)
        def _():
            dqg_ref[...] = jnp.zeros_like(dqg_ref)
            dkg_ref[...] = jnp.zeros_like(dkg_ref)

        def head_bwd(t, g, dout):
            r = lax.rsqrt(jnp.mean(t * t, axis=-1, keepdims=True) + EPS)
            n = t * r
            dy = dout * cos + _pair_swap(dout * sin)
            dn = dy * g
            return r * (dn - n * jnp.mean(dn * n, axis=-1, keepdims=True)), jnp.sum(dy * n, axis=0, keepdims=True)

        dqg = jnp.zeros((1, HD), F32)
        for h in range(nq):
            sl = slice(h * HD, (h + 1) * HD)
            dt, dg = head_bwd(p_ref[:, sl], qg_ref[...], jnp.where(i < nx, dq_ref[:, sl] * SM_SCALE, 0.0))
            dp_ref[:, sl] = dt.astype(dp_ref.dtype)
            dqg += dg
        dkg = jnp.zeros((1, HD), F32)
        for h in range(nk):
            sl = slice(h * HD, (h + 1) * HD)
            dt, dg = head_bwd(p_ref[:, D + h * HD:D + (h + 1) * HD], kg_ref[...], dk_ref[:, sl])
            dp_ref[:, D + h * HD:D + (h + 1) * HD] = dt.astype(dp_ref.dtype)
            dkg += dg
        dqg_ref[...] += dqg
        dkg_ref[...] += dkg
        dp_ref[:, D + KVW:W] = dv_ref[...].astype(dp_ref.dtype)

    return _call(body, name=name, grid=(R // TR,),
                 in_specs=[_row_spec(W), _row_spec(D, nx - 1), _row_spec(KVW), _row_spec(KVW), _full_spec((1, HD)),
                           _full_spec((1, HD)), _row_spec(HD), _row_spec(HD), pl.BlockSpec(memory_space=pl.ANY)],
                 out_specs=[_row_spec(W), _full_spec((1, HD)), _full_spec((1, HD))],
                 out_shape=[_sds((R, INW), _MXU), _sds((1, HD)), _sds((1, HD))],
                 aliases={8: 0})(P, dq, dk, dv, qg, kg, COS, SIN, dP)


def _stack_heads(ref, G, dtype=None):
    parts = [ref[:, g * HD:(g + 1) * HD] for g in range(G)]
    out = jnp.concatenate(parts, axis=0)
    return out if dtype is None else out.astype(dtype)


_KEY_CHUNKS = 4


def _attn_fwd(q, k, v, S, G, name, jobs=()):
    R, KVW = k.shape
    D = q.shape[1]
    Kh = KVW // HD
    tq = 128
    kc = R // _KEY_CHUNKS
    assert R % _KEY_CHUNKS == 0 and kc % 16 == 0

    def body(q_ref, k_ref, v_ref, o_ref, lse_ref):
        qs = _stack_heads(q_ref, G)
        m = l = acc = None
        for c in range(_KEY_CHUNKS):
            s = _dot_nt(qs, k_ref[c * kc:(c + 1) * kc, :])
            mc = jnp.max(s, axis=-1, keepdims=True)
            m_new = mc if c == 0 else jnp.maximum(m, mc)
            p = jnp.exp(s - m_new)
            ps = jnp.sum(p, axis=-1, keepdims=True)
            pv = _dot(p, v_ref[c * kc:(c + 1) * kc, :])
            if c == 0:
                l, acc = ps, pv
            else:
                alpha = jnp.exp(m - m_new)
                l = alpha * l + ps
                acc = alpha * acc + pv
            m = m_new
        o = acc / l
        lse = m + jnp.log(l)
        for g in range(G):
            o_ref[:, g * HD:(g + 1) * HD] = o[g * tq:(g + 1) * tq, :]
            lse_ref[g] = jnp.broadcast_to(lse[g * tq:(g + 1) * tq, :], (tq, HD))

    return _call(body, jobs=jobs, name=name, grid=(Kh, S // tq),
                 in_specs=[pl.BlockSpec((tq, G * HD), lambda h, i: (i, h)), pl.BlockSpec((R, HD), lambda h, i: (0, h)),
                           pl.BlockSpec((R, HD), lambda h, i: (0, h))],
                 out_specs=[pl.BlockSpec((tq, G * HD), lambda h, i: (i, h)),
                            pl.BlockSpec((None, G, tq, HD), lambda h, i: (h, 0, i, 0))],
                 out_shape=[_sds((S, D)), _sds((Kh, G, S, HD))])(q, k, v)


def _attn_bwd(q, k, v, O, LSE, dOb, S, G, name, jobs=()):
    R, KVW = k.shape
    D = q.shape[1]
    Kh = KVW // HD
    tq = 128
    kc = R // _KEY_CHUNKS

    def body(q_ref, k_ref, v_ref, o_ref, lse_ref, do_ref, dq_ref, dk_ref, dv_ref):
        @pl.when(pl.program_id(1) == 0)
        def _():
            dk_ref[...] = jnp.zeros_like(dk_ref)
            dv_ref[...] = jnp.zeros_like(dv_ref)

        qs = _stack_heads(q_ref, G)
        do = _stack_heads(do_ref, G)
        o = _stack_heads(o_ref, G)
        lse = jnp.concatenate([lse_ref[g][:, 0:1] for g in range(G)], axis=0)
        delta = jnp.sum(do.astype(F32) * o, axis=-1, keepdims=True)
        dq = None
        for c in range(_KEY_CHUNKS):
            rows = slice(c * kc, (c + 1) * kc)
            kk = k_ref[rows, :]
            p = jnp.exp(_dot_nt(qs, kk) - lse)
            dp = _dot_nt(do, v_ref[rows, :])
            ds = (p * (dp - delta)).astype(_MXU)
            dqc = _dot(ds, kk)
            dq = dqc if c == 0 else dq + dqc
            dk_ref[rows, :] += _dot_tn(ds, qs)
            dv_ref[rows, :] += _dot_tn(p, do)
        for g in range(G):
            dq_ref[:, g * HD:(g + 1) * HD] = dq[g * tq:(g + 1) * tq, :]

    qspec = pl.BlockSpec((tq, G * HD), lambda h, i: (i, h))
    kspec = pl.BlockSpec((R, HD), lambda h, i: (0, h))
    return _call(body, jobs=jobs, name=name, grid=(Kh, S // tq),
                 in_specs=[qspec, kspec, kspec, qspec, pl.BlockSpec((None, G, tq, HD), lambda h, i: (h, 0, i, 0)), qspec],
                 out_specs=[qspec, kspec, kspec],
                 out_shape=[_sds((S, D)), _sds((R, KVW)), _sds((R, KVW))])(q, k, v, O, LSE, dOb)


def _halo_specs(R, CB, col0):
    nt8 = TR // 8
    return [pl.BlockSpec((8, CB), lambda h, i: (jnp.maximum(i * nt8 - 1, 0), col0 + h)),
            pl.BlockSpec((TR, CB), lambda h, i: (i, col0 + h)),
            pl.BlockSpec((8, CB), lambda h, i: (jnp.minimum((i + 1) * nt8, R // 8 - 1), col0 + h))]


def _seq_pos(i, S, R, CB):
    t = i * TR - 8 + lax.broadcasted_iota(jnp.int32, (TR + 16, CB), 0)
    start = jnp.where(t >= S, S, 0)
    end = jnp.where(t >= S, R, S)
    return t - start, end - t


def _shift(cat, by):
    return pltpu.roll(cat, by % cat.shape[0], 0)


def _gate_mats(xcb, w_ref, dirn, nb):
    return jnp.concatenate([_dot(xcb[:, b * HD:(b + 1) * HD], w_ref[dirn, b]) for b in range(nb)], axis=1)


def _lru_gates_fwd(P, conv_w, conv_b, WA, WX, ba, bx, lam, S, D, col0, name):
    R = P.shape[0]
    CB = D // 2
    nb = CB // HD

    def body(xp_ref, x_ref, xn_ref, cw_ref, cb_ref, wa_ref, wx_ref, ba_ref, bx_ref, lam_ref,
             xc_ref, af_ref, uf_ref, ab_ref, ub_ref):
        i = pl.program_id(1)
        cat = jnp.concatenate([xp_ref[...], x_ref[...], xn_ref[...]], axis=0)
        from_start, to_end = _seq_pos(i, S, R, CB)
        conv = (cb_ref[...] + cw_ref[2:3, :] * cat
                + cw_ref[0:1, :] * jnp.where(from_start >= 2, _shift(cat, 2), 0.0)
                + cw_ref[1:2, :] * jnp.where(from_start >= 1, _shift(cat, 1), 0.0)
                + cw_ref[3:4, :] * jnp.where(to_end >= 2, _shift(cat, -1), 0.0))
        xc = conv[8:8 + TR, :]
        xc_ref[...] = xc
        xcb = xc.astype(_MXU)
        for dirn, (a_ref, u_ref) in enumerate(((af_ref, uf_ref), (ab_ref, ub_ref))):
            ra = _sigmoid(_gate_mats(xcb, wa_ref, dirn, nb) + ba_ref[dirn:dirn + 1, :])
            ia = _sigmoid(_gate_mats(xcb, wx_ref, dirn, nb) + bx_ref[dirn:dirn + 1, :])
            nl = -lam_ref[dirn:dirn + 1, :]
            sp = jnp.maximum(nl, 0.0) + jnp.log(1.0 + jnp.exp(-jnp.abs(nl)))
            la = (-LRU_C) * ra * sp
            a_ref[...] = jnp.exp(la)
            u_ref[...] = jnp.sqrt(1.0 - jnp.exp(2.0 * la)) * (ia * xc)

    def par(r):
        return pl.BlockSpec((r, CB), lambda h, i: (0, h))

    wspec = pl.BlockSpec((2, nb, HD, HD), lambda h, i: (0, h, 0, 0))
    out = pl.BlockSpec((TR, CB), lambda h, i: (i, h))
    return _call(body, name=name, grid=(2, R // TR),
                 in_specs=_halo_specs(R, CB, col0) + [par(4), par(1), wspec, wspec, par(2), par(2), par(2)],
                 out_specs=[out] * 5, out_shape=[_sds((R, D))] * 5,
                 )(P, P, P, conv_w, conv_b, WA, WX, ba, bx, lam)


def _lru_gates_bwd(xc, hfp, hbp, gf, gb, WA, WX, ba, bx, lam, name):
    R, D = xc.shape
    CB = D // 2
    nb = CB // HD

    def body(xc_ref, hfp_ref, hbp_ref, gf_ref, gb_ref, wa_ref, wx_ref, ba_ref, bx_ref, lam_ref,
             dxc_ref, dwa_ref, dwx_ref, dba_ref, dbx_ref, dlam_ref):
        @pl.when(pl.program_id(1) == 0)
        def _():
            for r in (dwa_ref, dwx_ref, dba_ref, dbx_ref, dlam_ref):
                r[...] = jnp.zeros_like(r)

        xc = xc_ref[...]
        xcb = xc.astype(_MXU)
        dxc = jnp.zeros_like(xc)
        for dirn, (hp_ref, g_ref) in enumerate(((hfp_ref, gf_ref), (hbp_ref, gb_ref))):
            ra = _sigmoid(_gate_mats(xcb, wa_ref, dirn, nb) + ba_ref[dirn:dirn + 1, :])
            ia = _sigmoid(_gate_mats(xcb, wx_ref, dirn, nb) + bx_ref[dirn:dirn + 1, :])
            nl = -lam_ref[dirn:dirn + 1, :]
            sp = jnp.maximum(nl, 0.0) + jnp.log(1.0 + jnp.exp(-jnp.abs(nl)))
            la = (-LRU_C) * ra * sp
            a = jnp.exp(la)
            e2 = jnp.exp(2.0 * la)
            s = jnp.sqrt(1.0 - e2)
            du = g_ref[...]
            dla = du * hp_ref[...] * a - du * (ia * xc) * (e2 / s)
            dxc += du * s * ia
            dza = (dla * (-LRU_C) * sp) * ra * (1.0 - ra)
            dzx = (du * s * xc) * ia * (1.0 - ia)
            dlam_ref[dirn:dirn + 1, :] += jnp.sum(dla * (LRU_C * ra) * _sigmoid(nl), axis=0, keepdims=True)
            dba_ref[dirn:dirn + 1, :] += jnp.sum(dza, axis=0, keepdims=True)
            dbx_ref[dirn:dirn + 1, :] += jnp.sum(dzx, axis=0, keepdims=True)
            dzab, dzxb = dza.astype(_MXU), dzx.astype(_MXU)
            parts = []
            for b in range(nb):
                sl = slice(b * HD, (b + 1) * HD)
                dwa_ref[dirn, b] += _dot_tn(xcb[:, sl], dzab[:, sl])
                dwx_ref[dirn, b] += _dot_tn(xcb[:, sl], dzxb[:, sl])
                parts.append(_dot_nt(dzab[:, sl], wa_ref[dirn, b]) + _dot_nt(dzxb[:, sl], wx_ref[dirn, b]))
            dxc += jnp.concatenate(parts, axis=1)
        dxc_ref[...] = dxc

    def par(r):
        return pl.BlockSpec((r, CB), lambda h, i: (0, h))

    wspec = pl.BlockSpec((2, nb, HD, HD), lambda h, i: (0, h, 0, 0))
    tile = pl.BlockSpec((TR, CB), lambda h, i: (i, h))
    nbt = D // HD
    return _call(body, name=name, grid=(2, R // TR),
                 in_specs=[tile] * 5 + [wspec, wspec, par(2), par(2), par(2)],
                 out_specs=[tile, wspec, wspec, par(2), par(2), par(2)],
                 out_shape=[_sds((R, D)), _sds((2, nbt, HD, HD)), _sds((2, nbt, HD, HD)), _sds((2, D)), _sds((2, D)),
                            _sds((2, D))])(xc, hfp, hbp, gf, gb, WA, WX, ba, bx, lam)


def _scan_rows(n_groups, step, init):
    return lax.fori_loop(0, n_groups, lambda gi, c: step(pl.multiple_of(gi * 8, 8), c), init)


def _lru_scan_fwd(af, uf, ab, ub, S, name):
    R, D = af.shape
    W = min(SCAN_W, D)
    nm, nx = R // TR, S // TR
    nc = nm - nx
    ng = TR // 8

    def body(af_ref, uf_ref, ab_ref, ub_ref, hf_ref, hfp_ref, hb_ref, hbp_ref, cf_ref, cb_ref):
        @pl.when(pl.program_id(1) == 0)
        def _():
            cf_ref[...] = jnp.zeros_like(cf_ref)
            cb_ref[...] = jnp.zeros_like(cb_ref)

        def step(base, carry):
            hf, hb = carry
            baseb = pl.multiple_of(TR - 8 - base, 8)
            for r in range(8):
                tf, tb = base + r, baseb + 7 - r
                hfp_ref[pl.ds(tf, 1), :] = hf
                hf = af_ref[pl.ds(tf, 1), :] * hf + uf_ref[pl.ds(tf, 1), :]
                hf_ref[pl.ds(tf, 1), :] = hf
                hbp_ref[pl.ds(tb, 1), :] = hb
                hb = ab_ref[pl.ds(tb, 1), :] * hb + ub_ref[pl.ds(tb, 1), :]
                hb_ref[pl.ds(tb, 1), :] = hb
            return hf, hb

        hf, hb = _scan_rows(ng, step, (cf_ref[0:1, :], cb_ref[0:1, :]))
        cf_ref[0:1, :] = hf
        cb_ref[0:1, :] = hb

    fmap = lambda j, s: (jnp.where(s < nc, nx + s, s - nc), j)
    bmap = lambda j, s: (nm - 1 - s, j)
    fs, bs = pl.BlockSpec((TR, W), fmap), pl.BlockSpec((TR, W), bmap)
    return _call(body, name=name, grid=(D // W, nm), in_specs=[fs, fs, bs, bs], out_specs=[fs, fs, bs, bs],
                 out_shape=[_sds((R, D))] * 4, scratch=[pltpu.VMEM((8, W), F32), pltpu.VMEM((8, W), F32)])(af, uf, ab, ub)


def _lru_scan_bwd(af, ab, dhs, S, name):
    R, D = af.shape
    W = min(SCAN_W, D)
    nm, nx = R // TR, S // TR
    ng = TR // 8

    def body(af_ref, dhf_ref, ab_ref, dhb_ref, gf_ref, gb_ref, cf_ref, cb_ref):
        @pl.when(pl.program_id(1) == 0)
        def _():
            cf_ref[...] = jnp.zeros_like(cf_ref)
            cb_ref[...] = jnp.zeros_like(cb_ref)

        def step(base, carry):
            cf, cb = carry
            based = pl.multiple_of(TR - 8 - base, 8)
            for r in range(8):
                tf, tb = based + 7 - r, base + r
                g = dhf_ref[pl.ds(tf, 1), :] + cf
                gf_ref[pl.ds(tf, 1), :] = g
                cf = af_ref[pl.ds(tf, 1), :] * g
                g = dhb_ref[pl.ds(tb, 1), :] + cb
                gb_ref[pl.ds(tb, 1), :] = g
                cb = ab_ref[pl.ds(tb, 1), :] * g
            return cf, cb

        cf, cb = _scan_rows(ng, step, (cf_ref[0:1, :], cb_ref[0:1, :]))
        cf_ref[0:1, :] = cf
        cb_ref[0:1, :] = cb

    fmap = lambda j, s: (jnp.where(s < nx, nx - 1 - s, nm - 1 - (s - nx)), j)
    bmap = lambda j, s: (s, j)
    fs, bs = pl.BlockSpec((TR, W), fmap), pl.BlockSpec((TR, W), bmap)
    return _call(body, name=name, grid=(D // W, nm), in_specs=[fs, fs, bs, bs], out_specs=[fs, bs],
                 out_shape=[_sds((R, D))] * 2, scratch=[pltpu.VMEM((8, W), F32), pltpu.VMEM((8, W), F32)])(af, dhs, ab, dhs)


def _merge_fwd(P, hf, hb, O, S, D, col_lg, name):
    R = P.shape[0]
    CB = D // 2
    nx = S // TR

    def body(lg_ref, ga_ref, gl_ref, hf_ref, hb_ref, o_ref, mix_ref):
        ge, _ = _gelu_and_grad(lg_ref[...])
        lru = (hf_ref[...] + hb_ref[...]) * ge
        mix_ref[...] = (_sigmoid(ga_ref[...]) * o_ref[...] + _sigmoid(gl_ref[...]) * lru).astype(mix_ref.dtype)

    def col(c0):
        return pl.BlockSpec((TR, CB), lambda h, i: (i, c0 + h))

    return _call(body, name=name, grid=(2, R // TR),
                 in_specs=[col(col_lg), col(col_lg + 2), col(col_lg + 4), col(0), col(0),
                           pl.BlockSpec((TR, CB), lambda h, i: (jnp.minimum(i, nx - 1), h))],
                 out_specs=col(0), out_shape=_sds((R, D), _MXU))(P, P, P, hf, hb, O)


def _merge_bwd(dmix, P, hf, hb, O, S, D, col_lg, name, jobs=()):
    R = P.shape[0]
    CB = D // 2
    nx = S // TR

    def body(dm_ref, lg_ref, ga_ref, gl_ref, hf_ref, hb_ref, o_ref, do_ref, dhs_ref, dp_ref, stash):
        i, sec = pl.program_id(1), pl.program_id(2)

        @pl.when(sec == 0)
        def _():
            dm = jnp.where(i < nx, dm_ref[...], 0.0)
            sa, sl = _sigmoid(ga_ref[...]), _sigmoid(gl_ref[...])
            ge, dge = _gelu_and_grad(lg_ref[...])
            hs = hf_ref[...] + hb_ref[...]
            o = o_ref[...]
            dl = dm * sl
            do_ref[...] = (dm * sa).astype(do_ref.dtype)
            dhs_ref[...] = dl * ge
            stash[0] = (dl * hs * dge).astype(stash.dtype)
            stash[1] = (dm * o * sa * (1.0 - sa)).astype(stash.dtype)
            stash[2] = (dm * (hs * ge) * sl * (1.0 - sl)).astype(stash.dtype)

        dp_ref[...] = stash[sec]

    def col(c0):
        return pl.BlockSpec((TR, CB), lambda h, i, s: (i, c0 + h))

    xrow = pl.BlockSpec((TR, CB), lambda h, i, s: (jnp.minimum(i, nx - 1), h))
    return _call(body, jobs=jobs, name=name, grid=(2, R // TR, 3),
                 in_specs=[xrow, col(col_lg), col(col_lg + 2), col(col_lg + 4), col(0), col(0), xrow],
                 out_specs=[col(0), col(0), pl.BlockSpec((TR, CB), lambda h, i, s: (i, col_lg + 2 * s + h))],
                 out_shape=[_sds((R, D), _MXU), _sds((R, D)), _sds(P.shape, _MXU)],
                 scratch=[pltpu.VMEM((3, TR, CB), _MXU)])(dmix, P, P, P, hf, hb, O)


def _conv_bwd(dxc, P, conv_w, dP, S, D, col0, name, jobs=()):
    R = P.shape[0]
    CB = D // 2

    def body(dp_, d_ref, dn_, xp_ref, x_ref, xn_ref, cw_ref, dp_in, dpo_ref, dcw_ref, dcb_ref):
        i = pl.program_id(1)

        @pl.when(i == 0)
        def _():
            dcw_ref[...] = jnp.zeros_like(dcw_ref)
            dcb_ref[...] = jnp.zeros_like(dcb_ref)

        d = d_ref[...]
        catd = jnp.concatenate([dp_[...], d, dn_[...]], axis=0)
        catx = jnp.concatenate([xp_ref[...], x_ref[...], xn_ref[...]], axis=0)
        from_start, to_end = _seq_pos(i, S, R, CB)
        dxl = (cw_ref[2:3, :] * catd
               + cw_ref[0:1, :] * jnp.where(to_end >= 3, _shift(catd, -2), 0.0)
               + cw_ref[1:2, :] * jnp.where(to_end >= 2, _shift(catd, -1), 0.0)
               + cw_ref[3:4, :] * jnp.where(from_start >= 1, _shift(catd, 1), 0.0))
        dpo_ref[...] = dxl[8:8 + TR, :].astype(dpo_ref.dtype)
        taps = (jnp.where(from_start >= 2, _shift(catx, 2), 0.0), jnp.where(from_start >= 1, _shift(catx, 1), 0.0),
                catx, jnp.where(to_end >= 2, _shift(catx, -1), 0.0))
        for kk in range(4):
            dcw_ref[kk:kk + 1, :] += jnp.sum(d * taps[kk][8:8 + TR, :], axis=0, keepdims=True)
        dcb_ref[...] += jnp.sum(d, axis=0, keepdims=True)

    return _call(body, jobs=jobs, name=name, grid=(2, R // TR),
                 in_specs=_halo_specs(R, CB, 0) + _halo_specs(R, CB, col0)
                 + [pl.BlockSpec((4, CB), lambda h, i: (0, h)), pl.BlockSpec(memory_space=pl.ANY)],
                 out_specs=[pl.BlockSpec((TR, CB), lambda h, i: (i, col0 + h)), pl.BlockSpec((4, CB), lambda h, i: (0, h)),
                            pl.BlockSpec((1, CB), lambda h, i: (0, h))],
                 out_shape=[_sds(dP.shape, dP.dtype), _sds((4, D)), _sds((1, D))],
                 aliases={7: 0})(dxc, dxc, dxc, P, P, P, conv_w, dP)


def _rope_tables(S, C):
    t = jnp.arange(S, dtype=jnp.int32)
    row = (t // GRID_W).astype(F32)
    col = (t % GRID_W).astype(F32)
    axis_dims = HD // 2
    freqs = ROPE_THETA ** (-jnp.arange(0, axis_dims, 2, dtype=F32) / axis_dims)
    ang = jnp.concatenate([row[:, None] * freqs, col[:, None] * freqs], axis=-1)
    cos = jnp.repeat(jnp.cos(ang), 2, axis=-1)
    sin = jnp.repeat(jnp.sin(ang), 2, axis=-1) * jnp.tile(jnp.array([-1.0, 1.0], F32), HD // 2)
    return (jnp.concatenate([cos, jnp.ones((C, HD), F32)], axis=0),
            jnp.concatenate([sin, jnp.zeros((C, HD), F32)], axis=0))


def _local_step(x, ctx, target, modx, modc, ng, shards, qg, kg, conv_w, conv_b, WA, WX, ba, bx, lam, fg, idx, opt):
    S, D = x.shape
    C = ctx.shape[0]
    R = S + C
    nx = S // TR
    Nb = shards['w_in'].shape[-1]
    Fb = shards['wd0'].shape[1]
    KVW = (ND * Nb - 5 * D) // 2
    G = D // KVW
    CB = D // 2
    col_lx = (D + 2 * KVW) // CB
    assert S % TR == 0 and C % TR == 0 and (D + 2 * KVW) % CB == 0 and CB % HD == 0
    mod2 = jnp.stack([modx, modc])
    X0 = jnp.concatenate([x, ctx], axis=0)
    COS, SIN = _rope_tables(S, C)
    ng0, ng1, ng2 = ng[0:1], ng[1:2], ng[2:3]
    ag = lambda n: _ag_job(shards[n])
    sib = lambda Gp: _rs_sibling_job(Gp.reshape(ND, -1, Gp.shape[-1]))
    add = lambda Gp, bufA, tag: _rs_add(Gp.reshape(ND, -1, Gp.shape[-1]), bufA, idx, f"{tag}_rs_add")
    out = {}

    def tn_specs(rows):
        tm = _tm(rows)
        return pl.BlockSpec((None, tm, Fb), lambda d, r: (d, r, 0)), pl.BlockSpec((tm, D), lambda d, r: (r, 0))

    wd_spec = pl.BlockSpec((None, None, Fb, D), lambda d, r: (d, 0, 0, 0))
    wg_spec = pl.BlockSpec((None, None, D, Fb), lambda d, r: (d, 0, 0, 0))

    ((WG0,),) = _comm_call([ag('wg0')], "ag_wg0")
    H1 = _norm_mod_fwd(X0, ng0, mod2, 0, R, nx, "ffn1_norm")
    G1, ((WU0,),) = _ffn_gate(H1, WG0, R, "ffn1_gate", jobs=[ag('wu0')])
    (U1, A1), ((WD0,),) = _ffn_up(H1, None, WU0, R, "ffn1_up", jobs=[ag('wd0')], G=G1)
    (Y1, X1), ((WIN,),) = _ffn_down(A1, WD0, 0, X0, mod2, 0, R, S, "ffn1_down", jobs=[ag('w_in')])
    H2 = _norm_mod_fwd(X1, ng1, mod2, 1, R, nx, "mix_norm")
    P = _proj_in(H2, WIN, "proj_in")
    q, k, v = _qkv_prep(P, qg, kg, COS, SIN, D, KVW, "qkv_prep")
    (O, LSE), ((WOUT,), (WG1,), (WU1,), (WD1,)) = _attn_fwd(
        q, k, v, S, G, "attn_fwd", jobs=[ag('w_out'), ag('wg1'), ag('wu1'), ag('wd1')])
    WOUT = WOUT.reshape(D, D)
    xc, af, uf, ab, ub = _lru_gates_fwd(P, conv_w, conv_b, WA, WX, ba, bx, lam, S, D, col_lx, "lru_gates")
    hf, hfp, hb, hbp = _lru_scan_fwd(af, uf, ab, ub, S, "lru_scan")
    mixb = _merge_fwd(P, hf, hb, O, S, D, col_lx + 2, "merge")
    Z, X2 = _proj_out(mixb, WOUT, X1, mod2, S, "proj_out")
    H3 = _norm_mod_fwd(X2, ng2, mod2, 2, S, nx, "ffn2_norm")
    G3, U3, A3 = _ffn_up(H3, WG1, WU1, S, "ffn2_up")
    Y3, X3 = _ffn_down(A3, WD1, 0, X2, mod2, 2, S, S, "ffn2_down")
    loss, dX3, dfg = _final_loss(X3, fg, target, "loss_head")

    dY3b, dg3 = _res_bwd(dX3, Y3, mod2, 2, 0.5, S, nx, "ffn2_dres")
    dG3, dU3 = _ffn_dact(dY3b, WD1, 0, G3, U3, S, "ffn2_dact")
    blk, row = tn_specs(S)
    dWD1 = _mm_tn(A3, blk, dY3b, row, (ND, 1, Fb, D), wd_spec, S, "ffn2_dwd")
    dWG1 = _mm_tn(H3, row, dG3, blk, (ND, 1, D, Fb), wg_spec, S, "ffn2_dwg")
    dWU1 = _mm_tn(H3, row, dU3, blk, (ND, 1, D, Fb), wg_spec, S, "ffn2_dwu")
    dH3, ((a_wd1,), (a_wg1,), (a_wu1,)) = _ffn_dh(dG3, dU3, WG1, WU1, S, "ffn2_dh", jobs=[sib(dWD1), sib(dWG1), sib(dWU1)])
    T_wd1, own_wd1 = add(dWD1, a_wd1, "wd1")
    T_wg1, own_wg1 = add(dWG1, a_wg1, "wg1")
    T_wu1, own_wu1 = add(dWU1, a_wu1, "wu1")
    dX2, dsh3, dsc3, dng2 = _norm_mod_bwd(X2, dH3, dX3, ng2, mod2, 2, S, nx, nx, "ffn2_dnorm")

    dZb, dg2 = _res_bwd(dX2, Z, mod2, 1, 1.0, S, nx, "mix_dres")
    dmix = _dproj_out(dZb, WOUT, "dproj_out")
    dWOUT = _mm_tn(mixb, pl.BlockSpec((_tm(S), D // ND), lambda d, r: (r, d)), dZb, pl.BlockSpec((_tm(S), D), lambda d, r: (r, 0)),
                   (ND, D // ND, D), pl.BlockSpec((None, D // ND, D), lambda d, r: (d, 0, 0)), S, "dw_out")
    (dOb, dhs, dP), ((a_wout,),) = _merge_bwd(dmix, P, hf, hb, O, S, D, col_lx + 2, "merge_bwd", jobs=[sib(dWOUT)])
    T_wout, own_wout = add(dWOUT, a_wout, "w_out")
    gf, gb = _lru_scan_bwd(af, ab, dhs, S, "lru_scan_bwd")
    dxc, dWA, dWX, dba, dbx, dlam = _lru_gates_bwd(xc, hfp, hbp, gf, gb, WA, WX, ba, bx, lam, "lru_gates_bwd")
    dLW = jnp.stack([dWA, dWX]).reshape(ND, -1, HD)
    (dP, dconv_w, dconv_b), ((a_lw,),) = _conv_bwd(dxc, P, conv_w, dP, S, D, col_lx, "conv_bwd", jobs=[sib(dLW)])
    T_lw, own_lw = add(dLW, a_lw, "lru_w")
    (dq, dk, dv), ((b_wd1,), (b_wg1,), (b_wu1,), (b_wout,), (b_lw,)) = _attn_bwd(
        q, k, v, O, LSE, dOb, S, G, "attn_bwd",
        jobs=[_rs_chips_job(T_wd1), _rs_chips_job(T_wg1), _rs_chips_job(T_wu1), _rs_chips_job(T_wout), _rs_chips_job(T_lw)])
    fin_wd = _rs_finish(own_wd1, b_wd1, "wd1_rs_finish", opt['ffn_wd'], 1, 2)
    fin_wg = _rs_finish(own_wg1, b_wg1, "wg1_rs_finish", opt['ffn_wg'], 1, 2)
    fin_wu = _rs_finish(own_wu1, b_wu1, "wu1_rs_finish", opt['ffn_wu'], 1, 2)
    out['w_out'] = _rs_finish(own_wout, b_wout, "w_out_rs_finish", opt['w_out'])
    (lw_sum,) = _rs_finish(own_lw, b_lw, "lru_w_rs_finish")
    dP, dqg, dkg = _qkv_bwd(P, dq, dk, dv, qg, kg, COS, SIN, dP, D, KVW, nx, "qkv_bwd")
    dH2, ((lw_full,),) = _dproj_in(dP, WIN, "dproj_in", jobs=[_ag_job(lw_sum)])
    dWIN = _mm_tn(H2, pl.BlockSpec((_tm(R), D), lambda d, r: (r, 0)), dP, pl.BlockSpec((_tm(R), Nb), lambda d, r: (r, d)),
                  (ND, D, Nb), pl.BlockSpec((None, D, Nb), lambda d, r: (d, 0, 0)), R, "dw_in")
    dX1, dsh2, dsc2, dng1 = _norm_mod_bwd(X1, dH2, dX2, ng1, mod2, 1, R, nx, nx, "mix_dnorm")

    dY1b, dg1 = _res_bwd(dX1, Y1, mod2, 0, 0.5, R, nx, "ffn1_dres")
    (dG1, dU1), ((a_win,),) = _ffn_dact(dY1b, WD0, 0, G1, U1, R, "ffn1_dact", jobs=[sib(dWIN)])
    T_win, own_win = add(dWIN, a_win, "w_in")
    blk, row = tn_specs(R)
    dWD0, ((b_win01,),) = _mm_tn(A1, blk, dY1b, row, (ND, 1, Fb, D), wd_spec, R, "ffn1_dwd", jobs=[_rs_chips_job(T_win, (0, 1))])
    dWG0, ((b_win2,), (a_wd0,)) = _mm_tn(H1, row, dG1, blk, (ND, 1, D, Fb), wg_spec, R, "ffn1_dwg",
                                         jobs=[_rs_chips_job(T_win, (2,)), sib(dWD0)])
    out['w_in'] = _rs_finish(own_win, [(b_win01, 0), (b_win01, 1), (b_win2, 0)], "w_in_rs_finish", opt['w_in'])
    T_wd0, own_wd0 = add(dWD0, a_wd0, "wd0")
    dWU0, ((a_wg0,), (b_wd0,)) = _mm_tn(H1, row, dU1, blk, (ND, 1, D, Fb), wg_spec, R, "ffn1_dwu",
                                        jobs=[sib(dWG0), _rs_chips_job(T_wd0)])
    T_wg0, own_wg0 = add(dWG0, a_wg0, "wg0")
    out['ffn_wd'] = _rs_finish(own_wd0, b_wd0, "wd0_rs_finish", opt['ffn_wd'], 0, 2, fin_wd)
    dH1, ((a_wu0,), (b_wg0,)) = _ffn_dh(dG1, dU1, WG0, WU0, R, "ffn1_dh", jobs=[sib(dWU0), _rs_chips_job(T_wg0)])
    T_wu0, own_wu0 = add(dWU0, a_wu0, "wu0")
    out['ffn_wg'] = _rs_finish(own_wg0, b_wg0, "wg0_rs_finish", opt['ffn_wg'], 0, 2, fin_wg)
    (dX0, dsh1, dsc1, dng0), ((b_wu0,),) = _norm_mod_bwd(
        X0, dH1, dX1, ng0, mod2, 0, R, nx, R // TR, "ffn1_dnorm", jobs=[_rs_chips_job(T_wu0)])
    out['ffn_wu'] = _rs_finish(own_wu0, b_wu0, "wu0_rs_finish", opt['ffn_wu'], 0, 2, fin_wu)
    out['lru_w'] = lw_full

    zero = jnp.zeros((1, D), F32)
    dmodx = jnp.concatenate([dsh1[0], dsc1[0], dg1[0], dsh2[0], dsc2[0], dg2[0], dsh3[0], dsc3[0], dg3[0]], axis=0)
    dmodc = jnp.concatenate([dsh1[1], dsc1[1], dg1[1], dsh2[1], dsc2[1], zero, zero, zero, zero], axis=0)
    out.update(loss=loss, grad_x=dX0[:S], dmodx=dmodx, dmodc=dmodc, norm_g=jnp.concatenate([dng0, dng1, dng2], axis=0),
               q_norm_g=dqg, k_norm_g=dkg, conv_w=dconv_w, conv_b=dconv_b, lru_ba=dba, lru_bx=dbx,
               lru_lambda=dlam, final_norm_g=dfg)
    return out


def _mesh_pos():
    return lax.axis_index("x"), lax.axis_index("y"), lax.axis_index("c")


def _all_gather(xb, name, in_vmem=False):
    space = pltpu.VMEM if in_vmem else pl.ANY

    def body(x_ref, out_ref, send_sems, recv_sems, local_sem):
        x, y, c = _mesh_pos()
        me, sibling = (x, y, c), (x, y, 1 - c)
        chips = [(1 - x, y), (x, 1 - y), (1 - x, 1 - y)]

        def slot(px, py, pc):
            return out_ref.at[4 * px + 2 * py + pc]

        def copy(k, block, to, src=None):
            return pltpu.make_async_remote_copy(
                src_ref=slot(*block) if src is None else src, dst_ref=slot(*block),
                send_sem=send_sems.at[k], recv_sem=recv_sems.at[k], device_id=to, device_id_type=MESH)

        mine = pltpu.make_async_copy(x_ref, slot(*me), local_sem)
        mine.start()
        first = [copy(0, me, sibling, src=x_ref)]
        first += [copy(1 + j, me, (*chip, c), src=x_ref) for j, chip in enumerate(chips)]
        for cp in first:
            cp.start()
        passed = [copy(4 + j, (*chip, c), sibling) for j, chip in enumerate(chips)]
        for j, chip in enumerate(chips):
            copy(1 + j, (*chip, c), me).wait_recv()
            passed[j].start()
        copy(0, sibling, me).wait_recv()
        for j, chip in enumerate(chips):
            copy(4 + j, (*chip, 1 - c), me).wait_recv()
        for cp in first + passed:
            cp.wait_send()
        mine.wait()

    return pl.pallas_call(
        body, name=name, out_shape=_sds((ND,) + xb.shape, xb.dtype),
        in_specs=[pl.BlockSpec(memory_space=space)], out_specs=pl.BlockSpec(memory_space=space),
        scratch_shapes=[pltpu.SemaphoreType.DMA((7,)), pltpu.SemaphoreType.DMA((7,)), pltpu.SemaphoreType.DMA(())])(xb)


def _ag_job(xb):
    def parts(ins, outs, sems, starting=False):
        x_ref, out_ref = ins[0], outs[0]
        send_sems, recv_sems, local_sem = sems
        x, y, c = _mesh_pos()
        me, sibling = (x, y, c), (x, y, 1 - c)
        chips = [(1 - x, y), (x, 1 - y), (1 - x, 1 - y)]

        def slot(px, py, pc):
            return out_ref.at[4 * px + 2 * py + pc]

        def copy(k, block, to, src=None):
            return pltpu.make_async_remote_copy(
                src_ref=slot(*block) if src is None else src, dst_ref=slot(*block),
                send_sem=send_sems.at[k], recv_sem=recv_sems.at[k], device_id=to, device_id_type=MESH)

        mine = pltpu.make_async_copy(x_ref, slot(*me), local_sem)
        first = [copy(0, me, sibling, src=x_ref)] + [copy(1 + j, me, (*chip, c), src=x_ref) for j, chip in enumerate(chips)]
        if starting:
            return mine, first
        passed = [copy(4 + j, (*chip, c), sibling) for j, chip in enumerate(chips)]
        landed = [copy(1 + j, (*chip, c), me) for j, chip in enumerate(chips)]
        from_sibling = [copy(0, sibling, me)] + [copy(4 + j, (*chip, 1 - c), me) for j, chip in enumerate(chips)]
        return mine, first, passed, landed, from_sibling

    def start(ins, outs, sems):
        mine, first = parts(ins, outs, sems, starting=True)
        mine.start()
        for cp in first:
            cp.start()

    def finish(ins, outs, sems):
        mine, first, passed, landed, from_sibling = parts(ins, outs, sems)
        for j in range(3):
            landed[j].wait_recv()
            passed[j].start()
        for cp in from_sibling:
            cp.wait_recv()
        for cp in first + passed:
            cp.wait_send()
        mine.wait()

    return _Job([xb], [_sds((ND,) + xb.shape, xb.dtype)],
                [pltpu.SemaphoreType.DMA((7,)), pltpu.SemaphoreType.DMA((7,)), pltpu.SemaphoreType.DMA(())], start, finish)


def _rs_sibling_job(Gp):
    def copies(ins, outs, sems):
        x, y, c = _mesh_pos()
        return [pltpu.make_async_remote_copy(
            src_ref=ins[0].at[2 * k + (1 - c)], dst_ref=outs[0].at[k], send_sem=sems[0].at[k], recv_sem=sems[1].at[k],
            device_id=(x, y, 1 - c), device_id_type=MESH) for k in range(4)]

    def start(ins, outs, sems):
        for cp in copies(ins, outs, sems):
            cp.start()

    def finish(ins, outs, sems):
        cps = copies(ins, outs, sems)
        for cp in cps:
            cp.wait_recv()
        for cp in cps:
            cp.wait_send()

    return _Job([Gp], [_sds((4,) + Gp.shape[1:], Gp.dtype)],
                [pltpu.SemaphoreType.DMA((4,)), pltpu.SemaphoreType.DMA((4,))], start, finish)


def _rs_chips_job(T, dests=(0, 1, 2)):
    def copies(ins, outs, sems):
        x, y, c = _mesh_pos()
        chips = [(1 - x, y), (x, 1 - y), (1 - x, 1 - y)]
        cps = []
        for i, j in enumerate(dests):
            px, py = chips[j]
            cps.append(pltpu.make_async_remote_copy(
                src_ref=ins[0].at[2 * px + py], dst_ref=outs[0].at[i], send_sem=sems[0].at[i], recv_sem=sems[1].at[i],
                device_id=(px, py, c), device_id_type=MESH))
        return cps

    def start(ins, outs, sems):
        for cp in copies(ins, outs, sems):
            cp.start()

    def finish(ins, outs, sems):
        cps = copies(ins, outs, sems)
        for cp in cps:
            cp.wait_recv()
        for cp in cps:
            cp.wait_send()

    n = len(dests)
    return _Job([T], [_sds((n,) + T.shape[1:], T.dtype)],
                [pltpu.SemaphoreType.DMA((n,)), pltpu.SemaphoreType.DMA((n,))], start, finish)


def _tile_rows(rows, cols):
    best = None
    for t in range(16, rows + 1, 16):
        if rows % t == 0 and t * cols * 4 <= (1 << 20):
            best = t
    return best if best is not None else rows


def _prefetch_call(body, *, name, grid, in_specs, out_specs, out_shape):
    return pl.pallas_call(
        body, name=name, out_shape=out_shape,
        grid_spec=pltpu.PrefetchScalarGridSpec(num_scalar_prefetch=1, grid=grid, in_specs=in_specs, out_specs=out_specs),
        compiler_params=pltpu.CompilerParams(dimension_semantics=("arbitrary",) * len(grid), vmem_limit_bytes=VMEM_LIMIT))


def _rs_add(Gp, bufA, idx, name):
    rows, cols = Gp.shape[1:]
    tr = _tile_rows(rows, cols)

    def body(i_ref, g_ref, a_ref, t_ref, own_ref):
        t = g_ref[...] + a_ref[...]
        t_ref[...] = t.astype(t_ref.dtype)

        @pl.when(pl.program_id(1) == i_ref[1])
        def _():
            own_ref[...] = t

    return _prefetch_call(
        body, name=name, grid=(rows // tr, 4),
        in_specs=[pl.BlockSpec((None, tr, cols), lambda r, k, i_ref: (2 * k + i_ref[0], r, 0)),
                  pl.BlockSpec((None, tr, cols), lambda r, k, i_ref: (k, r, 0))],
        out_specs=[pl.BlockSpec((None, tr, cols), lambda r, k, i_ref: (k, r, 0)),
                   pl.BlockSpec((tr, cols), lambda r, k, i_ref: (r, 0))],
        out_shape=[_sds((4, rows, cols), jnp.bfloat16), _sds((rows, cols))])(idx, Gp, bufA)


def _adam(w, g, m, v):
    m = ADAM_B1 * m + (1.0 - ADAM_B1) * g
    v = ADAM_B2 * v + (1.0 - ADAM_B2) * (g * g)
    m_hat = m / (1.0 - ADAM_B1 ** ADAM_STEP)
    v_hat = v / (1.0 - ADAM_B2 ** ADAM_STEP)
    return -ADAM_LR * (m_hat / (jnp.sqrt(v_hat) + ADAM_EPS) + ADAM_WD * w), m, v


def _rs_finish(Town, bufB, name, wmv=None, slab=0, n_slabs=1, prev=None):
    rows, cols = Town.shape
    tr = _tile_rows(rows, cols)
    nr = rows // tr
    n_in = 4 + (3 if wmv is not None else 0)
    n_out = 4 if wmv is not None else 1

    def body(*refs):
        ins, outs = refs[:n_in], refs[len(refs) - n_out:]
        g = ((ins[0][...] + ins[1][...].astype(F32)) + ins[2][...].astype(F32)) + ins[3][...].astype(F32)
        outs[0][...] = g
        if wmv is not None:
            d, m, v = _adam(ins[4][...], g, ins[5][...], ins[6][...])
            outs[1][...] = d
            outs[2][...] = m
            outs[3][...] = v

    plain = pl.BlockSpec((tr, cols), lambda r: (r, 0))
    slabbed = pl.BlockSpec((tr, cols), lambda r: (slab * nr + r, 0))
    pairs = bufB if isinstance(bufB, list) else [(bufB, j) for j in range(3)]
    in_specs = [plain] + [pl.BlockSpec((None, tr, cols), (lambda j: lambda r: (j, r, 0))(j)) for _, j in pairs]
    args = [Town] + [a for a, _ in pairs]
    if wmv is not None:
        in_specs += [slabbed] * 3
        args += list(wmv)
    aliases = None
    if prev is not None:
        in_specs += [pl.BlockSpec(memory_space=pl.ANY)] * n_out
        aliases = {len(args) + i: i for i in range(n_out)}
        args += list(prev)
    return _call(body, name=name, grid=(nr,), in_specs=in_specs, out_specs=[slabbed] * n_out,
                 out_shape=[_sds((n_slabs * rows, cols))] * n_out, aliases=aliases)(*args)


def _adamw_plain(w, g, m, v, name):
    rows, cols = w.shape
    tr = _tile_rows(rows, cols)

    def body(w_ref, g_ref, m_ref, v_ref, d_ref, mo_ref, vo_ref):
        d, m_, v_ = _adam(w_ref[...], g_ref[...], m_ref[...], v_ref[...])
        d_ref[...] = d
        mo_ref[...] = m_
        vo_ref[...] = v_

    spec = pl.BlockSpec((tr, cols), lambda r: (r, 0))
    return _call(body, name=name, grid=(rows // tr,), in_specs=[spec] * 4, out_specs=[spec] * 3,
                 out_shape=[_sds((rows, cols))] * 3)(w, g, m, v)


_MOD_TK = 512


def _mod_fwd(cc16, w_loc, b_loc, name):
    D, cols = w_loc.shape
    tk = min(_MOD_TK, D)
    nk = D // tk

    def body(c_ref, w_ref, b_ref, o_ref):
        kk = pl.program_id(0)

        @pl.when(kk == 0)
        def _():
            o_ref[...] = jnp.zeros_like(o_ref)

        cc = c_ref[...]
        o_ref[...] += _dot(cc * _sigmoid(cc), w_ref[...])

        @pl.when(kk == nk - 1)
        def _():
            o_ref[...] += b_ref[...]

    return _call(body, name=name, grid=(nk,),
                 in_specs=[pl.BlockSpec((16, tk), lambda kk: (0, kk)), pl.BlockSpec((tk, cols), lambda kk: (kk, 0)),
                           _full_spec((1, cols))],
                 out_specs=_full_spec((16, cols)), out_shape=_sds((16, cols)))(cc16, w_loc, b_loc)


def _mod_bwd(dm_loc, cc16, w_loc, name):
    D, cols = w_loc.shape

    def body(dm_ref, c_ref, w_ref, gw_ref, ds_ref):
        rows = [dm_ref[b, 0:1, :] for b in range(ND)]
        ctx = dm_ref[0, 1:2, :]
        for b in range(1, ND):
            ctx = ctx + dm_ref[b, 1:2, :]
        dm16 = jnp.concatenate(rows + [ctx, jnp.zeros((7, cols), F32)], axis=0)
        cc = c_ref[...]
        gw_ref[...] = _dot_tn(cc * _sigmoid(cc), dm16)
        ds_ref[...] = _dot_nt(dm16, w_ref[...])

    tk = min(_MOD_TK, D)
    return _call(body, name=name, grid=(D // tk,),
                 in_specs=[_full_spec((ND, 8, cols)), pl.BlockSpec((16, tk), lambda kk: (0, kk)),
                           pl.BlockSpec((tk, cols), lambda kk: (kk, 0))],
                 out_specs=[pl.BlockSpec((tk, cols), lambda kk: (kk, 0)), pl.BlockSpec((16, tk), lambda kk: (0, kk))],
                 out_shape=[_sds((D, cols)), _sds((16, D))])(dm_loc, cc16, w_loc)


def _bmod_grad(dm_all, name):
    n = dm_all.shape[-1]

    def body(dm_ref, o_ref):
        acc = dm_ref[0, 0:1, :] + dm_ref[0, 1:2, :]
        for b in range(1, ND):
            acc = (acc + dm_ref[b, 0:1, :]) + dm_ref[b, 1:2, :]
        o_ref[...] = acc

    return _call(body, name=name, grid=(1,), in_specs=[_full_spec((ND, 8, n))], out_specs=_full_spec((1, n)),
                 out_shape=_sds((1, n)))(dm_all)


_SMALL_ROWS = 24
_ROW_CCTX = 15


def _small_finish(parts, c_ctx, name):
    D = parts.shape[-1]

    def body(p_ref, c_ref, o_ref):
        acc = p_ref[0]
        for b in range(1, ND):
            acc = acc + p_ref[b]
        cc = c_ref[...]
        sg = _sigmoid(cc)
        dsilu = sg * (1.0 + cc * (1.0 - sg))
        row = lax.broadcasted_iota(jnp.int32, acc.shape, 0)
        o_ref[...] = jnp.where(row == _ROW_CCTX, acc * dsilu, acc)

    return _call(body, name=name, grid=(1,), in_specs=[_full_spec(parts.shape), _full_spec((1, D))],
                 out_specs=_full_spec((_SMALL_ROWS, D)), out_shape=_sds((_SMALL_ROWS, D)))(parts, c_ctx)


_WEIGHTS = ['c_ctx', 'w_mod', 'b_mod', 'norm_g', 'ffn_wg', 'ffn_wu', 'ffn_wd', 'w_in', 'w_out', 'q_norm_g', 'k_norm_g',
            'conv_w', 'conv_b', 'lru_wa', 'lru_ba', 'lru_wx', 'lru_bx', 'lru_lambda', 'final_norm_g']
_SMALL = ['c_ctx', 'b_mod', 'norm_g', 'q_norm_g', 'k_norm_g', 'conv_w', 'conv_b', 'lru_ba', 'lru_bx', 'lru_lambda',
          'final_norm_g']


def _pad_rows(a, rows):
    return jnp.pad(a, ((0, rows - a.shape[0]),) + ((0, 0),) * (a.ndim - 1))


def _step(w, m, v, x, c, ctx, loss_target):
    xi, yi, ci = _mesh_pos()
    me = 4 * xi + 2 * yi + ci
    idx = jnp.stack([ci, 2 * xi + yi]).astype(jnp.int32)
    S, D = x.shape[1:]
    Ds = D // ND
    cols = w['w_mod'].shape[-1]

    sp = jnp.concatenate([w['norm_g'][0], w['conv_w'][0], w['lru_ba'][0], w['lru_bx'][0], w['lru_lambda'][0]], axis=0)
    spg = _all_gather(_pad_rows(sp, 16), "ag_small_params", in_vmem=True)
    spf = jnp.transpose(spg, (1, 0, 2)).reshape(16, D)
    ng, conv_w, ba, bx, lam = spf[0:3], spf[3:7], spf[7:9], spf[9:11], spf[11:13]

    cg = _all_gather(_pad_rows(c, 8), "ag_cond", in_vmem=True)
    cc16 = _pad_rows(jnp.concatenate([cg[:, 0, :], w['c_ctx'][None, :]], axis=0), 16)
    b_loc = lax.dynamic_slice_in_dim(w['b_mod'], me * cols, cols, axis=1)
    mod_loc = _mod_fwd(cc16, w['w_mod'][0], b_loc, "mod_fwd")
    modg = _all_gather(mod_loc, "ag_mod", in_vmem=True)
    mod16 = jnp.transpose(modg, (1, 0, 2)).reshape(16, ND * cols)
    modx = lax.dynamic_index_in_dim(mod16, me, axis=0, keepdims=False).reshape(9, D)
    modc = mod16[8].reshape(9, D)

    shards = {'w_in': w['w_in'][0].astype(_MXU), 'w_out': w['w_out'][0].astype(_MXU)}
    for layer in range(2):
        shards[f'wg{layer}'] = w['ffn_wg'][0, layer].astype(_MXU)[None]
        shards[f'wu{layer}'] = w['ffn_wu'][0, layer].astype(_MXU)[None]
        shards[f'wd{layer}'] = w['ffn_wd'][0, layer].astype(_MXU)[None]
    WA, WX = w['lru_wa'][0].astype(_MXU), w['lru_wx'][0].astype(_MXU)
    big = ('ffn_wg', 'ffn_wu', 'ffn_wd', 'w_in', 'w_out')
    opt = {n: tuple(a[n].reshape(-1, a[n].shape[-1]) for a in (w, m, v)) for n in big}

    g = _local_step(x[0], ctx[0], loss_target[0], modx, modc, ng, shards, w['q_norm_g'], w['k_norm_g'],
                    conv_w, w['conv_b'], WA, WX, ba, bx, lam, w['final_norm_g'][None, :], idx, opt)

    grad, delta, new_m, new_v = {}, {}, {}, {}
    for n in big:
        grad[n], delta[n], new_m[n], new_v[n] = [o.reshape(w[n].shape) for o in g[n]]

    lfull = g['lru_w'].reshape((2,) + w['lru_wa'].shape[1:])
    for i, n in enumerate(('lru_wa', 'lru_wx')):
        shard = w[n].shape
        view = lambda a: a.reshape(-1, HD)
        grad[n] = lfull[i].reshape(shard)
        outs = _adamw_plain(view(w[n]), view(lfull[i]), view(m[n]), view(v[n]), f"adamw_{n}")
        delta[n], new_m[n], new_v[n] = [o.reshape(shard) for o in outs]

    dm = _pad_rows(jnp.stack([g['dmodx'].reshape(-1), g['dmodc'].reshape(-1)]), 8)
    dm_all = _all_gather(dm, "ag_dmod", in_vmem=True)
    dm_loc = lax.dynamic_slice_in_dim(dm_all, me * cols, cols, axis=2)
    gw_mod, dsil = _mod_bwd(dm_loc, cc16, w['w_mod'][0], "mod_bwd")
    grad['w_mod'] = gw_mod[None]
    outs = _adamw_plain(w['w_mod'][0], gw_mod, m['w_mod'][0], v['w_mod'][0], "adamw_w_mod")
    delta['w_mod'], new_m['w_mod'], new_v['w_mod'] = [o[None] for o in outs]
    grad['b_mod'] = _bmod_grad(dm_all, "bmod_grad")

    pad_d = lambda a: jnp.concatenate([a, jnp.zeros((1, D - a.shape[1]), F32)], axis=1)
    small = jnp.concatenate([g['norm_g'], g['conv_w'], g['conv_b'], g['lru_ba'], g['lru_bx'], g['lru_lambda'],
                             g['final_norm_g'], dsil[8:9], pad_d(g['q_norm_g']), pad_d(g['k_norm_g'])], axis=0)
    parts = _all_gather(_pad_rows(small, _SMALL_ROWS), "ag_small_grads", in_vmem=True)
    ssum = _small_finish(parts, w['c_ctx'][None, :], "small_finish")
    mine = lambda rows: lax.dynamic_slice_in_dim(rows, me * Ds, Ds, axis=1)
    grad['norm_g'] = mine(ssum[0:3])[None]
    grad['conv_w'] = mine(ssum[3:7])[None]
    grad['conv_b'] = ssum[7:8]
    grad['lru_ba'] = mine(ssum[8:10])[None]
    grad['lru_bx'] = mine(ssum[10:12])[None]
    grad['lru_lambda'] = mine(ssum[12:14])[None]
    grad['final_norm_g'] = ssum[14]
    grad['c_ctx'] = ssum[_ROW_CCTX]
    grad['q_norm_g'] = ssum[16:17, :HD]
    grad['k_norm_g'] = ssum[17:18, :HD]

    def pack(d):
        flat = jnp.concatenate([d[n].reshape(-1) for n in _SMALL])
        padded = -(-flat.shape[0] // 1024) * 1024
        return jnp.concatenate([flat, jnp.zeros((padded - flat.shape[0],), F32)]).reshape(-1, HD)

    outs = _adamw_plain(pack(w), pack(grad), pack(m), pack(v), "adamw_small")
    off = 0
    for n in _SMALL:
        size = math.prod(w[n].shape)
        for dst, o in zip((delta, new_m, new_v), outs):
            dst[n] = o.reshape(-1)[off:off + size].reshape(w[n].shape)
        off += size

    loss = lax.psum(g['loss'][0, 0], ("x", "y", "c"))
    return (loss, g['grad_x'][None], *[grad[n] for n in _WEIGHTS], *[delta[n] for n in _WEIGHTS],
            *[new_m[n] for n in _WEIGHTS], *[new_v[n] for n in _WEIGHTS])


def kernel(x, c, ctx, c_ctx, w_mod, b_mod, norm_g, ffn_wg, ffn_wu, ffn_wd, w_in, w_out, q_norm_g, k_norm_g, conv_w, conv_b, lru_wa, lru_ba, lru_wx, lru_bx, lru_lambda, final_norm_g, loss_target, m_c_ctx, m_w_mod, m_b_mod, m_norm_g, m_ffn_wg, m_ffn_wu, m_ffn_wd, m_w_in, m_w_out, m_q_norm_g, m_k_norm_g, m_conv_w, m_conv_b, m_lru_wa, m_lru_ba, m_lru_wx, m_lru_bx, m_lru_lambda, m_final_norm_g, v_c_ctx, v_w_mod, v_b_mod, v_norm_g, v_ffn_wg, v_ffn_wu, v_ffn_wd, v_w_in, v_w_out, v_q_norm_g, v_k_norm_g, v_conv_w, v_conv_b, v_lru_wa, v_lru_ba, v_lru_wx, v_lru_bx, v_lru_lambda, v_final_norm_g):
    given = dict(locals())
    w = {n: given[n] for n in _WEIGHTS}
    m = {n: given["m_" + n] for n in _WEIGHTS}
    v = {n: given["v_" + n] for n in _WEIGHTS}
    return _step(w, m, v, x, c, ctx, loss_target)
```

```python
import functools
import math

import jax
import jax.numpy as jnp
from jax import lax
from jax.experimental import pallas as pl
from jax.experimental.pallas import tpu as pltpu

F32 = jnp.float32
_MXU = jnp.bfloat16
ND = 8
TR = 256
HD = 128
EPS = 1e-6
GRID_W = 64
ROPE_THETA = 10000.0
LRU_C = 8.0
VMEM_LIMIT = 56 * 1024 * 1024
SCAN_W = 512
ADAM_LR, ADAM_B1, ADAM_B2, ADAM_EPS, ADAM_WD, ADAM_STEP = 0.001, 0.9, 0.999, 1e-08, 0.01, 10
MESH = pl.DeviceIdType.MESH


class _Job:
    def __init__(self, inputs, out_shapes, sems, start, finish):
        self.inputs, self.out_shapes, self.sems, self.start, self.finish = inputs, out_shapes, sems, start, finish


def _call(body, *, name, grid, in_specs, out_specs, out_shape, scratch=(), aliases=None, jobs=()):
    params = pltpu.CompilerParams(dimension_semantics=("arbitrary",) * len(grid), vmem_limit_bytes=VMEM_LIMIT)
    if not jobs:
        return pl.pallas_call(body, name=name, grid=grid, in_specs=in_specs, out_specs=out_specs, out_shape=out_shape,
                              scratch_shapes=scratch, input_output_aliases=aliases or {}, compiler_params=params)
    single = not isinstance(out_specs, (list, tuple))
    o_specs = [out_specs] if single else list(out_specs)
    o_shape = [out_shape] if single else list(out_shape)
    n_in, n_out, n_scr = len(in_specs), len(o_specs), len(scratch)
    j_in = [a for j in jobs for a in j.inputs]
    j_out = [s for j in jobs for s in j.out_shapes]
    j_sem = [s for j in jobs for s in j.sems]
    hbm = pl.BlockSpec(memory_space=pl.ANY)

    def wrapped(*refs):
        ins, rest = refs[:n_in], refs[n_in:]
        jin, rest = rest[:len(j_in)], rest[len(j_in):]
        outs, rest = rest[:n_out], rest[n_out:]
        jout, rest = rest[:len(j_out)], rest[len(j_out):]
        scr, jsem = rest[:n_scr], rest[n_scr:]
        first = functools.reduce(jnp.logical_and, [pl.program_id(a) == 0 for a in range(len(grid))])
        last = functools.reduce(jnp.logical_and, [pl.program_id(a) == grid[a] - 1 for a in range(len(grid))])

        def each(which):
            i = o = s = 0
            for j in jobs:
                ni, no, ns = len(j.inputs), len(j.out_shapes), len(j.sems)
                getattr(j, which)(jin[i:i + ni], jout[o:o + no], jsem[s:s + ns])
                i, o, s = i + ni, o + no, s + ns

        @pl.when(first)
        def _():
            each("start")

        body(*ins, *outs, *scr)

        @pl.when(last)
        def _():
            each("finish")

    call = pl.pallas_call(wrapped, name=name, grid=grid, in_specs=list(in_specs) + [hbm] * len(j_in),
                          out_specs=o_specs + [hbm] * len(j_out), out_shape=o_shape + j_out,
                          scratch_shapes=list(scratch) + j_sem, input_output_aliases=aliases or {}, compiler_params=params)

    def run(*args):
        res = call(*args, *j_in)
        comp = res[0] if single else list(res[:n_out])
        jres, o = [], n_out
        for j in jobs:
            jres.append(list(res[o:o + len(j.out_shapes)]))
            o += len(j.out_shapes)
        return comp, jres

    return run


def _comm_call(jobs, name):
    j_in = [a for j in jobs for a in j.inputs]
    j_out = [s for j in jobs for s in j.out_shapes]
    j_sem = [s for j in jobs for s in j.sems]
    hbm = pl.BlockSpec(memory_space=pl.ANY)

    def body(*refs):
        jin, jout, jsem = refs[:len(j_in)], refs[len(j_in):len(j_in) + len(j_out)], refs[len(j_in) + len(j_out):]
        for which in ("start", "finish"):
            i = o = s = 0
            for j in jobs:
                ni, no, ns = len(j.inputs), len(j.out_shapes), len(j.sems)
                getattr(j, which)(jin[i:i + ni], jout[o:o + no], jsem[s:s + ns])
                i, o, s = i + ni, o + no, s + ns

    res = pl.pallas_call(body, name=name, out_shape=j_out, in_specs=[hbm] * len(j_in), out_specs=[hbm] * len(j_out),
                         scratch_shapes=j_sem)(*j_in)
    jres, o = [], 0
    for j in jobs:
        jres.append(list(res[o:o + len(j.out_shapes)]))
        o += len(j.out_shapes)
    return jres


def _sds(shape, dtype=F32):
    return jax.ShapeDtypeStruct(tuple(shape), dtype)


def _dot(a, b):
    return jnp.dot(a.astype(_MXU), b.astype(_MXU), preferred_element_type=F32)


def _dot_nt(a, b):
    return lax.dot_general(a.astype(_MXU), b.astype(_MXU), (((1,), (1,)), ((), ())), preferred_element_type=F32)


def _dot_tn(a, b):
    return lax.dot_general(a.astype(_MXU), b.astype(_MXU), (((0,), (0,)), ((), ())), preferred_element_type=F32)


def _sigmoid(x):
    return 0.5 * jnp.tanh(0.5 * x) + 0.5


_GELU_C = math.sqrt(2.0 / math.pi)


def _gelu_and_grad(x):
    x2 = x * x
    t = jnp.tanh(_GELU_C * (x + 0.044715 * x * x2))
    ge = 0.5 * x * (1.0 + t)
    dge = 0.5 * (1.0 + t) + 0.5 * x * (1.0 - t * t) * (_GELU_C * (1.0 + 3.0 * 0.044715 * x2))
    return ge, dge


def _tm(rows):
    assert rows % 4 == 0 and (rows // 4) % 16 == 0
    return rows // 4


def _row_spec(width, nmax=None):
    if nmax is None:
        return pl.BlockSpec((TR, width), lambda i: (i, 0))
    return pl.BlockSpec((TR, width), lambda i: (jnp.minimum(i, nmax), 0))


def _full_spec(shape):
    n = len(shape)
    return pl.BlockSpec(tuple(shape), lambda *_: (0,) * n)


def _mod_spec(D, nx):
    return pl.BlockSpec((None, 9, D), lambda i: (i // nx, 0, 0))


def _norm_mod_fwd(X, ng, mod2, k, rows, nx, name):
    D = X.shape[1]

    def body(x_ref, g_ref, mod_ref, h_ref):
        x = x_ref[...]
        r = lax.rsqrt(jnp.mean(x * x, axis=-1, keepdims=True) + EPS)
        n = (x * r) * g_ref[...]
        h_ref[...] = (n * (1.0 + mod_ref[3 * k + 1:3 * k + 2, :]) + mod_ref[3 * k:3 * k + 1, :]).astype(h_ref.dtype)

    return _call(body, name=name, grid=(rows // TR,),
                 in_specs=[_row_spec(D), _full_spec((1, D)), _mod_spec(D, nx)],
                 out_specs=_row_spec(D), out_shape=_sds((rows, D), _MXU))(X, ng, mod2)


def _norm_mod_bwd(X, dH, dXres, ng, mod2, k, rows, nx, res_tiles, name, jobs=()):
    D = X.shape[1]
    ngroups = -(-(rows // TR) // nx)

    def body(x_ref, dh_ref, dres_ref, g_ref, mod_ref, dx_ref, dsh_ref, dsc_ref, dng_ref):
        i = pl.program_id(0)
        x = x_ref[...]
        dh = dh_ref[...]
        g = g_ref[...]
        r = lax.rsqrt(jnp.mean(x * x, axis=-1, keepdims=True) + EPS)
        xh = x * r
        n = xh * g
        dn_mod = dh * (1.0 + mod_ref[3 * k + 1:3 * k + 2, :])

        @pl.when(i % nx == 0)
        def _():
            dsh_ref[...] = jnp.zeros_like(dsh_ref)
            dsc_ref[...] = jnp.zeros_like(dsc_ref)

        @pl.when(i == 0)
        def _():
            dng_ref[...] = jnp.zeros_like(dng_ref)

        dsh_ref[...] += jnp.sum(dh, axis=0, keepdims=True)
        dsc_ref[...] += jnp.sum(dh * n, axis=0, keepdims=True)
        dng_ref[...] += jnp.sum(dn_mod * xh, axis=0, keepdims=True)
        dn = dn_mod * g
        dres = jnp.where(i < res_tiles, dres_ref[...], 0.0)
        dx_ref[...] = r * (dn - xh * jnp.mean(dn * xh, axis=-1, keepdims=True)) + dres

    grp = pl.BlockSpec((None, 1, D), lambda i: (i // nx, 0, 0))
    return _call(body, jobs=jobs, name=name, grid=(rows // TR,),
                 in_specs=[_row_spec(D), _row_spec(D), _row_spec(D, res_tiles - 1), _full_spec((1, D)), _mod_spec(D, nx)],
                 out_specs=[_row_spec(D), grp, grp, _full_spec((1, D))],
                 out_shape=[_sds((rows, D)), _sds((ngroups, 1, D)), _sds((ngroups, 1, D)), _sds((1, D))],
                 )(X, dH, dXres, ng, mod2)


def _res_bwd(dX, Y, mod2, k, coef, rows, nx, name):
    D = dX.shape[1]
    ngroups = -(-(rows // TR) // nx)

    def body(dx_ref, y_ref, mod_ref, dy_ref, dg_ref):
        i = pl.program_id(0)
        dx = dx_ref[...]

        @pl.when(i % nx == 0)
        def _():
            dg_ref[...] = jnp.zeros_like(dg_ref)

        dy_ref[...] = ((coef * mod_ref[3 * k + 2:3 * k + 3, :]) * dx).astype(dy_ref.dtype)
        dg_ref[...] += jnp.sum(coef * dx * y_ref[...], axis=0, keepdims=True)

    return _call(body, name=name, grid=(rows // TR,),
                 in_specs=[_row_spec(D), _row_spec(D), _mod_spec(D, nx)],
                 out_specs=[_row_spec(D), pl.BlockSpec((None, 1, D), lambda i: (i // nx, 0, 0))],
                 out_shape=[_sds((rows, D), _MXU), _sds((ngroups, 1, D))])(dX, Y, mod2)


def _final_loss(X3, fg, target, name):
    S, D = X3.shape

    def body(x_ref, g_ref, t_ref, loss_ref, dx_ref, dg_ref):
        i = pl.program_id(0)
        x = x_ref[...]
        g = g_ref[...]
        r = lax.rsqrt(jnp.mean(x * x, axis=-1, keepdims=True) + EPS)
        n = x * r
        err = n * g - t_ref[...]

        @pl.when(i == 0)
        def _():
            loss_ref[...] = jnp.zeros_like(loss_ref)
            dg_ref[...] = jnp.zeros_like(dg_ref)

        loss_ref[...] += 0.5 * jnp.sum(jnp.mean(err * err, axis=-1, keepdims=True), axis=0, keepdims=True)
        dy = err * (1.0 / D)
        dg_ref[...] += jnp.sum(dy * n, axis=0, keepdims=True)
        dn = dy * g
        dx_ref[...] = r * (dn - n * jnp.mean(dn * n, axis=-1, keepdims=True))

    return _call(body, name=name, grid=(S // TR,),
                 in_specs=[_row_spec(D), _full_spec((1, D)), _row_spec(D)],
                 out_specs=[_full_spec((1, 1)), _row_spec(D), _full_spec((1, D))],
                 out_shape=[_sds((1, 1)), _sds((S, D)), _sds((1, D))])(X3, fg, target)


def _ffn_gate(H, WG, rows, name, jobs=()):
    D = H.shape[1]
    Fb = WG.shape[-1]

    def body(h_ref, w_ref, g_ref):
        g_ref[...] = _dot(h_ref[...], w_ref[...])

    tm = _tm(rows)
    return _call(body, jobs=jobs, name=name, grid=(ND, rows // tm),
                 in_specs=[pl.BlockSpec((tm, D), lambda d, m: (m, 0)), pl.BlockSpec((None, None, D, Fb), lambda d, m: (d, 0, 0, 0))],
                 out_specs=pl.BlockSpec((None, tm, Fb), lambda d, m: (d, m, 0)), out_shape=_sds((ND, rows, Fb)))(H, WG)


def _ffn_up(H, WG, WU, rows, name, jobs=(), G=None):
    D = H.shape[1]
    Fb = WU.shape[-1]

    def body(h_ref, w_ref, x_ref, *outs):
        h = h_ref[...]
        g = x_ref[...] if G is not None else _dot(h, x_ref[...])
        u = _dot(h, w_ref[...])
        if G is None:
            outs[0][...] = g
        outs[-2][...] = u
        outs[-1][...] = ((g * _sigmoid(g)) * u).astype(outs[-1].dtype)

    tm = _tm(rows)
    blk = pl.BlockSpec((None, tm, Fb), lambda d, m: (d, m, 0))
    wspec = pl.BlockSpec((None, None, D, Fb), lambda d, m: (d, 0, 0, 0))
    f32o, bfo = _sds((ND, rows, Fb)), _sds((ND, rows, Fb), _MXU)
    return _call(body, jobs=jobs, name=name, grid=(ND, rows // tm),
                 in_specs=[pl.BlockSpec((tm, D), lambda d, m: (m, 0)), wspec, blk if G is not None else wspec],
                 out_specs=[blk, blk] if G is not None else [blk, blk, blk],
                 out_shape=[f32o, bfo] if G is not None else [f32o, f32o, bfo])(H, WU, G if G is not None else WG)


def _ffn_down(A, WD, layer, X, mod2, k, rows, S, name, jobs=()):
    Fb, D = WD.shape[-2:]
    tm = _tm(rows) // 2

    def body(a_ref, w_ref, x_ref, mod_ref, y_ref, xn_ref, acc_ref):
        d = pl.program_id(1)

        @pl.when(d == 0)
        def _():
            acc_ref[...] = jnp.zeros_like(acc_ref)

        acc_ref[...] += _dot(a_ref[...], w_ref[...])

        @pl.when(d == ND - 1)
        def _():
            y = acc_ref[...]
            y_ref[...] = y
            t = pl.program_id(0) * tm + lax.broadcasted_iota(jnp.int32, (tm, 1), 0)
            gate = jnp.where(t >= S, mod_ref[1, 3 * k + 2:3 * k + 3, :], mod_ref[0, 3 * k + 2:3 * k + 3, :])
            xn_ref[...] = x_ref[...] + (0.5 * gate) * y

    row = pl.BlockSpec((tm, D), lambda m, d: (m, 0))
    return _call(body, jobs=jobs, name=name, grid=(rows // tm, ND),
                 in_specs=[pl.BlockSpec((None, tm, Fb), lambda m, d: (d, m, 0)),
                           pl.BlockSpec((None, None, Fb, D), lambda m, d: (d, layer, 0, 0)),
                           row, pl.BlockSpec((2, 9, D), lambda m, d: (0, 0, 0))],
                 out_specs=[row, row], out_shape=[_sds((rows, D)), _sds((rows, D))],
                 scratch=[pltpu.VMEM((tm, D), F32)])(A, WD, X, mod2)


def _ffn_dact(dYb, WD, layer, G, U, rows, name, jobs=()):
    Fb, D = WD.shape[-2:]

    def body(dy_ref, w_ref, g_ref, u_ref, dg_ref, du_ref):
        da = _dot_nt(dy_ref[...], w_ref[...])
        g = g_ref[...]
        sg = _sigmoid(g)
        dg_ref[...] = (da * u_ref[...] * (sg * (1.0 + g * (1.0 - sg)))).astype(dg_ref.dtype)
        du_ref[...] = (da * (g * sg)).astype(du_ref.dtype)

    tm = _tm(rows)
    blk = pl.BlockSpec((None, tm, Fb), lambda d, m: (d, m, 0))
    return _call(body, jobs=jobs, name=name, grid=(ND, rows // tm),
                 in_specs=[pl.BlockSpec((tm, D), lambda d, m: (m, 0)),
                           pl.BlockSpec((None, None, Fb, D), lambda d, m: (d, layer, 0, 0)), blk, blk],
                 out_specs=[blk, blk],
                 out_shape=[_sds((ND, rows, Fb), _MXU), _sds((ND, rows, Fb), _MXU)])(dYb, WD, G, U)


def _ffn_dh(dG, dU, WG, WU, rows, name, jobs=()):
    D, Fb = WG.shape[-2:]

    def body(dg_ref, du_ref, wg_ref, wu_ref, dh_ref, acc_ref):
        d = pl.program_id(1)

        @pl.when(d == 0)
        def _():
            acc_ref[...] = jnp.zeros_like(acc_ref)

        acc_ref[...] += _dot_nt(dg_ref[...], wg_ref[...]) + _dot_nt(du_ref[...], wu_ref[...])

        @pl.when(d == ND - 1)
        def _():
            dh_ref[...] = acc_ref[...]

    tm = _tm(rows)
    blk = pl.BlockSpec((None, tm, Fb), lambda m, d: (d, m, 0))
    wspec = pl.BlockSpec((None, None, D, Fb), lambda m, d: (d, 0, 0, 0))
    return _call(body, jobs=jobs, name=name, grid=(rows // tm, ND), in_specs=[blk, blk, wspec, wspec],
                 out_specs=pl.BlockSpec((tm, D), lambda m, d: (m, 0)), out_shape=_sds((rows, D)),
                 scratch=[pltpu.VMEM((tm, D), F32)])(dG, dU, WG, WU)


def _mm_tn(A, a_spec, B, b_spec, out_shape, out_spec, rows, name, prev=None, jobs=()):
    def body(*refs):
        a_ref, b_ref, o_ref = refs[0], refs[1], refs[-1]

        @pl.when(pl.program_id(1) == 0)
        def _():
            o_ref[...] = jnp.zeros_like(o_ref)

        o_ref[...] += _dot_tn(a_ref[...], b_ref[...])

    in_specs = [a_spec, b_spec]
    args = [A, B]
    aliases = None
    if prev is not None:
        in_specs.append(pl.BlockSpec(memory_space=pl.ANY))
        args.append(prev)
        aliases = {2: 0}
    return _call(body, jobs=jobs, name=name, grid=(ND, rows // _tm(rows)), in_specs=in_specs, out_specs=out_spec,
                 out_shape=_sds(out_shape), aliases=aliases)(*args)


def _proj_in(H2, WIN, name, jobs=()):
    R, D = H2.shape
    Nb = WIN.shape[-1]

    def body(h_ref, w_ref, p_ref):
        p_ref[...] = _dot(h_ref[...], w_ref[...])

    tm = _tm(R)
    return _call(body, jobs=jobs, name=name, grid=(ND, R // tm),
                 in_specs=[pl.BlockSpec((tm, D), lambda d, m: (m, 0)), pl.BlockSpec((None, D, Nb), lambda d, m: (d, 0, 0))],
                 out_specs=pl.BlockSpec((tm, Nb), lambda d, m: (m, d)), out_shape=_sds((R, ND * Nb)))(H2, WIN)


def _dproj_in(dP, WIN, name, jobs=()):
    R = dP.shape[0]
    D, Nb = WIN.shape[-2:]

    def body(dp_ref, w_ref, dh_ref, acc_ref):
        d = pl.program_id(1)

        @pl.when(d == 0)
        def _():
            acc_ref[...] = jnp.zeros_like(acc_ref)

        acc_ref[...] += _dot_nt(dp_ref[...], w_ref[...])

        @pl.when(d == ND - 1)
        def _():
            dh_ref[...] = acc_ref[...]

    tm = _tm(R)
    return _call(body, jobs=jobs, name=name, grid=(R // tm, ND),
                 in_specs=[pl.BlockSpec((tm, Nb), lambda m, d: (m, d)), pl.BlockSpec((None, D, Nb), lambda m, d: (d, 0, 0))],
                 out_specs=pl.BlockSpec((tm, D), lambda m, d: (m, 0)), out_shape=_sds((R, D)),
                 scratch=[pltpu.VMEM((tm, D), F32)])(dP, WIN)


def _proj_out(mixb, WOUT, X1, mod2, S, name):
    D = WOUT.shape[0]

    def body(m_ref, w_ref, x_ref, mod_ref, z_ref, xn_ref):
        z = _dot(m_ref[...], w_ref[...])
        z_ref[...] = z
        xn_ref[...] = x_ref[...] + mod_ref[5:6, :] * z

    return _call(body, name=name, grid=(S // TR,),
                 in_specs=[_row_spec(D), _full_spec((D, D)), _row_spec(D), pl.BlockSpec((None, 9, D), lambda i: (0, 0, 0))],
                 out_specs=[_row_spec(D), _row_spec(D)], out_shape=[_sds((S, D)), _sds((S, D))])(mixb, WOUT, X1, mod2)


def _dproj_out(dZb, WOUT, name):
    S, D = dZb.shape

    def body(dz_ref, w_ref, dm_ref):
        dm_ref[...] = _dot_nt(dz_ref[...], w_ref[...])

    return _call(body, name=name, grid=(S // TR,), in_specs=[_row_spec(D), _full_spec((D, D))],
                 out_specs=_row_spec(D), out_shape=_sds((S, D)))(dZb, WOUT)


def _pair_swap(t):
    lane = lax.broadcasted_iota(jnp.int32, t.shape, 1)
    return jnp.where(lane % 2 == 0, pltpu.roll(t, HD - 1, 1), pltpu.roll(t, 1, 1))


SM_SCALE = HD ** -0.5


def _qkv_prep(P, qg, kg, COS, SIN, D, KVW, name):
    R = P.shape[0]
    W = D + 2 * KVW
    nq, nk = D // HD, KVW // HD

    def body(p_ref, qg_ref, kg_ref, cos_ref, sin_ref, q_ref, k_ref, v_ref):
        cos, sin = cos_ref[...], sin_ref[...]

        def head(t, g):
            y = (t * lax.rsqrt(jnp.mean(t * t, axis=-1, keepdims=True) + EPS)) * g
            return y * cos + _pair_swap(y) * sin

        for h in range(nq):
            q_ref[:, h * HD:(h + 1) * HD] = (head(p_ref[:, h * HD:(h + 1) * HD], qg_ref[...]) * SM_SCALE).astype(q_ref.dtype)
        for h in range(nk):
            k_ref[:, h * HD:(h + 1) * HD] = head(p_ref[:, D + h * HD:D + (h + 1) * HD], kg_ref[...]).astype(k_ref.dtype)
        v_ref[...] = p_ref[:, D + KVW:W].astype(v_ref.dtype)

    return _call(body, name=name, grid=(R // TR,),
                 in_specs=[_row_spec(W), _full_spec((1, HD)), _full_spec((1, HD)), _row_spec(HD), _row_spec(HD)],
                 out_specs=[_row_spec(D), _row_spec(KVW), _row_spec(KVW)],
                 out_shape=[_sds((R, D), _MXU), _sds((R, KVW), _MXU), _sds((R, KVW), _MXU)])(P, qg, kg, COS, SIN)


def _qkv_bwd(P, dq, dk, dv, qg, kg, COS, SIN, dP, D, KVW, nx, name):
    R, INW = P.shape
    W = D + 2 * KVW
    nq, nk = D // HD, KVW // HD

    def body(p_ref, dq_ref, dk_ref, dv_ref, qg_ref, kg_ref, cos_ref, sin_ref, dp_in, dp_ref, dqg_ref, dkg_ref):
        i = pl.program_id(0)
        cos, sin = cos_ref[...], sin_ref[...]

        @pl.when(i == 0)
        def _():
            dqg_ref[...] = jnp.zeros_like(dqg_ref)
            dkg_ref[...] = jnp.zeros_like(dkg_ref)

        def head_bwd(t, g, dout):
            r = lax.rsqrt(jnp.mean(t * t, axis=-1, keepdims=True) + EPS)
            n = t * r
            dy = dout * cos + _pair_swap(dout * sin)
            dn = dy * g
            return r * (dn - n * jnp.mean(dn * n, axis=-1, keepdims=True)), jnp.sum(dy * n, axis=0, keepdims=True)

        dqg = jnp.zeros((1, HD), F32)
        for h in range(nq):
            sl = slice(h * HD, (h + 1) * HD)
            dt, dg = head_bwd(p_ref[:, sl], qg_ref[...], jnp.where(i < nx, dq_ref[:, sl] * SM_SCALE, 0.0))
            dp_ref[:, sl] = dt.astype(dp_ref.dtype)
            dqg += dg
        dkg = jnp.zeros((1, HD), F32)
        for h in range(nk):
            sl = slice(h * HD, (h + 1) * HD)
            dt, dg = head_bwd(p_ref[:, D + h * HD:D + (h + 1) * HD], kg_ref[...], dk_ref[:, sl])
            dp_ref[:, D + h * HD:D + (h + 1) * HD] = dt.astype(dp_ref.dtype)
            dkg += dg
        dqg_ref[...] += dqg
        dkg_ref[...] += dkg
        dp_ref[:, D + KVW:W] = dv_ref[...].astype(dp_ref.dtype)

    return _call(body, name=name, grid=(R // TR,),
                 in_specs=[_row_spec(W), _row_spec(D, nx - 1), _row_spec(KVW), _row_spec(KVW), _full_spec((1, HD)),
                           _full_spec((1, HD)), _row_spec(HD), _row_spec(HD), pl.BlockSpec(memory_space=pl.ANY)],
                 out_specs=[_row_spec(W), _full_spec((1, HD)), _full_spec((1, HD))],
                 out_shape=[_sds((R, INW), _MXU), _sds((1, HD)), _sds((1, HD))],
                 aliases={8: 0})(P, dq, dk, dv, qg, kg, COS, SIN, dP)


def _stack_heads(ref, G, dtype=None):
    parts = [ref[:, g * HD:(g + 1) * HD] for g in range(G)]
    out = jnp.concatenate(parts, axis=0)
    return out if dtype is None else out.astype(dtype)


_KEY_CHUNKS = 4


def _attn_fwd(q, k, v, S, G, name, jobs=()):
    R, KVW = k.shape
    D = q.shape[1]
    Kh = KVW // HD
    tq = 128
    kc = R // _KEY_CHUNKS
    assert R % _KEY_CHUNKS == 0 and kc % 16 == 0

    def body(q_ref, k_ref, v_ref, o_ref, lse_ref):
        qs = _stack_heads(q_ref, G)
        m = l = acc = None
        for c in range(_KEY_CHUNKS):
            s = _dot_nt(qs, k_ref[c * kc:(c + 1) * kc, :])
            mc = jnp.max(s, axis=-1, keepdims=True)
            m_new = mc if c == 0 else jnp.maximum(m, mc)
            p = jnp.exp(s - m_new)
            ps = jnp.sum(p, axis=-1, keepdims=True)
            pv = _dot(p, v_ref[c * kc:(c + 1) * kc, :])
            if c == 0:
                l, acc = ps, pv
            else:
                alpha = jnp.exp(m - m_new)
                l = alpha * l + ps
                acc = alpha * acc + pv
            m = m_new
        o = acc / l
        lse = m + jnp.log(l)
        for g in range(G):
            o_ref[:, g * HD:(g + 1) * HD] = o[g * tq:(g + 1) * tq, :]
            lse_ref[g] = jnp.broadcast_to(lse[g * tq:(g + 1) * tq, :], (tq, HD))

    return _call(body, jobs=jobs, name=name, grid=(Kh, S // tq),
                 in_specs=[pl.BlockSpec((tq, G * HD), lambda h, i: (i, h)), pl.BlockSpec((R, HD), lambda h, i: (0, h)),
                           pl.BlockSpec((R, HD), lambda h, i: (0, h))],
                 out_specs=[pl.BlockSpec((tq, G * HD), lambda h, i: (i, h)),
                            pl.BlockSpec((None, G, tq, HD), lambda h, i: (h, 0, i, 0))],
                 out_shape=[_sds((S, D)), _sds((Kh, G, S, HD))])(q, k, v)


def _attn_bwd(q, k, v, O, LSE, dOb, S, G, name, jobs=()):
    R, KVW = k.shape
    D = q.shape[1]
    Kh = KVW // HD
    tq = 128
    kc = R // _KEY_CHUNKS

    def body(q_ref, k_ref, v_ref, o_ref, lse_ref, do_ref, dq_ref, dk_ref, dv_ref):
        @pl.when(pl.program_id(1) == 0)
        def _():
            dk_ref[...] = jnp.zeros_like(dk_ref)
            dv_ref[...] = jnp.zeros_like(dv_ref)

        qs = _stack_heads(q_ref, G)
        do = _stack_heads(do_ref, G)
        o = _stack_heads(o_ref, G)
        lse = jnp.concatenate([lse_ref[g][:, 0:1] for g in range(G)], axis=0)
        delta = jnp.sum(do.astype(F32) * o, axis=-1, keepdims=True)
        dq = None
        for c in range(_KEY_CHUNKS):
            rows = slice(c * kc, (c + 1) * kc)
            kk = k_ref[rows, :]
            p = jnp.exp(_dot_nt(qs, kk) - lse)
            dp = _dot_nt(do, v_ref[rows, :])
            ds = (p * (dp - delta)).astype(_MXU)
            dqc = _dot(ds, kk)
            dq = dqc if c == 0 else dq + dqc
            dk_ref[rows, :] += _dot_tn(ds, qs)
            dv_ref[rows, :] += _dot_tn(p, do)
        for g in range(G):
            dq_ref[:, g * HD:(g + 1) * HD] = dq[g * tq:(g + 1) * tq, :]

    qspec = pl.BlockSpec((tq, G * HD), lambda h, i: (i, h))
    kspec = pl.BlockSpec((R, HD), lambda h, i: (0, h))
    return _call(body, jobs=jobs, name=name, grid=(Kh, S // tq),
                 in_specs=[qspec, kspec, kspec, qspec, pl.BlockSpec((None, G, tq, HD), lambda h, i: (h, 0, i, 0)), qspec],
                 out_specs=[qspec, kspec, kspec],
                 out_shape=[_sds((S, D)), _sds((R, KVW)), _sds((R, KVW))])(q, k, v, O, LSE, dOb)


def _halo_specs(R, CB, col0):
    nt8 = TR // 8
    return [pl.BlockSpec((8, CB), lambda h, i: (jnp.maximum(i * nt8 - 1, 0), col0 + h)),
            pl.BlockSpec((TR, CB), lambda h, i: (i, col0 + h)),
            pl.BlockSpec((8, CB), lambda h, i: (jnp.minimum((i + 1) * nt8, R // 8 - 1), col0 + h))]


def _seq_pos(i, S, R, CB):
    t = i * TR - 8 + lax.broadcasted_iota(jnp.int32, (TR + 16, CB), 0)
    start = jnp.where(t >= S, S, 0)
    end = jnp.where(t >= S, R, S)
    return t - start, end - t


def _shift(cat, by):
    return pltpu.roll(cat, by % cat.shape[0], 0)


def _gate_mats(xcb, w_ref, dirn, nb):
    return jnp.concatenate([_dot(xcb[:, b * HD:(b + 1) * HD], w_ref[dirn, b]) for b in range(nb)], axis=1)


def _lru_gates_fwd(P, conv_w, conv_b, WA, WX, ba, bx, lam, S, D, col0, name):
    R = P.shape[0]
    CB = D // 2
    nb = CB // HD

    def body(xp_ref, x_ref, xn_ref, cw_ref, cb_ref, wa_ref, wx_ref, ba_ref, bx_ref, lam_ref,
             xc_ref, af_ref, uf_ref, ab_ref, ub_ref):
        i = pl.program_id(1)
        cat = jnp.concatenate([xp_ref[...], x_ref[...], xn_ref[...]], axis=0)
        from_start, to_end = _seq_pos(i, S, R, CB)
        conv = (cb_ref[...] + cw_ref[2:3, :] * cat
                + cw_ref[0:1, :] * jnp.where(from_start >= 2, _shift(cat, 2), 0.0)
                + cw_ref[1:2, :] * jnp.where(from_start >= 1, _shift(cat, 1), 0.0)
                + cw_ref[3:4, :] * jnp.where(to_end >= 2, _shift(cat, -1), 0.0))
        xc = conv[8:8 + TR, :]
        xc_ref[...] = xc
        xcb = xc.astype(_MXU)
        for dirn, (a_ref, u_ref) in enumerate(((af_ref, uf_ref), (ab_ref, ub_ref))):
            ra = _sigmoid(_gate_mats(xcb, wa_ref, dirn, nb) + ba_ref[dirn:dirn + 1, :])
            ia = _sigmoid(_gate_mats(xcb, wx_ref, dirn, nb) + bx_ref[dirn:dirn + 1, :])
            nl = -lam_ref[dirn:dirn + 1, :]
            sp = jnp.maximum(nl, 0.0) + jnp.log(1.0 + jnp.exp(-jnp.abs(nl)))
            la = (-LRU_C) * ra * sp
            a_ref[...] = jnp.exp(la)
            u_ref[...] = jnp.sqrt(1.0 - jnp.exp(2.0 * la)) * (ia * xc)

    def par(r):
        return pl.BlockSpec((r, CB), lambda h, i: (0, h))

    wspec = pl.BlockSpec((2, nb, HD, HD), lambda h, i: (0, h, 0, 0))
    out = pl.BlockSpec((TR, CB), lambda h, i: (i, h))
    return _call(body, name=name, grid=(2, R // TR),
                 in_specs=_halo_specs(R, CB, col0) + [par(4), par(1), wspec, wspec, par(2), par(2), par(2)],
                 out_specs=[out] * 5, out_shape=[_sds((R, D))] * 5,
                 )(P, P, P, conv_w, conv_b, WA, WX, ba, bx, lam)


def _lru_gates_bwd(xc, hf, hb, gf, gb, WA, WX, ba, bx, lam, S, name):
    R, D = xc.shape
    CB = D // 2
    nb = CB // HD

    def body(xc_ref, hfq_ref, hf_ref, hfl_ref, hb_ref, hbn_ref, gf_ref, gb_ref, wa_ref, wx_ref, ba_ref, bx_ref, lam_ref,
             dxc_ref, dwa_ref, dwx_ref, dba_ref, dbx_ref, dlam_ref):
        @pl.when(pl.program_id(1) == 0)
        def _():
            for r in (dwa_ref, dwx_ref, dba_ref, dbx_ref, dlam_ref):
                r[...] = jnp.zeros_like(r)

        t = pl.program_id(1) * TR + lax.broadcasted_iota(jnp.int32, (TR, CB), 0)
        hfp = _shift(jnp.concatenate([hfq_ref[...], hf_ref[...]], axis=0), 1)[8:8 + TR, :]
        hfp = jnp.where(t == 0, hfl_ref[7:8, :], jnp.where(t == S, 0.0, hfp))
        hbp = _shift(jnp.concatenate([hb_ref[...], hbn_ref[...]], axis=0), -1)[0:TR, :]
        hbp = jnp.where(t == R - 1, 0.0, hbp)

        xc = xc_ref[...]
        xcb = xc.astype(_MXU)
        dxc = jnp.zeros_like(xc)
        for dirn, (hp, g_ref) in enumerate(((hfp, gf_ref), (hbp, gb_ref))):
            ra = _sigmoid(_gate_mats(xcb, wa_ref, dirn, nb) + ba_ref[dirn:dirn + 1, :])
            ia = _sigmoid(_gate_mats(xcb, wx_ref, dirn, nb) + bx_ref[dirn:dirn + 1, :])
            nl = -lam_ref[dirn:dirn + 1, :]
            sp = jnp.maximum(nl, 0.0) + jnp.log(1.0 + jnp.exp(-jnp.abs(nl)))
            la = (-LRU_C) * ra * sp
            a = jnp.exp(la)
            e2 = jnp.exp(2.0 * la)
            s = jnp.sqrt(1.0 - e2)
            du = g_ref[...]
            dla = du * hp * a - du * (ia * xc) * (e2 / s)
            dxc += du * s * ia
            dza = (dla * (-LRU_C) * sp) * ra * (1.0 - ra)
            dzx = (du * s * xc) * ia * (1.0 - ia)
            dlam_ref[dirn:dirn + 1, :] += jnp.sum(dla * (LRU_C * ra) * _sigmoid(nl), axis=0, keepdims=True)
            dba_ref[dirn:dirn + 1, :] += jnp.sum(dza, axis=0, keepdims=True)
            dbx_ref[dirn:dirn + 1, :] += jnp.sum(dzx, axis=0, keepdims=True)
            dzab, dzxb = dza.astype(_MXU), dzx.astype(_MXU)
            parts = []
            for b in range(nb):
                sl = slice(b * HD, (b + 1) * HD)
                dwa_ref[dirn, b] += _dot_tn(xcb[:, sl], dzab[:, sl])
                dwx_ref[dirn, b] += _dot_tn(xcb[:, sl], dzxb[:, sl])
                parts.append(_dot_nt(dzab[:, sl], wa_ref[dirn, b]) + _dot_nt(dzxb[:, sl], wx_ref[dirn, b]))
            dxc += jnp.concatenate(parts, axis=1)
        dxc_ref[...] = dxc

    def par(r):
        return pl.BlockSpec((r, CB), lambda h, i: (0, h))

    wspec = pl.BlockSpec((2, nb, HD, HD), lambda h, i: (0, h, 0, 0))
    tile = pl.BlockSpec((TR, CB), lambda h, i: (i, h))
    before, _, after = _halo_specs(R, CB, 0)
    last = pl.BlockSpec((8, CB), lambda h, i: (R // 8 - 1, h))
    nbt = D // HD
    return _call(body, name=name, grid=(2, R // TR),
                 in_specs=[tile, before, tile, last, tile, after, tile, tile, wspec, wspec, par(2), par(2), par(2)],
                 out_specs=[tile, wspec, wspec, par(2), par(2), par(2)],
                 out_shape=[_sds((R, D)), _sds((2, nbt, HD, HD)), _sds((2, nbt, HD, HD)), _sds((2, D)), _sds((2, D)),
                            _sds((2, D))])(xc, hf, hf, hf, hb, hb, gf, gb, WA, WX, ba, bx, lam)


def _scan_rows(n_groups, step, init):
    return lax.fori_loop(0, n_groups, lambda gi, c: step(pl.multiple_of(gi * 8, 8), c), init)


def _lru_scan_fwd(af, uf, ab, ub, S, name):
    R, D = af.shape
    W = min(SCAN_W, D)
    nm, nx = R // TR, S // TR
    nc = nm - nx
    ng = TR // 8

    def body(af_ref, uf_ref, ab_ref, ub_ref, hf_ref, hb_ref, cf_ref, cb_ref):
        @pl.when(pl.program_id(1) == 0)
        def _():
            cf_ref[...] = jnp.zeros_like(cf_ref)
            cb_ref[...] = jnp.zeros_like(cb_ref)

        def step(base, carry):
            hf, hb = carry
            baseb = pl.multiple_of(TR - 8 - base, 8)
            for r in range(8):
                tf, tb = base + r, baseb + 7 - r
                hf = af_ref[pl.ds(tf, 1), :] * hf + uf_ref[pl.ds(tf, 1), :]
                hf_ref[pl.ds(tf, 1), :] = hf
                hb = ab_ref[pl.ds(tb, 1), :] * hb + ub_ref[pl.ds(tb, 1), :]
                hb_ref[pl.ds(tb, 1), :] = hb
            return hf, hb

        hf, hb = _scan_rows(ng, step, (cf_ref[0:1, :], cb_ref[0:1, :]))
        cf_ref[0:1, :] = hf
        cb_ref[0:1, :] = hb

    fmap = lambda j, s: (jnp.where(s < nc, nx + s, s - nc), j)
    bmap = lambda j, s: (nm - 1 - s, j)
    fs, bs = pl.BlockSpec((TR, W), fmap), pl.BlockSpec((TR, W), bmap)
    return _call(body, name=name, grid=(D // W, nm), in_specs=[fs, fs, bs, bs], out_specs=[fs, bs],
                 out_shape=[_sds((R, D))] * 2, scratch=[pltpu.VMEM((8, W), F32), pltpu.VMEM((8, W), F32)])(af, uf, ab, ub)


def _lru_scan_bwd(af, ab, dhs, S, name):
    R, D = af.shape
    W = min(SCAN_W, D)
    nm, nx = R // TR, S // TR
    ng = TR // 8

    def body(af_ref, dhf_ref, ab_ref, dhb_ref, gf_ref, gb_ref, cf_ref, cb_ref):
        @pl.when(pl.program_id(1) == 0)
        def _():
            cf_ref[...] = jnp.zeros_like(cf_ref)
            cb_ref[...] = jnp.zeros_like(cb_ref)

        def step(base, carry):
            cf, cb = carry
            based = pl.multiple_of(TR - 8 - base, 8)
            for r in range(8):
                tf, tb = based + 7 - r, base + r
                g = dhf_ref[pl.ds(tf, 1), :] + cf
                gf_ref[pl.ds(tf, 1), :] = g
                cf = af_ref[pl.ds(tf, 1), :] * g
                g = dhb_ref[pl.ds(tb, 1), :] + cb
                gb_ref[pl.ds(tb, 1), :] = g
                cb = ab_ref[pl.ds(tb, 1), :] * g
            return cf, cb

        cf, cb = _scan_rows(ng, step, (cf_ref[0:1, :], cb_ref[0:1, :]))
        cf_ref[0:1, :] = cf
        cb_ref[0:1, :] = cb

    fmap = lambda j, s: (jnp.where(s < nx, nx - 1 - s, nm - 1 - (s - nx)), j)
    bmap = lambda j, s: (s, j)
    fs, bs = pl.BlockSpec((TR, W), fmap), pl.BlockSpec((TR, W), bmap)
    return _call(body, name=name, grid=(D // W, nm), in_specs=[fs, fs, bs, bs], out_specs=[fs, bs],
                 out_shape=[_sds((R, D))] * 2, scratch=[pltpu.VMEM((8, W), F32), pltpu.VMEM((8, W), F32)])(af, dhs, ab, dhs)


def _merge_fwd(P, hf, hb, O, S, D, col_lg, name):
    R = P.shape[0]
    CB = D // 2
    nx = S // TR

    def body(lg_ref, ga_ref, gl_ref, hf_ref, hb_ref, o_ref, mix_ref):
        ge, _ = _gelu_and_grad(lg_ref[...])
        lru = (hf_ref[...] + hb_ref[...]) * ge
        mix_ref[...] = (_sigmoid(ga_ref[...]) * o_ref[...] + _sigmoid(gl_ref[...]) * lru).astype(mix_ref.dtype)

    def col(c0):
        return pl.BlockSpec((TR, CB), lambda h, i: (i, c0 + h))

    return _call(body, name=name, grid=(2, R // TR),
                 in_specs=[col(col_lg), col(col_lg + 2), col(col_lg + 4), col(0), col(0),
                           pl.BlockSpec((TR, CB), lambda h, i: (jnp.minimum(i, nx - 1), h))],
                 out_specs=col(0), out_shape=_sds((R, D), _MXU))(P, P, P, hf, hb, O)


def _merge_bwd(dmix, P, hf, hb, O, S, D, col_lg, name, jobs=()):
    R = P.shape[0]
    CB = D // 2
    nx = S // TR

    def body(dm_ref, lg_ref, ga_ref, gl_ref, hf_ref, hb_ref, o_ref, do_ref, dhs_ref, dp_ref, stash, sems):
        h, i = pl.program_id(0), pl.program_id(1)
        dm = jnp.where(i < nx, dm_ref[...], 0.0)
        sa, sl = _sigmoid(ga_ref[...]), _sigmoid(gl_ref[...])
        ge, dge = _gelu_and_grad(lg_ref[...])
        hs = hf_ref[...] + hb_ref[...]
        dl = dm * sl
        do_ref[...] = (dm * sa).astype(do_ref.dtype)
        dhs_ref[...] = dl * ge
        stash[0] = (dl * hs * dge).astype(stash.dtype)
        stash[1] = (dm * o_ref[...] * sa * (1.0 - sa)).astype(stash.dtype)
        stash[2] = (dm * (hs * ge) * sl * (1.0 - sl)).astype(stash.dtype)
        rows = pl.ds(pl.multiple_of(i * TR, TR), TR)
        copies = [pltpu.make_async_copy(stash.at[sec], dp_ref.at[rows, pl.ds(pl.multiple_of((col_lg + 2 * sec + h) * CB, CB), CB)],
                                        sems.at[sec]) for sec in range(3)]
        for cp in copies:
            cp.start()
        for cp in copies:
            cp.wait()

    def col(c0):
        return pl.BlockSpec((TR, CB), lambda h, i: (i, c0 + h))

    xrow = pl.BlockSpec((TR, CB), lambda h, i: (jnp.minimum(i, nx - 1), h))
    return _call(body, jobs=jobs, name=name, grid=(2, R // TR),
                 in_specs=[xrow, col(col_lg), col(col_lg + 2), col(col_lg + 4), col(0), col(0), xrow],
                 out_specs=[col(0), col(0), pl.BlockSpec(memory_space=pl.ANY)],
                 out_shape=[_sds((R, D), _MXU), _sds((R, D)), _sds(P.shape, _MXU)],
                 scratch=[pltpu.VMEM((3, TR, CB), _MXU), pltpu.SemaphoreType.DMA((3,))])(dmix, P, P, P, hf, hb, O)


def _conv_bwd(dxc, P, conv_w, dP, S, D, col0, name, jobs=()):
    R = P.shape[0]
    CB = D // 2

    def body(dp_, d_ref, dn_, xp_ref, x_ref, xn_ref, cw_ref, dp_in, dpo_ref, dcw_ref, dcb_ref):
        i = pl.program_id(1)

        @pl.when(i == 0)
        def _():
            dcw_ref[...] = jnp.zeros_like(dcw_ref)
            dcb_ref[...] = jnp.zeros_like(dcb_ref)

        d = d_ref[...]
        catd = jnp.concatenate([dp_[...], d, dn_[...]], axis=0)
        catx = jnp.concatenate([xp_ref[...], x_ref[...], xn_ref[...]], axis=0)
        from_start, to_end = _seq_pos(i, S, R, CB)
        dxl = (cw_ref[2:3, :] * catd
               + cw_ref[0:1, :] * jnp.where(to_end >= 3, _shift(catd, -2), 0.0)
               + cw_ref[1:2, :] * jnp.where(to_end >= 2, _shift(catd, -1), 0.0)
               + cw_ref[3:4, :] * jnp.where(from_start >= 1, _shift(catd, 1), 0.0))
        dpo_ref[...] = dxl[8:8 + TR, :].astype(dpo_ref.dtype)
        taps = (jnp.where(from_start >= 2, _shift(catx, 2), 0.0), jnp.where(from_start >= 1, _shift(catx, 1), 0.0),
                catx, jnp.where(to_end >= 2, _shift(catx, -1), 0.0))
        for kk in range(4):
            dcw_ref[kk:kk + 1, :] += jnp.sum(d * taps[kk][8:8 + TR, :], axis=0, keepdims=True)
        dcb_ref[...] += jnp.sum(d, axis=0, keepdims=True)

    return _call(body, jobs=jobs, name=name, grid=(2, R // TR),
                 in_specs=_halo_specs(R, CB, 0) + _halo_specs(R, CB, col0)
                 + [pl.BlockSpec((4, CB), lambda h, i: (0, h)), pl.BlockSpec(memory_space=pl.ANY)],
                 out_specs=[pl.BlockSpec((TR, CB), lambda h, i: (i, col0 + h)), pl.BlockSpec((4, CB), lambda h, i: (0, h)),
                            pl.BlockSpec((1, CB), lambda h, i: (0, h))],
                 out_shape=[_sds(dP.shape, dP.dtype), _sds((4, D)), _sds((1, D))],
                 aliases={7: 0})(dxc, dxc, dxc, P, P, P, conv_w, dP)


def _rope_tables(S, C):
    t = jnp.arange(S, dtype=jnp.int32)
    row = (t // GRID_W).astype(F32)
    col = (t % GRID_W).astype(F32)
    axis_dims = HD // 2
    freqs = ROPE_THETA ** (-jnp.arange(0, axis_dims, 2, dtype=F32) / axis_dims)
    ang = jnp.concatenate([row[:, None] * freqs, col[:, None] * freqs], axis=-1)
    cos = jnp.repeat(jnp.cos(ang), 2, axis=-1)
    sin = jnp.repeat(jnp.sin(ang), 2, axis=-1) * jnp.tile(jnp.array([-1.0, 1.0], F32), HD // 2)
    return (jnp.concatenate([cos, jnp.ones((C, HD), F32)], axis=0),
            jnp.concatenate([sin, jnp.zeros((C, HD), F32)], axis=0))


def _local_step(x, ctx, target, modx, modc, ng, shards, qg, kg, conv_w, conv_b, WA, WX, ba, bx, lam, fg, idx, opt):
    S, D = x.shape
    C = ctx.shape[0]
    R = S + C
    nx = S // TR
    Nb = shards['w_in'].shape[-1]
    Fb = shards['wd0'].shape[1]
    KVW = (ND * Nb - 5 * D) // 2
    G = D // KVW
    CB = D // 2
    col_lx = (D + 2 * KVW) // CB
    assert S % TR == 0 and C % TR == 0 and (D + 2 * KVW) % CB == 0 and CB % HD == 0
    mod2 = jnp.stack([modx, modc])
    X0 = jnp.concatenate([x, ctx], axis=0)
    COS, SIN = _rope_tables(S, C)
    ng0, ng1, ng2 = ng[0:1], ng[1:2], ng[2:3]
    ag = lambda n: _ag_job(shards[n])
    sib = lambda Gp: _rs_sibling_job(Gp.reshape(ND, -1, Gp.shape[-1]))
    add = lambda Gp, bufA, tag: _rs_add(Gp.reshape(ND, -1, Gp.shape[-1]), bufA, idx, f"{tag}_rs_add")
    out = {}

    def tn_specs(rows):
        tm = _tm(rows)
        return pl.BlockSpec((None, tm, Fb), lambda d, r: (d, r, 0)), pl.BlockSpec((tm, D), lambda d, r: (r, 0))

    wd_spec = pl.BlockSpec((None, None, Fb, D), lambda d, r: (d, 0, 0, 0))
    wg_spec = pl.BlockSpec((None, None, D, Fb), lambda d, r: (d, 0, 0, 0))

    ((WG0,),) = _comm_call([ag('wg0')], "ag_wg0")
    H1 = _norm_mod_fwd(X0, ng0, mod2, 0, R, nx, "ffn1_norm")
    G1, ((WU0,),) = _ffn_gate(H1, WG0, R, "ffn1_gate", jobs=[ag('wu0')])
    (U1, A1), ((WD0,),) = _ffn_up(H1, None, WU0, R, "ffn1_up", jobs=[ag('wd0')], G=G1)
    (Y1, X1), ((WIN,),) = _ffn_down(A1, WD0, 0, X0, mod2, 0, R, S, "ffn1_down", jobs=[ag('w_in')])
    H2 = _norm_mod_fwd(X1, ng1, mod2, 1, R, nx, "mix_norm")
    P, ((WOUT,), (WG1,)) = _proj_in(H2, WIN, "proj_in", jobs=[ag('w_out'), ag('wg1')])
    q, k, v = _qkv_prep(P, qg, kg, COS, SIN, D, KVW, "qkv_prep")
    (O, LSE), ((WU1,), (WD1,)) = _attn_fwd(q, k, v, S, G, "attn_fwd", jobs=[ag('wu1'), ag('wd1')])
    WOUT = WOUT.reshape(D, D)
    xc, af, uf, ab, ub = _lru_gates_fwd(P, conv_w, conv_b, WA, WX, ba, bx, lam, S, D, col_lx, "lru_gates")
    hf, hb = _lru_scan_fwd(af, uf, ab, ub, S, "lru_scan")
    mixb = _merge_fwd(P, hf, hb, O, S, D, col_lx + 2, "merge")
    Z, X2 = _proj_out(mixb, WOUT, X1, mod2, S, "proj_out")
    H3 = _norm_mod_fwd(X2, ng2, mod2, 2, S, nx, "ffn2_norm")
    G3, U3, A3 = _ffn_up(H3, WG1, WU1, S, "ffn2_up")
    Y3, X3 = _ffn_down(A3, WD1, 0, X2, mod2, 2, S, S, "ffn2_down")
    loss, dX3, dfg = _final_loss(X3, fg, target, "loss_head")

    dY3b, dg3 = _res_bwd(dX3, Y3, mod2, 2, 0.5, S, nx, "ffn2_dres")
    dG3, dU3 = _ffn_dact(dY3b, WD1, 0, G3, U3, S, "ffn2_dact")
    blk, row = tn_specs(S)
    dWD1 = _mm_tn(A3, blk, dY3b, row, (ND, 1, Fb, D), wd_spec, S, "ffn2_dwd")
    dWG1 = _mm_tn(H3, row, dG3, blk, (ND, 1, D, Fb), wg_spec, S, "ffn2_dwg")
    dWU1 = _mm_tn(H3, row, dU3, blk, (ND, 1, D, Fb), wg_spec, S, "ffn2_dwu")
    dH3, ((a_wd1,), (a_wg1,), (a_wu1,)) = _ffn_dh(dG3, dU3, WG1, WU1, S, "ffn2_dh", jobs=[sib(dWD1), sib(dWG1), sib(dWU1)])
    T_wd1, own_wd1 = add(dWD1, a_wd1, "wd1")
    T_wg1, own_wg1 = add(dWG1, a_wg1, "wg1")
    T_wu1, own_wu1 = add(dWU1, a_wu1, "wu1")
    dX2, dsh3, dsc3, dng2 = _norm_mod_bwd(X2, dH3, dX3, ng2, mod2, 2, S, nx, nx, "ffn2_dnorm")

    dZb, dg2 = _res_bwd(dX2, Z, mod2, 1, 1.0, S, nx, "mix_dres")
    dmix = _dproj_out(dZb, WOUT, "dproj_out")
    dWOUT = _mm_tn(mixb, pl.BlockSpec((_tm(S), D // ND), lambda d, r: (r, d)), dZb, pl.BlockSpec((_tm(S), D), lambda d, r: (r, 0)),
                   (ND, D // ND, D), pl.BlockSpec((None, D // ND, D), lambda d, r: (d, 0, 0)), S, "dw_out")
    (dOb, dhs, dP), ((a_wout,),) = _merge_bwd(dmix, P, hf, hb, O, S, D, col_lx + 2, "merge_bwd", jobs=[sib(dWOUT)])
    T_wout, own_wout = add(dWOUT, a_wout, "w_out")
    gf, gb = _lru_scan_bwd(af, ab, dhs, S, "lru_scan_bwd")
    dxc, dWA, dWX, dba, dbx, dlam = _lru_gates_bwd(xc, hf, hb, gf, gb, WA, WX, ba, bx, lam, S, "lru_gates_bwd")
    dLW = jnp.stack([dWA, dWX]).reshape(ND, -1, HD)
    (dP, dconv_w, dconv_b), ((a_lw,),) = _conv_bwd(dxc, P, conv_w, dP, S, D, col_lx, "conv_bwd", jobs=[sib(dLW)])
    T_lw, own_lw = add(dLW, a_lw, "lru_w")
    (dq, dk, dv), ((b_wd1,), (b_wg1,), (b_wu1,), (b_wout,), (b_lw,)) = _attn_bwd(
        q, k, v, O, LSE, dOb, S, G, "attn_bwd",
        jobs=[_rs_chips_job(T_wd1), _rs_chips_job(T_wg1), _rs_chips_job(T_wu1), _rs_chips_job(T_wout), _rs_chips_job(T_lw)])
    fin_wd = _rs_finish(own_wd1, b_wd1, "wd1_rs_finish", opt['ffn_wd'], 1, 2)
    fin_wg = _rs_finish(own_wg1, b_wg1, "wg1_rs_finish", opt['ffn_wg'], 1, 2)
    fin_wu = _rs_finish(own_wu1, b_wu1, "wu1_rs_finish", opt['ffn_wu'], 1, 2)
    out['w_out'] = _rs_finish(own_wout, b_wout, "w_out_rs_finish", opt['w_out'])
    (lw_sum,) = _rs_finish(own_lw, b_lw, "lru_w_rs_finish")
    dP, dqg, dkg = _qkv_bwd(P, dq, dk, dv, qg, kg, COS, SIN, dP, D, KVW, nx, "qkv_bwd")
    dH2, ((lw_full,),) = _dproj_in(dP, WIN, "dproj_in", jobs=[_ag_job(lw_sum)])
    dWIN = _mm_tn(H2, pl.BlockSpec((_tm(R), D), lambda d, r: (r, 0)), dP, pl.BlockSpec((_tm(R), Nb), lambda d, r: (r, d)),
                  (ND, D, Nb), pl.BlockSpec((None, D, Nb), lambda d, r: (d, 0, 0)), R, "dw_in")
    dX1, dsh2, dsc2, dng1 = _norm_mod_bwd(X1, dH2, dX2, ng1, mod2, 1, R, nx, nx, "mix_dnorm")

    dY1b, dg1 = _res_bwd(dX1, Y1, mod2, 0, 0.5, R, nx, "ffn1_dres")
    (dG1, dU1), ((a_win,),) = _ffn_dact(dY1b, WD0, 0, G1, U1, R, "ffn1_dact", jobs=[sib(dWIN)])
    T_win, own_win = add(dWIN, a_win, "w_in")
    blk, row = tn_specs(R)
    dWD0, ((b_win01,),) = _mm_tn(A1, blk, dY1b, row, (ND, 1, Fb, D), wd_spec, R, "ffn1_dwd", jobs=[_rs_chips_job(T_win, (0, 1))])
    dWG0, ((b_win2,), (a_wd0,)) = _mm_tn(H1, row, dG1, blk, (ND, 1, D, Fb), wg_spec, R, "ffn1_dwg",
                                         jobs=[_rs_chips_job(T_win, (2,)), sib(dWD0)])
    out['w_in'] = _rs_finish(own_win, [(b_win01, 0), (b_win01, 1), (b_win2, 0)], "w_in_rs_finish", opt['w_in'])
    T_wd0, own_wd0 = add(dWD0, a_wd0, "wd0")
    dWU0, ((a_wg0,), (b_wd0,)) = _mm_tn(H1, row, dU1, blk, (ND, 1, D, Fb), wg_spec, R, "ffn1_dwu",
                                        jobs=[sib(dWG0), _rs_chips_job(T_wd0)])
    T_wg0, own_wg0 = add(dWG0, a_wg0, "wg0")
    out['ffn_wd'] = _rs_finish(own_wd0, b_wd0, "wd0_rs_finish", opt['ffn_wd'], 0, 2, fin_wd)
    dH1, ((a_wu0,), (b_wg0,)) = _ffn_dh(dG1, dU1, WG0, WU0, R, "ffn1_dh", jobs=[sib(dWU0), _rs_chips_job(T_wg0)])
    T_wu0, own_wu0 = add(dWU0, a_wu0, "wu0")
    out['ffn_wg'] = _rs_finish(own_wg0, b_wg0, "wg0_rs_finish", opt['ffn_wg'], 0, 2, fin_wg)
    (dX0, dsh1, dsc1, dng0), ((b_wu0,),) = _norm_mod_bwd(
        X0, dH1, dX1, ng0, mod2, 0, R, nx, R // TR, "ffn1_dnorm", jobs=[_rs_chips_job(T_wu0)])
    out['ffn_wu'] = _rs_finish(own_wu0, b_wu0, "wu0_rs_finish", opt['ffn_wu'], 0, 2, fin_wu)
    out['lru_w'] = lw_full

    zero = jnp.zeros((1, D), F32)
    dmodx = jnp.concatenate([dsh1[0], dsc1[0], dg1[0], dsh2[0], dsc2[0], dg2[0], dsh3[0], dsc3[0], dg3[0]], axis=0)
    dmodc = jnp.concatenate([dsh1[1], dsc1[1], dg1[1], dsh2[1], dsc2[1], zero, zero, zero, zero], axis=0)
    out.update(loss=loss, grad_x=dX0[:S], dmodx=dmodx, dmodc=dmodc, norm_g=jnp.concatenate([dng0, dng1, dng2], axis=0),
               q_norm_g=dqg, k_norm_g=dkg, conv_w=dconv_w, conv_b=dconv_b, lru_ba=dba, lru_bx=dbx,
               lru_lambda=dlam, final_norm_g=dfg)
    return out


def _mesh_pos():
    return lax.axis_index("x"), lax.axis_index("y"), lax.axis_index("c")


def _all_gather(xb, name, in_vmem=False):
    space = pltpu.VMEM if in_vmem else pl.ANY

    def body(x_ref, out_ref, send_sems, recv_sems, local_sem):
        x, y, c = _mesh_pos()
        me, sibling = (x, y, c), (x, y, 1 - c)
        chips = [(1 - x, y), (x, 1 - y), (1 - x, 1 - y)]

        def slot(px, py, pc):
            return out_ref.at[4 * px + 2 * py + pc]

        def copy(k, block, to, src=None):
            return pltpu.make_async_remote_copy(
                src_ref=slot(*block) if src is None else src, dst_ref=slot(*block),
                send_sem=send_sems.at[k], recv_sem=recv_sems.at[k], device_id=to, device_id_type=MESH)

        mine = pltpu.make_async_copy(x_ref, slot(*me), local_sem)
        mine.start()
        first = [copy(0, me, sibling, src=x_ref)]
        first += [copy(1 + j, me, (*chip, c), src=x_ref) for j, chip in enumerate(chips)]
        for cp in first:
            cp.start()
        passed = [copy(4 + j, (*chip, c), sibling) for j, chip in enumerate(chips)]
        for j, chip in enumerate(chips):
            copy(1 + j, (*chip, c), me).wait_recv()
            passed[j].start()
        copy(0, sibling, me).wait_recv()
        for j, chip in enumerate(chips):
            copy(4 + j, (*chip, 1 - c), me).wait_recv()
        for cp in first + passed:
            cp.wait_send()
        mine.wait()

    return pl.pallas_call(
        body, name=name, out_shape=_sds((ND,) + xb.shape, xb.dtype),
        in_specs=[pl.BlockSpec(memory_space=space)], out_specs=pl.BlockSpec(memory_space=space),
        scratch_shapes=[pltpu.SemaphoreType.DMA((7,)), pltpu.SemaphoreType.DMA((7,)), pltpu.SemaphoreType.DMA(())])(xb)


def _ag_job(xb):
    def parts(ins, outs, sems, starting=False):
        x_ref, out_ref = ins[0], outs[0]
        send_sems, recv_sems, local_sem = sems
        x, y, c = _mesh_pos()
        me, sibling = (x, y, c), (x, y, 1 - c)
        chips = [(1 - x, y), (x, 1 - y), (1 - x, 1 - y)]

        def slot(px, py, pc):
            return out_ref.at[4 * px + 2 * py + pc]

        def copy(k, block, to, src=None):
            return pltpu.make_async_remote_copy(
                src_ref=slot(*block) if src is None else src, dst_ref=slot(*block),
                send_sem=send_sems.at[k], recv_sem=recv_sems.at[k], device_id=to, device_id_type=MESH)

        mine = pltpu.make_async_copy(x_ref, slot(*me), local_sem)
        first = [copy(0, me, sibling, src=x_ref)] + [copy(1 + j, me, (*chip, c), src=x_ref) for j, chip in enumerate(chips)]
        if starting:
            return mine, first
        passed = [copy(4 + j, (*chip, c), sibling) for j, chip in enumerate(chips)]
        landed = [copy(1 + j, (*chip, c), me) for j, chip in enumerate(chips)]
        from_sibling = [copy(0, sibling, me)] + [copy(4 + j, (*chip, 1 - c), me) for j, chip in enumerate(chips)]
        return mine, first, passed, landed, from_sibling

    def start(ins, outs, sems):
        mine, first = parts(ins, outs, sems, starting=True)
        mine.start()
        for cp in first:
            cp.start()

    def finish(ins, outs, sems):
        mine, first, passed, landed, from_sibling = parts(ins, outs, sems)
        for j in range(3):
            landed[j].wait_recv()
            passed[j].start()
        for cp in from_sibling:
            cp.wait_recv()
        for cp in first + passed:
            cp.wait_send()
        mine.wait()

    return _Job([xb], [_sds((ND,) + xb.shape, xb.dtype)],
                [pltpu.SemaphoreType.DMA((7,)), pltpu.SemaphoreType.DMA((7,)), pltpu.SemaphoreType.DMA(())], start, finish)


def _rs_sibling_job(Gp):
    def copies(ins, outs, sems):
        x, y, c = _mesh_pos()
        return [pltpu.make_async_remote_copy(
            src_ref=ins[0].at[2 * k + (1 - c)], dst_ref=outs[0].at[k], send_sem=sems[0].at[k], recv_sem=sems[1].at[k],
            device_id=(x, y, 1 - c), device_id_type=MESH) for k in range(4)]

    def start(ins, outs, sems):
        for cp in copies(ins, outs, sems):
            cp.start()

    def finish(ins, outs, sems):
        cps = copies(ins, outs, sems)
        for cp in cps:
            cp.wait_recv()
        for cp in cps:
            cp.wait_send()

    return _Job([Gp], [_sds((4,) + Gp.shape[1:], Gp.dtype)],
                [pltpu.SemaphoreType.DMA((4,)), pltpu.SemaphoreType.DMA((4,))], start, finish)


def _rs_chips_job(T, dests=(0, 1, 2)):
    def copies(ins, outs, sems):
        x, y, c = _mesh_pos()
        chips = [(1 - x, y), (x, 1 - y), (1 - x, 1 - y)]
        cps = []
        for i, j in enumerate(dests):
            px, py = chips[j]
            cps.append(pltpu.make_async_remote_copy(
                src_ref=ins[0].at[2 * px + py], dst_ref=outs[0].at[i], send_sem=sems[0].at[i], recv_sem=sems[1].at[i],
                device_id=(px, py, c), device_id_type=MESH))
        return cps

    def start(ins, outs, sems):
        for cp in copies(ins, outs, sems):
            cp.start()

    def finish(ins, outs, sems):
        cps = copies(ins, outs, sems)
        for cp in cps:
            cp.wait_recv()
        for cp in cps:
            cp.wait_send()

    n = len(dests)
    return _Job([T], [_sds((n,) + T.shape[1:], T.dtype)],
                [pltpu.SemaphoreType.DMA((n,)), pltpu.SemaphoreType.DMA((n,))], start, finish)


def _tile_rows(rows, cols):
    best = None
    for t in range(16, rows + 1, 16):
        if rows % t == 0 and t * cols * 4 <= (1 << 20):
            best = t
    return best if best is not None else rows


def _prefetch_call(body, *, name, grid, in_specs, out_specs, out_shape):
    return pl.pallas_call(
        body, name=name, out_shape=out_shape,
        grid_spec=pltpu.PrefetchScalarGridSpec(num_scalar_prefetch=1, grid=grid, in_specs=in_specs, out_specs=out_specs),
        compiler_params=pltpu.CompilerParams(dimension_semantics=("arbitrary",) * len(grid), vmem_limit_bytes=VMEM_LIMIT))


def _rs_add(Gp, bufA, idx, name):
    rows, cols = Gp.shape[1:]
    tr = _tile_rows(rows, cols)

    def body(i_ref, g_ref, a_ref, t_ref, own_ref):
        t = g_ref[...] + a_ref[...]
        t_ref[...] = t.astype(t_ref.dtype)

        @pl.when(pl.program_id(1) == i_ref[1])
        def _():
            own_ref[...] = t

    return _prefetch_call(
        body, name=name, grid=(rows // tr, 4),
        in_specs=[pl.BlockSpec((None, tr, cols), lambda r, k, i_ref: (2 * k + i_ref[0], r, 0)),
                  pl.BlockSpec((None, tr, cols), lambda r, k, i_ref: (k, r, 0))],
        out_specs=[pl.BlockSpec((None, tr, cols), lambda r, k, i_ref: (k, r, 0)),
                   pl.BlockSpec((tr, cols), lambda r, k, i_ref: (r, 0))],
        out_shape=[_sds((4, rows, cols), jnp.bfloat16), _sds((rows, cols))])(idx, Gp, bufA)


def _adam(w, g, m, v):
    m = ADAM_B1 * m + (1.0 - ADAM_B1) * g
    v = ADAM_B2 * v + (1.0 - ADAM_B2) * (g * g)
    m_hat = m / (1.0 - ADAM_B1 ** ADAM_STEP)
    v_hat = v / (1.0 - ADAM_B2 ** ADAM_STEP)
    return -ADAM_LR * (m_hat / (jnp.sqrt(v_hat) + ADAM_EPS) + ADAM_WD * w), m, v


def _rs_finish(Town, bufB, name, wmv=None, slab=0, n_slabs=1, prev=None):
    rows, cols = Town.shape
    tr = _tile_rows(rows, cols)
    nr = rows // tr
    n_in = 4 + (3 if wmv is not None else 0)
    n_out = 4 if wmv is not None else 1

    def body(*refs):
        ins, outs = refs[:n_in], refs[len(refs) - n_out:]
        g = ((ins[0][...] + ins[1][...].astype(F32)) + ins[2][...].astype(F32)) + ins[3][...].astype(F32)
        outs[0][...] = g
        if wmv is not None:
            d, m, v = _adam(ins[4][...], g, ins[5][...], ins[6][...])
            outs[1][...] = d
            outs[2][...] = m
            outs[3][...] = v

    plain = pl.BlockSpec((tr, cols), lambda r: (r, 0))
    slabbed = pl.BlockSpec((tr, cols), lambda r: (slab * nr + r, 0))
    pairs = bufB if isinstance(bufB, list) else [(bufB, j) for j in range(3)]
    in_specs = [plain] + [pl.BlockSpec((None, tr, cols), (lambda j: lambda r: (j, r, 0))(j)) for _, j in pairs]
    args = [Town] + [a for a, _ in pairs]
    if wmv is not None:
        in_specs += [slabbed] * 3
        args += list(wmv)
    aliases = None
    if prev is not None:
        in_specs += [pl.BlockSpec(memory_space=pl.ANY)] * n_out
        aliases = {len(args) + i: i for i in range(n_out)}
        args += list(prev)
    return _call(body, name=name, grid=(nr,), in_specs=in_specs, out_specs=[slabbed] * n_out,
                 out_shape=[_sds((n_slabs * rows, cols))] * n_out, aliases=aliases)(*args)


def _adamw_plain(w, g, m, v, name):
    rows, cols = w.shape
    tr = _tile_rows(rows, cols)

    def body(w_ref, g_ref, m_ref, v_ref, d_ref, mo_ref, vo_ref):
        d, m_, v_ = _adam(w_ref[...], g_ref[...], m_ref[...], v_ref[...])
        d_ref[...] = d
        mo_ref[...] = m_
        vo_ref[...] = v_

    spec = pl.BlockSpec((tr, cols), lambda r: (r, 0))
    return _call(body, name=name, grid=(rows // tr,), in_specs=[spec] * 4, out_specs=[spec] * 3,
                 out_shape=[_sds((rows, cols))] * 3)(w, g, m, v)


_MOD_TK = 512


def _mod_fwd(cc16, w_loc, b_loc, name):
    D, cols = w_loc.shape
    tk = min(_MOD_TK, D)
    nk = D // tk

    def body(c_ref, w_ref, b_ref, o_ref):
        kk = pl.program_id(0)

        @pl.when(kk == 0)
        def _():
            o_ref[...] = jnp.zeros_like(o_ref)

        cc = c_ref[...]
        o_ref[...] += _dot(cc * _sigmoid(cc), w_ref[...])

        @pl.when(kk == nk - 1)
        def _():
            o_ref[...] += b_ref[...]

    return _call(body, name=name, grid=(nk,),
                 in_specs=[pl.BlockSpec((16, tk), lambda kk: (0, kk)), pl.BlockSpec((tk, cols), lambda kk: (kk, 0)),
                           _full_spec((1, cols))],
                 out_specs=_full_spec((16, cols)), out_shape=_sds((16, cols)))(cc16, w_loc, b_loc)


def _mod_bwd(dm_loc, cc16, w_loc, name):
    D, cols = w_loc.shape

    def body(dm_ref, c_ref, w_ref, gw_ref, ds_ref):
        rows = [dm_ref[b, 0:1, :] for b in range(ND)]
        ctx = dm_ref[0, 1:2, :]
        for b in range(1, ND):
            ctx = ctx + dm_ref[b, 1:2, :]
        dm16 = jnp.concatenate(rows + [ctx, jnp.zeros((7, cols), F32)], axis=0)
        cc = c_ref[...]
        gw_ref[...] = _dot_tn(cc * _sigmoid(cc), dm16)
        ds_ref[...] = _dot_nt(dm16, w_ref[...])

    tk = min(_MOD_TK, D)
    return _call(body, name=name, grid=(D // tk,),
                 in_specs=[_full_spec((ND, 8, cols)), pl.BlockSpec((16, tk), lambda kk: (0, kk)),
                           pl.BlockSpec((tk, cols), lambda kk: (kk, 0))],
                 out_specs=[pl.BlockSpec((tk, cols), lambda kk: (kk, 0)), pl.BlockSpec((16, tk), lambda kk: (0, kk))],
                 out_shape=[_sds((D, cols)), _sds((16, D))])(dm_loc, cc16, w_loc)


def _bmod_grad(dm_all, name):
    n = dm_all.shape[-1]

    def body(dm_ref, o_ref):
        acc = dm_ref[0, 0:1, :] + dm_ref[0, 1:2, :]
        for b in range(1, ND):
            acc = (acc + dm_ref[b, 0:1, :]) + dm_ref[b, 1:2, :]
        o_ref[...] = acc

    return _call(body, name=name, grid=(1,), in_specs=[_full_spec((ND, 8, n))], out_specs=_full_spec((1, n)),
                 out_shape=_sds((1, n)))(dm_all)


_SMALL_ROWS = 24
_ROW_CCTX = 15


def _small_finish(parts, c_ctx, name):
    D = parts.shape[-1]

    def body(p_ref, c_ref, o_ref):
        acc = p_ref[0]
        for b in range(1, ND):
            acc = acc + p_ref[b]
        cc = c_ref[...]
        sg = _sigmoid(cc)
        dsilu = sg * (1.0 + cc * (1.0 - sg))
        row = lax.broadcasted_iota(jnp.int32, acc.shape, 0)
        o_ref[...] = jnp.where(row == _ROW_CCTX, acc * dsilu, acc)

    return _call(body, name=name, grid=(1,), in_specs=[_full_spec(parts.shape), _full_spec((1, D))],
                 out_specs=_full_spec((_SMALL_ROWS, D)), out_shape=_sds((_SMALL_ROWS, D)))(parts, c_ctx)


_WEIGHTS = ['c_ctx', 'w_mod', 'b_mod', 'norm_g', 'ffn_wg', 'ffn_wu', 'ffn_wd', 'w_in', 'w_out', 'q_norm_g', 'k_norm_g',
            'conv_w', 'conv_b', 'lru_wa', 'lru_ba', 'lru_wx', 'lru_bx', 'lru_lambda', 'final_norm_g']
_SMALL = ['c_ctx', 'b_mod', 'norm_g', 'q_norm_g', 'k_norm_g', 'conv_w', 'conv_b', 'lru_ba', 'lru_bx', 'lru_lambda',
          'final_norm_g']


def _pad_rows(a, rows):
    return jnp.pad(a, ((0, rows - a.shape[0]),) + ((0, 0),) * (a.ndim - 1))


def _step(w, m, v, x, c, ctx, loss_target):
    xi, yi, ci = _mesh_pos()
    me = 4 * xi + 2 * yi + ci
    idx = jnp.stack([ci, 2 * xi + yi]).astype(jnp.int32)
    S, D = x.shape[1:]
    Ds = D // ND
    cols = w['w_mod'].shape[-1]

    sp = jnp.concatenate([w['norm_g'][0], w['conv_w'][0], w['lru_ba'][0], w['lru_bx'][0], w['lru_lambda'][0]], axis=0)
    spg = _all_gather(_pad_rows(sp, 16), "ag_small_params", in_vmem=True)
    spf = jnp.transpose(spg, (1, 0, 2)).reshape(16, D)
    ng, conv_w, ba, bx, lam = spf[0:3], spf[3:7], spf[7:9], spf[9:11], spf[11:13]

    cg = _all_gather(_pad_rows(c, 8), "ag_cond", in_vmem=True)
    cc16 = _pad_rows(jnp.concatenate([cg[:, 0, :], w['c_ctx'][None, :]], axis=0), 16)
    b_loc = lax.dynamic_slice_in_dim(w['b_mod'], me * cols, cols, axis=1)
    mod_loc = _mod_fwd(cc16, w['w_mod'][0], b_loc, "mod_fwd")
    modg = _all_gather(mod_loc, "ag_mod", in_vmem=True)
    mod16 = jnp.transpose(modg, (1, 0, 2)).reshape(16, ND * cols)
    modx = lax.dynamic_index_in_dim(mod16, me, axis=0, keepdims=False).reshape(9, D)
    modc = mod16[8].reshape(9, D)

    shards = {'w_in': w['w_in'][0].astype(_MXU), 'w_out': w['w_out'][0].astype(_MXU)}
    for layer in range(2):
        shards[f'wg{layer}'] = w['ffn_wg'][0, layer].astype(_MXU)[None]
        shards[f'wu{layer}'] = w['ffn_wu'][0, layer].astype(_MXU)[None]
        shards[f'wd{layer}'] = w['ffn_wd'][0, layer].astype(_MXU)[None]
    WA, WX = w['lru_wa'][0].astype(_MXU), w['lru_wx'][0].astype(_MXU)
    big = ('ffn_wg', 'ffn_wu', 'ffn_wd', 'w_in', 'w_out')
    opt = {n: tuple(a[n].reshape(-1, a[n].shape[-1]) for a in (w, m, v)) for n in big}

    g = _local_step(x[0], ctx[0], loss_target[0], modx, modc, ng, shards, w['q_norm_g'], w['k_norm_g'],
                    conv_w, w['conv_b'], WA, WX, ba, bx, lam, w['final_norm_g'][None, :], idx, opt)

    grad, delta, new_m, new_v = {}, {}, {}, {}
    for n in big:
        grad[n], delta[n], new_m[n], new_v[n] = [o.reshape(w[n].shape) for o in g[n]]

    lfull = g['lru_w'].reshape((2,) + w['lru_wa'].shape[1:])
    for i, n in enumerate(('lru_wa', 'lru_wx')):
        shard = w[n].shape
        view = lambda a: a.reshape(-1, HD)
        grad[n] = lfull[i].reshape(shard)
        outs = _adamw_plain(view(w[n]), view(lfull[i]), view(m[n]), view(v[n]), f"adamw_{n}")
        delta[n], new_m[n], new_v[n] = [o.reshape(shard) for o in outs]

    dm = _pad_rows(jnp.stack([g['dmodx'].reshape(-1), g['dmodc'].reshape(-1)]), 8)
    dm_all = _all_gather(dm, "ag_dmod", in_vmem=True)
    dm_loc = lax.dynamic_slice_in_dim(dm_all, me * cols, cols, axis=2)
    gw_mod, dsil = _mod_bwd(dm_loc, cc16, w['w_mod'][0], "mod_bwd")
    grad['w_mod'] = gw_mod[None]
    outs = _adamw_plain(w['w_mod'][0], gw_mod, m['w_mod'][0], v['w_mod'][0], "adamw_w_mod")
    delta['w_mod'], new_m['w_mod'], new_v['w_mod'] = [o[None] for o in outs]
    grad['b_mod'] = _bmod_grad(dm_all, "bmod_grad")

    pad_d = lambda a: jnp.concatenate([a, jnp.zeros((1, D - a.shape[1]), F32)], axis=1)
    small = jnp.concatenate([g['norm_g'], g['conv_w'], g['conv_b'], g['lru_ba'], g['lru_bx'], g['lru_lambda'],
                             g['final_norm_g'], dsil[8:9], pad_d(g['q_norm_g']), pad_d(g['k_norm_g'])], axis=0)
    parts = _all_gather(_pad_rows(small, _SMALL_ROWS), "ag_small_grads", in_vmem=True)
    ssum = _small_finish(parts, w['c_ctx'][None, :], "small_finish")
    mine = lambda rows: lax.dynamic_slice_in_dim(rows, me * Ds, Ds, axis=1)
    grad['norm_g'] = mine(ssum[0:3])[None]
    grad['conv_w'] = mine(ssum[3:7])[None]
    grad['conv_b'] = ssum[7:8]
    grad['lru_ba'] = mine(ssum[8:10])[None]
    grad['lru_bx'] = mine(ssum[10:12])[None]
    grad['lru_lambda'] = mine(ssum[12:14])[None]
    grad['final_norm_g'] = ssum[14]
    grad['c_ctx'] = ssum[_ROW_CCTX]
    grad['q_norm_g'] = ssum[16:17, :HD]
    grad['k_norm_g'] = ssum[17:18, :HD]

    def pack(d):
        flat = jnp.concatenate([d[n].reshape(-1) for n in _SMALL])
        padded = -(-flat.shape[0] // 1024) * 1024
        return jnp.concatenate([flat, jnp.zeros((padded - flat.shape[0],), F32)]).reshape(-1, HD)

    outs = _adamw_plain(pack(w), pack(grad), pack(m), pack(v), "adamw_small")
    off = 0
    for n in _SMALL:
        size = math.prod(w[n].shape)
        for dst, o in zip((delta, new_m, new_v), outs):
            dst[n] = o.reshape(-1)[off:off + size].reshape(w[n].shape)
        off += size

    loss = lax.psum(g['loss'][0, 0], ("x", "y", "c"))
    return (loss, g['grad_x'][None], *[grad[n] for n in _WEIGHTS], *[delta[n] for n in _WEIGHTS],
            *[new_m[n] for n in _WEIGHTS], *[new_v[n] for n in _WEIGHTS])


def kernel(x, c, ctx, c_ctx, w_mod, b_mod, norm_g, ffn_wg, ffn_wu, ffn_wd, w_in, w_out, q_norm_g, k_norm_g, conv_w, conv_b, lru_wa, lru_ba, lru_wx, lru_bx, lru_lambda, final_norm_g, loss_target, m_c_ctx, m_w_mod, m_b_mod, m_norm_g, m_ffn_wg, m_ffn_wu, m_ffn_wd, m_w_in, m_w_out, m_q_norm_g, m_k_norm_g, m_conv_w, m_conv_b, m_lru_wa, m_lru_ba, m_lru_wx, m_lru_bx, m_lru_lambda, m_final_norm_g, v_c_ctx, v_w_mod, v_b_mod, v_norm_g, v_ffn_wg, v_ffn_wu, v_ffn_wd, v_w_in, v_w_out, v_q_norm_g, v_k_norm_g, v_conv_w, v_conv_b, v_lru_wa, v_lru_ba, v_lru_wx, v_lru_bx, v_lru_lambda, v_final_norm_g):
    given = dict(locals())
    w = {n: given[n] for n in _WEIGHTS}
    m = {n: given["m_" + n] for n in _WEIGHTS}
    v = {n: given["v_" + n] for n in _WEIGHTS}
    return _step(w, m, v, x, c, ctx, loss_target)
```

```python
import functools
import math

import jax
import jax.numpy as jnp
from jax import lax
from jax.experimental import pallas as pl
from jax.experimental.pallas import tpu as pltpu

F32 = jnp.float32
_MXU = jnp.bfloat16
ND = 8
TR = 256
HD = 128
EPS = 1e-6
GRID_W = 64
ROPE_THETA = 10000.0
LRU_C = 8.0
VMEM_LIMIT = 56 * 1024 * 1024
SCAN_W = 512
ADAM_LR, ADAM_B1, ADAM_B2, ADAM_EPS, ADAM_WD, ADAM_STEP = 0.001, 0.9, 0.999, 1e-08, 0.01, 10
MESH = pl.DeviceIdType.MESH


class _Job:
    def __init__(self, inputs, out_shapes, sems, start, finish):
        self.inputs, self.out_shapes, self.sems, self.start, self.finish = inputs, out_shapes, sems, start, finish


def _call(body, *, name, grid, in_specs, out_specs, out_shape, scratch=(), aliases=None, jobs=()):
    params = pltpu.CompilerParams(dimension_semantics=("arbitrary",) * len(grid), vmem_limit_bytes=VMEM_LIMIT)
    if not jobs:
        return pl.pallas_call(body, name=name, grid=grid, in_specs=in_specs, out_specs=out_specs, out_shape=out_shape,
                              scratch_shapes=scratch, input_output_aliases=aliases or {}, compiler_params=params)
    single = not isinstance(out_specs, (list, tuple))
    o_specs = [out_specs] if single else list(out_specs)
    o_shape = [out_shape] if single else list(out_shape)
    n_in, n_out, n_scr = len(in_specs), len(o_specs), len(scratch)
    j_in = [a for j in jobs for a in j.inputs]
    j_out = [s for j in jobs for s in j.out_shapes]
    j_sem = [s for j in jobs for s in j.sems]
    hbm = pl.BlockSpec(memory_space=pl.ANY)

    def wrapped(*refs):
        ins, rest = refs[:n_in], refs[n_in:]
        jin, rest = rest[:len(j_in)], rest[len(j_in):]
        outs, rest = rest[:n_out], rest[n_out:]
        jout, rest = rest[:len(j_out)], rest[len(j_out):]
        scr, jsem = rest[:n_scr], rest[n_scr:]
        first = functools.reduce(jnp.logical_and, [pl.program_id(a) == 0 for a in range(len(grid))])
        last = functools.reduce(jnp.logical_and, [pl.program_id(a) == grid[a] - 1 for a in range(len(grid))])

        def each(which):
            i = o = s = 0
            for j in jobs:
                ni, no, ns = len(j.inputs), len(j.out_shapes), len(j.sems)
                getattr(j, which)(jin[i:i + ni], jout[o:o + no], jsem[s:s + ns])
                i, o, s = i + ni, o + no, s + ns

        @pl.when(first)
        def _():
            each("start")

        body(*ins, *outs, *scr)

        @pl.when(last)
        def _():
            each("finish")

    call = pl.pallas_call(wrapped, name=name, grid=grid, in_specs=list(in_specs) + [hbm] * len(j_in),
                          out_specs=o_specs + [hbm] * len(j_out), out_shape=o_shape + j_out,
                          scratch_shapes=list(scratch) + j_sem, input_output_aliases=aliases or {}, compiler_params=params)

    def run(*args):
        res = call(*args, *j_in)
        comp = res[0] if single else list(res[:n_out])
        jres, o = [], n_out
        for j in jobs:
            jres.append(list(res[o:o + len(j.out_shapes)]))
            o += len(j.out_shapes)
        return comp, jres

    return run


def _comm_call(jobs, name):
    j_in = [a for j in jobs for a in j.inputs]
    j_out = [s for j in jobs for s in j.out_shapes]
    j_sem = [s for j in jobs for s in j.sems]
    hbm = pl.BlockSpec(memory_space=pl.ANY)

    def body(*refs):
        jin, jout, jsem = refs[:len(j_in)], refs[len(j_in):len(j_in) + len(j_out)], refs[len(j_in) + len(j_out):]
        for which in ("start", "finish"):
            i = o = s = 0
            for j in jobs:
                ni, no, ns = len(j.inputs), len(j.out_shapes), len(j.sems)
                getattr(j, which)(jin[i:i + ni], jout[o:o + no], jsem[s:s + ns])
                i, o, s = i + ni, o + no, s + ns

    res = pl.pallas_call(body, name=name, out_shape=j_out, in_specs=[hbm] * len(j_in), out_specs=[hbm] * len(j_out),
                         scratch_shapes=j_sem)(*j_in)
    jres, o = [], 0
    for j in jobs:
        jres.append(list(res[o:o + len(j.out_shapes)]))
        o += len(j.out_shapes)
    return jres


def _sds(shape, dtype=F32):
    return jax.ShapeDtypeStruct(tuple(shape), dtype)


def _dot(a, b):
    return jnp.dot(a.astype(_MXU), b.astype(_MXU), preferred_element_type=F32)


def _dot_nt(a, b):
    return lax.dot_general(a.astype(_MXU), b.astype(_MXU), (((1,), (1,)), ((), ())), preferred_element_type=F32)


def _dot_tn(a, b):
    return lax.dot_general(a.astype(_MXU), b.astype(_MXU), (((0,), (0,)), ((), ())), preferred_element_type=F32)


def _sigmoid(x):
    return 0.5 * jnp.tanh(0.5 * x) + 0.5


_GELU_C = math.sqrt(2.0 / math.pi)


def _gelu_and_grad(x):
    x2 = x * x
    t = jnp.tanh(_GELU_C * (x + 0.044715 * x * x2))
    ge = 0.5 * x * (1.0 + t)
    dge = 0.5 * (1.0 + t) + 0.5 * x * (1.0 - t * t) * (_GELU_C * (1.0 + 3.0 * 0.044715 * x2))
    return ge, dge


def _tm(rows):
    assert rows % 4 == 0 and (rows // 4) % 16 == 0
    return rows // 4


def _row_spec(width, nmax=None):
    if nmax is None:
        return pl.BlockSpec((TR, width), lambda i: (i, 0))
    return pl.BlockSpec((TR, width), lambda i: (jnp.minimum(i, nmax), 0))


def _full_spec(shape):
    n = len(shape)
    return pl.BlockSpec(tuple(shape), lambda *_: (0,) * n)


def _mod_spec(D, nx):
    return pl.BlockSpec((None, 9, D), lambda i: (i // nx, 0, 0))


def _norm_mod_fwd(X, ng, mod2, k, rows, nx, name):
    D = X.shape[1]

    def body(x_ref, g_ref, mod_ref, h_ref):
        x = x_ref[...]
        r = lax.rsqrt(jnp.mean(x * x, axis=-1, keepdims=True) + EPS)
        n = (x * r) * g_ref[...]
        h_ref[...] = (n * (1.0 + mod_ref[3 * k + 1:3 * k + 2, :]) + mod_ref[3 * k:3 * k + 1, :]).astype(h_ref.dtype)

    return _call(body, name=name, grid=(rows // TR,),
                 in_specs=[_row_spec(D), _full_spec((1, D)), _mod_spec(D, nx)],
                 out_specs=_row_spec(D), out_shape=_sds((rows, D), _MXU))(X, ng, mod2)


def _norm_mod_bwd(X, dH, dXres, ng, mod2, k, rows, nx, res_tiles, name, jobs=(), branch=None):
    D = X.shape[1]
    ngroups = -(-(rows // TR) // nx)
    n_in = 5 + (1 if branch is not None else 0)

    def body(*refs):
        x_ref, dh_ref, dres_ref, g_ref, mod_ref = refs[:5]
        dx_ref, dsh_ref, dsc_ref, dng_ref = refs[n_in:n_in + 4]
        i = pl.program_id(0)
        x = x_ref[...]
        dh = dh_ref[...]
        g = g_ref[...]
        r = lax.rsqrt(jnp.mean(x * x, axis=-1, keepdims=True) + EPS)
        xh = x * r
        n = xh * g
        dn_mod = dh * (1.0 + mod_ref[3 * k + 1:3 * k + 2, :])

        @pl.when(i % nx == 0)
        def _():
            dsh_ref[...] = jnp.zeros_like(dsh_ref)
            dsc_ref[...] = jnp.zeros_like(dsc_ref)
            if branch is not None:
                refs[n_in + 5][...] = jnp.zeros_like(refs[n_in + 5])

        @pl.when(i == 0)
        def _():
            dng_ref[...] = jnp.zeros_like(dng_ref)

        dsh_ref[...] += jnp.sum(dh, axis=0, keepdims=True)
        dsc_ref[...] += jnp.sum(dh * n, axis=0, keepdims=True)
        dng_ref[...] += jnp.sum(dn_mod * xh, axis=0, keepdims=True)
        dn = dn_mod * g
        dres = jnp.where(i < res_tiles, dres_ref[...], 0.0)
        dx = r * (dn - xh * jnp.mean(dn * xh, axis=-1, keepdims=True)) + dres
        dx_ref[...] = dx
        if branch is not None:
            _, k2, coef = branch
            refs[n_in + 4][...] = ((coef * mod_ref[3 * k2 + 2:3 * k2 + 3, :]) * dx).astype(refs[n_in + 4].dtype)
            refs[n_in + 5][...] += jnp.sum(coef * dx * refs[5][...], axis=0, keepdims=True)

    grp = pl.BlockSpec((None, 1, D), lambda i: (i // nx, 0, 0))
    in_specs = [_row_spec(D), _row_spec(D), _row_spec(D, res_tiles - 1), _full_spec((1, D)), _mod_spec(D, nx)]
    out_specs = [_row_spec(D), grp, grp, _full_spec((1, D))]
    out_shape = [_sds((rows, D)), _sds((ngroups, 1, D)), _sds((ngroups, 1, D)), _sds((1, D))]
    args = [X, dH, dXres, ng, mod2]
    if branch is not None:
        in_specs.append(_row_spec(D))
        args.append(branch[0])
        out_specs += [_row_spec(D), grp]
        out_shape += [_sds((rows, D), _MXU), _sds((ngroups, 1, D))]
    return _call(body, jobs=jobs, name=name, grid=(rows // TR,), in_specs=in_specs, out_specs=out_specs,
                 out_shape=out_shape)(*args)


def _ffn_gate(H, WG, rows, name, jobs=()):
    D = H.shape[1]
    Fb = WG.shape[-1]

    def body(h_ref, w_ref, g_ref):
        g_ref[...] = _dot(h_ref[...], w_ref[...])

    tm = _tm(rows)
    return _call(body, jobs=jobs, name=name, grid=(ND, rows // tm),
                 in_specs=[pl.BlockSpec((tm, D), lambda d, m: (m, 0)), pl.BlockSpec((None, None, D, Fb), lambda d, m: (d, 0, 0, 0))],
                 out_specs=pl.BlockSpec((None, tm, Fb), lambda d, m: (d, m, 0)), out_shape=_sds((ND, rows, Fb)))(H, WG)


def _ffn_up(H, WG, WU, rows, name, jobs=(), G=None):
    D = H.shape[1]
    Fb = WU.shape[-1]

    def body(h_ref, w_ref, x_ref, *outs):
        h = h_ref[...]
        g = x_ref[...] if G is not None else _dot(h, x_ref[...])
        u = _dot(h, w_ref[...])
        if G is None:
            outs[0][...] = g
        outs[-2][...] = u
        outs[-1][...] = ((g * _sigmoid(g)) * u).astype(outs[-1].dtype)

    tm = _tm(rows)
    blk = pl.BlockSpec((None, tm, Fb), lambda d, m: (d, m, 0))
    wspec = pl.BlockSpec((None, None, D, Fb), lambda d, m: (d, 0, 0, 0))
    f32o, bfo = _sds((ND, rows, Fb)), _sds((ND, rows, Fb), _MXU)
    return _call(body, jobs=jobs, name=name, grid=(ND, rows // tm),
                 in_specs=[pl.BlockSpec((tm, D), lambda d, m: (m, 0)), wspec, blk if G is not None else wspec],
                 out_specs=[blk, blk] if G is not None else [blk, blk, blk],
                 out_shape=[f32o, bfo] if G is not None else [f32o, f32o, bfo])(H, WU, G if G is not None else WG)


def _ffn_down(A, WD, layer, X, mod2, k, rows, S, name, jobs=(), next_norm=None):
    Fb, D = WD.shape[-2:]
    tm = _tm(rows) // 2

    def body(a_ref, w_ref, x_ref, mod_ref, *rest):
        if next_norm is not None:
            g_ref, y_ref, xn_ref, h_ref, acc_ref = rest
        else:
            y_ref, xn_ref, acc_ref = rest
        d = pl.program_id(1)

        @pl.when(d == 0)
        def _():
            acc_ref[...] = jnp.zeros_like(acc_ref)

        acc_ref[...] += _dot(a_ref[...], w_ref[...])

        @pl.when(d == ND - 1)
        def _():
            y = acc_ref[...]
            y_ref[...] = y
            is_ctx = (pl.program_id(0) * tm + lax.broadcasted_iota(jnp.int32, (tm, 1), 0)) >= S

            def mod_row(j):
                return jnp.where(is_ctx, mod_ref[1, j:j + 1, :], mod_ref[0, j:j + 1, :])

            xn = x_ref[...] + (0.5 * mod_row(3 * k + 2)) * y
            xn_ref[...] = xn
            if next_norm is not None:
                k2 = next_norm[1]
                n = (xn * lax.rsqrt(jnp.mean(xn * xn, axis=-1, keepdims=True) + EPS)) * g_ref[...]
                h_ref[...] = (n * (1.0 + mod_row(3 * k2 + 1)) + mod_row(3 * k2)).astype(h_ref.dtype)

    row = pl.BlockSpec((tm, D), lambda m, d: (m, 0))
    in_specs = [pl.BlockSpec((None, tm, Fb), lambda m, d: (d, m, 0)),
                pl.BlockSpec((None, None, Fb, D), lambda m, d: (d, layer, 0, 0)),
                row, pl.BlockSpec((2, 9, D), lambda m, d: (0, 0, 0))]
    args = [A, WD, X, mod2]
    out_specs, out_shape = [row, row], [_sds((rows, D)), _sds((rows, D))]
    if next_norm is not None:
        in_specs.append(pl.BlockSpec((1, D), lambda m, d: (0, 0)))
        args.append(next_norm[0])
        out_specs.append(row)
        out_shape.append(_sds((rows, D), _MXU))
    return _call(body, jobs=jobs, name=name, grid=(rows // tm, ND), in_specs=in_specs, out_specs=out_specs,
                 out_shape=out_shape, scratch=[pltpu.VMEM((tm, D), F32)])(*args)


def _ffn_down_loss(A, WD, X, mod2, k, fg, target, name):
    Fb, D = WD.shape[-2:]
    S = X.shape[0]
    tm = _tm(S) // 2

    def body(a_ref, w_ref, x_ref, mod_ref, g_ref, t_ref, loss_ref, dx_ref, dy_ref, dgate_ref, dg_ref, acc_ref):
        m, d = pl.program_id(0), pl.program_id(1)

        @pl.when(d == 0)
        def _():
            acc_ref[...] = jnp.zeros_like(acc_ref)

        @pl.when(jnp.logical_and(m == 0, d == 0))
        def _():
            loss_ref[...] = jnp.zeros_like(loss_ref)
            dgate_ref[...] = jnp.zeros_like(dgate_ref)
            dg_ref[...] = jnp.zeros_like(dg_ref)

        acc_ref[...] += _dot(a_ref[...], w_ref[...])

        @pl.when(d == ND - 1)
        def _():
            y = acc_ref[...]
            gate = 0.5 * mod_ref[3 * k + 2:3 * k + 3, :]
            x = x_ref[...] + gate * y
            g = g_ref[...]
            r = lax.rsqrt(jnp.mean(x * x, axis=-1, keepdims=True) + EPS)
            n = x * r
            err = n * g - t_ref[...]
            loss_ref[...] += 0.5 * jnp.sum(jnp.mean(err * err, axis=-1, keepdims=True), axis=0, keepdims=True)
            dy = err * (1.0 / D)
            dg_ref[...] += jnp.sum(dy * n, axis=0, keepdims=True)
            dn = dy * g
            dx = r * (dn - n * jnp.mean(dn * n, axis=-1, keepdims=True))
            dx_ref[...] = dx
            dy_ref[...] = (gate * dx).astype(dy_ref.dtype)
            dgate_ref[...] += jnp.sum(0.5 * dx * y, axis=0, keepdims=True)

    row = pl.BlockSpec((tm, D), lambda m, d: (m, 0))
    vec = pl.BlockSpec((1, D), lambda m, d: (0, 0))
    return _call(body, name=name, grid=(S // tm, ND),
                 in_specs=[pl.BlockSpec((None, tm, Fb), lambda m, d: (d, m, 0)),
                           pl.BlockSpec((None, None, Fb, D), lambda m, d: (d, 0, 0, 0)),
                           row, pl.BlockSpec((None, 9, D), lambda m, d: (0, 0, 0)), vec, row],
                 out_specs=[pl.BlockSpec((1, 1), lambda m, d: (0, 0)), row, row, vec, vec],
                 out_shape=[_sds((1, 1)), _sds((S, D)), _sds((S, D), _MXU), _sds((1, D)), _sds((1, D))],
                 scratch=[pltpu.VMEM((tm, D), F32)])(A, WD, X, mod2, fg, target)


def _ffn_dact(dYb, WD, layer, G, U, rows, name, jobs=()):
    Fb, D = WD.shape[-2:]

    def body(dy_ref, w_ref, g_ref, u_ref, dg_ref, du_ref):
        da = _dot_nt(dy_ref[...], w_ref[...])
        g = g_ref[...]
        sg = _sigmoid(g)
        dg_ref[...] = (da * u_ref[...] * (sg * (1.0 + g * (1.0 - sg)))).astype(dg_ref.dtype)
        du_ref[...] = (da * (g * sg)).astype(du_ref.dtype)

    tm = _tm(rows)
    blk = pl.BlockSpec((None, tm, Fb), lambda d, m: (d, m, 0))
    return _call(body, jobs=jobs, name=name, grid=(ND, rows // tm),
                 in_specs=[pl.BlockSpec((tm, D), lambda d, m: (m, 0)),
                           pl.BlockSpec((None, None, Fb, D), lambda d, m: (d, layer, 0, 0)), blk, blk],
                 out_specs=[blk, blk],
                 out_shape=[_sds((ND, rows, Fb), _MXU), _sds((ND, rows, Fb), _MXU)])(dYb, WD, G, U)


def _ffn_dh(dG, dU, WG, WU, rows, name, jobs=()):
    D, Fb = WG.shape[-2:]

    def body(dg_ref, du_ref, wg_ref, wu_ref, dh_ref, acc_ref):
        d = pl.program_id(1)

        @pl.when(d == 0)
        def _():
            acc_ref[...] = jnp.zeros_like(acc_ref)

        acc_ref[...] += _dot_nt(dg_ref[...], wg_ref[...]) + _dot_nt(du_ref[...], wu_ref[...])

        @pl.when(d == ND - 1)
        def _():
            dh_ref[...] = acc_ref[...]

    tm = _tm(rows)
    blk = pl.BlockSpec((None, tm, Fb), lambda m, d: (d, m, 0))
    wspec = pl.BlockSpec((None, None, D, Fb), lambda m, d: (d, 0, 0, 0))
    return _call(body, jobs=jobs, name=name, grid=(rows // tm, ND), in_specs=[blk, blk, wspec, wspec],
                 out_specs=pl.BlockSpec((tm, D), lambda m, d: (m, 0)), out_shape=_sds((rows, D)),
                 scratch=[pltpu.VMEM((tm, D), F32)])(dG, dU, WG, WU)


def _mm_tn(A, a_spec, B, b_spec, out_shape, out_spec, rows, name, prev=None, jobs=()):
    def body(*refs):
        a_ref, b_ref, o_ref = refs[0], refs[1], refs[-1]

        @pl.when(pl.program_id(1) == 0)
        def _():
            o_ref[...] = jnp.zeros_like(o_ref)

        o_ref[...] += _dot_tn(a_ref[...], b_ref[...])

    in_specs = [a_spec, b_spec]
    args = [A, B]
    aliases = None
    if prev is not None:
        in_specs.append(pl.BlockSpec(memory_space=pl.ANY))
        args.append(prev)
        aliases = {2: 0}
    return _call(body, jobs=jobs, name=name, grid=(ND, rows // _tm(rows)), in_specs=in_specs, out_specs=out_spec,
                 out_shape=_sds(out_shape), aliases=aliases)(*args)


def _proj_in(H2, WIN, name, jobs=()):
    R, D = H2.shape
    Nb = WIN.shape[-1]

    def body(h_ref, w_ref, p_ref):
        p_ref[...] = _dot(h_ref[...], w_ref[...])

    tm = _tm(R)
    return _call(body, jobs=jobs, name=name, grid=(ND, R // tm),
                 in_specs=[pl.BlockSpec((tm, D), lambda d, m: (m, 0)), pl.BlockSpec((None, D, Nb), lambda d, m: (d, 0, 0))],
                 out_specs=pl.BlockSpec((tm, Nb), lambda d, m: (m, d)), out_shape=_sds((R, ND * Nb)))(H2, WIN)


def _dproj_in(dP, WIN, name, jobs=()):
    R = dP.shape[0]
    D, Nb = WIN.shape[-2:]

    def body(dp_ref, w_ref, dh_ref, acc_ref):
        d = pl.program_id(1)

        @pl.when(d == 0)
        def _():
            acc_ref[...] = jnp.zeros_like(acc_ref)

        acc_ref[...] += _dot_nt(dp_ref[...], w_ref[...])

        @pl.when(d == ND - 1)
        def _():
            dh_ref[...] = acc_ref[...]

    tm = _tm(R)
    return _call(body, jobs=jobs, name=name, grid=(R // tm, ND),
                 in_specs=[pl.BlockSpec((tm, Nb), lambda m, d: (m, d)), pl.BlockSpec((None, D, Nb), lambda m, d: (d, 0, 0))],
                 out_specs=pl.BlockSpec((tm, D), lambda m, d: (m, 0)), out_shape=_sds((R, D)),
                 scratch=[pltpu.VMEM((tm, D), F32)])(dP, WIN)


def _proj_out(mixb, WOUT, X1, mod2, ng, S, name):
    D = WOUT.shape[0]

    def body(m_ref, w_ref, x_ref, mod_ref, g_ref, z_ref, xn_ref, h_ref):
        z = _dot(m_ref[...], w_ref[...])
        z_ref[...] = z
        xn = x_ref[...] + mod_ref[5:6, :] * z
        xn_ref[...] = xn
        n = (xn * lax.rsqrt(jnp.mean(xn * xn, axis=-1, keepdims=True) + EPS)) * g_ref[...]
        h_ref[...] = (n * (1.0 + mod_ref[7:8, :]) + mod_ref[6:7, :]).astype(h_ref.dtype)

    return _call(body, name=name, grid=(S // TR,),
                 in_specs=[_row_spec(D), _full_spec((D, D)), _row_spec(D), pl.BlockSpec((None, 9, D), lambda i: (0, 0, 0)),
                           _full_spec((1, D))],
                 out_specs=[_row_spec(D), _row_spec(D), _row_spec(D)],
                 out_shape=[_sds((S, D)), _sds((S, D)), _sds((S, D), _MXU)])(mixb, WOUT, X1, mod2, ng)


def _dproj_out(dZb, WOUT, name):
    S, D = dZb.shape

    def body(dz_ref, w_ref, dm_ref):
        dm_ref[...] = _dot_nt(dz_ref[...], w_ref[...])

    return _call(body, name=name, grid=(S // TR,), in_specs=[_row_spec(D), _full_spec((D, D))],
                 out_specs=_row_spec(D), out_shape=_sds((S, D)))(dZb, WOUT)


def _pair_swap(t):
    lane = lax.broadcasted_iota(jnp.int32, t.shape, 1)
    return jnp.where(lane % 2 == 0, pltpu.roll(t, HD - 1, 1), pltpu.roll(t, 1, 1))


SM_SCALE = HD ** -0.5


def _qkv_prep(P, qg, kg, COS, SIN, D, KVW, name):
    R = P.shape[0]
    W = D + 2 * KVW
    nq, nk = D // HD, KVW // HD

    def body(p_ref, qg_ref, kg_ref, cos_ref, sin_ref, q_ref, k_ref, v_ref):
        cos, sin = cos_ref[...], sin_ref[...]

        def head(t, g):
            y = (t * lax.rsqrt(jnp.mean(t * t, axis=-1, keepdims=True) + EPS)) * g
            return y * cos + _pair_swap(y) * sin

        for h in range(nq):
            q_ref[:, h * HD:(h + 1) * HD] = (head(p_ref[:, h * HD:(h + 1) * HD], qg_ref[...]) * SM_SCALE).astype(q_ref.dtype)
        for h in range(nk):
            k_ref[:, h * HD:(h + 1) * HD] = head(p_ref[:, D + h * HD:D + (h + 1) * HD], kg_ref[...]).astype(k_ref.dtype)
        v_ref[...] = p_ref[:, D + KVW:W].astype(v_ref.dtype)

    return _call(body, name=name, grid=(R // TR,),
                 in_specs=[_row_spec(W), _full_spec((1, HD)), _full_spec((1, HD)), _row_spec(HD), _row_spec(HD)],
                 out_specs=[_row_spec(D), _row_spec(KVW), _row_spec(KVW)],
                 out_shape=[_sds((R, D), _MXU), _sds((R, KVW), _MXU), _sds((R, KVW), _MXU)])(P, qg, kg, COS, SIN)


def _qkv_bwd(P, dq, dk, dv, qg, kg, COS, SIN, dP, D, KVW, nx, name):
    R, INW = P.shape
    W = D + 2 * KVW
    nq, nk = D // HD, KVW // HD

    def body(p_ref, dq_ref, dk_ref, dv_ref, qg_ref, kg_ref, cos_ref, sin_ref, dp_in, dp_ref, dqg_ref, dkg_ref):
        i = pl.program_id(0)
        cos, sin = cos_ref[...], sin_ref[...]

        @pl.when(i == 0)
        def _():
            dqg_ref[...] = jnp.zeros_like(dqg_ref)
            dkg_ref[...] = jnp.zeros_like(dkg_ref)

        def head_bwd(t, g, dout):
            r = lax.rsqrt(jnp.mean(t * t, axis=-1, keepdims=True) + EPS)
            n = t * r
            dy = dout * cos + _pair_swap(dout * sin)
            dn = dy * g
            return r * (dn - n * jnp.mean(dn * n, axis=-1, keepdims=True)), jnp.sum(dy * n, axis=0, keepdims=True)

        dqg = jnp.zeros((1, HD), F32)
        for h in range(nq):
            sl = slice(h * HD, (h + 1) * HD)
            dt, dg = head_bwd(p_ref[:, sl], qg_ref[...], jnp.where(i < nx, dq_ref[:, sl] * SM_SCALE, 0.0))
            dp_ref[:, sl] = dt.astype(dp_ref.dtype)
            dqg += dg
        dkg = jnp.zeros((1, HD), F32)
        for h in range(nk):
            sl = slice(h * HD, (h + 1) * HD)
            dt, dg = head_bwd(p_ref[:, D + h * HD:D + (h + 1) * HD], kg_ref[...], dk_ref[:, sl])
            dp_ref[:, D + h * HD:D + (h + 1) * HD] = dt.astype(dp_ref.dtype)
            dkg += dg
        dqg_ref[...] += dqg
        dkg_ref[...] += dkg
        dp_ref[:, D + KVW:W] = dv_ref[...].astype(dp_ref.dtype)

    return _call(body, name=name, grid=(R // TR,),
                 in_specs=[_row_spec(W), _row_spec(D, nx - 1), _row_spec(KVW), _row_spec(KVW), _full_spec((1, HD)),
                           _full_spec((1, HD)), _row_spec(HD), _row_spec(HD), pl.BlockSpec(memory_space=pl.ANY)],
                 out_specs=[_row_spec(W), _full_spec((1, HD)), _full_spec((1, HD))],
                 out_shape=[_sds((R, INW), _MXU), _sds((1, HD)), _sds((1, HD))],
                 aliases={8: 0})(P, dq, dk, dv, qg, kg, COS, SIN, dP)


def _stack_heads(ref, G, dtype=None):
    parts = [ref[:, g * HD:(g + 1) * HD] for g in range(G)]
    out = jnp.concatenate(parts, axis=0)
    return out if dtype is None else out.astype(dtype)


_KEY_CHUNKS = 4


def _key_chunks(R):
    unit = 256 if R % 256 == 0 else 16
    nt = R // unit
    n = min(_KEY_CHUNKS, nt)
    bounds = [0]
    for i in range(n):
        bounds.append(bounds[-1] + (nt // n + (1 if i < nt % n else 0)) * unit)
    return bounds


def _attn_fwd(q, k, v, S, G, name, jobs=()):
    R, KVW = k.shape
    D = q.shape[1]
    Kh = KVW // HD
    tq = 128
    kb = _key_chunks(R)

    def body(q_ref, k_ref, v_ref, o_ref, lse_ref):
        qs = _stack_heads(q_ref, G)
        m = l = acc = None
        for c in range(len(kb) - 1):
            s = _dot_nt(qs, k_ref[kb[c]:kb[c + 1], :])
            mc = jnp.max(s, axis=-1, keepdims=True)
            m_new = mc if c == 0 else jnp.maximum(m, mc)
            p = jnp.exp(s - m_new)
            ps = jnp.sum(p, axis=-1, keepdims=True)
            pv = _dot(p, v_ref[kb[c]:kb[c + 1], :])
            if c == 0:
                l, acc = ps, pv
            else:
                alpha = jnp.exp(m - m_new)
                l = alpha * l + ps
                acc = alpha * acc + pv
            m = m_new
        o = acc / l
        lse = m + jnp.log(l)
        for g in range(G):
            o_ref[:, g * HD:(g + 1) * HD] = o[g * tq:(g + 1) * tq, :]
            lse_ref[g] = jnp.broadcast_to(lse[g * tq:(g + 1) * tq, :], (tq, HD))

    return _call(body, jobs=jobs, name=name, grid=(Kh, S // tq),
                 in_specs=[pl.BlockSpec((tq, G * HD), lambda h, i: (i, h)), pl.BlockSpec((R, HD), lambda h, i: (0, h)),
                           pl.BlockSpec((R, HD), lambda h, i: (0, h))],
                 out_specs=[pl.BlockSpec((tq, G * HD), lambda h, i: (i, h)),
                            pl.BlockSpec((None, G, tq, HD), lambda h, i: (h, 0, i, 0))],
                 out_shape=[_sds((S, D)), _sds((Kh, G, S, HD))])(q, k, v)


def _attn_bwd(q, k, v, O, LSE, dOb, S, G, name, jobs=()):
    R, KVW = k.shape
    D = q.shape[1]
    Kh = KVW // HD
    tq = 128
    kb = _key_chunks(R)

    def body(q_ref, k_ref, v_ref, o_ref, lse_ref, do_ref, dq_ref, dk_ref, dv_ref):
        @pl.when(pl.program_id(1) == 0)
        def _():
            dk_ref[...] = jnp.zeros_like(dk_ref)
            dv_ref[...] = jnp.zeros_like(dv_ref)

        qs = _stack_heads(q_ref, G)
        do = _stack_heads(do_ref, G)
        o = _stack_heads(o_ref, G)
        lse = jnp.concatenate([lse_ref[g][:, 0:1] for g in range(G)], axis=0)
        delta = jnp.sum(do.astype(F32) * o, axis=-1, keepdims=True)
        dq = None
        for c in range(len(kb) - 1):
            rows = slice(kb[c], kb[c + 1])
            kk = k_ref[rows, :]
            p = jnp.exp(_dot_nt(qs, kk) - lse)
            dp = _dot_nt(do, v_ref[rows, :])
            ds = (p * (dp - delta)).astype(_MXU)
            dqc = _dot(ds, kk)
            dq = dqc if c == 0 else dq + dqc
            dk_ref[rows, :] += _dot_tn(ds, qs)
            dv_ref[rows, :] += _dot_tn(p, do)
        for g in range(G):
            dq_ref[:, g * HD:(g + 1) * HD] = dq[g * tq:(g + 1) * tq, :]

    qspec = pl.BlockSpec((tq, G * HD), lambda h, i: (i, h))
    kspec = pl.BlockSpec((R, HD), lambda h, i: (0, h))
    return _call(body, jobs=jobs, name=name, grid=(Kh, S // tq),
                 in_specs=[qspec, kspec, kspec, qspec, pl.BlockSpec((None, G, tq, HD), lambda h, i: (h, 0, i, 0)), qspec],
                 out_specs=[qspec, kspec, kspec],
                 out_shape=[_sds((S, D)), _sds((R, KVW)), _sds((R, KVW))])(q, k, v, O, LSE, dOb)


def _halo_specs(R, CB, col0):
    nt8 = TR // 8
    return [pl.BlockSpec((8, CB), lambda h, i: (jnp.maximum(i * nt8 - 1, 0), col0 + h)),
            pl.BlockSpec((TR, CB), lambda h, i: (i, col0 + h)),
            pl.BlockSpec((8, CB), lambda h, i: (jnp.minimum((i + 1) * nt8, R // 8 - 1), col0 + h))]


def _seq_pos(i, S, R, CB):
    t = i * TR - 8 + lax.broadcasted_iota(jnp.int32, (TR + 16, CB), 0)
    start = jnp.where(t >= S, S, 0)
    end = jnp.where(t >= S, R, S)
    return t - start, end - t


def _shift(cat, by):
    return pltpu.roll(cat, by % cat.shape[0], 0)


def _gate_mats(xcb, w_ref, dirn, nb):
    return jnp.concatenate([_dot(xcb[:, b * HD:(b + 1) * HD], w_ref[dirn, b]) for b in range(nb)], axis=1)


def _lru_gates_fwd(P, conv_w, conv_b, WA, WX, ba, bx, lam, S, D, col0, name):
    R = P.shape[0]
    CB = D // 2
    nb = CB // HD

    def body(xp_ref, x_ref, xn_ref, cw_ref, cb_ref, wa_ref, wx_ref, ba_ref, bx_ref, lam_ref,
             xc_ref, af_ref, uf_ref, ab_ref, ub_ref):
        i = pl.program_id(1)
        cat = jnp.concatenate([xp_ref[...], x_ref[...], xn_ref[...]], axis=0)
        from_start, to_end = _seq_pos(i, S, R, CB)
        conv = (cb_ref[...] + cw_ref[2:3, :] * cat
                + cw_ref[0:1, :] * jnp.where(from_start >= 2, _shift(cat, 2), 0.0)
                + cw_ref[1:2, :] * jnp.where(from_start >= 1, _shift(cat, 1), 0.0)
                + cw_ref[3:4, :] * jnp.where(to_end >= 2, _shift(cat, -1), 0.0))
        xc = conv[8:8 + TR, :]
        xc_ref[...] = xc
        xcb = xc.astype(_MXU)
        for dirn, (a_ref, u_ref) in enumerate(((af_ref, uf_ref), (ab_ref, ub_ref))):
            ra = _sigmoid(_gate_mats(xcb, wa_ref, dirn, nb) + ba_ref[dirn:dirn + 1, :])
            ia = _sigmoid(_gate_mats(xcb, wx_ref, dirn, nb) + bx_ref[dirn:dirn + 1, :])
            nl = -lam_ref[dirn:dirn + 1, :]
            sp = jnp.maximum(nl, 0.0) + jnp.log(1.0 + jnp.exp(-jnp.abs(nl)))
            la = (-LRU_C) * ra * sp
            a_ref[...] = jnp.exp(la)
            u_ref[...] = jnp.sqrt(1.0 - jnp.exp(2.0 * la)) * (ia * xc)

    def par(r):
        return pl.BlockSpec((r, CB), lambda h, i: (0, h))

    wspec = pl.BlockSpec((2, nb, HD, HD), lambda h, i: (0, h, 0, 0))
    out = pl.BlockSpec((TR, CB), lambda h, i: (i, h))
    return _call(body, name=name, grid=(2, R // TR),
                 in_specs=_halo_specs(R, CB, col0) + [par(4), par(1), wspec, wspec, par(2), par(2), par(2)],
                 out_specs=[out] * 5, out_shape=[_sds((R, D))] * 5,
                 )(P, P, P, conv_w, conv_b, WA, WX, ba, bx, lam)


def _lru_gates_bwd(xc, hf, hb, gf, gb, WA, WX, ba, bx, lam, S, name):
    R, D = xc.shape
    CB = D // 2
    nb = CB // HD

    def body(xc_ref, hfq_ref, hf_ref, hfl_ref, hb_ref, hbn_ref, gf_ref, gb_ref, wa_ref, wx_ref, ba_ref, bx_ref, lam_ref,
             dxc_ref, dwa_ref, dwx_ref, dba_ref, dbx_ref, dlam_ref):
        @pl.when(pl.program_id(1) == 0)
        def _():
            for r in (dwa_ref, dwx_ref, dba_ref, dbx_ref, dlam_ref):
                r[...] = jnp.zeros_like(r)

        t = pl.program_id(1) * TR + lax.broadcasted_iota(jnp.int32, (TR, CB), 0)
        hfp = _shift(jnp.concatenate([hfq_ref[...], hf_ref[...]], axis=0), 1)[8:8 + TR, :]
        hfp = jnp.where(t == 0, hfl_ref[7:8, :], jnp.where(t == S, 0.0, hfp))
        hbp = _shift(jnp.concatenate([hb_ref[...], hbn_ref[...]], axis=0), -1)[0:TR, :]
        hbp = jnp.where(t == R - 1, 0.0, hbp)

        xc = xc_ref[...]
        xcb = xc.astype(_MXU)
        dxc = jnp.zeros_like(xc)
        for dirn, (hp, g_ref) in enumerate(((hfp, gf_ref), (hbp, gb_ref))):
            ra = _sigmoid(_gate_mats(xcb, wa_ref, dirn, nb) + ba_ref[dirn:dirn + 1, :])
            ia = _sigmoid(_gate_mats(xcb, wx_ref, dirn, nb) + bx_ref[dirn:dirn + 1, :])
            nl = -lam_ref[dirn:dirn + 1, :]
            sp = jnp.maximum(nl, 0.0) + jnp.log(1.0 + jnp.exp(-jnp.abs(nl)))
            la = (-LRU_C) * ra * sp
            a = jnp.exp(la)
            e2 = jnp.exp(2.0 * la)
            s = jnp.sqrt(1.0 - e2)
            du = g_ref[...]
            dla = du * hp * a - du * (ia * xc) * (e2 / s)
            dxc += du * s * ia
            dza = (dla * (-LRU_C) * sp) * ra * (1.0 - ra)
            dzx = (du * s * xc) * ia * (1.0 - ia)
            dlam_ref[dirn:dirn + 1, :] += jnp.sum(dla * (LRU_C * ra) * _sigmoid(nl), axis=0, keepdims=True)
            dba_ref[dirn:dirn + 1, :] += jnp.sum(dza, axis=0, keepdims=True)
            dbx_ref[dirn:dirn + 1, :] += jnp.sum(dzx, axis=0, keepdims=True)
            dzab, dzxb = dza.astype(_MXU), dzx.astype(_MXU)
            parts = []
            for b in range(nb):
                sl = slice(b * HD, (b + 1) * HD)
                dwa_ref[dirn, b] += _dot_tn(xcb[:, sl], dzab[:, sl])
                dwx_ref[dirn, b] += _dot_tn(xcb[:, sl], dzxb[:, sl])
                parts.append(_dot_nt(dzab[:, sl], wa_ref[dirn, b]) + _dot_nt(dzxb[:, sl], wx_ref[dirn, b]))
            dxc += jnp.concatenate(parts, axis=1)
        dxc_ref[...] = dxc

    def par(r):
        return pl.BlockSpec((r, CB), lambda h, i: (0, h))

    wspec = pl.BlockSpec((2, nb, HD, HD), lambda h, i: (0, h, 0, 0))
    tile = pl.BlockSpec((TR, CB), lambda h, i: (i, h))
    before, _, after = _halo_specs(R, CB, 0)
    last = pl.BlockSpec((8, CB), lambda h, i: (R // 8 - 1, h))
    nbt = D // HD
    return _call(body, name=name, grid=(2, R // TR),
                 in_specs=[tile, before, tile, last, tile, after, tile, tile, wspec, wspec, par(2), par(2), par(2)],
                 out_specs=[tile, wspec, wspec, par(2), par(2), par(2)],
                 out_shape=[_sds((R, D)), _sds((2, nbt, HD, HD)), _sds((2, nbt, HD, HD)), _sds((2, D)), _sds((2, D)),
                            _sds((2, D))])(xc, hf, hf, hf, hb, hb, gf, gb, WA, WX, ba, bx, lam)


def _scan_rows(n_groups, step, init):
    return lax.fori_loop(0, n_groups, lambda gi, c: step(pl.multiple_of(gi * 8, 8), c), init)


def _lru_scan_fwd(af, uf, ab, ub, S, name):
    R, D = af.shape
    W = min(SCAN_W, D)
    nm, nx = R // TR, S // TR
    nc = nm - nx
    ng = TR // 8

    def body(af_ref, uf_ref, ab_ref, ub_ref, hf_ref, hb_ref, cf_ref, cb_ref):
        @pl.when(pl.program_id(1) == 0)
        def _():
            cf_ref[...] = jnp.zeros_like(cf_ref)
            cb_ref[...] = jnp.zeros_like(cb_ref)

        def step(base, carry):
            hf, hb = carry
            baseb = pl.multiple_of(TR - 8 - base, 8)
            for r in range(8):
                tf, tb = base + r, baseb + 7 - r
                hf = af_ref[pl.ds(tf, 1), :] * hf + uf_ref[pl.ds(tf, 1), :]
                hf_ref[pl.ds(tf, 1), :] = hf
                hb = ab_ref[pl.ds(tb, 1), :] * hb + ub_ref[pl.ds(tb, 1), :]
                hb_ref[pl.ds(tb, 1), :] = hb
            return hf, hb

        hf, hb = _scan_rows(ng, step, (cf_ref[0:1, :], cb_ref[0:1, :]))
        cf_ref[0:1, :] = hf
        cb_ref[0:1, :] = hb

    fmap = lambda j, s: (jnp.where(s < nc, nx + s, s - nc), j)
    bmap = lambda j, s: (nm - 1 - s, j)
    fs, bs = pl.BlockSpec((TR, W), fmap), pl.BlockSpec((TR, W), bmap)
    return _call(body, name=name, grid=(D // W, nm), in_specs=[fs, fs, bs, bs], out_specs=[fs, bs],
                 out_shape=[_sds((R, D))] * 2, scratch=[pltpu.VMEM((8, W), F32), pltpu.VMEM((8, W), F32)])(af, uf, ab, ub)


def _lru_scan_bwd(af, ab, dhs, S, name):
    R, D = af.shape
    W = min(SCAN_W, D)
    nm, nx = R // TR, S // TR
    ng = TR // 8

    def body(af_ref, dhf_ref, ab_ref, dhb_ref, gf_ref, gb_ref, cf_ref, cb_ref):
        @pl.when(pl.program_id(1) == 0)
        def _():
            cf_ref[...] = jnp.zeros_like(cf_ref)
            cb_ref[...] = jnp.zeros_like(cb_ref)

        def step(base, carry):
            cf, cb = carry
            based = pl.multiple_of(TR - 8 - base, 8)
            for r in range(8):
                tf, tb = based + 7 - r, base + r
                g = dhf_ref[pl.ds(tf, 1), :] + cf
                gf_ref[pl.ds(tf, 1), :] = g
                cf = af_ref[pl.ds(tf, 1), :] * g
                g = dhb_ref[pl.ds(tb, 1), :] + cb
                gb_ref[pl.ds(tb, 1), :] = g
                cb = ab_ref[pl.ds(tb, 1), :] * g
            return cf, cb

        cf, cb = _scan_rows(ng, step, (cf_ref[0:1, :], cb_ref[0:1, :]))
        cf_ref[0:1, :] = cf
        cb_ref[0:1, :] = cb

    fmap = lambda j, s: (jnp.where(s < nx, nx - 1 - s, nm - 1 - (s - nx)), j)
    bmap = lambda j, s: (s, j)
    fs, bs = pl.BlockSpec((TR, W), fmap), pl.BlockSpec((TR, W), bmap)
    return _call(body, name=name, grid=(D // W, nm), in_specs=[fs, fs, bs, bs], out_specs=[fs, bs],
                 out_shape=[_sds((R, D))] * 2, scratch=[pltpu.VMEM((8, W), F32), pltpu.VMEM((8, W), F32)])(af, dhs, ab, dhs)


def _merge_fwd(P, hf, hb, O, S, D, col_lg, name):
    R = P.shape[0]
    CB = D // 2
    nx = S // TR

    def body(lg_ref, ga_ref, gl_ref, hf_ref, hb_ref, o_ref, mix_ref):
        ge, _ = _gelu_and_grad(lg_ref[...])
        lru = (hf_ref[...] + hb_ref[...]) * ge
        mix_ref[...] = (_sigmoid(ga_ref[...]) * o_ref[...] + _sigmoid(gl_ref[...]) * lru).astype(mix_ref.dtype)

    def col(c0):
        return pl.BlockSpec((TR, CB), lambda h, i: (i, c0 + h))

    return _call(body, name=name, grid=(2, R // TR),
                 in_specs=[col(col_lg), col(col_lg + 2), col(col_lg + 4), col(0), col(0),
                           pl.BlockSpec((TR, CB), lambda h, i: (jnp.minimum(i, nx - 1), h))],
                 out_specs=col(0), out_shape=_sds((R, D), _MXU))(P, P, P, hf, hb, O)


def _merge_bwd(dmix, P, hf, hb, O, S, D, col_lg, name, jobs=()):
    R = P.shape[0]
    CB = D // 2
    nx = S // TR

    def body(dm_ref, lg_ref, ga_ref, gl_ref, hf_ref, hb_ref, o_ref, do_ref, dhs_ref, dp_ref, stash, sems):
        h, i = pl.program_id(0), pl.program_id(1)
        dm = jnp.where(i < nx, dm_ref[...], 0.0)
        sa, sl = _sigmoid(ga_ref[...]), _sigmoid(gl_ref[...])
        ge, dge = _gelu_and_grad(lg_ref[...])
        hs = hf_ref[...] + hb_ref[...]
        dl = dm * sl
        do_ref[...] = (dm * sa).astype(do_ref.dtype)
        dhs_ref[...] = dl * ge
        stash[0] = (dl * hs * dge).astype(stash.dtype)
        stash[1] = (dm * o_ref[...] * sa * (1.0 - sa)).astype(stash.dtype)
        stash[2] = (dm * (hs * ge) * sl * (1.0 - sl)).astype(stash.dtype)
        rows = pl.ds(pl.multiple_of(i * TR, TR), TR)
        copies = [pltpu.make_async_copy(stash.at[sec], dp_ref.at[rows, pl.ds(pl.multiple_of((col_lg + 2 * sec + h) * CB, CB), CB)],
                                        sems.at[sec]) for sec in range(3)]
        for cp in copies:
            cp.start()
        for cp in copies:
            cp.wait()

    def col(c0):
        return pl.BlockSpec((TR, CB), lambda h, i: (i, c0 + h))

    xrow = pl.BlockSpec((TR, CB), lambda h, i: (jnp.minimum(i, nx - 1), h))
    return _call(body, jobs=jobs, name=name, grid=(2, R // TR),
                 in_specs=[xrow, col(col_lg), col(col_lg + 2), col(col_lg + 4), col(0), col(0), xrow],
                 out_specs=[col(0), col(0), pl.BlockSpec(memory_space=pl.ANY)],
                 out_shape=[_sds((R, D), _MXU), _sds((R, D)), _sds(P.shape, _MXU)],
                 scratch=[pltpu.VMEM((3, TR, CB), _MXU), pltpu.SemaphoreType.DMA((3,))])(dmix, P, P, P, hf, hb, O)


def _conv_bwd(dxc, P, conv_w, dP, S, D, col0, name, jobs=()):
    R = P.shape[0]
    CB = D // 2

    def body(dp_, d_ref, dn_, xp_ref, x_ref, xn_ref, cw_ref, dp_in, dpo_ref, dcw_ref, dcb_ref):
        i = pl.program_id(1)

        @pl.when(i == 0)
        def _():
            dcw_ref[...] = jnp.zeros_like(dcw_ref)
            dcb_ref[...] = jnp.zeros_like(dcb_ref)

        d = d_ref[...]
        catd = jnp.concatenate([dp_[...], d, dn_[...]], axis=0)
        catx = jnp.concatenate([xp_ref[...], x_ref[...], xn_ref[...]], axis=0)
        from_start, to_end = _seq_pos(i, S, R, CB)
        dxl = (cw_ref[2:3, :] * catd
               + cw_ref[0:1, :] * jnp.where(to_end >= 3, _shift(catd, -2), 0.0)
               + cw_ref[1:2, :] * jnp.where(to_end >= 2, _shift(catd, -1), 0.0)
               + cw_ref[3:4, :] * jnp.where(from_start >= 1, _shift(catd, 1), 0.0))
        dpo_ref[...] = dxl[8:8 + TR, :].astype(dpo_ref.dtype)
        taps = (jnp.where(from_start >= 2, _shift(catx, 2), 0.0), jnp.where(from_start >= 1, _shift(catx, 1), 0.0),
                catx, jnp.where(to_end >= 2, _shift(catx, -1), 0.0))
        for kk in range(4):
            dcw_ref[kk:kk + 1, :] += jnp.sum(d * taps[kk][8:8 + TR, :], axis=0, keepdims=True)
        dcb_ref[...] += jnp.sum(d, axis=0, keepdims=True)

    return _call(body, jobs=jobs, name=name, grid=(2, R // TR),
                 in_specs=_halo_specs(R, CB, 0) + _halo_specs(R, CB, col0)
                 + [pl.BlockSpec((4, CB), lambda h, i: (0, h)), pl.BlockSpec(memory_space=pl.ANY)],
                 out_specs=[pl.BlockSpec((TR, CB), lambda h, i: (i, col0 + h)), pl.BlockSpec((4, CB), lambda h, i: (0, h)),
                            pl.BlockSpec((1, CB), lambda h, i: (0, h))],
                 out_shape=[_sds(dP.shape, dP.dtype), _sds((4, D)), _sds((1, D))],
                 aliases={7: 0})(dxc, dxc, dxc, P, P, P, conv_w, dP)


def _rope_tables(S, C):
    t = jnp.arange(S, dtype=jnp.int32)
    row = (t // GRID_W).astype(F32)
    col = (t % GRID_W).astype(F32)
    axis_dims = HD // 2
    freqs = ROPE_THETA ** (-jnp.arange(0, axis_dims, 2, dtype=F32) / axis_dims)
    ang = jnp.concatenate([row[:, None] * freqs, col[:, None] * freqs], axis=-1)
    cos = jnp.repeat(jnp.cos(ang), 2, axis=-1)
    sin = jnp.repeat(jnp.sin(ang), 2, axis=-1) * jnp.tile(jnp.array([-1.0, 1.0], F32), HD // 2)
    return (jnp.concatenate([cos, jnp.ones((C, HD), F32)], axis=0),
            jnp.concatenate([sin, jnp.zeros((C, HD), F32)], axis=0))


def _local_step(x, ctx, target, modx, modc, ng, shards, qg, kg, conv_w, conv_b, WA, WX, ba, bx, lam, fg, idx, opt):
    S, D = x.shape
    C = ctx.shape[0]
    R = S + C
    nx = S // TR
    Nb = shards['w_in'].shape[-1]
    Fb = shards['wd0'].shape[1]
    KVW = (ND * Nb - 5 * D) // 2
    G = D // KVW
    CB = D // 2
    col_lx = (D + 2 * KVW) // CB
    assert S % TR == 0 and C % TR == 0 and (D + 2 * KVW) % CB == 0 and CB % HD == 0
    mod2 = jnp.stack([modx, modc])
    X0 = jnp.concatenate([x, ctx], axis=0)
    COS, SIN = _rope_tables(S, C)
    ng0, ng1, ng2 = ng[0:1], ng[1:2], ng[2:3]
    ag = lambda n: _ag_job(shards[n])
    sib = lambda Gp: _rs_sibling_job(Gp.reshape(ND, -1, Gp.shape[-1]))
    add = lambda Gp, bufA, tag: _rs_add(Gp.reshape(ND, -1, Gp.shape[-1]), bufA, idx, f"{tag}_rs_add")
    out = {}

    def tn_specs(rows):
        tm = _tm(rows)
        return pl.BlockSpec((None, tm, Fb), lambda d, r: (d, r, 0)), pl.BlockSpec((tm, D), lambda d, r: (r, 0))

    wd_spec = pl.BlockSpec((None, None, Fb, D), lambda d, r: (d, 0, 0, 0))
    wg_spec = pl.BlockSpec((None, None, D, Fb), lambda d, r: (d, 0, 0, 0))

    ((WG0,),) = _comm_call([ag('wg0')], "ag_wg0")
    H1 = _norm_mod_fwd(X0, ng0, mod2, 0, R, nx, "ffn1_norm")
    G1, ((WU0,),) = _ffn_gate(H1, WG0, R, "ffn1_gate", jobs=[ag('wu0')])
    (U1, A1), ((WD0,),) = _ffn_up(H1, None, WU0, R, "ffn1_up", jobs=[ag('wd0')], G=G1)
    (Y1, X1, H2), ((WIN,),) = _ffn_down(A1, WD0, 0, X0, mod2, 0, R, S, "ffn1_down", jobs=[ag('w_in')], next_norm=(ng1, 1))
    P, ((WOUT,), (WG1,)) = _proj_in(H2, WIN, "proj_in", jobs=[ag('w_out'), ag('wg1')])
    q, k, v = _qkv_prep(P, qg, kg, COS, SIN, D, KVW, "qkv_prep")
    (O, LSE), ((WU1,), (WD1,)) = _attn_fwd(q, k, v, S, G, "attn_fwd", jobs=[ag('wu1'), ag('wd1')])
    WOUT = WOUT.reshape(D, D)
    xc, af, uf, ab, ub = _lru_gates_fwd(P, conv_w, conv_b, WA, WX, ba, bx, lam, S, D, col_lx, "lru_gates")
    hf, hb = _lru_scan_fwd(af, uf, ab, ub, S, "lru_scan")
    mixb = _merge_fwd(P, hf, hb, O, S, D, col_lx + 2, "merge")
    Z, X2, H3 = _proj_out(mixb, WOUT, X1, mod2, ng2, S, "proj_out")
    G3, U3, A3 = _ffn_up(H3, WG1, WU1, S, "ffn2_up")
    loss, dX3, dY3b, dg3, dfg = _ffn_down_loss(A3, WD1, X2, mod2, 2, fg, target, "ffn2_down_loss")
    dg3 = dg3[None]

    dG3, dU3 = _ffn_dact(dY3b, WD1, 0, G3, U3, S, "ffn2_dact")
    blk, row = tn_specs(S)
    dWD1 = _mm_tn(A3, blk, dY3b, row, (ND, 1, Fb, D), wd_spec, S, "ffn2_dwd")
    dWG1 = _mm_tn(H3, row, dG3, blk, (ND, 1, D, Fb), wg_spec, S, "ffn2_dwg")
    dWU1 = _mm_tn(H3, row, dU3, blk, (ND, 1, D, Fb), wg_spec, S, "ffn2_dwu")
    dH3, ((a_wd1,), (a_wg1,), (a_wu1,)) = _ffn_dh(dG3, dU3, WG1, WU1, S, "ffn2_dh", jobs=[sib(dWD1), sib(dWG1), sib(dWU1)])
    T_wd1, own_wd1 = add(dWD1, a_wd1, "wd1")
    T_wg1, own_wg1 = add(dWG1, a_wg1, "wg1")
    T_wu1, own_wu1 = add(dWU1, a_wu1, "wu1")
    dX2, dsh3, dsc3, dng2, dZb, dg2 = _norm_mod_bwd(X2, dH3, dX3, ng2, mod2, 2, S, nx, nx, "ffn2_dnorm", branch=(Z, 1, 1.0))

    dmix = _dproj_out(dZb, WOUT, "dproj_out")
    dWOUT = _mm_tn(mixb, pl.BlockSpec((_tm(S), D // ND), lambda d, r: (r, d)), dZb, pl.BlockSpec((_tm(S), D), lambda d, r: (r, 0)),
                   (ND, D // ND, D), pl.BlockSpec((None, D // ND, D), lambda d, r: (d, 0, 0)), S, "dw_out")
    (dOb, dhs, dP), ((a_wout,),) = _merge_bwd(dmix, P, hf, hb, O, S, D, col_lx + 2, "merge_bwd", jobs=[sib(dWOUT)])
    T_wout, own_wout = add(dWOUT, a_wout, "w_out")
    gf, gb = _lru_scan_bwd(af, ab, dhs, S, "lru_scan_bwd")
    dxc, dWA, dWX, dba, dbx, dlam = _lru_gates_bwd(xc, hf, hb, gf, gb, WA, WX, ba, bx, lam, S, "lru_gates_bwd")
    dLW = jnp.stack([dWA, dWX]).reshape(ND, -1, HD)
    (dP, dconv_w, dconv_b), ((a_lw,),) = _conv_bwd(dxc, P, conv_w, dP, S, D, col_lx, "conv_bwd", jobs=[sib(dLW)])
    T_lw, own_lw = add(dLW, a_lw, "lru_w")
    (dq, dk, dv), ((b_wd1,), (b_wg1,), (b_wu1,), (b_wout,), (b_lw,)) = _attn_bwd(
        q, k, v, O, LSE, dOb, S, G, "attn_bwd",
        jobs=[_rs_chips_job(T_wd1), _rs_chips_job(T_wg1), _rs_chips_job(T_wu1), _rs_chips_job(T_wout), _rs_chips_job(T_lw)])
    fin_wd = _rs_finish(own_wd1, b_wd1, "wd1_rs_finish", opt['ffn_wd'], 1, 2)
    fin_wg = _rs_finish(own_wg1, b_wg1, "wg1_rs_finish", opt['ffn_wg'], 1, 2)
    fin_wu = _rs_finish(own_wu1, b_wu1, "wu1_rs_finish", opt['ffn_wu'], 1, 2)
    out['w_out'] = _rs_finish(own_wout, b_wout, "w_out_rs_finish", opt['w_out'])
    (lw_sum,) = _rs_finish(own_lw, b_lw, "lru_w_rs_finish")
    dP, dqg, dkg = _qkv_bwd(P, dq, dk, dv, qg, kg, COS, SIN, dP, D, KVW, nx, "qkv_bwd")
    dH2, ((lw_full,),) = _dproj_in(dP, WIN, "dproj_in", jobs=[_ag_job(lw_sum)])
    dWIN = _mm_tn(H2, pl.BlockSpec((_tm(R), D), lambda d, r: (r, 0)), dP, pl.BlockSpec((_tm(R), Nb), lambda d, r: (r, d)),
                  (ND, D, Nb), pl.BlockSpec((None, D, Nb), lambda d, r: (d, 0, 0)), R, "dw_in")
    dX1, dsh2, dsc2, dng1, dY1b, dg1 = _norm_mod_bwd(X1, dH2, dX2, ng1, mod2, 1, R, nx, nx, "mix_dnorm", branch=(Y1, 0, 0.5))

    (dG1, dU1), ((a_win,),) = _ffn_dact(dY1b, WD0, 0, G1, U1, R, "ffn1_dact", jobs=[sib(dWIN)])
    T_win, own_win = add(dWIN, a_win, "w_in")
    blk, row = tn_specs(R)
    dWD0, ((b_win01,),) = _mm_tn(A1, blk, dY1b, row, (ND, 1, Fb, D), wd_spec, R, "ffn1_dwd", jobs=[_rs_chips_job(T_win, (0, 1))])
    dWG0, ((b_win2,), (a_wd0,)) = _mm_tn(H1, row, dG1, blk, (ND, 1, D, Fb), wg_spec, R, "ffn1_dwg",
                                         jobs=[_rs_chips_job(T_win, (2,)), sib(dWD0)])
    out['w_in'] = _rs_finish(own_win, [(b_win01, 0), (b_win01, 1), (b_win2, 0)], "w_in_rs_finish", opt['w_in'])
    T_wd0, own_wd0 = add(dWD0, a_wd0, "wd0")
    dWU0, ((a_wg0,), (b_wd0,)) = _mm_tn(H1, row, dU1, blk, (ND, 1, D, Fb), wg_spec, R, "ffn1_dwu",
                                        jobs=[sib(dWG0), _rs_chips_job(T_wd0)])
    T_wg0, own_wg0 = add(dWG0, a_wg0, "wg0")
    out['ffn_wd'] = _rs_finish(own_wd0, b_wd0, "wd0_rs_finish", opt['ffn_wd'], 0, 2, fin_wd)
    dH1, ((a_wu0,), (b_wg0,)) = _ffn_dh(dG1, dU1, WG0, WU0, R, "ffn1_dh", jobs=[sib(dWU0), _rs_chips_job(T_wg0)])
    T_wu0, own_wu0 = add(dWU0, a_wu0, "wu0")
    out['ffn_wg'] = _rs_finish(own_wg0, b_wg0, "wg0_rs_finish", opt['ffn_wg'], 0, 2, fin_wg)
    (dX0, dsh1, dsc1, dng0), ((b_wu0,),) = _norm_mod_bwd(
        X0, dH1, dX1, ng0, mod2, 0, R, nx, R // TR, "ffn1_dnorm", jobs=[_rs_chips_job(T_wu0)])
    out['ffn_wu'] = _rs_finish(own_wu0, b_wu0, "wu0_rs_finish", opt['ffn_wu'], 0, 2, fin_wu)
    out['lru_w'] = lw_full

    zero = jnp.zeros((1, D), F32)
    dmodx = jnp.concatenate([dsh1[0], dsc1[0], dg1[0], dsh2[0], dsc2[0], dg2[0], dsh3[0], dsc3[0], dg3[0]], axis=0)
    dmodc = jnp.concatenate([dsh1[1], dsc1[1], dg1[1], dsh2[1], dsc2[1], zero, zero, zero, zero], axis=0)
    out.update(loss=loss, grad_x=dX0[:S], dmodx=dmodx, dmodc=dmodc, norm_g=jnp.concatenate([dng0, dng1, dng2], axis=0),
               q_norm_g=dqg, k_norm_g=dkg, conv_w=dconv_w, conv_b=dconv_b, lru_ba=dba, lru_bx=dbx,
               lru_lambda=dlam, final_norm_g=dfg)
    return out


def _mesh_pos():
    return lax.axis_index("x"), lax.axis_index("y"), lax.axis_index("c")


def _all_gather(xb, name, in_vmem=False):
    space = pltpu.VMEM if in_vmem else pl.ANY

    def body(x_ref, out_ref, send_sems, recv_sems, local_sem):
        x, y, c = _mesh_pos()
        me, sibling = (x, y, c), (x, y, 1 - c)
        chips = [(1 - x, y), (x, 1 - y), (1 - x, 1 - y)]

        def slot(px, py, pc):
            return out_ref.at[4 * px + 2 * py + pc]

        def copy(k, block, to, src=None):
            return pltpu.make_async_remote_copy(
                src_ref=slot(*block) if src is None else src, dst_ref=slot(*block),
                send_sem=send_sems.at[k], recv_sem=recv_sems.at[k], device_id=to, device_id_type=MESH)

        mine = pltpu.make_async_copy(x_ref, slot(*me), local_sem)
        mine.start()
        first = [copy(0, me, sibling, src=x_ref)]
        first += [copy(1 + j, me, (*chip, c), src=x_ref) for j, chip in enumerate(chips)]
        for cp in first:
            cp.start()
        passed = [copy(4 + j, (*chip, c), sibling) for j, chip in enumerate(chips)]
        for j, chip in enumerate(chips):
            copy(1 + j, (*chip, c), me).wait_recv()
            passed[j].start()
        copy(0, sibling, me).wait_recv()
        for j, chip in enumerate(chips):
            copy(4 + j, (*chip, 1 - c), me).wait_recv()
        for cp in first + passed:
            cp.wait_send()
        mine.wait()

    return pl.pallas_call(
        body, name=name, out_shape=_sds((ND,) + xb.shape, xb.dtype),
        in_specs=[pl.BlockSpec(memory_space=space)], out_specs=pl.BlockSpec(memory_space=space),
        scratch_shapes=[pltpu.SemaphoreType.DMA((7,)), pltpu.SemaphoreType.DMA((7,)), pltpu.SemaphoreType.DMA(())])(xb)


def _ag_job(xb):
    def parts(ins, outs, sems, starting=False):
        x_ref, out_ref = ins[0], outs[0]
        send_sems, recv_sems, local_sem = sems
        x, y, c = _mesh_pos()
        me, sibling = (x, y, c), (x, y, 1 - c)
        chips = [(1 - x, y), (x, 1 - y), (1 - x, 1 - y)]

        def slot(px, py, pc):
            return out_ref.at[4 * px + 2 * py + pc]

        def copy(k, block, to, src=None):
            return pltpu.make_async_remote_copy(
                src_ref=slot(*block) if src is None else src, dst_ref=slot(*block),
                send_sem=send_sems.at[k], recv_sem=recv_sems.at[k], device_id=to, device_id_type=MESH)

        mine = pltpu.make_async_copy(x_ref, slot(*me), local_sem)
        first = [copy(0, me, sibling, src=x_ref)] + [copy(1 + j, me, (*chip, c), src=x_ref) for j, chip in enumerate(chips)]
        if starting:
            return mine, first
        passed = [copy(4 + j, (*chip, c), sibling) for j, chip in enumerate(chips)]
        landed = [copy(1 + j, (*chip, c), me) for j, chip in enumerate(chips)]
        from_sibling = [copy(0, sibling, me)] + [copy(4 + j, (*chip, 1 - c), me) for j, chip in enumerate(chips)]
        return mine, first, passed, landed, from_sibling

    def start(ins, outs, sems):
        mine, first = parts(ins, outs, sems, starting=True)
        mine.start()
        for cp in first:
            cp.start()

    def finish(ins, outs, sems):
        mine, first, passed, landed, from_sibling = parts(ins, outs, sems)
        for j in range(3):
            landed[j].wait_recv()
            passed[j].start()
        for cp in from_sibling:
            cp.wait_recv()
        for cp in first + passed:
            cp.wait_send()
        mine.wait()

    return _Job([xb], [_sds((ND,) + xb.shape, xb.dtype)],
                [pltpu.SemaphoreType.DMA((7,)), pltpu.SemaphoreType.DMA((7,)), pltpu.SemaphoreType.DMA(())], start, finish)


def _rs_sibling_job(Gp):
    def copies(ins, outs, sems):
        x, y, c = _mesh_pos()
        return [pltpu.make_async_remote_copy(
            src_ref=ins[0].at[2 * k + (1 - c)], dst_ref=outs[0].at[k], send_sem=sems[0].at[k], recv_sem=sems[1].at[k],
            device_id=(x, y, 1 - c), device_id_type=MESH) for k in range(4)]

    def start(ins, outs, sems):
        for cp in copies(ins, outs, sems):
            cp.start()

    def finish(ins, outs, sems):
        cps = copies(ins, outs, sems)
        for cp in cps:
            cp.wait_recv()
        for cp in cps:
            cp.wait_send()

    return _Job([Gp], [_sds((4,) + Gp.shape[1:], Gp.dtype)],
                [pltpu.SemaphoreType.DMA((4,)), pltpu.SemaphoreType.DMA((4,))], start, finish)


def _rs_chips_job(T, dests=(0, 1, 2)):
    def copies(ins, outs, sems):
        x, y, c = _mesh_pos()
        chips = [(1 - x, y), (x, 1 - y), (1 - x, 1 - y)]
        cps = []
        for i, j in enumerate(dests):
            px, py = chips[j]
            cps.append(pltpu.make_async_remote_copy(
                src_ref=ins[0].at[2 * px + py], dst_ref=outs[0].at[i], send_sem=sems[0].at[i], recv_sem=sems[1].at[i],
                device_id=(px, py, c), device_id_type=MESH))
        return cps

    def start(ins, outs, sems):
        for cp in copies(ins, outs, sems):
            cp.start()

    def finish(ins, outs, sems):
        cps = copies(ins, outs, sems)
        for cp in cps:
            cp.wait_recv()
        for cp in cps:
            cp.wait_send()

    n = len(dests)
    return _Job([T], [_sds((n,) + T.shape[1:], T.dtype)],
                [pltpu.SemaphoreType.DMA((n,)), pltpu.SemaphoreType.DMA((n,))], start, finish)


def _tile_rows(rows, cols):
    best = None
    for t in range(16, rows + 1, 16):
        if rows % t == 0 and t * cols * 4 <= (1 << 20):
            best = t
    return best if best is not None else rows


def _prefetch_call(body, *, name, grid, in_specs, out_specs, out_shape):
    return pl.pallas_call(
        body, name=name, out_shape=out_shape,
        grid_spec=pltpu.PrefetchScalarGridSpec(num_scalar_prefetch=1, grid=grid, in_specs=in_specs, out_specs=out_specs),
        compiler_params=pltpu.CompilerParams(dimension_semantics=("arbitrary",) * len(grid), vmem_limit_bytes=VMEM_LIMIT))


def _rs_add(Gp, bufA, idx, name):
    rows, cols = Gp.shape[1:]
    tr = _tile_rows(rows, cols)

    def body(i_ref, g_ref, a_ref, t_ref, own_ref):
        t = g_ref[...] + a_ref[...]
        t_ref[...] = t.astype(t_ref.dtype)

        @pl.when(pl.program_id(1) == i_ref[1])
        def _():
            own_ref[...] = t

    return _prefetch_call(
        body, name=name, grid=(rows // tr, 4),
        in_specs=[pl.BlockSpec((None, tr, cols), lambda r, k, i_ref: (2 * k + i_ref[0], r, 0)),
                  pl.BlockSpec((None, tr, cols), lambda r, k, i_ref: (k, r, 0))],
        out_specs=[pl.BlockSpec((None, tr, cols), lambda r, k, i_ref: (k, r, 0)),
                   pl.BlockSpec((tr, cols), lambda r, k, i_ref: (r, 0))],
        out_shape=[_sds((4, rows, cols), jnp.bfloat16), _sds((rows, cols))])(idx, Gp, bufA)


def _adam(w, g, m, v):
    m = ADAM_B1 * m + (1.0 - ADAM_B1) * g
    v = ADAM_B2 * v + (1.0 - ADAM_B2) * (g * g)
    m_hat = m / (1.0 - ADAM_B1 ** ADAM_STEP)
    v_hat = v / (1.0 - ADAM_B2 ** ADAM_STEP)
    return -ADAM_LR * (m_hat / (jnp.sqrt(v_hat) + ADAM_EPS) + ADAM_WD * w), m, v


def _rs_finish(Town, bufB, name, wmv=None, slab=0, n_slabs=1, prev=None):
    rows, cols = Town.shape
    tr = _tile_rows(rows, cols)
    nr = rows // tr
    n_in = 4 + (3 if wmv is not None else 0)
    n_out = 4 if wmv is not None else 1

    def body(*refs):
        ins, outs = refs[:n_in], refs[len(refs) - n_out:]
        g = ((ins[0][...] + ins[1][...].astype(F32)) + ins[2][...].astype(F32)) + ins[3][...].astype(F32)
        outs[0][...] = g
        if wmv is not None:
            d, m, v = _adam(ins[4][...], g, ins[5][...], ins[6][...])
            outs[1][...] = d
            outs[2][...] = m
            outs[3][...] = v

    plain = pl.BlockSpec((tr, cols), lambda r: (r, 0))
    slabbed = pl.BlockSpec((tr, cols), lambda r: (slab * nr + r, 0))
    pairs = bufB if isinstance(bufB, list) else [(bufB, j) for j in range(3)]
    in_specs = [plain] + [pl.BlockSpec((None, tr, cols), (lambda j: lambda r: (j, r, 0))(j)) for _, j in pairs]
    args = [Town] + [a for a, _ in pairs]
    if wmv is not None:
        in_specs += [slabbed] * 3
        args += list(wmv)
    aliases = None
    if prev is not None:
        in_specs += [pl.BlockSpec(memory_space=pl.ANY)] * n_out
        aliases = {len(args) + i: i for i in range(n_out)}
        args += list(prev)
    return _call(body, name=name, grid=(nr,), in_specs=in_specs, out_specs=[slabbed] * n_out,
                 out_shape=[_sds((n_slabs * rows, cols))] * n_out, aliases=aliases)(*args)


def _adamw_plain(w, g, m, v, name):
    rows, cols = w.shape
    tr = _tile_rows(rows, cols)

    def body(w_ref, g_ref, m_ref, v_ref, d_ref, mo_ref, vo_ref):
        d, m_, v_ = _adam(w_ref[...], g_ref[...], m_ref[...], v_ref[...])
        d_ref[...] = d
        mo_ref[...] = m_
        vo_ref[...] = v_

    spec = pl.BlockSpec((tr, cols), lambda r: (r, 0))
    return _call(body, name=name, grid=(rows // tr,), in_specs=[spec] * 4, out_specs=[spec] * 3,
                 out_shape=[_sds((rows, cols))] * 3)(w, g, m, v)


_MOD_TK = 512


def _mod_fwd(cc16, w_loc, b_loc, name):
    D, cols = w_loc.shape
    tk = min(_MOD_TK, D)
    nk = D // tk

    def body(c_ref, w_ref, b_ref, o_ref):
        kk = pl.program_id(0)

        @pl.when(kk == 0)
        def _():
            o_ref[...] = jnp.zeros_like(o_ref)

        cc = c_ref[...]
        o_ref[...] += _dot(cc * _sigmoid(cc), w_ref[...])

        @pl.when(kk == nk - 1)
        def _():
            o_ref[...] += b_ref[...]

    return _call(body, name=name, grid=(nk,),
                 in_specs=[pl.BlockSpec((16, tk), lambda kk: (0, kk)), pl.BlockSpec((tk, cols), lambda kk: (kk, 0)),
                           _full_spec((1, cols))],
                 out_specs=_full_spec((16, cols)), out_shape=_sds((16, cols)))(cc16, w_loc, b_loc)


def _mod_bwd(dm_loc, cc16, w_loc, name):
    D, cols = w_loc.shape

    def body(dm_ref, c_ref, w_ref, gw_ref, ds_ref):
        rows = [dm_ref[b, 0:1, :] for b in range(ND)]
        ctx = dm_ref[0, 1:2, :]
        for b in range(1, ND):
            ctx = ctx + dm_ref[b, 1:2, :]
        dm16 = jnp.concatenate(rows + [ctx, jnp.zeros((7, cols), F32)], axis=0)
        cc = c_ref[...]
        gw_ref[...] = _dot_tn(cc * _sigmoid(cc), dm16)
        ds_ref[...] = _dot_nt(dm16, w_ref[...])

    tk = min(_MOD_TK, D)
    return _call(body, name=name, grid=(D // tk,),
                 in_specs=[_full_spec((ND, 8, cols)), pl.BlockSpec((16, tk), lambda kk: (0, kk)),
                           pl.BlockSpec((tk, cols), lambda kk: (kk, 0))],
                 out_specs=[pl.BlockSpec((tk, cols), lambda kk: (kk, 0)), pl.BlockSpec((16, tk), lambda kk: (0, kk))],
                 out_shape=[_sds((D, cols)), _sds((16, D))])(dm_loc, cc16, w_loc)


def _bmod_grad(dm_all, name):
    n = dm_all.shape[-1]

    def body(dm_ref, o_ref):
        acc = dm_ref[0, 0:1, :] + dm_ref[0, 1:2, :]
        for b in range(1, ND):
            acc = (acc + dm_ref[b, 0:1, :]) + dm_ref[b, 1:2, :]
        o_ref[...] = acc

    return _call(body, name=name, grid=(1,), in_specs=[_full_spec((ND, 8, n))], out_specs=_full_spec((1, n)),
                 out_shape=_sds((1, n)))(dm_all)


_SMALL_ROWS = 24
_ROW_CCTX = 15


def _small_finish(parts, c_ctx, name):
    D = parts.shape[-1]

    def body(p_ref, c_ref, o_ref):
        acc = p_ref[0]
        for b in range(1, ND):
            acc = acc + p_ref[b]
        cc = c_ref[...]
        sg = _sigmoid(cc)
        dsilu = sg * (1.0 + cc * (1.0 - sg))
        row = lax.broadcasted_iota(jnp.int32, acc.shape, 0)
        o_ref[...] = jnp.where(row == _ROW_CCTX, acc * dsilu, acc)

    return _call(body, name=name, grid=(1,), in_specs=[_full_spec(parts.shape), _full_spec((1, D))],
                 out_specs=_full_spec((_SMALL_ROWS, D)), out_shape=_sds((_SMALL_ROWS, D)))(parts, c_ctx)


_WEIGHTS = ['c_ctx', 'w_mod', 'b_mod', 'norm_g', 'ffn_wg', 'ffn_wu', 'ffn_wd', 'w_in', 'w_out', 'q_norm_g', 'k_norm_g',
            'conv_w', 'conv_b', 'lru_wa', 'lru_ba', 'lru_wx', 'lru_bx', 'lru_lambda', 'final_norm_g']
_SMALL = ['c_ctx', 'b_mod', 'norm_g', 'q_norm_g', 'k_norm_g', 'conv_w', 'conv_b', 'lru_ba', 'lru_bx', 'lru_lambda',
          'final_norm_g']


def _pad_rows(a, rows):
    return jnp.pad(a, ((0, rows - a.shape[0]),) + ((0, 0),) * (a.ndim - 1))


def _step(w, m, v, x, c, ctx, loss_target):
    xi, yi, ci = _mesh_pos()
    me = 4 * xi + 2 * yi + ci
    idx = jnp.stack([ci, 2 * xi + yi]).astype(jnp.int32)
    S, D = x.shape[1:]
    Ds = D // ND
    cols = w['w_mod'].shape[-1]

    sp = jnp.concatenate([w['norm_g'][0], w['conv_w'][0], w['lru_ba'][0], w['lru_bx'][0], w['lru_lambda'][0]], axis=0)
    spg = _all_gather(_pad_rows(sp, 16), "ag_small_params", in_vmem=True)
    spf = jnp.transpose(spg, (1, 0, 2)).reshape(16, D)
    ng, conv_w, ba, bx, lam = spf[0:3], spf[3:7], spf[7:9], spf[9:11], spf[11:13]

    cg = _all_gather(_pad_rows(c, 8), "ag_cond", in_vmem=True)
    cc16 = _pad_rows(jnp.concatenate([cg[:, 0, :], w['c_ctx'][None, :]], axis=0), 16)
    b_loc = lax.dynamic_slice_in_dim(w['b_mod'], me * cols, cols, axis=1)
    mod_loc = _mod_fwd(cc16, w['w_mod'][0], b_loc, "mod_fwd")
    modg = _all_gather(mod_loc, "ag_mod", in_vmem=True)
    mod16 = jnp.transpose(modg, (1, 0, 2)).reshape(16, ND * cols)
    modx = lax.dynamic_index_in_dim(mod16, me, axis=0, keepdims=False).reshape(9, D)
    modc = mod16[8].reshape(9, D)

    shards = {'w_in': w['w_in'][0].astype(_MXU), 'w_out': w['w_out'][0].astype(_MXU)}
    for layer in range(2):
        shards[f'wg{layer}'] = w['ffn_wg'][0, layer].astype(_MXU)[None]
        shards[f'wu{layer}'] = w['ffn_wu'][0, layer].astype(_MXU)[None]
        shards[f'wd{layer}'] = w['ffn_wd'][0, layer].astype(_MXU)[None]
    WA, WX = w['lru_wa'][0].astype(_MXU), w['lru_wx'][0].astype(_MXU)
    big = ('ffn_wg', 'ffn_wu', 'ffn_wd', 'w_in', 'w_out')
    opt = {n: tuple(a[n].reshape(-1, a[n].shape[-1]) for a in (w, m, v)) for n in big}

    g = _local_step(x[0], ctx[0], loss_target[0], modx, modc, ng, shards, w['q_norm_g'], w['k_norm_g'],
                    conv_w, w['conv_b'], WA, WX, ba, bx, lam, w['final_norm_g'][None, :], idx, opt)

    grad, delta, new_m, new_v = {}, {}, {}, {}
    for n in big:
        grad[n], delta[n], new_m[n], new_v[n] = [o.reshape(w[n].shape) for o in g[n]]

    lfull = g['lru_w'].reshape((2,) + w['lru_wa'].shape[1:])
    for i, n in enumerate(('lru_wa', 'lru_wx')):
        shard = w[n].shape
        view = lambda a: a.reshape(-1, HD)
        grad[n] = lfull[i].reshape(shard)
        outs = _adamw_plain(view(w[n]), view(lfull[i]), view(m[n]), view(v[n]), f"adamw_{n}")
        delta[n], new_m[n], new_v[n] = [o.reshape(shard) for o in outs]

    dm = _pad_rows(jnp.stack([g['dmodx'].reshape(-1), g['dmodc'].reshape(-1)]), 8)
    dm_all = _all_gather(dm, "ag_dmod", in_vmem=True)
    dm_loc = lax.dynamic_slice_in_dim(dm_all, me * cols, cols, axis=2)
    gw_mod, dsil = _mod_bwd(dm_loc, cc16, w['w_mod'][0], "mod_bwd")
    grad['w_mod'] = gw_mod[None]
    outs = _adamw_plain(w['w_mod'][0], gw_mod, m['w_mod'][0], v['w_mod'][0], "adamw_w_mod")
    delta['w_mod'], new_m['w_mod'], new_v['w_mod'] = [o[None] for o in outs]
    grad['b_mod'] = _bmod_grad(dm_all, "bmod_grad")

    pad_d = lambda a: jnp.concatenate([a, jnp.zeros((1, D - a.shape[1]), F32)], axis=1)
    small = jnp.concatenate([g['norm_g'], g['conv_w'], g['conv_b'], g['lru_ba'], g['lru_bx'], g['lru_lambda'],
                             g['final_norm_g'], dsil[8:9], pad_d(g['q_norm_g']), pad_d(g['k_norm_g'])], axis=0)
    parts = _all_gather(_pad_rows(small, _SMALL_ROWS), "ag_small_grads", in_vmem=True)
    ssum = _small_finish(parts, w['c_ctx'][None, :], "small_finish")
    mine = lambda rows: lax.dynamic_slice_in_dim(rows, me * Ds, Ds, axis=1)
    grad['norm_g'] = mine(ssum[0:3])[None]
    grad['conv_w'] = mine(ssum[3:7])[None]
    grad['conv_b'] = ssum[7:8]
    grad['lru_ba'] = mine(ssum[8:10])[None]
    grad['lru_bx'] = mine(ssum[10:12])[None]
    grad['lru_lambda'] = mine(ssum[12:14])[None]
    grad['final_norm_g'] = ssum[14]
    grad['c_ctx'] = ssum[_ROW_CCTX]
    grad['q_norm_g'] = ssum[16:17, :HD]
    grad['k_norm_g'] = ssum[17:18, :HD]

    def pack(d):
        flat = jnp.concatenate([d[n].reshape(-1) for n in _SMALL])
        padded = -(-flat.shape[0] // 1024) * 1024
        return jnp.concatenate([flat, jnp.zeros((padded - flat.shape[0],), F32)]).reshape(-1, HD)

    outs = _adamw_plain(pack(w), pack(grad), pack(m), pack(v), "adamw_small")
    off = 0
    for n in _SMALL:
        size = math.prod(w[n].shape)
        for dst, o in zip((delta, new_m, new_v), outs):
            dst[n] = o.reshape(-1)[off:off + size].reshape(w[n].shape)
        off += size

    loss = lax.psum(g['loss'][0, 0], ("x", "y", "c"))
    return (loss, g['grad_x'][None], *[grad[n] for n in _WEIGHTS], *[delta[n] for n in _WEIGHTS],
            *[new_m[n] for n in _WEIGHTS], *[new_v[n] for n in _WEIGHTS])


def kernel(x, c, ctx, c_ctx, w_mod, b_mod, norm_g, ffn_wg, ffn_wu, ffn_wd, w_in, w_out, q_norm_g, k_norm_g, conv_w, conv_b, lru_wa, lru_ba, lru_wx, lru_bx, lru_lambda, final_norm_g, loss_target, m_c_ctx, m_w_mod, m_b_mod, m_norm_g, m_ffn_wg, m_ffn_wu, m_ffn_wd, m_w_in, m_w_out, m_q_norm_g, m_k_norm_g, m_conv_w, m_conv_b, m_lru_wa, m_lru_ba, m_lru_wx, m_lru_bx, m_lru_lambda, m_final_norm_g, v_c_ctx, v_w_mod, v_b_mod, v_norm_g, v_ffn_wg, v_ffn_wu, v_ffn_wd, v_w_in, v_w_out, v_q_norm_g, v_k_norm_g, v_conv_w, v_conv_b, v_lru_wa, v_lru_ba, v_lru_wx, v_lru_bx, v_lru_lambda, v_final_norm_g):
    given = dict(locals())
    w = {n: given[n] for n in _WEIGHTS}
    m = {n: given["m_" + n] for n in _WEIGHTS}
    v = {n: given["v_" + n] for n in _WEIGHTS}
    return _step(w, m, v, x, c, ctx, loss_target)
```

```python
import functools
import math

import jax
import jax.numpy as jnp
from jax import lax
from jax.experimental import pallas as pl
from jax.experimental.pallas import tpu as pltpu

F32 = jnp.float32
_MXU = jnp.bfloat16
ND = 8
TR = 256
HD = 128
EPS = 1e-6
GRID_W = 64
ROPE_THETA = 10000.0
LRU_C = 8.0
VMEM_LIMIT = 56 * 1024 * 1024
SCAN_W = 512
ADAM_LR, ADAM_B1, ADAM_B2, ADAM_EPS, ADAM_WD, ADAM_STEP = 0.001, 0.9, 0.999, 1e-08, 0.01, 10
MESH = pl.DeviceIdType.MESH


class _Job:
    def __init__(self, inputs, out_shapes, sems, start, finish):
        self.inputs, self.out_shapes, self.sems, self.start, self.finish = inputs, out_shapes, sems, start, finish


def _call(body, *, name, grid, in_specs, out_specs, out_shape, scratch=(), aliases=None, jobs=()):
    params = pltpu.CompilerParams(dimension_semantics=("arbitrary",) * len(grid), vmem_limit_bytes=VMEM_LIMIT)
    if not jobs:
        return pl.pallas_call(body, name=name, grid=grid, in_specs=in_specs, out_specs=out_specs, out_shape=out_shape,
                              scratch_shapes=scratch, input_output_aliases=aliases or {}, compiler_params=params)
    single = not isinstance(out_specs, (list, tuple))
    o_specs = [out_specs] if single else list(out_specs)
    o_shape = [out_shape] if single else list(out_shape)
    n_in, n_out, n_scr = len(in_specs), len(o_specs), len(scratch)
    j_in = [a for j in jobs for a in j.inputs]
    j_out = [s for j in jobs for s in j.out_shapes]
    j_sem = [s for j in jobs for s in j.sems]
    hbm = pl.BlockSpec(memory_space=pl.ANY)

    def wrapped(*refs):
        ins, rest = refs[:n_in], refs[n_in:]
        jin, rest = rest[:len(j_in)], rest[len(j_in):]
        outs, rest = rest[:n_out], rest[n_out:]
        jout, rest = rest[:len(j_out)], rest[len(j_out):]
        scr, jsem = rest[:n_scr], rest[n_scr:]
        first = functools.reduce(jnp.logical_and, [pl.program_id(a) == 0 for a in range(len(grid))])
        last = functools.reduce(jnp.logical_and, [pl.program_id(a) == grid[a] - 1 for a in range(len(grid))])

        def each(which):
            i = o = s = 0
            for j in jobs:
                ni, no, ns = len(j.inputs), len(j.out_shapes), len(j.sems)
                getattr(j, which)(jin[i:i + ni], jout[o:o + no], jsem[s:s + ns])
                i, o, s = i + ni, o + no, s + ns

        @pl.when(first)
        def _():
            each("start")

        body(*ins, *outs, *scr)

        @pl.when(last)
        def _():
            each("finish")

    call = pl.pallas_call(wrapped, name=name, grid=grid, in_specs=list(in_specs) + [hbm] * len(j_in),
                          out_specs=o_specs + [hbm] * len(j_out), out_shape=o_shape + j_out,
                          scratch_shapes=list(scratch) + j_sem, input_output_aliases=aliases or {}, compiler_params=params)

    def run(*args):
        res = call(*args, *j_in)
        comp = res[0] if single else list(res[:n_out])
        jres, o = [], n_out
        for j in jobs:
            jres.append(list(res[o:o + len(j.out_shapes)]))
            o += len(j.out_shapes)
        return comp, jres

    return run


def _comm_call(jobs, name):
    j_in = [a for j in jobs for a in j.inputs]
    j_out = [s for j in jobs for s in j.out_shapes]
    j_sem = [s for j in jobs for s in j.sems]
    hbm = pl.BlockSpec(memory_space=pl.ANY)

    def body(*refs):
        jin, jout, jsem = refs[:len(j_in)], refs[len(j_in):len(j_in) + len(j_out)], refs[len(j_in) + len(j_out):]
        for which in ("start", "finish"):
            i = o = s = 0
            for j in jobs:
                ni, no, ns = len(j.inputs), len(j.out_shapes), len(j.sems)
                getattr(j, which)(jin[i:i + ni], jout[o:o + no], jsem[s:s + ns])
                i, o, s = i + ni, o + no, s + ns

    res = pl.pallas_call(body, name=name, out_shape=j_out, in_specs=[hbm] * len(j_in), out_specs=[hbm] * len(j_out),
                         scratch_shapes=j_sem)(*j_in)
    jres, o = [], 0
    for j in jobs:
        jres.append(list(res[o:o + len(j.out_shapes)]))
        o += len(j.out_shapes)
    return jres


def _sds(shape, dtype=F32):
    return jax.ShapeDtypeStruct(tuple(shape), dtype)


def _dot(a, b):
    return jnp.dot(a.astype(_MXU), b.astype(_MXU), preferred_element_type=F32)


def _dot_nt(a, b):
    return lax.dot_general(a.astype(_MXU), b.astype(_MXU), (((1,), (1,)), ((), ())), preferred_element_type=F32)


def _dot_tn(a, b):
    return lax.dot_general(a.astype(_MXU), b.astype(_MXU), (((0,), (0,)), ((), ())), preferred_element_type=F32)


def _sigmoid(x):
    return 0.5 * jnp.tanh(0.5 * x) + 0.5


_GELU_C = math.sqrt(2.0 / math.pi)


def _gelu_and_grad(x):
    x2 = x * x
    t = jnp.tanh(_GELU_C * (x + 0.044715 * x * x2))
    ge = 0.5 * x * (1.0 + t)
    dge = 0.5 * (1.0 + t) + 0.5 * x * (1.0 - t * t) * (_GELU_C * (1.0 + 3.0 * 0.044715 * x2))
    return ge, dge


def _tm(rows):
    assert rows % 4 == 0 and (rows // 4) % 16 == 0
    return rows // 4


def _row_spec(width, nmax=None):
    if nmax is None:
        return pl.BlockSpec((TR, width), lambda i: (i, 0))
    return pl.BlockSpec((TR, width), lambda i: (jnp.minimum(i, nmax), 0))


def _full_spec(shape):
    n = len(shape)
    return pl.BlockSpec(tuple(shape), lambda *_: (0,) * n)


def _mod_spec(D, nx):
    return pl.BlockSpec((None, 9, D), lambda i: (i // nx, 0, 0))


def _norm_mod_fwd(X, ng, mod2, k, rows, nx, name):
    D = X.shape[1]

    def body(x_ref, g_ref, mod_ref, h_ref):
        x = x_ref[...]
        r = lax.rsqrt(jnp.mean(x * x, axis=-1, keepdims=True) + EPS)
        n = (x * r) * g_ref[...]
        h_ref[...] = (n * (1.0 + mod_ref[3 * k + 1:3 * k + 2, :]) + mod_ref[3 * k:3 * k + 1, :]).astype(h_ref.dtype)

    return _call(body, name=name, grid=(rows // TR,),
                 in_specs=[_row_spec(D), _full_spec((1, D)), _mod_spec(D, nx)],
                 out_specs=_row_spec(D), out_shape=_sds((rows, D), _MXU))(X, ng, mod2)


def _norm_mod_bwd(X, dH, dXres, ng, mod2, k, rows, nx, res_tiles, name, jobs=(), branch=None):
    D = X.shape[1]
    ngroups = -(-(rows // TR) // nx)
    n_in = 5 + (1 if branch is not None else 0)

    def body(*refs):
        x_ref, dh_ref, dres_ref, g_ref, mod_ref = refs[:5]
        dx_ref, dsh_ref, dsc_ref, dng_ref = refs[n_in:n_in + 4]
        i = pl.program_id(0)
        x = x_ref[...]
        dh = dh_ref[...]
        g = g_ref[...]
        r = lax.rsqrt(jnp.mean(x * x, axis=-1, keepdims=True) + EPS)
        xh = x * r
        n = xh * g
        dn_mod = dh * (1.0 + mod_ref[3 * k + 1:3 * k + 2, :])

        @pl.when(i % nx == 0)
        def _():
            dsh_ref[...] = jnp.zeros_like(dsh_ref)
            dsc_ref[...] = jnp.zeros_like(dsc_ref)
            if branch is not None:
                refs[n_in + 5][...] = jnp.zeros_like(refs[n_in + 5])

        @pl.when(i == 0)
        def _():
            dng_ref[...] = jnp.zeros_like(dng_ref)

        dsh_ref[...] += jnp.sum(dh, axis=0, keepdims=True)
        dsc_ref[...] += jnp.sum(dh * n, axis=0, keepdims=True)
        dng_ref[...] += jnp.sum(dn_mod * xh, axis=0, keepdims=True)
        dn = dn_mod * g
        dres = jnp.where(i < res_tiles, dres_ref[...], 0.0)
        dx = r * (dn - xh * jnp.mean(dn * xh, axis=-1, keepdims=True)) + dres
        dx_ref[...] = dx
        if branch is not None:
            _, k2, coef = branch
            refs[n_in + 4][...] = ((coef * mod_ref[3 * k2 + 2:3 * k2 + 3, :]) * dx).astype(refs[n_in + 4].dtype)
            refs[n_in + 5][...] += jnp.sum(coef * dx * refs[5][...], axis=0, keepdims=True)

    grp = pl.BlockSpec((None, 1, D), lambda i: (i // nx, 0, 0))
    in_specs = [_row_spec(D), _row_spec(D), _row_spec(D, res_tiles - 1), _full_spec((1, D)), _mod_spec(D, nx)]
    out_specs = [_row_spec(D), grp, grp, _full_spec((1, D))]
    out_shape = [_sds((rows, D)), _sds((ngroups, 1, D)), _sds((ngroups, 1, D)), _sds((1, D))]
    args = [X, dH, dXres, ng, mod2]
    if branch is not None:
        in_specs.append(_row_spec(D))
        args.append(branch[0])
        out_specs += [_row_spec(D), grp]
        out_shape += [_sds((rows, D), _MXU), _sds((ngroups, 1, D))]
    return _call(body, jobs=jobs, name=name, grid=(rows // TR,), in_specs=in_specs, out_specs=out_specs,
                 out_shape=out_shape)(*args)


def _ffn_gate(H, WG, rows, name, jobs=()):
    D = H.shape[1]
    Fb = WG.shape[-2]

    def body(h_ref, w_ref, g_ref):
        g_ref[...] = _dot_nt(h_ref[...], w_ref[...])

    tm = _tm(rows)
    return _call(body, jobs=jobs, name=name, grid=(ND, rows // tm),
                 in_specs=[pl.BlockSpec((tm, D), lambda d, m: (m, 0)), pl.BlockSpec((None, None, Fb, D), lambda d, m: (d, 0, 0, 0))],
                 out_specs=pl.BlockSpec((None, tm, Fb), lambda d, m: (d, m, 0)), out_shape=_sds((ND, rows, Fb)))(H, WG)


def _ffn_up(H, WG, WU, rows, name, jobs=(), G=None):
    D = H.shape[1]
    Fb = WU.shape[-2]

    def body(h_ref, w_ref, x_ref, *outs):
        h = h_ref[...]
        g = x_ref[...] if G is not None else _dot_nt(h, x_ref[...])
        u = _dot_nt(h, w_ref[...])
        if G is None:
            outs[0][...] = g
        outs[-2][...] = u
        outs[-1][...] = ((g * _sigmoid(g)) * u).astype(outs[-1].dtype)

    tm = _tm(rows)
    blk = pl.BlockSpec((None, tm, Fb), lambda d, m: (d, m, 0))
    wspec = pl.BlockSpec((None, None, Fb, D), lambda d, m: (d, 0, 0, 0))
    f32o, bfo = _sds((ND, rows, Fb)), _sds((ND, rows, Fb), _MXU)
    return _call(body, jobs=jobs, name=name, grid=(ND, rows // tm),
                 in_specs=[pl.BlockSpec((tm, D), lambda d, m: (m, 0)), wspec, blk if G is not None else wspec],
                 out_specs=[blk, blk] if G is not None else [blk, blk, blk],
                 out_shape=[f32o, bfo] if G is not None else [f32o, f32o, bfo])(H, WU, G if G is not None else WG)


def _ffn_down(A, WD, layer, X, mod2, k, rows, S, name, jobs=(), next_norm=None):
    Fb, D = WD.shape[-2:]
    tm = _tm(rows) // 2

    def body(a_ref, w_ref, x_ref, mod_ref, *rest):
        if next_norm is not None:
            g_ref, y_ref, xn_ref, h_ref, acc_ref = rest
        else:
            y_ref, xn_ref, acc_ref = rest
        d = pl.program_id(1)

        @pl.when(d == 0)
        def _():
            acc_ref[...] = jnp.zeros_like(acc_ref)

        acc_ref[...] += _dot(a_ref[...], w_ref[...])

        @pl.when(d == ND - 1)
        def _():
            y = acc_ref[...]
            y_ref[...] = y
            is_ctx = (pl.program_id(0) * tm + lax.broadcasted_iota(jnp.int32, (tm, 1), 0)) >= S

            def mod_row(j):
                return jnp.where(is_ctx, mod_ref[1, j:j + 1, :], mod_ref[0, j:j + 1, :])

            xn = x_ref[...] + (0.5 * mod_row(3 * k + 2)) * y
            xn_ref[...] = xn
            if next_norm is not None:
                k2 = next_norm[1]
                n = (xn * lax.rsqrt(jnp.mean(xn * xn, axis=-1, keepdims=True) + EPS)) * g_ref[...]
                h_ref[...] = (n * (1.0 + mod_row(3 * k2 + 1)) + mod_row(3 * k2)).astype(h_ref.dtype)

    row = pl.BlockSpec((tm, D), lambda m, d: (m, 0))
    in_specs = [pl.BlockSpec((None, tm, Fb), lambda m, d: (d, m, 0)),
                pl.BlockSpec((None, None, Fb, D), lambda m, d: (d, layer, 0, 0)),
                row, pl.BlockSpec((2, 9, D), lambda m, d: (0, 0, 0))]
    args = [A, WD, X, mod2]
    out_specs, out_shape = [row, row], [_sds((rows, D)), _sds((rows, D))]
    if next_norm is not None:
        in_specs.append(pl.BlockSpec((1, D), lambda m, d: (0, 0)))
        args.append(next_norm[0])
        out_specs.append(row)
        out_shape.append(_sds((rows, D), _MXU))
    return _call(body, jobs=jobs, name=name, grid=(rows // tm, ND), in_specs=in_specs, out_specs=out_specs,
                 out_shape=out_shape, scratch=[pltpu.VMEM((tm, D), F32)])(*args)


def _ffn_down_loss(A, WD, X, mod2, k, fg, target, name):
    Fb, D = WD.shape[-2:]
    S = X.shape[0]
    tm = _tm(S) // 2

    def body(a_ref, w_ref, x_ref, mod_ref, g_ref, t_ref, loss_ref, dx_ref, dy_ref, dgate_ref, dg_ref, acc_ref):
        m, d = pl.program_id(0), pl.program_id(1)

        @pl.when(d == 0)
        def _():
            acc_ref[...] = jnp.zeros_like(acc_ref)

        @pl.when(jnp.logical_and(m == 0, d == 0))
        def _():
            loss_ref[...] = jnp.zeros_like(loss_ref)
            dgate_ref[...] = jnp.zeros_like(dgate_ref)
            dg_ref[...] = jnp.zeros_like(dg_ref)

        acc_ref[...] += _dot(a_ref[...], w_ref[...])

        @pl.when(d == ND - 1)
        def _():
            y = acc_ref[...]
            gate = 0.5 * mod_ref[3 * k + 2:3 * k + 3, :]
            x = x_ref[...] + gate * y
            g = g_ref[...]
            r = lax.rsqrt(jnp.mean(x * x, axis=-1, keepdims=True) + EPS)
            n = x * r
            err = n * g - t_ref[...]
            loss_ref[...] += 0.5 * jnp.sum(jnp.mean(err * err, axis=-1, keepdims=True), axis=0, keepdims=True)
            dy = err * (1.0 / D)
            dg_ref[...] += jnp.sum(dy * n, axis=0, keepdims=True)
            dn = dy * g
            dx = r * (dn - n * jnp.mean(dn * n, axis=-1, keepdims=True))
            dx_ref[...] = dx
            dy_ref[...] = (gate * dx).astype(dy_ref.dtype)
            dgate_ref[...] += jnp.sum(0.5 * dx * y, axis=0, keepdims=True)

    row = pl.BlockSpec((tm, D), lambda m, d: (m, 0))
    vec = pl.BlockSpec((1, D), lambda m, d: (0, 0))
    return _call(body, name=name, grid=(S // tm, ND),
                 in_specs=[pl.BlockSpec((None, tm, Fb), lambda m, d: (d, m, 0)),
                           pl.BlockSpec((None, None, Fb, D), lambda m, d: (d, 0, 0, 0)),
                           row, pl.BlockSpec((None, 9, D), lambda m, d: (0, 0, 0)), vec, row],
                 out_specs=[pl.BlockSpec((1, 1), lambda m, d: (0, 0)), row, row, vec, vec],
                 out_shape=[_sds((1, 1)), _sds((S, D)), _sds((S, D), _MXU), _sds((1, D)), _sds((1, D))],
                 scratch=[pltpu.VMEM((tm, D), F32)])(A, WD, X, mod2, fg, target)


def _ffn_dact(dYb, WD, layer, G, U, rows, name, jobs=()):
    Fb, D = WD.shape[-2:]

    def body(dy_ref, w_ref, g_ref, u_ref, dg_ref, du_ref):
        da = _dot_nt(dy_ref[...], w_ref[...])
        g = g_ref[...]
        sg = _sigmoid(g)
        dg_ref[...] = (da * u_ref[...] * (sg * (1.0 + g * (1.0 - sg)))).astype(dg_ref.dtype)
        du_ref[...] = (da * (g * sg)).astype(du_ref.dtype)

    tm = _tm(rows)
    blk = pl.BlockSpec((None, tm, Fb), lambda m, d: (d, m, 0))
    return _call(body, jobs=jobs, name=name, grid=(rows // tm, ND),
                 in_specs=[pl.BlockSpec((tm, D), lambda m, d: (m, 0)),
                           pl.BlockSpec((None, None, Fb, D), lambda m, d: (d, layer, 0, 0)), blk, blk],
                 out_specs=[blk, blk],
                 out_shape=[_sds((ND, rows, Fb), _MXU), _sds((ND, rows, Fb), _MXU)])(dYb, WD, G, U)


def _ffn_dh(dG, dU, WG, WU, rows, name, jobs=()):
    Fb, D = WG.shape[-2:]

    def body(dg_ref, du_ref, wg_ref, wu_ref, dh_ref, acc_ref):
        d = pl.program_id(1)

        @pl.when(d == 0)
        def _():
            acc_ref[...] = jnp.zeros_like(acc_ref)

        acc_ref[...] += _dot(dg_ref[...], wg_ref[...]) + _dot(du_ref[...], wu_ref[...])

        @pl.when(d == ND - 1)
        def _():
            dh_ref[...] = acc_ref[...]

    tm = _tm(rows)
    blk = pl.BlockSpec((None, tm, Fb), lambda m, d: (d, m, 0))
    wspec = pl.BlockSpec((None, None, Fb, D), lambda m, d: (d, 0, 0, 0))
    return _call(body, jobs=jobs, name=name, grid=(rows // tm, ND), in_specs=[blk, blk, wspec, wspec],
                 out_specs=pl.BlockSpec((tm, D), lambda m, d: (m, 0)), out_shape=_sds((rows, D)),
                 scratch=[pltpu.VMEM((tm, D), F32)])(dG, dU, WG, WU)


def _mm_tn(A, a_spec, B, b_spec, out_shape, out_spec, rows, name, prev=None, jobs=()):
    def body(*refs):
        a_ref, b_ref, o_ref = refs[0], refs[1], refs[-1]

        @pl.when(pl.program_id(1) == 0)
        def _():
            o_ref[...] = jnp.zeros_like(o_ref)

        o_ref[...] += _dot_tn(a_ref[...], b_ref[...])

    in_specs = [a_spec, b_spec]
    args = [A, B]
    aliases = None
    if prev is not None:
        in_specs.append(pl.BlockSpec(memory_space=pl.ANY))
        args.append(prev)
        aliases = {2: 0}
    return _call(body, jobs=jobs, name=name, grid=(ND, rows // _tm(rows)), in_specs=in_specs, out_specs=out_spec,
                 out_shape=_sds(out_shape), aliases=aliases)(*args)


def _proj_in(H2, WIN, name, jobs=()):
    R, D = H2.shape
    Nb = WIN.shape[-1]

    def body(h_ref, w_ref, p_ref):
        p_ref[...] = _dot(h_ref[...], w_ref[...])

    tm = _tm(R)
    return _call(body, jobs=jobs, name=name, grid=(ND, R // tm),
                 in_specs=[pl.BlockSpec((tm, D), lambda d, m: (m, 0)), pl.BlockSpec((None, D, Nb), lambda d, m: (d, 0, 0))],
                 out_specs=pl.BlockSpec((tm, Nb), lambda d, m: (m, d)), out_shape=_sds((R, ND * Nb)))(H2, WIN)


def _dproj_in(dP, WIN, name, jobs=()):
    R = dP.shape[0]
    D, Nb = WIN.shape[-2:]

    def body(dp_ref, w_ref, dh_ref, acc_ref):
        d = pl.program_id(1)

        @pl.when(d == 0)
        def _():
            acc_ref[...] = jnp.zeros_like(acc_ref)

        acc_ref[...] += _dot_nt(dp_ref[...], w_ref[...])

        @pl.when(d == ND - 1)
        def _():
            dh_ref[...] = acc_ref[...]

    tm = _tm(R)
    return _call(body, jobs=jobs, name=name, grid=(R // tm, ND),
                 in_specs=[pl.BlockSpec((tm, Nb), lambda m, d: (m, d)), pl.BlockSpec((None, D, Nb), lambda m, d: (d, 0, 0))],
                 out_specs=pl.BlockSpec((tm, D), lambda m, d: (m, 0)), out_shape=_sds((R, D)),
                 scratch=[pltpu.VMEM((tm, D), F32)])(dP, WIN)


def _proj_out(mixb, WOUT, X1, mod2, ng, S, name):
    D = WOUT.shape[0]

    def body(m_ref, w_ref, x_ref, mod_ref, g_ref, z_ref, xn_ref, h_ref):
        z = _dot(m_ref[...], w_ref[...])
        z_ref[...] = z
        xn = x_ref[...] + mod_ref[5:6, :] * z
        xn_ref[...] = xn
        n = (xn * lax.rsqrt(jnp.mean(xn * xn, axis=-1, keepdims=True) + EPS)) * g_ref[...]
        h_ref[...] = (n * (1.0 + mod_ref[7:8, :]) + mod_ref[6:7, :]).astype(h_ref.dtype)

    return _call(body, name=name, grid=(S // TR,),
                 in_specs=[_row_spec(D), _full_spec((D, D)), _row_spec(D), pl.BlockSpec((None, 9, D), lambda i: (0, 0, 0)),
                           _full_spec((1, D))],
                 out_specs=[_row_spec(D), _row_spec(D), _row_spec(D)],
                 out_shape=[_sds((S, D)), _sds((S, D)), _sds((S, D), _MXU)])(mixb, WOUT, X1, mod2, ng)


def _dproj_out(dZb, WOUT, name):
    S, D = dZb.shape

    def body(dz_ref, w_ref, dm_ref):
        dm_ref[...] = _dot_nt(dz_ref[...], w_ref[...])

    return _call(body, name=name, grid=(S // TR,), in_specs=[_row_spec(D), _full_spec((D, D))],
                 out_specs=_row_spec(D), out_shape=_sds((S, D)))(dZb, WOUT)


def _pair_swap(t):
    lane = lax.broadcasted_iota(jnp.int32, t.shape, 1)
    return jnp.where(lane % 2 == 0, pltpu.roll(t, HD - 1, 1), pltpu.roll(t, 1, 1))


SM_SCALE = HD ** -0.5


def _qkv_prep(P, qg, kg, COS, SIN, D, KVW, name):
    R = P.shape[0]
    W = D + 2 * KVW
    nq, nk = D // HD, KVW // HD

    def body(p_ref, qg_ref, kg_ref, cos_ref, sin_ref, q_ref, k_ref, v_ref):
        cos, sin = cos_ref[...], sin_ref[...]

        def head(t, g):
            y = (t * lax.rsqrt(jnp.mean(t * t, axis=-1, keepdims=True) + EPS)) * g
            return y * cos + _pair_swap(y) * sin

        for h in range(nq):
            q_ref[:, h * HD:(h + 1) * HD] = (head(p_ref[:, h * HD:(h + 1) * HD], qg_ref[...]) * SM_SCALE).astype(q_ref.dtype)
        for h in range(nk):
            k_ref[:, h * HD:(h + 1) * HD] = head(p_ref[:, D + h * HD:D + (h + 1) * HD], kg_ref[...]).astype(k_ref.dtype)
        v_ref[...] = p_ref[:, D + KVW:W].astype(v_ref.dtype)

    return _call(body, name=name, grid=(R // TR,),
                 in_specs=[_row_spec(W), _full_spec((1, HD)), _full_spec((1, HD)), _row_spec(HD), _row_spec(HD)],
                 out_specs=[_row_spec(D), _row_spec(KVW), _row_spec(KVW)],
                 out_shape=[_sds((R, D), _MXU), _sds((R, KVW), _MXU), _sds((R, KVW), _MXU)])(P, qg, kg, COS, SIN)


def _qkv_bwd(P, dq, dk, dv, qg, kg, COS, SIN, dP, D, KVW, nx, name):
    R, INW = P.shape
    W = D + 2 * KVW
    nq, nk = D // HD, KVW // HD

    def body(p_ref, dq_ref, dk_ref, dv_ref, qg_ref, kg_ref, cos_ref, sin_ref, dp_in, dp_ref, dqg_ref, dkg_ref):
        i = pl.program_id(0)
        cos, sin = cos_ref[...], sin_ref[...]

        @pl.when(i == 0)
        def _():
            dqg_ref[...] = jnp.zeros_like(dqg_ref)
            dkg_ref[...] = jnp.zeros_like(dkg_ref)

        def head_bwd(t, g, dout):
            r = lax.rsqrt(jnp.mean(t * t, axis=-1, keepdims=True) + EPS)
            n = t * r
            dy = dout * cos + _pair_swap(dout * sin)
            dn = dy * g
            return r * (dn - n * jnp.mean(dn * n, axis=-1, keepdims=True)), jnp.sum(dy * n, axis=0, keepdims=True)

        dqg = jnp.zeros((1, HD), F32)
        for h in range(nq):
            sl = slice(h * HD, (h + 1) * HD)
            dt, dg = head_bwd(p_ref[:, sl], qg_ref[...], jnp.where(i < nx, dq_ref[:, sl] * SM_SCALE, 0.0))
            dp_ref[:, sl] = dt.astype(dp_ref.dtype)
            dqg += dg
        dkg = jnp.zeros((1, HD), F32)
        for h in range(nk):
            sl = slice(h * HD, (h + 1) * HD)
            dt, dg = head_bwd(p_ref[:, D + h * HD:D + (h + 1) * HD], kg_ref[...], dk_ref[:, sl])
            dp_ref[:, D + h * HD:D + (h + 1) * HD] = dt.astype(dp_ref.dtype)
            dkg += dg
        dqg_ref[...] += dqg
        dkg_ref[...] += dkg
        dp_ref[:, D + KVW:W] = dv_ref[...].astype(dp_ref.dtype)

    return _call(body, name=name, grid=(R // TR,),
                 in_specs=[_row_spec(W), _row_spec(D, nx - 1), _row_spec(KVW), _row_spec(KVW), _full_spec((1, HD)),
                           _full_spec((1, HD)), _row_spec(HD), _row_spec(HD), pl.BlockSpec(memory_space=pl.ANY)],
                 out_specs=[_row_spec(W), _full_spec((1, HD)), _full_spec((1, HD))],
                 out_shape=[_sds((R, INW), _MXU), _sds((1, HD)), _sds((1, HD))],
                 aliases={8: 0})(P, dq, dk, dv, qg, kg, COS, SIN, dP)


def _stack_heads(ref, G, dtype=None):
    parts = [ref[:, g * HD:(g + 1) * HD] for g in range(G)]
    out = jnp.concatenate(parts, axis=0)
    return out if dtype is None else out.astype(dtype)


_KEY_CHUNKS = 4


def _key_chunks(R):
    unit = 256 if R % 256 == 0 else 16
    nt = R // unit
    n = min(_KEY_CHUNKS, nt)
    bounds = [0]
    for i in range(n):
        bounds.append(bounds[-1] + (nt // n + (1 if i < nt % n else 0)) * unit)
    return bounds


def _attn_fwd(q, k, v, S, G, name, jobs=()):
    R, KVW = k.shape
    D = q.shape[1]
    Kh = KVW // HD
    tq = 128
    kb = _key_chunks(R)

    def body(q_ref, k_ref, v_ref, o_ref, lse_ref):
        qs = _stack_heads(q_ref, G)
        m = l = acc = None
        for c in range(len(kb) - 1):
            s = _dot_nt(qs, k_ref[kb[c]:kb[c + 1], :])
            mc = jnp.max(s, axis=-1, keepdims=True)
            m_new = mc if c == 0 else jnp.maximum(m, mc)
            p = jnp.exp(s - m_new)
            ps = jnp.sum(p, axis=-1, keepdims=True)
            pv = _dot(p, v_ref[kb[c]:kb[c + 1], :])
            if c == 0:
                l, acc = ps, pv
            else:
                alpha = jnp.exp(m - m_new)
                l = alpha * l + ps
                acc = alpha * acc + pv
            m = m_new
        o = acc / l
        lse = m + jnp.log(l)
        for g in range(G):
            o_ref[:, g * HD:(g + 1) * HD] = o[g * tq:(g + 1) * tq, :]
            lse_ref[g] = jnp.broadcast_to(lse[g * tq:(g + 1) * tq, :], (tq, HD))

    return _call(body, jobs=jobs, name=name, grid=(Kh, S // tq),
                 in_specs=[pl.BlockSpec((tq, G * HD), lambda h, i: (i, h)), pl.BlockSpec((R, HD), lambda h, i: (0, h)),
                           pl.BlockSpec((R, HD), lambda h, i: (0, h))],
                 out_specs=[pl.BlockSpec((tq, G * HD), lambda h, i: (i, h)),
                            pl.BlockSpec((None, G, tq, HD), lambda h, i: (h, 0, i, 0))],
                 out_shape=[_sds((S, D)), _sds((Kh, G, S, HD))])(q, k, v)


def _attn_bwd(q, k, v, O, LSE, dOb, S, G, name, jobs=()):
    R, KVW = k.shape
    D = q.shape[1]
    Kh = KVW // HD
    tq = 128
    kb = _key_chunks(R)

    def body(q_ref, k_ref, v_ref, o_ref, lse_ref, do_ref, dq_ref, dk_ref, dv_ref):
        @pl.when(pl.program_id(1) == 0)
        def _():
            dk_ref[...] = jnp.zeros_like(dk_ref)
            dv_ref[...] = jnp.zeros_like(dv_ref)

        qs = _stack_heads(q_ref, G)
        do = _stack_heads(do_ref, G)
        o = _stack_heads(o_ref, G)
        lse = jnp.concatenate([lse_ref[g][:, 0:1] for g in range(G)], axis=0)
        delta = jnp.sum(do.astype(F32) * o, axis=-1, keepdims=True)
        dq = None
        for c in range(len(kb) - 1):
            rows = slice(kb[c], kb[c + 1])
            kk = k_ref[rows, :]
            p = jnp.exp(_dot_nt(qs, kk) - lse)
            dp = _dot_nt(do, v_ref[rows, :])
            ds = (p * (dp - delta)).astype(_MXU)
            dqc = _dot(ds, kk)
            dq = dqc if c == 0 else dq + dqc
            dk_ref[rows, :] += _dot_tn(ds, qs)
            dv_ref[rows, :] += _dot_tn(p, do)
        for g in range(G):
            dq_ref[:, g * HD:(g + 1) * HD] = dq[g * tq:(g + 1) * tq, :]

    qspec = pl.BlockSpec((tq, G * HD), lambda h, i: (i, h))
    kspec = pl.BlockSpec((R, HD), lambda h, i: (0, h))
    return _call(body, jobs=jobs, name=name, grid=(Kh, S // tq),
                 in_specs=[qspec, kspec, kspec, qspec, pl.BlockSpec((None, G, tq, HD), lambda h, i: (h, 0, i, 0)), qspec],
                 out_specs=[qspec, kspec, kspec],
                 out_shape=[_sds((S, D)), _sds((R, KVW)), _sds((R, KVW))])(q, k, v, O, LSE, dOb)


def _halo_specs(R, CB, col0):
    nt8 = TR // 8
    return [pl.BlockSpec((8, CB), lambda h, i: (jnp.maximum(i * nt8 - 1, 0), col0 + h)),
            pl.BlockSpec((TR, CB), lambda h, i: (i, col0 + h)),
            pl.BlockSpec((8, CB), lambda h, i: (jnp.minimum((i + 1) * nt8, R // 8 - 1), col0 + h))]


def _seq_pos(i, S, R, CB):
    t = i * TR - 8 + lax.broadcasted_iota(jnp.int32, (TR + 16, CB), 0)
    start = jnp.where(t >= S, S, 0)
    end = jnp.where(t >= S, R, S)
    return t - start, end - t


def _shift(cat, by):
    return pltpu.roll(cat, by % cat.shape[0], 0)


def _gate_mats(xcb, w_ref, dirn, nb):
    return jnp.concatenate([_dot(xcb[:, b * HD:(b + 1) * HD], w_ref[dirn, b]) for b in range(nb)], axis=1)


def _lru_gates_fwd(P, conv_w, conv_b, WA, WX, ba, bx, lam, S, D, col0, name):
    R = P.shape[0]
    CB = D // 2
    nb = CB // HD

    def body(xp_ref, x_ref, xn_ref, cw_ref, cb_ref, wa_ref, wx_ref, ba_ref, bx_ref, lam_ref,
             xc_ref, af_ref, uf_ref, ab_ref, ub_ref):
        i = pl.program_id(1)
        cat = jnp.concatenate([xp_ref[...], x_ref[...], xn_ref[...]], axis=0)
        from_start, to_end = _seq_pos(i, S, R, CB)
        conv = (cb_ref[...] + cw_ref[2:3, :] * cat
                + cw_ref[0:1, :] * jnp.where(from_start >= 2, _shift(cat, 2), 0.0)
                + cw_ref[1:2, :] * jnp.where(from_start >= 1, _shift(cat, 1), 0.0)
                + cw_ref[3:4, :] * jnp.where(to_end >= 2, _shift(cat, -1), 0.0))
        xc = conv[8:8 + TR, :]
        xc_ref[...] = xc
        xcb = xc.astype(_MXU)
        for dirn, (a_ref, u_ref) in enumerate(((af_ref, uf_ref), (ab_ref, ub_ref))):
            ra = _sigmoid(_gate_mats(xcb, wa_ref, dirn, nb) + ba_ref[dirn:dirn + 1, :])
            ia = _sigmoid(_gate_mats(xcb, wx_ref, dirn, nb) + bx_ref[dirn:dirn + 1, :])
            nl = -lam_ref[dirn:dirn + 1, :]
            sp = jnp.maximum(nl, 0.0) + jnp.log(1.0 + jnp.exp(-jnp.abs(nl)))
            la = (-LRU_C) * ra * sp
            a_ref[...] = jnp.exp(la)
            u_ref[...] = jnp.sqrt(1.0 - jnp.exp(2.0 * la)) * (ia * xc)

    def par(r):
        return pl.BlockSpec((r, CB), lambda h, i: (0, h))

    wspec = pl.BlockSpec((2, nb, HD, HD), lambda h, i: (0, h, 0, 0))
    out = pl.BlockSpec((TR, CB), lambda h, i: (i, h))
    return _call(body, name=name, grid=(2, R // TR),
                 in_specs=_halo_specs(R, CB, col0) + [par(4), par(1), wspec, wspec, par(2), par(2), par(2)],
                 out_specs=[out] * 5, out_shape=[_sds((R, D))] * 5,
                 )(P, P, P, conv_w, conv_b, WA, WX, ba, bx, lam)


def _lru_gates_bwd(xc, hf, hb, gf, gb, WA, WX, ba, bx, lam, S, name):
    R, D = xc.shape
    CB = D // 2
    nb = CB // HD

    def body(xc_ref, hfq_ref, hf_ref, hfl_ref, hb_ref, hbn_ref, gf_ref, gb_ref, wa_ref, wx_ref, ba_ref, bx_ref, lam_ref,
             dxc_ref, dwa_ref, dwx_ref, dba_ref, dbx_ref, dlam_ref):
        @pl.when(pl.program_id(1) == 0)
        def _():
            for r in (dwa_ref, dwx_ref, dba_ref, dbx_ref, dlam_ref):
                r[...] = jnp.zeros_like(r)

        t = pl.program_id(1) * TR + lax.broadcasted_iota(jnp.int32, (TR, CB), 0)
        hfp = _shift(jnp.concatenate([hfq_ref[...], hf_ref[...]], axis=0), 1)[8:8 + TR, :]
        hfp = jnp.where(t == 0, hfl_ref[7:8, :], jnp.where(t == S, 0.0, hfp))
        hbp = _shift(jnp.concatenate([hb_ref[...], hbn_ref[...]], axis=0), -1)[0:TR, :]
        hbp = jnp.where(t == R - 1, 0.0, hbp)

        xc = xc_ref[...]
        xcb = xc.astype(_MXU)
        dxc = jnp.zeros_like(xc)
        for dirn, (hp, g_ref) in enumerate(((hfp, gf_ref), (hbp, gb_ref))):
            ra = _sigmoid(_gate_mats(xcb, wa_ref, dirn, nb) + ba_ref[dirn:dirn + 1, :])
            ia = _sigmoid(_gate_mats(xcb, wx_ref, dirn, nb) + bx_ref[dirn:dirn + 1, :])
            nl = -lam_ref[dirn:dirn + 1, :]
            sp = jnp.maximum(nl, 0.0) + jnp.log(1.0 + jnp.exp(-jnp.abs(nl)))
            la = (-LRU_C) * ra * sp
            a = jnp.exp(la)
            e2 = jnp.exp(2.0 * la)
            s = jnp.sqrt(1.0 - e2)
            du = g_ref[...]
            dla = du * hp * a - du * (ia * xc) * (e2 / s)
            dxc += du * s * ia
            dza = (dla * (-LRU_C) * sp) * ra * (1.0 - ra)
            dzx = (du * s * xc) * ia * (1.0 - ia)
            dlam_ref[dirn:dirn + 1, :] += jnp.sum(dla * (LRU_C * ra) * _sigmoid(nl), axis=0, keepdims=True)
            dba_ref[dirn:dirn + 1, :] += jnp.sum(dza, axis=0, keepdims=True)
            dbx_ref[dirn:dirn + 1, :] += jnp.sum(dzx, axis=0, keepdims=True)
            dzab, dzxb = dza.astype(_MXU), dzx.astype(_MXU)
            parts = []
            for b in range(nb):
                sl = slice(b * HD, (b + 1) * HD)
                dwa_ref[dirn, b] += _dot_tn(xcb[:, sl], dzab[:, sl])
                dwx_ref[dirn, b] += _dot_tn(xcb[:, sl], dzxb[:, sl])
                parts.append(_dot_nt(dzab[:, sl], wa_ref[dirn, b]) + _dot_nt(dzxb[:, sl], wx_ref[dirn, b]))
            dxc += jnp.concatenate(parts, axis=1)
        dxc_ref[...] = dxc

    def par(r):
        return pl.BlockSpec((r, CB), lambda h, i: (0, h))

    wspec = pl.BlockSpec((2, nb, HD, HD), lambda h, i: (0, h, 0, 0))
    tile = pl.BlockSpec((TR, CB), lambda h, i: (i, h))
    before, _, after = _halo_specs(R, CB, 0)
    last = pl.BlockSpec((8, CB), lambda h, i: (R // 8 - 1, h))
    nbt = D // HD
    return _call(body, name=name, grid=(2, R // TR),
                 in_specs=[tile, before, tile, last, tile, after, tile, tile, wspec, wspec, par(2), par(2), par(2)],
                 out_specs=[tile, wspec, wspec, par(2), par(2), par(2)],
                 out_shape=[_sds((R, D)), _sds((2, nbt, HD, HD)), _sds((2, nbt, HD, HD)), _sds((2, D)), _sds((2, D)),
                            _sds((2, D))])(xc, hf, hf, hf, hb, hb, gf, gb, WA, WX, ba, bx, lam)


def _scan_rows(n_groups, step, init):
    return lax.fori_loop(0, n_groups, lambda gi, c: step(pl.multiple_of(gi * 8, 8), c), init)


def _lru_scan_fwd(af, uf, ab, ub, S, name):
    R, D = af.shape
    W = min(SCAN_W, D)
    nm, nx = R // TR, S // TR
    nc = nm - nx
    ng = TR // 8

    def body(af_ref, uf_ref, ab_ref, ub_ref, hf_ref, hb_ref, cf_ref, cb_ref):
        @pl.when(pl.program_id(1) == 0)
        def _():
            cf_ref[...] = jnp.zeros_like(cf_ref)
            cb_ref[...] = jnp.zeros_like(cb_ref)

        def step(base, carry):
            hf, hb = carry
            baseb = pl.multiple_of(TR - 8 - base, 8)
            for r in range(8):
                tf, tb = base + r, baseb + 7 - r
                hf = af_ref[pl.ds(tf, 1), :] * hf + uf_ref[pl.ds(tf, 1), :]
                hf_ref[pl.ds(tf, 1), :] = hf
                hb = ab_ref[pl.ds(tb, 1), :] * hb + ub_ref[pl.ds(tb, 1), :]
                hb_ref[pl.ds(tb, 1), :] = hb
            return hf, hb

        hf, hb = _scan_rows(ng, step, (cf_ref[0:1, :], cb_ref[0:1, :]))
        cf_ref[0:1, :] = hf
        cb_ref[0:1, :] = hb

    fmap = lambda j, s: (jnp.where(s < nc, nx + s, s - nc), j)
    bmap = lambda j, s: (nm - 1 - s, j)
    fs, bs = pl.BlockSpec((TR, W), fmap), pl.BlockSpec((TR, W), bmap)
    return _call(body, name=name, grid=(D // W, nm), in_specs=[fs, fs, bs, bs], out_specs=[fs, bs],
                 out_shape=[_sds((R, D))] * 2, scratch=[pltpu.VMEM((8, W), F32), pltpu.VMEM((8, W), F32)])(af, uf, ab, ub)


def _lru_scan_bwd(af, ab, dhs, S, name):
    R, D = af.shape
    W = min(SCAN_W, D)
    nm, nx = R // TR, S // TR
    ng = TR // 8

    def body(af_ref, dhf_ref, ab_ref, dhb_ref, gf_ref, gb_ref, cf_ref, cb_ref):
        @pl.when(pl.program_id(1) == 0)
        def _():
            cf_ref[...] = jnp.zeros_like(cf_ref)
            cb_ref[...] = jnp.zeros_like(cb_ref)

        def step(base, carry):
            cf, cb = carry
            based = pl.multiple_of(TR - 8 - base, 8)
            for r in range(8):
                tf, tb = based + 7 - r, base + r
                g = dhf_ref[pl.ds(tf, 1), :] + cf
                gf_ref[pl.ds(tf, 1), :] = g
                cf = af_ref[pl.ds(tf, 1), :] * g
                g = dhb_ref[pl.ds(tb, 1), :] + cb
                gb_ref[pl.ds(tb, 1), :] = g
                cb = ab_ref[pl.ds(tb, 1), :] * g
            return cf, cb

        cf, cb = _scan_rows(ng, step, (cf_ref[0:1, :], cb_ref[0:1, :]))
        cf_ref[0:1, :] = cf
        cb_ref[0:1, :] = cb

    fmap = lambda j, s: (jnp.where(s < nx, nx - 1 - s, nm - 1 - (s - nx)), j)
    bmap = lambda j, s: (s, j)
    fs, bs = pl.BlockSpec((TR, W), fmap), pl.BlockSpec((TR, W), bmap)
    return _call(body, name=name, grid=(D // W, nm), in_specs=[fs, fs, bs, bs], out_specs=[fs, bs],
                 out_shape=[_sds((R, D))] * 2, scratch=[pltpu.VMEM((8, W), F32), pltpu.VMEM((8, W), F32)])(af, dhs, ab, dhs)


def _merge_fwd(P, hf, hb, O, S, D, col_lg, name):
    R = P.shape[0]
    CB = D // 2
    nx = S // TR

    def body(lg_ref, ga_ref, gl_ref, hf_ref, hb_ref, o_ref, mix_ref):
        ge, _ = _gelu_and_grad(lg_ref[...])
        lru = (hf_ref[...] + hb_ref[...]) * ge
        mix_ref[...] = (_sigmoid(ga_ref[...]) * o_ref[...] + _sigmoid(gl_ref[...]) * lru).astype(mix_ref.dtype)

    def col(c0):
        return pl.BlockSpec((TR, CB), lambda h, i: (i, c0 + h))

    return _call(body, name=name, grid=(2, R // TR),
                 in_specs=[col(col_lg), col(col_lg + 2), col(col_lg + 4), col(0), col(0),
                           pl.BlockSpec((TR, CB), lambda h, i: (jnp.minimum(i, nx - 1), h))],
                 out_specs=col(0), out_shape=_sds((R, D), _MXU))(P, P, P, hf, hb, O)


def _merge_bwd(dmix, P, hf, hb, O, S, D, col_lg, name, jobs=()):
    R = P.shape[0]
    CB = D // 2
    nx = S // TR

    def body(dm_ref, lg_ref, ga_ref, gl_ref, hf_ref, hb_ref, o_ref, do_ref, dhs_ref, dp_ref, stash, sems):
        h, i = pl.program_id(0), pl.program_id(1)
        dm = jnp.where(i < nx, dm_ref[...], 0.0)
        sa, sl = _sigmoid(ga_ref[...]), _sigmoid(gl_ref[...])
        ge, dge = _gelu_and_grad(lg_ref[...])
        hs = hf_ref[...] + hb_ref[...]
        dl = dm * sl
        do_ref[...] = (dm * sa).astype(do_ref.dtype)
        dhs_ref[...] = dl * ge
        stash[0] = (dl * hs * dge).astype(stash.dtype)
        stash[1] = (dm * o_ref[...] * sa * (1.0 - sa)).astype(stash.dtype)
        stash[2] = (dm * (hs * ge) * sl * (1.0 - sl)).astype(stash.dtype)
        rows = pl.ds(pl.multiple_of(i * TR, TR), TR)
        copies = [pltpu.make_async_copy(stash.at[sec], dp_ref.at[rows, pl.ds(pl.multiple_of((col_lg + 2 * sec + h) * CB, CB), CB)],
                                        sems.at[sec]) for sec in range(3)]
        for cp in copies:
            cp.start()
        for cp in copies:
            cp.wait()

    def col(c0):
        return pl.BlockSpec((TR, CB), lambda h, i: (i, c0 + h))

    xrow = pl.BlockSpec((TR, CB), lambda h, i: (jnp.minimum(i, nx - 1), h))
    return _call(body, jobs=jobs, name=name, grid=(2, R // TR),
                 in_specs=[xrow, col(col_lg), col(col_lg + 2), col(col_lg + 4), col(0), col(0), xrow],
                 out_specs=[col(0), col(0), pl.BlockSpec(memory_space=pl.ANY)],
                 out_shape=[_sds((R, D), _MXU), _sds((R, D)), _sds(P.shape, _MXU)],
                 scratch=[pltpu.VMEM((3, TR, CB), _MXU), pltpu.SemaphoreType.DMA((3,))])(dmix, P, P, P, hf, hb, O)


def _conv_bwd(dxc, P, conv_w, dP, S, D, col0, name, jobs=()):
    R = P.shape[0]
    CB = D // 2

    def body(dp_, d_ref, dn_, xp_ref, x_ref, xn_ref, cw_ref, dp_in, dpo_ref, dcw_ref, dcb_ref):
        i = pl.program_id(1)

        @pl.when(i == 0)
        def _():
            dcw_ref[...] = jnp.zeros_like(dcw_ref)
            dcb_ref[...] = jnp.zeros_like(dcb_ref)

        d = d_ref[...]
        catd = jnp.concatenate([dp_[...], d, dn_[...]], axis=0)
        catx = jnp.concatenate([xp_ref[...], x_ref[...], xn_ref[...]], axis=0)
        from_start, to_end = _seq_pos(i, S, R, CB)
        dxl = (cw_ref[2:3, :] * catd
               + cw_ref[0:1, :] * jnp.where(to_end >= 3, _shift(catd, -2), 0.0)
               + cw_ref[1:2, :] * jnp.where(to_end >= 2, _shift(catd, -1), 0.0)
               + cw_ref[3:4, :] * jnp.where(from_start >= 1, _shift(catd, 1), 0.0))
        dpo_ref[...] = dxl[8:8 + TR, :].astype(dpo_ref.dtype)
        taps = (jnp.where(from_start >= 2, _shift(catx, 2), 0.0), jnp.where(from_start >= 1, _shift(catx, 1), 0.0),
                catx, jnp.where(to_end >= 2, _shift(catx, -1), 0.0))
        for kk in range(4):
            dcw_ref[kk:kk + 1, :] += jnp.sum(d * taps[kk][8:8 + TR, :], axis=0, keepdims=True)
        dcb_ref[...] += jnp.sum(d, axis=0, keepdims=True)

    return _call(body, jobs=jobs, name=name, grid=(2, R // TR),
                 in_specs=_halo_specs(R, CB, 0) + _halo_specs(R, CB, col0)
                 + [pl.BlockSpec((4, CB), lambda h, i: (0, h)), pl.BlockSpec(memory_space=pl.ANY)],
                 out_specs=[pl.BlockSpec((TR, CB), lambda h, i: (i, col0 + h)), pl.BlockSpec((4, CB), lambda h, i: (0, h)),
                            pl.BlockSpec((1, CB), lambda h, i: (0, h))],
                 out_shape=[_sds(dP.shape, dP.dtype), _sds((4, D)), _sds((1, D))],
                 aliases={7: 0})(dxc, dxc, dxc, P, P, P, conv_w, dP)


def _rope_tables(S, C):
    t = jnp.arange(S, dtype=jnp.int32)
    row = (t // GRID_W).astype(F32)
    col = (t % GRID_W).astype(F32)
    axis_dims = HD // 2
    freqs = ROPE_THETA ** (-jnp.arange(0, axis_dims, 2, dtype=F32) / axis_dims)
    ang = jnp.concatenate([row[:, None] * freqs, col[:, None] * freqs], axis=-1)
    cos = jnp.repeat(jnp.cos(ang), 2, axis=-1)
    sin = jnp.repeat(jnp.sin(ang), 2, axis=-1) * jnp.tile(jnp.array([-1.0, 1.0], F32), HD // 2)
    return (jnp.concatenate([cos, jnp.ones((C, HD), F32)], axis=0),
            jnp.concatenate([sin, jnp.zeros((C, HD), F32)], axis=0))


def _local_step(x, ctx, target, modx, modc, ng, shards, qg, kg, conv_w, conv_b, WA, WX, ba, bx, lam, fg, idx, opt):
    S, D = x.shape
    C = ctx.shape[0]
    R = S + C
    nx = S // TR
    Nb = shards['w_in'].shape[-1]
    Fb = shards['wd0'].shape[1]
    KVW = (ND * Nb - 5 * D) // 2
    G = D // KVW
    CB = D // 2
    col_lx = (D + 2 * KVW) // CB
    assert S % TR == 0 and C % TR == 0 and (D + 2 * KVW) % CB == 0 and CB % HD == 0
    mod2 = jnp.stack([modx, modc])
    X0 = jnp.concatenate([x, ctx], axis=0)
    COS, SIN = _rope_tables(S, C)
    ng0, ng1, ng2 = ng[0:1], ng[1:2], ng[2:3]
    ag = lambda n: _ag_job(shards[n])
    sib = lambda Gp: _rs_sibling_job(Gp.reshape(ND, -1, Gp.shape[-1]))
    add = lambda Gp, bufA, tag: _rs_add(Gp.reshape(ND, -1, Gp.shape[-1]), bufA, idx, f"{tag}_rs_add")
    out = {}

    def tn_specs(rows):
        tm = _tm(rows)
        return pl.BlockSpec((None, tm, Fb), lambda d, r: (d, r, 0)), pl.BlockSpec((tm, D), lambda d, r: (r, 0))

    wd_spec = pl.BlockSpec((None, None, Fb, D), lambda d, r: (d, 0, 0, 0))

    ((WG0,),) = _comm_call([ag('wg0')], "ag_wg0")
    H1 = _norm_mod_fwd(X0, ng0, mod2, 0, R, nx, "ffn1_norm")
    G1, ((WU0,),) = _ffn_gate(H1, WG0, R, "ffn1_gate", jobs=[ag('wu0')])
    (U1, A1), ((WD0,),) = _ffn_up(H1, None, WU0, R, "ffn1_up", jobs=[ag('wd0')], G=G1)
    (Y1, X1, H2), ((WIN,),) = _ffn_down(A1, WD0, 0, X0, mod2, 0, R, S, "ffn1_down", jobs=[ag('w_in')], next_norm=(ng1, 1))
    P, ((WOUT,), (WG1,)) = _proj_in(H2, WIN, "proj_in", jobs=[ag('w_out'), ag('wg1')])
    q, k, v = _qkv_prep(P, qg, kg, COS, SIN, D, KVW, "qkv_prep")
    (O, LSE), ((WU1,), (WD1,)) = _attn_fwd(q, k, v, S, G, "attn_fwd", jobs=[ag('wu1'), ag('wd1')])
    WOUT = WOUT.reshape(D, D)
    xc, af, uf, ab, ub = _lru_gates_fwd(P, conv_w, conv_b, WA, WX, ba, bx, lam, S, D, col_lx, "lru_gates")
    hf, hb = _lru_scan_fwd(af, uf, ab, ub, S, "lru_scan")
    mixb = _merge_fwd(P, hf, hb, O, S, D, col_lx + 2, "merge")
    Z, X2, H3 = _proj_out(mixb, WOUT, X1, mod2, ng2, S, "proj_out")
    G3, U3, A3 = _ffn_up(H3, WG1, WU1, S, "ffn2_up")
    loss, dX3, dY3b, dg3, dfg = _ffn_down_loss(A3, WD1, X2, mod2, 2, fg, target, "ffn2_down_loss")
    dg3 = dg3[None]

    dG3, dU3 = _ffn_dact(dY3b, WD1, 0, G3, U3, S, "ffn2_dact")
    blk, row = tn_specs(S)
    dWD1 = _mm_tn(A3, blk, dY3b, row, (ND, 1, Fb, D), wd_spec, S, "ffn2_dwd")
    dWG1 = _mm_tn(dG3, blk, H3, row, (ND, 1, Fb, D), wd_spec, S, "ffn2_dwg")
    dWU1 = _mm_tn(dU3, blk, H3, row, (ND, 1, Fb, D), wd_spec, S, "ffn2_dwu")
    dH3, ((a_wd1,), (a_wg1,), (a_wu1,)) = _ffn_dh(dG3, dU3, WG1, WU1, S, "ffn2_dh", jobs=[sib(dWD1), sib(dWG1), sib(dWU1)])
    T_wd1, own_wd1 = add(dWD1, a_wd1, "wd1")
    T_wg1, own_wg1 = add(dWG1, a_wg1, "wg1")
    T_wu1, own_wu1 = add(dWU1, a_wu1, "wu1")
    dX2, dsh3, dsc3, dng2, dZb, dg2 = _norm_mod_bwd(X2, dH3, dX3, ng2, mod2, 2, S, nx, nx, "ffn2_dnorm", branch=(Z, 1, 1.0))

    dmix = _dproj_out(dZb, WOUT, "dproj_out")
    dWOUT = _mm_tn(mixb, pl.BlockSpec((_tm(S), D // ND), lambda d, r: (r, d)), dZb, pl.BlockSpec((_tm(S), D), lambda d, r: (r, 0)),
                   (ND, D // ND, D), pl.BlockSpec((None, D // ND, D), lambda d, r: (d, 0, 0)), S, "dw_out")
    (dOb, dhs, dP), ((a_wout,),) = _merge_bwd(dmix, P, hf, hb, O, S, D, col_lx + 2, "merge_bwd", jobs=[sib(dWOUT)])
    T_wout, own_wout = add(dWOUT, a_wout, "w_out")
    gf, gb = _lru_scan_bwd(af, ab, dhs, S, "lru_scan_bwd")
    dxc, dWA, dWX, dba, dbx, dlam = _lru_gates_bwd(xc, hf, hb, gf, gb, WA, WX, ba, bx, lam, S, "lru_gates_bwd")
    dLW = jnp.stack([dWA, dWX]).reshape(ND, -1, HD)
    (dP, dconv_w, dconv_b), ((a_lw,),) = _conv_bwd(dxc, P, conv_w, dP, S, D, col_lx, "conv_bwd", jobs=[sib(dLW)])
    T_lw, own_lw = add(dLW, a_lw, "lru_w")
    (dq, dk, dv), ((b_wd1,), (b_wg1,), (b_wu1,), (b_wout,), (b_lw,)) = _attn_bwd(
        q, k, v, O, LSE, dOb, S, G, "attn_bwd",
        jobs=[_rs_chips_job(T_wd1), _rs_chips_job(T_wg1), _rs_chips_job(T_wu1), _rs_chips_job(T_wout), _rs_chips_job(T_lw)])
    fin_wd = _rs_finish(own_wd1, b_wd1, "wd1_rs_finish", opt['ffn_wd'], 1, 2)
    fin_wg = _rs_finish(own_wg1, b_wg1, "wg1_rs_finish", opt['ffn_wg'], 1, 2)
    fin_wu = _rs_finish(own_wu1, b_wu1, "wu1_rs_finish", opt['ffn_wu'], 1, 2)
    out['w_out'] = _rs_finish(own_wout, b_wout, "w_out_rs_finish", opt['w_out'])
    (lw_sum,) = _rs_finish(own_lw, b_lw, "lru_w_rs_finish")
    dP, dqg, dkg = _qkv_bwd(P, dq, dk, dv, qg, kg, COS, SIN, dP, D, KVW, nx, "qkv_bwd")
    dH2, ((lw_full,),) = _dproj_in(dP, WIN, "dproj_in", jobs=[_ag_job(lw_sum)])
    dWIN = _mm_tn(H2, pl.BlockSpec((_tm(R), D), lambda d, r: (r, 0)), dP, pl.BlockSpec((_tm(R), Nb), lambda d, r: (r, d)),
                  (ND, D, Nb), pl.BlockSpec((None, D, Nb), lambda d, r: (d, 0, 0)), R, "dw_in")
    dX1, dsh2, dsc2, dng1, dY1b, dg1 = _norm_mod_bwd(X1, dH2, dX2, ng1, mod2, 1, R, nx, nx, "mix_dnorm", branch=(Y1, 0, 0.5))

    (dG1, dU1), ((a_win,),) = _ffn_dact(dY1b, WD0, 0, G1, U1, R, "ffn1_dact", jobs=[sib(dWIN)])
    T_win, own_win = add(dWIN, a_win, "w_in")
    blk, row = tn_specs(R)
    dWD0, ((b_win01,),) = _mm_tn(A1, blk, dY1b, row, (ND, 1, Fb, D), wd_spec, R, "ffn1_dwd", jobs=[_rs_chips_job(T_win, (0, 1))])
    dWG0, ((b_win2,), (a_wd0,)) = _mm_tn(dG1, blk, H1, row, (ND, 1, Fb, D), wd_spec, R, "ffn1_dwg",
                                         jobs=[_rs_chips_job(T_win, (2,)), sib(dWD0)])
    out['w_in'] = _rs_finish(own_win, [(b_win01, 0), (b_win01, 1), (b_win2, 0)], "w_in_rs_finish", opt['w_in'])
    T_wd0, own_wd0 = add(dWD0, a_wd0, "wd0")
    dWU0, ((a_wg0,), (b_wd0,)) = _mm_tn(dU1, blk, H1, row, (ND, 1, Fb, D), wd_spec, R, "ffn1_dwu",
                                        jobs=[sib(dWG0), _rs_chips_job(T_wd0)])
    T_wg0, own_wg0 = add(dWG0, a_wg0, "wg0")
    out['ffn_wd'] = _rs_finish(own_wd0, b_wd0, "wd0_rs_finish", opt['ffn_wd'], 0, 2, fin_wd)
    dH1, ((a_wu0,), (b_wg0,)) = _ffn_dh(dG1, dU1, WG0, WU0, R, "ffn1_dh", jobs=[sib(dWU0), _rs_chips_job(T_wg0)])
    T_wu0, own_wu0 = add(dWU0, a_wu0, "wu0")
    out['ffn_wg'] = _rs_finish(own_wg0, b_wg0, "wg0_rs_finish", opt['ffn_wg'], 0, 2, fin_wg)
    (dX0, dsh1, dsc1, dng0), ((b_wu0,),) = _norm_mod_bwd(
        X0, dH1, dX1, ng0, mod2, 0, R, nx, R // TR, "ffn1_dnorm", jobs=[_rs_chips_job(T_wu0, (0, 1))])
    out['wu0_pending'] = (T_wu0, own_wu0, b_wu0, fin_wu)
    out['lru_w'] = lw_full

    zero = jnp.zeros((1, D), F32)
    dmodx = jnp.concatenate([dsh1[0], dsc1[0], dg1[0], dsh2[0], dsc2[0], dg2[0], dsh3[0], dsc3[0], dg3[0]], axis=0)
    dmodc = jnp.concatenate([dsh1[1], dsc1[1], dg1[1], dsh2[1], dsc2[1], zero, zero, zero, zero], axis=0)
    out.update(loss=loss, grad_x=dX0[:S], dmodx=dmodx, dmodc=dmodc, norm_g=jnp.concatenate([dng0, dng1, dng2], axis=0),
               q_norm_g=dqg, k_norm_g=dkg, conv_w=dconv_w, conv_b=dconv_b, lru_ba=dba, lru_bx=dbx,
               lru_lambda=dlam, final_norm_g=dfg)
    return out


def _mesh_pos():
    return lax.axis_index("x"), lax.axis_index("y"), lax.axis_index("c")


def _all_gather(xb, name, in_vmem=False):
    space = pltpu.VMEM if in_vmem else pl.ANY

    def body(x_ref, out_ref, send_sems, recv_sems, local_sem):
        x, y, c = _mesh_pos()
        me, sibling = (x, y, c), (x, y, 1 - c)
        chips = [(1 - x, y), (x, 1 - y), (1 - x, 1 - y)]

        def slot(px, py, pc):
            return out_ref.at[4 * px + 2 * py + pc]

        def copy(k, block, to, src=None):
            return pltpu.make_async_remote_copy(
                src_ref=slot(*block) if src is None else src, dst_ref=slot(*block),
                send_sem=send_sems.at[k], recv_sem=recv_sems.at[k], device_id=to, device_id_type=MESH)

        mine = pltpu.make_async_copy(x_ref, slot(*me), local_sem)
        mine.start()
        first = [copy(0, me, sibling, src=x_ref)]
        first += [copy(1 + j, me, (*chip, c), src=x_ref) for j, chip in enumerate(chips)]
        for cp in first:
            cp.start()
        passed = [copy(4 + j, (*chip, c), sibling) for j, chip in enumerate(chips)]
        for j, chip in enumerate(chips):
            copy(1 + j, (*chip, c), me).wait_recv()
            passed[j].start()
        copy(0, sibling, me).wait_recv()
        for j, chip in enumerate(chips):
            copy(4 + j, (*chip, 1 - c), me).wait_recv()
        for cp in first + passed:
            cp.wait_send()
        mine.wait()

    return pl.pallas_call(
        body, name=name, out_shape=_sds((ND,) + xb.shape, xb.dtype),
        in_specs=[pl.BlockSpec(memory_space=space)], out_specs=pl.BlockSpec(memory_space=space),
        scratch_shapes=[pltpu.SemaphoreType.DMA((7,)), pltpu.SemaphoreType.DMA((7,)), pltpu.SemaphoreType.DMA(())])(xb)


def _ag_job(xb):
    def parts(ins, outs, sems, starting=False):
        x_ref, out_ref = ins[0], outs[0]
        send_sems, recv_sems, local_sem = sems
        x, y, c = _mesh_pos()
        me, sibling = (x, y, c), (x, y, 1 - c)
        chips = [(1 - x, y), (x, 1 - y), (1 - x, 1 - y)]

        def slot(px, py, pc):
            return out_ref.at[4 * px + 2 * py + pc]

        def copy(k, block, to, src=None):
            return pltpu.make_async_remote_copy(
                src_ref=slot(*block) if src is None else src, dst_ref=slot(*block),
                send_sem=send_sems.at[k], recv_sem=recv_sems.at[k], device_id=to, device_id_type=MESH)

        mine = pltpu.make_async_copy(x_ref, slot(*me), local_sem)
        first = [copy(0, me, sibling, src=x_ref)] + [copy(1 + j, me, (*chip, c), src=x_ref) for j, chip in enumerate(chips)]
        if starting:
            return mine, first
        passed = [copy(4 + j, (*chip, c), sibling) for j, chip in enumerate(chips)]
        landed = [copy(1 + j, (*chip, c), me) for j, chip in enumerate(chips)]
        from_sibling = [copy(0, sibling, me)] + [copy(4 + j, (*chip, 1 - c), me) for j, chip in enumerate(chips)]
        return mine, first, passed, landed, from_sibling

    def start(ins, outs, sems):
        mine, first = parts(ins, outs, sems, starting=True)
        mine.start()
        for cp in first:
            cp.start()

    def finish(ins, outs, sems):
        mine, first, passed, landed, from_sibling = parts(ins, outs, sems)
        for j in range(3):
            landed[j].wait_recv()
            passed[j].start()
        for cp in from_sibling:
            cp.wait_recv()
        for cp in first + passed:
            cp.wait_send()
        mine.wait()

    return _Job([xb], [_sds((ND,) + xb.shape, xb.dtype)],
                [pltpu.SemaphoreType.DMA((7,)), pltpu.SemaphoreType.DMA((7,)), pltpu.SemaphoreType.DMA(())], start, finish)


def _rs_sibling_job(Gp):
    def copies(ins, outs, sems):
        x, y, c = _mesh_pos()
        return [pltpu.make_async_remote_copy(
            src_ref=ins[0].at[2 * k + (1 - c)], dst_ref=outs[0].at[k], send_sem=sems[0].at[k], recv_sem=sems[1].at[k],
            device_id=(x, y, 1 - c), device_id_type=MESH) for k in range(4)]

    def start(ins, outs, sems):
        for cp in copies(ins, outs, sems):
            cp.start()

    def finish(ins, outs, sems):
        cps = copies(ins, outs, sems)
        for cp in cps:
            cp.wait_recv()
        for cp in cps:
            cp.wait_send()

    return _Job([Gp], [_sds((4,) + Gp.shape[1:], Gp.dtype)],
                [pltpu.SemaphoreType.DMA((4,)), pltpu.SemaphoreType.DMA((4,))], start, finish)


def _rs_chips_job(T, dests=(0, 1, 2)):
    def copies(ins, outs, sems):
        x, y, c = _mesh_pos()
        chips = [(1 - x, y), (x, 1 - y), (1 - x, 1 - y)]
        cps = []
        for i, j in enumerate(dests):
            px, py = chips[j]
            cps.append(pltpu.make_async_remote_copy(
                src_ref=ins[0].at[2 * px + py], dst_ref=outs[0].at[i], send_sem=sems[0].at[i], recv_sem=sems[1].at[i],
                device_id=(px, py, c), device_id_type=MESH))
        return cps

    def start(ins, outs, sems):
        for cp in copies(ins, outs, sems):
            cp.start()

    def finish(ins, outs, sems):
        cps = copies(ins, outs, sems)
        for cp in cps:
            cp.wait_recv()
        for cp in cps:
            cp.wait_send()

    n = len(dests)
    return _Job([T], [_sds((n,) + T.shape[1:], T.dtype)],
                [pltpu.SemaphoreType.DMA((n,)), pltpu.SemaphoreType.DMA((n,))], start, finish)


def _tile_rows(rows, cols):
    best = None
    for t in range(16, rows + 1, 16):
        if rows % t == 0 and t * cols * 4 <= (1 << 20):
            best = t
    return best if best is not None else rows


def _prefetch_call(body, *, name, grid, in_specs, out_specs, out_shape):
    return pl.pallas_call(
        body, name=name, out_shape=out_shape,
        grid_spec=pltpu.PrefetchScalarGridSpec(num_scalar_prefetch=1, grid=grid, in_specs=in_specs, out_specs=out_specs),
        compiler_params=pltpu.CompilerParams(dimension_semantics=("arbitrary",) * len(grid), vmem_limit_bytes=VMEM_LIMIT))


def _rs_add(Gp, bufA, idx, name):
    rows, cols = Gp.shape[1:]
    tr = _tile_rows(rows, cols)

    def body(i_ref, g_ref, a_ref, t_ref, own_ref):
        t = g_ref[...] + a_ref[...]
        t_ref[...] = t.astype(t_ref.dtype)

        @pl.when(pl.program_id(1) == i_ref[1])
        def _():
            own_ref[...] = t

    return _prefetch_call(
        body, name=name, grid=(rows // tr, 4),
        in_specs=[pl.BlockSpec((None, tr, cols), lambda r, k, i_ref: (2 * k + i_ref[0], r, 0)),
                  pl.BlockSpec((None, tr, cols), lambda r, k, i_ref: (k, r, 0))],
        out_specs=[pl.BlockSpec((None, tr, cols), lambda r, k, i_ref: (k, r, 0)),
                   pl.BlockSpec((tr, cols), lambda r, k, i_ref: (r, 0))],
        out_shape=[_sds((4, rows, cols), jnp.bfloat16), _sds((rows, cols))])(idx, Gp, bufA)


def _adam(w, g, m, v):
    m = ADAM_B1 * m + (1.0 - ADAM_B1) * g
    v = ADAM_B2 * v + (1.0 - ADAM_B2) * (g * g)
    m_hat = m / (1.0 - ADAM_B1 ** ADAM_STEP)
    v_hat = v / (1.0 - ADAM_B2 ** ADAM_STEP)
    return -ADAM_LR * (m_hat / (jnp.sqrt(v_hat) + ADAM_EPS) + ADAM_WD * w), m, v


def _rs_finish(Town, bufB, name, wmv=None, slab=0, n_slabs=1, prev=None):
    rows, cols = Town.shape
    tr = _tile_rows(rows, cols)
    nr = rows // tr
    n_in = 4 + (3 if wmv is not None else 0)
    n_out = 4 if wmv is not None else 1

    def body(*refs):
        ins, outs = refs[:n_in], refs[len(refs) - n_out:]
        g = ((ins[0][...] + ins[1][...].astype(F32)) + ins[2][...].astype(F32)) + ins[3][...].astype(F32)
        outs[0][...] = g
        if wmv is not None:
            d, m, v = _adam(ins[4][...], g, ins[5][...], ins[6][...])
            outs[1][...] = d
            outs[2][...] = m
            outs[3][...] = v

    plain = pl.BlockSpec((tr, cols), lambda r: (r, 0))
    slabbed = pl.BlockSpec((tr, cols), lambda r: (slab * nr + r, 0))
    pairs = bufB if isinstance(bufB, list) else [(bufB, j) for j in range(3)]
    in_specs = [plain] + [pl.BlockSpec((None, tr, cols), (lambda j: lambda r: (j, r, 0))(j)) for _, j in pairs]
    args = [Town] + [a for a, _ in pairs]
    if wmv is not None:
        in_specs += [slabbed] * 3
        args += list(wmv)
    aliases = None
    if prev is not None:
        in_specs += [pl.BlockSpec(memory_space=pl.ANY)] * n_out
        aliases = {len(args) + i: i for i in range(n_out)}
        args += list(prev)
    return _call(body, name=name, grid=(nr,), in_specs=in_specs, out_specs=[slabbed] * n_out,
                 out_shape=[_sds((n_slabs * rows, cols))] * n_out, aliases=aliases)(*args)


def _adamw_plain(w, g, m, v, name, jobs=()):
    rows, cols = w.shape
    tr = _tile_rows(rows, cols)

    def body(w_ref, g_ref, m_ref, v_ref, d_ref, mo_ref, vo_ref):
        d, m_, v_ = _adam(w_ref[...], g_ref[...], m_ref[...], v_ref[...])
        d_ref[...] = d
        mo_ref[...] = m_
        vo_ref[...] = v_

    spec = pl.BlockSpec((tr, cols), lambda r: (r, 0))
    return _call(body, jobs=jobs, name=name, grid=(rows // tr,), in_specs=[spec] * 4, out_specs=[spec] * 3,
                 out_shape=[_sds((rows, cols))] * 3)(w, g, m, v)


_MOD_TK = 512


def _mod_fwd(cc16, w_loc, b_loc, name):
    D, cols = w_loc.shape
    tk = min(_MOD_TK, D)
    nk = D // tk

    def body(c_ref, w_ref, b_ref, o_ref):
        kk = pl.program_id(0)

        @pl.when(kk == 0)
        def _():
            o_ref[...] = jnp.zeros_like(o_ref)

        cc = c_ref[...]
        o_ref[...] += _dot(cc * _sigmoid(cc), w_ref[...])

        @pl.when(kk == nk - 1)
        def _():
            o_ref[...] += b_ref[...]

    return _call(body, name=name, grid=(nk,),
                 in_specs=[pl.BlockSpec((16, tk), lambda kk: (0, kk)), pl.BlockSpec((tk, cols), lambda kk: (kk, 0)),
                           _full_spec((1, cols))],
                 out_specs=_full_spec((16, cols)), out_shape=_sds((16, cols)))(cc16, w_loc, b_loc)


def _mod_bwd(dm_loc, cc16, w_loc, name):
    D, cols = w_loc.shape

    def body(dm_ref, c_ref, w_ref, gw_ref, ds_ref):
        rows = [dm_ref[b, 0:1, :] for b in range(ND)]
        ctx = dm_ref[0, 1:2, :]
        for b in range(1, ND):
            ctx = ctx + dm_ref[b, 1:2, :]
        dm16 = jnp.concatenate(rows + [ctx, jnp.zeros((7, cols), F32)], axis=0)
        cc = c_ref[...]
        gw_ref[...] = _dot_tn(cc * _sigmoid(cc), dm16)
        ds_ref[...] = _dot_nt(dm16, w_ref[...])

    tk = min(_MOD_TK, D)
    return _call(body, name=name, grid=(D // tk,),
                 in_specs=[_full_spec((ND, 8, cols)), pl.BlockSpec((16, tk), lambda kk: (0, kk)),
                           pl.BlockSpec((tk, cols), lambda kk: (kk, 0))],
                 out_specs=[pl.BlockSpec((tk, cols), lambda kk: (kk, 0)), pl.BlockSpec((16, tk), lambda kk: (0, kk))],
                 out_shape=[_sds((D, cols)), _sds((16, D))])(dm_loc, cc16, w_loc)


def _bmod_grad(dm_all, name):
    n = dm_all.shape[-1]

    def body(dm_ref, o_ref):
        acc = dm_ref[0, 0:1, :] + dm_ref[0, 1:2, :]
        for b in range(1, ND):
            acc = (acc + dm_ref[b, 0:1, :]) + dm_ref[b, 1:2, :]
        o_ref[...] = acc

    return _call(body, name=name, grid=(1,), in_specs=[_full_spec((ND, 8, n))], out_specs=_full_spec((1, n)),
                 out_shape=_sds((1, n)))(dm_all)


_SMALL_ROWS = 24
_ROW_CCTX = 15


def _small_finish(parts, c_ctx, name):
    D = parts.shape[-1]

    def body(p_ref, c_ref, o_ref):
        acc = p_ref[0]
        for b in range(1, ND):
            acc = acc + p_ref[b]
        cc = c_ref[...]
        sg = _sigmoid(cc)
        dsilu = sg * (1.0 + cc * (1.0 - sg))
        row = lax.broadcasted_iota(jnp.int32, acc.shape, 0)
        o_ref[...] = jnp.where(row == _ROW_CCTX, acc * dsilu, acc)

    return _call(body, name=name, grid=(1,), in_specs=[_full_spec(parts.shape), _full_spec((1, D))],
                 out_specs=_full_spec((_SMALL_ROWS, D)), out_shape=_sds((_SMALL_ROWS, D)))(parts, c_ctx)


_WEIGHTS = ['c_ctx', 'w_mod', 'b_mod', 'norm_g', 'ffn_wg', 'ffn_wu', 'ffn_wd', 'w_in', 'w_out', 'q_norm_g', 'k_norm_g',
            'conv_w', 'conv_b', 'lru_wa', 'lru_ba', 'lru_wx', 'lru_bx', 'lru_lambda', 'final_norm_g']
_SMALL = ['c_ctx', 'b_mod', 'norm_g', 'q_norm_g', 'k_norm_g', 'conv_w', 'conv_b', 'lru_ba', 'lru_bx', 'lru_lambda',
          'final_norm_g']


def _pad_rows(a, rows):
    return jnp.pad(a, ((0, rows - a.shape[0]),) + ((0, 0),) * (a.ndim - 1))


def _step(w, m, v, x, c, ctx, loss_target):
    xi, yi, ci = _mesh_pos()
    me = 4 * xi + 2 * yi + ci
    idx = jnp.stack([ci, 2 * xi + yi]).astype(jnp.int32)
    S, D = x.shape[1:]
    Ds = D // ND
    cols = w['w_mod'].shape[-1]

    sp = jnp.concatenate([w['norm_g'][0], w['conv_w'][0], w['lru_ba'][0], w['lru_bx'][0], w['lru_lambda'][0]], axis=0)
    spg = _all_gather(_pad_rows(sp, 16), "ag_small_params", in_vmem=True)
    spf = jnp.transpose(spg, (1, 0, 2)).reshape(16, D)
    ng, conv_w, ba, bx, lam = spf[0:3], spf[3:7], spf[7:9], spf[9:11], spf[11:13]

    cg = _all_gather(_pad_rows(c, 8), "ag_cond", in_vmem=True)
    cc16 = _pad_rows(jnp.concatenate([cg[:, 0, :], w['c_ctx'][None, :]], axis=0), 16)
    b_loc = lax.dynamic_slice_in_dim(w['b_mod'], me * cols, cols, axis=1)
    mod_loc = _mod_fwd(cc16, w['w_mod'][0], b_loc, "mod_fwd")
    modg = _all_gather(mod_loc, "ag_mod", in_vmem=True)
    mod16 = jnp.transpose(modg, (1, 0, 2)).reshape(16, ND * cols)
    modx = lax.dynamic_index_in_dim(mod16, me, axis=0, keepdims=False).reshape(9, D)
    modc = mod16[8].reshape(9, D)

    shards = {'w_in': w['w_in'][0].astype(_MXU), 'w_out': w['w_out'][0].astype(_MXU)}
    for layer in range(2):
        shards[f'wg{layer}'] = w['ffn_wg'][0, layer].T.astype(_MXU)[None]
        shards[f'wu{layer}'] = w['ffn_wu'][0, layer].T.astype(_MXU)[None]
        shards[f'wd{layer}'] = w['ffn_wd'][0, layer].astype(_MXU)[None]
    WA, WX = w['lru_wa'][0].astype(_MXU), w['lru_wx'][0].astype(_MXU)
    big = ('ffn_wg', 'ffn_wu', 'ffn_wd', 'w_in', 'w_out')
    transposed = ('ffn_wg', 'ffn_wu')
    tr_view = lambda a: jnp.swapaxes(a, -1, -2)
    opt = {n: tuple((tr_view(a[n]) if n in transposed else a[n]) for a in (w, m, v)) for n in big}
    opt = {n: tuple(a.reshape(-1, a.shape[-1]) for a in t) for n, t in opt.items()}

    g = _local_step(x[0], ctx[0], loss_target[0], modx, modc, ng, shards, w['q_norm_g'], w['k_norm_g'],
                    conv_w, w['conv_b'], WA, WX, ba, bx, lam, w['final_norm_g'][None, :], idx, opt)

    grad, delta, new_m, new_v = {}, {}, {}, {}

    lfull = g['lru_w'].reshape((2,) + w['lru_wa'].shape[1:])
    for i, n in enumerate(('lru_wa', 'lru_wx')):
        shard = w[n].shape
        view = lambda a: a.reshape(-1, HD)
        grad[n] = lfull[i].reshape(shard)
        outs = _adamw_plain(view(w[n]), view(lfull[i]), view(m[n]), view(v[n]), f"adamw_{n}")
        delta[n], new_m[n], new_v[n] = [o.reshape(shard) for o in outs]

    dm = _pad_rows(jnp.stack([g['dmodx'].reshape(-1), g['dmodc'].reshape(-1)]), 8)
    dm_all = _all_gather(dm, "ag_dmod", in_vmem=True)
    dm_loc = lax.dynamic_slice_in_dim(dm_all, me * cols, cols, axis=2)
    gw_mod, dsil = _mod_bwd(dm_loc, cc16, w['w_mod'][0], "mod_bwd")
    grad['w_mod'] = gw_mod[None]
    T_wu0, own_wu0, b_wu0, fin_wu = g['wu0_pending']
    outs, ((b_wu0d,),) = _adamw_plain(w['w_mod'][0], gw_mod, m['w_mod'][0], v['w_mod'][0], "adamw_w_mod",
                                      jobs=[_rs_chips_job(T_wu0, (2,))])
    delta['w_mod'], new_m['w_mod'], new_v['w_mod'] = [o[None] for o in outs]
    g['ffn_wu'] = _rs_finish(own_wu0, [(b_wu0, 0), (b_wu0, 1), (b_wu0d, 0)], "wu0_rs_finish", opt['ffn_wu'], 0, 2, fin_wu)
    for n in big:
        if n in transposed:
            shape_t = w[n].shape[:-2] + (w[n].shape[-1], w[n].shape[-2])
            grad[n], delta[n], new_m[n], new_v[n] = [tr_view(o.reshape(shape_t)) for o in g[n]]
        else:
            grad[n], delta[n], new_m[n], new_v[n] = [o.reshape(w[n].shape) for o in g[n]]
    grad['b_mod'] = _bmod_grad(dm_all, "bmod_grad")

    pad_d = lambda a: jnp.concatenate([a, jnp.zeros((1, D - a.shape[1]), F32)], axis=1)
    small = jnp.concatenate([g['norm_g'], g['conv_w'], g['conv_b'], g['lru_ba'], g['lru_bx'], g['lru_lambda'],
                             g['final_norm_g'], dsil[8:9], pad_d(g['q_norm_g']), pad_d(g['k_norm_g'])], axis=0)
    parts = _all_gather(_pad_rows(small, _SMALL_ROWS), "ag_small_grads", in_vmem=True)
    ssum = _small_finish(parts, w['c_ctx'][None, :], "small_finish")
    mine = lambda rows: lax.dynamic_slice_in_dim(rows, me * Ds, Ds, axis=1)
    grad['norm_g'] = mine(ssum[0:3])[None]
    grad['conv_w'] = mine(ssum[3:7])[None]
    grad['conv_b'] = ssum[7:8]
    grad['lru_ba'] = mine(ssum[8:10])[None]
    grad['lru_bx'] = mine(ssum[10:12])[None]
    grad['lru_lambda'] = mine(ssum[12:14])[None]
    grad['final_norm_g'] = ssum[14]
    grad['c_ctx'] = ssum[_ROW_CCTX]
    grad['q_norm_g'] = ssum[16:17, :HD]
    grad['k_norm_g'] = ssum[17:18, :HD]

    def pack(d):
        flat = jnp.concatenate([d[n].reshape(-1) for n in _SMALL])
        padded = -(-flat.shape[0] // 1024) * 1024
        return jnp.concatenate([flat, jnp.zeros((padded - flat.shape[0],), F32)]).reshape(-1, HD)

    outs = _adamw_plain(pack(w), pack(grad), pack(m), pack(v), "adamw_small")
    off = 0
    for n in _SMALL:
        size = math.prod(w[n].shape)
        for dst, o in zip((delta, new_m, new_v), outs):
            dst[n] = o.reshape(-1)[off:off + size].reshape(w[n].shape)
        off += size

    loss = lax.psum(g['loss'][0, 0], ("x", "y", "c"))
    return (loss, g['grad_x'][None], *[grad[n] for n in _WEIGHTS], *[delta[n] for n in _WEIGHTS],
            *[new_m[n] for n in _WEIGHTS], *[new_v[n] for n in _WEIGHTS])


def kernel(x, c, ctx, c_ctx, w_mod, b_mod, norm_g, ffn_wg, ffn_wu, ffn_wd, w_in, w_out, q_norm_g, k_norm_g, conv_w, conv_b, lru_wa, lru_ba, lru_wx, lru_bx, lru_lambda, final_norm_g, loss_target, m_c_ctx, m_w_mod, m_b_mod, m_norm_g, m_ffn_wg, m_ffn_wu, m_ffn_wd, m_w_in, m_w_out, m_q_norm_g, m_k_norm_g, m_conv_w, m_conv_b, m_lru_wa, m_lru_ba, m_lru_wx, m_lru_bx, m_lru_lambda, m_final_norm_g, v_c_ctx, v_w_mod, v_b_mod, v_norm_g, v_ffn_wg, v_ffn_wu, v_ffn_wd, v_w_in, v_w_out, v_q_norm_g, v_k_norm_g, v_conv_w, v_conv_b, v_lru_wa, v_lru_ba, v_lru_wx, v_lru_bx, v_lru_lambda, v_final_norm_g):
    given = dict(locals())
    w = {n: given[n] for n in _WEIGHTS}
    m = {n: given["m_" + n] for n in _WEIGHTS}
    v = {n: given["v_" + n] for n in _WEIGHTS}
    return _step(w, m, v, x, c, ctx, loss_target)
```

```python
import functools
import math

import jax
import jax.numpy as jnp
from jax import lax
from jax.experimental import pallas as pl
from jax.experimental.pallas import tpu as pltpu

F32 = jnp.float32
_MXU = jnp.bfloat16
ND = 8
TR = 256
HD = 128
EPS = 1e-6
GRID_W = 64
ROPE_THETA = 10000.0
LRU_C = 8.0
VMEM_LIMIT = 56 * 1024 * 1024
SCAN_W = 512
ADAM_LR, ADAM_B1, ADAM_B2, ADAM_EPS, ADAM_WD, ADAM_STEP = 0.001, 0.9, 0.999, 1e-08, 0.01, 10
MESH = pl.DeviceIdType.MESH


class _Job:
    def __init__(self, inputs, out_shapes, sems, start, finish, relay=None):
        self.inputs, self.out_shapes, self.sems, self.start, self.finish = inputs, out_shapes, sems, start, finish
        self.relay = relay


def _call(body, *, name, grid, in_specs, out_specs, out_shape, scratch=(), aliases=None, jobs=()):
    params = pltpu.CompilerParams(dimension_semantics=("arbitrary",) * len(grid), vmem_limit_bytes=VMEM_LIMIT)
    if not jobs:
        return pl.pallas_call(body, name=name, grid=grid, in_specs=in_specs, out_specs=out_specs, out_shape=out_shape,
                              scratch_shapes=scratch, input_output_aliases=aliases or {}, compiler_params=params)
    single = not isinstance(out_specs, (list, tuple))
    o_specs = [out_specs] if single else list(out_specs)
    o_shape = [out_shape] if single else list(out_shape)
    n_in, n_out, n_scr = len(in_specs), len(o_specs), len(scratch)
    j_in = [a for j in jobs for a in j.inputs]
    j_out = [s for j in jobs for s in j.out_shapes]
    j_sem = [s for j in jobs for s in j.sems]
    hbm = pl.BlockSpec(memory_space=pl.ANY)

    def wrapped(*refs):
        ins, rest = refs[:n_in], refs[n_in:]
        jin, rest = rest[:len(j_in)], rest[len(j_in):]
        outs, rest = rest[:n_out], rest[n_out:]
        jout, rest = rest[:len(j_out)], rest[len(j_out):]
        scr, jsem = rest[:n_scr], rest[n_scr:]
        first = functools.reduce(jnp.logical_and, [pl.program_id(a) == 0 for a in range(len(grid))])
        last = functools.reduce(jnp.logical_and, [pl.program_id(a) == grid[a] - 1 for a in range(len(grid))])
        late = functools.reduce(jnp.logical_and, [pl.program_id(a) == (0 if a else (3 * grid[0]) // 4)
                                                  for a in range(len(grid))])

        def each(which):
            i = o = s = 0
            for j in jobs:
                ni, no, ns = len(j.inputs), len(j.out_shapes), len(j.sems)
                if getattr(j, which, None) is not None:
                    getattr(j, which)(jin[i:i + ni], jout[o:o + no], jsem[s:s + ns])
                i, o, s = i + ni, o + no, s + ns

        @pl.when(first)
        def _():
            each("start")

        relay_early = grid[0] >= 4
        if relay_early:
            @pl.when(late)
            def _():
                each("relay")

        body(*ins, *outs, *scr)

        @pl.when(last)
        def _():
            if not relay_early:
                each("relay")
            each("finish")

    call = pl.pallas_call(wrapped, name=name, grid=grid, in_specs=list(in_specs) + [hbm] * len(j_in),
                          out_specs=o_specs + [hbm] * len(j_out), out_shape=o_shape + j_out,
                          scratch_shapes=list(scratch) + j_sem, input_output_aliases=aliases or {}, compiler_params=params)

    def run(*args):
        res = call(*args, *j_in)
        comp = res[0] if single else list(res[:n_out])
        jres, o = [], n_out
        for j in jobs:
            jres.append(list(res[o:o + len(j.out_shapes)]))
            o += len(j.out_shapes)
        return comp, jres

    return run


def _comm_call(jobs, name):
    j_in = [a for j in jobs for a in j.inputs]
    j_out = [s for j in jobs for s in j.out_shapes]
    j_sem = [s for j in jobs for s in j.sems]
    hbm = pl.BlockSpec(memory_space=pl.ANY)

    def body(*refs):
        jin, jout, jsem = refs[:len(j_in)], refs[len(j_in):len(j_in) + len(j_out)], refs[len(j_in) + len(j_out):]
        for which in ("start", "relay", "finish"):
            i = o = s = 0
            for j in jobs:
                ni, no, ns = len(j.inputs), len(j.out_shapes), len(j.sems)
                if getattr(j, which, None) is not None:
                    getattr(j, which)(jin[i:i + ni], jout[o:o + no], jsem[s:s + ns])
                i, o, s = i + ni, o + no, s + ns

    res = pl.pallas_call(body, name=name, out_shape=j_out, in_specs=[hbm] * len(j_in), out_specs=[hbm] * len(j_out),
                         scratch_shapes=j_sem)(*j_in)
    jres, o = [], 0
    for j in jobs:
        jres.append(list(res[o:o + len(j.out_shapes)]))
        o += len(j.out_shapes)
    return jres


def _sds(shape, dtype=F32):
    return jax.ShapeDtypeStruct(tuple(shape), dtype)


def _dot(a, b):
    return jnp.dot(a.astype(_MXU), b.astype(_MXU), preferred_element_type=F32)


def _dot_nt(a, b):
    return lax.dot_general(a.astype(_MXU), b.astype(_MXU), (((1,), (1,)), ((), ())), preferred_element_type=F32)


def _dot_tn(a, b):
    return lax.dot_general(a.astype(_MXU), b.astype(_MXU), (((0,), (0,)), ((), ())), preferred_element_type=F32)


def _sigmoid(x):
    return 0.5 * jnp.tanh(0.5 * x) + 0.5


_GELU_C = math.sqrt(2.0 / math.pi)


def _gelu_and_grad(x):
    x2 = x * x
    t = jnp.tanh(_GELU_C * (x + 0.044715 * x * x2))
    ge = 0.5 * x * (1.0 + t)
    dge = 0.5 * (1.0 + t) + 0.5 * x * (1.0 - t * t) * (_GELU_C * (1.0 + 3.0 * 0.044715 * x2))
    return ge, dge


def _tm(rows):
    assert rows % 4 == 0 and (rows // 4) % 16 == 0
    return rows // 4


def _row_spec(width, nmax=None):
    if nmax is None:
        return pl.BlockSpec((TR, width), lambda i: (i, 0))
    return pl.BlockSpec((TR, width), lambda i: (jnp.minimum(i, nmax), 0))


def _full_spec(shape):
    n = len(shape)
    return pl.BlockSpec(tuple(shape), lambda *_: (0,) * n)


def _mod_spec(D, nx):
    return pl.BlockSpec((None, 9, D), lambda i: (i // nx, 0, 0))


def _norm_mod_fwd(X, ng, mod2, k, rows, nx, name):
    D = X.shape[1]

    def body(x_ref, g_ref, mod_ref, h_ref):
        x = x_ref[...]
        r = lax.rsqrt(jnp.mean(x * x, axis=-1, keepdims=True) + EPS)
        n = (x * r) * g_ref[...]
        h_ref[...] = (n * (1.0 + mod_ref[3 * k + 1:3 * k + 2, :]) + mod_ref[3 * k:3 * k + 1, :]).astype(h_ref.dtype)

    return _call(body, name=name, grid=(rows // TR,),
                 in_specs=[_row_spec(D), _full_spec((1, D)), _mod_spec(D, nx)],
                 out_specs=_row_spec(D), out_shape=_sds((rows, D), _MXU))(X, ng, mod2)


def _norm_mod_bwd(X, dH, dXres, ng, mod2, k, rows, nx, res_tiles, name, jobs=(), branch=None):
    D = X.shape[1]
    ngroups = -(-(rows // TR) // nx)
    n_in = 5 + (1 if branch is not None else 0)

    def body(*refs):
        x_ref, dh_ref, dres_ref, g_ref, mod_ref = refs[:5]
        dx_ref, dsh_ref, dsc_ref, dng_ref = refs[n_in:n_in + 4]
        i = pl.program_id(0)
        x = x_ref[...]
        dh = dh_ref[...]
        g = g_ref[...]
        r = lax.rsqrt(jnp.mean(x * x, axis=-1, keepdims=True) + EPS)
        xh = x * r
        n = xh * g
        dn_mod = dh * (1.0 + mod_ref[3 * k + 1:3 * k + 2, :])

        @pl.when(i % nx == 0)
        def _():
            dsh_ref[...] = jnp.zeros_like(dsh_ref)
            dsc_ref[...] = jnp.zeros_like(dsc_ref)
            if branch is not None:
                refs[n_in + 5][...] = jnp.zeros_like(refs[n_in + 5])

        @pl.when(i == 0)
        def _():
            dng_ref[...] = jnp.zeros_like(dng_ref)

        dsh_ref[...] += jnp.sum(dh, axis=0, keepdims=True)
        dsc_ref[...] += jnp.sum(dh * n, axis=0, keepdims=True)
        dng_ref[...] += jnp.sum(dn_mod * xh, axis=0, keepdims=True)
        dn = dn_mod * g
        dres = jnp.where(i < res_tiles, dres_ref[...], 0.0)
        dx = r * (dn - xh * jnp.mean(dn * xh, axis=-1, keepdims=True)) + dres
        dx_ref[...] = dx
        if branch is not None:
            _, k2, coef = branch
            refs[n_in + 4][...] = ((coef * mod_ref[3 * k2 + 2:3 * k2 + 3, :]) * dx).astype(refs[n_in + 4].dtype)
            refs[n_in + 5][...] += jnp.sum(coef * dx * refs[5][...], axis=0, keepdims=True)

    grp = pl.BlockSpec((None, 1, D), lambda i: (i // nx, 0, 0))
    in_specs = [_row_spec(D), _row_spec(D), _row_spec(D, res_tiles - 1), _full_spec((1, D)), _mod_spec(D, nx)]
    out_specs = [_row_spec(D), grp, grp, _full_spec((1, D))]
    out_shape = [_sds((rows, D)), _sds((ngroups, 1, D)), _sds((ngroups, 1, D)), _sds((1, D))]
    args = [X, dH, dXres, ng, mod2]
    if branch is not None:
        in_specs.append(_row_spec(D))
        args.append(branch[0])
        out_specs += [_row_spec(D), grp]
        out_shape += [_sds((rows, D), _MXU), _sds((ngroups, 1, D))]
    return _call(body, jobs=jobs, name=name, grid=(rows // TR,), in_specs=in_specs, out_specs=out_specs,
                 out_shape=out_shape)(*args)


def _ffn_gate(H, WG, rows, name, jobs=()):
    D = H.shape[1]
    Fb = WG.shape[-2]

    def body(h_ref, w_ref, g_ref):
        g_ref[...] = _dot_nt(h_ref[...], w_ref[...])

    tm = _tm(rows)
    return _call(body, jobs=jobs, name=name, grid=(ND, rows // tm),
                 in_specs=[pl.BlockSpec((tm, D), lambda d, m: (m, 0)), pl.BlockSpec((None, None, Fb, D), lambda d, m: (d, 0, 0, 0))],
                 out_specs=pl.BlockSpec((None, tm, Fb), lambda d, m: (d, m, 0)), out_shape=_sds((ND, rows, Fb)))(H, WG)


def _ffn_up(H, WG, WU, rows, name, jobs=(), G=None):
    D = H.shape[1]
    Fb = WU.shape[-2]

    def body(h_ref, w_ref, x_ref, *outs):
        h = h_ref[...]
        g = x_ref[...] if G is not None else _dot_nt(h, x_ref[...])
        u = _dot_nt(h, w_ref[...])
        if G is None:
            outs[0][...] = g
        outs[-2][...] = u
        outs[-1][...] = ((g * _sigmoid(g)) * u).astype(outs[-1].dtype)

    tm = _tm(rows)
    blk = pl.BlockSpec((None, tm, Fb), lambda d, m: (d, m, 0))
    wspec = pl.BlockSpec((None, None, Fb, D), lambda d, m: (d, 0, 0, 0))
    f32o, bfo = _sds((ND, rows, Fb)), _sds((ND, rows, Fb), _MXU)
    return _call(body, jobs=jobs, name=name, grid=(ND, rows // tm),
                 in_specs=[pl.BlockSpec((tm, D), lambda d, m: (m, 0)), wspec, blk if G is not None else wspec],
                 out_specs=[blk, blk] if G is not None else [blk, blk, blk],
                 out_shape=[f32o, bfo] if G is not None else [f32o, f32o, bfo])(H, WU, G if G is not None else WG)


def _ffn_down(A, WD, layer, X, mod2, k, rows, S, name, jobs=(), next_norm=None):
    Fb, D = WD.shape[-2:]
    tm = _tm(rows) // 2

    def body(a_ref, w_ref, x_ref, mod_ref, *rest):
        if next_norm is not None:
            g_ref, y_ref, xn_ref, h_ref, acc_ref = rest
        else:
            y_ref, xn_ref, acc_ref = rest
        d = pl.program_id(1)

        @pl.when(d == 0)
        def _():
            acc_ref[...] = jnp.zeros_like(acc_ref)

        acc_ref[...] += _dot(a_ref[...], w_ref[...])

        @pl.when(d == ND - 1)
        def _():
            y = acc_ref[...]
            y_ref[...] = y
            is_ctx = (pl.program_id(0) * tm + lax.broadcasted_iota(jnp.int32, (tm, 1), 0)) >= S

            def mod_row(j):
                return jnp.where(is_ctx, mod_ref[1, j:j + 1, :], mod_ref[0, j:j + 1, :])

            xn = x_ref[...] + (0.5 * mod_row(3 * k + 2)) * y
            xn_ref[...] = xn
            if next_norm is not None:
                k2 = next_norm[1]
                n = (xn * lax.rsqrt(jnp.mean(xn * xn, axis=-1, keepdims=True) + EPS)) * g_ref[...]
                h_ref[...] = (n * (1.0 + mod_row(3 * k2 + 1)) + mod_row(3 * k2)).astype(h_ref.dtype)

    row = pl.BlockSpec((tm, D), lambda m, d: (m, 0))
    in_specs = [pl.BlockSpec((None, tm, Fb), lambda m, d: (d, m, 0)),
                pl.BlockSpec((None, None, Fb, D), lambda m, d: (d, layer, 0, 0)),
                row, pl.BlockSpec((2, 9, D), lambda m, d: (0, 0, 0))]
    args = [A, WD, X, mod2]
    out_specs, out_shape = [row, row], [_sds((rows, D)), _sds((rows, D))]
    if next_norm is not None:
        in_specs.append(pl.BlockSpec((1, D), lambda m, d: (0, 0)))
        args.append(next_norm[0])
        out_specs.append(row)
        out_shape.append(_sds((rows, D), _MXU))
    return _call(body, jobs=jobs, name=name, grid=(rows // tm, ND), in_specs=in_specs, out_specs=out_specs,
                 out_shape=out_shape, scratch=[pltpu.VMEM((tm, D), F32)])(*args)


def _ffn_down_loss(A, WD, X, mod2, k, fg, target, name):
    Fb, D = WD.shape[-2:]
    S = X.shape[0]
    tm = _tm(S) // 2

    def body(a_ref, w_ref, x_ref, mod_ref, g_ref, t_ref, loss_ref, dx_ref, dy_ref, dgate_ref, dg_ref, acc_ref):
        m, d = pl.program_id(0), pl.program_id(1)

        @pl.when(d == 0)
        def _():
            acc_ref[...] = jnp.zeros_like(acc_ref)

        @pl.when(jnp.logical_and(m == 0, d == 0))
        def _():
            loss_ref[...] = jnp.zeros_like(loss_ref)
            dgate_ref[...] = jnp.zeros_like(dgate_ref)
            dg_ref[...] = jnp.zeros_like(dg_ref)

        acc_ref[...] += _dot(a_ref[...], w_ref[...])

        @pl.when(d == ND - 1)
        def _():
            y = acc_ref[...]
            gate = 0.5 * mod_ref[3 * k + 2:3 * k + 3, :]
            x = x_ref[...] + gate * y
            g = g_ref[...]
            r = lax.rsqrt(jnp.mean(x * x, axis=-1, keepdims=True) + EPS)
            n = x * r
            err = n * g - t_ref[...]
            loss_ref[...] += 0.5 * jnp.sum(jnp.mean(err * err, axis=-1, keepdims=True), axis=0, keepdims=True)
            dy = err * (1.0 / D)
            dg_ref[...] += jnp.sum(dy * n, axis=0, keepdims=True)
            dn = dy * g
            dx = r * (dn - n * jnp.mean(dn * n, axis=-1, keepdims=True))
            dx_ref[...] = dx
            dy_ref[...] = (gate * dx).astype(dy_ref.dtype)
            dgate_ref[...] += jnp.sum(0.5 * dx * y, axis=0, keepdims=True)

    row = pl.BlockSpec((tm, D), lambda m, d: (m, 0))
    vec = pl.BlockSpec((1, D), lambda m, d: (0, 0))
    return _call(body, name=name, grid=(S // tm, ND),
                 in_specs=[pl.BlockSpec((None, tm, Fb), lambda m, d: (d, m, 0)),
                           pl.BlockSpec((None, None, Fb, D), lambda m, d: (d, 0, 0, 0)),
                           row, pl.BlockSpec((None, 9, D), lambda m, d: (0, 0, 0)), vec, row],
                 out_specs=[pl.BlockSpec((1, 1), lambda m, d: (0, 0)), row, row, vec, vec],
                 out_shape=[_sds((1, 1)), _sds((S, D)), _sds((S, D), _MXU), _sds((1, D)), _sds((1, D))],
                 scratch=[pltpu.VMEM((tm, D), F32)])(A, WD, X, mod2, fg, target)


def _ffn_dact(dYb, WD, layer, G, U, rows, name, jobs=()):
    Fb, D = WD.shape[-2:]

    def body(dy_ref, w_ref, g_ref, u_ref, dg_ref, du_ref):
        da = _dot_nt(dy_ref[...], w_ref[...])
        g = g_ref[...]
        sg = _sigmoid(g)
        dg_ref[...] = (da * u_ref[...] * (sg * (1.0 + g * (1.0 - sg)))).astype(dg_ref.dtype)
        du_ref[...] = (da * (g * sg)).astype(du_ref.dtype)

    tm = _tm(rows)
    blk = pl.BlockSpec((None, tm, Fb), lambda m, d: (d, m, 0))
    return _call(body, jobs=jobs, name=name, grid=(rows // tm, ND),
                 in_specs=[pl.BlockSpec((tm, D), lambda m, d: (m, 0)),
                           pl.BlockSpec((None, None, Fb, D), lambda m, d: (d, layer, 0, 0)), blk, blk],
                 out_specs=[blk, blk],
                 out_shape=[_sds((ND, rows, Fb), _MXU), _sds((ND, rows, Fb), _MXU)])(dYb, WD, G, U)


def _ffn_dh(dG, dU, WG, WU, rows, name, jobs=()):
    Fb, D = WG.shape[-2:]

    def body(dg_ref, du_ref, wg_ref, wu_ref, dh_ref, acc_ref):
        d = pl.program_id(1)

        @pl.when(d == 0)
        def _():
            acc_ref[...] = jnp.zeros_like(acc_ref)

        acc_ref[...] += _dot(dg_ref[...], wg_ref[...]) + _dot(du_ref[...], wu_ref[...])

        @pl.when(d == ND - 1)
        def _():
            dh_ref[...] = acc_ref[...]

    tm = _tm(rows)
    blk = pl.BlockSpec((None, tm, Fb), lambda m, d: (d, m, 0))
    wspec = pl.BlockSpec((None, None, Fb, D), lambda m, d: (d, 0, 0, 0))
    return _call(body, jobs=jobs, name=name, grid=(rows // tm, ND), in_specs=[blk, blk, wspec, wspec],
                 out_specs=pl.BlockSpec((tm, D), lambda m, d: (m, 0)), out_shape=_sds((rows, D)),
                 scratch=[pltpu.VMEM((tm, D), F32)])(dG, dU, WG, WU)


def _mm_tn(A, a_spec, B, b_spec, out_shape, out_spec, rows, name, prev=None, jobs=()):
    def body(*refs):
        a_ref, b_ref, o_ref = refs[0], refs[1], refs[-1]

        @pl.when(pl.program_id(1) == 0)
        def _():
            o_ref[...] = jnp.zeros_like(o_ref)

        o_ref[...] += _dot_tn(a_ref[...], b_ref[...])

    in_specs = [a_spec, b_spec]
    args = [A, B]
    aliases = None
    if prev is not None:
        in_specs.append(pl.BlockSpec(memory_space=pl.ANY))
        args.append(prev)
        aliases = {2: 0}
    return _call(body, jobs=jobs, name=name, grid=(ND, rows // _tm(rows)), in_specs=in_specs, out_specs=out_spec,
                 out_shape=_sds(out_shape), aliases=aliases)(*args)


def _proj_in(H2, WIN, name, jobs=()):
    R, D = H2.shape
    Nb = WIN.shape[-1]

    def body(h_ref, w_ref, p_ref):
        p_ref[...] = _dot(h_ref[...], w_ref[...])

    tm = _tm(R)
    return _call(body, jobs=jobs, name=name, grid=(ND, R // tm),
                 in_specs=[pl.BlockSpec((tm, D), lambda d, m: (m, 0)), pl.BlockSpec((None, D, Nb), lambda d, m: (d, 0, 0))],
                 out_specs=pl.BlockSpec((tm, Nb), lambda d, m: (m, d)), out_shape=_sds((R, ND * Nb)))(H2, WIN)


def _dproj_in(dP, WIN, name, jobs=()):
    R = dP.shape[0]
    D, Nb = WIN.shape[-2:]

    def body(dp_ref, w_ref, dh_ref, acc_ref):
        d = pl.program_id(1)

        @pl.when(d == 0)
        def _():
            acc_ref[...] = jnp.zeros_like(acc_ref)

        acc_ref[...] += _dot_nt(dp_ref[...], w_ref[...])

        @pl.when(d == ND - 1)
        def _():
            dh_ref[...] = acc_ref[...]

    tm = _tm(R)
    return _call(body, jobs=jobs, name=name, grid=(R // tm, ND),
                 in_specs=[pl.BlockSpec((tm, Nb), lambda m, d: (m, d)), pl.BlockSpec((None, D, Nb), lambda m, d: (d, 0, 0))],
                 out_specs=pl.BlockSpec((tm, D), lambda m, d: (m, 0)), out_shape=_sds((R, D)),
                 scratch=[pltpu.VMEM((tm, D), F32)])(dP, WIN)


def _proj_out(mixb, WOUT, X1, mod2, ng, S, name):
    D = WOUT.shape[0]

    def body(m_ref, w_ref, x_ref, mod_ref, g_ref, z_ref, xn_ref, h_ref):
        z = _dot(m_ref[...], w_ref[...])
        z_ref[...] = z
        xn = x_ref[...] + mod_ref[5:6, :] * z
        xn_ref[...] = xn
        n = (xn * lax.rsqrt(jnp.mean(xn * xn, axis=-1, keepdims=True) + EPS)) * g_ref[...]
        h_ref[...] = (n * (1.0 + mod_ref[7:8, :]) + mod_ref[6:7, :]).astype(h_ref.dtype)

    return _call(body, name=name, grid=(S // TR,),
                 in_specs=[_row_spec(D), _full_spec((D, D)), _row_spec(D), pl.BlockSpec((None, 9, D), lambda i: (0, 0, 0)),
                           _full_spec((1, D))],
                 out_specs=[_row_spec(D), _row_spec(D), _row_spec(D)],
                 out_shape=[_sds((S, D)), _sds((S, D)), _sds((S, D), _MXU)])(mixb, WOUT, X1, mod2, ng)


def _dproj_out(dZb, WOUT, name):
    S, D = dZb.shape

    def body(dz_ref, w_ref, dm_ref):
        dm_ref[...] = _dot_nt(dz_ref[...], w_ref[...])

    return _call(body, name=name, grid=(S // TR,), in_specs=[_row_spec(D), _full_spec((D, D))],
                 out_specs=_row_spec(D), out_shape=_sds((S, D)))(dZb, WOUT)


def _pair_swap(t):
    lane = lax.broadcasted_iota(jnp.int32, t.shape, 1)
    return jnp.where(lane % 2 == 0, pltpu.roll(t, HD - 1, 1), pltpu.roll(t, 1, 1))


SM_SCALE = HD ** -0.5


def _qkv_prep(P, qg, kg, COS, SIN, D, KVW, name):
    R = P.shape[0]
    W = D + 2 * KVW
    nq, nk = D // HD, KVW // HD

    def body(p_ref, qg_ref, kg_ref, cos_ref, sin_ref, q_ref, k_ref, v_ref):
        cos, sin = cos_ref[...], sin_ref[...]

        def head(t, g):
            y = (t * lax.rsqrt(jnp.mean(t * t, axis=-1, keepdims=True) + EPS)) * g
            return y * cos + _pair_swap(y) * sin

        for h in range(nq):
            q_ref[:, h * HD:(h + 1) * HD] = (head(p_ref[:, h * HD:(h + 1) * HD], qg_ref[...]) * SM_SCALE).astype(q_ref.dtype)
        for h in range(nk):
            k_ref[:, h * HD:(h + 1) * HD] = head(p_ref[:, D + h * HD:D + (h + 1) * HD], kg_ref[...]).astype(k_ref.dtype)
        v_ref[...] = p_ref[:, D + KVW:W].astype(v_ref.dtype)

    return _call(body, name=name, grid=(R // TR,),
                 in_specs=[_row_spec(W), _full_spec((1, HD)), _full_spec((1, HD)), _row_spec(HD), _row_spec(HD)],
                 out_specs=[_row_spec(D), _row_spec(KVW), _row_spec(KVW)],
                 out_shape=[_sds((R, D), _MXU), _sds((R, KVW), _MXU), _sds((R, KVW), _MXU)])(P, qg, kg, COS, SIN)


def _qkv_bwd(P, dq, dk, dv, qg, kg, COS, SIN, dP, D, KVW, nx, name):
    R, INW = P.shape
    W = D + 2 * KVW
    nq, nk = D // HD, KVW // HD

    def body(p_ref, dq_ref, dk_ref, dv_ref, qg_ref, kg_ref, cos_ref, sin_ref, dp_in, dp_ref, dqg_ref, dkg_ref):
        i = pl.program_id(0)
        cos, sin = cos_ref[...], sin_ref[...]

        @pl.when(i == 0)
        def _():
            dqg_ref[...] = jnp.zeros_like(dqg_ref)
            dkg_ref[...] = jnp.zeros_like(dkg_ref)

        def head_bwd(t, g, dout):
            r = lax.rsqrt(jnp.mean(t * t, axis=-1, keepdims=True) + EPS)
            n = t * r
            dy = dout * cos + _pair_swap(dout * sin)
            dn = dy * g
            return r * (dn - n * jnp.mean(dn * n, axis=-1, keepdims=True)), jnp.sum(dy * n, axis=0, keepdims=True)

        dqg = jnp.zeros((1, HD), F32)
        for h in range(nq):
            sl = slice(h * HD, (h + 1) * HD)
            dt, dg = head_bwd(p_ref[:, sl], qg_ref[...], jnp.where(i < nx, dq_ref[:, sl] * SM_SCALE, 0.0))
            dp_ref[:, sl] = dt.astype(dp_ref.dtype)
            dqg += dg
        dkg = jnp.zeros((1, HD), F32)
        for h in range(nk):
            sl = slice(h * HD, (h + 1) * HD)
            dt, dg = head_bwd(p_ref[:, D + h * HD:D + (h + 1) * HD], kg_ref[...], dk_ref[:, sl])
            dp_ref[:, D + h * HD:D + (h + 1) * HD] = dt.astype(dp_ref.dtype)
            dkg += dg
        dqg_ref[...] += dqg
        dkg_ref[...] += dkg
        dp_ref[:, D + KVW:W] = dv_ref[...].astype(dp_ref.dtype)

    return _call(body, name=name, grid=(R // TR,),
                 in_specs=[_row_spec(W), _row_spec(D, nx - 1), _row_spec(KVW), _row_spec(KVW), _full_spec((1, HD)),
                           _full_spec((1, HD)), _row_spec(HD), _row_spec(HD), pl.BlockSpec(memory_space=pl.ANY)],
                 out_specs=[_row_spec(W), _full_spec((1, HD)), _full_spec((1, HD))],
                 out_shape=[_sds((R, INW), _MXU), _sds((1, HD)), _sds((1, HD))],
                 aliases={8: 0})(P, dq, dk, dv, qg, kg, COS, SIN, dP)


def _stack_heads(ref, G, dtype=None):
    parts = [ref[:, g * HD:(g + 1) * HD] for g in range(G)]
    out = jnp.concatenate(parts, axis=0)
    return out if dtype is None else out.astype(dtype)


_KEY_CHUNKS = 4


def _key_chunks(R):
    unit = 256 if R % 256 == 0 else 16
    nt = R // unit
    n = min(_KEY_CHUNKS, nt)
    bounds = [0]
    for i in range(n):
        bounds.append(bounds[-1] + (nt // n + (1 if i < nt % n else 0)) * unit)
    return bounds


def _attn_fwd(q, k, v, S, G, name, jobs=()):
    R, KVW = k.shape
    D = q.shape[1]
    Kh = KVW // HD
    tq = 128
    kb = _key_chunks(R)

    def body(q_ref, k_ref, v_ref, o_ref, lse_ref):
        qs = _stack_heads(q_ref, G)
        m = l = acc = None
        for c in range(len(kb) - 1):
            s = _dot_nt(qs, k_ref[kb[c]:kb[c + 1], :])
            mc = jnp.max(s, axis=-1, keepdims=True)
            m_new = mc if c == 0 else jnp.maximum(m, mc)
            p = jnp.exp(s - m_new)
            ps = jnp.sum(p, axis=-1, keepdims=True)
            pv = _dot(p, v_ref[kb[c]:kb[c + 1], :])
            if c == 0:
                l, acc = ps, pv
            else:
                alpha = jnp.exp(m - m_new)
                l = alpha * l + ps
                acc = alpha * acc + pv
            m = m_new
        o = acc / l
        lse = m + jnp.log(l)
        for g in range(G):
            o_ref[:, g * HD:(g + 1) * HD] = o[g * tq:(g + 1) * tq, :]
            lse_ref[g] = jnp.broadcast_to(lse[g * tq:(g + 1) * tq, :], (tq, HD))

    return _call(body, jobs=jobs, name=name, grid=(Kh, S // tq),
                 in_specs=[pl.BlockSpec((tq, G * HD), lambda h, i: (i, h)), pl.BlockSpec((R, HD), lambda h, i: (0, h)),
                           pl.BlockSpec((R, HD), lambda h, i: (0, h))],
                 out_specs=[pl.BlockSpec((tq, G * HD), lambda h, i: (i, h)),
                            pl.BlockSpec((None, G, tq, HD), lambda h, i: (h, 0, i, 0))],
                 out_shape=[_sds((S, D)), _sds((Kh, G, S, HD))])(q, k, v)


def _attn_bwd(q, k, v, O, LSE, dOb, S, G, name, jobs=()):
    R, KVW = k.shape
    D = q.shape[1]
    Kh = KVW // HD
    tq = 128
    kb = _key_chunks(R)

    def body(q_ref, k_ref, v_ref, o_ref, lse_ref, do_ref, dq_ref, dk_ref, dv_ref):
        @pl.when(pl.program_id(1) == 0)
        def _():
            dk_ref[...] = jnp.zeros_like(dk_ref)
            dv_ref[...] = jnp.zeros_like(dv_ref)

        qs = _stack_heads(q_ref, G)
        do = _stack_heads(do_ref, G)
        o = _stack_heads(o_ref, G)
        lse = jnp.concatenate([lse_ref[g][:, 0:1] for g in range(G)], axis=0)
        delta = jnp.sum(do.astype(F32) * o, axis=-1, keepdims=True)
        dq = None
        for c in range(len(kb) - 1):
            rows = slice(kb[c], kb[c + 1])
            kk = k_ref[rows, :]
            p = jnp.exp(_dot_nt(qs, kk) - lse)
            dp = _dot_nt(do, v_ref[rows, :])
            ds = (p * (dp - delta)).astype(_MXU)
            dqc = _dot(ds, kk)
            dq = dqc if c == 0 else dq + dqc
            dk_ref[rows, :] += _dot_tn(ds, qs)
            dv_ref[rows, :] += _dot_tn(p, do)
        for g in range(G):
            dq_ref[:, g * HD:(g + 1) * HD] = dq[g * tq:(g + 1) * tq, :]

    qspec = pl.BlockSpec((tq, G * HD), lambda h, i: (i, h))
    kspec = pl.BlockSpec((R, HD), lambda h, i: (0, h))
    return _call(body, jobs=jobs, name=name, grid=(Kh, S // tq),
                 in_specs=[qspec, kspec, kspec, qspec, pl.BlockSpec((None, G, tq, HD), lambda h, i: (h, 0, i, 0)), qspec],
                 out_specs=[qspec, kspec, kspec],
                 out_shape=[_sds((S, D)), _sds((R, KVW)), _sds((R, KVW))])(q, k, v, O, LSE, dOb)


def _halo_specs(R, CB, col0):
    nt8 = TR // 8
    return [pl.BlockSpec((8, CB), lambda h, i: (jnp.maximum(i * nt8 - 1, 0), col0 + h)),
            pl.BlockSpec((TR, CB), lambda h, i: (i, col0 + h)),
            pl.BlockSpec((8, CB), lambda h, i: (jnp.minimum((i + 1) * nt8, R // 8 - 1), col0 + h))]


def _seq_pos(i, S, R, CB):
    t = i * TR - 8 + lax.broadcasted_iota(jnp.int32, (TR + 16, CB), 0)
    start = jnp.where(t >= S, S, 0)
    end = jnp.where(t >= S, R, S)
    return t - start, end - t


def _shift(cat, by):
    return pltpu.roll(cat, by % cat.shape[0], 0)


def _gate_mats(xcb, w_ref, dirn, nb):
    return jnp.concatenate([_dot(xcb[:, b * HD:(b + 1) * HD], w_ref[dirn, b]) for b in range(nb)], axis=1)


def _lru_gates_fwd(P, conv_w, conv_b, WA, WX, ba, bx, lam, S, D, col0, name):
    R = P.shape[0]
    CB = D // 2
    nb = CB // HD

    def body(xp_ref, x_ref, xn_ref, cw_ref, cb_ref, wa_ref, wx_ref, ba_ref, bx_ref, lam_ref,
             xc_ref, af_ref, uf_ref, ab_ref, ub_ref):
        i = pl.program_id(1)
        cat = jnp.concatenate([xp_ref[...], x_ref[...], xn_ref[...]], axis=0)
        from_start, to_end = _seq_pos(i, S, R, CB)
        conv = (cb_ref[...] + cw_ref[2:3, :] * cat
                + cw_ref[0:1, :] * jnp.where(from_start >= 2, _shift(cat, 2), 0.0)
                + cw_ref[1:2, :] * jnp.where(from_start >= 1, _shift(cat, 1), 0.0)
                + cw_ref[3:4, :] * jnp.where(to_end >= 2, _shift(cat, -1), 0.0))
        xc = conv[8:8 + TR, :]
        xc_ref[...] = xc
        xcb = xc.astype(_MXU)
        for dirn, (a_ref, u_ref) in enumerate(((af_ref, uf_ref), (ab_ref, ub_ref))):
            ra = _sigmoid(_gate_mats(xcb, wa_ref, dirn, nb) + ba_ref[dirn:dirn + 1, :])
            ia = _sigmoid(_gate_mats(xcb, wx_ref, dirn, nb) + bx_ref[dirn:dirn + 1, :])
            nl = -lam_ref[dirn:dirn + 1, :]
            sp = jnp.maximum(nl, 0.0) + jnp.log(1.0 + jnp.exp(-jnp.abs(nl)))
            la = (-LRU_C) * ra * sp
            a = jnp.exp(la)
            a_ref[...] = a
            u_ref[...] = jnp.sqrt(1.0 - a * a) * (ia * xc)

    def par(r):
        return pl.BlockSpec((r, CB), lambda h, i: (0, h))

    wspec = pl.BlockSpec((2, nb, HD, HD), lambda h, i: (0, h, 0, 0))
    out = pl.BlockSpec((TR, CB), lambda h, i: (i, h))
    return _call(body, name=name, grid=(2, R // TR),
                 in_specs=_halo_specs(R, CB, col0) + [par(4), par(1), wspec, wspec, par(2), par(2), par(2)],
                 out_specs=[out] * 5, out_shape=[_sds((R, D))] * 5,
                 )(P, P, P, conv_w, conv_b, WA, WX, ba, bx, lam)


def _lru_gates_bwd(xc, hf, hb, gf, gb, WA, WX, ba, bx, lam, S, name):
    R, D = xc.shape
    CB = D // 2
    nb = CB // HD

    def body(xc_ref, hfq_ref, hf_ref, hfl_ref, hb_ref, hbn_ref, gf_ref, gb_ref, wa_ref, wx_ref, ba_ref, bx_ref, lam_ref,
             dxc_ref, dwa_ref, dwx_ref, dba_ref, dbx_ref, dlam_ref):
        @pl.when(pl.program_id(1) == 0)
        def _():
            for r in (dwa_ref, dwx_ref, dba_ref, dbx_ref, dlam_ref):
                r[...] = jnp.zeros_like(r)

        t = pl.program_id(1) * TR + lax.broadcasted_iota(jnp.int32, (TR, CB), 0)
        hfp = _shift(jnp.concatenate([hfq_ref[...], hf_ref[...]], axis=0), 1)[8:8 + TR, :]
        hfp = jnp.where(t == 0, hfl_ref[7:8, :], jnp.where(t == S, 0.0, hfp))
        hbp = _shift(jnp.concatenate([hb_ref[...], hbn_ref[...]], axis=0), -1)[0:TR, :]
        hbp = jnp.where(t == R - 1, 0.0, hbp)

        xc = xc_ref[...]
        xcb = xc.astype(_MXU)
        dxc = jnp.zeros_like(xc)
        for dirn, (hp, g_ref) in enumerate(((hfp, gf_ref), (hbp, gb_ref))):
            ra = _sigmoid(_gate_mats(xcb, wa_ref, dirn, nb) + ba_ref[dirn:dirn + 1, :])
            ia = _sigmoid(_gate_mats(xcb, wx_ref, dirn, nb) + bx_ref[dirn:dirn + 1, :])
            nl = -lam_ref[dirn:dirn + 1, :]
            sp = jnp.maximum(nl, 0.0) + jnp.log(1.0 + jnp.exp(-jnp.abs(nl)))
            la = (-LRU_C) * ra * sp
            a = jnp.exp(la)
            e2 = a * a
            rs = lax.rsqrt(1.0 - e2)
            s = (1.0 - e2) * rs
            du = g_ref[...]
            dla = du * hp * a - du * (ia * xc) * (e2 * rs)
            dxc += du * s * ia
            dza = (dla * (-LRU_C) * sp) * ra * (1.0 - ra)
            dzx = (du * s * xc) * ia * (1.0 - ia)
            dlam_ref[dirn:dirn + 1, :] += jnp.sum(dla * (LRU_C * ra) * _sigmoid(nl), axis=0, keepdims=True)
            dba_ref[dirn:dirn + 1, :] += jnp.sum(dza, axis=0, keepdims=True)
            dbx_ref[dirn:dirn + 1, :] += jnp.sum(dzx, axis=0, keepdims=True)
            dzab, dzxb = dza.astype(_MXU), dzx.astype(_MXU)
            parts = []
            for b in range(nb):
                sl = slice(b * HD, (b + 1) * HD)
                dwa_ref[dirn, b] += _dot_tn(xcb[:, sl], dzab[:, sl])
                dwx_ref[dirn, b] += _dot_tn(xcb[:, sl], dzxb[:, sl])
                parts.append(_dot_nt(dzab[:, sl], wa_ref[dirn, b]) + _dot_nt(dzxb[:, sl], wx_ref[dirn, b]))
            dxc += jnp.concatenate(parts, axis=1)
        dxc_ref[...] = dxc

    def par(r):
        return pl.BlockSpec((r, CB), lambda h, i: (0, h))

    wspec = pl.BlockSpec((2, nb, HD, HD), lambda h, i: (0, h, 0, 0))
    tile = pl.BlockSpec((TR, CB), lambda h, i: (i, h))
    before, _, after = _halo_specs(R, CB, 0)
    last = pl.BlockSpec((8, CB), lambda h, i: (R // 8 - 1, h))
    nbt = D // HD
    return _call(body, name=name, grid=(2, R // TR),
                 in_specs=[tile, before, tile, last, tile, after, tile, tile, wspec, wspec, par(2), par(2), par(2)],
                 out_specs=[tile, wspec, wspec, par(2), par(2), par(2)],
                 out_shape=[_sds((R, D)), _sds((2, nbt, HD, HD)), _sds((2, nbt, HD, HD)), _sds((2, D)), _sds((2, D)),
                            _sds((2, D))])(xc, hf, hf, hf, hb, hb, gf, gb, WA, WX, ba, bx, lam)


def _scan_rows(n_groups, step, init):
    return lax.fori_loop(0, n_groups, lambda gi, c: step(pl.multiple_of(gi * 8, 8), c), init)


def _lru_scan_fwd(af, uf, ab, ub, S, name):
    R, D = af.shape
    W = min(SCAN_W, D)
    nm, nx = R // TR, S // TR
    nc = nm - nx
    ng = TR // 8

    def body(af_ref, uf_ref, ab_ref, ub_ref, hf_ref, hb_ref, cf_ref, cb_ref):
        @pl.when(pl.program_id(1) == 0)
        def _():
            cf_ref[...] = jnp.zeros_like(cf_ref)
            cb_ref[...] = jnp.zeros_like(cb_ref)

        def step(base, carry):
            hf, hb = carry
            baseb = pl.multiple_of(TR - 8 - base, 8)
            for r in range(8):
                tf, tb = base + r, baseb + 7 - r
                hf = af_ref[pl.ds(tf, 1), :] * hf + uf_ref[pl.ds(tf, 1), :]
                hf_ref[pl.ds(tf, 1), :] = hf
                hb = ab_ref[pl.ds(tb, 1), :] * hb + ub_ref[pl.ds(tb, 1), :]
                hb_ref[pl.ds(tb, 1), :] = hb
            return hf, hb

        hf, hb = _scan_rows(ng, step, (cf_ref[0:1, :], cb_ref[0:1, :]))
        cf_ref[0:1, :] = hf
        cb_ref[0:1, :] = hb

    fmap = lambda j, s: (jnp.where(s < nc, nx + s, s - nc), j)
    bmap = lambda j, s: (nm - 1 - s, j)
    fs, bs = pl.BlockSpec((TR, W), fmap), pl.BlockSpec((TR, W), bmap)
    return _call(body, name=name, grid=(D // W, nm), in_specs=[fs, fs, bs, bs], out_specs=[fs, bs],
                 out_shape=[_sds((R, D))] * 2, scratch=[pltpu.VMEM((8, W), F32), pltpu.VMEM((8, W), F32)])(af, uf, ab, ub)


def _lru_scan_bwd(af, ab, dhs, S, name):
    R, D = af.shape
    W = min(SCAN_W, D)
    nm, nx = R // TR, S // TR
    ng = TR // 8

    def body(af_ref, dhf_ref, ab_ref, dhb_ref, gf_ref, gb_ref, cf_ref, cb_ref):
        @pl.when(pl.program_id(1) == 0)
        def _():
            cf_ref[...] = jnp.zeros_like(cf_ref)
            cb_ref[...] = jnp.zeros_like(cb_ref)

        def step(base, carry):
            cf, cb = carry
            based = pl.multiple_of(TR - 8 - base, 8)
            for r in range(8):
                tf, tb = based + 7 - r, base + r
                g = dhf_ref[pl.ds(tf, 1), :] + cf
                gf_ref[pl.ds(tf, 1), :] = g
                cf = af_ref[pl.ds(tf, 1), :] * g
                g = dhb_ref[pl.ds(tb, 1), :] + cb
                gb_ref[pl.ds(tb, 1), :] = g
                cb = ab_ref[pl.ds(tb, 1), :] * g
            return cf, cb

        cf, cb = _scan_rows(ng, step, (cf_ref[0:1, :], cb_ref[0:1, :]))
        cf_ref[0:1, :] = cf
        cb_ref[0:1, :] = cb

    fmap = lambda j, s: (jnp.where(s < nx, nx - 1 - s, nm - 1 - (s - nx)), j)
    bmap = lambda j, s: (s, j)
    fs, bs = pl.BlockSpec((TR, W), fmap), pl.BlockSpec((TR, W), bmap)
    return _call(body, name=name, grid=(D // W, nm), in_specs=[fs, fs, bs, bs], out_specs=[fs, bs],
                 out_shape=[_sds((R, D))] * 2, scratch=[pltpu.VMEM((8, W), F32), pltpu.VMEM((8, W), F32)])(af, dhs, ab, dhs)


def _merge_fwd(P, hf, hb, O, S, D, col_lg, name):
    R = P.shape[0]
    CB = D // 2
    nx = S // TR

    def body(lg_ref, ga_ref, gl_ref, hf_ref, hb_ref, o_ref, mix_ref):
        ge, _ = _gelu_and_grad(lg_ref[...])
        lru = (hf_ref[...] + hb_ref[...]) * ge
        mix_ref[...] = (_sigmoid(ga_ref[...]) * o_ref[...] + _sigmoid(gl_ref[...]) * lru).astype(mix_ref.dtype)

    def col(c0):
        return pl.BlockSpec((TR, CB), lambda h, i: (i, c0 + h))

    return _call(body, name=name, grid=(2, R // TR),
                 in_specs=[col(col_lg), col(col_lg + 2), col(col_lg + 4), col(0), col(0),
                           pl.BlockSpec((TR, CB), lambda h, i: (jnp.minimum(i, nx - 1), h))],
                 out_specs=col(0), out_shape=_sds((R, D), _MXU))(P, P, P, hf, hb, O)


def _merge_bwd(dmix, P, hf, hb, O, S, D, col_lg, name, jobs=()):
    R = P.shape[0]
    CB = D // 2
    nx = S // TR

    def body(dm_ref, lg_ref, ga_ref, gl_ref, hf_ref, hb_ref, o_ref, do_ref, dhs_ref, dp_ref, stash, sems):
        h, i = pl.program_id(0), pl.program_id(1)
        dm = jnp.where(i < nx, dm_ref[...], 0.0)
        sa, sl = _sigmoid(ga_ref[...]), _sigmoid(gl_ref[...])
        ge, dge = _gelu_and_grad(lg_ref[...])
        hs = hf_ref[...] + hb_ref[...]
        dl = dm * sl
        do_ref[...] = (dm * sa).astype(do_ref.dtype)
        dhs_ref[...] = dl * ge
        stash[0] = (dl * hs * dge).astype(stash.dtype)
        stash[1] = (dm * o_ref[...] * sa * (1.0 - sa)).astype(stash.dtype)
        stash[2] = (dm * (hs * ge) * sl * (1.0 - sl)).astype(stash.dtype)
        rows = pl.ds(pl.multiple_of(i * TR, TR), TR)
        copies = [pltpu.make_async_copy(stash.at[sec], dp_ref.at[rows, pl.ds(pl.multiple_of((col_lg + 2 * sec + h) * CB, CB), CB)],
                                        sems.at[sec]) for sec in range(3)]
        for cp in copies:
            cp.start()
        for cp in copies:
            cp.wait()

    def col(c0):
        return pl.BlockSpec((TR, CB), lambda h, i: (i, c0 + h))

    xrow = pl.BlockSpec((TR, CB), lambda h, i: (jnp.minimum(i, nx - 1), h))
    return _call(body, jobs=jobs, name=name, grid=(2, R // TR),
                 in_specs=[xrow, col(col_lg), col(col_lg + 2), col(col_lg + 4), col(0), col(0), xrow],
                 out_specs=[col(0), col(0), pl.BlockSpec(memory_space=pl.ANY)],
                 out_shape=[_sds((R, D), _MXU), _sds((R, D)), _sds(P.shape, _MXU)],
                 scratch=[pltpu.VMEM((3, TR, CB), _MXU), pltpu.SemaphoreType.DMA((3,))])(dmix, P, P, P, hf, hb, O)


def _conv_bwd(dxc, P, conv_w, dP, S, D, col0, name, jobs=()):
    R = P.shape[0]
    CB = D // 2

    def body(dp_, d_ref, dn_, xp_ref, x_ref, xn_ref, cw_ref, dp_in, dpo_ref, dcw_ref, dcb_ref):
        i = pl.program_id(1)

        @pl.when(i == 0)
        def _():
            dcw_ref[...] = jnp.zeros_like(dcw_ref)
            dcb_ref[...] = jnp.zeros_like(dcb_ref)

        d = d_ref[...]
        catd = jnp.concatenate([dp_[...], d, dn_[...]], axis=0)
        catx = jnp.concatenate([xp_ref[...], x_ref[...], xn_ref[...]], axis=0)
        from_start, to_end = _seq_pos(i, S, R, CB)
        dxl = (cw_ref[2:3, :] * catd
               + cw_ref[0:1, :] * jnp.where(to_end >= 3, _shift(catd, -2), 0.0)
               + cw_ref[1:2, :] * jnp.where(to_end >= 2, _shift(catd, -1), 0.0)
               + cw_ref[3:4, :] * jnp.where(from_start >= 1, _shift(catd, 1), 0.0))
        dpo_ref[...] = dxl[8:8 + TR, :].astype(dpo_ref.dtype)
        taps = (jnp.where(from_start >= 2, _shift(catx, 2), 0.0), jnp.where(from_start >= 1, _shift(catx, 1), 0.0),
                catx, jnp.where(to_end >= 2, _shift(catx, -1), 0.0))
        for kk in range(4):
            dcw_ref[kk:kk + 1, :] += jnp.sum(d * taps[kk][8:8 + TR, :], axis=0, keepdims=True)
        dcb_ref[...] += jnp.sum(d, axis=0, keepdims=True)

    return _call(body, jobs=jobs, name=name, grid=(2, R // TR),
                 in_specs=_halo_specs(R, CB, 0) + _halo_specs(R, CB, col0)
                 + [pl.BlockSpec((4, CB), lambda h, i: (0, h)), pl.BlockSpec(memory_space=pl.ANY)],
                 out_specs=[pl.BlockSpec((TR, CB), lambda h, i: (i, col0 + h)), pl.BlockSpec((4, CB), lambda h, i: (0, h)),
                            pl.BlockSpec((1, CB), lambda h, i: (0, h))],
                 out_shape=[_sds(dP.shape, dP.dtype), _sds((4, D)), _sds((1, D))],
                 aliases={7: 0})(dxc, dxc, dxc, P, P, P, conv_w, dP)


def _rope_tables(S, C):
    t = jnp.arange(S, dtype=jnp.int32)
    row = (t // GRID_W).astype(F32)
    col = (t % GRID_W).astype(F32)
    axis_dims = HD // 2
    freqs = ROPE_THETA ** (-jnp.arange(0, axis_dims, 2, dtype=F32) / axis_dims)
    ang = jnp.concatenate([row[:, None] * freqs, col[:, None] * freqs], axis=-1)
    cos = jnp.repeat(jnp.cos(ang), 2, axis=-1)
    sin = jnp.repeat(jnp.sin(ang), 2, axis=-1) * jnp.tile(jnp.array([-1.0, 1.0], F32), HD // 2)
    return (jnp.concatenate([cos, jnp.ones((C, HD), F32)], axis=0),
            jnp.concatenate([sin, jnp.zeros((C, HD), F32)], axis=0))


def _local_step(x, ctx, target, modx, modc, ng, shards, qg, kg, conv_w, conv_b, WA, WX, ba, bx, lam, fg, idx, opt):
    S, D = x.shape
    C = ctx.shape[0]
    R = S + C
    nx = S // TR
    Nb = shards['w_in'].shape[-1]
    Fb = shards['wd0'].shape[1]
    KVW = (ND * Nb - 5 * D) // 2
    G = D // KVW
    CB = D // 2
    col_lx = (D + 2 * KVW) // CB
    assert S % TR == 0 and C % TR == 0 and (D + 2 * KVW) % CB == 0 and CB % HD == 0
    mod2 = jnp.stack([modx, modc])
    X0 = jnp.concatenate([x, ctx], axis=0)
    COS, SIN = _rope_tables(S, C)
    ng0, ng1, ng2 = ng[0:1], ng[1:2], ng[2:3]
    ag = lambda n: _ag_job(shards[n])
    sib = lambda Gp: _rs_sibling_job(Gp.reshape(ND, -1, Gp.shape[-1]))
    add = lambda Gp, bufA, tag: _rs_add(Gp.reshape(ND, -1, Gp.shape[-1]), bufA, idx, f"{tag}_rs_add")
    out = {}

    def tn_specs(rows):
        tm = _tm(rows)
        return pl.BlockSpec((None, tm, Fb), lambda d, r: (d, r, 0)), pl.BlockSpec((tm, D), lambda d, r: (r, 0))

    wd_spec = pl.BlockSpec((None, None, Fb, D), lambda d, r: (d, 0, 0, 0))

    ((WG0,),) = _comm_call([ag('wg0')], "ag_wg0")
    H1 = _norm_mod_fwd(X0, ng0, mod2, 0, R, nx, "ffn1_norm")
    G1, ((WU0,),) = _ffn_gate(H1, WG0, R, "ffn1_gate", jobs=[ag('wu0')])
    (U1, A1), ((WD0,),) = _ffn_up(H1, None, WU0, R, "ffn1_up", jobs=[ag('wd0')], G=G1)
    (Y1, X1, H2), ((WIN,),) = _ffn_down(A1, WD0, 0, X0, mod2, 0, R, S, "ffn1_down", jobs=[ag('w_in')], next_norm=(ng1, 1))
    P, ((WOUT,), (WG1,)) = _proj_in(H2, WIN, "proj_in", jobs=[ag('w_out'), ag('wg1')])
    q, k, v = _qkv_prep(P, qg, kg, COS, SIN, D, KVW, "qkv_prep")
    (O, LSE), ((WU1,), (WD1,)) = _attn_fwd(q, k, v, S, G, "attn_fwd", jobs=[ag('wu1'), ag('wd1')])
    WOUT = WOUT.reshape(D, D)
    xc, af, uf, ab, ub = _lru_gates_fwd(P, conv_w, conv_b, WA, WX, ba, bx, lam, S, D, col_lx, "lru_gates")
    hf, hb = _lru_scan_fwd(af, uf, ab, ub, S, "lru_scan")
    mixb = _merge_fwd(P, hf, hb, O, S, D, col_lx + 2, "merge")
    Z, X2, H3 = _proj_out(mixb, WOUT, X1, mod2, ng2, S, "proj_out")
    G3, U3, A3 = _ffn_up(H3, WG1, WU1, S, "ffn2_up")
    loss, dX3, dY3b, dg3, dfg = _ffn_down_loss(A3, WD1, X2, mod2, 2, fg, target, "ffn2_down_loss")
    dg3 = dg3[None]

    dG3, dU3 = _ffn_dact(dY3b, WD1, 0, G3, U3, S, "ffn2_dact")
    blk, row = tn_specs(S)
    dWD1 = _mm_tn(A3, blk, dY3b, row, (ND, 1, Fb, D), wd_spec, S, "ffn2_dwd")
    dWG1 = _mm_tn(dG3, blk, H3, row, (ND, 1, Fb, D), wd_spec, S, "ffn2_dwg")
    dWU1 = _mm_tn(dU3, blk, H3, row, (ND, 1, Fb, D), wd_spec, S, "ffn2_dwu")
    dH3, ((a_wd1,), (a_wg1,), (a_wu1,)) = _ffn_dh(dG3, dU3, WG1, WU1, S, "ffn2_dh", jobs=[sib(dWD1), sib(dWG1), sib(dWU1)])
    T_wd1, own_wd1 = add(dWD1, a_wd1, "wd1")
    T_wg1, own_wg1 = add(dWG1, a_wg1, "wg1")
    T_wu1, own_wu1 = add(dWU1, a_wu1, "wu1")
    dX2, dsh3, dsc3, dng2, dZb, dg2 = _norm_mod_bwd(X2, dH3, dX3, ng2, mod2, 2, S, nx, nx, "ffn2_dnorm", branch=(Z, 1, 1.0))

    dmix = _dproj_out(dZb, WOUT, "dproj_out")
    dWOUT = _mm_tn(mixb, pl.BlockSpec((_tm(S), D // ND), lambda d, r: (r, d)), dZb, pl.BlockSpec((_tm(S), D), lambda d, r: (r, 0)),
                   (ND, D // ND, D), pl.BlockSpec((None, D // ND, D), lambda d, r: (d, 0, 0)), S, "dw_out")
    (dOb, dhs, dP), ((a_wout,),) = _merge_bwd(dmix, P, hf, hb, O, S, D, col_lx + 2, "merge_bwd", jobs=[sib(dWOUT)])
    T_wout, own_wout = add(dWOUT, a_wout, "w_out")
    gf, gb = _lru_scan_bwd(af, ab, dhs, S, "lru_scan_bwd")
    dxc, dWA, dWX, dba, dbx, dlam = _lru_gates_bwd(xc, hf, hb, gf, gb, WA, WX, ba, bx, lam, S, "lru_gates_bwd")
    dLW = jnp.stack([dWA, dWX]).reshape(ND, -1, HD)
    (dP, dconv_w, dconv_b), ((a_lw,),) = _conv_bwd(dxc, P, conv_w, dP, S, D, col_lx, "conv_bwd", jobs=[sib(dLW)])
    T_lw, own_lw = add(dLW, a_lw, "lru_w")
    (dq, dk, dv), ((b_wd1,), (b_wg1,), (b_wu1,), (b_wout,), (b_lw,)) = _attn_bwd(
        q, k, v, O, LSE, dOb, S, G, "attn_bwd",
        jobs=[_rs_chips_job(T_wd1), _rs_chips_job(T_wg1), _rs_chips_job(T_wu1), _rs_chips_job(T_wout), _rs_chips_job(T_lw)])
    fin_wd = _rs_finish(own_wd1, b_wd1, "wd1_rs_finish", opt['ffn_wd'], 1, 2)
    fin_wg = _rs_finish(own_wg1, b_wg1, "wg1_rs_finish", opt['ffn_wg'], 1, 2)
    fin_wu = _rs_finish(own_wu1, b_wu1, "wu1_rs_finish", opt['ffn_wu'], 1, 2)
    out['w_out'] = _rs_finish(own_wout, b_wout, "w_out_rs_finish", opt['w_out'])
    (lw_sum,) = _rs_finish(own_lw, b_lw, "lru_w_rs_finish")
    dP, dqg, dkg = _qkv_bwd(P, dq, dk, dv, qg, kg, COS, SIN, dP, D, KVW, nx, "qkv_bwd")
    dH2, ((lw_full,),) = _dproj_in(dP, WIN, "dproj_in", jobs=[_ag_job(lw_sum)])
    dWIN = _mm_tn(H2, pl.BlockSpec((_tm(R), D), lambda d, r: (r, 0)), dP, pl.BlockSpec((_tm(R), Nb), lambda d, r: (r, d)),
                  (ND, D, Nb), pl.BlockSpec((None, D, Nb), lambda d, r: (d, 0, 0)), R, "dw_in")
    dX1, dsh2, dsc2, dng1, dY1b, dg1 = _norm_mod_bwd(X1, dH2, dX2, ng1, mod2, 1, R, nx, nx, "mix_dnorm", branch=(Y1, 0, 0.5))

    (dG1, dU1), ((a_win,),) = _ffn_dact(dY1b, WD0, 0, G1, U1, R, "ffn1_dact", jobs=[sib(dWIN)])
    T_win, own_win = add(dWIN, a_win, "w_in")
    blk, row = tn_specs(R)
    dWD0, ((b_win01,),) = _mm_tn(A1, blk, dY1b, row, (ND, 1, Fb, D), wd_spec, R, "ffn1_dwd", jobs=[_rs_chips_job(T_win, (0, 1))])
    dWG0, ((b_win2,), (a_wd0,)) = _mm_tn(dG1, blk, H1, row, (ND, 1, Fb, D), wd_spec, R, "ffn1_dwg",
                                         jobs=[_rs_chips_job(T_win, (2,)), sib(dWD0)])
    out['w_in'] = _rs_finish(own_win, [(b_win01, 0), (b_win01, 1), (b_win2, 0)], "w_in_rs_finish", opt['w_in'])
    T_wd0, own_wd0 = add(dWD0, a_wd0, "wd0")
    dWU0, ((a_wg0,), (b_wd0,)) = _mm_tn(dU1, blk, H1, row, (ND, 1, Fb, D), wd_spec, R, "ffn1_dwu",
                                        jobs=[sib(dWG0), _rs_chips_job(T_wd0)])
    T_wg0, own_wg0 = add(dWG0, a_wg0, "wg0")
    out['ffn_wd'] = _rs_finish(own_wd0, b_wd0, "wd0_rs_finish", opt['ffn_wd'], 0, 2, fin_wd)
    dH1, ((a_wu0,), (b_wg0,)) = _ffn_dh(dG1, dU1, WG0, WU0, R, "ffn1_dh", jobs=[sib(dWU0), _rs_chips_job(T_wg0)])
    T_wu0, own_wu0 = add(dWU0, a_wu0, "wu0")
    out['ffn_wg'] = _rs_finish(own_wg0, b_wg0, "wg0_rs_finish", opt['ffn_wg'], 0, 2, fin_wg)
    (dX0, dsh1, dsc1, dng0), ((b_wu0,),) = _norm_mod_bwd(
        X0, dH1, dX1, ng0, mod2, 0, R, nx, R // TR, "ffn1_dnorm", jobs=[_rs_chips_job(T_wu0, (0, 1))])
    out['wu0_pending'] = (T_wu0, own_wu0, b_wu0, fin_wu)
    out['lru_w'] = lw_full

    zero = jnp.zeros((1, D), F32)
    dmodx = jnp.concatenate([dsh1[0], dsc1[0], dg1[0], dsh2[0], dsc2[0], dg2[0], dsh3[0], dsc3[0], dg3[0]], axis=0)
    dmodc = jnp.concatenate([dsh1[1], dsc1[1], dg1[1], dsh2[1], dsc2[1], zero, zero, zero, zero], axis=0)
    out.update(loss=loss, grad_x=dX0[:S], dmodx=dmodx, dmodc=dmodc, norm_g=jnp.concatenate([dng0, dng1, dng2], axis=0),
               q_norm_g=dqg, k_norm_g=dkg, conv_w=dconv_w, conv_b=dconv_b, lru_ba=dba, lru_bx=dbx,
               lru_lambda=dlam, final_norm_g=dfg)
    return out


def _mesh_pos():
    return lax.axis_index("x"), lax.axis_index("y"), lax.axis_index("c")


def _all_gather(xb, name, in_vmem=False):
    space = pltpu.VMEM if in_vmem else pl.ANY

    def body(x_ref, out_ref, send_sems, recv_sems, local_sem):
        x, y, c = _mesh_pos()
        me, sibling = (x, y, c), (x, y, 1 - c)
        chips = [(1 - x, y), (x, 1 - y), (1 - x, 1 - y)]

        def slot(px, py, pc):
            return out_ref.at[4 * px + 2 * py + pc]

        def copy(k, block, to, src=None):
            return pltpu.make_async_remote_copy(
                src_ref=slot(*block) if src is None else src, dst_ref=slot(*block),
                send_sem=send_sems.at[k], recv_sem=recv_sems.at[k], device_id=to, device_id_type=MESH)

        mine = pltpu.make_async_copy(x_ref, slot(*me), local_sem)
        mine.start()
        first = [copy(0, me, sibling, src=x_ref)]
        first += [copy(1 + j, me, (*chip, c), src=x_ref) for j, chip in enumerate(chips)]
        for cp in first:
            cp.start()
        passed = [copy(4 + j, (*chip, c), sibling) for j, chip in enumerate(chips)]
        for j, chip in enumerate(chips):
            copy(1 + j, (*chip, c), me).wait_recv()
            passed[j].start()
        copy(0, sibling, me).wait_recv()
        for j, chip in enumerate(chips):
            copy(4 + j, (*chip, 1 - c), me).wait_recv()
        for cp in first + passed:
            cp.wait_send()
        mine.wait()

    return pl.pallas_call(
        body, name=name, out_shape=_sds((ND,) + xb.shape, xb.dtype),
        in_specs=[pl.BlockSpec(memory_space=space)], out_specs=pl.BlockSpec(memory_space=space),
        scratch_shapes=[pltpu.SemaphoreType.DMA((7,)), pltpu.SemaphoreType.DMA((7,)), pltpu.SemaphoreType.DMA(())])(xb)


def _ag_job(xb):
    def parts(ins, outs, sems, starting=False, relaying=False, finishing=False):
        x_ref, out_ref = ins[0], outs[0]
        send_sems, recv_sems, local_sem = sems
        x, y, c = _mesh_pos()
        me, sibling = (x, y, c), (x, y, 1 - c)
        chips = [(1 - x, y), (x, 1 - y), (1 - x, 1 - y)]

        def slot(px, py, pc):
            return out_ref.at[4 * px + 2 * py + pc]

        def copy(k, block, to, src=None):
            return pltpu.make_async_remote_copy(
                src_ref=slot(*block) if src is None else src, dst_ref=slot(*block),
                send_sem=send_sems.at[k], recv_sem=recv_sems.at[k], device_id=to, device_id_type=MESH)

        passed = [copy(4 + j, (*chip, c), sibling) for j, chip in enumerate(chips)] if not starting else None
        if relaying:
            return None, None, passed, [copy(1 + j, (*chip, c), me) for j, chip in enumerate(chips)], None
        mine = pltpu.make_async_copy(x_ref, slot(*me), local_sem)
        first = [copy(0, me, sibling, src=x_ref)] + [copy(1 + j, me, (*chip, c), src=x_ref) for j, chip in enumerate(chips)]
        if starting:
            return mine, first
        landed = None
        from_sibling = [copy(0, sibling, me)] + [copy(4 + j, (*chip, 1 - c), me) for j, chip in enumerate(chips)]
        return mine, first, passed, landed, from_sibling

    def start(ins, outs, sems):
        mine, first = parts(ins, outs, sems, starting=True)
        mine.start()
        for cp in first:
            cp.start()

    def relay(ins, outs, sems):
        _, _, passed, landed, _ = parts(ins, outs, sems, relaying=True)
        for j in range(3):
            landed[j].wait_recv()
            passed[j].start()

    def finish(ins, outs, sems):
        mine, first, passed, _, from_sibling = parts(ins, outs, sems, finishing=True)
        for cp in from_sibling:
            cp.wait_recv()
        for cp in first + passed:
            cp.wait_send()
        mine.wait()

    return _Job([xb], [_sds((ND,) + xb.shape, xb.dtype)],
                [pltpu.SemaphoreType.DMA((7,)), pltpu.SemaphoreType.DMA((7,)), pltpu.SemaphoreType.DMA(())], start, finish, relay)


def _rs_sibling_job(Gp):
    def copies(ins, outs, sems):
        x, y, c = _mesh_pos()
        return [pltpu.make_async_remote_copy(
            src_ref=ins[0].at[2 * k + (1 - c)], dst_ref=outs[0].at[k], send_sem=sems[0].at[k], recv_sem=sems[1].at[k],
            device_id=(x, y, 1 - c), device_id_type=MESH) for k in range(4)]

    def start(ins, outs, sems):
        for cp in copies(ins, outs, sems):
            cp.start()

    def finish(ins, outs, sems):
        cps = copies(ins, outs, sems)
        for cp in cps:
            cp.wait_recv()
        for cp in cps:
            cp.wait_send()

    return _Job([Gp], [_sds((4,) + Gp.shape[1:], Gp.dtype)],
                [pltpu.SemaphoreType.DMA((4,)), pltpu.SemaphoreType.DMA((4,))], start, finish)


def _rs_chips_job(T, dests=(0, 1, 2)):
    def copies(ins, outs, sems):
        x, y, c = _mesh_pos()
        chips = [(1 - x, y), (x, 1 - y), (1 - x, 1 - y)]
        cps = []
        for i, j in enumerate(dests):
            px, py = chips[j]
            cps.append(pltpu.make_async_remote_copy(
                src_ref=ins[0].at[2 * px + py], dst_ref=outs[0].at[i], send_sem=sems[0].at[i], recv_sem=sems[1].at[i],
                device_id=(px, py, c), device_id_type=MESH))
        return cps

    def start(ins, outs, sems):
        for cp in copies(ins, outs, sems):
            cp.start()

    def finish(ins, outs, sems):
        cps = copies(ins, outs, sems)
        for cp in cps:
            cp.wait_recv()
        for cp in cps:
            cp.wait_send()

    n = len(dests)
    return _Job([T], [_sds((n,) + T.shape[1:], T.dtype)],
                [pltpu.SemaphoreType.DMA((n,)), pltpu.SemaphoreType.DMA((n,))], start, finish)


def _tile_rows(rows, cols):
    best = None
    for t in range(16, rows + 1, 16):
        if rows % t == 0 and t * cols * 4 <= (3 << 19):
            best = t
    return best if best is not None else rows


def _prefetch_call(body, *, name, grid, in_specs, out_specs, out_shape):
    return pl.pallas_call(
        body, name=name, out_shape=out_shape,
        grid_spec=pltpu.PrefetchScalarGridSpec(num_scalar_prefetch=1, grid=grid, in_specs=in_specs, out_specs=out_specs),
        compiler_params=pltpu.CompilerParams(dimension_semantics=("arbitrary",) * len(grid), vmem_limit_bytes=VMEM_LIMIT))


def _rs_add(Gp, bufA, idx, name):
    rows, cols = Gp.shape[1:]
    tr = _tile_rows(rows, cols)

    def body(i_ref, g_ref, a_ref, t_ref, own_ref):
        t = g_ref[...] + a_ref[...]
        t_ref[...] = t.astype(t_ref.dtype)

        @pl.when(pl.program_id(1) == i_ref[1])
        def _():
            own_ref[...] = t

    return _prefetch_call(
        body, name=name, grid=(rows // tr, 4),
        in_specs=[pl.BlockSpec((None, tr, cols), lambda r, k, i_ref: (2 * k + i_ref[0], r, 0)),
                  pl.BlockSpec((None, tr, cols), lambda r, k, i_ref: (k, r, 0))],
        out_specs=[pl.BlockSpec((None, tr, cols), lambda r, k, i_ref: (k, r, 0)),
                   pl.BlockSpec((tr, cols), lambda r, k, i_ref: (r, 0))],
        out_shape=[_sds((4, rows, cols), jnp.bfloat16), _sds((rows, cols))])(idx, Gp, bufA)


def _adam(w, g, m, v):
    m = ADAM_B1 * m + (1.0 - ADAM_B1) * g
    v = ADAM_B2 * v + (1.0 - ADAM_B2) * (g * g)
    m_hat = m / (1.0 - ADAM_B1 ** ADAM_STEP)
    v_hat = v / (1.0 - ADAM_B2 ** ADAM_STEP)
    return -ADAM_LR * (m_hat / (jnp.sqrt(v_hat) + ADAM_EPS) + ADAM_WD * w), m, v


def _rs_finish(Town, bufB, name, wmv=None, slab=0, n_slabs=1, prev=None):
    rows, cols = Town.shape
    tr = _tile_rows(rows, cols)
    nr = rows // tr
    n_in = 4 + (3 if wmv is not None else 0)
    n_out = 4 if wmv is not None else 1

    def body(*refs):
        ins, outs = refs[:n_in], refs[len(refs) - n_out:]
        g = ((ins[0][...] + ins[1][...].astype(F32)) + ins[2][...].astype(F32)) + ins[3][...].astype(F32)
        outs[0][...] = g
        if wmv is not None:
            d, m, v = _adam(ins[4][...], g, ins[5][...], ins[6][...])
            outs[1][...] = d
            outs[2][...] = m
            outs[3][...] = v

    plain = pl.BlockSpec((tr, cols), lambda r: (r, 0))
    slabbed = pl.BlockSpec((tr, cols), lambda r: (slab * nr + r, 0))
    pairs = bufB if isinstance(bufB, list) else [(bufB, j) for j in range(3)]
    in_specs = [plain] + [pl.BlockSpec((None, tr, cols), (lambda j: lambda r: (j, r, 0))(j)) for _, j in pairs]
    args = [Town] + [a for a, _ in pairs]
    if wmv is not None:
        in_specs += [slabbed] * 3
        args += list(wmv)
    aliases = None
    if prev is not None:
        in_specs += [pl.BlockSpec(memory_space=pl.ANY)] * n_out
        aliases = {len(args) + i: i for i in range(n_out)}
        args += list(prev)
    return _call(body, name=name, grid=(nr,), in_specs=in_specs, out_specs=[slabbed] * n_out,
                 out_shape=[_sds((n_slabs * rows, cols))] * n_out, aliases=aliases)(*args)


def _adamw_plain(w, g, m, v, name, jobs=()):
    rows, cols = w.shape
    tr = _tile_rows(rows, cols)

    def body(w_ref, g_ref, m_ref, v_ref, d_ref, mo_ref, vo_ref):
        d, m_, v_ = _adam(w_ref[...], g_ref[...], m_ref[...], v_ref[...])
        d_ref[...] = d
        mo_ref[...] = m_
        vo_ref[...] = v_

    spec = pl.BlockSpec((tr, cols), lambda r: (r, 0))
    return _call(body, jobs=jobs, name=name, grid=(rows // tr,), in_specs=[spec] * 4, out_specs=[spec] * 3,
                 out_shape=[_sds((rows, cols))] * 3)(w, g, m, v)


_MOD_TK = 512


def _mod_fwd(cc16, w_loc, b_loc, name):
    D, cols = w_loc.shape
    tk = min(_MOD_TK, D)
    nk = D // tk

    def body(c_ref, w_ref, b_ref, o_ref):
        kk = pl.program_id(0)

        @pl.when(kk == 0)
        def _():
            o_ref[...] = jnp.zeros_like(o_ref)

        cc = c_ref[...]
        o_ref[...] += _dot(cc * _sigmoid(cc), w_ref[...])

        @pl.when(kk == nk - 1)
        def _():
            o_ref[...] += b_ref[...]

    return _call(body, name=name, grid=(nk,),
                 in_specs=[pl.BlockSpec((16, tk), lambda kk: (0, kk)), pl.BlockSpec((tk, cols), lambda kk: (kk, 0)),
                           _full_spec((1, cols))],
                 out_specs=_full_spec((16, cols)), out_shape=_sds((16, cols)))(cc16, w_loc, b_loc)


def _mod_bwd(dm_loc, cc16, w_loc, name):
    D, cols = w_loc.shape

    def body(dm_ref, c_ref, w_ref, gw_ref, ds_ref):
        rows = [dm_ref[b, 0:1, :] for b in range(ND)]
        ctx = dm_ref[0, 1:2, :]
        for b in range(1, ND):
            ctx = ctx + dm_ref[b, 1:2, :]
        dm16 = jnp.concatenate(rows + [ctx, jnp.zeros((7, cols), F32)], axis=0)
        cc = c_ref[...]
        gw_ref[...] = _dot_tn(cc * _sigmoid(cc), dm16)
        ds_ref[...] = _dot_nt(dm16, w_ref[...])

    tk = min(_MOD_TK, D)
    return _call(body, name=name, grid=(D // tk,),
                 in_specs=[_full_spec((ND, 8, cols)), pl.BlockSpec((16, tk), lambda kk: (0, kk)),
                           pl.BlockSpec((tk, cols), lambda kk: (kk, 0))],
                 out_specs=[pl.BlockSpec((tk, cols), lambda kk: (kk, 0)), pl.BlockSpec((16, tk), lambda kk: (0, kk))],
                 out_shape=[_sds((D, cols)), _sds((16, D))])(dm_loc, cc16, w_loc)


def _bmod_grad(dm_all, name):
    n = dm_all.shape[-1]

    def body(dm_ref, o_ref):
        acc = dm_ref[0, 0:1, :] + dm_ref[0, 1:2, :]
        for b in range(1, ND):
            acc = (acc + dm_ref[b, 0:1, :]) + dm_ref[b, 1:2, :]
        o_ref[...] = acc

    return _call(body, name=name, grid=(1,), in_specs=[_full_spec((ND, 8, n))], out_specs=_full_spec((1, n)),
                 out_shape=_sds((1, n)))(dm_all)


_SMALL_ROWS = 24
_ROW_CCTX = 15


def _small_finish(parts, c_ctx, name):
    D = parts.shape[-1]

    def body(p_ref, c_ref, o_ref):
        acc = p_ref[0]
        for b in range(1, ND):
            acc = acc + p_ref[b]
        cc = c_ref[...]
        sg = _sigmoid(cc)
        dsilu = sg * (1.0 + cc * (1.0 - sg))
        row = lax.broadcasted_iota(jnp.int32, acc.shape, 0)
        o_ref[...] = jnp.where(row == _ROW_CCTX, acc * dsilu, acc)

    return _call(body, name=name, grid=(1,), in_specs=[_full_spec(parts.shape), _full_spec((1, D))],
                 out_specs=_full_spec((_SMALL_ROWS, D)), out_shape=_sds((_SMALL_ROWS, D)))(parts, c_ctx)


_WEIGHTS = ['c_ctx', 'w_mod', 'b_mod', 'norm_g', 'ffn_wg', 'ffn_wu', 'ffn_wd', 'w_in', 'w_out', 'q_norm_g', 'k_norm_g',
            'conv_w', 'conv_b', 'lru_wa', 'lru_ba', 'lru_wx', 'lru_bx', 'lru_lambda', 'final_norm_g']
_SMALL = ['c_ctx', 'b_mod', 'norm_g', 'q_norm_g', 'k_norm_g', 'conv_w', 'conv_b', 'lru_ba', 'lru_bx', 'lru_lambda',
          'final_norm_g']


def _pad_rows(a, rows):
    return jnp.pad(a, ((0, rows - a.shape[0]),) + ((0, 0),) * (a.ndim - 1))


def _step(w, m, v, x, c, ctx, loss_target):
    xi, yi, ci = _mesh_pos()
    me = 4 * xi + 2 * yi + ci
    idx = jnp.stack([ci, 2 * xi + yi]).astype(jnp.int32)
    S, D = x.shape[1:]
    Ds = D // ND
    cols = w['w_mod'].shape[-1]

    sp = jnp.concatenate([w['norm_g'][0], w['conv_w'][0], w['lru_ba'][0], w['lru_bx'][0], w['lru_lambda'][0]], axis=0)
    spg = _all_gather(_pad_rows(sp, 16), "ag_small_params", in_vmem=True)
    spf = jnp.transpose(spg, (1, 0, 2)).reshape(16, D)
    ng, conv_w, ba, bx, lam = spf[0:3], spf[3:7], spf[7:9], spf[9:11], spf[11:13]

    cg = _all_gather(_pad_rows(c, 8), "ag_cond", in_vmem=True)
    cc16 = _pad_rows(jnp.concatenate([cg[:, 0, :], w['c_ctx'][None, :]], axis=0), 16)
    b_loc = lax.dynamic_slice_in_dim(w['b_mod'], me * cols, cols, axis=1)
    mod_loc = _mod_fwd(cc16, w['w_mod'][0], b_loc, "mod_fwd")
    modg = _all_gather(mod_loc, "ag_mod", in_vmem=True)
    mod16 = jnp.transpose(modg, (1, 0, 2)).reshape(16, ND * cols)
    modx = lax.dynamic_index_in_dim(mod16, me, axis=0, keepdims=False).reshape(9, D)
    modc = mod16[8].reshape(9, D)

    shards = {'w_in': w['w_in'][0].astype(_MXU), 'w_out': w['w_out'][0].astype(_MXU)}
    for layer in range(2):
        shards[f'wg{layer}'] = w['ffn_wg'][0, layer].T.astype(_MXU)[None]
        shards[f'wu{layer}'] = w['ffn_wu'][0, layer].T.astype(_MXU)[None]
        shards[f'wd{layer}'] = w['ffn_wd'][0, layer].astype(_MXU)[None]
    WA, WX = w['lru_wa'][0].astype(_MXU), w['lru_wx'][0].astype(_MXU)
    big = ('ffn_wg', 'ffn_wu', 'ffn_wd', 'w_in', 'w_out')
    transposed = ('ffn_wg', 'ffn_wu')
    tr_view = lambda a: jnp.swapaxes(a, -1, -2)
    opt = {n: tuple((tr_view(a[n]) if n in transposed else a[n]) for a in (w, m, v)) for n in big}
    opt = {n: tuple(a.reshape(-1, a.shape[-1]) for a in t) for n, t in opt.items()}

    g = _local_step(x[0], ctx[0], loss_target[0], modx, modc, ng, shards, w['q_norm_g'], w['k_norm_g'],
                    conv_w, w['conv_b'], WA, WX, ba, bx, lam, w['final_norm_g'][None, :], idx, opt)

    grad, delta, new_m, new_v = {}, {}, {}, {}

    lfull = g['lru_w'].reshape((2,) + w['lru_wa'].shape[1:])
    for i, n in enumerate(('lru_wa', 'lru_wx')):
        shard = w[n].shape
        view = lambda a: a.reshape(-1, HD)
        grad[n] = lfull[i].reshape(shard)
        outs = _adamw_plain(view(w[n]), view(lfull[i]), view(m[n]), view(v[n]), f"adamw_{n}")
        delta[n], new_m[n], new_v[n] = [o.reshape(shard) for o in outs]

    dm = _pad_rows(jnp.stack([g['dmodx'].reshape(-1), g['dmodc'].reshape(-1)]), 8)
    dm_all = _all_gather(dm, "ag_dmod", in_vmem=True)
    dm_loc = lax.dynamic_slice_in_dim(dm_all, me * cols, cols, axis=2)
    gw_mod, dsil = _mod_bwd(dm_loc, cc16, w['w_mod'][0], "mod_bwd")
    grad['w_mod'] = gw_mod[None]
    T_wu0, own_wu0, b_wu0, fin_wu = g['wu0_pending']
    outs, ((b_wu0d,),) = _adamw_plain(w['w_mod'][0], gw_mod, m['w_mod'][0], v['w_mod'][0], "adamw_w_mod",
                                      jobs=[_rs_chips_job(T_wu0, (2,))])
    delta['w_mod'], new_m['w_mod'], new_v['w_mod'] = [o[None] for o in outs]
    g['ffn_wu'] = _rs_finish(own_wu0, [(b_wu0, 0), (b_wu0, 1), (b_wu0d, 0)], "wu0_rs_finish", opt['ffn_wu'], 0, 2, fin_wu)
    for n in big:
        if n in transposed:
            shape_t = w[n].shape[:-2] + (w[n].shape[-1], w[n].shape[-2])
            grad[n], delta[n], new_m[n], new_v[n] = [tr_view(o.reshape(shape_t)) for o in g[n]]
        else:
            grad[n], delta[n], new_m[n], new_v[n] = [o.reshape(w[n].shape) for o in g[n]]
    grad['b_mod'] = _bmod_grad(dm_all, "bmod_grad")

    pad_d = lambda a: jnp.concatenate([a, jnp.zeros((1, D - a.shape[1]), F32)], axis=1)
    small = jnp.concatenate([g['norm_g'], g['conv_w'], g['conv_b'], g['lru_ba'], g['lru_bx'], g['lru_lambda'],
                             g['final_norm_g'], dsil[8:9], pad_d(g['q_norm_g']), pad_d(g['k_norm_g'])], axis=0)
    parts = _all_gather(_pad_rows(small, _SMALL_ROWS), "ag_small_grads", in_vmem=True)
    ssum = _small_finish(parts, w['c_ctx'][None, :], "small_finish")
    mine = lambda rows: lax.dynamic_slice_in_dim(rows, me * Ds, Ds, axis=1)
    grad['norm_g'] = mine(ssum[0:3])[None]
    grad['conv_w'] = mine(ssum[3:7])[None]
    grad['conv_b'] = ssum[7:8]
    grad['lru_ba'] = mine(ssum[8:10])[None]
    grad['lru_bx'] = mine(ssum[10:12])[None]
    grad['lru_lambda'] = mine(ssum[12:14])[None]
    grad['final_norm_g'] = ssum[14]
    grad['c_ctx'] = ssum[_ROW_CCTX]
    grad['q_norm_g'] = ssum[16:17, :HD]
    grad['k_norm_g'] = ssum[17:18, :HD]

    def pack(d):
        flat = jnp.concatenate([d[n].reshape(-1) for n in _SMALL])
        padded = -(-flat.shape[0] // 1024) * 1024
        return jnp.concatenate([flat, jnp.zeros((padded - flat.shape[0],), F32)]).reshape(-1, HD)

    outs = _adamw_plain(pack(w), pack(grad), pack(m), pack(v), "adamw_small")
    off = 0
    for n in _SMALL:
        size = math.prod(w[n].shape)
        for dst, o in zip((delta, new_m, new_v), outs):
            dst[n] = o.reshape(-1)[off:off + size].reshape(w[n].shape)
        off += size

    loss = lax.psum(g['loss'][0, 0], ("x", "y", "c"))
    return (loss, g['grad_x'][None], *[grad[n] for n in _WEIGHTS], *[delta[n] for n in _WEIGHTS],
            *[new_m[n] for n in _WEIGHTS], *[new_v[n] for n in _WEIGHTS])


def kernel(x, c, ctx, c_ctx, w_mod, b_mod, norm_g, ffn_wg, ffn_wu, ffn_wd, w_in, w_out, q_norm_g, k_norm_g, conv_w, conv_b, lru_wa, lru_ba, lru_wx, lru_bx, lru_lambda, final_norm_g, loss_target, m_c_ctx, m_w_mod, m_b_mod, m_norm_g, m_ffn_wg, m_ffn_wu, m_ffn_wd, m_w_in, m_w_out, m_q_norm_g, m_k_norm_g, m_conv_w, m_conv_b, m_lru_wa, m_lru_ba, m_lru_wx, m_lru_bx, m_lru_lambda, m_final_norm_g, v_c_ctx, v_w_mod, v_b_mod, v_norm_g, v_ffn_wg, v_ffn_wu, v_ffn_wd, v_w_in, v_w_out, v_q_norm_g, v_k_norm_g, v_conv_w, v_conv_b, v_lru_wa, v_lru_ba, v_lru_wx, v_lru_bx, v_lru_lambda, v_final_norm_g):
    given = dict(locals())
    w = {n: given[n] for n in _WEIGHTS}
    m = {n: given["m_" + n] for n in _WEIGHTS}
    v = {n: given["v_" + n] for n in _WEIGHTS}
    return _step(w, m, v, x, c, ctx, loss_target)
```

```python
import functools
import math

import jax
import jax.numpy as jnp
from jax import lax
from jax.experimental import pallas as pl
from jax.experimental.pallas import tpu as pltpu

F32 = jnp.float32
_MXU = jnp.bfloat16
ND = 8
TR = 256
HD = 128
EPS = 1e-6
GRID_W = 64
ROPE_THETA = 10000.0
LRU_C = 8.0
VMEM_LIMIT = 56 * 1024 * 1024
SCAN_W = 512
ADAM_LR, ADAM_B1, ADAM_B2, ADAM_EPS, ADAM_WD, ADAM_STEP = 0.001, 0.9, 0.999, 1e-08, 0.01, 10
MESH = pl.DeviceIdType.MESH


class _Job:
    def __init__(self, inputs, out_shapes, sems, start, finish, relay=None, relay2=None):
        self.inputs, self.out_shapes, self.sems, self.start, self.finish = inputs, out_shapes, sems, start, finish
        self.relay, self.relay2 = relay, relay2


def _call(body, *, name, grid, in_specs, out_specs, out_shape, scratch=(), aliases=None, jobs=()):
    params = pltpu.CompilerParams(dimension_semantics=("arbitrary",) * len(grid), vmem_limit_bytes=VMEM_LIMIT)
    if not jobs:
        return pl.pallas_call(body, name=name, grid=grid, in_specs=in_specs, out_specs=out_specs, out_shape=out_shape,
                              scratch_shapes=scratch, input_output_aliases=aliases or {}, compiler_params=params)
    single = not isinstance(out_specs, (list, tuple))
    o_specs = [out_specs] if single else list(out_specs)
    o_shape = [out_shape] if single else list(out_shape)
    n_in, n_out, n_scr = len(in_specs), len(o_specs), len(scratch)
    j_in = [a for j in jobs for a in j.inputs]
    j_out = [s for j in jobs for s in j.out_shapes]
    j_sem = [s for j in jobs for s in j.sems]
    hbm = pl.BlockSpec(memory_space=pl.ANY)

    def wrapped(*refs):
        ins, rest = refs[:n_in], refs[n_in:]
        jin, rest = rest[:len(j_in)], rest[len(j_in):]
        outs, rest = rest[:n_out], rest[n_out:]
        jout, rest = rest[:len(j_out)], rest[len(j_out):]
        scr, jsem = rest[:n_scr], rest[n_scr:]
        first = functools.reduce(jnp.logical_and, [pl.program_id(a) == 0 for a in range(len(grid))])
        last = functools.reduce(jnp.logical_and, [pl.program_id(a) == grid[a] - 1 for a in range(len(grid))])
        def at_outer(step):
            return functools.reduce(jnp.logical_and, [pl.program_id(a) == (0 if a else step) for a in range(len(grid))])

        def each(which):
            i = o = s = 0
            for j in jobs:
                ni, no, ns = len(j.inputs), len(j.out_shapes), len(j.sems)
                if getattr(j, which, None) is not None:
                    getattr(j, which)(jin[i:i + ni], jout[o:o + no], jsem[s:s + ns])
                i, o, s = i + ni, o + no, s + ns

        @pl.when(first)
        def _():
            each("start")

        relay_early = grid[0] >= 4
        if relay_early:
            @pl.when(at_outer(grid[0] // 2))
            def _():
                each("relay")

            @pl.when(at_outer((7 * grid[0]) // 8))
            def _():
                each("relay2")

        body(*ins, *outs, *scr)

        @pl.when(last)
        def _():
            if not relay_early:
                each("relay")
                each("relay2")
            each("finish")

    call = pl.pallas_call(wrapped, name=name, grid=grid, in_specs=list(in_specs) + [hbm] * len(j_in),
                          out_specs=o_specs + [hbm] * len(j_out), out_shape=o_shape + j_out,
                          scratch_shapes=list(scratch) + j_sem, input_output_aliases=aliases or {}, compiler_params=params)

    def run(*args):
        res = call(*args, *j_in)
        comp = res[0] if single else list(res[:n_out])
        jres, o = [], n_out
        for j in jobs:
            jres.append(list(res[o:o + len(j.out_shapes)]))
            o += len(j.out_shapes)
        return comp, jres

    return run


def _comm_call(jobs, name):
    j_in = [a for j in jobs for a in j.inputs]
    j_out = [s for j in jobs for s in j.out_shapes]
    j_sem = [s for j in jobs for s in j.sems]
    hbm = pl.BlockSpec(memory_space=pl.ANY)

    def body(*refs):
        jin, jout, jsem = refs[:len(j_in)], refs[len(j_in):len(j_in) + len(j_out)], refs[len(j_in) + len(j_out):]
        for which in ("start", "relay", "relay2", "finish"):
            i = o = s = 0
            for j in jobs:
                ni, no, ns = len(j.inputs), len(j.out_shapes), len(j.sems)
                if getattr(j, which, None) is not None:
                    getattr(j, which)(jin[i:i + ni], jout[o:o + no], jsem[s:s + ns])
                i, o, s = i + ni, o + no, s + ns

    res = pl.pallas_call(body, name=name, out_shape=j_out, in_specs=[hbm] * len(j_in), out_specs=[hbm] * len(j_out),
                         scratch_shapes=j_sem)(*j_in)
    jres, o = [], 0
    for j in jobs:
        jres.append(list(res[o:o + len(j.out_shapes)]))
        o += len(j.out_shapes)
    return jres


def _sds(shape, dtype=F32):
    return jax.ShapeDtypeStruct(tuple(shape), dtype)


def _dot(a, b):
    return jnp.dot(a.astype(_MXU), b.astype(_MXU), preferred_element_type=F32)


def _dot_nt(a, b):
    return lax.dot_general(a.astype(_MXU), b.astype(_MXU), (((1,), (1,)), ((), ())), preferred_element_type=F32)


def _dot_tn(a, b):
    return lax.dot_general(a.astype(_MXU), b.astype(_MXU), (((0,), (0,)), ((), ())), preferred_element_type=F32)


def _sigmoid(x):
    return 0.5 * jnp.tanh(0.5 * x) + 0.5


_GELU_C = math.sqrt(2.0 / math.pi)


def _gelu_and_grad(x):
    x2 = x * x
    t = jnp.tanh(_GELU_C * (x + 0.044715 * x * x2))
    ge = 0.5 * x * (1.0 + t)
    dge = 0.5 * (1.0 + t) + 0.5 * x * (1.0 - t * t) * (_GELU_C * (1.0 + 3.0 * 0.044715 * x2))
    return ge, dge


def _tm(rows):
    assert rows % 4 == 0 and (rows // 4) % 16 == 0
    return rows // 4


def _row_spec(width, nmax=None):
    if nmax is None:
        return pl.BlockSpec((TR, width), lambda i: (i, 0))
    return pl.BlockSpec((TR, width), lambda i: (jnp.minimum(i, nmax), 0))


def _full_spec(shape):
    n = len(shape)
    return pl.BlockSpec(tuple(shape), lambda *_: (0,) * n)


def _mod_spec(D, nx):
    return pl.BlockSpec((None, 9, D), lambda i: (i // nx, 0, 0))


def _norm_mod_fwd(X, ng, mod2, k, rows, nx, name):
    D = X.shape[1]

    def body(x_ref, g_ref, mod_ref, h_ref):
        x = x_ref[...]
        r = lax.rsqrt(jnp.mean(x * x, axis=-1, keepdims=True) + EPS)
        n = (x * r) * g_ref[...]
        h_ref[...] = (n * (1.0 + mod_ref[3 * k + 1:3 * k + 2, :]) + mod_ref[3 * k:3 * k + 1, :]).astype(h_ref.dtype)

    return _call(body, name=name, grid=(rows // TR,),
                 in_specs=[_row_spec(D), _full_spec((1, D)), _mod_spec(D, nx)],
                 out_specs=_row_spec(D), out_shape=_sds((rows, D), _MXU))(X, ng, mod2)


def _norm_mod_bwd(X, dH, dXres, ng, mod2, k, rows, nx, res_tiles, name, jobs=(), branch=None):
    D = X.shape[1]
    ngroups = -(-(rows // TR) // nx)
    n_in = 5 + (1 if branch is not None else 0)

    def body(*refs):
        x_ref, dh_ref, dres_ref, g_ref, mod_ref = refs[:5]
        dx_ref, dsh_ref, dsc_ref, dng_ref = refs[n_in:n_in + 4]
        i = pl.program_id(0)
        x = x_ref[...]
        dh = dh_ref[...]
        g = g_ref[...]
        r = lax.rsqrt(jnp.mean(x * x, axis=-1, keepdims=True) + EPS)
        xh = x * r
        n = xh * g
        dn_mod = dh * (1.0 + mod_ref[3 * k + 1:3 * k + 2, :])

        @pl.when(i % nx == 0)
        def _():
            dsh_ref[...] = jnp.zeros_like(dsh_ref)
            dsc_ref[...] = jnp.zeros_like(dsc_ref)
            if branch is not None:
                refs[n_in + 5][...] = jnp.zeros_like(refs[n_in + 5])

        @pl.when(i == 0)
        def _():
            dng_ref[...] = jnp.zeros_like(dng_ref)

        dsh_ref[...] += jnp.sum(dh, axis=0, keepdims=True)
        dsc_ref[...] += jnp.sum(dh * n, axis=0, keepdims=True)
        dng_ref[...] += jnp.sum(dn_mod * xh, axis=0, keepdims=True)
        dn = dn_mod * g
        dres = jnp.where(i < res_tiles, dres_ref[...], 0.0)
        dx = r * (dn - xh * jnp.mean(dn * xh, axis=-1, keepdims=True)) + dres
        dx_ref[...] = dx
        if branch is not None:
            _, k2, coef = branch
            refs[n_in + 4][...] = ((coef * mod_ref[3 * k2 + 2:3 * k2 + 3, :]) * dx).astype(refs[n_in + 4].dtype)
            refs[n_in + 5][...] += jnp.sum(coef * dx * refs[5][...], axis=0, keepdims=True)

    grp = pl.BlockSpec((None, 1, D), lambda i: (i // nx, 0, 0))
    in_specs = [_row_spec(D), _row_spec(D), _row_spec(D, res_tiles - 1), _full_spec((1, D)), _mod_spec(D, nx)]
    out_specs = [_row_spec(D), grp, grp, _full_spec((1, D))]
    out_shape = [_sds((rows, D)), _sds((ngroups, 1, D)), _sds((ngroups, 1, D)), _sds((1, D))]
    args = [X, dH, dXres, ng, mod2]
    if branch is not None:
        in_specs.append(_row_spec(D))
        args.append(branch[0])
        out_specs += [_row_spec(D), grp]
        out_shape += [_sds((rows, D), _MXU), _sds((ngroups, 1, D))]
    return _call(body, jobs=jobs, name=name, grid=(rows // TR,), in_specs=in_specs, out_specs=out_specs,
                 out_shape=out_shape)(*args)


def _ffn_gate(H, WG, rows, name, jobs=()):
    D = H.shape[1]
    Fb = WG.shape[-2]

    def body(h_ref, w_ref, g_ref):
        g_ref[...] = _dot_nt(h_ref[...], w_ref[...])

    tm = _tm(rows)
    return _call(body, jobs=jobs, name=name, grid=(ND, rows // tm),
                 in_specs=[pl.BlockSpec((tm, D), lambda d, m: (m, 0)), pl.BlockSpec((None, None, Fb, D), lambda d, m: (d, 0, 0, 0))],
                 out_specs=pl.BlockSpec((None, tm, Fb), lambda d, m: (d, m, 0)), out_shape=_sds((ND, rows, Fb)))(H, WG)


def _ffn_up(H, WG, WU, rows, name, jobs=(), G=None):
    D = H.shape[1]
    Fb = WU.shape[-2]

    def body(h_ref, w_ref, x_ref, *outs):
        h = h_ref[...]
        g = x_ref[...] if G is not None else _dot_nt(h, x_ref[...])
        u = _dot_nt(h, w_ref[...])
        if G is None:
            outs[0][...] = g
        outs[-2][...] = u
        outs[-1][...] = ((g * _sigmoid(g)) * u).astype(outs[-1].dtype)

    tm = _tm(rows)
    blk = pl.BlockSpec((None, tm, Fb), lambda d, m: (d, m, 0))
    wspec = pl.BlockSpec((None, None, Fb, D), lambda d, m: (d, 0, 0, 0))
    f32o, bfo = _sds((ND, rows, Fb)), _sds((ND, rows, Fb), _MXU)
    return _call(body, jobs=jobs, name=name, grid=(ND, rows // tm),
                 in_specs=[pl.BlockSpec((tm, D), lambda d, m: (m, 0)), wspec, blk if G is not None else wspec],
                 out_specs=[blk, blk] if G is not None else [blk, blk, blk],
                 out_shape=[f32o, bfo] if G is not None else [f32o, f32o, bfo])(H, WU, G if G is not None else WG)


def _ffn_down(A, WD, layer, X, mod2, k, rows, S, name, jobs=(), next_norm=None):
    Fb, D = WD.shape[-2:]
    tm = _tm(rows) // 2

    def body(a_ref, w_ref, x_ref, mod_ref, *rest):
        if next_norm is not None:
            g_ref, y_ref, xn_ref, h_ref, acc_ref = rest
        else:
            y_ref, xn_ref, acc_ref = rest
        d = pl.program_id(1)

        @pl.when(d == 0)
        def _():
            acc_ref[...] = jnp.zeros_like(acc_ref)

        acc_ref[...] += _dot(a_ref[...], w_ref[...])

        @pl.when(d == ND - 1)
        def _():
            y = acc_ref[...]
            y_ref[...] = y
            is_ctx = (pl.program_id(0) * tm + lax.broadcasted_iota(jnp.int32, (tm, 1), 0)) >= S

            def mod_row(j):
                return jnp.where(is_ctx, mod_ref[1, j:j + 1, :], mod_ref[0, j:j + 1, :])

            xn = x_ref[...] + (0.5 * mod_row(3 * k + 2)) * y
            xn_ref[...] = xn
            if next_norm is not None:
                k2 = next_norm[1]
                n = (xn * lax.rsqrt(jnp.mean(xn * xn, axis=-1, keepdims=True) + EPS)) * g_ref[...]
                h_ref[...] = (n * (1.0 + mod_row(3 * k2 + 1)) + mod_row(3 * k2)).astype(h_ref.dtype)

    row = pl.BlockSpec((tm, D), lambda m, d: (m, 0))
    in_specs = [pl.BlockSpec((None, tm, Fb), lambda m, d: (d, m, 0)),
                pl.BlockSpec((None, None, Fb, D), lambda m, d: (d, layer, 0, 0)),
                row, pl.BlockSpec((2, 9, D), lambda m, d: (0, 0, 0))]
    args = [A, WD, X, mod2]
    out_specs, out_shape = [row, row], [_sds((rows, D)), _sds((rows, D))]
    if next_norm is not None:
        in_specs.append(pl.BlockSpec((1, D), lambda m, d: (0, 0)))
        args.append(next_norm[0])
        out_specs.append(row)
        out_shape.append(_sds((rows, D), _MXU))
    return _call(body, jobs=jobs, name=name, grid=(rows // tm, ND), in_specs=in_specs, out_specs=out_specs,
                 out_shape=out_shape, scratch=[pltpu.VMEM((tm, D), F32)])(*args)


def _ffn_down_loss(A, WD, X, mod2, k, fg, target, name):
    Fb, D = WD.shape[-2:]
    S = X.shape[0]
    tm = _tm(S) // 2

    def body(a_ref, w_ref, x_ref, mod_ref, g_ref, t_ref, loss_ref, dx_ref, dy_ref, dgate_ref, dg_ref, acc_ref):
        m, d = pl.program_id(0), pl.program_id(1)

        @pl.when(d == 0)
        def _():
            acc_ref[...] = jnp.zeros_like(acc_ref)

        @pl.when(jnp.logical_and(m == 0, d == 0))
        def _():
            loss_ref[...] = jnp.zeros_like(loss_ref)
            dgate_ref[...] = jnp.zeros_like(dgate_ref)
            dg_ref[...] = jnp.zeros_like(dg_ref)

        acc_ref[...] += _dot(a_ref[...], w_ref[...])

        @pl.when(d == ND - 1)
        def _():
            y = acc_ref[...]
            gate = 0.5 * mod_ref[3 * k + 2:3 * k + 3, :]
            x = x_ref[...] + gate * y
            g = g_ref[...]
            r = lax.rsqrt(jnp.mean(x * x, axis=-1, keepdims=True) + EPS)
            n = x * r
            err = n * g - t_ref[...]
            loss_ref[...] += 0.5 * jnp.sum(jnp.mean(err * err, axis=-1, keepdims=True), axis=0, keepdims=True)
            dy = err * (1.0 / D)
            dg_ref[...] += jnp.sum(dy * n, axis=0, keepdims=True)
            dn = dy * g
            dx = r * (dn - n * jnp.mean(dn * n, axis=-1, keepdims=True))
            dx_ref[...] = dx
            dy_ref[...] = (gate * dx).astype(dy_ref.dtype)
            dgate_ref[...] += jnp.sum(0.5 * dx * y, axis=0, keepdims=True)

    row = pl.BlockSpec((tm, D), lambda m, d: (m, 0))
    vec = pl.BlockSpec((1, D), lambda m, d: (0, 0))
    return _call(body, name=name, grid=(S // tm, ND),
                 in_specs=[pl.BlockSpec((None, tm, Fb), lambda m, d: (d, m, 0)),
                           pl.BlockSpec((None, None, Fb, D), lambda m, d: (d, 0, 0, 0)),
                           row, pl.BlockSpec((None, 9, D), lambda m, d: (0, 0, 0)), vec, row],
                 out_specs=[pl.BlockSpec((1, 1), lambda m, d: (0, 0)), row, row, vec, vec],
                 out_shape=[_sds((1, 1)), _sds((S, D)), _sds((S, D), _MXU), _sds((1, D)), _sds((1, D))],
                 scratch=[pltpu.VMEM((tm, D), F32)])(A, WD, X, mod2, fg, target)


def _ffn_dact(dYb, WD, layer, G, U, rows, name, jobs=()):
    Fb, D = WD.shape[-2:]

    def body(dy_ref, w_ref, g_ref, u_ref, dg_ref, du_ref):
        da = _dot_nt(dy_ref[...], w_ref[...])
        g = g_ref[...]
        sg = _sigmoid(g)
        dg_ref[...] = (da * u_ref[...] * (sg * (1.0 + g * (1.0 - sg)))).astype(dg_ref.dtype)
        du_ref[...] = (da * (g * sg)).astype(du_ref.dtype)

    tm = _tm(rows)
    blk = pl.BlockSpec((None, tm, Fb), lambda m, d: (d, m, 0))
    return _call(body, jobs=jobs, name=name, grid=(rows // tm, ND),
                 in_specs=[pl.BlockSpec((tm, D), lambda m, d: (m, 0)),
                           pl.BlockSpec((None, None, Fb, D), lambda m, d: (d, layer, 0, 0)), blk, blk],
                 out_specs=[blk, blk],
                 out_shape=[_sds((ND, rows, Fb), _MXU), _sds((ND, rows, Fb), _MXU)])(dYb, WD, G, U)


def _ffn_dh(dG, dU, WG, WU, rows, name, jobs=()):
    Fb, D = WG.shape[-2:]

    def body(dg_ref, du_ref, wg_ref, wu_ref, dh_ref, acc_ref):
        d = pl.program_id(1)

        @pl.when(d == 0)
        def _():
            acc_ref[...] = jnp.zeros_like(acc_ref)

        acc_ref[...] += _dot(dg_ref[...], wg_ref[...]) + _dot(du_ref[...], wu_ref[...])

        @pl.when(d == ND - 1)
        def _():
            dh_ref[...] = acc_ref[...]

    tm = _tm(rows)
    blk = pl.BlockSpec((None, tm, Fb), lambda m, d: (d, m, 0))
    wspec = pl.BlockSpec((None, None, Fb, D), lambda m, d: (d, 0, 0, 0))
    return _call(body, jobs=jobs, name=name, grid=(rows // tm, ND), in_specs=[blk, blk, wspec, wspec],
                 out_specs=pl.BlockSpec((tm, D), lambda m, d: (m, 0)), out_shape=_sds((rows, D)),
                 scratch=[pltpu.VMEM((tm, D), F32)])(dG, dU, WG, WU)


def _mm_tn(A, a_spec, B, b_spec, out_shape, out_spec, rows, name, prev=None, jobs=()):
    def body(*refs):
        a_ref, b_ref, o_ref = refs[0], refs[1], refs[-1]

        @pl.when(pl.program_id(1) == 0)
        def _():
            o_ref[...] = jnp.zeros_like(o_ref)

        o_ref[...] += _dot_tn(a_ref[...], b_ref[...])

    in_specs = [a_spec, b_spec]
    args = [A, B]
    aliases = None
    if prev is not None:
        in_specs.append(pl.BlockSpec(memory_space=pl.ANY))
        args.append(prev)
        aliases = {2: 0}
    return _call(body, jobs=jobs, name=name, grid=(ND, rows // _tm(rows)), in_specs=in_specs, out_specs=out_spec,
                 out_shape=_sds(out_shape), aliases=aliases)(*args)


def _proj_in(H2, WIN, name, jobs=()):
    R, D = H2.shape
    Nb = WIN.shape[-1]

    def body(h_ref, w_ref, p_ref):
        p_ref[...] = _dot(h_ref[...], w_ref[...])

    tm = _tm(R)
    return _call(body, jobs=jobs, name=name, grid=(ND, R // tm),
                 in_specs=[pl.BlockSpec((tm, D), lambda d, m: (m, 0)), pl.BlockSpec((None, D, Nb), lambda d, m: (d, 0, 0))],
                 out_specs=pl.BlockSpec((tm, Nb), lambda d, m: (m, d)), out_shape=_sds((R, ND * Nb)))(H2, WIN)


def _dproj_in(dP, WIN, name, jobs=()):
    R = dP.shape[0]
    D, Nb = WIN.shape[-2:]

    def body(dp_ref, w_ref, dh_ref, acc_ref):
        d = pl.program_id(1)

        @pl.when(d == 0)
        def _():
            acc_ref[...] = jnp.zeros_like(acc_ref)

        acc_ref[...] += _dot_nt(dp_ref[...], w_ref[...])

        @pl.when(d == ND - 1)
        def _():
            dh_ref[...] = acc_ref[...]

    tm = _tm(R)
    return _call(body, jobs=jobs, name=name, grid=(R // tm, ND),
                 in_specs=[pl.BlockSpec((tm, Nb), lambda m, d: (m, d)), pl.BlockSpec((None, D, Nb), lambda m, d: (d, 0, 0))],
                 out_specs=pl.BlockSpec((tm, D), lambda m, d: (m, 0)), out_shape=_sds((R, D)),
                 scratch=[pltpu.VMEM((tm, D), F32)])(dP, WIN)


def _proj_out(mixb, WOUT, X1, mod2, ng, S, name):
    D = WOUT.shape[0]

    def body(m_ref, w_ref, x_ref, mod_ref, g_ref, z_ref, xn_ref, h_ref):
        z = _dot(m_ref[...], w_ref[...])
        z_ref[...] = z
        xn = x_ref[...] + mod_ref[5:6, :] * z
        xn_ref[...] = xn
        n = (xn * lax.rsqrt(jnp.mean(xn * xn, axis=-1, keepdims=True) + EPS)) * g_ref[...]
        h_ref[...] = (n * (1.0 + mod_ref[7:8, :]) + mod_ref[6:7, :]).astype(h_ref.dtype)

    return _call(body, name=name, grid=(S // TR,),
                 in_specs=[_row_spec(D), _full_spec((D, D)), _row_spec(D), pl.BlockSpec((None, 9, D), lambda i: (0, 0, 0)),
                           _full_spec((1, D))],
                 out_specs=[_row_spec(D), _row_spec(D), _row_spec(D)],
                 out_shape=[_sds((S, D)), _sds((S, D)), _sds((S, D), _MXU)])(mixb, WOUT, X1, mod2, ng)


def _dproj_out(dZb, WOUT, name):
    S, D = dZb.shape

    def body(dz_ref, w_ref, dm_ref):
        dm_ref[...] = _dot_nt(dz_ref[...], w_ref[...])

    return _call(body, name=name, grid=(S // TR,), in_specs=[_row_spec(D), _full_spec((D, D))],
                 out_specs=_row_spec(D), out_shape=_sds((S, D)))(dZb, WOUT)


def _pair_swap(t):
    lane = lax.broadcasted_iota(jnp.int32, t.shape, 1)
    return jnp.where(lane % 2 == 0, pltpu.roll(t, HD - 1, 1), pltpu.roll(t, 1, 1))


SM_SCALE = HD ** -0.5


def _qkv_prep(P, qg, kg, COS, SIN, D, KVW, name):
    R = P.shape[0]
    W = D + 2 * KVW
    nq, nk = D // HD, KVW // HD

    def body(p_ref, qg_ref, kg_ref, cos_ref, sin_ref, q_ref, k_ref, v_ref):
        cos, sin = cos_ref[...], sin_ref[...]

        def head(t, g):
            y = (t * lax.rsqrt(jnp.mean(t * t, axis=-1, keepdims=True) + EPS)) * g
            return y * cos + _pair_swap(y) * sin

        for h in range(nq):
            q_ref[:, h * HD:(h + 1) * HD] = (head(p_ref[:, h * HD:(h + 1) * HD], qg_ref[...]) * SM_SCALE).astype(q_ref.dtype)
        for h in range(nk):
            k_ref[:, h * HD:(h + 1) * HD] = head(p_ref[:, D + h * HD:D + (h + 1) * HD], kg_ref[...]).astype(k_ref.dtype)
        v_ref[...] = p_ref[:, D + KVW:W].astype(v_ref.dtype)

    return _call(body, name=name, grid=(R // TR,),
                 in_specs=[_row_spec(W), _full_spec((1, HD)), _full_spec((1, HD)), _row_spec(HD), _row_spec(HD)],
                 out_specs=[_row_spec(D), _row_spec(KVW), _row_spec(KVW)],
                 out_shape=[_sds((R, D), _MXU), _sds((R, KVW), _MXU), _sds((R, KVW), _MXU)])(P, qg, kg, COS, SIN)


def _qkv_bwd(P, dq, dk, dv, qg, kg, COS, SIN, dP, D, KVW, nx, name):
    R, INW = P.shape
    W = D + 2 * KVW
    nq, nk = D // HD, KVW // HD

    def body(p_ref, dq_ref, dk_ref, dv_ref, qg_ref, kg_ref, cos_ref, sin_ref, dp_in, dp_ref, dqg_ref, dkg_ref):
        i = pl.program_id(0)
        cos, sin = cos_ref[...], sin_ref[...]

        @pl.when(i == 0)
        def _():
            dqg_ref[...] = jnp.zeros_like(dqg_ref)
            dkg_ref[...] = jnp.zeros_like(dkg_ref)

        def head_bwd(t, g, dout):
            r = lax.rsqrt(jnp.mean(t * t, axis=-1, keepdims=True) + EPS)
            n = t * r
            dy = dout * cos + _pair_swap(dout * sin)
            dn = dy * g
            return r * (dn - n * jnp.mean(dn * n, axis=-1, keepdims=True)), jnp.sum(dy * n, axis=0, keepdims=True)

        dqg = jnp.zeros((1, HD), F32)
        for h in range(nq):
            sl = slice(h * HD, (h + 1) * HD)
            dt, dg = head_bwd(p_ref[:, sl], qg_ref[...], jnp.where(i < nx, dq_ref[:, sl] * SM_SCALE, 0.0))
            dp_ref[:, sl] = dt.astype(dp_ref.dtype)
            dqg += dg
        dkg = jnp.zeros((1, HD), F32)
        for h in range(nk):
            sl = slice(h * HD, (h + 1) * HD)
            dt, dg = head_bwd(p_ref[:, D + h * HD:D + (h + 1) * HD], kg_ref[...], dk_ref[:, sl])
            dp_ref[:, D + h * HD:D + (h + 1) * HD] = dt.astype(dp_ref.dtype)
            dkg += dg
        dqg_ref[...] += dqg
        dkg_ref[...] += dkg
        dp_ref[:, D + KVW:W] = dv_ref[...].astype(dp_ref.dtype)

    return _call(body, name=name, grid=(R // TR,),
                 in_specs=[_row_spec(W), _row_spec(D, nx - 1), _row_spec(KVW), _row_spec(KVW), _full_spec((1, HD)),
                           _full_spec((1, HD)), _row_spec(HD), _row_spec(HD), pl.BlockSpec(memory_space=pl.ANY)],
                 out_specs=[_row_spec(W), _full_spec((1, HD)), _full_spec((1, HD))],
                 out_shape=[_sds((R, INW), _MXU), _sds((1, HD)), _sds((1, HD))],
                 aliases={8: 0})(P, dq, dk, dv, qg, kg, COS, SIN, dP)


def _stack_heads(ref, G, dtype=None):
    parts = [ref[:, g * HD:(g + 1) * HD] for g in range(G)]
    out = jnp.concatenate(parts, axis=0)
    return out if dtype is None else out.astype(dtype)


_KEY_CHUNKS = 4


def _key_chunks(R):
    unit = 256 if R % 256 == 0 else 16
    nt = R // unit
    n = min(_KEY_CHUNKS, nt)
    bounds = [0]
    for i in range(n):
        bounds.append(bounds[-1] + (nt // n + (1 if i < nt % n else 0)) * unit)
    return bounds


def _attn_fwd(q, k, v, S, G, name, jobs=()):
    R, KVW = k.shape
    D = q.shape[1]
    Kh = KVW // HD
    tq = 128
    kb = _key_chunks(R)

    def body(q_ref, k_ref, v_ref, o_ref, lse_ref):
        qs = _stack_heads(q_ref, G)
        m = l = acc = None
        for c in range(len(kb) - 1):
            s = _dot_nt(qs, k_ref[kb[c]:kb[c + 1], :])
            mc = jnp.max(s, axis=-1, keepdims=True)
            m_new = mc if c == 0 else jnp.maximum(m, mc)
            p = jnp.exp(s - m_new)
            ps = jnp.sum(p, axis=-1, keepdims=True)
            pv = _dot(p, v_ref[kb[c]:kb[c + 1], :])
            if c == 0:
                l, acc = ps, pv
            else:
                alpha = jnp.exp(m - m_new)
                l = alpha * l + ps
                acc = alpha * acc + pv
            m = m_new
        o = acc / l
        lse = m + jnp.log(l)
        for g in range(G):
            o_ref[:, g * HD:(g + 1) * HD] = o[g * tq:(g + 1) * tq, :]
            lse_ref[g] = jnp.broadcast_to(lse[g * tq:(g + 1) * tq, :], (tq, HD))

    return _call(body, jobs=jobs, name=name, grid=(Kh, S // tq),
                 in_specs=[pl.BlockSpec((tq, G * HD), lambda h, i: (i, h)), pl.BlockSpec((R, HD), lambda h, i: (0, h)),
                           pl.BlockSpec((R, HD), lambda h, i: (0, h))],
                 out_specs=[pl.BlockSpec((tq, G * HD), lambda h, i: (i, h)),
                            pl.BlockSpec((None, G, tq, HD), lambda h, i: (h, 0, i, 0))],
                 out_shape=[_sds((S, D)), _sds((Kh, G, S, HD))])(q, k, v)


def _attn_bwd(q, k, v, O, LSE, dOb, S, G, name, jobs=()):
    R, KVW = k.shape
    D = q.shape[1]
    Kh = KVW // HD
    tq = 128
    kb = _key_chunks(R)

    def body(q_ref, k_ref, v_ref, o_ref, lse_ref, do_ref, dq_ref, dk_ref, dv_ref):
        @pl.when(pl.program_id(1) == 0)
        def _():
            dk_ref[...] = jnp.zeros_like(dk_ref)
            dv_ref[...] = jnp.zeros_like(dv_ref)

        qs = _stack_heads(q_ref, G)
        do = _stack_heads(do_ref, G)
        o = _stack_heads(o_ref, G)
        lse = jnp.concatenate([lse_ref[g][:, 0:1] for g in range(G)], axis=0)
        delta = jnp.sum(do.astype(F32) * o, axis=-1, keepdims=True)
        dq = None
        for c in range(len(kb) - 1):
            rows = slice(kb[c], kb[c + 1])
            kk = k_ref[rows, :]
            p = jnp.exp(_dot_nt(qs, kk) - lse)
            dp = _dot_nt(do, v_ref[rows, :])
            ds = (p * (dp - delta)).astype(_MXU)
            dqc = _dot(ds, kk)
            dq = dqc if c == 0 else dq + dqc
            dk_ref[rows, :] += _dot_tn(ds, qs)
            dv_ref[rows, :] += _dot_tn(p, do)
        for g in range(G):
            dq_ref[:, g * HD:(g + 1) * HD] = dq[g * tq:(g + 1) * tq, :]

    qspec = pl.BlockSpec((tq, G * HD), lambda h, i: (i, h))
    kspec = pl.BlockSpec((R, HD), lambda h, i: (0, h))
    return _call(body, jobs=jobs, name=name, grid=(Kh, S // tq),
                 in_specs=[qspec, kspec, kspec, qspec, pl.BlockSpec((None, G, tq, HD), lambda h, i: (h, 0, i, 0)), qspec],
                 out_specs=[qspec, kspec, kspec],
                 out_shape=[_sds((S, D)), _sds((R, KVW)), _sds((R, KVW))])(q, k, v, O, LSE, dOb)


def _halo_specs(R, CB, col0):
    nt8 = TR // 8
    return [pl.BlockSpec((8, CB), lambda h, i: (jnp.maximum(i * nt8 - 1, 0), col0 + h)),
            pl.BlockSpec((TR, CB), lambda h, i: (i, col0 + h)),
            pl.BlockSpec((8, CB), lambda h, i: (jnp.minimum((i + 1) * nt8, R // 8 - 1), col0 + h))]


def _seq_pos(i, S, R, CB):
    t = i * TR - 8 + lax.broadcasted_iota(jnp.int32, (TR + 16, CB), 0)
    start = jnp.where(t >= S, S, 0)
    end = jnp.where(t >= S, R, S)
    return t - start, end - t


def _shift(cat, by):
    return pltpu.roll(cat, by % cat.shape[0], 0)


def _gate_mats(xcb, w_ref, dirn, nb):
    return jnp.concatenate([_dot(xcb[:, b * HD:(b + 1) * HD], w_ref[dirn, b]) for b in range(nb)], axis=1)


def _lru_gates_fwd(P, conv_w, conv_b, WA, WX, ba, bx, lam, S, D, col0, name):
    R = P.shape[0]
    CB = D // 2
    nb = CB // HD

    def body(xp_ref, x_ref, xn_ref, cw_ref, cb_ref, wa_ref, wx_ref, ba_ref, bx_ref, lam_ref,
             xc_ref, af_ref, uf_ref, ab_ref, ub_ref):
        i = pl.program_id(1)
        cat = jnp.concatenate([xp_ref[...], x_ref[...], xn_ref[...]], axis=0)
        from_start, to_end = _seq_pos(i, S, R, CB)
        conv = (cb_ref[...] + cw_ref[2:3, :] * cat
                + cw_ref[0:1, :] * jnp.where(from_start >= 2, _shift(cat, 2), 0.0)
                + cw_ref[1:2, :] * jnp.where(from_start >= 1, _shift(cat, 1), 0.0)
                + cw_ref[3:4, :] * jnp.where(to_end >= 2, _shift(cat, -1), 0.0))
        xc = conv[8:8 + TR, :]
        xc_ref[...] = xc
        xcb = xc.astype(_MXU)
        for dirn, (a_ref, u_ref) in enumerate(((af_ref, uf_ref), (ab_ref, ub_ref))):
            ra = _sigmoid(_gate_mats(xcb, wa_ref, dirn, nb) + ba_ref[dirn:dirn + 1, :])
            ia = _sigmoid(_gate_mats(xcb, wx_ref, dirn, nb) + bx_ref[dirn:dirn + 1, :])
            nl = -lam_ref[dirn:dirn + 1, :]
            sp = jnp.maximum(nl, 0.0) + jnp.log(1.0 + jnp.exp(-jnp.abs(nl)))
            la = (-LRU_C) * ra * sp
            a = jnp.exp(la)
            a_ref[...] = a
            u_ref[...] = jnp.sqrt(1.0 - a * a) * (ia * xc)

    def par(r):
        return pl.BlockSpec((r, CB), lambda h, i: (0, h))

    wspec = pl.BlockSpec((2, nb, HD, HD), lambda h, i: (0, h, 0, 0))
    out = pl.BlockSpec((TR, CB), lambda h, i: (i, h))
    return _call(body, name=name, grid=(2, R // TR),
                 in_specs=_halo_specs(R, CB, col0) + [par(4), par(1), wspec, wspec, par(2), par(2), par(2)],
                 out_specs=[out] * 5, out_shape=[_sds((R, D))] * 5,
                 )(P, P, P, conv_w, conv_b, WA, WX, ba, bx, lam)


def _lru_gates_bwd(xc, hf, hb, gf, gb, WA, WX, ba, bx, lam, S, name):
    R, D = xc.shape
    CB = D // 2
    nb = CB // HD

    def body(xc_ref, hfq_ref, hf_ref, hfl_ref, hb_ref, hbn_ref, gf_ref, gb_ref, wa_ref, wx_ref, ba_ref, bx_ref, lam_ref,
             dxc_ref, dwa_ref, dwx_ref, dba_ref, dbx_ref, dlam_ref):
        @pl.when(pl.program_id(1) == 0)
        def _():
            for r in (dwa_ref, dwx_ref, dba_ref, dbx_ref, dlam_ref):
                r[...] = jnp.zeros_like(r)

        t = pl.program_id(1) * TR + lax.broadcasted_iota(jnp.int32, (TR, CB), 0)
        hfp = _shift(jnp.concatenate([hfq_ref[...], hf_ref[...]], axis=0), 1)[8:8 + TR, :]
        hfp = jnp.where(t == 0, hfl_ref[7:8, :], jnp.where(t == S, 0.0, hfp))
        hbp = _shift(jnp.concatenate([hb_ref[...], hbn_ref[...]], axis=0), -1)[0:TR, :]
        hbp = jnp.where(t == R - 1, 0.0, hbp)

        xc = xc_ref[...]
        xcb = xc.astype(_MXU)
        dxc = jnp.zeros_like(xc)
        for dirn, (hp, g_ref) in enumerate(((hfp, gf_ref), (hbp, gb_ref))):
            ra = _sigmoid(_gate_mats(xcb, wa_ref, dirn, nb) + ba_ref[dirn:dirn + 1, :])
            ia = _sigmoid(_gate_mats(xcb, wx_ref, dirn, nb) + bx_ref[dirn:dirn + 1, :])
            nl = -lam_ref[dirn:dirn + 1, :]
            sp = jnp.maximum(nl, 0.0) + jnp.log(1.0 + jnp.exp(-jnp.abs(nl)))
            la = (-LRU_C) * ra * sp
            a = jnp.exp(la)
            e2 = a * a
            rs = lax.rsqrt(1.0 - e2)
            s = (1.0 - e2) * rs
            du = g_ref[...]
            dla = du * hp * a - du * (ia * xc) * (e2 * rs)
            dxc += du * s * ia
            dza = (dla * (-LRU_C) * sp) * ra * (1.0 - ra)
            dzx = (du * s * xc) * ia * (1.0 - ia)
            dlam_ref[dirn:dirn + 1, :] += jnp.sum(dla * (LRU_C * ra) * _sigmoid(nl), axis=0, keepdims=True)
            dba_ref[dirn:dirn + 1, :] += jnp.sum(dza, axis=0, keepdims=True)
            dbx_ref[dirn:dirn + 1, :] += jnp.sum(dzx, axis=0, keepdims=True)
            dzab, dzxb = dza.astype(_MXU), dzx.astype(_MXU)
            parts = []
            for b in range(nb):
                sl = slice(b * HD, (b + 1) * HD)
                dwa_ref[dirn, b] += _dot_tn(xcb[:, sl], dzab[:, sl])
                dwx_ref[dirn, b] += _dot_tn(xcb[:, sl], dzxb[:, sl])
                parts.append(_dot_nt(dzab[:, sl], wa_ref[dirn, b]) + _dot_nt(dzxb[:, sl], wx_ref[dirn, b]))
            dxc += jnp.concatenate(parts, axis=1)
        dxc_ref[...] = dxc

    def par(r):
        return pl.BlockSpec((r, CB), lambda h, i: (0, h))

    wspec = pl.BlockSpec((2, nb, HD, HD), lambda h, i: (0, h, 0, 0))
    tile = pl.BlockSpec((TR, CB), lambda h, i: (i, h))
    before, _, after = _halo_specs(R, CB, 0)
    last = pl.BlockSpec((8, CB), lambda h, i: (R // 8 - 1, h))
    nbt = D // HD
    return _call(body, name=name, grid=(2, R // TR),
                 in_specs=[tile, before, tile, last, tile, after, tile, tile, wspec, wspec, par(2), par(2), par(2)],
                 out_specs=[tile, wspec, wspec, par(2), par(2), par(2)],
                 out_shape=[_sds((R, D)), _sds((2, nbt, HD, HD)), _sds((2, nbt, HD, HD)), _sds((2, D)), _sds((2, D)),
                            _sds((2, D))])(xc, hf, hf, hf, hb, hb, gf, gb, WA, WX, ba, bx, lam)


def _scan_rows(n_groups, step, init):
    return lax.fori_loop(0, n_groups, lambda gi, c: step(pl.multiple_of(gi * 8, 8), c), init)


def _lru_scan_fwd(af, uf, ab, ub, S, name):
    R, D = af.shape
    W = min(SCAN_W, D)
    nm, nx = R // TR, S // TR
    nc = nm - nx
    ng = TR // 8

    def body(af_ref, uf_ref, ab_ref, ub_ref, hf_ref, hb_ref, cf_ref, cb_ref):
        @pl.when(pl.program_id(1) == 0)
        def _():
            cf_ref[...] = jnp.zeros_like(cf_ref)
            cb_ref[...] = jnp.zeros_like(cb_ref)

        def step(base, carry):
            hf, hb = carry
            baseb = pl.multiple_of(TR - 8 - base, 8)
            for r in range(8):
                tf, tb = base + r, baseb + 7 - r
                hf = af_ref[pl.ds(tf, 1), :] * hf + uf_ref[pl.ds(tf, 1), :]
                hf_ref[pl.ds(tf, 1), :] = hf
                hb = ab_ref[pl.ds(tb, 1), :] * hb + ub_ref[pl.ds(tb, 1), :]
                hb_ref[pl.ds(tb, 1), :] = hb
            return hf, hb

        hf, hb = _scan_rows(ng, step, (cf_ref[0:1, :], cb_ref[0:1, :]))
        cf_ref[0:1, :] = hf
        cb_ref[0:1, :] = hb

    fmap = lambda j, s: (jnp.where(s < nc, nx + s, s - nc), j)
    bmap = lambda j, s: (nm - 1 - s, j)
    fs, bs = pl.BlockSpec((TR, W), fmap), pl.BlockSpec((TR, W), bmap)
    return _call(body, name=name, grid=(D // W, nm), in_specs=[fs, fs, bs, bs], out_specs=[fs, bs],
                 out_shape=[_sds((R, D))] * 2, scratch=[pltpu.VMEM((8, W), F32), pltpu.VMEM((8, W), F32)])(af, uf, ab, ub)


def _lru_scan_bwd(af, ab, dhs, S, name):
    R, D = af.shape
    W = min(SCAN_W, D)
    nm, nx = R // TR, S // TR
    ng = TR // 8

    def body(af_ref, dhf_ref, ab_ref, dhb_ref, gf_ref, gb_ref, cf_ref, cb_ref):
        @pl.when(pl.program_id(1) == 0)
        def _():
            cf_ref[...] = jnp.zeros_like(cf_ref)
            cb_ref[...] = jnp.zeros_like(cb_ref)

        def step(base, carry):
            cf, cb = carry
            based = pl.multiple_of(TR - 8 - base, 8)
            for r in range(8):
                tf, tb = based + 7 - r, base + r
                g = dhf_ref[pl.ds(tf, 1), :] + cf
                gf_ref[pl.ds(tf, 1), :] = g
                cf = af_ref[pl.ds(tf, 1), :] * g
                g = dhb_ref[pl.ds(tb, 1), :] + cb
                gb_ref[pl.ds(tb, 1), :] = g
                cb = ab_ref[pl.ds(tb, 1), :] * g
            return cf, cb

        cf, cb = _scan_rows(ng, step, (cf_ref[0:1, :], cb_ref[0:1, :]))
        cf_ref[0:1, :] = cf
        cb_ref[0:1, :] = cb

    fmap = lambda j, s: (jnp.where(s < nx, nx - 1 - s, nm - 1 - (s - nx)), j)
    bmap = lambda j, s: (s, j)
    fs, bs = pl.BlockSpec((TR, W), fmap), pl.BlockSpec((TR, W), bmap)
    return _call(body, name=name, grid=(D // W, nm), in_specs=[fs, fs, bs, bs], out_specs=[fs, bs],
                 out_shape=[_sds((R, D))] * 2, scratch=[pltpu.VMEM((8, W), F32), pltpu.VMEM((8, W), F32)])(af, dhs, ab, dhs)


def _merge_fwd(P, hf, hb, O, S, D, col_lg, name):
    R = P.shape[0]
    CB = D // 2
    nx = S // TR

    def body(lg_ref, ga_ref, gl_ref, hf_ref, hb_ref, o_ref, mix_ref):
        ge, _ = _gelu_and_grad(lg_ref[...])
        lru = (hf_ref[...] + hb_ref[...]) * ge
        mix_ref[...] = (_sigmoid(ga_ref[...]) * o_ref[...] + _sigmoid(gl_ref[...]) * lru).astype(mix_ref.dtype)

    def col(c0):
        return pl.BlockSpec((TR, CB), lambda h, i: (i, c0 + h))

    return _call(body, name=name, grid=(2, R // TR),
                 in_specs=[col(col_lg), col(col_lg + 2), col(col_lg + 4), col(0), col(0),
                           pl.BlockSpec((TR, CB), lambda h, i: (jnp.minimum(i, nx - 1), h))],
                 out_specs=col(0), out_shape=_sds((R, D), _MXU))(P, P, P, hf, hb, O)


def _merge_bwd(dmix, P, hf, hb, O, S, D, col_lg, name, jobs=()):
    R = P.shape[0]
    CB = D // 2
    nx = S // TR

    def body(dm_ref, lg_ref, ga_ref, gl_ref, hf_ref, hb_ref, o_ref, do_ref, dhs_ref, dp_ref, stash, sems):
        h, i = pl.program_id(0), pl.program_id(1)
        dm = jnp.where(i < nx, dm_ref[...], 0.0)
        sa, sl = _sigmoid(ga_ref[...]), _sigmoid(gl_ref[...])
        ge, dge = _gelu_and_grad(lg_ref[...])
        hs = hf_ref[...] + hb_ref[...]
        dl = dm * sl
        do_ref[...] = (dm * sa).astype(do_ref.dtype)
        dhs_ref[...] = dl * ge
        stash[0] = (dl * hs * dge).astype(stash.dtype)
        stash[1] = (dm * o_ref[...] * sa * (1.0 - sa)).astype(stash.dtype)
        stash[2] = (dm * (hs * ge) * sl * (1.0 - sl)).astype(stash.dtype)
        rows = pl.ds(pl.multiple_of(i * TR, TR), TR)
        copies = [pltpu.make_async_copy(stash.at[sec], dp_ref.at[rows, pl.ds(pl.multiple_of((col_lg + 2 * sec + h) * CB, CB), CB)],
                                        sems.at[sec]) for sec in range(3)]
        for cp in copies:
            cp.start()
        for cp in copies:
            cp.wait()

    def col(c0):
        return pl.BlockSpec((TR, CB), lambda h, i: (i, c0 + h))

    xrow = pl.BlockSpec((TR, CB), lambda h, i: (jnp.minimum(i, nx - 1), h))
    return _call(body, jobs=jobs, name=name, grid=(2, R // TR),
                 in_specs=[xrow, col(col_lg), col(col_lg + 2), col(col_lg + 4), col(0), col(0), xrow],
                 out_specs=[col(0), col(0), pl.BlockSpec(memory_space=pl.ANY)],
                 out_shape=[_sds((R, D), _MXU), _sds((R, D)), _sds(P.shape, _MXU)],
                 scratch=[pltpu.VMEM((3, TR, CB), _MXU), pltpu.SemaphoreType.DMA((3,))])(dmix, P, P, P, hf, hb, O)


def _conv_bwd(dxc, P, conv_w, dP, S, D, col0, name, jobs=()):
    R = P.shape[0]
    CB = D // 2

    def body(dp_, d_ref, dn_, xp_ref, x_ref, xn_ref, cw_ref, dp_in, dpo_ref, dcw_ref, dcb_ref):
        i = pl.program_id(1)

        @pl.when(i == 0)
        def _():
            dcw_ref[...] = jnp.zeros_like(dcw_ref)
            dcb_ref[...] = jnp.zeros_like(dcb_ref)

        d = d_ref[...]
        catd = jnp.concatenate([dp_[...], d, dn_[...]], axis=0)
        catx = jnp.concatenate([xp_ref[...], x_ref[...], xn_ref[...]], axis=0)
        from_start, to_end = _seq_pos(i, S, R, CB)
        dxl = (cw_ref[2:3, :] * catd
               + cw_ref[0:1, :] * jnp.where(to_end >= 3, _shift(catd, -2), 0.0)
               + cw_ref[1:2, :] * jnp.where(to_end >= 2, _shift(catd, -1), 0.0)
               + cw_ref[3:4, :] * jnp.where(from_start >= 1, _shift(catd, 1), 0.0))
        dpo_ref[...] = dxl[8:8 + TR, :].astype(dpo_ref.dtype)
        taps = (jnp.where(from_start >= 2, _shift(catx, 2), 0.0), jnp.where(from_start >= 1, _shift(catx, 1), 0.0),
                catx, jnp.where(to_end >= 2, _shift(catx, -1), 0.0))
        for kk in range(4):
            dcw_ref[kk:kk + 1, :] += jnp.sum(d * taps[kk][8:8 + TR, :], axis=0, keepdims=True)
        dcb_ref[...] += jnp.sum(d, axis=0, keepdims=True)

    return _call(body, jobs=jobs, name=name, grid=(2, R // TR),
                 in_specs=_halo_specs(R, CB, 0) + _halo_specs(R, CB, col0)
                 + [pl.BlockSpec((4, CB), lambda h, i: (0, h)), pl.BlockSpec(memory_space=pl.ANY)],
                 out_specs=[pl.BlockSpec((TR, CB), lambda h, i: (i, col0 + h)), pl.BlockSpec((4, CB), lambda h, i: (0, h)),
                            pl.BlockSpec((1, CB), lambda h, i: (0, h))],
                 out_shape=[_sds(dP.shape, dP.dtype), _sds((4, D)), _sds((1, D))],
                 aliases={7: 0})(dxc, dxc, dxc, P, P, P, conv_w, dP)


def _rope_tables(S, C):
    t = jnp.arange(S, dtype=jnp.int32)
    row = (t // GRID_W).astype(F32)
    col = (t % GRID_W).astype(F32)
    axis_dims = HD // 2
    freqs = ROPE_THETA ** (-jnp.arange(0, axis_dims, 2, dtype=F32) / axis_dims)
    ang = jnp.concatenate([row[:, None] * freqs, col[:, None] * freqs], axis=-1)
    cos = jnp.repeat(jnp.cos(ang), 2, axis=-1)
    sin = jnp.repeat(jnp.sin(ang), 2, axis=-1) * jnp.tile(jnp.array([-1.0, 1.0], F32), HD // 2)
    return (jnp.concatenate([cos, jnp.ones((C, HD), F32)], axis=0),
            jnp.concatenate([sin, jnp.zeros((C, HD), F32)], axis=0))


def _local_step(x, ctx, target, modx, modc, ng, shards, qg, kg, conv_w, conv_b, WA, WX, ba, bx, lam, fg, idx, opt):
    S, D = x.shape
    C = ctx.shape[0]
    R = S + C
    nx = S // TR
    Nb = shards['w_in'].shape[-1]
    Fb = shards['wd0'].shape[1]
    KVW = (ND * Nb - 5 * D) // 2
    G = D // KVW
    CB = D // 2
    col_lx = (D + 2 * KVW) // CB
    assert S % TR == 0 and C % TR == 0 and (D + 2 * KVW) % CB == 0 and CB % HD == 0
    mod2 = jnp.stack([modx, modc])
    X0 = jnp.concatenate([x, ctx], axis=0)
    COS, SIN = _rope_tables(S, C)
    ng0, ng1, ng2 = ng[0:1], ng[1:2], ng[2:3]
    ag = lambda n: _ag_job(shards[n])
    sib = lambda Gp: _rs_sibling_job(Gp.reshape(ND, -1, Gp.shape[-1]))
    add = lambda Gp, bufA, tag: _rs_add(Gp.reshape(ND, -1, Gp.shape[-1]), bufA, idx, f"{tag}_rs_add")
    out = {}

    def tn_specs(rows):
        tm = _tm(rows)
        return pl.BlockSpec((None, tm, Fb), lambda d, r: (d, r, 0)), pl.BlockSpec((tm, D), lambda d, r: (r, 0))

    wd_spec = pl.BlockSpec((None, None, Fb, D), lambda d, r: (d, 0, 0, 0))

    ((WG0,),) = _comm_call([ag('wg0')], "ag_wg0")
    H1 = _norm_mod_fwd(X0, ng0, mod2, 0, R, nx, "ffn1_norm")
    G1, ((WU0,),) = _ffn_gate(H1, WG0, R, "ffn1_gate", jobs=[ag('wu0')])
    (U1, A1), ((WD0,),) = _ffn_up(H1, None, WU0, R, "ffn1_up", jobs=[ag('wd0')], G=G1)
    (Y1, X1, H2), ((WIN,),) = _ffn_down(A1, WD0, 0, X0, mod2, 0, R, S, "ffn1_down", jobs=[ag('w_in')], next_norm=(ng1, 1))
    P, ((WOUT,), (WG1,)) = _proj_in(H2, WIN, "proj_in", jobs=[ag('w_out'), ag('wg1')])
    q, k, v = _qkv_prep(P, qg, kg, COS, SIN, D, KVW, "qkv_prep")
    (O, LSE), ((WU1,), (WD1,)) = _attn_fwd(q, k, v, S, G, "attn_fwd", jobs=[ag('wu1'), ag('wd1')])
    WOUT = WOUT.reshape(D, D)
    xc, af, uf, ab, ub = _lru_gates_fwd(P, conv_w, conv_b, WA, WX, ba, bx, lam, S, D, col_lx, "lru_gates")
    hf, hb = _lru_scan_fwd(af, uf, ab, ub, S, "lru_scan")
    mixb = _merge_fwd(P, hf, hb, O, S, D, col_lx + 2, "merge")
    Z, X2, H3 = _proj_out(mixb, WOUT, X1, mod2, ng2, S, "proj_out")
    G3, U3, A3 = _ffn_up(H3, WG1, WU1, S, "ffn2_up")
    loss, dX3, dY3b, dg3, dfg = _ffn_down_loss(A3, WD1, X2, mod2, 2, fg, target, "ffn2_down_loss")
    dg3 = dg3[None]

    dG3, dU3 = _ffn_dact(dY3b, WD1, 0, G3, U3, S, "ffn2_dact")
    blk, row = tn_specs(S)
    dWD1 = _mm_tn(A3, blk, dY3b, row, (ND, 1, Fb, D), wd_spec, S, "ffn2_dwd")
    dWG1 = _mm_tn(dG3, blk, H3, row, (ND, 1, Fb, D), wd_spec, S, "ffn2_dwg")
    dWU1 = _mm_tn(dU3, blk, H3, row, (ND, 1, Fb, D), wd_spec, S, "ffn2_dwu")
    dH3, ((a_wd1,), (a_wg1,), (a_wu1,)) = _ffn_dh(dG3, dU3, WG1, WU1, S, "ffn2_dh", jobs=[sib(dWD1), sib(dWG1), sib(dWU1)])
    T_wd1, own_wd1 = add(dWD1, a_wd1, "wd1")
    T_wg1, own_wg1 = add(dWG1, a_wg1, "wg1")
    T_wu1, own_wu1 = add(dWU1, a_wu1, "wu1")
    dX2, dsh3, dsc3, dng2, dZb, dg2 = _norm_mod_bwd(X2, dH3, dX3, ng2, mod2, 2, S, nx, nx, "ffn2_dnorm", branch=(Z, 1, 1.0))

    dmix = _dproj_out(dZb, WOUT, "dproj_out")
    dWOUT = _mm_tn(mixb, pl.BlockSpec((_tm(S), D // ND), lambda d, r: (r, d)), dZb, pl.BlockSpec((_tm(S), D), lambda d, r: (r, 0)),
                   (ND, D // ND, D), pl.BlockSpec((None, D // ND, D), lambda d, r: (d, 0, 0)), S, "dw_out")
    (dOb, dhs, dP), ((a_wout,),) = _merge_bwd(dmix, P, hf, hb, O, S, D, col_lx + 2, "merge_bwd", jobs=[sib(dWOUT)])
    T_wout, own_wout = add(dWOUT, a_wout, "w_out")
    gf, gb = _lru_scan_bwd(af, ab, dhs, S, "lru_scan_bwd")
    dxc, dWA, dWX, dba, dbx, dlam = _lru_gates_bwd(xc, hf, hb, gf, gb, WA, WX, ba, bx, lam, S, "lru_gates_bwd")
    dLW = jnp.stack([dWA, dWX]).reshape(ND, -1, HD)
    (dP, dconv_w, dconv_b), ((a_lw,),) = _conv_bwd(dxc, P, conv_w, dP, S, D, col_lx, "conv_bwd", jobs=[sib(dLW)])
    T_lw, own_lw = add(dLW, a_lw, "lru_w")
    (dq, dk, dv), ((b_wd1,), (b_wg1,), (b_wu1,), (b_wout,), (b_lw,)) = _attn_bwd(
        q, k, v, O, LSE, dOb, S, G, "attn_bwd",
        jobs=[_rs_chips_job(T_wd1), _rs_chips_job(T_wg1), _rs_chips_job(T_wu1), _rs_chips_job(T_wout), _rs_chips_job(T_lw)])
    fin_wd = _rs_finish(own_wd1, b_wd1, "wd1_rs_finish", opt['ffn_wd'], 1, 2)
    fin_wg = _rs_finish(own_wg1, b_wg1, "wg1_rs_finish", opt['ffn_wg'], 1, 2)
    fin_wu = _rs_finish(own_wu1, b_wu1, "wu1_rs_finish", opt['ffn_wu'], 1, 2)
    out['w_out'] = _rs_finish(own_wout, b_wout, "w_out_rs_finish", opt['w_out'])
    (lw_sum,) = _rs_finish(own_lw, b_lw, "lru_w_rs_finish")
    dP, dqg, dkg = _qkv_bwd(P, dq, dk, dv, qg, kg, COS, SIN, dP, D, KVW, nx, "qkv_bwd")
    dH2, ((lw_full,),) = _dproj_in(dP, WIN, "dproj_in", jobs=[_ag_job(lw_sum)])
    dWIN = _mm_tn(H2, pl.BlockSpec((_tm(R), D), lambda d, r: (r, 0)), dP, pl.BlockSpec((_tm(R), Nb), lambda d, r: (r, d)),
                  (ND, D, Nb), pl.BlockSpec((None, D, Nb), lambda d, r: (d, 0, 0)), R, "dw_in")
    dX1, dsh2, dsc2, dng1, dY1b, dg1 = _norm_mod_bwd(X1, dH2, dX2, ng1, mod2, 1, R, nx, nx, "mix_dnorm", branch=(Y1, 0, 0.5))

    (dG1, dU1), ((a_win,),) = _ffn_dact(dY1b, WD0, 0, G1, U1, R, "ffn1_dact", jobs=[sib(dWIN)])
    T_win, own_win = add(dWIN, a_win, "w_in")
    blk, row = tn_specs(R)
    dWD0, ((b_win01,),) = _mm_tn(A1, blk, dY1b, row, (ND, 1, Fb, D), wd_spec, R, "ffn1_dwd", jobs=[_rs_chips_job(T_win, (0, 1))])
    dWG0, ((b_win2,), (a_wd0,)) = _mm_tn(dG1, blk, H1, row, (ND, 1, Fb, D), wd_spec, R, "ffn1_dwg",
                                         jobs=[_rs_chips_job(T_win, (2,)), sib(dWD0)])
    out['w_in'] = _rs_finish(own_win, [(b_win01, 0), (b_win01, 1), (b_win2, 0)], "w_in_rs_finish", opt['w_in'])
    T_wd0, own_wd0 = add(dWD0, a_wd0, "wd0")
    dWU0, ((a_wg0,), (b_wd0,)) = _mm_tn(dU1, blk, H1, row, (ND, 1, Fb, D), wd_spec, R, "ffn1_dwu",
                                        jobs=[sib(dWG0), _rs_chips_job(T_wd0)])
    T_wg0, own_wg0 = add(dWG0, a_wg0, "wg0")
    out['ffn_wd'] = _rs_finish(own_wd0, b_wd0, "wd0_rs_finish", opt['ffn_wd'], 0, 2, fin_wd)
    dH1, ((a_wu0,), (b_wg0,)) = _ffn_dh(dG1, dU1, WG0, WU0, R, "ffn1_dh", jobs=[sib(dWU0), _rs_chips_job(T_wg0)])
    T_wu0, own_wu0 = add(dWU0, a_wu0, "wu0")
    out['ffn_wg'] = _rs_finish(own_wg0, b_wg0, "wg0_rs_finish", opt['ffn_wg'], 0, 2, fin_wg)
    (dX0, dsh1, dsc1, dng0), ((b_wu0,),) = _norm_mod_bwd(
        X0, dH1, dX1, ng0, mod2, 0, R, nx, R // TR, "ffn1_dnorm", jobs=[_rs_chips_job(T_wu0, (0, 1))])
    out['wu0_pending'] = (T_wu0, own_wu0, b_wu0, fin_wu)
    out['lru_w'] = lw_full

    zero = jnp.zeros((1, D), F32)
    dmodx = jnp.concatenate([dsh1[0], dsc1[0], dg1[0], dsh2[0], dsc2[0], dg2[0], dsh3[0], dsc3[0], dg3[0]], axis=0)
    dmodc = jnp.concatenate([dsh1[1], dsc1[1], dg1[1], dsh2[1], dsc2[1], zero, zero, zero, zero], axis=0)
    out.update(loss=loss, grad_x=dX0[:S], dmodx=dmodx, dmodc=dmodc, norm_g=jnp.concatenate([dng0, dng1, dng2], axis=0),
               q_norm_g=dqg, k_norm_g=dkg, conv_w=dconv_w, conv_b=dconv_b, lru_ba=dba, lru_bx=dbx,
               lru_lambda=dlam, final_norm_g=dfg)
    return out


def _mesh_pos():
    return lax.axis_index("x"), lax.axis_index("y"), lax.axis_index("c")


def _all_gather(xb, name, in_vmem=False):
    space = pltpu.VMEM if in_vmem else pl.ANY

    def body(x_ref, out_ref, send_sems, recv_sems, local_sem):
        x, y, c = _mesh_pos()
        me, sibling = (x, y, c), (x, y, 1 - c)
        chips = [(1 - x, y), (x, 1 - y), (1 - x, 1 - y)]

        def slot(px, py, pc):
            return out_ref.at[4 * px + 2 * py + pc]

        def copy(k, block, to, src=None):
            return pltpu.make_async_remote_copy(
                src_ref=slot(*block) if src is None else src, dst_ref=slot(*block),
                send_sem=send_sems.at[k], recv_sem=recv_sems.at[k], device_id=to, device_id_type=MESH)

        mine = pltpu.make_async_copy(x_ref, slot(*me), local_sem)
        mine.start()
        first = [copy(0, me, sibling, src=x_ref)]
        first += [copy(1 + j, me, (*chip, c), src=x_ref) for j, chip in enumerate(chips)]
        for cp in first:
            cp.start()
        passed = [copy(4 + j, (*chip, c), sibling) for j, chip in enumerate(chips)]
        for j, chip in enumerate(chips):
            copy(1 + j, (*chip, c), me).wait_recv()
            passed[j].start()
        copy(0, sibling, me).wait_recv()
        for j, chip in enumerate(chips):
            copy(4 + j, (*chip, 1 - c), me).wait_recv()
        for cp in first + passed:
            cp.wait_send()
        mine.wait()

    return pl.pallas_call(
        body, name=name, out_shape=_sds((ND,) + xb.shape, xb.dtype),
        in_specs=[pl.BlockSpec(memory_space=space)], out_specs=pl.BlockSpec(memory_space=space),
        scratch_shapes=[pltpu.SemaphoreType.DMA((7,)), pltpu.SemaphoreType.DMA((7,)), pltpu.SemaphoreType.DMA(())])(xb)


def _ag_job(xb):
    def env(ins, outs, sems):
        x_ref, out_ref = ins[0], outs[0]
        send_sems, recv_sems, local_sem = sems
        x, y, c = _mesh_pos()

        def slot(px, py, pc):
            return out_ref.at[4 * px + 2 * py + pc]

        def copy(k, block, to, src=None):
            return pltpu.make_async_remote_copy(
                src_ref=slot(*block) if src is None else src, dst_ref=slot(*block),
                send_sem=send_sems.at[k], recv_sem=recv_sems.at[k], device_id=to, device_id_type=MESH)

        return x_ref, local_sem, slot, copy, (x, y, c)

    def start(ins, outs, sems):
        x_ref, local_sem, slot, copy, (x, y, c) = env(ins, outs, sems)
        me = (x, y, c)
        pltpu.make_async_copy(x_ref, slot(*me), local_sem).start()
        copy(0, me, (x, y, 1 - c), src=x_ref).start()
        copy(1, me, (1 - x, y, c), src=x_ref).start()
        copy(2, me, (x, 1 - y, c), src=x_ref).start()

    def relay(ins, outs, sems):
        _, _, _, copy, (x, y, c) = env(ins, outs, sems)
        xn, yn = (1 - x, y, c), (x, 1 - y, c)
        copy(1, xn, (x, y, c)).wait_recv()
        copy(2, yn, (x, y, c)).wait_recv()
        src = (jnp.where(c == 0, x, 1 - x), jnp.where(c == 0, 1 - y, y), c)
        dst = (jnp.where(c == 0, 1 - x, x), jnp.where(c == 0, y, 1 - y), c)
        copy(3, src, dst).start()
        copy(4, xn, (x, y, 1 - c)).start()
        copy(5, yn, (x, y, 1 - c)).start()

    def relay2(ins, outs, sems):
        _, _, _, copy, (x, y, c) = env(ins, outs, sems)
        dg = (1 - x, 1 - y, c)
        copy(3, dg, (x, y, c)).wait_recv()
        copy(6, dg, (x, y, 1 - c)).start()

    def finish(ins, outs, sems):
        x_ref, local_sem, slot, copy, (x, y, c) = env(ins, outs, sems)
        me, sib = (x, y, c), (x, y, 1 - c)
        copy(0, sib, me).wait_recv()
        for k, chip in ((4, (1 - x, y)), (5, (x, 1 - y)), (6, (1 - x, 1 - y))):
            copy(k, (*chip, 1 - c), me).wait_recv()
        for k in range(7):
            copy(k, me, me).wait_send()
        pltpu.make_async_copy(x_ref, slot(*me), local_sem).wait()

    return _Job([xb], [_sds((ND,) + xb.shape, xb.dtype)],
                [pltpu.SemaphoreType.DMA((7,)), pltpu.SemaphoreType.DMA((7,)), pltpu.SemaphoreType.DMA(())],
                start, finish, relay, relay2)


def _rs_sibling_job(Gp):
    def copies(ins, outs, sems):
        x, y, c = _mesh_pos()
        return [pltpu.make_async_remote_copy(
            src_ref=ins[0].at[2 * k + (1 - c)], dst_ref=outs[0].at[k], send_sem=sems[0].at[k], recv_sem=sems[1].at[k],
            device_id=(x, y, 1 - c), device_id_type=MESH) for k in range(4)]

    def start(ins, outs, sems):
        for cp in copies(ins, outs, sems):
            cp.start()

    def finish(ins, outs, sems):
        cps = copies(ins, outs, sems)
        for cp in cps:
            cp.wait_recv()
        for cp in cps:
            cp.wait_send()

    return _Job([Gp], [_sds((4,) + Gp.shape[1:], Gp.dtype)],
                [pltpu.SemaphoreType.DMA((4,)), pltpu.SemaphoreType.DMA((4,))], start, finish)


def _rs_chips_job(T, dests=(0, 1, 2)):
    def copies(ins, outs, sems):
        x, y, c = _mesh_pos()
        chips = [(1 - x, y), (x, 1 - y), (1 - x, 1 - y)]
        cps = []
        for i, j in enumerate(dests):
            px, py = chips[j]
            cps.append(pltpu.make_async_remote_copy(
                src_ref=ins[0].at[2 * px + py], dst_ref=outs[0].at[i], send_sem=sems[0].at[i], recv_sem=sems[1].at[i],
                device_id=(px, py, c), device_id_type=MESH))
        return cps

    def start(ins, outs, sems):
        for cp in copies(ins, outs, sems):
            cp.start()

    def finish(ins, outs, sems):
        cps = copies(ins, outs, sems)
        for cp in cps:
            cp.wait_recv()
        for cp in cps:
            cp.wait_send()

    n = len(dests)
    return _Job([T], [_sds((n,) + T.shape[1:], T.dtype)],
                [pltpu.SemaphoreType.DMA((n,)), pltpu.SemaphoreType.DMA((n,))], start, finish)


def _tile_rows(rows, cols):
    best = None
    for t in range(16, rows + 1, 16):
        if rows % t == 0 and t * cols * 4 <= (3 << 19):
            best = t
    return best if best is not None else rows


def _prefetch_call(body, *, name, grid, in_specs, out_specs, out_shape):
    return pl.pallas_call(
        body, name=name, out_shape=out_shape,
        grid_spec=pltpu.PrefetchScalarGridSpec(num_scalar_prefetch=1, grid=grid, in_specs=in_specs, out_specs=out_specs),
        compiler_params=pltpu.CompilerParams(dimension_semantics=("arbitrary",) * len(grid), vmem_limit_bytes=VMEM_LIMIT))


def _rs_add(Gp, bufA, idx, name):
    rows, cols = Gp.shape[1:]
    tr = _tile_rows(rows, cols)

    def body(i_ref, g_ref, a_ref, t_ref, own_ref):
        t = g_ref[...] + a_ref[...]
        t_ref[...] = t.astype(t_ref.dtype)

        @pl.when(pl.program_id(1) == i_ref[1])
        def _():
            own_ref[...] = t

    return _prefetch_call(
        body, name=name, grid=(rows // tr, 4),
        in_specs=[pl.BlockSpec((None, tr, cols), lambda r, k, i_ref: (2 * k + i_ref[0], r, 0)),
                  pl.BlockSpec((None, tr, cols), lambda r, k, i_ref: (k, r, 0))],
        out_specs=[pl.BlockSpec((None, tr, cols), lambda r, k, i_ref: (k, r, 0)),
                   pl.BlockSpec((tr, cols), lambda r, k, i_ref: (r, 0))],
        out_shape=[_sds((4, rows, cols), jnp.bfloat16), _sds((rows, cols))])(idx, Gp, bufA)


def _adam(w, g, m, v):
    m = ADAM_B1 * m + (1.0 - ADAM_B1) * g
    v = ADAM_B2 * v + (1.0 - ADAM_B2) * (g * g)
    m_hat = m / (1.0 - ADAM_B1 ** ADAM_STEP)
    v_hat = v / (1.0 - ADAM_B2 ** ADAM_STEP)
    return -ADAM_LR * (m_hat / (jnp.sqrt(v_hat) + ADAM_EPS) + ADAM_WD * w), m, v


def _rs_finish(Town, bufB, name, wmv=None, slab=0, n_slabs=1, prev=None):
    rows, cols = Town.shape
    tr = _tile_rows(rows, cols)
    nr = rows // tr
    n_in = 4 + (3 if wmv is not None else 0)
    n_out = 4 if wmv is not None else 1

    def body(*refs):
        ins, outs = refs[:n_in], refs[len(refs) - n_out:]
        g = ((ins[0][...] + ins[1][...].astype(F32)) + ins[2][...].astype(F32)) + ins[3][...].astype(F32)
        outs[0][...] = g
        if wmv is not None:
            d, m, v = _adam(ins[4][...], g, ins[5][...], ins[6][...])
            outs[1][...] = d
            outs[2][...] = m
            outs[3][...] = v

    plain = pl.BlockSpec((tr, cols), lambda r: (r, 0))
    slabbed = pl.BlockSpec((tr, cols), lambda r: (slab * nr + r, 0))
    pairs = bufB if isinstance(bufB, list) else [(bufB, j) for j in range(3)]
    in_specs = [plain] + [pl.BlockSpec((None, tr, cols), (lambda j: lambda r: (j, r, 0))(j)) for _, j in pairs]
    args = [Town] + [a for a, _ in pairs]
    if wmv is not None:
        in_specs += [slabbed] * 3
        args += list(wmv)
    aliases = None
    if prev is not None:
        in_specs += [pl.BlockSpec(memory_space=pl.ANY)] * n_out
        aliases = {len(args) + i: i for i in range(n_out)}
        args += list(prev)
    return _call(body, name=name, grid=(nr,), in_specs=in_specs, out_specs=[slabbed] * n_out,
                 out_shape=[_sds((n_slabs * rows, cols))] * n_out, aliases=aliases)(*args)


def _adamw_plain(w, g, m, v, name, jobs=()):
    rows, cols = w.shape
    tr = _tile_rows(rows, cols)

    def body(w_ref, g_ref, m_ref, v_ref, d_ref, mo_ref, vo_ref):
        d, m_, v_ = _adam(w_ref[...], g_ref[...], m_ref[...], v_ref[...])
        d_ref[...] = d
        mo_ref[...] = m_
        vo_ref[...] = v_

    spec = pl.BlockSpec((tr, cols), lambda r: (r, 0))
    return _call(body, jobs=jobs, name=name, grid=(rows // tr,), in_specs=[spec] * 4, out_specs=[spec] * 3,
                 out_shape=[_sds((rows, cols))] * 3)(w, g, m, v)


_MOD_TK = 512


def _mod_fwd(cc16, w_loc, b_loc, name):
    D, cols = w_loc.shape
    tk = min(_MOD_TK, D)
    nk = D // tk

    def body(c_ref, w_ref, b_ref, o_ref):
        kk = pl.program_id(0)

        @pl.when(kk == 0)
        def _():
            o_ref[...] = jnp.zeros_like(o_ref)

        cc = c_ref[...]
        o_ref[...] += _dot(cc * _sigmoid(cc), w_ref[...])

        @pl.when(kk == nk - 1)
        def _():
            o_ref[...] += b_ref[...]

    return _call(body, name=name, grid=(nk,),
                 in_specs=[pl.BlockSpec((16, tk), lambda kk: (0, kk)), pl.BlockSpec((tk, cols), lambda kk: (kk, 0)),
                           _full_spec((1, cols))],
                 out_specs=_full_spec((16, cols)), out_shape=_sds((16, cols)))(cc16, w_loc, b_loc)


def _mod_bwd(dm_loc, cc16, w_loc, name):
    D, cols = w_loc.shape

    def body(dm_ref, c_ref, w_ref, gw_ref, ds_ref):
        rows = [dm_ref[b, 0:1, :] for b in range(ND)]
        ctx = dm_ref[0, 1:2, :]
        for b in range(1, ND):
            ctx = ctx + dm_ref[b, 1:2, :]
        dm16 = jnp.concatenate(rows + [ctx, jnp.zeros((7, cols), F32)], axis=0)
        cc = c_ref[...]
        gw_ref[...] = _dot_tn(cc * _sigmoid(cc), dm16)
        ds_ref[...] = _dot_nt(dm16, w_ref[...])

    tk = min(_MOD_TK, D)
    return _call(body, name=name, grid=(D // tk,),
                 in_specs=[_full_spec((ND, 8, cols)), pl.BlockSpec((16, tk), lambda kk: (0, kk)),
                           pl.BlockSpec((tk, cols), lambda kk: (kk, 0))],
                 out_specs=[pl.BlockSpec((tk, cols), lambda kk: (kk, 0)), pl.BlockSpec((16, tk), lambda kk: (0, kk))],
                 out_shape=[_sds((D, cols)), _sds((16, D))])(dm_loc, cc16, w_loc)


def _bmod_grad(dm_all, name):
    n = dm_all.shape[-1]

    def body(dm_ref, o_ref):
        acc = dm_ref[0, 0:1, :] + dm_ref[0, 1:2, :]
        for b in range(1, ND):
            acc = (acc + dm_ref[b, 0:1, :]) + dm_ref[b, 1:2, :]
        o_ref[...] = acc

    return _call(body, name=name, grid=(1,), in_specs=[_full_spec((ND, 8, n))], out_specs=_full_spec((1, n)),
                 out_shape=_sds((1, n)))(dm_all)


_SMALL_ROWS = 24
_ROW_CCTX = 15


def _small_finish(parts, c_ctx, name):
    D = parts.shape[-1]

    def body(p_ref, c_ref, o_ref):
        acc = p_ref[0]
        for b in range(1, ND):
            acc = acc + p_ref[b]
        cc = c_ref[...]
        sg = _sigmoid(cc)
        dsilu = sg * (1.0 + cc * (1.0 - sg))
        row = lax.broadcasted_iota(jnp.int32, acc.shape, 0)
        o_ref[...] = jnp.where(row == _ROW_CCTX, acc * dsilu, acc)

    return _call(body, name=name, grid=(1,), in_specs=[_full_spec(parts.shape), _full_spec((1, D))],
                 out_specs=_full_spec((_SMALL_ROWS, D)), out_shape=_sds((_SMALL_ROWS, D)))(parts, c_ctx)


_WEIGHTS = ['c_ctx', 'w_mod', 'b_mod', 'norm_g', 'ffn_wg', 'ffn_wu', 'ffn_wd', 'w_in', 'w_out', 'q_norm_g', 'k_norm_g',
            'conv_w', 'conv_b', 'lru_wa', 'lru_ba', 'lru_wx', 'lru_bx', 'lru_lambda', 'final_norm_g']
_SMALL = ['c_ctx', 'b_mod', 'norm_g', 'q_norm_g', 'k_norm_g', 'conv_w', 'conv_b', 'lru_ba', 'lru_bx', 'lru_lambda',
          'final_norm_g']


def _pad_rows(a, rows):
    return jnp.pad(a, ((0, rows - a.shape[0]),) + ((0, 0),) * (a.ndim - 1))


def _step(w, m, v, x, c, ctx, loss_target):
    xi, yi, ci = _mesh_pos()
    me = 4 * xi + 2 * yi + ci
    idx = jnp.stack([ci, 2 * xi + yi]).astype(jnp.int32)
    S, D = x.shape[1:]
    Ds = D // ND
    cols = w['w_mod'].shape[-1]

    sp = jnp.concatenate([w['norm_g'][0], w['conv_w'][0], w['lru_ba'][0], w['lru_bx'][0], w['lru_lambda'][0]], axis=0)
    spg = _all_gather(_pad_rows(sp, 16), "ag_small_params", in_vmem=True)
    spf = jnp.transpose(spg, (1, 0, 2)).reshape(16, D)
    ng, conv_w, ba, bx, lam = spf[0:3], spf[3:7], spf[7:9], spf[9:11], spf[11:13]

    cg = _all_gather(_pad_rows(c, 8), "ag_cond", in_vmem=True)
    cc16 = _pad_rows(jnp.concatenate([cg[:, 0, :], w['c_ctx'][None, :]], axis=0), 16)
    b_loc = lax.dynamic_slice_in_dim(w['b_mod'], me * cols, cols, axis=1)
    mod_loc = _mod_fwd(cc16, w['w_mod'][0], b_loc, "mod_fwd")
    modg = _all_gather(mod_loc, "ag_mod", in_vmem=True)
    mod16 = jnp.transpose(modg, (1, 0, 2)).reshape(16, ND * cols)
    modx = lax.dynamic_index_in_dim(mod16, me, axis=0, keepdims=False).reshape(9, D)
    modc = mod16[8].reshape(9, D)

    shards = {'w_in': w['w_in'][0].astype(_MXU), 'w_out': w['w_out'][0].astype(_MXU)}
    for layer in range(2):
        shards[f'wg{layer}'] = w['ffn_wg'][0, layer].T.astype(_MXU)[None]
        shards[f'wu{layer}'] = w['ffn_wu'][0, layer].T.astype(_MXU)[None]
        shards[f'wd{layer}'] = w['ffn_wd'][0, layer].astype(_MXU)[None]
    WA, WX = w['lru_wa'][0].astype(_MXU), w['lru_wx'][0].astype(_MXU)
    big = ('ffn_wg', 'ffn_wu', 'ffn_wd', 'w_in', 'w_out')
    transposed = ('ffn_wg', 'ffn_wu')
    tr_view = lambda a: jnp.swapaxes(a, -1, -2)
    opt = {n: tuple((tr_view(a[n]) if n in transposed else a[n]) for a in (w, m, v)) for n in big}
    opt = {n: tuple(a.reshape(-1, a.shape[-1]) for a in t) for n, t in opt.items()}

    g = _local_step(x[0], ctx[0], loss_target[0], modx, modc, ng, shards, w['q_norm_g'], w['k_norm_g'],
                    conv_w, w['conv_b'], WA, WX, ba, bx, lam, w['final_norm_g'][None, :], idx, opt)

    grad, delta, new_m, new_v = {}, {}, {}, {}

    lfull = g['lru_w'].reshape((2,) + w['lru_wa'].shape[1:])
    for i, n in enumerate(('lru_wa', 'lru_wx')):
        shard = w[n].shape
        view = lambda a: a.reshape(-1, HD)
        grad[n] = lfull[i].reshape(shard)
        outs = _adamw_plain(view(w[n]), view(lfull[i]), view(m[n]), view(v[n]), f"adamw_{n}")
        delta[n], new_m[n], new_v[n] = [o.reshape(shard) for o in outs]

    dm = _pad_rows(jnp.stack([g['dmodx'].reshape(-1), g['dmodc'].reshape(-1)]), 8)
    dm_all = _all_gather(dm, "ag_dmod", in_vmem=True)
    dm_loc = lax.dynamic_slice_in_dim(dm_all, me * cols, cols, axis=2)
    gw_mod, dsil = _mod_bwd(dm_loc, cc16, w['w_mod'][0], "mod_bwd")
    grad['w_mod'] = gw_mod[None]
    T_wu0, own_wu0, b_wu0, fin_wu = g['wu0_pending']
    outs, ((b_wu0d,),) = _adamw_plain(w['w_mod'][0], gw_mod, m['w_mod'][0], v['w_mod'][0], "adamw_w_mod",
                                      jobs=[_rs_chips_job(T_wu0, (2,))])
    delta['w_mod'], new_m['w_mod'], new_v['w_mod'] = [o[None] for o in outs]
    g['ffn_wu'] = _rs_finish(own_wu0, [(b_wu0, 0), (b_wu0, 1), (b_wu0d, 0)], "wu0_rs_finish", opt['ffn_wu'], 0, 2, fin_wu)
    for n in big:
        if n in transposed:
            shape_t = w[n].shape[:-2] + (w[n].shape[-1], w[n].shape[-2])
            grad[n], delta[n], new_m[n], new_v[n] = [tr_view(o.reshape(shape_t)) for o in g[n]]
        else:
            grad[n], delta[n], new_m[n], new_v[n] = [o.reshape(w[n].shape) for o in g[n]]
    grad['b_mod'] = _bmod_grad(dm_all, "bmod_grad")

    pad_d = lambda a: jnp.concatenate([a, jnp.zeros((1, D - a.shape[1]), F32)], axis=1)
    small = jnp.concatenate([g['norm_g'], g['conv_w'], g['conv_b'], g['lru_ba'], g['lru_bx'], g['lru_lambda'],
                             g['final_norm_g'], dsil[8:9], pad_d(g['q_norm_g']), pad_d(g['k_norm_g'])], axis=0)
    parts = _all_gather(_pad_rows(small, _SMALL_ROWS), "ag_small_grads", in_vmem=True)
    ssum = _small_finish(parts, w['c_ctx'][None, :], "small_finish")
    mine = lambda rows: lax.dynamic_slice_in_dim(rows, me * Ds, Ds, axis=1)
    grad['norm_g'] = mine(ssum[0:3])[None]
    grad['conv_w'] = mine(ssum[3:7])[None]
    grad['conv_b'] = ssum[7:8]
    grad['lru_ba'] = mine(ssum[8:10])[None]
    grad['lru_bx'] = mine(ssum[10:12])[None]
    grad['lru_lambda'] = mine(ssum[12:14])[None]
    grad['final_norm_g'] = ssum[14]
    grad['c_ctx'] = ssum[_ROW_CCTX]
    grad['q_norm_g'] = ssum[16:17, :HD]
    grad['k_norm_g'] = ssum[17:18, :HD]

    def pack(d):
        flat = jnp.concatenate([d[n].reshape(-1) for n in _SMALL])
        padded = -(-flat.shape[0] // 1024) * 1024
        return jnp.concatenate([flat, jnp.zeros((padded - flat.shape[0],), F32)]).reshape(-1, HD)

    outs = _adamw_plain(pack(w), pack(grad), pack(m), pack(v), "adamw_small")
    off = 0
    for n in _SMALL:
        size = math.prod(w[n].shape)
        for dst, o in zip((delta, new_m, new_v), outs):
            dst[n] = o.reshape(-1)[off:off + size].reshape(w[n].shape)
        off += size

    loss = lax.psum(g['loss'][0, 0], ("x", "y", "c"))
    return (loss, g['grad_x'][None], *[grad[n] for n in _WEIGHTS], *[delta[n] for n in _WEIGHTS],
            *[new_m[n] for n in _WEIGHTS], *[new_v[n] for n in _WEIGHTS])


def kernel(x, c, ctx, c_ctx, w_mod, b_mod, norm_g, ffn_wg, ffn_wu, ffn_wd, w_in, w_out, q_norm_g, k_norm_g, conv_w, conv_b, lru_wa, lru_ba, lru_wx, lru_bx, lru_lambda, final_norm_g, loss_target, m_c_ctx, m_w_mod, m_b_mod, m_norm_g, m_ffn_wg, m_ffn_wu, m_ffn_wd, m_w_in, m_w_out, m_q_norm_g, m_k_norm_g, m_conv_w, m_conv_b, m_lru_wa, m_lru_ba, m_lru_wx, m_lru_bx, m_lru_lambda, m_final_norm_g, v_c_ctx, v_w_mod, v_b_mod, v_norm_g, v_ffn_wg, v_ffn_wu, v_ffn_wd, v_w_in, v_w_out, v_q_norm_g, v_k_norm_g, v_conv_w, v_conv_b, v_lru_wa, v_lru_ba, v_lru_wx, v_lru_bx, v_lru_lambda, v_final_norm_g):
    given = dict(locals())
    w = {n: given[n] for n in _WEIGHTS}
    m = {n: given["m_" + n] for n in _WEIGHTS}
    v = {n: given["v_" + n] for n in _WEIGHTS}
    return _step(w, m, v, x, c, ctx, loss_target)
```

```python
import functools
import math

import jax
import jax.numpy as jnp
from jax import lax
from jax.experimental import pallas as pl
from jax.experimental.pallas import tpu as pltpu

F32 = jnp.float32
_MXU = jnp.bfloat16
ND = 8
TR = 256
HD = 128
EPS = 1e-6
GRID_W = 64
ROPE_THETA = 10000.0
LRU_C = 8.0
VMEM_LIMIT = 56 * 1024 * 1024
SCAN_W = 1024
ADAM_LR, ADAM_B1, ADAM_B2, ADAM_EPS, ADAM_WD, ADAM_STEP = 0.001, 0.9, 0.999, 1e-08, 0.01, 10
MESH = pl.DeviceIdType.MESH


class _Job:
    def __init__(self, inputs, out_shapes, sems, start, finish, relay=None, relay2=None):
        self.inputs, self.out_shapes, self.sems, self.start, self.finish = inputs, out_shapes, sems, start, finish
        self.relay, self.relay2 = relay, relay2


def _call(body, *, name, grid, in_specs, out_specs, out_shape, scratch=(), aliases=None, jobs=()):
    params = pltpu.CompilerParams(dimension_semantics=("arbitrary",) * len(grid), vmem_limit_bytes=VMEM_LIMIT)
    if not jobs:
        return pl.pallas_call(body, name=name, grid=grid, in_specs=in_specs, out_specs=out_specs, out_shape=out_shape,
                              scratch_shapes=scratch, input_output_aliases=aliases or {}, compiler_params=params)
    single = not isinstance(out_specs, (list, tuple))
    o_specs = [out_specs] if single else list(out_specs)
    o_shape = [out_shape] if single else list(out_shape)
    n_in, n_out, n_scr = len(in_specs), len(o_specs), len(scratch)
    j_in = [a for j in jobs for a in j.inputs]
    j_out = [s for j in jobs for s in j.out_shapes]
    j_sem = [s for j in jobs for s in j.sems]
    hbm = pl.BlockSpec(memory_space=pl.ANY)

    def wrapped(*refs):
        ins, rest = refs[:n_in], refs[n_in:]
        jin, rest = rest[:len(j_in)], rest[len(j_in):]
        outs, rest = rest[:n_out], rest[n_out:]
        jout, rest = rest[:len(j_out)], rest[len(j_out):]
        scr, jsem = rest[:n_scr], rest[n_scr:]
        first = functools.reduce(jnp.logical_and, [pl.program_id(a) == 0 for a in range(len(grid))])
        last = functools.reduce(jnp.logical_and, [pl.program_id(a) == grid[a] - 1 for a in range(len(grid))])
        def at_outer(step):
            return functools.reduce(jnp.logical_and, [pl.program_id(a) == (0 if a else step) for a in range(len(grid))])

        def each(which):
            i = o = s = 0
            for j in jobs:
                ni, no, ns = len(j.inputs), len(j.out_shapes), len(j.sems)
                if getattr(j, which, None) is not None:
                    getattr(j, which)(jin[i:i + ni], jout[o:o + no], jsem[s:s + ns])
                i, o, s = i + ni, o + no, s + ns

        @pl.when(first)
        def _():
            each("start")

        relay_early = grid[0] >= 4
        if relay_early:
            @pl.when(at_outer(grid[0] // 2))
            def _():
                each("relay")

            @pl.when(at_outer((7 * grid[0]) // 8))
            def _():
                each("relay2")

        body(*ins, *outs, *scr)

        @pl.when(last)
        def _():
            if not relay_early:
                each("relay")
                each("relay2")
            each("finish")

    call = pl.pallas_call(wrapped, name=name, grid=grid, in_specs=list(in_specs) + [hbm] * len(j_in),
                          out_specs=o_specs + [hbm] * len(j_out), out_shape=o_shape + j_out,
                          scratch_shapes=list(scratch) + j_sem, input_output_aliases=aliases or {}, compiler_params=params)

    def run(*args):
        res = call(*args, *j_in)
        comp = res[0] if single else list(res[:n_out])
        jres, o = [], n_out
        for j in jobs:
            jres.append(list(res[o:o + len(j.out_shapes)]))
            o += len(j.out_shapes)
        return comp, jres

    return run


def _comm_call(jobs, name):
    j_in = [a for j in jobs for a in j.inputs]
    j_out = [s for j in jobs for s in j.out_shapes]
    j_sem = [s for j in jobs for s in j.sems]
    hbm = pl.BlockSpec(memory_space=pl.ANY)

    def body(*refs):
        jin, jout, jsem = refs[:len(j_in)], refs[len(j_in):len(j_in) + len(j_out)], refs[len(j_in) + len(j_out):]
        for which in ("start", "relay", "relay2", "finish"):
            i = o = s = 0
            for j in jobs:
                ni, no, ns = len(j.inputs), len(j.out_shapes), len(j.sems)
                if getattr(j, which, None) is not None:
                    getattr(j, which)(jin[i:i + ni], jout[o:o + no], jsem[s:s + ns])
                i, o, s = i + ni, o + no, s + ns

    res = pl.pallas_call(body, name=name, out_shape=j_out, in_specs=[hbm] * len(j_in), out_specs=[hbm] * len(j_out),
                         scratch_shapes=j_sem)(*j_in)
    jres, o = [], 0
    for j in jobs:
        jres.append(list(res[o:o + len(j.out_shapes)]))
        o += len(j.out_shapes)
    return jres


def _sds(shape, dtype=F32):
    return jax.ShapeDtypeStruct(tuple(shape), dtype)


def _dot(a, b):
    return jnp.dot(a.astype(_MXU), b.astype(_MXU), preferred_element_type=F32)


def _dot_nt(a, b):
    return lax.dot_general(a.astype(_MXU), b.astype(_MXU), (((1,), (1,)), ((), ())), preferred_element_type=F32)


def _dot_tn(a, b):
    return lax.dot_general(a.astype(_MXU), b.astype(_MXU), (((0,), (0,)), ((), ())), preferred_element_type=F32)


def _sigmoid(x):
    return 0.5 * jnp.tanh(0.5 * x) + 0.5


_GELU_C = math.sqrt(2.0 / math.pi)


def _gelu_and_grad(x):
    x2 = x * x
    t = jnp.tanh(_GELU_C * (x + 0.044715 * x * x2))
    ge = 0.5 * x * (1.0 + t)
    dge = 0.5 * (1.0 + t) + 0.5 * x * (1.0 - t * t) * (_GELU_C * (1.0 + 3.0 * 0.044715 * x2))
    return ge, dge


def _tm(rows):
    assert rows % 4 == 0 and (rows // 4) % 16 == 0
    return rows // 4


def _row_spec(width, nmax=None):
    if nmax is None:
        return pl.BlockSpec((TR, width), lambda i: (i, 0))
    return pl.BlockSpec((TR, width), lambda i: (jnp.minimum(i, nmax), 0))


def _full_spec(shape):
    n = len(shape)
    return pl.BlockSpec(tuple(shape), lambda *_: (0,) * n)


def _mod_spec(D, nx):
    return pl.BlockSpec((None, 9, D), lambda i: (i // nx, 0, 0))


def _norm_mod_fwd(X, ng, mod2, k, rows, nx, name):
    D = X.shape[1]

    def body(x_ref, g_ref, mod_ref, h_ref):
        x = x_ref[...]
        r = lax.rsqrt(jnp.mean(x * x, axis=-1, keepdims=True) + EPS)
        n = (x * r) * g_ref[...]
        h_ref[...] = (n * (1.0 + mod_ref[3 * k + 1:3 * k + 2, :]) + mod_ref[3 * k:3 * k + 1, :]).astype(h_ref.dtype)

    return _call(body, name=name, grid=(rows // TR,),
                 in_specs=[_row_spec(D), _full_spec((1, D)), _mod_spec(D, nx)],
                 out_specs=_row_spec(D), out_shape=_sds((rows, D), _MXU))(X, ng, mod2)


def _norm_mod_bwd(X, dH, dXres, ng, mod2, k, rows, nx, res_tiles, name, jobs=(), branch=None):
    D = X.shape[1]
    ngroups = -(-(rows // TR) // nx)
    n_in = 5 + (1 if branch is not None else 0)

    def body(*refs):
        x_ref, dh_ref, dres_ref, g_ref, mod_ref = refs[:5]
        dx_ref, dsh_ref, dsc_ref, dng_ref = refs[n_in:n_in + 4]
        i = pl.program_id(0)
        x = x_ref[...]
        dh = dh_ref[...]
        g = g_ref[...]
        r = lax.rsqrt(jnp.mean(x * x, axis=-1, keepdims=True) + EPS)
        xh = x * r
        n = xh * g
        dn_mod = dh * (1.0 + mod_ref[3 * k + 1:3 * k + 2, :])

        @pl.when(i % nx == 0)
        def _():
            dsh_ref[...] = jnp.zeros_like(dsh_ref)
            dsc_ref[...] = jnp.zeros_like(dsc_ref)
            if branch is not None:
                refs[n_in + 5][...] = jnp.zeros_like(refs[n_in + 5])

        @pl.when(i == 0)
        def _():
            dng_ref[...] = jnp.zeros_like(dng_ref)

        dsh_ref[...] += jnp.sum(dh, axis=0, keepdims=True)
        dsc_ref[...] += jnp.sum(dh * n, axis=0, keepdims=True)
        dng_ref[...] += jnp.sum(dn_mod * xh, axis=0, keepdims=True)
        dn = dn_mod * g
        dres = jnp.where(i < res_tiles, dres_ref[...], 0.0)
        dx = r * (dn - xh * jnp.mean(dn * xh, axis=-1, keepdims=True)) + dres
        dx_ref[...] = dx
        if branch is not None:
            _, k2, coef = branch
            refs[n_in + 4][...] = ((coef * mod_ref[3 * k2 + 2:3 * k2 + 3, :]) * dx).astype(refs[n_in + 4].dtype)
            refs[n_in + 5][...] += jnp.sum(coef * dx * refs[5][...], axis=0, keepdims=True)

    grp = pl.BlockSpec((None, 1, D), lambda i: (i // nx, 0, 0))
    in_specs = [_row_spec(D), _row_spec(D), _row_spec(D, res_tiles - 1), _full_spec((1, D)), _mod_spec(D, nx)]
    out_specs = [_row_spec(D), grp, grp, _full_spec((1, D))]
    out_shape = [_sds((rows, D)), _sds((ngroups, 1, D)), _sds((ngroups, 1, D)), _sds((1, D))]
    args = [X, dH, dXres, ng, mod2]
    if branch is not None:
        in_specs.append(_row_spec(D))
        args.append(branch[0])
        out_specs += [_row_spec(D), grp]
        out_shape += [_sds((rows, D), _MXU), _sds((ngroups, 1, D))]
    return _call(body, jobs=jobs, name=name, grid=(rows // TR,), in_specs=in_specs, out_specs=out_specs,
                 out_shape=out_shape)(*args)


def _ffn_gate(H, WG, rows, name, jobs=()):
    D = H.shape[1]
    Fb = WG.shape[-2]

    def body(h_ref, w_ref, g_ref):
        g_ref[...] = _dot_nt(h_ref[...], w_ref[...])

    tm = _tm(rows)
    return _call(body, jobs=jobs, name=name, grid=(ND, rows // tm),
                 in_specs=[pl.BlockSpec((tm, D), lambda d, m: (m, 0)), pl.BlockSpec((None, None, Fb, D), lambda d, m: (d, 0, 0, 0))],
                 out_specs=pl.BlockSpec((None, tm, Fb), lambda d, m: (d, m, 0)), out_shape=_sds((ND, rows, Fb)))(H, WG)


def _ffn_up(H, WG, WU, rows, name, jobs=(), G=None):
    D = H.shape[1]
    Fb = WU.shape[-2]

    def body(h_ref, w_ref, x_ref, *outs):
        h = h_ref[...]
        g = x_ref[...] if G is not None else _dot_nt(h, x_ref[...])
        u = _dot_nt(h, w_ref[...])
        if G is None:
            outs[0][...] = g
        outs[-2][...] = u
        outs[-1][...] = ((g * _sigmoid(g)) * u).astype(outs[-1].dtype)

    tm = _tm(rows)
    blk = pl.BlockSpec((None, tm, Fb), lambda d, m: (d, m, 0))
    wspec = pl.BlockSpec((None, None, Fb, D), lambda d, m: (d, 0, 0, 0))
    f32o, bfo = _sds((ND, rows, Fb)), _sds((ND, rows, Fb), _MXU)
    return _call(body, jobs=jobs, name=name, grid=(ND, rows // tm),
                 in_specs=[pl.BlockSpec((tm, D), lambda d, m: (m, 0)), wspec, blk if G is not None else wspec],
                 out_specs=[blk, blk] if G is not None else [blk, blk, blk],
                 out_shape=[f32o, bfo] if G is not None else [f32o, f32o, bfo])(H, WU, G if G is not None else WG)


def _ffn_down(A, WD, layer, X, mod2, k, rows, S, name, jobs=(), next_norm=None):
    Fb, D = WD.shape[-2:]
    tm = _tm(rows) // 2

    def body(a_ref, w_ref, x_ref, mod_ref, *rest):
        if next_norm is not None:
            g_ref, y_ref, xn_ref, h_ref, acc_ref = rest
        else:
            y_ref, xn_ref, acc_ref = rest
        d = pl.program_id(1)

        @pl.when(d == 0)
        def _():
            acc_ref[...] = jnp.zeros_like(acc_ref)

        acc_ref[...] += _dot(a_ref[...], w_ref[...])

        @pl.when(d == ND - 1)
        def _():
            y = acc_ref[...]
            y_ref[...] = y
            is_ctx = (pl.program_id(0) * tm + lax.broadcasted_iota(jnp.int32, (tm, 1), 0)) >= S

            def mod_row(j):
                return jnp.where(is_ctx, mod_ref[1, j:j + 1, :], mod_ref[0, j:j + 1, :])

            xn = x_ref[...] + (0.5 * mod_row(3 * k + 2)) * y
            xn_ref[...] = xn
            if next_norm is not None:
                k2 = next_norm[1]
                n = (xn * lax.rsqrt(jnp.mean(xn * xn, axis=-1, keepdims=True) + EPS)) * g_ref[...]
                h_ref[...] = (n * (1.0 + mod_row(3 * k2 + 1)) + mod_row(3 * k2)).astype(h_ref.dtype)

    row = pl.BlockSpec((tm, D), lambda m, d: (m, 0))
    in_specs = [pl.BlockSpec((None, tm, Fb), lambda m, d: (d, m, 0)),
                pl.BlockSpec((None, None, Fb, D), lambda m, d: (d, layer, 0, 0)),
                row, pl.BlockSpec((2, 9, D), lambda m, d: (0, 0, 0))]
    args = [A, WD, X, mod2]
    out_specs, out_shape = [row, row], [_sds((rows, D)), _sds((rows, D))]
    if next_norm is not None:
        in_specs.append(pl.BlockSpec((1, D), lambda m, d: (0, 0)))
        args.append(next_norm[0])
        out_specs.append(row)
        out_shape.append(_sds((rows, D), _MXU))
    return _call(body, jobs=jobs, name=name, grid=(rows // tm, ND), in_specs=in_specs, out_specs=out_specs,
                 out_shape=out_shape, scratch=[pltpu.VMEM((tm, D), F32)])(*args)


def _ffn_down_loss(A, WD, X, mod2, k, fg, target, name):
    Fb, D = WD.shape[-2:]
    S = X.shape[0]
    tm = _tm(S) // 2

    def body(a_ref, w_ref, x_ref, mod_ref, g_ref, t_ref, loss_ref, dx_ref, dy_ref, dgate_ref, dg_ref, acc_ref):
        m, d = pl.program_id(0), pl.program_id(1)

        @pl.when(d == 0)
        def _():
            acc_ref[...] = jnp.zeros_like(acc_ref)

        @pl.when(jnp.logical_and(m == 0, d == 0))
        def _():
            loss_ref[...] = jnp.zeros_like(loss_ref)
            dgate_ref[...] = jnp.zeros_like(dgate_ref)
            dg_ref[...] = jnp.zeros_like(dg_ref)

        acc_ref[...] += _dot(a_ref[...], w_ref[...])

        @pl.when(d == ND - 1)
        def _():
            y = acc_ref[...]
            gate = 0.5 * mod_ref[3 * k + 2:3 * k + 3, :]
            x = x_ref[...] + gate * y
            g = g_ref[...]
            r = lax.rsqrt(jnp.mean(x * x, axis=-1, keepdims=True) + EPS)
            n = x * r
            err = n * g - t_ref[...]
            loss_ref[...] += 0.5 * jnp.sum(jnp.mean(err * err, axis=-1, keepdims=True), axis=0, keepdims=True)
            dy = err * (1.0 / D)
            dg_ref[...] += jnp.sum(dy * n, axis=0, keepdims=True)
            dn = dy * g
            dx = r * (dn - n * jnp.mean(dn * n, axis=-1, keepdims=True))
            dx_ref[...] = dx
            dy_ref[...] = (gate * dx).astype(dy_ref.dtype)
            dgate_ref[...] += jnp.sum(0.5 * dx * y, axis=0, keepdims=True)

    row = pl.BlockSpec((tm, D), lambda m, d: (m, 0))
    vec = pl.BlockSpec((1, D), lambda m, d: (0, 0))
    return _call(body, name=name, grid=(S // tm, ND),
                 in_specs=[pl.BlockSpec((None, tm, Fb), lambda m, d: (d, m, 0)),
                           pl.BlockSpec((None, None, Fb, D), lambda m, d: (d, 0, 0, 0)),
                           row, pl.BlockSpec((None, 9, D), lambda m, d: (0, 0, 0)), vec, row],
                 out_specs=[pl.BlockSpec((1, 1), lambda m, d: (0, 0)), row, row, vec, vec],
                 out_shape=[_sds((1, 1)), _sds((S, D)), _sds((S, D), _MXU), _sds((1, D)), _sds((1, D))],
                 scratch=[pltpu.VMEM((tm, D), F32)])(A, WD, X, mod2, fg, target)


def _ffn_dact(dYb, WD, layer, G, U, rows, name, jobs=()):
    Fb, D = WD.shape[-2:]

    def body(dy_ref, w_ref, g_ref, u_ref, dg_ref, du_ref):
        da = _dot_nt(dy_ref[...], w_ref[...])
        g = g_ref[...]
        sg = _sigmoid(g)
        dg_ref[...] = (da * u_ref[...] * (sg * (1.0 + g * (1.0 - sg)))).astype(dg_ref.dtype)
        du_ref[...] = (da * (g * sg)).astype(du_ref.dtype)

    tm = _tm(rows)
    blk = pl.BlockSpec((None, tm, Fb), lambda m, d: (d, m, 0))
    return _call(body, jobs=jobs, name=name, grid=(rows // tm, ND),
                 in_specs=[pl.BlockSpec((tm, D), lambda m, d: (m, 0)),
                           pl.BlockSpec((None, None, Fb, D), lambda m, d: (d, layer, 0, 0)), blk, blk],
                 out_specs=[blk, blk],
                 out_shape=[_sds((ND, rows, Fb), _MXU), _sds((ND, rows, Fb), _MXU)])(dYb, WD, G, U)


def _ffn_dh(dG, dU, WG, WU, rows, name, jobs=()):
    Fb, D = WG.shape[-2:]

    def body(dg_ref, du_ref, wg_ref, wu_ref, dh_ref, acc_ref):
        d = pl.program_id(1)

        @pl.when(d == 0)
        def _():
            acc_ref[...] = jnp.zeros_like(acc_ref)

        acc_ref[...] += _dot(dg_ref[...], wg_ref[...]) + _dot(du_ref[...], wu_ref[...])

        @pl.when(d == ND - 1)
        def _():
            dh_ref[...] = acc_ref[...]

    tm = _tm(rows)
    blk = pl.BlockSpec((None, tm, Fb), lambda m, d: (d, m, 0))
    wspec = pl.BlockSpec((None, None, Fb, D), lambda m, d: (d, 0, 0, 0))
    return _call(body, jobs=jobs, name=name, grid=(rows // tm, ND), in_specs=[blk, blk, wspec, wspec],
                 out_specs=pl.BlockSpec((tm, D), lambda m, d: (m, 0)), out_shape=_sds((rows, D)),
                 scratch=[pltpu.VMEM((tm, D), F32)])(dG, dU, WG, WU)


def _mm_tn(A, a_spec, B, b_spec, out_shape, out_spec, rows, name, prev=None, jobs=()):
    def body(*refs):
        a_ref, b_ref, o_ref = refs[0], refs[1], refs[-1]

        @pl.when(pl.program_id(1) == 0)
        def _():
            o_ref[...] = jnp.zeros_like(o_ref)

        o_ref[...] += _dot_tn(a_ref[...], b_ref[...])

    in_specs = [a_spec, b_spec]
    args = [A, B]
    aliases = None
    if prev is not None:
        in_specs.append(pl.BlockSpec(memory_space=pl.ANY))
        args.append(prev)
        aliases = {2: 0}
    return _call(body, jobs=jobs, name=name, grid=(ND, rows // _tm(rows)), in_specs=in_specs, out_specs=out_spec,
                 out_shape=_sds(out_shape), aliases=aliases)(*args)


def _proj_in(H2, WIN, name, jobs=()):
    R, D = H2.shape
    Nb = WIN.shape[-1]

    def body(h_ref, w_ref, p_ref):
        p_ref[...] = _dot(h_ref[...], w_ref[...])

    tm = _tm(R)
    return _call(body, jobs=jobs, name=name, grid=(ND, R // tm),
                 in_specs=[pl.BlockSpec((tm, D), lambda d, m: (m, 0)), pl.BlockSpec((None, D, Nb), lambda d, m: (d, 0, 0))],
                 out_specs=pl.BlockSpec((tm, Nb), lambda d, m: (m, d)), out_shape=_sds((R, ND * Nb)))(H2, WIN)


def _dproj_in(dP, WIN, name, jobs=()):
    R = dP.shape[0]
    D, Nb = WIN.shape[-2:]

    def body(dp_ref, w_ref, dh_ref, acc_ref):
        d = pl.program_id(1)

        @pl.when(d == 0)
        def _():
            acc_ref[...] = jnp.zeros_like(acc_ref)

        acc_ref[...] += _dot_nt(dp_ref[...], w_ref[...])

        @pl.when(d == ND - 1)
        def _():
            dh_ref[...] = acc_ref[...]

    tm = _tm(R)
    return _call(body, jobs=jobs, name=name, grid=(R // tm, ND),
                 in_specs=[pl.BlockSpec((tm, Nb), lambda m, d: (m, d)), pl.BlockSpec((None, D, Nb), lambda m, d: (d, 0, 0))],
                 out_specs=pl.BlockSpec((tm, D), lambda m, d: (m, 0)), out_shape=_sds((R, D)),
                 scratch=[pltpu.VMEM((tm, D), F32)])(dP, WIN)


def _proj_out(mixb, WOUT, X1, mod2, ng, S, name):
    D = WOUT.shape[0]

    def body(m_ref, w_ref, x_ref, mod_ref, g_ref, z_ref, xn_ref, h_ref):
        z = _dot(m_ref[...], w_ref[...])
        z_ref[...] = z
        xn = x_ref[...] + mod_ref[5:6, :] * z
        xn_ref[...] = xn
        n = (xn * lax.rsqrt(jnp.mean(xn * xn, axis=-1, keepdims=True) + EPS)) * g_ref[...]
        h_ref[...] = (n * (1.0 + mod_ref[7:8, :]) + mod_ref[6:7, :]).astype(h_ref.dtype)

    return _call(body, name=name, grid=(S // TR,),
                 in_specs=[_row_spec(D), _full_spec((D, D)), _row_spec(D), pl.BlockSpec((None, 9, D), lambda i: (0, 0, 0)),
                           _full_spec((1, D))],
                 out_specs=[_row_spec(D), _row_spec(D), _row_spec(D)],
                 out_shape=[_sds((S, D)), _sds((S, D)), _sds((S, D), _MXU)])(mixb, WOUT, X1, mod2, ng)


def _dproj_out(dZb, WOUT, name):
    S, D = dZb.shape

    def body(dz_ref, w_ref, dm_ref):
        dm_ref[...] = _dot_nt(dz_ref[...], w_ref[...])

    return _call(body, name=name, grid=(S // TR,), in_specs=[_row_spec(D), _full_spec((D, D))],
                 out_specs=_row_spec(D), out_shape=_sds((S, D)))(dZb, WOUT)


def _pair_swap(t):
    lane = lax.broadcasted_iota(jnp.int32, t.shape, 1)
    return jnp.where(lane % 2 == 0, pltpu.roll(t, HD - 1, 1), pltpu.roll(t, 1, 1))


SM_SCALE = HD ** -0.5


def _qkv_prep(P, qg, kg, COS, SIN, D, KVW, name):
    R = P.shape[0]
    W = D + 2 * KVW
    nq, nk = D // HD, KVW // HD

    def body(p_ref, qg_ref, kg_ref, cos_ref, sin_ref, q_ref, k_ref, v_ref):
        cos, sin = cos_ref[...], sin_ref[...]

        def head(t, g):
            y = (t * lax.rsqrt(jnp.mean(t * t, axis=-1, keepdims=True) + EPS)) * g
            return y * cos + _pair_swap(y) * sin

        for h in range(nq):
            q_ref[:, h * HD:(h + 1) * HD] = (head(p_ref[:, h * HD:(h + 1) * HD], qg_ref[...]) * SM_SCALE).astype(q_ref.dtype)
        for h in range(nk):
            k_ref[:, h * HD:(h + 1) * HD] = head(p_ref[:, D + h * HD:D + (h + 1) * HD], kg_ref[...]).astype(k_ref.dtype)
        v_ref[...] = p_ref[:, D + KVW:W].astype(v_ref.dtype)

    return _call(body, name=name, grid=(R // TR,),
                 in_specs=[_row_spec(W), _full_spec((1, HD)), _full_spec((1, HD)), _row_spec(HD), _row_spec(HD)],
                 out_specs=[_row_spec(D), _row_spec(KVW), _row_spec(KVW)],
                 out_shape=[_sds((R, D), _MXU), _sds((R, KVW), _MXU), _sds((R, KVW), _MXU)])(P, qg, kg, COS, SIN)


def _qkv_bwd(P, dq, dk, dv, qg, kg, COS, SIN, dP, D, KVW, nx, name):
    R, INW = P.shape
    W = D + 2 * KVW
    nq, nk = D // HD, KVW // HD

    def body(p_ref, dq_ref, dk_ref, dv_ref, qg_ref, kg_ref, cos_ref, sin_ref, dp_in, dp_ref, dqg_ref, dkg_ref):
        i = pl.program_id(0)
        cos, sin = cos_ref[...], sin_ref[...]

        @pl.when(i == 0)
        def _():
            dqg_ref[...] = jnp.zeros_like(dqg_ref)
            dkg_ref[...] = jnp.zeros_like(dkg_ref)

        def head_bwd(t, g, dout):
            r = lax.rsqrt(jnp.mean(t * t, axis=-1, keepdims=True) + EPS)
            n = t * r
            dy = dout * cos + _pair_swap(dout * sin)
            dn = dy * g
            return r * (dn - n * jnp.mean(dn * n, axis=-1, keepdims=True)), jnp.sum(dy * n, axis=0, keepdims=True)

        dqg = jnp.zeros((1, HD), F32)
        for h in range(nq):
            sl = slice(h * HD, (h + 1) * HD)
            dt, dg = head_bwd(p_ref[:, sl], qg_ref[...], jnp.where(i < nx, dq_ref[:, sl] * SM_SCALE, 0.0))
            dp_ref[:, sl] = dt.astype(dp_ref.dtype)
            dqg += dg
        dkg = jnp.zeros((1, HD), F32)
        for h in range(nk):
            sl = slice(h * HD, (h + 1) * HD)
            dt, dg = head_bwd(p_ref[:, D + h * HD:D + (h + 1) * HD], kg_ref[...], dk_ref[:, sl])
            dp_ref[:, D + h * HD:D + (h + 1) * HD] = dt.astype(dp_ref.dtype)
            dkg += dg
        dqg_ref[...] += dqg
        dkg_ref[...] += dkg
        dp_ref[:, D + KVW:W] = dv_ref[...].astype(dp_ref.dtype)

    return _call(body, name=name, grid=(R // TR,),
                 in_specs=[_row_spec(W), _row_spec(D, nx - 1), _row_spec(KVW), _row_spec(KVW), _full_spec((1, HD)),
                           _full_spec((1, HD)), _row_spec(HD), _row_spec(HD), pl.BlockSpec(memory_space=pl.ANY)],
                 out_specs=[_row_spec(W), _full_spec((1, HD)), _full_spec((1, HD))],
                 out_shape=[_sds((R, INW), _MXU), _sds((1, HD)), _sds((1, HD))],
                 aliases={8: 0})(P, dq, dk, dv, qg, kg, COS, SIN, dP)


def _stack_heads(ref, G, dtype=None):
    parts = [ref[:, g * HD:(g + 1) * HD] for g in range(G)]
    out = jnp.concatenate(parts, axis=0)
    return out if dtype is None else out.astype(dtype)


_KEY_CHUNKS = 4


def _key_chunks(R):
    unit = 256 if R % 256 == 0 else 16
    nt = R // unit
    n = min(_KEY_CHUNKS, nt)
    bounds = [0]
    for i in range(n):
        bounds.append(bounds[-1] + (nt // n + (1 if i < nt % n else 0)) * unit)
    return bounds


def _attn_fwd(q, k, v, S, G, name, jobs=()):
    R, KVW = k.shape
    D = q.shape[1]
    Kh = KVW // HD
    tq = 128
    kb = _key_chunks(R)

    def body(q_ref, k_ref, v_ref, o_ref, lse_ref):
        qs = _stack_heads(q_ref, G)
        m = l = acc = None
        for c in range(len(kb) - 1):
            s = _dot_nt(qs, k_ref[kb[c]:kb[c + 1], :])
            mc = jnp.max(s, axis=-1, keepdims=True)
            m_new = mc if c == 0 else jnp.maximum(m, mc)
            p = jnp.exp(s - m_new)
            ps = jnp.sum(p, axis=-1, keepdims=True)
            pv = _dot(p, v_ref[kb[c]:kb[c + 1], :])
            if c == 0:
                l, acc = ps, pv
            else:
                alpha = jnp.exp(m - m_new)
                l = alpha * l + ps
                acc = alpha * acc + pv
            m = m_new
        o = acc / l
        lse = m + jnp.log(l)
        for g in range(G):
            o_ref[:, g * HD:(g + 1) * HD] = o[g * tq:(g + 1) * tq, :]
            lse_ref[g] = jnp.broadcast_to(lse[g * tq:(g + 1) * tq, :], (tq, HD))

    return _call(body, jobs=jobs, name=name, grid=(Kh, S // tq),
                 in_specs=[pl.BlockSpec((tq, G * HD), lambda h, i: (i, h)), pl.BlockSpec((R, HD), lambda h, i: (0, h)),
                           pl.BlockSpec((R, HD), lambda h, i: (0, h))],
                 out_specs=[pl.BlockSpec((tq, G * HD), lambda h, i: (i, h)),
                            pl.BlockSpec((None, G, tq, HD), lambda h, i: (h, 0, i, 0))],
                 out_shape=[_sds((S, D)), _sds((Kh, G, S, HD))])(q, k, v)


def _attn_bwd(q, k, v, O, LSE, dOb, S, G, name, jobs=()):
    R, KVW = k.shape
    D = q.shape[1]
    Kh = KVW // HD
    tq = 128
    kb = _key_chunks(R)

    def body(q_ref, k_ref, v_ref, o_ref, lse_ref, do_ref, dq_ref, dk_ref, dv_ref):
        @pl.when(pl.program_id(1) == 0)
        def _():
            dk_ref[...] = jnp.zeros_like(dk_ref)
            dv_ref[...] = jnp.zeros_like(dv_ref)

        qs = _stack_heads(q_ref, G)
        do = _stack_heads(do_ref, G)
        o = _stack_heads(o_ref, G)
        lse = jnp.concatenate([lse_ref[g][:, 0:1] for g in range(G)], axis=0)
        delta = jnp.sum(do.astype(F32) * o, axis=-1, keepdims=True)
        dq = None
        for c in range(len(kb) - 1):
            rows = slice(kb[c], kb[c + 1])
            kk = k_ref[rows, :]
            p = jnp.exp(_dot_nt(qs, kk) - lse)
            dp = _dot_nt(do, v_ref[rows, :])
            ds = (p * (dp - delta)).astype(_MXU)
            dqc = _dot(ds, kk)
            dq = dqc if c == 0 else dq + dqc
            dk_ref[rows, :] += _dot_tn(ds, qs)
            dv_ref[rows, :] += _dot_tn(p, do)
        for g in range(G):
            dq_ref[:, g * HD:(g + 1) * HD] = dq[g * tq:(g + 1) * tq, :]

    qspec = pl.BlockSpec((tq, G * HD), lambda h, i: (i, h))
    kspec = pl.BlockSpec((R, HD), lambda h, i: (0, h))
    return _call(body, jobs=jobs, name=name, grid=(Kh, S // tq),
                 in_specs=[qspec, kspec, kspec, qspec, pl.BlockSpec((None, G, tq, HD), lambda h, i: (h, 0, i, 0)), qspec],
                 out_specs=[qspec, kspec, kspec],
                 out_shape=[_sds((S, D)), _sds((R, KVW)), _sds((R, KVW))])(q, k, v, O, LSE, dOb)


def _halo_specs(R, CB, col0):
    nt8 = TR // 8
    return [pl.BlockSpec((8, CB), lambda h, i: (jnp.maximum(i * nt8 - 1, 0), col0 + h)),
            pl.BlockSpec((TR, CB), lambda h, i: (i, col0 + h)),
            pl.BlockSpec((8, CB), lambda h, i: (jnp.minimum((i + 1) * nt8, R // 8 - 1), col0 + h))]


def _seq_pos(i, S, R, CB):
    t = i * TR - 8 + lax.broadcasted_iota(jnp.int32, (TR + 16, CB), 0)
    start = jnp.where(t >= S, S, 0)
    end = jnp.where(t >= S, R, S)
    return t - start, end - t


def _shift(cat, by):
    return pltpu.roll(cat, by % cat.shape[0], 0)


def _gate_mats(xcb, w_ref, dirn, nb):
    return jnp.concatenate([_dot(xcb[:, b * HD:(b + 1) * HD], w_ref[dirn, b]) for b in range(nb)], axis=1)


def _lru_gates_fwd(P, conv_w, conv_b, WA, WX, ba, bx, lam, S, D, col0, name):
    R = P.shape[0]
    CB = D // 2
    nb = CB // HD

    def body(xp_ref, x_ref, xn_ref, cw_ref, cb_ref, wa_ref, wx_ref, ba_ref, bx_ref, lam_ref,
             xc_ref, af_ref, uf_ref, ab_ref, ub_ref):
        i = pl.program_id(1)
        cat = jnp.concatenate([xp_ref[...], x_ref[...], xn_ref[...]], axis=0)
        from_start, to_end = _seq_pos(i, S, R, CB)
        conv = (cb_ref[...] + cw_ref[2:3, :] * cat
                + cw_ref[0:1, :] * jnp.where(from_start >= 2, _shift(cat, 2), 0.0)
                + cw_ref[1:2, :] * jnp.where(from_start >= 1, _shift(cat, 1), 0.0)
                + cw_ref[3:4, :] * jnp.where(to_end >= 2, _shift(cat, -1), 0.0))
        xc = conv[8:8 + TR, :]
        xc_ref[...] = xc
        xcb = xc.astype(_MXU)
        for dirn, (a_ref, u_ref) in enumerate(((af_ref, uf_ref), (ab_ref, ub_ref))):
            ra = _sigmoid(_gate_mats(xcb, wa_ref, dirn, nb) + ba_ref[dirn:dirn + 1, :])
            ia = _sigmoid(_gate_mats(xcb, wx_ref, dirn, nb) + bx_ref[dirn:dirn + 1, :])
            nl = -lam_ref[dirn:dirn + 1, :]
            sp = jnp.maximum(nl, 0.0) + jnp.log(1.0 + jnp.exp(-jnp.abs(nl)))
            la = (-LRU_C) * ra * sp
            a = jnp.exp(la)
            a_ref[...] = a
            u_ref[...] = jnp.sqrt(1.0 - a * a) * (ia * xc)

    def par(r):
        return pl.BlockSpec((r, CB), lambda h, i: (0, h))

    wspec = pl.BlockSpec((2, nb, HD, HD), lambda h, i: (0, h, 0, 0))
    out = pl.BlockSpec((TR, CB), lambda h, i: (i, h))
    return _call(body, name=name, grid=(2, R // TR),
                 in_specs=_halo_specs(R, CB, col0) + [par(4), par(1), wspec, wspec, par(2), par(2), par(2)],
                 out_specs=[out] * 5, out_shape=[_sds((R, D))] * 5,
                 )(P, P, P, conv_w, conv_b, WA, WX, ba, bx, lam)


def _lru_gates_bwd(xc, hf, hb, gf, gb, WA, WX, ba, bx, lam, S, name):
    R, D = xc.shape
    CB = D // 2
    nb = CB // HD

    def body(xc_ref, hfq_ref, hf_ref, hfl_ref, hb_ref, hbn_ref, gf_ref, gb_ref, wa_ref, wx_ref, ba_ref, bx_ref, lam_ref,
             dxc_ref, dwa_ref, dwx_ref, dba_ref, dbx_ref, dlam_ref):
        @pl.when(pl.program_id(1) == 0)
        def _():
            for r in (dwa_ref, dwx_ref, dba_ref, dbx_ref, dlam_ref):
                r[...] = jnp.zeros_like(r)

        t = pl.program_id(1) * TR + lax.broadcasted_iota(jnp.int32, (TR, CB), 0)
        hfp = _shift(jnp.concatenate([hfq_ref[...], hf_ref[...]], axis=0), 1)[8:8 + TR, :]
        hfp = jnp.where(t == 0, hfl_ref[7:8, :], jnp.where(t == S, 0.0, hfp))
        hbp = _shift(jnp.concatenate([hb_ref[...], hbn_ref[...]], axis=0), -1)[0:TR, :]
        hbp = jnp.where(t == R - 1, 0.0, hbp)

        xc = xc_ref[...]
        xcb = xc.astype(_MXU)
        dxc = jnp.zeros_like(xc)
        for dirn, (hp, g_ref) in enumerate(((hfp, gf_ref), (hbp, gb_ref))):
            ra = _sigmoid(_gate_mats(xcb, wa_ref, dirn, nb) + ba_ref[dirn:dirn + 1, :])
            ia = _sigmoid(_gate_mats(xcb, wx_ref, dirn, nb) + bx_ref[dirn:dirn + 1, :])
            nl = -lam_ref[dirn:dirn + 1, :]
            sp = jnp.maximum(nl, 0.0) + jnp.log(1.0 + jnp.exp(-jnp.abs(nl)))
            la = (-LRU_C) * ra * sp
            a = jnp.exp(la)
            e2 = a * a
            rs = lax.rsqrt(1.0 - e2)
            s = (1.0 - e2) * rs
            du = g_ref[...]
            dla = du * hp * a - du * (ia * xc) * (e2 * rs)
            dxc += du * s * ia
            dza = (dla * (-LRU_C) * sp) * ra * (1.0 - ra)
            dzx = (du * s * xc) * ia * (1.0 - ia)
            dlam_ref[dirn:dirn + 1, :] += jnp.sum(dla * (LRU_C * ra) * _sigmoid(nl), axis=0, keepdims=True)
            dba_ref[dirn:dirn + 1, :] += jnp.sum(dza, axis=0, keepdims=True)
            dbx_ref[dirn:dirn + 1, :] += jnp.sum(dzx, axis=0, keepdims=True)
            dzab, dzxb = dza.astype(_MXU), dzx.astype(_MXU)
            parts = []
            for b in range(nb):
                sl = slice(b * HD, (b + 1) * HD)
                dwa_ref[dirn, b] += _dot_tn(xcb[:, sl], dzab[:, sl])
                dwx_ref[dirn, b] += _dot_tn(xcb[:, sl], dzxb[:, sl])
                parts.append(_dot_nt(dzab[:, sl], wa_ref[dirn, b]) + _dot_nt(dzxb[:, sl], wx_ref[dirn, b]))
            dxc += jnp.concatenate(parts, axis=1)
        dxc_ref[...] = dxc

    def par(r):
        return pl.BlockSpec((r, CB), lambda h, i: (0, h))

    wspec = pl.BlockSpec((2, nb, HD, HD), lambda h, i: (0, h, 0, 0))
    tile = pl.BlockSpec((TR, CB), lambda h, i: (i, h))
    before, _, after = _halo_specs(R, CB, 0)
    last = pl.BlockSpec((8, CB), lambda h, i: (R // 8 - 1, h))
    nbt = D // HD
    return _call(body, name=name, grid=(2, R // TR),
                 in_specs=[tile, before, tile, last, tile, after, tile, tile, wspec, wspec, par(2), par(2), par(2)],
                 out_specs=[tile, wspec, wspec, par(2), par(2), par(2)],
                 out_shape=[_sds((R, D)), _sds((2, nbt, HD, HD)), _sds((2, nbt, HD, HD)), _sds((2, D)), _sds((2, D)),
                            _sds((2, D))])(xc, hf, hf, hf, hb, hb, gf, gb, WA, WX, ba, bx, lam)


def _scan_rows(n_groups, step, init):
    return lax.fori_loop(0, n_groups, lambda gi, c: step(pl.multiple_of(gi * 8, 8), c), init)


def _lru_scan_fwd(af, uf, ab, ub, S, name):
    R, D = af.shape
    W = min(SCAN_W, D)
    nm, nx = R // TR, S // TR
    nc = nm - nx
    ng = TR // 8

    def body(af_ref, uf_ref, ab_ref, ub_ref, hf_ref, hb_ref, cf_ref, cb_ref):
        @pl.when(pl.program_id(1) == 0)
        def _():
            cf_ref[...] = jnp.zeros_like(cf_ref)
            cb_ref[...] = jnp.zeros_like(cb_ref)

        def step(base, carry):
            hf, hb = carry
            baseb = pl.multiple_of(TR - 8 - base, 8)
            for r in range(8):
                tf, tb = base + r, baseb + 7 - r
                hf = af_ref[pl.ds(tf, 1), :] * hf + uf_ref[pl.ds(tf, 1), :]
                hf_ref[pl.ds(tf, 1), :] = hf
                hb = ab_ref[pl.ds(tb, 1), :] * hb + ub_ref[pl.ds(tb, 1), :]
                hb_ref[pl.ds(tb, 1), :] = hb
            return hf, hb

        hf, hb = _scan_rows(ng, step, (cf_ref[0:1, :], cb_ref[0:1, :]))
        cf_ref[0:1, :] = hf
        cb_ref[0:1, :] = hb

    fmap = lambda j, s: (jnp.where(s < nc, nx + s, s - nc), j)
    bmap = lambda j, s: (nm - 1 - s, j)
    fs, bs = pl.BlockSpec((TR, W), fmap), pl.BlockSpec((TR, W), bmap)
    return _call(body, name=name, grid=(D // W, nm), in_specs=[fs, fs, bs, bs], out_specs=[fs, bs],
                 out_shape=[_sds((R, D))] * 2, scratch=[pltpu.VMEM((8, W), F32), pltpu.VMEM((8, W), F32)])(af, uf, ab, ub)


def _lru_scan_bwd(af, ab, dhs, S, name):
    R, D = af.shape
    W = min(SCAN_W, D)
    nm, nx = R // TR, S // TR
    ng = TR // 8

    def body(af_ref, dhf_ref, ab_ref, dhb_ref, gf_ref, gb_ref, cf_ref, cb_ref):
        @pl.when(pl.program_id(1) == 0)
        def _():
            cf_ref[...] = jnp.zeros_like(cf_ref)
            cb_ref[...] = jnp.zeros_like(cb_ref)

        def step(base, carry):
            cf, cb = carry
            based = pl.multiple_of(TR - 8 - base, 8)
            for r in range(8):
                tf, tb = based + 7 - r, base + r
                g = dhf_ref[pl.ds(tf, 1), :] + cf
                gf_ref[pl.ds(tf, 1), :] = g
                cf = af_ref[pl.ds(tf, 1), :] * g
                g = dhb_ref[pl.ds(tb, 1), :] + cb
                gb_ref[pl.ds(tb, 1), :] = g
                cb = ab_ref[pl.ds(tb, 1), :] * g
            return cf, cb

        cf, cb = _scan_rows(ng, step, (cf_ref[0:1, :], cb_ref[0:1, :]))
        cf_ref[0:1, :] = cf
        cb_ref[0:1, :] = cb

    fmap = lambda j, s: (jnp.where(s < nx, nx - 1 - s, nm - 1 - (s - nx)), j)
    bmap = lambda j, s: (s, j)
    fs, bs = pl.BlockSpec((TR, W), fmap), pl.BlockSpec((TR, W), bmap)
    return _call(body, name=name, grid=(D // W, nm), in_specs=[fs, fs, bs, bs], out_specs=[fs, bs],
                 out_shape=[_sds((R, D))] * 2, scratch=[pltpu.VMEM((8, W), F32), pltpu.VMEM((8, W), F32)])(af, dhs, ab, dhs)


def _merge_fwd(P, hf, hb, O, S, D, col_lg, name):
    R = P.shape[0]
    CB = D // 2
    nx = S // TR

    def body(lg_ref, ga_ref, gl_ref, hf_ref, hb_ref, o_ref, mix_ref):
        ge, _ = _gelu_and_grad(lg_ref[...])
        lru = (hf_ref[...] + hb_ref[...]) * ge
        mix_ref[...] = (_sigmoid(ga_ref[...]) * o_ref[...] + _sigmoid(gl_ref[...]) * lru).astype(mix_ref.dtype)

    def col(c0):
        return pl.BlockSpec((TR, CB), lambda h, i: (i, c0 + h))

    return _call(body, name=name, grid=(2, R // TR),
                 in_specs=[col(col_lg), col(col_lg + 2), col(col_lg + 4), col(0), col(0),
                           pl.BlockSpec((TR, CB), lambda h, i: (jnp.minimum(i, nx - 1), h))],
                 out_specs=col(0), out_shape=_sds((R, D), _MXU))(P, P, P, hf, hb, O)


def _merge_bwd(dmix, P, hf, hb, O, S, D, col_lg, name, jobs=()):
    R = P.shape[0]
    CB = D // 2
    nx = S // TR

    def body(dm_ref, lg_ref, ga_ref, gl_ref, hf_ref, hb_ref, o_ref, do_ref, dhs_ref, dp_ref, stash, sems):
        h, i = pl.program_id(0), pl.program_id(1)
        dm = jnp.where(i < nx, dm_ref[...], 0.0)
        sa, sl = _sigmoid(ga_ref[...]), _sigmoid(gl_ref[...])
        ge, dge = _gelu_and_grad(lg_ref[...])
        hs = hf_ref[...] + hb_ref[...]
        dl = dm * sl
        do_ref[...] = (dm * sa).astype(do_ref.dtype)
        dhs_ref[...] = dl * ge
        stash[0] = (dl * hs * dge).astype(stash.dtype)
        stash[1] = (dm * o_ref[...] * sa * (1.0 - sa)).astype(stash.dtype)
        stash[2] = (dm * (hs * ge) * sl * (1.0 - sl)).astype(stash.dtype)
        rows = pl.ds(pl.multiple_of(i * TR, TR), TR)
        copies = [pltpu.make_async_copy(stash.at[sec], dp_ref.at[rows, pl.ds(pl.multiple_of((col_lg + 2 * sec + h) * CB, CB), CB)],
                                        sems.at[sec]) for sec in range(3)]
        for cp in copies:
            cp.start()
        for cp in copies:
            cp.wait()

    def col(c0):
        return pl.BlockSpec((TR, CB), lambda h, i: (i, c0 + h))

    xrow = pl.BlockSpec((TR, CB), lambda h, i: (jnp.minimum(i, nx - 1), h))
    return _call(body, jobs=jobs, name=name, grid=(2, R // TR),
                 in_specs=[xrow, col(col_lg), col(col_lg + 2), col(col_lg + 4), col(0), col(0), xrow],
                 out_specs=[col(0), col(0), pl.BlockSpec(memory_space=pl.ANY)],
                 out_shape=[_sds((R, D), _MXU), _sds((R, D)), _sds(P.shape, _MXU)],
                 scratch=[pltpu.VMEM((3, TR, CB), _MXU), pltpu.SemaphoreType.DMA((3,))])(dmix, P, P, P, hf, hb, O)


def _conv_bwd(dxc, P, conv_w, dP, S, D, col0, name, jobs=()):
    R = P.shape[0]
    CB = D // 2

    def body(dp_, d_ref, dn_, xp_ref, x_ref, xn_ref, cw_ref, dp_in, dpo_ref, dcw_ref, dcb_ref):
        i = pl.program_id(1)

        @pl.when(i == 0)
        def _():
            dcw_ref[...] = jnp.zeros_like(dcw_ref)
            dcb_ref[...] = jnp.zeros_like(dcb_ref)

        d = d_ref[...]
        catd = jnp.concatenate([dp_[...], d, dn_[...]], axis=0)
        catx = jnp.concatenate([xp_ref[...], x_ref[...], xn_ref[...]], axis=0)
        from_start, to_end = _seq_pos(i, S, R, CB)
        dxl = (cw_ref[2:3, :] * catd
               + cw_ref[0:1, :] * jnp.where(to_end >= 3, _shift(catd, -2), 0.0)
               + cw_ref[1:2, :] * jnp.where(to_end >= 2, _shift(catd, -1), 0.0)
               + cw_ref[3:4, :] * jnp.where(from_start >= 1, _shift(catd, 1), 0.0))
        dpo_ref[...] = dxl[8:8 + TR, :].astype(dpo_ref.dtype)
        taps = (jnp.where(from_start >= 2, _shift(catx, 2), 0.0), jnp.where(from_start >= 1, _shift(catx, 1), 0.0),
                catx, jnp.where(to_end >= 2, _shift(catx, -1), 0.0))
        for kk in range(4):
            dcw_ref[kk:kk + 1, :] += jnp.sum(d * taps[kk][8:8 + TR, :], axis=0, keepdims=True)
        dcb_ref[...] += jnp.sum(d, axis=0, keepdims=True)

    return _call(body, jobs=jobs, name=name, grid=(2, R // TR),
                 in_specs=_halo_specs(R, CB, 0) + _halo_specs(R, CB, col0)
                 + [pl.BlockSpec((4, CB), lambda h, i: (0, h)), pl.BlockSpec(memory_space=pl.ANY)],
                 out_specs=[pl.BlockSpec((TR, CB), lambda h, i: (i, col0 + h)), pl.BlockSpec((4, CB), lambda h, i: (0, h)),
                            pl.BlockSpec((1, CB), lambda h, i: (0, h))],
                 out_shape=[_sds(dP.shape, dP.dtype), _sds((4, D)), _sds((1, D))],
                 aliases={7: 0})(dxc, dxc, dxc, P, P, P, conv_w, dP)


def _rope_tables(S, C):
    t = jnp.arange(S, dtype=jnp.int32)
    row = (t // GRID_W).astype(F32)
    col = (t % GRID_W).astype(F32)
    axis_dims = HD // 2
    freqs = ROPE_THETA ** (-jnp.arange(0, axis_dims, 2, dtype=F32) / axis_dims)
    ang = jnp.concatenate([row[:, None] * freqs, col[:, None] * freqs], axis=-1)
    cos = jnp.repeat(jnp.cos(ang), 2, axis=-1)
    sin = jnp.repeat(jnp.sin(ang), 2, axis=-1) * jnp.tile(jnp.array([-1.0, 1.0], F32), HD // 2)
    return (jnp.concatenate([cos, jnp.ones((C, HD), F32)], axis=0),
            jnp.concatenate([sin, jnp.zeros((C, HD), F32)], axis=0))


def _local_step(x, ctx, target, modx, modc, ng, shards, qg, kg, conv_w, conv_b, WA, WX, ba, bx, lam, fg, idx, opt):
    S, D = x.shape
    C = ctx.shape[0]
    R = S + C
    nx = S // TR
    Nb = shards['w_in'].shape[-1]
    Fb = shards['wd0'].shape[1]
    KVW = (ND * Nb - 5 * D) // 2
    G = D // KVW
    CB = D // 2
    col_lx = (D + 2 * KVW) // CB
    assert S % TR == 0 and C % TR == 0 and (D + 2 * KVW) % CB == 0 and CB % HD == 0
    mod2 = jnp.stack([modx, modc])
    X0 = jnp.concatenate([x, ctx], axis=0)
    COS, SIN = _rope_tables(S, C)
    ng0, ng1, ng2 = ng[0:1], ng[1:2], ng[2:3]
    ag = lambda n: _ag_job(shards[n])
    sib = lambda Gp: _rs_sibling_job(Gp.reshape(ND, -1, Gp.shape[-1]))
    add = lambda Gp, bufA, tag: _rs_add(Gp.reshape(ND, -1, Gp.shape[-1]), bufA, idx, f"{tag}_rs_add")
    out = {}

    def tn_specs(rows):
        tm = _tm(rows)
        return pl.BlockSpec((None, tm, Fb), lambda d, r: (d, r, 0)), pl.BlockSpec((tm, D), lambda d, r: (r, 0))

    wd_spec = pl.BlockSpec((None, None, Fb, D), lambda d, r: (d, 0, 0, 0))

    ((WG0,),) = _comm_call([ag('wg0')], "ag_wg0")
    H1 = _norm_mod_fwd(X0, ng0, mod2, 0, R, nx, "ffn1_norm")
    G1, ((WU0,),) = _ffn_gate(H1, WG0, R, "ffn1_gate", jobs=[ag('wu0')])
    (U1, A1), ((WD0,),) = _ffn_up(H1, None, WU0, R, "ffn1_up", jobs=[ag('wd0')], G=G1)
    (Y1, X1, H2), ((WIN,),) = _ffn_down(A1, WD0, 0, X0, mod2, 0, R, S, "ffn1_down", jobs=[ag('w_in')], next_norm=(ng1, 1))
    P, ((WOUT,), (WG1,)) = _proj_in(H2, WIN, "proj_in", jobs=[ag('w_out'), ag('wg1')])
    q, k, v = _qkv_prep(P, qg, kg, COS, SIN, D, KVW, "qkv_prep")
    (O, LSE), ((WU1,), (WD1,)) = _attn_fwd(q, k, v, S, G, "attn_fwd", jobs=[ag('wu1'), ag('wd1')])
    WOUT = WOUT.reshape(D, D)
    xc, af, uf, ab, ub = _lru_gates_fwd(P, conv_w, conv_b, WA, WX, ba, bx, lam, S, D, col_lx, "lru_gates")
    hf, hb = _lru_scan_fwd(af, uf, ab, ub, S, "lru_scan")
    mixb = _merge_fwd(P, hf, hb, O, S, D, col_lx + 2, "merge")
    Z, X2, H3 = _proj_out(mixb, WOUT, X1, mod2, ng2, S, "proj_out")
    G3, U3, A3 = _ffn_up(H3, WG1, WU1, S, "ffn2_up")
    loss, dX3, dY3b, dg3, dfg = _ffn_down_loss(A3, WD1, X2, mod2, 2, fg, target, "ffn2_down_loss")
    dg3 = dg3[None]

    dG3, dU3 = _ffn_dact(dY3b, WD1, 0, G3, U3, S, "ffn2_dact")
    blk, row = tn_specs(S)
    dWD1 = _mm_tn(A3, blk, dY3b, row, (ND, 1, Fb, D), wd_spec, S, "ffn2_dwd")
    dWG1 = _mm_tn(dG3, blk, H3, row, (ND, 1, Fb, D), wd_spec, S, "ffn2_dwg")
    dWU1 = _mm_tn(dU3, blk, H3, row, (ND, 1, Fb, D), wd_spec, S, "ffn2_dwu")
    dH3, ((a_wd1,), (a_wg1,), (a_wu1,)) = _ffn_dh(dG3, dU3, WG1, WU1, S, "ffn2_dh", jobs=[sib(dWD1), sib(dWG1), sib(dWU1)])
    T_wd1, own_wd1 = add(dWD1, a_wd1, "wd1")
    T_wg1, own_wg1 = add(dWG1, a_wg1, "wg1")
    T_wu1, own_wu1 = add(dWU1, a_wu1, "wu1")
    dX2, dsh3, dsc3, dng2, dZb, dg2 = _norm_mod_bwd(X2, dH3, dX3, ng2, mod2, 2, S, nx, nx, "ffn2_dnorm", branch=(Z, 1, 1.0))

    dmix = _dproj_out(dZb, WOUT, "dproj_out")
    dWOUT = _mm_tn(mixb, pl.BlockSpec((_tm(S), D // ND), lambda d, r: (r, d)), dZb, pl.BlockSpec((_tm(S), D), lambda d, r: (r, 0)),
                   (ND, D // ND, D), pl.BlockSpec((None, D // ND, D), lambda d, r: (d, 0, 0)), S, "dw_out")
    (dOb, dhs, dP), ((a_wout,),) = _merge_bwd(dmix, P, hf, hb, O, S, D, col_lx + 2, "merge_bwd", jobs=[sib(dWOUT)])
    T_wout, own_wout = add(dWOUT, a_wout, "w_out")
    gf, gb = _lru_scan_bwd(af, ab, dhs, S, "lru_scan_bwd")
    dxc, dWA, dWX, dba, dbx, dlam = _lru_gates_bwd(xc, hf, hb, gf, gb, WA, WX, ba, bx, lam, S, "lru_gates_bwd")
    dLW = jnp.stack([dWA, dWX]).reshape(ND, -1, HD)
    (dP, dconv_w, dconv_b), ((a_lw,),) = _conv_bwd(dxc, P, conv_w, dP, S, D, col_lx, "conv_bwd", jobs=[sib(dLW)])
    T_lw, own_lw = add(dLW, a_lw, "lru_w")
    (dq, dk, dv), ((b_wd1,), (b_wg1,), (b_wu1,), (b_wout,), (b_lw,)) = _attn_bwd(
        q, k, v, O, LSE, dOb, S, G, "attn_bwd",
        jobs=[_rs_chips_job(T_wd1), _rs_chips_job(T_wg1), _rs_chips_job(T_wu1), _rs_chips_job(T_wout), _rs_chips_job(T_lw)])
    fin_wd = _rs_finish(own_wd1, b_wd1, "wd1_rs_finish", opt['ffn_wd'], 1, 2)
    fin_wg = _rs_finish(own_wg1, b_wg1, "wg1_rs_finish", opt['ffn_wg'], 1, 2)
    fin_wu = _rs_finish(own_wu1, b_wu1, "wu1_rs_finish", opt['ffn_wu'], 1, 2)
    out['w_out'] = _rs_finish(own_wout, b_wout, "w_out_rs_finish", opt['w_out'])
    (lw_sum,) = _rs_finish(own_lw, b_lw, "lru_w_rs_finish")
    dP, dqg, dkg = _qkv_bwd(P, dq, dk, dv, qg, kg, COS, SIN, dP, D, KVW, nx, "qkv_bwd")
    dH2, ((lw_full,),) = _dproj_in(dP, WIN, "dproj_in", jobs=[_ag_job(lw_sum)])
    dWIN = _mm_tn(H2, pl.BlockSpec((_tm(R), D), lambda d, r: (r, 0)), dP, pl.BlockSpec((_tm(R), Nb), lambda d, r: (r, d)),
                  (ND, D, Nb), pl.BlockSpec((None, D, Nb), lambda d, r: (d, 0, 0)), R, "dw_in")
    dX1, dsh2, dsc2, dng1, dY1b, dg1 = _norm_mod_bwd(X1, dH2, dX2, ng1, mod2, 1, R, nx, nx, "mix_dnorm", branch=(Y1, 0, 0.5))

    (dG1, dU1), ((a_win,),) = _ffn_dact(dY1b, WD0, 0, G1, U1, R, "ffn1_dact", jobs=[sib(dWIN)])
    T_win, own_win = add(dWIN, a_win, "w_in")
    blk, row = tn_specs(R)
    dWD0, ((b_win01,),) = _mm_tn(A1, blk, dY1b, row, (ND, 1, Fb, D), wd_spec, R, "ffn1_dwd", jobs=[_rs_chips_job(T_win, (0, 1))])
    dWG0, ((b_win2,), (a_wd0,)) = _mm_tn(dG1, blk, H1, row, (ND, 1, Fb, D), wd_spec, R, "ffn1_dwg",
                                         jobs=[_rs_chips_job(T_win, (2,)), sib(dWD0)])
    out['w_in'] = _rs_finish(own_win, [(b_win01, 0), (b_win01, 1), (b_win2, 0)], "w_in_rs_finish", opt['w_in'])
    T_wd0, own_wd0 = add(dWD0, a_wd0, "wd0")
    dWU0, ((a_wg0,), (b_wd0,)) = _mm_tn(dU1, blk, H1, row, (ND, 1, Fb, D), wd_spec, R, "ffn1_dwu",
                                        jobs=[sib(dWG0), _rs_chips_job(T_wd0)])
    T_wg0, own_wg0 = add(dWG0, a_wg0, "wg0")
    out['ffn_wd'] = _rs_finish(own_wd0, b_wd0, "wd0_rs_finish", opt['ffn_wd'], 0, 2, fin_wd)
    dH1, ((a_wu0,), (b_wg0,)) = _ffn_dh(dG1, dU1, WG0, WU0, R, "ffn1_dh", jobs=[sib(dWU0), _rs_chips_job(T_wg0)])
    T_wu0, own_wu0 = add(dWU0, a_wu0, "wu0")
    out['ffn_wg'] = _rs_finish(own_wg0, b_wg0, "wg0_rs_finish", opt['ffn_wg'], 0, 2, fin_wg)
    (dX0, dsh1, dsc1, dng0), ((b_wu0,),) = _norm_mod_bwd(
        X0, dH1, dX1, ng0, mod2, 0, R, nx, R // TR, "ffn1_dnorm", jobs=[_rs_chips_job(T_wu0, (0, 1))])
    out['wu0_pending'] = (T_wu0, own_wu0, b_wu0, fin_wu)
    out['lru_w'] = lw_full

    zero = jnp.zeros((1, D), F32)
    dmodx = jnp.concatenate([dsh1[0], dsc1[0], dg1[0], dsh2[0], dsc2[0], dg2[0], dsh3[0], dsc3[0], dg3[0]], axis=0)
    dmodc = jnp.concatenate([dsh1[1], dsc1[1], dg1[1], dsh2[1], dsc2[1], zero, zero, zero, zero], axis=0)
    out.update(loss=loss, grad_x=dX0[:S], dmodx=dmodx, dmodc=dmodc, norm_g=jnp.concatenate([dng0, dng1, dng2], axis=0),
               q_norm_g=dqg, k_norm_g=dkg, conv_w=dconv_w, conv_b=dconv_b, lru_ba=dba, lru_bx=dbx,
               lru_lambda=dlam, final_norm_g=dfg)
    return out


def _mesh_pos():
    return lax.axis_index("x"), lax.axis_index("y"), lax.axis_index("c")


def _all_gather(xb, name, in_vmem=False):
    space = pltpu.VMEM if in_vmem else pl.ANY

    def body(x_ref, out_ref, send_sems, recv_sems, local_sem):
        x, y, c = _mesh_pos()
        me, sibling = (x, y, c), (x, y, 1 - c)
        chips = [(1 - x, y), (x, 1 - y), (1 - x, 1 - y)]

        def slot(px, py, pc):
            return out_ref.at[4 * px + 2 * py + pc]

        def copy(k, block, to, src=None):
            return pltpu.make_async_remote_copy(
                src_ref=slot(*block) if src is None else src, dst_ref=slot(*block),
                send_sem=send_sems.at[k], recv_sem=recv_sems.at[k], device_id=to, device_id_type=MESH)

        mine = pltpu.make_async_copy(x_ref, slot(*me), local_sem)
        mine.start()
        first = [copy(0, me, sibling, src=x_ref)]
        first += [copy(1 + j, me, (*chip, c), src=x_ref) for j, chip in enumerate(chips)]
        for cp in first:
            cp.start()
        passed = [copy(4 + j, (*chip, c), sibling) for j, chip in enumerate(chips)]
        for j, chip in enumerate(chips):
            copy(1 + j, (*chip, c), me).wait_recv()
            passed[j].start()
        copy(0, sibling, me).wait_recv()
        for j, chip in enumerate(chips):
            copy(4 + j, (*chip, 1 - c), me).wait_recv()
        for cp in first + passed:
            cp.wait_send()
        mine.wait()

    return pl.pallas_call(
        body, name=name, out_shape=_sds((ND,) + xb.shape, xb.dtype),
        in_specs=[pl.BlockSpec(memory_space=space)], out_specs=pl.BlockSpec(memory_space=space),
        scratch_shapes=[pltpu.SemaphoreType.DMA((7,)), pltpu.SemaphoreType.DMA((7,)), pltpu.SemaphoreType.DMA(())])(xb)


def _ag_job(xb):
    def env(ins, outs, sems):
        x_ref, out_ref = ins[0], outs[0]
        send_sems, recv_sems, local_sem = sems
        x, y, c = _mesh_pos()

        def slot(px, py, pc):
            return out_ref.at[4 * px + 2 * py + pc]

        def copy(k, block, to, src=None):
            return pltpu.make_async_remote_copy(
                src_ref=slot(*block) if src is None else src, dst_ref=slot(*block),
                send_sem=send_sems.at[k], recv_sem=recv_sems.at[k], device_id=to, device_id_type=MESH)

        return x_ref, local_sem, slot, copy, (x, y, c)

    def start(ins, outs, sems):
        x_ref, local_sem, slot, copy, (x, y, c) = env(ins, outs, sems)
        me = (x, y, c)
        pltpu.make_async_copy(x_ref, slot(*me), local_sem).start()
        copy(0, me, (x, y, 1 - c), src=x_ref).start()
        copy(1, me, (1 - x, y, c), src=x_ref).start()
        copy(2, me, (x, 1 - y, c), src=x_ref).start()

    def relay(ins, outs, sems):
        _, _, _, copy, (x, y, c) = env(ins, outs, sems)
        xn, yn = (1 - x, y, c), (x, 1 - y, c)
        copy(1, xn, (x, y, c)).wait_recv()
        copy(2, yn, (x, y, c)).wait_recv()
        src = (jnp.where(c == 0, x, 1 - x), jnp.where(c == 0, 1 - y, y), c)
        dst = (jnp.where(c == 0, 1 - x, x), jnp.where(c == 0, y, 1 - y), c)
        copy(3, src, dst).start()
        copy(4, xn, (x, y, 1 - c)).start()
        copy(5, yn, (x, y, 1 - c)).start()

    def relay2(ins, outs, sems):
        _, _, _, copy, (x, y, c) = env(ins, outs, sems)
        dg = (1 - x, 1 - y, c)
        copy(3, dg, (x, y, c)).wait_recv()
        copy(6, dg, (x, y, 1 - c)).start()

    def finish(ins, outs, sems):
        x_ref, local_sem, slot, copy, (x, y, c) = env(ins, outs, sems)
        me, sib = (x, y, c), (x, y, 1 - c)
        copy(0, sib, me).wait_recv()
        for k, chip in ((4, (1 - x, y)), (5, (x, 1 - y)), (6, (1 - x, 1 - y))):
            copy(k, (*chip, 1 - c), me).wait_recv()
        for k in range(7):
            copy(k, me, me).wait_send()
        pltpu.make_async_copy(x_ref, slot(*me), local_sem).wait()

    return _Job([xb], [_sds((ND,) + xb.shape, xb.dtype)],
                [pltpu.SemaphoreType.DMA((7,)), pltpu.SemaphoreType.DMA((7,)), pltpu.SemaphoreType.DMA(())],
                start, finish, relay, relay2)


def _rs_sibling_job(Gp):
    def copies(ins, outs, sems):
        x, y, c = _mesh_pos()
        return [pltpu.make_async_remote_copy(
            src_ref=ins[0].at[2 * k + (1 - c)], dst_ref=outs[0].at[k], send_sem=sems[0].at[k], recv_sem=sems[1].at[k],
            device_id=(x, y, 1 - c), device_id_type=MESH) for k in range(4)]

    def start(ins, outs, sems):
        for cp in copies(ins, outs, sems):
            cp.start()

    def finish(ins, outs, sems):
        cps = copies(ins, outs, sems)
        for cp in cps:
            cp.wait_recv()
        for cp in cps:
            cp.wait_send()

    return _Job([Gp], [_sds((4,) + Gp.shape[1:], Gp.dtype)],
                [pltpu.SemaphoreType.DMA((4,)), pltpu.SemaphoreType.DMA((4,))], start, finish)


def _rs_chips_job(T, dests=(0, 1, 2)):
    def copies(ins, outs, sems):
        x, y, c = _mesh_pos()
        chips = [(1 - x, y), (x, 1 - y), (1 - x, 1 - y)]
        cps = []
        for i, j in enumerate(dests):
            px, py = chips[j]
            cps.append(pltpu.make_async_remote_copy(
                src_ref=ins[0].at[2 * px + py], dst_ref=outs[0].at[i], send_sem=sems[0].at[i], recv_sem=sems[1].at[i],
                device_id=(px, py, c), device_id_type=MESH))
        return cps

    def start(ins, outs, sems):
        for cp in copies(ins, outs, sems):
            cp.start()

    def finish(ins, outs, sems):
        cps = copies(ins, outs, sems)
        for cp in cps:
            cp.wait_recv()
        for cp in cps:
            cp.wait_send()

    n = len(dests)
    return _Job([T], [_sds((n,) + T.shape[1:], T.dtype)],
                [pltpu.SemaphoreType.DMA((n,)), pltpu.SemaphoreType.DMA((n,))], start, finish)


def _tile_rows(rows, cols):
    best = None
    for t in range(16, rows + 1, 16):
        if rows % t == 0 and t * cols * 4 <= (3 << 19):
            best = t
    return best if best is not None else rows


def _prefetch_call(body, *, name, grid, in_specs, out_specs, out_shape):
    return pl.pallas_call(
        body, name=name, out_shape=out_shape,
        grid_spec=pltpu.PrefetchScalarGridSpec(num_scalar_prefetch=1, grid=grid, in_specs=in_specs, out_specs=out_specs),
        compiler_params=pltpu.CompilerParams(dimension_semantics=("arbitrary",) * len(grid), vmem_limit_bytes=VMEM_LIMIT))


def _rs_add(Gp, bufA, idx, name):
    rows, cols = Gp.shape[1:]
    tr = _tile_rows(rows, cols)

    def body(i_ref, g_ref, a_ref, t_ref, own_ref):
        t = g_ref[...] + a_ref[...]
        t_ref[...] = t.astype(t_ref.dtype)

        @pl.when(pl.program_id(1) == i_ref[1])
        def _():
            own_ref[...] = t

    return _prefetch_call(
        body, name=name, grid=(rows // tr, 4),
        in_specs=[pl.BlockSpec((None, tr, cols), lambda r, k, i_ref: (2 * k + i_ref[0], r, 0)),
                  pl.BlockSpec((None, tr, cols), lambda r, k, i_ref: (k, r, 0))],
        out_specs=[pl.BlockSpec((None, tr, cols), lambda r, k, i_ref: (k, r, 0)),
                   pl.BlockSpec((tr, cols), lambda r, k, i_ref: (r, 0))],
        out_shape=[_sds((4, rows, cols), jnp.bfloat16), _sds((rows, cols))])(idx, Gp, bufA)


def _adam(w, g, m, v):
    m = ADAM_B1 * m + (1.0 - ADAM_B1) * g
    v = ADAM_B2 * v + (1.0 - ADAM_B2) * (g * g)
    m_hat = m / (1.0 - ADAM_B1 ** ADAM_STEP)
    v_hat = v / (1.0 - ADAM_B2 ** ADAM_STEP)
    return -ADAM_LR * (m_hat / (jnp.sqrt(v_hat) + ADAM_EPS) + ADAM_WD * w), m, v


def _rs_finish(Town, bufB, name, wmv=None, slab=0, n_slabs=1, prev=None):
    rows, cols = Town.shape
    tr = _tile_rows(rows, cols)
    nr = rows // tr
    n_in = 4 + (3 if wmv is not None else 0)
    n_out = 4 if wmv is not None else 1

    def body(*refs):
        ins, outs = refs[:n_in], refs[len(refs) - n_out:]
        g = ((ins[0][...] + ins[1][...].astype(F32)) + ins[2][...].astype(F32)) + ins[3][...].astype(F32)
        outs[0][...] = g
        if wmv is not None:
            d, m, v = _adam(ins[4][...], g, ins[5][...], ins[6][...])
            outs[1][...] = d
            outs[2][...] = m
            outs[3][...] = v

    plain = pl.BlockSpec((tr, cols), lambda r: (r, 0))
    slabbed = pl.BlockSpec((tr, cols), lambda r: (slab * nr + r, 0))
    pairs = bufB if isinstance(bufB, list) else [(bufB, j) for j in range(3)]
    in_specs = [plain] + [pl.BlockSpec((None, tr, cols), (lambda j: lambda r: (j, r, 0))(j)) for _, j in pairs]
    args = [Town] + [a for a, _ in pairs]
    if wmv is not None:
        in_specs += [slabbed] * 3
        args += list(wmv)
    aliases = None
    if prev is not None:
        in_specs += [pl.BlockSpec(memory_space=pl.ANY)] * n_out
        aliases = {len(args) + i: i for i in range(n_out)}
        args += list(prev)
    return _call(body, name=name, grid=(nr,), in_specs=in_specs, out_specs=[slabbed] * n_out,
                 out_shape=[_sds((n_slabs * rows, cols))] * n_out, aliases=aliases)(*args)


def _adamw_plain(w, g, m, v, name, jobs=()):
    rows, cols = w.shape
    tr = _tile_rows(rows, cols)

    def body(w_ref, g_ref, m_ref, v_ref, d_ref, mo_ref, vo_ref):
        d, m_, v_ = _adam(w_ref[...], g_ref[...], m_ref[...], v_ref[...])
        d_ref[...] = d
        mo_ref[...] = m_
        vo_ref[...] = v_

    spec = pl.BlockSpec((tr, cols), lambda r: (r, 0))
    return _call(body, jobs=jobs, name=name, grid=(rows // tr,), in_specs=[spec] * 4, out_specs=[spec] * 3,
                 out_shape=[_sds((rows, cols))] * 3)(w, g, m, v)


_MOD_TK = 512


def _mod_fwd(cc16, w_loc, b_loc, name):
    D, cols = w_loc.shape
    tk = min(_MOD_TK, D)
    nk = D // tk

    def body(c_ref, w_ref, b_ref, o_ref):
        kk = pl.program_id(0)

        @pl.when(kk == 0)
        def _():
            o_ref[...] = jnp.zeros_like(o_ref)

        cc = c_ref[...]
        o_ref[...] += _dot(cc * _sigmoid(cc), w_ref[...])

        @pl.when(kk == nk - 1)
        def _():
            o_ref[...] += b_ref[...]

    return _call(body, name=name, grid=(nk,),
                 in_specs=[pl.BlockSpec((16, tk), lambda kk: (0, kk)), pl.BlockSpec((tk, cols), lambda kk: (kk, 0)),
                           _full_spec((1, cols))],
                 out_specs=_full_spec((16, cols)), out_shape=_sds((16, cols)))(cc16, w_loc, b_loc)


def _mod_bwd(dm_loc, cc16, w_loc, name):
    D, cols = w_loc.shape

    def body(dm_ref, c_ref, w_ref, gw_ref, ds_ref):
        rows = [dm_ref[b, 0:1, :] for b in range(ND)]
        ctx = dm_ref[0, 1:2, :]
        for b in range(1, ND):
            ctx = ctx + dm_ref[b, 1:2, :]
        dm16 = jnp.concatenate(rows + [ctx, jnp.zeros((7, cols), F32)], axis=0)
        cc = c_ref[...]
        gw_ref[...] = _dot_tn(cc * _sigmoid(cc), dm16)
        ds_ref[...] = _dot_nt(dm16, w_ref[...])

    tk = min(_MOD_TK, D)
    return _call(body, name=name, grid=(D // tk,),
                 in_specs=[_full_spec((ND, 8, cols)), pl.BlockSpec((16, tk), lambda kk: (0, kk)),
                           pl.BlockSpec((tk, cols), lambda kk: (kk, 0))],
                 out_specs=[pl.BlockSpec((tk, cols), lambda kk: (kk, 0)), pl.BlockSpec((16, tk), lambda kk: (0, kk))],
                 out_shape=[_sds((D, cols)), _sds((16, D))])(dm_loc, cc16, w_loc)


def _bmod_grad(dm_all, name):
    n = dm_all.shape[-1]

    def body(dm_ref, o_ref):
        acc = dm_ref[0, 0:1, :] + dm_ref[0, 1:2, :]
        for b in range(1, ND):
            acc = (acc + dm_ref[b, 0:1, :]) + dm_ref[b, 1:2, :]
        o_ref[...] = acc

    return _call(body, name=name, grid=(1,), in_specs=[_full_spec((ND, 8, n))], out_specs=_full_spec((1, n)),
                 out_shape=_sds((1, n)))(dm_all)


_SMALL_ROWS = 24
_ROW_CCTX = 15


def _small_finish(parts, c_ctx, name):
    D = parts.shape[-1]

    def body(p_ref, c_ref, o_ref):
        acc = p_ref[0]
        for b in range(1, ND):
            acc = acc + p_ref[b]
        cc = c_ref[...]
        sg = _sigmoid(cc)
        dsilu = sg * (1.0 + cc * (1.0 - sg))
        row = lax.broadcasted_iota(jnp.int32, acc.shape, 0)
        o_ref[...] = jnp.where(row == _ROW_CCTX, acc * dsilu, acc)

    return _call(body, name=name, grid=(1,), in_specs=[_full_spec(parts.shape), _full_spec((1, D))],
                 out_specs=_full_spec((_SMALL_ROWS, D)), out_shape=_sds((_SMALL_ROWS, D)))(parts, c_ctx)


_WEIGHTS = ['c_ctx', 'w_mod', 'b_mod', 'norm_g', 'ffn_wg', 'ffn_wu', 'ffn_wd', 'w_in', 'w_out', 'q_norm_g', 'k_norm_g',
            'conv_w', 'conv_b', 'lru_wa', 'lru_ba', 'lru_wx', 'lru_bx', 'lru_lambda', 'final_norm_g']
_SMALL = ['c_ctx', 'b_mod', 'norm_g', 'q_norm_g', 'k_norm_g', 'conv_w', 'conv_b', 'lru_ba', 'lru_bx', 'lru_lambda',
          'final_norm_g']


def _pad_rows(a, rows):
    return jnp.pad(a, ((0, rows - a.shape[0]),) + ((0, 0),) * (a.ndim - 1))


def _step(w, m, v, x, c, ctx, loss_target):
    xi, yi, ci = _mesh_pos()
    me = 4 * xi + 2 * yi + ci
    idx = jnp.stack([ci, 2 * xi + yi]).astype(jnp.int32)
    S, D = x.shape[1:]
    Ds = D // ND
    cols = w['w_mod'].shape[-1]

    sp = jnp.concatenate([w['norm_g'][0], w['conv_w'][0], w['lru_ba'][0], w['lru_bx'][0], w['lru_lambda'][0]], axis=0)
    spg = _all_gather(_pad_rows(sp, 16), "ag_small_params", in_vmem=True)
    spf = jnp.transpose(spg, (1, 0, 2)).reshape(16, D)
    ng, conv_w, ba, bx, lam = spf[0:3], spf[3:7], spf[7:9], spf[9:11], spf[11:13]

    cg = _all_gather(_pad_rows(c, 8), "ag_cond", in_vmem=True)
    cc16 = _pad_rows(jnp.concatenate([cg[:, 0, :], w['c_ctx'][None, :]], axis=0), 16)
    b_loc = lax.dynamic_slice_in_dim(w['b_mod'], me * cols, cols, axis=1)
    mod_loc = _mod_fwd(cc16, w['w_mod'][0], b_loc, "mod_fwd")
    modg = _all_gather(mod_loc, "ag_mod", in_vmem=True)
    mod16 = jnp.transpose(modg, (1, 0, 2)).reshape(16, ND * cols)
    modx = lax.dynamic_index_in_dim(mod16, me, axis=0, keepdims=False).reshape(9, D)
    modc = mod16[8].reshape(9, D)

    shards = {'w_in': w['w_in'][0].astype(_MXU), 'w_out': w['w_out'][0].astype(_MXU)}
    for layer in range(2):
        shards[f'wg{layer}'] = w['ffn_wg'][0, layer].T.astype(_MXU)[None]
        shards[f'wu{layer}'] = w['ffn_wu'][0, layer].T.astype(_MXU)[None]
        shards[f'wd{layer}'] = w['ffn_wd'][0, layer].astype(_MXU)[None]
    WA, WX = w['lru_wa'][0].astype(_MXU), w['lru_wx'][0].astype(_MXU)
    big = ('ffn_wg', 'ffn_wu', 'ffn_wd', 'w_in', 'w_out')
    transposed = ('ffn_wg', 'ffn_wu')
    tr_view = lambda a: jnp.swapaxes(a, -1, -2)
    opt = {n: tuple((tr_view(a[n]) if n in transposed else a[n]) for a in (w, m, v)) for n in big}
    opt = {n: tuple(a.reshape(-1, a.shape[-1]) for a in t) for n, t in opt.items()}

    g = _local_step(x[0], ctx[0], loss_target[0], modx, modc, ng, shards, w['q_norm_g'], w['k_norm_g'],
                    conv_w, w['conv_b'], WA, WX, ba, bx, lam, w['final_norm_g'][None, :], idx, opt)

    grad, delta, new_m, new_v = {}, {}, {}, {}

    lfull = g['lru_w'].reshape((2,) + w['lru_wa'].shape[1:])
    for i, n in enumerate(('lru_wa', 'lru_wx')):
        shard = w[n].shape
        view = lambda a: a.reshape(-1, HD)
        grad[n] = lfull[i].reshape(shard)
        outs = _adamw_plain(view(w[n]), view(lfull[i]), view(m[n]), view(v[n]), f"adamw_{n}")
        delta[n], new_m[n], new_v[n] = [o.reshape(shard) for o in outs]

    dm = _pad_rows(jnp.stack([g['dmodx'].reshape(-1), g['dmodc'].reshape(-1)]), 8)
    dm_all = _all_gather(dm, "ag_dmod", in_vmem=True)
    dm_loc = lax.dynamic_slice_in_dim(dm_all, me * cols, cols, axis=2)
    gw_mod, dsil = _mod_bwd(dm_loc, cc16, w['w_mod'][0], "mod_bwd")
    grad['w_mod'] = gw_mod[None]
    T_wu0, own_wu0, b_wu0, fin_wu = g['wu0_pending']
    outs, ((b_wu0d,),) = _adamw_plain(w['w_mod'][0], gw_mod, m['w_mod'][0], v['w_mod'][0], "adamw_w_mod",
                                      jobs=[_rs_chips_job(T_wu0, (2,))])
    delta['w_mod'], new_m['w_mod'], new_v['w_mod'] = [o[None] for o in outs]
    g['ffn_wu'] = _rs_finish(own_wu0, [(b_wu0, 0), (b_wu0, 1), (b_wu0d, 0)], "wu0_rs_finish", opt['ffn_wu'], 0, 2, fin_wu)
    for n in big:
        if n in transposed:
            shape_t = w[n].shape[:-2] + (w[n].shape[-1], w[n].shape[-2])
            grad[n], delta[n], new_m[n], new_v[n] = [tr_view(o.reshape(shape_t)) for o in g[n]]
        else:
            grad[n], delta[n], new_m[n], new_v[n] = [o.reshape(w[n].shape) for o in g[n]]
    grad['b_mod'] = _bmod_grad(dm_all, "bmod_grad")

    pad_d = lambda a: jnp.concatenate([a, jnp.zeros((1, D - a.shape[1]), F32)], axis=1)
    small = jnp.concatenate([g['norm_g'], g['conv_w'], g['conv_b'], g['lru_ba'], g['lru_bx'], g['lru_lambda'],
                             g['final_norm_g'], dsil[8:9], pad_d(g['q_norm_g']), pad_d(g['k_norm_g'])], axis=0)
    parts = _all_gather(_pad_rows(small, _SMALL_ROWS), "ag_small_grads", in_vmem=True)
    ssum = _small_finish(parts, w['c_ctx'][None, :], "small_finish")
    mine = lambda rows: lax.dynamic_slice_in_dim(rows, me * Ds, Ds, axis=1)
    grad['norm_g'] = mine(ssum[0:3])[None]
    grad['conv_w'] = mine(ssum[3:7])[None]
    grad['conv_b'] = ssum[7:8]
    grad['lru_ba'] = mine(ssum[8:10])[None]
    grad['lru_bx'] = mine(ssum[10:12])[None]
    grad['lru_lambda'] = mine(ssum[12:14])[None]
    grad['final_norm_g'] = ssum[14]
    grad['c_ctx'] = ssum[_ROW_CCTX]
    grad['q_norm_g'] = ssum[16:17, :HD]
    grad['k_norm_g'] = ssum[17:18, :HD]

    def pack(d):
        flat = jnp.concatenate([d[n].reshape(-1) for n in _SMALL])
        padded = -(-flat.shape[0] // 1024) * 1024
        return jnp.concatenate([flat, jnp.zeros((padded - flat.shape[0],), F32)]).reshape(-1, HD)

    outs = _adamw_plain(pack(w), pack(grad), pack(m), pack(v), "adamw_small")
    off = 0
    for n in _SMALL:
        size = math.prod(w[n].shape)
        for dst, o in zip((delta, new_m, new_v), outs):
            dst[n] = o.reshape(-1)[off:off + size].reshape(w[n].shape)
        off += size

    loss = lax.psum(g['loss'][0, 0], ("x", "y", "c"))
    return (loss, g['grad_x'][None], *[grad[n] for n in _WEIGHTS], *[delta[n] for n in _WEIGHTS],
            *[new_m[n] for n in _WEIGHTS], *[new_v[n] for n in _WEIGHTS])


def kernel(x, c, ctx, c_ctx, w_mod, b_mod, norm_g, ffn_wg, ffn_wu, ffn_wd, w_in, w_out, q_norm_g, k_norm_g, conv_w, conv_b, lru_wa, lru_ba, lru_wx, lru_bx, lru_lambda, final_norm_g, loss_target, m_c_ctx, m_w_mod, m_b_mod, m_norm_g, m_ffn_wg, m_ffn_wu, m_ffn_wd, m_w_in, m_w_out, m_q_norm_g, m_k_norm_g, m_conv_w, m_conv_b, m_lru_wa, m_lru_ba, m_lru_wx, m_lru_bx, m_lru_lambda, m_final_norm_g, v_c_ctx, v_w_mod, v_b_mod, v_norm_g, v_ffn_wg, v_ffn_wu, v_ffn_wd, v_w_in, v_w_out, v_q_norm_g, v_k_norm_g, v_conv_w, v_conv_b, v_lru_wa, v_lru_ba, v_lru_wx, v_lru_bx, v_lru_lambda, v_final_norm_g):
    given = dict(locals())
    w = {n: given[n] for n in _WEIGHTS}
    m = {n: given["m_" + n] for n in _WEIGHTS}
    v = {n: given["v_" + n] for n in _WEIGHTS}
    return _step(w, m, v, x, c, ctx, loss_target)
```

```python
import functools
import math

import jax
import jax.numpy as jnp
from jax import lax
from jax.experimental import pallas as pl
from jax.experimental.pallas import tpu as pltpu

F32 = jnp.float32
_MXU = jnp.bfloat16
ND = 8
TR = 256
HD = 128
EPS = 1e-6
GRID_W = 64
ROPE_THETA = 10000.0
LRU_C = 8.0
VMEM_LIMIT = 56 * 1024 * 1024
SCAN_W = 2048
ADAM_LR, ADAM_B1, ADAM_B2, ADAM_EPS, ADAM_WD, ADAM_STEP = 0.001, 0.9, 0.999, 1e-08, 0.01, 10
MESH = pl.DeviceIdType.MESH


class _Job:
    def __init__(self, inputs, out_shapes, sems, start, finish, relay=None, relay2=None):
        self.inputs, self.out_shapes, self.sems, self.start, self.finish = inputs, out_shapes, sems, start, finish
        self.relay, self.relay2 = relay, relay2


def _call(body, *, name, grid, in_specs, out_specs, out_shape, scratch=(), aliases=None, jobs=()):
    params = pltpu.CompilerParams(dimension_semantics=("arbitrary",) * len(grid), vmem_limit_bytes=VMEM_LIMIT)
    if not jobs:
        return pl.pallas_call(body, name=name, grid=grid, in_specs=in_specs, out_specs=out_specs, out_shape=out_shape,
                              scratch_shapes=scratch, input_output_aliases=aliases or {}, compiler_params=params)
    single = not isinstance(out_specs, (list, tuple))
    o_specs = [out_specs] if single else list(out_specs)
    o_shape = [out_shape] if single else list(out_shape)
    n_in, n_out, n_scr = len(in_specs), len(o_specs), len(scratch)
    j_in = [a for j in jobs for a in j.inputs]
    j_out = [s for j in jobs for s in j.out_shapes]
    j_sem = [s for j in jobs for s in j.sems]
    hbm = pl.BlockSpec(memory_space=pl.ANY)

    def wrapped(*refs):
        ins, rest = refs[:n_in], refs[n_in:]
        jin, rest = rest[:len(j_in)], rest[len(j_in):]
        outs, rest = rest[:n_out], rest[n_out:]
        jout, rest = rest[:len(j_out)], rest[len(j_out):]
        scr, jsem = rest[:n_scr], rest[n_scr:]
        first = functools.reduce(jnp.logical_and, [pl.program_id(a) == 0 for a in range(len(grid))])
        last = functools.reduce(jnp.logical_and, [pl.program_id(a) == grid[a] - 1 for a in range(len(grid))])
        def at_outer(step):
            return functools.reduce(jnp.logical_and, [pl.program_id(a) == (0 if a else step) for a in range(len(grid))])

        def each(which):
            i = o = s = 0
            for j in jobs:
                ni, no, ns = len(j.inputs), len(j.out_shapes), len(j.sems)
                if getattr(j, which, None) is not None:
                    getattr(j, which)(jin[i:i + ni], jout[o:o + no], jsem[s:s + ns])
                i, o, s = i + ni, o + no, s + ns

        @pl.when(first)
        def _():
            each("start")

        relay_early = grid[0] >= 4
        if relay_early:
            @pl.when(at_outer(grid[0] // 2))
            def _():
                each("relay")

            @pl.when(at_outer((7 * grid[0]) // 8))
            def _():
                each("relay2")

        body(*ins, *outs, *scr)

        @pl.when(last)
        def _():
            if not relay_early:
                each("relay")
                each("relay2")
            each("finish")

    call = pl.pallas_call(wrapped, name=name, grid=grid, in_specs=list(in_specs) + [hbm] * len(j_in),
                          out_specs=o_specs + [hbm] * len(j_out), out_shape=o_shape + j_out,
                          scratch_shapes=list(scratch) + j_sem, input_output_aliases=aliases or {}, compiler_params=params)

    def run(*args):
        res = call(*args, *j_in)
        comp = res[0] if single else list(res[:n_out])
        jres, o = [], n_out
        for j in jobs:
            jres.append(list(res[o:o + len(j.out_shapes)]))
            o += len(j.out_shapes)
        return comp, jres

    return run


def _comm_call(jobs, name):
    j_in = [a for j in jobs for a in j.inputs]
    j_out = [s for j in jobs for s in j.out_shapes]
    j_sem = [s for j in jobs for s in j.sems]
    hbm = pl.BlockSpec(memory_space=pl.ANY)

    def body(*refs):
        jin, jout, jsem = refs[:len(j_in)], refs[len(j_in):len(j_in) + len(j_out)], refs[len(j_in) + len(j_out):]
        for which in ("start", "relay", "relay2", "finish"):
            i = o = s = 0
            for j in jobs:
                ni, no, ns = len(j.inputs), len(j.out_shapes), len(j.sems)
                if getattr(j, which, None) is not None:
                    getattr(j, which)(jin[i:i + ni], jout[o:o + no], jsem[s:s + ns])
                i, o, s = i + ni, o + no, s + ns

    res = pl.pallas_call(body, name=name, out_shape=j_out, in_specs=[hbm] * len(j_in), out_specs=[hbm] * len(j_out),
                         scratch_shapes=j_sem)(*j_in)
    jres, o = [], 0
    for j in jobs:
        jres.append(list(res[o:o + len(j.out_shapes)]))
        o += len(j.out_shapes)
    return jres


def _sds(shape, dtype=F32):
    return jax.ShapeDtypeStruct(tuple(shape), dtype)


def _dot(a, b):
    return jnp.dot(a.astype(_MXU), b.astype(_MXU), preferred_element_type=F32)


def _dot_nt(a, b):
    return lax.dot_general(a.astype(_MXU), b.astype(_MXU), (((1,), (1,)), ((), ())), preferred_element_type=F32)


def _dot_tn(a, b):
    return lax.dot_general(a.astype(_MXU), b.astype(_MXU), (((0,), (0,)), ((), ())), preferred_element_type=F32)


def _sigmoid(x):
    return 0.5 * jnp.tanh(0.5 * x) + 0.5


_GELU_C = math.sqrt(2.0 / math.pi)


def _gelu_and_grad(x):
    x2 = x * x
    t = jnp.tanh(_GELU_C * (x + 0.044715 * x * x2))
    ge = 0.5 * x * (1.0 + t)
    dge = 0.5 * (1.0 + t) + 0.5 * x * (1.0 - t * t) * (_GELU_C * (1.0 + 3.0 * 0.044715 * x2))
    return ge, dge


def _tm(rows):
    assert rows % 4 == 0 and (rows // 4) % 16 == 0
    return rows // 4


def _row_spec(width, nmax=None):
    if nmax is None:
        return pl.BlockSpec((TR, width), lambda i: (i, 0))
    return pl.BlockSpec((TR, width), lambda i: (jnp.minimum(i, nmax), 0))


def _full_spec(shape):
    n = len(shape)
    return pl.BlockSpec(tuple(shape), lambda *_: (0,) * n)


def _mod_spec(D, nx):
    return pl.BlockSpec((None, 9, D), lambda i: (i // nx, 0, 0))


def _norm_mod_fwd(X, ng, mod2, k, rows, nx, name):
    D = X.shape[1]

    def body(x_ref, g_ref, mod_ref, h_ref):
        x = x_ref[...]
        r = lax.rsqrt(jnp.mean(x * x, axis=-1, keepdims=True) + EPS)
        n = (x * r) * g_ref[...]
        h_ref[...] = (n * (1.0 + mod_ref[3 * k + 1:3 * k + 2, :]) + mod_ref[3 * k:3 * k + 1, :]).astype(h_ref.dtype)

    return _call(body, name=name, grid=(rows // TR,),
                 in_specs=[_row_spec(D), _full_spec((1, D)), _mod_spec(D, nx)],
                 out_specs=_row_spec(D), out_shape=_sds((rows, D), _MXU))(X, ng, mod2)


def _norm_mod_bwd(X, dH, dXres, ng, mod2, k, rows, nx, res_tiles, name, jobs=(), branch=None):
    D = X.shape[1]
    ngroups = -(-(rows // TR) // nx)
    n_in = 5 + (1 if branch is not None else 0)

    def body(*refs):
        x_ref, dh_ref, dres_ref, g_ref, mod_ref = refs[:5]
        dx_ref, dsh_ref, dsc_ref, dng_ref = refs[n_in:n_in + 4]
        i = pl.program_id(0)
        x = x_ref[...]
        dh = dh_ref[...]
        g = g_ref[...]
        r = lax.rsqrt(jnp.mean(x * x, axis=-1, keepdims=True) + EPS)
        xh = x * r
        n = xh * g
        dn_mod = dh * (1.0 + mod_ref[3 * k + 1:3 * k + 2, :])

        @pl.when(i % nx == 0)
        def _():
            dsh_ref[...] = jnp.zeros_like(dsh_ref)
            dsc_ref[...] = jnp.zeros_like(dsc_ref)
            if branch is not None:
                refs[n_in + 5][...] = jnp.zeros_like(refs[n_in + 5])

        @pl.when(i == 0)
        def _():
            dng_ref[...] = jnp.zeros_like(dng_ref)

        dsh_ref[...] += jnp.sum(dh, axis=0, keepdims=True)
        dsc_ref[...] += jnp.sum(dh * n, axis=0, keepdims=True)
        dng_ref[...] += jnp.sum(dn_mod * xh, axis=0, keepdims=True)
        dn = dn_mod * g
        dres = jnp.where(i < res_tiles, dres_ref[...], 0.0)
        dx = r * (dn - xh * jnp.mean(dn * xh, axis=-1, keepdims=True)) + dres
        dx_ref[...] = dx
        if branch is not None:
            _, k2, coef = branch
            refs[n_in + 4][...] = ((coef * mod_ref[3 * k2 + 2:3 * k2 + 3, :]) * dx).astype(refs[n_in + 4].dtype)
            refs[n_in + 5][...] += jnp.sum(coef * dx * refs[5][...], axis=0, keepdims=True)

    grp = pl.BlockSpec((None, 1, D), lambda i: (i // nx, 0, 0))
    in_specs = [_row_spec(D), _row_spec(D), _row_spec(D, res_tiles - 1), _full_spec((1, D)), _mod_spec(D, nx)]
    out_specs = [_row_spec(D), grp, grp, _full_spec((1, D))]
    out_shape = [_sds((rows, D)), _sds((ngroups, 1, D)), _sds((ngroups, 1, D)), _sds((1, D))]
    args = [X, dH, dXres, ng, mod2]
    if branch is not None:
        in_specs.append(_row_spec(D))
        args.append(branch[0])
        out_specs += [_row_spec(D), grp]
        out_shape += [_sds((rows, D), _MXU), _sds((ngroups, 1, D))]
    return _call(body, jobs=jobs, name=name, grid=(rows // TR,), in_specs=in_specs, out_specs=out_specs,
                 out_shape=out_shape)(*args)


def _ffn_gate(H, WG, rows, name, jobs=()):
    D = H.shape[1]
    Fb = WG.shape[-2]

    def body(h_ref, w_ref, g_ref):
        g_ref[...] = _dot_nt(h_ref[...], w_ref[...])

    tm = _tm(rows)
    return _call(body, jobs=jobs, name=name, grid=(ND, rows // tm),
                 in_specs=[pl.BlockSpec((tm, D), lambda d, m: (m, 0)), pl.BlockSpec((None, None, Fb, D), lambda d, m: (d, 0, 0, 0))],
                 out_specs=pl.BlockSpec((None, tm, Fb), lambda d, m: (d, m, 0)), out_shape=_sds((ND, rows, Fb)))(H, WG)


def _ffn_up(H, WG, WU, rows, name, jobs=(), G=None):
    D = H.shape[1]
    Fb = WU.shape[-2]

    def body(h_ref, w_ref, x_ref, *outs):
        h = h_ref[...]
        g = x_ref[...] if G is not None else _dot_nt(h, x_ref[...])
        u = _dot_nt(h, w_ref[...])
        if G is None:
            outs[0][...] = g
        outs[-2][...] = u
        outs[-1][...] = ((g * _sigmoid(g)) * u).astype(outs[-1].dtype)

    tm = _tm(rows)
    blk = pl.BlockSpec((None, tm, Fb), lambda d, m: (d, m, 0))
    wspec = pl.BlockSpec((None, None, Fb, D), lambda d, m: (d, 0, 0, 0))
    f32o, bfo = _sds((ND, rows, Fb)), _sds((ND, rows, Fb), _MXU)
    return _call(body, jobs=jobs, name=name, grid=(ND, rows // tm),
                 in_specs=[pl.BlockSpec((tm, D), lambda d, m: (m, 0)), wspec, blk if G is not None else wspec],
                 out_specs=[blk, blk] if G is not None else [blk, blk, blk],
                 out_shape=[f32o, bfo] if G is not None else [f32o, f32o, bfo])(H, WU, G if G is not None else WG)


def _ffn_down(A, WD, layer, X, mod2, k, rows, S, name, jobs=(), next_norm=None):
    Fb, D = WD.shape[-2:]
    tm = _tm(rows) // 2

    def body(a_ref, w_ref, x_ref, mod_ref, *rest):
        if next_norm is not None:
            g_ref, y_ref, xn_ref, h_ref, acc_ref = rest
        else:
            y_ref, xn_ref, acc_ref = rest
        d = pl.program_id(1)

        @pl.when(d == 0)
        def _():
            acc_ref[...] = jnp.zeros_like(acc_ref)

        acc_ref[...] += _dot(a_ref[...], w_ref[...])

        @pl.when(d == ND - 1)
        def _():
            y = acc_ref[...]
            y_ref[...] = y
            is_ctx = (pl.program_id(0) * tm + lax.broadcasted_iota(jnp.int32, (tm, 1), 0)) >= S

            def mod_row(j):
                return jnp.where(is_ctx, mod_ref[1, j:j + 1, :], mod_ref[0, j:j + 1, :])

            xn = x_ref[...] + (0.5 * mod_row(3 * k + 2)) * y
            xn_ref[...] = xn
            if next_norm is not None:
                k2 = next_norm[1]
                n = (xn * lax.rsqrt(jnp.mean(xn * xn, axis=-1, keepdims=True) + EPS)) * g_ref[...]
                h_ref[...] = (n * (1.0 + mod_row(3 * k2 + 1)) + mod_row(3 * k2)).astype(h_ref.dtype)

    row = pl.BlockSpec((tm, D), lambda m, d: (m, 0))
    in_specs = [pl.BlockSpec((None, tm, Fb), lambda m, d: (d, m, 0)),
                pl.BlockSpec((None, None, Fb, D), lambda m, d: (d, layer, 0, 0)),
                row, pl.BlockSpec((2, 9, D), lambda m, d: (0, 0, 0))]
    args = [A, WD, X, mod2]
    out_specs, out_shape = [row, row], [_sds((rows, D)), _sds((rows, D))]
    if next_norm is not None:
        in_specs.append(pl.BlockSpec((1, D), lambda m, d: (0, 0)))
        args.append(next_norm[0])
        out_specs.append(row)
        out_shape.append(_sds((rows, D), _MXU))
    return _call(body, jobs=jobs, name=name, grid=(rows // tm, ND), in_specs=in_specs, out_specs=out_specs,
                 out_shape=out_shape, scratch=[pltpu.VMEM((tm, D), F32)])(*args)


def _ffn_down_loss(A, WD, X, mod2, k, fg, target, name):
    Fb, D = WD.shape[-2:]
    S = X.shape[0]
    tm = _tm(S) // 2

    def body(a_ref, w_ref, x_ref, mod_ref, g_ref, t_ref, loss_ref, dx_ref, dy_ref, dgate_ref, dg_ref, acc_ref):
        m, d = pl.program_id(0), pl.program_id(1)

        @pl.when(d == 0)
        def _():
            acc_ref[...] = jnp.zeros_like(acc_ref)

        @pl.when(jnp.logical_and(m == 0, d == 0))
        def _():
            loss_ref[...] = jnp.zeros_like(loss_ref)
            dgate_ref[...] = jnp.zeros_like(dgate_ref)
            dg_ref[...] = jnp.zeros_like(dg_ref)

        acc_ref[...] += _dot(a_ref[...], w_ref[...])

        @pl.when(d == ND - 1)
        def _():
            y = acc_ref[...]
            gate = 0.5 * mod_ref[3 * k + 2:3 * k + 3, :]
            x = x_ref[...] + gate * y
            g = g_ref[...]
            r = lax.rsqrt(jnp.mean(x * x, axis=-1, keepdims=True) + EPS)
            n = x * r
            err = n * g - t_ref[...]
            loss_ref[...] += 0.5 * jnp.sum(jnp.mean(err * err, axis=-1, keepdims=True), axis=0, keepdims=True)
            dy = err * (1.0 / D)
            dg_ref[...] += jnp.sum(dy * n, axis=0, keepdims=True)
            dn = dy * g
            dx = r * (dn - n * jnp.mean(dn * n, axis=-1, keepdims=True))
            dx_ref[...] = dx
            dy_ref[...] = (gate * dx).astype(dy_ref.dtype)
            dgate_ref[...] += jnp.sum(0.5 * dx * y, axis=0, keepdims=True)

    row = pl.BlockSpec((tm, D), lambda m, d: (m, 0))
    vec = pl.BlockSpec((1, D), lambda m, d: (0, 0))
    return _call(body, name=name, grid=(S // tm, ND),
                 in_specs=[pl.BlockSpec((None, tm, Fb), lambda m, d: (d, m, 0)),
                           pl.BlockSpec((None, None, Fb, D), lambda m, d: (d, 0, 0, 0)),
                           row, pl.BlockSpec((None, 9, D), lambda m, d: (0, 0, 0)), vec, row],
                 out_specs=[pl.BlockSpec((1, 1), lambda m, d: (0, 0)), row, row, vec, vec],
                 out_shape=[_sds((1, 1)), _sds((S, D)), _sds((S, D), _MXU), _sds((1, D)), _sds((1, D))],
                 scratch=[pltpu.VMEM((tm, D), F32)])(A, WD, X, mod2, fg, target)


def _ffn_dact(dYb, WD, layer, G, U, rows, name, jobs=()):
    Fb, D = WD.shape[-2:]

    def body(dy_ref, w_ref, g_ref, u_ref, dg_ref, du_ref):
        da = _dot_nt(dy_ref[...], w_ref[...])
        g = g_ref[...]
        sg = _sigmoid(g)
        dg_ref[...] = (da * u_ref[...] * (sg * (1.0 + g * (1.0 - sg)))).astype(dg_ref.dtype)
        du_ref[...] = (da * (g * sg)).astype(du_ref.dtype)

    tm = _tm(rows)
    blk = pl.BlockSpec((None, tm, Fb), lambda m, d: (d, m, 0))
    return _call(body, jobs=jobs, name=name, grid=(rows // tm, ND),
                 in_specs=[pl.BlockSpec((tm, D), lambda m, d: (m, 0)),
                           pl.BlockSpec((None, None, Fb, D), lambda m, d: (d, layer, 0, 0)), blk, blk],
                 out_specs=[blk, blk],
                 out_shape=[_sds((ND, rows, Fb), _MXU), _sds((ND, rows, Fb), _MXU)])(dYb, WD, G, U)


def _ffn_dh(dG, dU, WG, WU, rows, name, jobs=()):
    Fb, D = WG.shape[-2:]

    def body(dg_ref, du_ref, wg_ref, wu_ref, dh_ref, acc_ref):
        d = pl.program_id(1)

        @pl.when(d == 0)
        def _():
            acc_ref[...] = jnp.zeros_like(acc_ref)

        acc_ref[...] += _dot(dg_ref[...], wg_ref[...]) + _dot(du_ref[...], wu_ref[...])

        @pl.when(d == ND - 1)
        def _():
            dh_ref[...] = acc_ref[...]

    tm = _tm(rows)
    blk = pl.BlockSpec((None, tm, Fb), lambda m, d: (d, m, 0))
    wspec = pl.BlockSpec((None, None, Fb, D), lambda m, d: (d, 0, 0, 0))
    return _call(body, jobs=jobs, name=name, grid=(rows // tm, ND), in_specs=[blk, blk, wspec, wspec],
                 out_specs=pl.BlockSpec((tm, D), lambda m, d: (m, 0)), out_shape=_sds((rows, D)),
                 scratch=[pltpu.VMEM((tm, D), F32)])(dG, dU, WG, WU)


def _mm_tn(A, a_spec, B, b_spec, out_shape, out_spec, rows, name, prev=None, jobs=()):
    def body(*refs):
        a_ref, b_ref, o_ref = refs[0], refs[1], refs[-1]

        @pl.when(pl.program_id(1) == 0)
        def _():
            o_ref[...] = jnp.zeros_like(o_ref)

        o_ref[...] += _dot_tn(a_ref[...], b_ref[...])

    in_specs = [a_spec, b_spec]
    args = [A, B]
    aliases = None
    if prev is not None:
        in_specs.append(pl.BlockSpec(memory_space=pl.ANY))
        args.append(prev)
        aliases = {2: 0}
    return _call(body, jobs=jobs, name=name, grid=(ND, rows // _tm(rows)), in_specs=in_specs, out_specs=out_spec,
                 out_shape=_sds(out_shape), aliases=aliases)(*args)


def _proj_in(H2, WIN, name, jobs=()):
    R, D = H2.shape
    Nb = WIN.shape[-1]

    def body(h_ref, w_ref, p_ref):
        p_ref[...] = _dot(h_ref[...], w_ref[...])

    tm = _tm(R)
    return _call(body, jobs=jobs, name=name, grid=(ND, R // tm),
                 in_specs=[pl.BlockSpec((tm, D), lambda d, m: (m, 0)), pl.BlockSpec((None, D, Nb), lambda d, m: (d, 0, 0))],
                 out_specs=pl.BlockSpec((tm, Nb), lambda d, m: (m, d)), out_shape=_sds((R, ND * Nb)))(H2, WIN)


def _dproj_in(dP, WIN, name, jobs=()):
    R = dP.shape[0]
    D, Nb = WIN.shape[-2:]

    def body(dp_ref, w_ref, dh_ref, acc_ref):
        d = pl.program_id(1)

        @pl.when(d == 0)
        def _():
            acc_ref[...] = jnp.zeros_like(acc_ref)

        acc_ref[...] += _dot_nt(dp_ref[...], w_ref[...])

        @pl.when(d == ND - 1)
        def _():
            dh_ref[...] = acc_ref[...]

    tm = _tm(R)
    return _call(body, jobs=jobs, name=name, grid=(R // tm, ND),
                 in_specs=[pl.BlockSpec((tm, Nb), lambda m, d: (m, d)), pl.BlockSpec((None, D, Nb), lambda m, d: (d, 0, 0))],
                 out_specs=pl.BlockSpec((tm, D), lambda m, d: (m, 0)), out_shape=_sds((R, D)),
                 scratch=[pltpu.VMEM((tm, D), F32)])(dP, WIN)


def _proj_out(mixb, WOUT, X1, mod2, ng, S, name):
    D = WOUT.shape[0]

    def body(m_ref, w_ref, x_ref, mod_ref, g_ref, z_ref, xn_ref, h_ref):
        z = _dot(m_ref[...], w_ref[...])
        z_ref[...] = z
        xn = x_ref[...] + mod_ref[5:6, :] * z
        xn_ref[...] = xn
        n = (xn * lax.rsqrt(jnp.mean(xn * xn, axis=-1, keepdims=True) + EPS)) * g_ref[...]
        h_ref[...] = (n * (1.0 + mod_ref[7:8, :]) + mod_ref[6:7, :]).astype(h_ref.dtype)

    return _call(body, name=name, grid=(S // TR,),
                 in_specs=[_row_spec(D), _full_spec((D, D)), _row_spec(D), pl.BlockSpec((None, 9, D), lambda i: (0, 0, 0)),
                           _full_spec((1, D))],
                 out_specs=[_row_spec(D), _row_spec(D), _row_spec(D)],
                 out_shape=[_sds((S, D)), _sds((S, D)), _sds((S, D), _MXU)])(mixb, WOUT, X1, mod2, ng)


def _dproj_out(dZb, WOUT, name):
    S, D = dZb.shape

    def body(dz_ref, w_ref, dm_ref):
        dm_ref[...] = _dot_nt(dz_ref[...], w_ref[...])

    return _call(body, name=name, grid=(S // TR,), in_specs=[_row_spec(D), _full_spec((D, D))],
                 out_specs=_row_spec(D), out_shape=_sds((S, D)))(dZb, WOUT)


def _pair_swap(t):
    lane = lax.broadcasted_iota(jnp.int32, t.shape, 1)
    return jnp.where(lane % 2 == 0, pltpu.roll(t, HD - 1, 1), pltpu.roll(t, 1, 1))


SM_SCALE = HD ** -0.5


def _qkv_prep(P, qg, kg, COS, SIN, D, KVW, name):
    R = P.shape[0]
    W = D + 2 * KVW
    nq, nk = D // HD, KVW // HD

    def body(p_ref, qg_ref, kg_ref, cos_ref, sin_ref, q_ref, k_ref, v_ref):
        cos, sin = cos_ref[...], sin_ref[...]

        def head(t, g):
            y = (t * lax.rsqrt(jnp.mean(t * t, axis=-1, keepdims=True) + EPS)) * g
            return y * cos + _pair_swap(y) * sin

        for h in range(nq):
            q_ref[:, h * HD:(h + 1) * HD] = (head(p_ref[:, h * HD:(h + 1) * HD], qg_ref[...]) * SM_SCALE).astype(q_ref.dtype)
        for h in range(nk):
            k_ref[:, h * HD:(h + 1) * HD] = head(p_ref[:, D + h * HD:D + (h + 1) * HD], kg_ref[...]).astype(k_ref.dtype)
        v_ref[...] = p_ref[:, D + KVW:W].astype(v_ref.dtype)

    return _call(body, name=name, grid=(R // TR,),
                 in_specs=[_row_spec(W), _full_spec((1, HD)), _full_spec((1, HD)), _row_spec(HD), _row_spec(HD)],
                 out_specs=[_row_spec(D), _row_spec(KVW), _row_spec(KVW)],
                 out_shape=[_sds((R, D), _MXU), _sds((R, KVW), _MXU), _sds((R, KVW), _MXU)])(P, qg, kg, COS, SIN)


def _qkv_bwd(P, dq, dk, dv, qg, kg, COS, SIN, dP, D, KVW, nx, name):
    R, INW = P.shape
    W = D + 2 * KVW
    nq, nk = D // HD, KVW // HD

    def body(p_ref, dq_ref, dk_ref, dv_ref, qg_ref, kg_ref, cos_ref, sin_ref, dp_in, dp_ref, dqg_ref, dkg_ref):
        i = pl.program_id(0)
        cos, sin = cos_ref[...], sin_ref[...]

        @pl.when(i == 0)
        def _():
            dqg_ref[...] = jnp.zeros_like(dqg_ref)
            dkg_ref[...] = jnp.zeros_like(dkg_ref)

        def head_bwd(t, g, dout):
            r = lax.rsqrt(jnp.mean(t * t, axis=-1, keepdims=True) + EPS)
            n = t * r
            dy = dout * cos + _pair_swap(dout * sin)
            dn = dy * g
            return r * (dn - n * jnp.mean(dn * n, axis=-1, keepdims=True)), jnp.sum(dy * n, axis=0, keepdims=True)

        dqg = jnp.zeros((1, HD), F32)
        for h in range(nq):
            sl = slice(h * HD, (h + 1) * HD)
            dt, dg = head_bwd(p_ref[:, sl], qg_ref[...], jnp.where(i < nx, dq_ref[:, sl] * SM_SCALE, 0.0))
            dp_ref[:, sl] = dt.astype(dp_ref.dtype)
            dqg += dg
        dkg = jnp.zeros((1, HD), F32)
        for h in range(nk):
            sl = slice(h * HD, (h + 1) * HD)
            dt, dg = head_bwd(p_ref[:, D + h * HD:D + (h + 1) * HD], kg_ref[...], dk_ref[:, sl])
            dp_ref[:, D + h * HD:D + (h + 1) * HD] = dt.astype(dp_ref.dtype)
            dkg += dg
        dqg_ref[...] += dqg
        dkg_ref[...] += dkg
        dp_ref[:, D + KVW:W] = dv_ref[...].astype(dp_ref.dtype)

    return _call(body, name=name, grid=(R // TR,),
                 in_specs=[_row_spec(W), _row_spec(D, nx - 1), _row_spec(KVW), _row_spec(KVW), _full_spec((1, HD)),
                           _full_spec((1, HD)), _row_spec(HD), _row_spec(HD), pl.BlockSpec(memory_space=pl.ANY)],
                 out_specs=[_row_spec(W), _full_spec((1, HD)), _full_spec((1, HD))],
                 out_shape=[_sds((R, INW), _MXU), _sds((1, HD)), _sds((1, HD))],
                 aliases={8: 0})(P, dq, dk, dv, qg, kg, COS, SIN, dP)


def _stack_heads(ref, G, dtype=None):
    parts = [ref[:, g * HD:(g + 1) * HD] for g in range(G)]
    out = jnp.concatenate(parts, axis=0)
    return out if dtype is None else out.astype(dtype)


_KEY_CHUNKS = 4


def _key_chunks(R):
    unit = 256 if R % 256 == 0 else 16
    nt = R // unit
    n = min(_KEY_CHUNKS, nt)
    bounds = [0]
    for i in range(n):
        bounds.append(bounds[-1] + (nt // n + (1 if i < nt % n else 0)) * unit)
    return bounds


def _attn_fwd(q, k, v, S, G, name, jobs=()):
    R, KVW = k.shape
    D = q.shape[1]
    Kh = KVW // HD
    tq = 128
    kb = _key_chunks(R)

    def body(q_ref, k_ref, v_ref, o_ref, lse_ref):
        qs = _stack_heads(q_ref, G)
        m = l = acc = None
        for c in range(len(kb) - 1):
            s = _dot_nt(qs, k_ref[kb[c]:kb[c + 1], :])
            mc = jnp.max(s, axis=-1, keepdims=True)
            m_new = mc if c == 0 else jnp.maximum(m, mc)
            p = jnp.exp(s - m_new)
            ps = jnp.sum(p, axis=-1, keepdims=True)
            pv = _dot(p, v_ref[kb[c]:kb[c + 1], :])
            if c == 0:
                l, acc = ps, pv
            else:
                alpha = jnp.exp(m - m_new)
                l = alpha * l + ps
                acc = alpha * acc + pv
            m = m_new
        o = acc / l
        lse = m + jnp.log(l)
        for g in range(G):
            o_ref[:, g * HD:(g + 1) * HD] = o[g * tq:(g + 1) * tq, :]
            lse_ref[g] = jnp.broadcast_to(lse[g * tq:(g + 1) * tq, :], (tq, HD))

    return _call(body, jobs=jobs, name=name, grid=(Kh, S // tq),
                 in_specs=[pl.BlockSpec((tq, G * HD), lambda h, i: (i, h)), pl.BlockSpec((R, HD), lambda h, i: (0, h)),
                           pl.BlockSpec((R, HD), lambda h, i: (0, h))],
                 out_specs=[pl.BlockSpec((tq, G * HD), lambda h, i: (i, h)),
                            pl.BlockSpec((None, G, tq, HD), lambda h, i: (h, 0, i, 0))],
                 out_shape=[_sds((S, D)), _sds((Kh, G, S, HD))])(q, k, v)


def _attn_bwd(q, k, v, O, LSE, dOb, S, G, name, jobs=()):
    R, KVW = k.shape
    D = q.shape[1]
    Kh = KVW // HD
    tq = 128
    kb = _key_chunks(R)

    def body(q_ref, k_ref, v_ref, o_ref, lse_ref, do_ref, dq_ref, dk_ref, dv_ref):
        @pl.when(pl.program_id(1) == 0)
        def _():
            dk_ref[...] = jnp.zeros_like(dk_ref)
            dv_ref[...] = jnp.zeros_like(dv_ref)

        qs = _stack_heads(q_ref, G)
        do = _stack_heads(do_ref, G)
        o = _stack_heads(o_ref, G)
        lse = jnp.concatenate([lse_ref[g][:, 0:1] for g in range(G)], axis=0)
        delta = jnp.sum(do.astype(F32) * o, axis=-1, keepdims=True)
        dq = None
        for c in range(len(kb) - 1):
            rows = slice(kb[c], kb[c + 1])
            kk = k_ref[rows, :]
            p = jnp.exp(_dot_nt(qs, kk) - lse)
            dp = _dot_nt(do, v_ref[rows, :])
            ds = (p * (dp - delta)).astype(_MXU)
            dqc = _dot(ds, kk)
            dq = dqc if c == 0 else dq + dqc
            dk_ref[rows, :] += _dot_tn(ds, qs)
            dv_ref[rows, :] += _dot_tn(p, do)
        for g in range(G):
            dq_ref[:, g * HD:(g + 1) * HD] = dq[g * tq:(g + 1) * tq, :]

    qspec = pl.BlockSpec((tq, G * HD), lambda h, i: (i, h))
    kspec = pl.BlockSpec((R, HD), lambda h, i: (0, h))
    return _call(body, jobs=jobs, name=name, grid=(Kh, S // tq),
                 in_specs=[qspec, kspec, kspec, qspec, pl.BlockSpec((None, G, tq, HD), lambda h, i: (h, 0, i, 0)), qspec],
                 out_specs=[qspec, kspec, kspec],
                 out_shape=[_sds((S, D)), _sds((R, KVW)), _sds((R, KVW))])(q, k, v, O, LSE, dOb)


def _halo_specs(R, CB, col0):
    nt8 = TR // 8
    return [pl.BlockSpec((8, CB), lambda h, i: (jnp.maximum(i * nt8 - 1, 0), col0 + h)),
            pl.BlockSpec((TR, CB), lambda h, i: (i, col0 + h)),
            pl.BlockSpec((8, CB), lambda h, i: (jnp.minimum((i + 1) * nt8, R // 8 - 1), col0 + h))]


def _seq_pos(i, S, R, CB):
    t = i * TR - 8 + lax.broadcasted_iota(jnp.int32, (TR + 16, CB), 0)
    start = jnp.where(t >= S, S, 0)
    end = jnp.where(t >= S, R, S)
    return t - start, end - t


def _shift(cat, by):
    return pltpu.roll(cat, by % cat.shape[0], 0)


def _gate_mats(xcb, w_ref, dirn, nb):
    return jnp.concatenate([_dot(xcb[:, b * HD:(b + 1) * HD], w_ref[dirn, b]) for b in range(nb)], axis=1)


def _lru_gates_fwd(P, conv_w, conv_b, WA, WX, ba, bx, lam, S, D, col0, name):
    R = P.shape[0]
    CB = D // 2
    nb = CB // HD

    def body(xp_ref, x_ref, xn_ref, cw_ref, cb_ref, wa_ref, wx_ref, ba_ref, bx_ref, lam_ref,
             xc_ref, af_ref, uf_ref, ab_ref, ub_ref):
        i = pl.program_id(1)
        cat = jnp.concatenate([xp_ref[...], x_ref[...], xn_ref[...]], axis=0)
        from_start, to_end = _seq_pos(i, S, R, CB)
        conv = (cb_ref[...] + cw_ref[2:3, :] * cat
                + cw_ref[0:1, :] * jnp.where(from_start >= 2, _shift(cat, 2), 0.0)
                + cw_ref[1:2, :] * jnp.where(from_start >= 1, _shift(cat, 1), 0.0)
                + cw_ref[3:4, :] * jnp.where(to_end >= 2, _shift(cat, -1), 0.0))
        xc = conv[8:8 + TR, :]
        xc_ref[...] = xc
        xcb = xc.astype(_MXU)
        for dirn, (a_ref, u_ref) in enumerate(((af_ref, uf_ref), (ab_ref, ub_ref))):
            ra = _sigmoid(_gate_mats(xcb, wa_ref, dirn, nb) + ba_ref[dirn:dirn + 1, :])
            ia = _sigmoid(_gate_mats(xcb, wx_ref, dirn, nb) + bx_ref[dirn:dirn + 1, :])
            nl = -lam_ref[dirn:dirn + 1, :]
            sp = jnp.maximum(nl, 0.0) + jnp.log(1.0 + jnp.exp(-jnp.abs(nl)))
            la = (-LRU_C) * ra * sp
            a = jnp.exp(la)
            a_ref[...] = a
            u_ref[...] = jnp.sqrt(1.0 - a * a) * (ia * xc)

    def par(r):
        return pl.BlockSpec((r, CB), lambda h, i: (0, h))

    wspec = pl.BlockSpec((2, nb, HD, HD), lambda h, i: (0, h, 0, 0))
    out = pl.BlockSpec((TR, CB), lambda h, i: (i, h))
    return _call(body, name=name, grid=(2, R // TR),
                 in_specs=_halo_specs(R, CB, col0) + [par(4), par(1), wspec, wspec, par(2), par(2), par(2)],
                 out_specs=[out] * 5, out_shape=[_sds((R, D))] * 5,
                 )(P, P, P, conv_w, conv_b, WA, WX, ba, bx, lam)


def _lru_gates_bwd(xc, hf, hb, gf, gb, WA, WX, ba, bx, lam, S, name):
    R, D = xc.shape
    CB = D // 2
    nb = CB // HD

    def body(xc_ref, hfq_ref, hf_ref, hfl_ref, hb_ref, hbn_ref, gf_ref, gb_ref, wa_ref, wx_ref, ba_ref, bx_ref, lam_ref,
             dxc_ref, dwa_ref, dwx_ref, dba_ref, dbx_ref, dlam_ref):
        @pl.when(pl.program_id(1) == 0)
        def _():
            for r in (dwa_ref, dwx_ref, dba_ref, dbx_ref, dlam_ref):
                r[...] = jnp.zeros_like(r)

        t = pl.program_id(1) * TR + lax.broadcasted_iota(jnp.int32, (TR, CB), 0)
        hfp = _shift(jnp.concatenate([hfq_ref[...], hf_ref[...]], axis=0), 1)[8:8 + TR, :]
        hfp = jnp.where(t == 0, hfl_ref[7:8, :], jnp.where(t == S, 0.0, hfp))
        hbp = _shift(jnp.concatenate([hb_ref[...], hbn_ref[...]], axis=0), -1)[0:TR, :]
        hbp = jnp.where(t == R - 1, 0.0, hbp)

        xc = xc_ref[...]
        xcb = xc.astype(_MXU)
        dxc = jnp.zeros_like(xc)
        for dirn, (hp, g_ref) in enumerate(((hfp, gf_ref), (hbp, gb_ref))):
            ra = _sigmoid(_gate_mats(xcb, wa_ref, dirn, nb) + ba_ref[dirn:dirn + 1, :])
            ia = _sigmoid(_gate_mats(xcb, wx_ref, dirn, nb) + bx_ref[dirn:dirn + 1, :])
            nl = -lam_ref[dirn:dirn + 1, :]
            sp = jnp.maximum(nl, 0.0) + jnp.log(1.0 + jnp.exp(-jnp.abs(nl)))
            la = (-LRU_C) * ra * sp
            a = jnp.exp(la)
            e2 = a * a
            rs = lax.rsqrt(1.0 - e2)
            s = (1.0 - e2) * rs
            du = g_ref[...]
            dla = du * hp * a - du * (ia * xc) * (e2 * rs)
            dxc += du * s * ia
            dza = (dla * (-LRU_C) * sp) * ra * (1.0 - ra)
            dzx = (du * s * xc) * ia * (1.0 - ia)
            dlam_ref[dirn:dirn + 1, :] += jnp.sum(dla * (LRU_C * ra) * _sigmoid(nl), axis=0, keepdims=True)
            dba_ref[dirn:dirn + 1, :] += jnp.sum(dza, axis=0, keepdims=True)
            dbx_ref[dirn:dirn + 1, :] += jnp.sum(dzx, axis=0, keepdims=True)
            dzab, dzxb = dza.astype(_MXU), dzx.astype(_MXU)
            parts = []
            for b in range(nb):
                sl = slice(b * HD, (b + 1) * HD)
                dwa_ref[dirn, b] += _dot_tn(xcb[:, sl], dzab[:, sl])
                dwx_ref[dirn, b] += _dot_tn(xcb[:, sl], dzxb[:, sl])
                parts.append(_dot_nt(dzab[:, sl], wa_ref[dirn, b]) + _dot_nt(dzxb[:, sl], wx_ref[dirn, b]))
            dxc += jnp.concatenate(parts, axis=1)
        dxc_ref[...] = dxc

    def par(r):
        return pl.BlockSpec((r, CB), lambda h, i: (0, h))

    wspec = pl.BlockSpec((2, nb, HD, HD), lambda h, i: (0, h, 0, 0))
    tile = pl.BlockSpec((TR, CB), lambda h, i: (i, h))
    before, _, after = _halo_specs(R, CB, 0)
    last = pl.BlockSpec((8, CB), lambda h, i: (R // 8 - 1, h))
    nbt = D // HD
    return _call(body, name=name, grid=(2, R // TR),
                 in_specs=[tile, before, tile, last, tile, after, tile, tile, wspec, wspec, par(2), par(2), par(2)],
                 out_specs=[tile, wspec, wspec, par(2), par(2), par(2)],
                 out_shape=[_sds((R, D)), _sds((2, nbt, HD, HD)), _sds((2, nbt, HD, HD)), _sds((2, D)), _sds((2, D)),
                            _sds((2, D))])(xc, hf, hf, hf, hb, hb, gf, gb, WA, WX, ba, bx, lam)


def _scan_rows(n_groups, step, init):
    return lax.fori_loop(0, n_groups, lambda gi, c: step(pl.multiple_of(gi * 8, 8), c), init)


def _lru_scan_fwd(af, uf, ab, ub, S, name):
    R, D = af.shape
    W = min(SCAN_W, D)
    nm, nx = R // TR, S // TR
    nc = nm - nx
    ng = TR // 8

    def body(af_ref, uf_ref, ab_ref, ub_ref, hf_ref, hb_ref, cf_ref, cb_ref):
        @pl.when(pl.program_id(1) == 0)
        def _():
            cf_ref[...] = jnp.zeros_like(cf_ref)
            cb_ref[...] = jnp.zeros_like(cb_ref)

        def step(base, carry):
            hf, hb = carry
            baseb = pl.multiple_of(TR - 8 - base, 8)
            for r in range(8):
                tf, tb = base + r, baseb + 7 - r
                hf = af_ref[pl.ds(tf, 1), :] * hf + uf_ref[pl.ds(tf, 1), :]
                hf_ref[pl.ds(tf, 1), :] = hf
                hb = ab_ref[pl.ds(tb, 1), :] * hb + ub_ref[pl.ds(tb, 1), :]
                hb_ref[pl.ds(tb, 1), :] = hb
            return hf, hb

        hf, hb = _scan_rows(ng, step, (cf_ref[0:1, :], cb_ref[0:1, :]))
        cf_ref[0:1, :] = hf
        cb_ref[0:1, :] = hb

    fmap = lambda j, s: (jnp.where(s < nc, nx + s, s - nc), j)
    bmap = lambda j, s: (nm - 1 - s, j)
    fs, bs = pl.BlockSpec((TR, W), fmap), pl.BlockSpec((TR, W), bmap)
    return _call(body, name=name, grid=(D // W, nm), in_specs=[fs, fs, bs, bs], out_specs=[fs, bs],
                 out_shape=[_sds((R, D))] * 2, scratch=[pltpu.VMEM((8, W), F32), pltpu.VMEM((8, W), F32)])(af, uf, ab, ub)


def _lru_scan_bwd(af, ab, dhs, S, name):
    R, D = af.shape
    W = min(SCAN_W, D)
    nm, nx = R // TR, S // TR
    ng = TR // 8

    def body(af_ref, dhf_ref, ab_ref, dhb_ref, gf_ref, gb_ref, cf_ref, cb_ref):
        @pl.when(pl.program_id(1) == 0)
        def _():
            cf_ref[...] = jnp.zeros_like(cf_ref)
            cb_ref[...] = jnp.zeros_like(cb_ref)

        def step(base, carry):
            cf, cb = carry
            based = pl.multiple_of(TR - 8 - base, 8)
            for r in range(8):
                tf, tb = based + 7 - r, base + r
                g = dhf_ref[pl.ds(tf, 1), :] + cf
                gf_ref[pl.ds(tf, 1), :] = g
                cf = af_ref[pl.ds(tf, 1), :] * g
                g = dhb_ref[pl.ds(tb, 1), :] + cb
                gb_ref[pl.ds(tb, 1), :] = g
                cb = ab_ref[pl.ds(tb, 1), :] * g
            return cf, cb

        cf, cb = _scan_rows(ng, step, (cf_ref[0:1, :], cb_ref[0:1, :]))
        cf_ref[0:1, :] = cf
        cb_ref[0:1, :] = cb

    fmap = lambda j, s: (jnp.where(s < nx, nx - 1 - s, nm - 1 - (s - nx)), j)
    bmap = lambda j, s: (s, j)
    fs, bs = pl.BlockSpec((TR, W), fmap), pl.BlockSpec((TR, W), bmap)
    return _call(body, name=name, grid=(D // W, nm), in_specs=[fs, fs, bs, bs], out_specs=[fs, bs],
                 out_shape=[_sds((R, D))] * 2, scratch=[pltpu.VMEM((8, W), F32), pltpu.VMEM((8, W), F32)])(af, dhs, ab, dhs)


def _merge_fwd(P, hf, hb, O, S, D, col_lg, name):
    R = P.shape[0]
    CB = D // 2
    nx = S // TR

    def body(lg_ref, ga_ref, gl_ref, hf_ref, hb_ref, o_ref, mix_ref):
        ge, _ = _gelu_and_grad(lg_ref[...])
        lru = (hf_ref[...] + hb_ref[...]) * ge
        mix_ref[...] = (_sigmoid(ga_ref[...]) * o_ref[...] + _sigmoid(gl_ref[...]) * lru).astype(mix_ref.dtype)

    def col(c0):
        return pl.BlockSpec((TR, CB), lambda h, i: (i, c0 + h))

    return _call(body, name=name, grid=(2, R // TR),
                 in_specs=[col(col_lg), col(col_lg + 2), col(col_lg + 4), col(0), col(0),
                           pl.BlockSpec((TR, CB), lambda h, i: (jnp.minimum(i, nx - 1), h))],
                 out_specs=col(0), out_shape=_sds((R, D), _MXU))(P, P, P, hf, hb, O)


def _merge_bwd(dmix, P, hf, hb, O, S, D, col_lg, name, jobs=()):
    R = P.shape[0]
    CB = D // 2
    nx = S // TR

    def body(dm_ref, lg_ref, ga_ref, gl_ref, hf_ref, hb_ref, o_ref, do_ref, dhs_ref, dp_ref, stash, sems):
        h, i = pl.program_id(0), pl.program_id(1)
        dm = jnp.where(i < nx, dm_ref[...], 0.0)
        sa, sl = _sigmoid(ga_ref[...]), _sigmoid(gl_ref[...])
        ge, dge = _gelu_and_grad(lg_ref[...])
        hs = hf_ref[...] + hb_ref[...]
        dl = dm * sl
        do_ref[...] = (dm * sa).astype(do_ref.dtype)
        dhs_ref[...] = dl * ge
        stash[0] = (dl * hs * dge).astype(stash.dtype)
        stash[1] = (dm * o_ref[...] * sa * (1.0 - sa)).astype(stash.dtype)
        stash[2] = (dm * (hs * ge) * sl * (1.0 - sl)).astype(stash.dtype)
        rows = pl.ds(pl.multiple_of(i * TR, TR), TR)
        copies = [pltpu.make_async_copy(stash.at[sec], dp_ref.at[rows, pl.ds(pl.multiple_of((col_lg + 2 * sec + h) * CB, CB), CB)],
                                        sems.at[sec]) for sec in range(3)]
        for cp in copies:
            cp.start()
        for cp in copies:
            cp.wait()

    def col(c0):
        return pl.BlockSpec((TR, CB), lambda h, i: (i, c0 + h))

    xrow = pl.BlockSpec((TR, CB), lambda h, i: (jnp.minimum(i, nx - 1), h))
    return _call(body, jobs=jobs, name=name, grid=(2, R // TR),
                 in_specs=[xrow, col(col_lg), col(col_lg + 2), col(col_lg + 4), col(0), col(0), xrow],
                 out_specs=[col(0), col(0), pl.BlockSpec(memory_space=pl.ANY)],
                 out_shape=[_sds((R, D), _MXU), _sds((R, D)), _sds(P.shape, _MXU)],
                 scratch=[pltpu.VMEM((3, TR, CB), _MXU), pltpu.SemaphoreType.DMA((3,))])(dmix, P, P, P, hf, hb, O)


def _conv_bwd(dxc, P, conv_w, dP, S, D, col0, name, jobs=()):
    R = P.shape[0]
    CB = D // 2

    def body(dp_, d_ref, dn_, xp_ref, x_ref, xn_ref, cw_ref, dp_in, dpo_ref, dcw_ref, dcb_ref):
        i = pl.program_id(1)

        @pl.when(i == 0)
        def _():
            dcw_ref[...] = jnp.zeros_like(dcw_ref)
            dcb_ref[...] = jnp.zeros_like(dcb_ref)

        d = d_ref[...]
        catd = jnp.concatenate([dp_[...], d, dn_[...]], axis=0)
        catx = jnp.concatenate([xp_ref[...], x_ref[...], xn_ref[...]], axis=0)
        from_start, to_end = _seq_pos(i, S, R, CB)
        dxl = (cw_ref[2:3, :] * catd
               + cw_ref[0:1, :] * jnp.where(to_end >= 3, _shift(catd, -2), 0.0)
               + cw_ref[1:2, :] * jnp.where(to_end >= 2, _shift(catd, -1), 0.0)
               + cw_ref[3:4, :] * jnp.where(from_start >= 1, _shift(catd, 1), 0.0))
        dpo_ref[...] = dxl[8:8 + TR, :].astype(dpo_ref.dtype)
        taps = (jnp.where(from_start >= 2, _shift(catx, 2), 0.0), jnp.where(from_start >= 1, _shift(catx, 1), 0.0),
                catx, jnp.where(to_end >= 2, _shift(catx, -1), 0.0))
        for kk in range(4):
            dcw_ref[kk:kk + 1, :] += jnp.sum(d * taps[kk][8:8 + TR, :], axis=0, keepdims=True)
        dcb_ref[...] += jnp.sum(d, axis=0, keepdims=True)

    return _call(body, jobs=jobs, name=name, grid=(2, R // TR),
                 in_specs=_halo_specs(R, CB, 0) + _halo_specs(R, CB, col0)
                 + [pl.BlockSpec((4, CB), lambda h, i: (0, h)), pl.BlockSpec(memory_space=pl.ANY)],
                 out_specs=[pl.BlockSpec((TR, CB), lambda h, i: (i, col0 + h)), pl.BlockSpec((4, CB), lambda h, i: (0, h)),
                            pl.BlockSpec((1, CB), lambda h, i: (0, h))],
                 out_shape=[_sds(dP.shape, dP.dtype), _sds((4, D)), _sds((1, D))],
                 aliases={7: 0})(dxc, dxc, dxc, P, P, P, conv_w, dP)


def _rope_tables(S, C):
    t = jnp.arange(S, dtype=jnp.int32)
    row = (t // GRID_W).astype(F32)
    col = (t % GRID_W).astype(F32)
    axis_dims = HD // 2
    freqs = ROPE_THETA ** (-jnp.arange(0, axis_dims, 2, dtype=F32) / axis_dims)
    ang = jnp.concatenate([row[:, None] * freqs, col[:, None] * freqs], axis=-1)
    cos = jnp.repeat(jnp.cos(ang), 2, axis=-1)
    sin = jnp.repeat(jnp.sin(ang), 2, axis=-1) * jnp.tile(jnp.array([-1.0, 1.0], F32), HD // 2)
    return (jnp.concatenate([cos, jnp.ones((C, HD), F32)], axis=0),
            jnp.concatenate([sin, jnp.zeros((C, HD), F32)], axis=0))


def _local_step(x, ctx, target, modx, modc, ng, shards, qg, kg, conv_w, conv_b, WA, WX, ba, bx, lam, fg, idx, opt):
    S, D = x.shape
    C = ctx.shape[0]
    R = S + C
    nx = S // TR
    Nb = shards['w_in'].shape[-1]
    Fb = shards['wd0'].shape[1]
    KVW = (ND * Nb - 5 * D) // 2
    G = D // KVW
    CB = D // 2
    col_lx = (D + 2 * KVW) // CB
    assert S % TR == 0 and C % TR == 0 and (D + 2 * KVW) % CB == 0 and CB % HD == 0
    mod2 = jnp.stack([modx, modc])
    X0 = jnp.concatenate([x, ctx], axis=0)
    COS, SIN = _rope_tables(S, C)
    ng0, ng1, ng2 = ng[0:1], ng[1:2], ng[2:3]
    ag = lambda n: _ag_job(shards[n])
    sib = lambda Gp: _rs_sibling_job(Gp.reshape(ND, -1, Gp.shape[-1]))
    add = lambda Gp, bufA, tag: _rs_add(Gp.reshape(ND, -1, Gp.shape[-1]), bufA, idx, f"{tag}_rs_add")
    out = {}

    def tn_specs(rows):
        tm = _tm(rows)
        return pl.BlockSpec((None, tm, Fb), lambda d, r: (d, r, 0)), pl.BlockSpec((tm, D), lambda d, r: (r, 0))

    wd_spec = pl.BlockSpec((None, None, Fb, D), lambda d, r: (d, 0, 0, 0))

    ((WG0,),) = _comm_call([ag('wg0')], "ag_wg0")
    H1 = _norm_mod_fwd(X0, ng0, mod2, 0, R, nx, "ffn1_norm")
    G1, ((WU0,),) = _ffn_gate(H1, WG0, R, "ffn1_gate", jobs=[ag('wu0')])
    (U1, A1), ((WD0,),) = _ffn_up(H1, None, WU0, R, "ffn1_up", jobs=[ag('wd0')], G=G1)
    (Y1, X1, H2), ((WIN,),) = _ffn_down(A1, WD0, 0, X0, mod2, 0, R, S, "ffn1_down", jobs=[ag('w_in')], next_norm=(ng1, 1))
    P, ((WOUT,), (WG1,)) = _proj_in(H2, WIN, "proj_in", jobs=[ag('w_out'), ag('wg1')])
    q, k, v = _qkv_prep(P, qg, kg, COS, SIN, D, KVW, "qkv_prep")
    (O, LSE), ((WU1,), (WD1,)) = _attn_fwd(q, k, v, S, G, "attn_fwd", jobs=[ag('wu1'), ag('wd1')])
    WOUT = WOUT.reshape(D, D)
    xc, af, uf, ab, ub = _lru_gates_fwd(P, conv_w, conv_b, WA, WX, ba, bx, lam, S, D, col_lx, "lru_gates")
    hf, hb = _lru_scan_fwd(af, uf, ab, ub, S, "lru_scan")
    mixb = _merge_fwd(P, hf, hb, O, S, D, col_lx + 2, "merge")
    Z, X2, H3 = _proj_out(mixb, WOUT, X1, mod2, ng2, S, "proj_out")
    G3, U3, A3 = _ffn_up(H3, WG1, WU1, S, "ffn2_up")
    loss, dX3, dY3b, dg3, dfg = _ffn_down_loss(A3, WD1, X2, mod2, 2, fg, target, "ffn2_down_loss")
    dg3 = dg3[None]

    dG3, dU3 = _ffn_dact(dY3b, WD1, 0, G3, U3, S, "ffn2_dact")
    blk, row = tn_specs(S)
    dWD1 = _mm_tn(A3, blk, dY3b, row, (ND, 1, Fb, D), wd_spec, S, "ffn2_dwd")
    dWG1 = _mm_tn(dG3, blk, H3, row, (ND, 1, Fb, D), wd_spec, S, "ffn2_dwg")
    dWU1 = _mm_tn(dU3, blk, H3, row, (ND, 1, Fb, D), wd_spec, S, "ffn2_dwu")
    dH3, ((a_wd1,), (a_wg1,), (a_wu1,)) = _ffn_dh(dG3, dU3, WG1, WU1, S, "ffn2_dh", jobs=[sib(dWD1), sib(dWG1), sib(dWU1)])
    T_wd1, own_wd1 = add(dWD1, a_wd1, "wd1")
    T_wg1, own_wg1 = add(dWG1, a_wg1, "wg1")
    T_wu1, own_wu1 = add(dWU1, a_wu1, "wu1")
    dX2, dsh3, dsc3, dng2, dZb, dg2 = _norm_mod_bwd(X2, dH3, dX3, ng2, mod2, 2, S, nx, nx, "ffn2_dnorm", branch=(Z, 1, 1.0))

    dmix = _dproj_out(dZb, WOUT, "dproj_out")
    dWOUT = _mm_tn(mixb, pl.BlockSpec((_tm(S), D // ND), lambda d, r: (r, d)), dZb, pl.BlockSpec((_tm(S), D), lambda d, r: (r, 0)),
                   (ND, D // ND, D), pl.BlockSpec((None, D // ND, D), lambda d, r: (d, 0, 0)), S, "dw_out")
    (dOb, dhs, dP), ((a_wout,),) = _merge_bwd(dmix, P, hf, hb, O, S, D, col_lx + 2, "merge_bwd", jobs=[sib(dWOUT)])
    T_wout, own_wout = add(dWOUT, a_wout, "w_out")
    gf, gb = _lru_scan_bwd(af, ab, dhs, S, "lru_scan_bwd")
    dxc, dWA, dWX, dba, dbx, dlam = _lru_gates_bwd(xc, hf, hb, gf, gb, WA, WX, ba, bx, lam, S, "lru_gates_bwd")
    dLW = jnp.stack([dWA, dWX]).reshape(ND, -1, HD)
    (dP, dconv_w, dconv_b), ((a_lw,),) = _conv_bwd(dxc, P, conv_w, dP, S, D, col_lx, "conv_bwd", jobs=[sib(dLW)])
    T_lw, own_lw = add(dLW, a_lw, "lru_w")
    (dq, dk, dv), ((b_wd1,), (b_wg1,), (b_wu1,), (b_wout,), (b_lw,)) = _attn_bwd(
        q, k, v, O, LSE, dOb, S, G, "attn_bwd",
        jobs=[_rs_chips_job(T_wd1), _rs_chips_job(T_wg1), _rs_chips_job(T_wu1), _rs_chips_job(T_wout), _rs_chips_job(T_lw)])
    fin_wd = _rs_finish(own_wd1, b_wd1, "wd1_rs_finish", opt['ffn_wd'], 1, 2)
    fin_wg = _rs_finish(own_wg1, b_wg1, "wg1_rs_finish", opt['ffn_wg'], 1, 2)
    fin_wu = _rs_finish(own_wu1, b_wu1, "wu1_rs_finish", opt['ffn_wu'], 1, 2)
    out['w_out'] = _rs_finish(own_wout, b_wout, "w_out_rs_finish", opt['w_out'])
    (lw_sum,) = _rs_finish(own_lw, b_lw, "lru_w_rs_finish")
    dP, dqg, dkg = _qkv_bwd(P, dq, dk, dv, qg, kg, COS, SIN, dP, D, KVW, nx, "qkv_bwd")
    dH2, ((lw_full,),) = _dproj_in(dP, WIN, "dproj_in", jobs=[_ag_job(lw_sum)])
    dWIN = _mm_tn(H2, pl.BlockSpec((_tm(R), D), lambda d, r: (r, 0)), dP, pl.BlockSpec((_tm(R), Nb), lambda d, r: (r, d)),
                  (ND, D, Nb), pl.BlockSpec((None, D, Nb), lambda d, r: (d, 0, 0)), R, "dw_in")
    dX1, dsh2, dsc2, dng1, dY1b, dg1 = _norm_mod_bwd(X1, dH2, dX2, ng1, mod2, 1, R, nx, nx, "mix_dnorm", branch=(Y1, 0, 0.5))

    (dG1, dU1), ((a_win,),) = _ffn_dact(dY1b, WD0, 0, G1, U1, R, "ffn1_dact", jobs=[sib(dWIN)])
    T_win, own_win = add(dWIN, a_win, "w_in")
    blk, row = tn_specs(R)
    dWD0, ((b_win01,),) = _mm_tn(A1, blk, dY1b, row, (ND, 1, Fb, D), wd_spec, R, "ffn1_dwd", jobs=[_rs_chips_job(T_win, (0, 1))])
    dWG0, ((b_win2,), (a_wd0,)) = _mm_tn(dG1, blk, H1, row, (ND, 1, Fb, D), wd_spec, R, "ffn1_dwg",
                                         jobs=[_rs_chips_job(T_win, (2,)), sib(dWD0)])
    out['w_in'] = _rs_finish(own_win, [(b_win01, 0), (b_win01, 1), (b_win2, 0)], "w_in_rs_finish", opt['w_in'])
    T_wd0, own_wd0 = add(dWD0, a_wd0, "wd0")
    dWU0, ((a_wg0,), (b_wd0,)) = _mm_tn(dU1, blk, H1, row, (ND, 1, Fb, D), wd_spec, R, "ffn1_dwu",
                                        jobs=[sib(dWG0), _rs_chips_job(T_wd0)])
    T_wg0, own_wg0 = add(dWG0, a_wg0, "wg0")
    out['ffn_wd'] = _rs_finish(own_wd0, b_wd0, "wd0_rs_finish", opt['ffn_wd'], 0, 2, fin_wd)
    dH1, ((a_wu0,), (b_wg0,)) = _ffn_dh(dG1, dU1, WG0, WU0, R, "ffn1_dh", jobs=[sib(dWU0), _rs_chips_job(T_wg0)])
    T_wu0, own_wu0 = add(dWU0, a_wu0, "wu0")
    out['ffn_wg'] = _rs_finish(own_wg0, b_wg0, "wg0_rs_finish", opt['ffn_wg'], 0, 2, fin_wg)
    (dX0, dsh1, dsc1, dng0), ((b_wu0,),) = _norm_mod_bwd(
        X0, dH1, dX1, ng0, mod2, 0, R, nx, R // TR, "ffn1_dnorm", jobs=[_rs_chips_job(T_wu0, (0, 1))])
    out['wu0_pending'] = (T_wu0, own_wu0, b_wu0, fin_wu)
    out['lru_w'] = lw_full

    zero = jnp.zeros((1, D), F32)
    dmodx = jnp.concatenate([dsh1[0], dsc1[0], dg1[0], dsh2[0], dsc2[0], dg2[0], dsh3[0], dsc3[0], dg3[0]], axis=0)
    dmodc = jnp.concatenate([dsh1[1], dsc1[1], dg1[1], dsh2[1], dsc2[1], zero, zero, zero, zero], axis=0)
    out.update(loss=loss, grad_x=dX0[:S], dmodx=dmodx, dmodc=dmodc, norm_g=jnp.concatenate([dng0, dng1, dng2], axis=0),
               q_norm_g=dqg, k_norm_g=dkg, conv_w=dconv_w, conv_b=dconv_b, lru_ba=dba, lru_bx=dbx,
               lru_lambda=dlam, final_norm_g=dfg)
    return out


def _mesh_pos():
    return lax.axis_index("x"), lax.axis_index("y"), lax.axis_index("c")


def _all_gather(xb, name, in_vmem=False):
    space = pltpu.VMEM if in_vmem else pl.ANY

    def body(x_ref, out_ref, send_sems, recv_sems, local_sem):
        x, y, c = _mesh_pos()
        me, sibling = (x, y, c), (x, y, 1 - c)
        chips = [(1 - x, y), (x, 1 - y), (1 - x, 1 - y)]

        def slot(px, py, pc):
            return out_ref.at[4 * px + 2 * py + pc]

        def copy(k, block, to, src=None):
            return pltpu.make_async_remote_copy(
                src_ref=slot(*block) if src is None else src, dst_ref=slot(*block),
                send_sem=send_sems.at[k], recv_sem=recv_sems.at[k], device_id=to, device_id_type=MESH)

        mine = pltpu.make_async_copy(x_ref, slot(*me), local_sem)
        mine.start()
        first = [copy(0, me, sibling, src=x_ref)]
        first += [copy(1 + j, me, (*chip, c), src=x_ref) for j, chip in enumerate(chips)]
        for cp in first:
            cp.start()
        passed = [copy(4 + j, (*chip, c), sibling) for j, chip in enumerate(chips)]
        for j, chip in enumerate(chips):
            copy(1 + j, (*chip, c), me).wait_recv()
            passed[j].start()
        copy(0, sibling, me).wait_recv()
        for j, chip in enumerate(chips):
            copy(4 + j, (*chip, 1 - c), me).wait_recv()
        for cp in first + passed:
            cp.wait_send()
        mine.wait()

    return pl.pallas_call(
        body, name=name, out_shape=_sds((ND,) + xb.shape, xb.dtype),
        in_specs=[pl.BlockSpec(memory_space=space)], out_specs=pl.BlockSpec(memory_space=space),
        scratch_shapes=[pltpu.SemaphoreType.DMA((7,)), pltpu.SemaphoreType.DMA((7,)), pltpu.SemaphoreType.DMA(())])(xb)


def _ag_job(xb):
    def env(ins, outs, sems):
        x_ref, out_ref = ins[0], outs[0]
        send_sems, recv_sems, local_sem = sems
        x, y, c = _mesh_pos()

        def slot(px, py, pc):
            return out_ref.at[4 * px + 2 * py + pc]

        def copy(k, block, to, src=None):
            return pltpu.make_async_remote_copy(
                src_ref=slot(*block) if src is None else src, dst_ref=slot(*block),
                send_sem=send_sems.at[k], recv_sem=recv_sems.at[k], device_id=to, device_id_type=MESH)

        return x_ref, local_sem, slot, copy, (x, y, c)

    def start(ins, outs, sems):
        x_ref, local_sem, slot, copy, (x, y, c) = env(ins, outs, sems)
        me = (x, y, c)
        pltpu.make_async_copy(x_ref, slot(*me), local_sem).start()
        copy(0, me, (x, y, 1 - c), src=x_ref).start()
        copy(1, me, (1 - x, y, c), src=x_ref).start()
        copy(2, me, (x, 1 - y, c), src=x_ref).start()

    def relay(ins, outs, sems):
        _, _, _, copy, (x, y, c) = env(ins, outs, sems)
        xn, yn = (1 - x, y, c), (x, 1 - y, c)
        copy(1, xn, (x, y, c)).wait_recv()
        copy(2, yn, (x, y, c)).wait_recv()
        src = (jnp.where(c == 0, x, 1 - x), jnp.where(c == 0, 1 - y, y), c)
        dst = (jnp.where(c == 0, 1 - x, x), jnp.where(c == 0, y, 1 - y), c)
        copy(3, src, dst).start()
        copy(4, xn, (x, y, 1 - c)).start()
        copy(5, yn, (x, y, 1 - c)).start()

    def relay2(ins, outs, sems):
        _, _, _, copy, (x, y, c) = env(ins, outs, sems)
        dg = (1 - x, 1 - y, c)
        copy(3, dg, (x, y, c)).wait_recv()
        copy(6, dg, (x, y, 1 - c)).start()

    def finish(ins, outs, sems):
        x_ref, local_sem, slot, copy, (x, y, c) = env(ins, outs, sems)
        me, sib = (x, y, c), (x, y, 1 - c)
        copy(0, sib, me).wait_recv()
        for k, chip in ((4, (1 - x, y)), (5, (x, 1 - y)), (6, (1 - x, 1 - y))):
            copy(k, (*chip, 1 - c), me).wait_recv()
        for k in range(7):
            copy(k, me, me).wait_send()
        pltpu.make_async_copy(x_ref, slot(*me), local_sem).wait()

    return _Job([xb], [_sds((ND,) + xb.shape, xb.dtype)],
                [pltpu.SemaphoreType.DMA((7,)), pltpu.SemaphoreType.DMA((7,)), pltpu.SemaphoreType.DMA(())],
                start, finish, relay, relay2)


def _rs_sibling_job(Gp):
    def copies(ins, outs, sems):
        x, y, c = _mesh_pos()
        return [pltpu.make_async_remote_copy(
            src_ref=ins[0].at[2 * k + (1 - c)], dst_ref=outs[0].at[k], send_sem=sems[0].at[k], recv_sem=sems[1].at[k],
            device_id=(x, y, 1 - c), device_id_type=MESH) for k in range(4)]

    def start(ins, outs, sems):
        for cp in copies(ins, outs, sems):
            cp.start()

    def finish(ins, outs, sems):
        cps = copies(ins, outs, sems)
        for cp in cps:
            cp.wait_recv()
        for cp in cps:
            cp.wait_send()

    return _Job([Gp], [_sds((4,) + Gp.shape[1:], Gp.dtype)],
                [pltpu.SemaphoreType.DMA((4,)), pltpu.SemaphoreType.DMA((4,))], start, finish)


def _rs_chips_job(T, dests=(0, 1, 2)):
    def copies(ins, outs, sems):
        x, y, c = _mesh_pos()
        chips = [(1 - x, y), (x, 1 - y), (1 - x, 1 - y)]
        cps = []
        for i, j in enumerate(dests):
            px, py = chips[j]
            cps.append(pltpu.make_async_remote_copy(
                src_ref=ins[0].at[2 * px + py], dst_ref=outs[0].at[i], send_sem=sems[0].at[i], recv_sem=sems[1].at[i],
                device_id=(px, py, c), device_id_type=MESH))
        return cps

    def start(ins, outs, sems):
        for cp in copies(ins, outs, sems):
            cp.start()

    def finish(ins, outs, sems):
        cps = copies(ins, outs, sems)
        for cp in cps:
            cp.wait_recv()
        for cp in cps:
            cp.wait_send()

    n = len(dests)
    return _Job([T], [_sds((n,) + T.shape[1:], T.dtype)],
                [pltpu.SemaphoreType.DMA((n,)), pltpu.SemaphoreType.DMA((n,))], start, finish)


def _tile_rows(rows, cols):
    best = None
    for t in range(16, rows + 1, 16):
        if rows % t == 0 and t * cols * 4 <= (3 << 19):
            best = t
    return best if best is not None else rows


def _prefetch_call(body, *, name, grid, in_specs, out_specs, out_shape):
    return pl.pallas_call(
        body, name=name, out_shape=out_shape,
        grid_spec=pltpu.PrefetchScalarGridSpec(num_scalar_prefetch=1, grid=grid, in_specs=in_specs, out_specs=out_specs),
        compiler_params=pltpu.CompilerParams(dimension_semantics=("arbitrary",) * len(grid), vmem_limit_bytes=VMEM_LIMIT))


def _rs_add(Gp, bufA, idx, name):
    rows, cols = Gp.shape[1:]
    tr = _tile_rows(rows, cols)

    def body(i_ref, g_ref, a_ref, t_ref, own_ref):
        t = g_ref[...] + a_ref[...]
        t_ref[...] = t.astype(t_ref.dtype)

        @pl.when(pl.program_id(1) == i_ref[1])
        def _():
            own_ref[...] = t

    return _prefetch_call(
        body, name=name, grid=(rows // tr, 4),
        in_specs=[pl.BlockSpec((None, tr, cols), lambda r, k, i_ref: (2 * k + i_ref[0], r, 0)),
                  pl.BlockSpec((None, tr, cols), lambda r, k, i_ref: (k, r, 0))],
        out_specs=[pl.BlockSpec((None, tr, cols), lambda r, k, i_ref: (k, r, 0)),
                   pl.BlockSpec((tr, cols), lambda r, k, i_ref: (r, 0))],
        out_shape=[_sds((4, rows, cols), jnp.bfloat16), _sds((rows, cols))])(idx, Gp, bufA)


def _adam(w, g, m, v):
    m = ADAM_B1 * m + (1.0 - ADAM_B1) * g
    v = ADAM_B2 * v + (1.0 - ADAM_B2) * (g * g)
    m_hat = m / (1.0 - ADAM_B1 ** ADAM_STEP)
    v_hat = v / (1.0 - ADAM_B2 ** ADAM_STEP)
    return -ADAM_LR * (m_hat / (jnp.sqrt(v_hat) + ADAM_EPS) + ADAM_WD * w), m, v


def _rs_finish(Town, bufB, name, wmv=None, slab=0, n_slabs=1, prev=None):
    rows, cols = Town.shape
    tr = _tile_rows(rows, cols)
    nr = rows // tr
    n_in = 4 + (3 if wmv is not None else 0)
    n_out = 4 if wmv is not None else 1

    def body(*refs):
        ins, outs = refs[:n_in], refs[len(refs) - n_out:]
        g = ((ins[0][...] + ins[1][...].astype(F32)) + ins[2][...].astype(F32)) + ins[3][...].astype(F32)
        outs[0][...] = g
        if wmv is not None:
            d, m, v = _adam(ins[4][...], g, ins[5][...], ins[6][...])
            outs[1][...] = d
            outs[2][...] = m
            outs[3][...] = v

    plain = pl.BlockSpec((tr, cols), lambda r: (r, 0))
    slabbed = pl.BlockSpec((tr, cols), lambda r: (slab * nr + r, 0))
    pairs = bufB if isinstance(bufB, list) else [(bufB, j) for j in range(3)]
    in_specs = [plain] + [pl.BlockSpec((None, tr, cols), (lambda j: lambda r: (j, r, 0))(j)) for _, j in pairs]
    args = [Town] + [a for a, _ in pairs]
    if wmv is not None:
        in_specs += [slabbed] * 3
        args += list(wmv)
    aliases = None
    if prev is not None:
        in_specs += [pl.BlockSpec(memory_space=pl.ANY)] * n_out
        aliases = {len(args) + i: i for i in range(n_out)}
        args += list(prev)
    return _call(body, name=name, grid=(nr,), in_specs=in_specs, out_specs=[slabbed] * n_out,
                 out_shape=[_sds((n_slabs * rows, cols))] * n_out, aliases=aliases)(*args)


def _adamw_plain(w, g, m, v, name, jobs=()):
    rows, cols = w.shape
    tr = _tile_rows(rows, cols)

    def body(w_ref, g_ref, m_ref, v_ref, d_ref, mo_ref, vo_ref):
        d, m_, v_ = _adam(w_ref[...], g_ref[...], m_ref[...], v_ref[...])
        d_ref[...] = d
        mo_ref[...] = m_
        vo_ref[...] = v_

    spec = pl.BlockSpec((tr, cols), lambda r: (r, 0))
    return _call(body, jobs=jobs, name=name, grid=(rows // tr,), in_specs=[spec] * 4, out_specs=[spec] * 3,
                 out_shape=[_sds((rows, cols))] * 3)(w, g, m, v)


_MOD_TK = 512


def _mod_fwd(cc16, w_loc, b_loc, name):
    D, cols = w_loc.shape
    tk = min(_MOD_TK, D)
    nk = D // tk

    def body(c_ref, w_ref, b_ref, o_ref):
        kk = pl.program_id(0)

        @pl.when(kk == 0)
        def _():
            o_ref[...] = jnp.zeros_like(o_ref)

        cc = c_ref[...]
        o_ref[...] += _dot(cc * _sigmoid(cc), w_ref[...])

        @pl.when(kk == nk - 1)
        def _():
            o_ref[...] += b_ref[...]

    return _call(body, name=name, grid=(nk,),
                 in_specs=[pl.BlockSpec((16, tk), lambda kk: (0, kk)), pl.BlockSpec((tk, cols), lambda kk: (kk, 0)),
                           _full_spec((1, cols))],
                 out_specs=_full_spec((16, cols)), out_shape=_sds((16, cols)))(cc16, w_loc, b_loc)


def _mod_bwd(dm_loc, cc16, w_loc, name):
    D, cols = w_loc.shape

    def body(dm_ref, c_ref, w_ref, gw_ref, ds_ref):
        rows = [dm_ref[b, 0:1, :] for b in range(ND)]
        ctx = dm_ref[0, 1:2, :]
        for b in range(1, ND):
            ctx = ctx + dm_ref[b, 1:2, :]
        dm16 = jnp.concatenate(rows + [ctx, jnp.zeros((7, cols), F32)], axis=0)
        cc = c_ref[...]
        gw_ref[...] = _dot_tn(cc * _sigmoid(cc), dm16)
        ds_ref[...] = _dot_nt(dm16, w_ref[...])

    tk = min(_MOD_TK, D)
    return _call(body, name=name, grid=(D // tk,),
                 in_specs=[_full_spec((ND, 8, cols)), pl.BlockSpec((16, tk), lambda kk: (0, kk)),
                           pl.BlockSpec((tk, cols), lambda kk: (kk, 0))],
                 out_specs=[pl.BlockSpec((tk, cols), lambda kk: (kk, 0)), pl.BlockSpec((16, tk), lambda kk: (0, kk))],
                 out_shape=[_sds((D, cols)), _sds((16, D))])(dm_loc, cc16, w_loc)


def _bmod_grad(dm_all, name):
    n = dm_all.shape[-1]

    def body(dm_ref, o_ref):
        acc = dm_ref[0, 0:1, :] + dm_ref[0, 1:2, :]
        for b in range(1, ND):
            acc = (acc + dm_ref[b, 0:1, :]) + dm_ref[b, 1:2, :]
        o_ref[...] = acc

    return _call(body, name=name, grid=(1,), in_specs=[_full_spec((ND, 8, n))], out_specs=_full_spec((1, n)),
                 out_shape=_sds((1, n)))(dm_all)


_SMALL_ROWS = 24
_ROW_CCTX = 15


def _small_finish(parts, c_ctx, name):
    D = parts.shape[-1]

    def body(p_ref, c_ref, o_ref):
        acc = p_ref[0]
        for b in range(1, ND):
            acc = acc + p_ref[b]
        cc = c_ref[...]
        sg = _sigmoid(cc)
        dsilu = sg * (1.0 + cc * (1.0 - sg))
        row = lax.broadcasted_iota(jnp.int32, acc.shape, 0)
        o_ref[...] = jnp.where(row == _ROW_CCTX, acc * dsilu, acc)

    return _call(body, name=name, grid=(1,), in_specs=[_full_spec(parts.shape), _full_spec((1, D))],
                 out_specs=_full_spec((_SMALL_ROWS, D)), out_shape=_sds((_SMALL_ROWS, D)))(parts, c_ctx)


_WEIGHTS = ['c_ctx', 'w_mod', 'b_mod', 'norm_g', 'ffn_wg', 'ffn_wu', 'ffn_wd', 'w_in', 'w_out', 'q_norm_g', 'k_norm_g',
            'conv_w', 'conv_b', 'lru_wa', 'lru_ba', 'lru_wx', 'lru_bx', 'lru_lambda', 'final_norm_g']
_SMALL = ['c_ctx', 'b_mod', 'norm_g', 'q_norm_g', 'k_norm_g', 'conv_w', 'conv_b', 'lru_ba', 'lru_bx', 'lru_lambda',
          'final_norm_g']


def _pad_rows(a, rows):
    return jnp.pad(a, ((0, rows - a.shape[0]),) + ((0, 0),) * (a.ndim - 1))


def _step(w, m, v, x, c, ctx, loss_target):
    xi, yi, ci = _mesh_pos()
    me = 4 * xi + 2 * yi + ci
    idx = jnp.stack([ci, 2 * xi + yi]).astype(jnp.int32)
    S, D = x.shape[1:]
    Ds = D // ND
    cols = w['w_mod'].shape[-1]

    sp = jnp.concatenate([w['norm_g'][0], w['conv_w'][0], w['lru_ba'][0], w['lru_bx'][0], w['lru_lambda'][0]], axis=0)
    spg = _all_gather(_pad_rows(sp, 16), "ag_small_params", in_vmem=True)
    spf = jnp.transpose(spg, (1, 0, 2)).reshape(16, D)
    ng, conv_w, ba, bx, lam = spf[0:3], spf[3:7], spf[7:9], spf[9:11], spf[11:13]

    cg = _all_gather(_pad_rows(c, 8), "ag_cond", in_vmem=True)
    cc16 = _pad_rows(jnp.concatenate([cg[:, 0, :], w['c_ctx'][None, :]], axis=0), 16)
    b_loc = lax.dynamic_slice_in_dim(w['b_mod'], me * cols, cols, axis=1)
    mod_loc = _mod_fwd(cc16, w['w_mod'][0], b_loc, "mod_fwd")
    modg = _all_gather(mod_loc, "ag_mod", in_vmem=True)
    mod16 = jnp.transpose(modg, (1, 0, 2)).reshape(16, ND * cols)
    modx = lax.dynamic_index_in_dim(mod16, me, axis=0, keepdims=False).reshape(9, D)
    modc = mod16[8].reshape(9, D)

    shards = {'w_in': w['w_in'][0].astype(_MXU), 'w_out': w['w_out'][0].astype(_MXU)}
    for layer in range(2):
        shards[f'wg{layer}'] = w['ffn_wg'][0, layer].T.astype(_MXU)[None]
        shards[f'wu{layer}'] = w['ffn_wu'][0, layer].T.astype(_MXU)[None]
        shards[f'wd{layer}'] = w['ffn_wd'][0, layer].astype(_MXU)[None]
    WA, WX = w['lru_wa'][0].astype(_MXU), w['lru_wx'][0].astype(_MXU)
    big = ('ffn_wg', 'ffn_wu', 'ffn_wd', 'w_in', 'w_out')
    transposed = ('ffn_wg', 'ffn_wu')
    tr_view = lambda a: jnp.swapaxes(a, -1, -2)
    opt = {n: tuple((tr_view(a[n]) if n in transposed else a[n]) for a in (w, m, v)) for n in big}
    opt = {n: tuple(a.reshape(-1, a.shape[-1]) for a in t) for n, t in opt.items()}

    g = _local_step(x[0], ctx[0], loss_target[0], modx, modc, ng, shards, w['q_norm_g'], w['k_norm_g'],
                    conv_w, w['conv_b'], WA, WX, ba, bx, lam, w['final_norm_g'][None, :], idx, opt)

    grad, delta, new_m, new_v = {}, {}, {}, {}

    lfull = g['lru_w'].reshape((2,) + w['lru_wa'].shape[1:])
    for i, n in enumerate(('lru_wa', 'lru_wx')):
        shard = w[n].shape
        view = lambda a: a.reshape(-1, HD)
        grad[n] = lfull[i].reshape(shard)
        outs = _adamw_plain(view(w[n]), view(lfull[i]), view(m[n]), view(v[n]), f"adamw_{n}")
        delta[n], new_m[n], new_v[n] = [o.reshape(shard) for o in outs]

    dm = _pad_rows(jnp.stack([g['dmodx'].reshape(-1), g['dmodc'].reshape(-1)]), 8)
    dm_all = _all_gather(dm, "ag_dmod", in_vmem=True)
    dm_loc = lax.dynamic_slice_in_dim(dm_all, me * cols, cols, axis=2)
    gw_mod, dsil = _mod_bwd(dm_loc, cc16, w['w_mod'][0], "mod_bwd")
    grad['w_mod'] = gw_mod[None]
    T_wu0, own_wu0, b_wu0, fin_wu = g['wu0_pending']
    outs, ((b_wu0d,),) = _adamw_plain(w['w_mod'][0], gw_mod, m['w_mod'][0], v['w_mod'][0], "adamw_w_mod",
                                      jobs=[_rs_chips_job(T_wu0, (2,))])
    delta['w_mod'], new_m['w_mod'], new_v['w_mod'] = [o[None] for o in outs]
    g['ffn_wu'] = _rs_finish(own_wu0, [(b_wu0, 0), (b_wu0, 1), (b_wu0d, 0)], "wu0_rs_finish", opt['ffn_wu'], 0, 2, fin_wu)
    for n in big:
        if n in transposed:
            shape_t = w[n].shape[:-2] + (w[n].shape[-1], w[n].shape[-2])
            grad[n], delta[n], new_m[n], new_v[n] = [tr_view(o.reshape(shape_t)) for o in g[n]]
        else:
            grad[n], delta[n], new_m[n], new_v[n] = [o.reshape(w[n].shape) for o in g[n]]
    grad['b_mod'] = _bmod_grad(dm_all, "bmod_grad")

    pad_d = lambda a: jnp.concatenate([a, jnp.zeros((1, D - a.shape[1]), F32)], axis=1)
    small = jnp.concatenate([g['norm_g'], g['conv_w'], g['conv_b'], g['lru_ba'], g['lru_bx'], g['lru_lambda'],
                             g['final_norm_g'], dsil[8:9], pad_d(g['q_norm_g']), pad_d(g['k_norm_g'])], axis=0)
    parts = _all_gather(_pad_rows(small, _SMALL_ROWS), "ag_small_grads", in_vmem=True)
    ssum = _small_finish(parts, w['c_ctx'][None, :], "small_finish")
    mine = lambda rows: lax.dynamic_slice_in_dim(rows, me * Ds, Ds, axis=1)
    grad['norm_g'] = mine(ssum[0:3])[None]
    grad['conv_w'] = mine(ssum[3:7])[None]
    grad['conv_b'] = ssum[7:8]
    grad['lru_ba'] = mine(ssum[8:10])[None]
    grad['lru_bx'] = mine(ssum[10:12])[None]
    grad['lru_lambda'] = mine(ssum[12:14])[None]
    grad['final_norm_g'] = ssum[14]
    grad['c_ctx'] = ssum[_ROW_CCTX]
    grad['q_norm_g'] = ssum[16:17, :HD]
    grad['k_norm_g'] = ssum[17:18, :HD]

    def pack(d):
        flat = jnp.concatenate([d[n].reshape(-1) for n in _SMALL])
        padded = -(-flat.shape[0] // 1024) * 1024
        return jnp.concatenate([flat, jnp.zeros((padded - flat.shape[0],), F32)]).reshape(-1, HD)

    outs = _adamw_plain(pack(w), pack(grad), pack(m), pack(v), "adamw_small")
    off = 0
    for n in _SMALL:
        size = math.prod(w[n].shape)
        for dst, o in zip((delta, new_m, new_v), outs):
            dst[n] = o.reshape(-1)[off:off + size].reshape(w[n].shape)
        off += size

    loss = lax.psum(g['loss'][0, 0], ("x", "y", "c"))
    return (loss, g['grad_x'][None], *[grad[n] for n in _WEIGHTS], *[delta[n] for n in _WEIGHTS],
            *[new_m[n] for n in _WEIGHTS], *[new_v[n] for n in _WEIGHTS])


def kernel(x, c, ctx, c_ctx, w_mod, b_mod, norm_g, ffn_wg, ffn_wu, ffn_wd, w_in, w_out, q_norm_g, k_norm_g, conv_w, conv_b, lru_wa, lru_ba, lru_wx, lru_bx, lru_lambda, final_norm_g, loss_target, m_c_ctx, m_w_mod, m_b_mod, m_norm_g, m_ffn_wg, m_ffn_wu, m_ffn_wd, m_w_in, m_w_out, m_q_norm_g, m_k_norm_g, m_conv_w, m_conv_b, m_lru_wa, m_lru_ba, m_lru_wx, m_lru_bx, m_lru_lambda, m_final_norm_g, v_c_ctx, v_w_mod, v_b_mod, v_norm_g, v_ffn_wg, v_ffn_wu, v_ffn_wd, v_w_in, v_w_out, v_q_norm_g, v_k_norm_g, v_conv_w, v_conv_b, v_lru_wa, v_lru_ba, v_lru_wx, v_lru_bx, v_lru_lambda, v_final_norm_g):
    given = dict(locals())
    w = {n: given[n] for n in _WEIGHTS}
    m = {n: given["m_" + n] for n in _WEIGHTS}
    v = {n: given["v_" + n] for n in _WEIGHTS}
    return _step(w, m, v, x, c, ctx, loss_target)
```

```python
import functools
import math

import jax
import jax.numpy as jnp
from jax import lax
from jax.experimental import pallas as pl
from jax.experimental.pallas import tpu as pltpu

F32 = jnp.float32
_MXU = jnp.bfloat16
ND = 8
TR = 256
HD = 128
EPS = 1e-6
GRID_W = 64
ROPE_THETA = 10000.0
LRU_C = 8.0
VMEM_LIMIT = 56 * 1024 * 1024
SCAN_W = 2048
ADAM_LR, ADAM_B1, ADAM_B2, ADAM_EPS, ADAM_WD, ADAM_STEP = 0.001, 0.9, 0.999, 1e-08, 0.01, 10
MESH = pl.DeviceIdType.MESH


class _Job:
    def __init__(self, inputs, out_shapes, sems, start, finish, relay=None, relay2=None):
        self.inputs, self.out_shapes, self.sems, self.start, self.finish = inputs, out_shapes, sems, start, finish
        self.relay, self.relay2 = relay, relay2


def _call(body, *, name, grid, in_specs, out_specs, out_shape, scratch=(), aliases=None, jobs=()):
    params = pltpu.CompilerParams(dimension_semantics=("arbitrary",) * len(grid), vmem_limit_bytes=VMEM_LIMIT)
    if not jobs:
        return pl.pallas_call(body, name=name, grid=grid, in_specs=in_specs, out_specs=out_specs, out_shape=out_shape,
                              scratch_shapes=scratch, input_output_aliases=aliases or {}, compiler_params=params)
    single = not isinstance(out_specs, (list, tuple))
    o_specs = [out_specs] if single else list(out_specs)
    o_shape = [out_shape] if single else list(out_shape)
    n_in, n_out, n_scr = len(in_specs), len(o_specs), len(scratch)
    j_in = [a for j in jobs for a in j.inputs]
    j_out = [s for j in jobs for s in j.out_shapes]
    j_sem = [s for j in jobs for s in j.sems]
    hbm = pl.BlockSpec(memory_space=pl.ANY)

    def wrapped(*refs):
        ins, rest = refs[:n_in], refs[n_in:]
        jin, rest = rest[:len(j_in)], rest[len(j_in):]
        outs, rest = rest[:n_out], rest[n_out:]
        jout, rest = rest[:len(j_out)], rest[len(j_out):]
        scr, jsem = rest[:n_scr], rest[n_scr:]
        first = functools.reduce(jnp.logical_and, [pl.program_id(a) == 0 for a in range(len(grid))])
        last = functools.reduce(jnp.logical_and, [pl.program_id(a) == grid[a] - 1 for a in range(len(grid))])
        def at_outer(step):
            return functools.reduce(jnp.logical_and, [pl.program_id(a) == (0 if a else step) for a in range(len(grid))])

        def each(which):
            i = o = s = 0
            for j in jobs:
                ni, no, ns = len(j.inputs), len(j.out_shapes), len(j.sems)
                if getattr(j, which, None) is not None:
                    getattr(j, which)(jin[i:i + ni], jout[o:o + no], jsem[s:s + ns])
                i, o, s = i + ni, o + no, s + ns

        @pl.when(first)
        def _():
            each("start")

        relay_early = grid[0] >= 4
        if relay_early:
            @pl.when(at_outer(grid[0] // 2))
            def _():
                each("relay")

            @pl.when(at_outer((7 * grid[0]) // 8))
            def _():
                each("relay2")

        body(*ins, *outs, *scr)

        @pl.when(last)
        def _():
            if not relay_early:
                each("relay")
                each("relay2")
            each("finish")

    call = pl.pallas_call(wrapped, name=name, grid=grid, in_specs=list(in_specs) + [hbm] * len(j_in),
                          out_specs=o_specs + [hbm] * len(j_out), out_shape=o_shape + j_out,
                          scratch_shapes=list(scratch) + j_sem, input_output_aliases=aliases or {}, compiler_params=params)

    def run(*args):
        res = call(*args, *j_in)
        comp = res[0] if single else list(res[:n_out])
        jres, o = [], n_out
        for j in jobs:
            jres.append(list(res[o:o + len(j.out_shapes)]))
            o += len(j.out_shapes)
        return comp, jres

    return run


def _comm_call(jobs, name):
    j_in = [a for j in jobs for a in j.inputs]
    j_out = [s for j in jobs for s in j.out_shapes]
    j_sem = [s for j in jobs for s in j.sems]
    hbm = pl.BlockSpec(memory_space=pl.ANY)

    def body(*refs):
        jin, jout, jsem = refs[:len(j_in)], refs[len(j_in):len(j_in) + len(j_out)], refs[len(j_in) + len(j_out):]
        for which in ("start", "relay", "relay2", "finish"):
            i = o = s = 0
            for j in jobs:
                ni, no, ns = len(j.inputs), len(j.out_shapes), len(j.sems)
                if getattr(j, which, None) is not None:
                    getattr(j, which)(jin[i:i + ni], jout[o:o + no], jsem[s:s + ns])
                i, o, s = i + ni, o + no, s + ns

    res = pl.pallas_call(body, name=name, out_shape=j_out, in_specs=[hbm] * len(j_in), out_specs=[hbm] * len(j_out),
                         scratch_shapes=j_sem)(*j_in)
    jres, o = [], 0
    for j in jobs:
        jres.append(list(res[o:o + len(j.out_shapes)]))
        o += len(j.out_shapes)
    return jres


def _sds(shape, dtype=F32):
    return jax.ShapeDtypeStruct(tuple(shape), dtype)


def _dot(a, b):
    return jnp.dot(a.astype(_MXU), b.astype(_MXU), preferred_element_type=F32)


def _dot_nt(a, b):
    return lax.dot_general(a.astype(_MXU), b.astype(_MXU), (((1,), (1,)), ((), ())), preferred_element_type=F32)


def _dot_tn(a, b):
    return lax.dot_general(a.astype(_MXU), b.astype(_MXU), (((0,), (0,)), ((), ())), preferred_element_type=F32)


def _sigmoid(x):
    return 0.5 * jnp.tanh(0.5 * x) + 0.5


_GELU_C = math.sqrt(2.0 / math.pi)


def _gelu_and_grad(x):
    x2 = x * x
    t = jnp.tanh(_GELU_C * (x + 0.044715 * x * x2))
    ge = 0.5 * x * (1.0 + t)
    dge = 0.5 * (1.0 + t) + 0.5 * x * (1.0 - t * t) * (_GELU_C * (1.0 + 3.0 * 0.044715 * x2))
    return ge, dge


def _tm(rows):
    assert rows % 4 == 0 and (rows // 4) % 16 == 0
    return rows // 4


def _row_spec(width, nmax=None):
    if nmax is None:
        return pl.BlockSpec((TR, width), lambda i: (i, 0))
    return pl.BlockSpec((TR, width), lambda i: (jnp.minimum(i, nmax), 0))


def _full_spec(shape):
    n = len(shape)
    return pl.BlockSpec(tuple(shape), lambda *_: (0,) * n)


def _mod_spec(D, nx):
    return pl.BlockSpec((None, 9, D), lambda i: (i // nx, 0, 0))


def _norm_mod_fwd(X, ng, mod2, k, rows, nx, name):
    D = X.shape[1]

    def body(x_ref, g_ref, mod_ref, h_ref):
        x = x_ref[...]
        r = lax.rsqrt(jnp.mean(x * x, axis=-1, keepdims=True) + EPS)
        n = (x * r) * g_ref[...]
        h_ref[...] = (n * (1.0 + mod_ref[3 * k + 1:3 * k + 2, :]) + mod_ref[3 * k:3 * k + 1, :]).astype(h_ref.dtype)

    return _call(body, name=name, grid=(rows // TR,),
                 in_specs=[_row_spec(D), _full_spec((1, D)), _mod_spec(D, nx)],
                 out_specs=_row_spec(D), out_shape=_sds((rows, D), _MXU))(X, ng, mod2)


def _norm_mod_bwd(X, dH, dXres, ng, mod2, k, rows, nx, res_tiles, name, jobs=(), branch=None):
    D = X.shape[1]
    ngroups = -(-(rows // TR) // nx)
    n_in = 5 + (1 if branch is not None else 0)

    def body(*refs):
        x_ref, dh_ref, dres_ref, g_ref, mod_ref = refs[:5]
        dx_ref, dsh_ref, dsc_ref, dng_ref = refs[n_in:n_in + 4]
        i = pl.program_id(0)
        x = x_ref[...]
        dh = dh_ref[...]
        g = g_ref[...]
        r = lax.rsqrt(jnp.mean(x * x, axis=-1, keepdims=True) + EPS)
        xh = x * r
        n = xh * g
        dn_mod = dh * (1.0 + mod_ref[3 * k + 1:3 * k + 2, :])

        @pl.when(i % nx == 0)
        def _():
            dsh_ref[...] = jnp.zeros_like(dsh_ref)
            dsc_ref[...] = jnp.zeros_like(dsc_ref)
            if branch is not None:
                refs[n_in + 5][...] = jnp.zeros_like(refs[n_in + 5])

        @pl.when(i == 0)
        def _():
            dng_ref[...] = jnp.zeros_like(dng_ref)

        dsh_ref[...] += jnp.sum(dh, axis=0, keepdims=True)
        dsc_ref[...] += jnp.sum(dh * n, axis=0, keepdims=True)
        dng_ref[...] += jnp.sum(dn_mod * xh, axis=0, keepdims=True)
        dn = dn_mod * g
        dres = jnp.where(i < res_tiles, dres_ref[...], 0.0)
        dx = r * (dn - xh * jnp.mean(dn * xh, axis=-1, keepdims=True)) + dres
        dx_ref[...] = dx
        if branch is not None:
            _, k2, coef = branch
            refs[n_in + 4][...] = ((coef * mod_ref[3 * k2 + 2:3 * k2 + 3, :]) * dx).astype(refs[n_in + 4].dtype)
            refs[n_in + 5][...] += jnp.sum(coef * dx * refs[5][...], axis=0, keepdims=True)

    grp = pl.BlockSpec((None, 1, D), lambda i: (i // nx, 0, 0))
    in_specs = [_row_spec(D), _row_spec(D), _row_spec(D, res_tiles - 1), _full_spec((1, D)), _mod_spec(D, nx)]
    out_specs = [_row_spec(D), grp, grp, _full_spec((1, D))]
    out_shape = [_sds((rows, D)), _sds((ngroups, 1, D)), _sds((ngroups, 1, D)), _sds((1, D))]
    args = [X, dH, dXres, ng, mod2]
    if branch is not None:
        in_specs.append(_row_spec(D))
        args.append(branch[0])
        out_specs += [_row_spec(D), grp]
        out_shape += [_sds((rows, D), _MXU), _sds((ngroups, 1, D))]
    return _call(body, jobs=jobs, name=name, grid=(rows // TR,), in_specs=in_specs, out_specs=out_specs,
                 out_shape=out_shape)(*args)


def _ffn_gate(H, WG, rows, name, jobs=()):
    D = H.shape[1]
    Fb = WG.shape[-2]

    def body(h_ref, w_ref, g_ref):
        g_ref[...] = _dot_nt(h_ref[...], w_ref[...])

    tm = _tm(rows)
    return _call(body, jobs=jobs, name=name, grid=(ND, rows // tm),
                 in_specs=[pl.BlockSpec((tm, D), lambda d, m: (m, 0)), pl.BlockSpec((None, None, Fb, D), lambda d, m: (d, 0, 0, 0))],
                 out_specs=pl.BlockSpec((None, tm, Fb), lambda d, m: (d, m, 0)), out_shape=_sds((ND, rows, Fb)))(H, WG)


def _ffn_up(H, WG, WU, rows, name, jobs=(), G=None):
    D = H.shape[1]
    Fb = WU.shape[-2]

    def body(h_ref, w_ref, x_ref, *outs):
        h = h_ref[...]
        g = x_ref[...] if G is not None else _dot_nt(h, x_ref[...])
        u = _dot_nt(h, w_ref[...])
        if G is None:
            outs[0][...] = g
        outs[-2][...] = u
        outs[-1][...] = ((g * _sigmoid(g)) * u).astype(outs[-1].dtype)

    tm = _tm(rows)
    blk = pl.BlockSpec((None, tm, Fb), lambda m, d: (d, m, 0))
    wspec = pl.BlockSpec((None, None, Fb, D), lambda m, d: (d, 0, 0, 0))
    f32o, bfo = _sds((ND, rows, Fb)), _sds((ND, rows, Fb), _MXU)
    return _call(body, jobs=jobs, name=name, grid=(rows // tm, ND),
                 in_specs=[pl.BlockSpec((tm, D), lambda m, d: (m, 0)), wspec, blk if G is not None else wspec],
                 out_specs=[blk, blk] if G is not None else [blk, blk, blk],
                 out_shape=[f32o, bfo] if G is not None else [f32o, f32o, bfo])(H, WU, G if G is not None else WG)


def _ffn_down(A, WD, layer, X, mod2, k, rows, S, name, jobs=(), next_norm=None):
    Fb, D = WD.shape[-2:]
    tm = _tm(rows) // 2

    def body(a_ref, w_ref, x_ref, mod_ref, *rest):
        if next_norm is not None:
            g_ref, y_ref, xn_ref, h_ref, acc_ref = rest
        else:
            y_ref, xn_ref, acc_ref = rest
        d = pl.program_id(1)

        @pl.when(d == 0)
        def _():
            acc_ref[...] = jnp.zeros_like(acc_ref)

        acc_ref[...] += _dot(a_ref[...], w_ref[...])

        @pl.when(d == ND - 1)
        def _():
            y = acc_ref[...]
            y_ref[...] = y
            is_ctx = (pl.program_id(0) * tm + lax.broadcasted_iota(jnp.int32, (tm, 1), 0)) >= S

            def mod_row(j):
                return jnp.where(is_ctx, mod_ref[1, j:j + 1, :], mod_ref[0, j:j + 1, :])

            xn = x_ref[...] + (0.5 * mod_row(3 * k + 2)) * y
            xn_ref[...] = xn
            if next_norm is not None:
                k2 = next_norm[1]
                n = (xn * lax.rsqrt(jnp.mean(xn * xn, axis=-1, keepdims=True) + EPS)) * g_ref[...]
                h_ref[...] = (n * (1.0 + mod_row(3 * k2 + 1)) + mod_row(3 * k2)).astype(h_ref.dtype)

    row = pl.BlockSpec((tm, D), lambda m, d: (m, 0))
    in_specs = [pl.BlockSpec((None, tm, Fb), lambda m, d: (d, m, 0)),
                pl.BlockSpec((None, None, Fb, D), lambda m, d: (d, layer, 0, 0)),
                row, pl.BlockSpec((2, 9, D), lambda m, d: (0, 0, 0))]
    args = [A, WD, X, mod2]
    out_specs, out_shape = [row, row], [_sds((rows, D)), _sds((rows, D))]
    if next_norm is not None:
        in_specs.append(pl.BlockSpec((1, D), lambda m, d: (0, 0)))
        args.append(next_norm[0])
        out_specs.append(row)
        out_shape.append(_sds((rows, D), _MXU))
    return _call(body, jobs=jobs, name=name, grid=(rows // tm, ND), in_specs=in_specs, out_specs=out_specs,
                 out_shape=out_shape, scratch=[pltpu.VMEM((tm, D), F32)])(*args)


def _ffn_down_loss(A, WD, X, mod2, k, fg, target, name):
    Fb, D = WD.shape[-2:]
    S = X.shape[0]
    tm = _tm(S) // 2

    def body(a_ref, w_ref, x_ref, mod_ref, g_ref, t_ref, loss_ref, dx_ref, dy_ref, dgate_ref, dg_ref, acc_ref):
        m, d = pl.program_id(0), pl.program_id(1)

        @pl.when(d == 0)
        def _():
            acc_ref[...] = jnp.zeros_like(acc_ref)

        @pl.when(jnp.logical_and(m == 0, d == 0))
        def _():
            loss_ref[...] = jnp.zeros_like(loss_ref)
            dgate_ref[...] = jnp.zeros_like(dgate_ref)
            dg_ref[...] = jnp.zeros_like(dg_ref)

        acc_ref[...] += _dot(a_ref[...], w_ref[...])

        @pl.when(d == ND - 1)
        def _():
            y = acc_ref[...]
            gate = 0.5 * mod_ref[3 * k + 2:3 * k + 3, :]
            x = x_ref[...] + gate * y
            g = g_ref[...]
            r = lax.rsqrt(jnp.mean(x * x, axis=-1, keepdims=True) + EPS)
            n = x * r
            err = n * g - t_ref[...]
            loss_ref[...] += 0.5 * jnp.sum(jnp.mean(err * err, axis=-1, keepdims=True), axis=0, keepdims=True)
            dy = err * (1.0 / D)
            dg_ref[...] += jnp.sum(dy * n, axis=0, keepdims=True)
            dn = dy * g
            dx = r * (dn - n * jnp.mean(dn * n, axis=-1, keepdims=True))
            dx_ref[...] = dx
            dy_ref[...] = (gate * dx).astype(dy_ref.dtype)
            dgate_ref[...] += jnp.sum(0.5 * dx * y, axis=0, keepdims=True)

    row = pl.BlockSpec((tm, D), lambda m, d: (m, 0))
    vec = pl.BlockSpec((1, D), lambda m, d: (0, 0))
    return _call(body, name=name, grid=(S // tm, ND),
                 in_specs=[pl.BlockSpec((None, tm, Fb), lambda m, d: (d, m, 0)),
                           pl.BlockSpec((None, None, Fb, D), lambda m, d: (d, 0, 0, 0)),
                           row, pl.BlockSpec((None, 9, D), lambda m, d: (0, 0, 0)), vec, row],
                 out_specs=[pl.BlockSpec((1, 1), lambda m, d: (0, 0)), row, row, vec, vec],
                 out_shape=[_sds((1, 1)), _sds((S, D)), _sds((S, D), _MXU), _sds((1, D)), _sds((1, D))],
                 scratch=[pltpu.VMEM((tm, D), F32)])(A, WD, X, mod2, fg, target)


def _ffn_dact(dYb, WD, layer, G, U, rows, name, jobs=()):
    Fb, D = WD.shape[-2:]

    def body(dy_ref, w_ref, g_ref, u_ref, dg_ref, du_ref):
        da = _dot_nt(dy_ref[...], w_ref[...])
        g = g_ref[...]
        sg = _sigmoid(g)
        dg_ref[...] = (da * u_ref[...] * (sg * (1.0 + g * (1.0 - sg)))).astype(dg_ref.dtype)
        du_ref[...] = (da * (g * sg)).astype(du_ref.dtype)

    tm = _tm(rows)
    blk = pl.BlockSpec((None, tm, Fb), lambda m, d: (d, m, 0))
    return _call(body, jobs=jobs, name=name, grid=(rows // tm, ND),
                 in_specs=[pl.BlockSpec((tm, D), lambda m, d: (m, 0)),
                           pl.BlockSpec((None, None, Fb, D), lambda m, d: (d, layer, 0, 0)), blk, blk],
                 out_specs=[blk, blk],
                 out_shape=[_sds((ND, rows, Fb), _MXU), _sds((ND, rows, Fb), _MXU)])(dYb, WD, G, U)


def _ffn_dh(dG, dU, WG, WU, rows, name, jobs=()):
    Fb, D = WG.shape[-2:]

    def body(dg_ref, du_ref, wg_ref, wu_ref, dh_ref, acc_ref):
        d = pl.program_id(1)

        @pl.when(d == 0)
        def _():
            acc_ref[...] = jnp.zeros_like(acc_ref)

        acc_ref[...] += _dot(dg_ref[...], wg_ref[...]) + _dot(du_ref[...], wu_ref[...])

        @pl.when(d == ND - 1)
        def _():
            dh_ref[...] = acc_ref[...]

    tm = _tm(rows)
    blk = pl.BlockSpec((None, tm, Fb), lambda m, d: (d, m, 0))
    wspec = pl.BlockSpec((None, None, Fb, D), lambda m, d: (d, 0, 0, 0))
    return _call(body, jobs=jobs, name=name, grid=(rows // tm, ND), in_specs=[blk, blk, wspec, wspec],
                 out_specs=pl.BlockSpec((tm, D), lambda m, d: (m, 0)), out_shape=_sds((rows, D)),
                 scratch=[pltpu.VMEM((tm, D), F32)])(dG, dU, WG, WU)


def _mm_tn(A, a_spec, B, b_spec, out_shape, out_spec, rows, name, prev=None, jobs=()):
    def body(*refs):
        a_ref, b_ref, o_ref = refs[0], refs[1], refs[-1]

        @pl.when(pl.program_id(1) == 0)
        def _():
            o_ref[...] = jnp.zeros_like(o_ref)

        o_ref[...] += _dot_tn(a_ref[...], b_ref[...])

    in_specs = [a_spec, b_spec]
    args = [A, B]
    aliases = None
    if prev is not None:
        in_specs.append(pl.BlockSpec(memory_space=pl.ANY))
        args.append(prev)
        aliases = {2: 0}
    return _call(body, jobs=jobs, name=name, grid=(ND, rows // _tm(rows)), in_specs=in_specs, out_specs=out_spec,
                 out_shape=_sds(out_shape), aliases=aliases)(*args)


def _proj_in(H2, WIN, name, jobs=()):
    R, D = H2.shape
    Nb = WIN.shape[-1]

    def body(h_ref, w_ref, p_ref):
        p_ref[...] = _dot(h_ref[...], w_ref[...])

    tm = _tm(R)
    return _call(body, jobs=jobs, name=name, grid=(ND, R // tm),
                 in_specs=[pl.BlockSpec((tm, D), lambda d, m: (m, 0)), pl.BlockSpec((None, D, Nb), lambda d, m: (d, 0, 0))],
                 out_specs=pl.BlockSpec((tm, Nb), lambda d, m: (m, d)), out_shape=_sds((R, ND * Nb)))(H2, WIN)


def _dproj_in(dP, WIN, name, jobs=()):
    R = dP.shape[0]
    D, Nb = WIN.shape[-2:]

    def body(dp_ref, w_ref, dh_ref, acc_ref):
        d = pl.program_id(1)

        @pl.when(d == 0)
        def _():
            acc_ref[...] = jnp.zeros_like(acc_ref)

        acc_ref[...] += _dot_nt(dp_ref[...], w_ref[...])

        @pl.when(d == ND - 1)
        def _():
            dh_ref[...] = acc_ref[...]

    tm = _tm(R)
    return _call(body, jobs=jobs, name=name, grid=(R // tm, ND),
                 in_specs=[pl.BlockSpec((tm, Nb), lambda m, d: (m, d)), pl.BlockSpec((None, D, Nb), lambda m, d: (d, 0, 0))],
                 out_specs=pl.BlockSpec((tm, D), lambda m, d: (m, 0)), out_shape=_sds((R, D)),
                 scratch=[pltpu.VMEM((tm, D), F32)])(dP, WIN)


def _proj_out(mixb, WOUT, X1, mod2, ng, S, name):
    D = WOUT.shape[0]

    def body(m_ref, w_ref, x_ref, mod_ref, g_ref, z_ref, xn_ref, h_ref):
        z = _dot(m_ref[...], w_ref[...])
        z_ref[...] = z
        xn = x_ref[...] + mod_ref[5:6, :] * z
        xn_ref[...] = xn
        n = (xn * lax.rsqrt(jnp.mean(xn * xn, axis=-1, keepdims=True) + EPS)) * g_ref[...]
        h_ref[...] = (n * (1.0 + mod_ref[7:8, :]) + mod_ref[6:7, :]).astype(h_ref.dtype)

    return _call(body, name=name, grid=(S // TR,),
                 in_specs=[_row_spec(D), _full_spec((D, D)), _row_spec(D), pl.BlockSpec((None, 9, D), lambda i: (0, 0, 0)),
                           _full_spec((1, D))],
                 out_specs=[_row_spec(D), _row_spec(D), _row_spec(D)],
                 out_shape=[_sds((S, D)), _sds((S, D)), _sds((S, D), _MXU)])(mixb, WOUT, X1, mod2, ng)


def _dproj_out(dZb, WOUT, name):
    S, D = dZb.shape

    def body(dz_ref, w_ref, dm_ref):
        dm_ref[...] = _dot_nt(dz_ref[...], w_ref[...])

    return _call(body, name=name, grid=(S // TR,), in_specs=[_row_spec(D), _full_spec((D, D))],
                 out_specs=_row_spec(D), out_shape=_sds((S, D)))(dZb, WOUT)


def _pair_swap(t):
    lane = lax.broadcasted_iota(jnp.int32, t.shape, 1)
    return jnp.where(lane % 2 == 0, pltpu.roll(t, HD - 1, 1), pltpu.roll(t, 1, 1))


SM_SCALE = HD ** -0.5


def _qkv_prep(P, qg, kg, COS, SIN, D, KVW, name):
    R = P.shape[0]
    W = D + 2 * KVW
    nq, nk = D // HD, KVW // HD

    def body(p_ref, qg_ref, kg_ref, cos_ref, sin_ref, q_ref, k_ref, v_ref):
        cos, sin = cos_ref[...], sin_ref[...]

        def head(t, g):
            y = (t * lax.rsqrt(jnp.mean(t * t, axis=-1, keepdims=True) + EPS)) * g
            return y * cos + _pair_swap(y) * sin

        for h in range(nq):
            q_ref[:, h * HD:(h + 1) * HD] = (head(p_ref[:, h * HD:(h + 1) * HD], qg_ref[...]) * SM_SCALE).astype(q_ref.dtype)
        for h in range(nk):
            k_ref[:, h * HD:(h + 1) * HD] = head(p_ref[:, D + h * HD:D + (h + 1) * HD], kg_ref[...]).astype(k_ref.dtype)
        v_ref[...] = p_ref[:, D + KVW:W].astype(v_ref.dtype)

    return _call(body, name=name, grid=(R // TR,),
                 in_specs=[_row_spec(W), _full_spec((1, HD)), _full_spec((1, HD)), _row_spec(HD), _row_spec(HD)],
                 out_specs=[_row_spec(D), _row_spec(KVW), _row_spec(KVW)],
                 out_shape=[_sds((R, D), _MXU), _sds((R, KVW), _MXU), _sds((R, KVW), _MXU)])(P, qg, kg, COS, SIN)


def _qkv_bwd(P, dq, dk, dv, qg, kg, COS, SIN, dP, D, KVW, nx, name):
    R, INW = P.shape
    W = D + 2 * KVW
    nq, nk = D // HD, KVW // HD

    def body(p_ref, dq_ref, dk_ref, dv_ref, qg_ref, kg_ref, cos_ref, sin_ref, dp_in, dp_ref, dqg_ref, dkg_ref):
        i = pl.program_id(0)
        cos, sin = cos_ref[...], sin_ref[...]

        @pl.when(i == 0)
        def _():
            dqg_ref[...] = jnp.zeros_like(dqg_ref)
            dkg_ref[...] = jnp.zeros_like(dkg_ref)

        def head_bwd(t, g, dout):
            r = lax.rsqrt(jnp.mean(t * t, axis=-1, keepdims=True) + EPS)
            n = t * r
            dy = dout * cos + _pair_swap(dout * sin)
            dn = dy * g
            return r * (dn - n * jnp.mean(dn * n, axis=-1, keepdims=True)), jnp.sum(dy * n, axis=0, keepdims=True)

        dqg = jnp.zeros((1, HD), F32)
        for h in range(nq):
            sl = slice(h * HD, (h + 1) * HD)
            dt, dg = head_bwd(p_ref[:, sl], qg_ref[...], jnp.where(i < nx, dq_ref[:, sl] * SM_SCALE, 0.0))
            dp_ref[:, sl] = dt.astype(dp_ref.dtype)
            dqg += dg
        dkg = jnp.zeros((1, HD), F32)
        for h in range(nk):
            sl = slice(h * HD, (h + 1) * HD)
            dt, dg = head_bwd(p_ref[:, D + h * HD:D + (h + 1) * HD], kg_ref[...], dk_ref[:, sl])
            dp_ref[:, D + h * HD:D + (h + 1) * HD] = dt.astype(dp_ref.dtype)
            dkg += dg
        dqg_ref[...] += dqg
        dkg_ref[...] += dkg
        dp_ref[:, D + KVW:W] = dv_ref[...].astype(dp_ref.dtype)

    return _call(body, name=name, grid=(R // TR,),
                 in_specs=[_row_spec(W), _row_spec(D, nx - 1), _row_spec(KVW), _row_spec(KVW), _full_spec((1, HD)),
                           _full_spec((1, HD)), _row_spec(HD), _row_spec(HD), pl.BlockSpec(memory_space=pl.ANY)],
                 out_specs=[_row_spec(W), _full_spec((1, HD)), _full_spec((1, HD))],
                 out_shape=[_sds((R, INW), _MXU), _sds((1, HD)), _sds((1, HD))],
                 aliases={8: 0})(P, dq, dk, dv, qg, kg, COS, SIN, dP)


def _stack_heads(ref, G, dtype=None):
    parts = [ref[:, g * HD:(g + 1) * HD] for g in range(G)]
    out = jnp.concatenate(parts, axis=0)
    return out if dtype is None else out.astype(dtype)


_KEY_CHUNKS = 4


def _key_chunks(R):
    unit = 256 if R % 256 == 0 else 16
    nt = R // unit
    n = min(_KEY_CHUNKS, nt)
    bounds = [0]
    for i in range(n):
        bounds.append(bounds[-1] + (nt // n + (1 if i < nt % n else 0)) * unit)
    return bounds


def _attn_fwd(q, k, v, S, G, name, jobs=()):
    R, KVW = k.shape
    D = q.shape[1]
    Kh = KVW // HD
    tq = 128
    kb = _key_chunks(R)

    def body(q_ref, k_ref, v_ref, o_ref, lse_ref):
        qs = _stack_heads(q_ref, G)
        m = l = acc = None
        for c in range(len(kb) - 1):
            s = _dot_nt(qs, k_ref[kb[c]:kb[c + 1], :])
            mc = jnp.max(s, axis=-1, keepdims=True)
            m_new = mc if c == 0 else jnp.maximum(m, mc)
            p = jnp.exp(s - m_new)
            ps = jnp.sum(p, axis=-1, keepdims=True)
            pv = _dot(p, v_ref[kb[c]:kb[c + 1], :])
            if c == 0:
                l, acc = ps, pv
            else:
                alpha = jnp.exp(m - m_new)
                l = alpha * l + ps
                acc = alpha * acc + pv
            m = m_new
        o = acc / l
        lse = m + jnp.log(l)
        for g in range(G):
            o_ref[:, g * HD:(g + 1) * HD] = o[g * tq:(g + 1) * tq, :]
            lse_ref[g] = jnp.broadcast_to(lse[g * tq:(g + 1) * tq, :], (tq, HD))

    return _call(body, jobs=jobs, name=name, grid=(Kh, S // tq),
                 in_specs=[pl.BlockSpec((tq, G * HD), lambda h, i: (i, h)), pl.BlockSpec((R, HD), lambda h, i: (0, h)),
                           pl.BlockSpec((R, HD), lambda h, i: (0, h))],
                 out_specs=[pl.BlockSpec((tq, G * HD), lambda h, i: (i, h)),
                            pl.BlockSpec((None, G, tq, HD), lambda h, i: (h, 0, i, 0))],
                 out_shape=[_sds((S, D)), _sds((Kh, G, S, HD))])(q, k, v)


def _attn_bwd(q, k, v, O, LSE, dOb, S, G, name, jobs=()):
    R, KVW = k.shape
    D = q.shape[1]
    Kh = KVW // HD
    tq = 128
    kb = _key_chunks(R)

    def body(q_ref, k_ref, v_ref, o_ref, lse_ref, do_ref, dq_ref, dk_ref, dv_ref):
        @pl.when(pl.program_id(1) == 0)
        def _():
            dk_ref[...] = jnp.zeros_like(dk_ref)
            dv_ref[...] = jnp.zeros_like(dv_ref)

        qs = _stack_heads(q_ref, G)
        do = _stack_heads(do_ref, G)
        o = _stack_heads(o_ref, G)
        lse = jnp.concatenate([lse_ref[g][:, 0:1] for g in range(G)], axis=0)
        delta = jnp.sum(do.astype(F32) * o, axis=-1, keepdims=True)
        dq = None
        for c in range(len(kb) - 1):
            rows = slice(kb[c], kb[c + 1])
            kk = k_ref[rows, :]
            p = jnp.exp(_dot_nt(qs, kk) - lse)
            dp = _dot_nt(do, v_ref[rows, :])
            ds = (p * (dp - delta)).astype(_MXU)
            dqc = _dot(ds, kk)
            dq = dqc if c == 0 else dq + dqc
            dk_ref[rows, :] += _dot_tn(ds, qs)
            dv_ref[rows, :] += _dot_tn(p, do)
        for g in range(G):
            dq_ref[:, g * HD:(g + 1) * HD] = dq[g * tq:(g + 1) * tq, :]

    qspec = pl.BlockSpec((tq, G * HD), lambda h, i: (i, h))
    kspec = pl.BlockSpec((R, HD), lambda h, i: (0, h))
    return _call(body, jobs=jobs, name=name, grid=(Kh, S // tq),
                 in_specs=[qspec, kspec, kspec, qspec, pl.BlockSpec((None, G, tq, HD), lambda h, i: (h, 0, i, 0)), qspec],
                 out_specs=[qspec, kspec, kspec],
                 out_shape=[_sds((S, D)), _sds((R, KVW)), _sds((R, KVW))])(q, k, v, O, LSE, dOb)


def _halo_specs(R, CB, col0):
    nt8 = TR // 8
    return [pl.BlockSpec((8, CB), lambda h, i: (jnp.maximum(i * nt8 - 1, 0), col0 + h)),
            pl.BlockSpec((TR, CB), lambda h, i: (i, col0 + h)),
            pl.BlockSpec((8, CB), lambda h, i: (jnp.minimum((i + 1) * nt8, R // 8 - 1), col0 + h))]


def _seq_pos(i, S, R, CB):
    t = i * TR - 8 + lax.broadcasted_iota(jnp.int32, (TR + 16, CB), 0)
    start = jnp.where(t >= S, S, 0)
    end = jnp.where(t >= S, R, S)
    return t - start, end - t


def _shift(cat, by):
    return pltpu.roll(cat, by % cat.shape[0], 0)


def _gate_mats(xcb, w_ref, dirn, nb):
    return jnp.concatenate([_dot(xcb[:, b * HD:(b + 1) * HD], w_ref[dirn, b]) for b in range(nb)], axis=1)


def _lru_gates_fwd(P, conv_w, conv_b, WA, WX, ba, bx, lam, S, D, col0, name):
    R = P.shape[0]
    CB = D // 2
    nb = CB // HD

    def body(xp_ref, x_ref, xn_ref, cw_ref, cb_ref, wa_ref, wx_ref, ba_ref, bx_ref, lam_ref,
             xc_ref, af_ref, uf_ref, ab_ref, ub_ref):
        i = pl.program_id(1)
        cat = jnp.concatenate([xp_ref[...], x_ref[...], xn_ref[...]], axis=0)
        from_start, to_end = _seq_pos(i, S, R, CB)
        conv = (cb_ref[...] + cw_ref[2:3, :] * cat
                + cw_ref[0:1, :] * jnp.where(from_start >= 2, _shift(cat, 2), 0.0)
                + cw_ref[1:2, :] * jnp.where(from_start >= 1, _shift(cat, 1), 0.0)
                + cw_ref[3:4, :] * jnp.where(to_end >= 2, _shift(cat, -1), 0.0))
        xc = conv[8:8 + TR, :]
        xc_ref[...] = xc
        xcb = xc.astype(_MXU)
        for dirn, (a_ref, u_ref) in enumerate(((af_ref, uf_ref), (ab_ref, ub_ref))):
            ra = _sigmoid(_gate_mats(xcb, wa_ref, dirn, nb) + ba_ref[dirn:dirn + 1, :])
            ia = _sigmoid(_gate_mats(xcb, wx_ref, dirn, nb) + bx_ref[dirn:dirn + 1, :])
            nl = -lam_ref[dirn:dirn + 1, :]
            sp = jnp.maximum(nl, 0.0) + jnp.log(1.0 + jnp.exp(-jnp.abs(nl)))
            la = (-LRU_C) * ra * sp
            a = jnp.exp(la)
            a_ref[...] = a
            u_ref[...] = jnp.sqrt(1.0 - a * a) * (ia * xc)

    def par(r):
        return pl.BlockSpec((r, CB), lambda h, i: (0, h))

    wspec = pl.BlockSpec((2, nb, HD, HD), lambda h, i: (0, h, 0, 0))
    out = pl.BlockSpec((TR, CB), lambda h, i: (i, h))
    return _call(body, name=name, grid=(2, R // TR),
                 in_specs=_halo_specs(R, CB, col0) + [par(4), par(1), wspec, wspec, par(2), par(2), par(2)],
                 out_specs=[out] * 5, out_shape=[_sds((R, D))] * 5,
                 )(P, P, P, conv_w, conv_b, WA, WX, ba, bx, lam)


def _lru_gates_bwd(xc, hf, hb, gf, gb, WA, WX, ba, bx, lam, S, name):
    R, D = xc.shape
    CB = D // 2
    nb = CB // HD

    def body(xc_ref, hfq_ref, hf_ref, hfl_ref, hb_ref, hbn_ref, gf_ref, gb_ref, wa_ref, wx_ref, ba_ref, bx_ref, lam_ref,
             dxc_ref, dwa_ref, dwx_ref, dba_ref, dbx_ref, dlam_ref):
        @pl.when(pl.program_id(1) == 0)
        def _():
            for r in (dwa_ref, dwx_ref, dba_ref, dbx_ref, dlam_ref):
                r[...] = jnp.zeros_like(r)

        t = pl.program_id(1) * TR + lax.broadcasted_iota(jnp.int32, (TR, CB), 0)
        hfp = _shift(jnp.concatenate([hfq_ref[...], hf_ref[...]], axis=0), 1)[8:8 + TR, :]
        hfp = jnp.where(t == 0, hfl_ref[7:8, :], jnp.where(t == S, 0.0, hfp))
        hbp = _shift(jnp.concatenate([hb_ref[...], hbn_ref[...]], axis=0), -1)[0:TR, :]
        hbp = jnp.where(t == R - 1, 0.0, hbp)

        xc = xc_ref[...]
        xcb = xc.astype(_MXU)
        dxc = jnp.zeros_like(xc)
        for dirn, (hp, g_ref) in enumerate(((hfp, gf_ref), (hbp, gb_ref))):
            ra = _sigmoid(_gate_mats(xcb, wa_ref, dirn, nb) + ba_ref[dirn:dirn + 1, :])
            ia = _sigmoid(_gate_mats(xcb, wx_ref, dirn, nb) + bx_ref[dirn:dirn + 1, :])
            nl = -lam_ref[dirn:dirn + 1, :]
            sp = jnp.maximum(nl, 0.0) + jnp.log(1.0 + jnp.exp(-jnp.abs(nl)))
            la = (-LRU_C) * ra * sp
            a = jnp.exp(la)
            e2 = a * a
            rs = lax.rsqrt(1.0 - e2)
            s = (1.0 - e2) * rs
            du = g_ref[...]
            dla = du * hp * a - du * (ia * xc) * (e2 * rs)
            dxc += du * s * ia
            dza = (dla * (-LRU_C) * sp) * ra * (1.0 - ra)
            dzx = (du * s * xc) * ia * (1.0 - ia)
            dlam_ref[dirn:dirn + 1, :] += jnp.sum(dla * (LRU_C * ra) * _sigmoid(nl), axis=0, keepdims=True)
            dba_ref[dirn:dirn + 1, :] += jnp.sum(dza, axis=0, keepdims=True)
            dbx_ref[dirn:dirn + 1, :] += jnp.sum(dzx, axis=0, keepdims=True)
            dzab, dzxb = dza.astype(_MXU), dzx.astype(_MXU)
            parts = []
            for b in range(nb):
                sl = slice(b * HD, (b + 1) * HD)
                dwa_ref[dirn, b] += _dot_tn(xcb[:, sl], dzab[:, sl])
                dwx_ref[dirn, b] += _dot_tn(xcb[:, sl], dzxb[:, sl])
                parts.append(_dot_nt(dzab[:, sl], wa_ref[dirn, b]) + _dot_nt(dzxb[:, sl], wx_ref[dirn, b]))
            dxc += jnp.concatenate(parts, axis=1)
        dxc_ref[...] = dxc

    def par(r):
        return pl.BlockSpec((r, CB), lambda h, i: (0, h))

    wspec = pl.BlockSpec((2, nb, HD, HD), lambda h, i: (0, h, 0, 0))
    tile = pl.BlockSpec((TR, CB), lambda h, i: (i, h))
    before, _, after = _halo_specs(R, CB, 0)
    last = pl.BlockSpec((8, CB), lambda h, i: (R // 8 - 1, h))
    nbt = D // HD
    return _call(body, name=name, grid=(2, R // TR),
                 in_specs=[tile, before, tile, last, tile, after, tile, tile, wspec, wspec, par(2), par(2), par(2)],
                 out_specs=[tile, wspec, wspec, par(2), par(2), par(2)],
                 out_shape=[_sds((R, D)), _sds((2, nbt, HD, HD)), _sds((2, nbt, HD, HD)), _sds((2, D)), _sds((2, D)),
                            _sds((2, D))])(xc, hf, hf, hf, hb, hb, gf, gb, WA, WX, ba, bx, lam)


def _scan_rows(n_groups, step, init):
    return lax.fori_loop(0, n_groups, lambda gi, c: step(pl.multiple_of(gi * 8, 8), c), init)


def _lru_scan_fwd(af, uf, ab, ub, S, name):
    R, D = af.shape
    W = min(SCAN_W, D)
    nm, nx = R // TR, S // TR
    nc = nm - nx
    ng = TR // 8

    def body(af_ref, uf_ref, ab_ref, ub_ref, hf_ref, hb_ref, cf_ref, cb_ref):
        @pl.when(pl.program_id(1) == 0)
        def _():
            cf_ref[...] = jnp.zeros_like(cf_ref)
            cb_ref[...] = jnp.zeros_like(cb_ref)

        def step(base, carry):
            hf, hb = carry
            baseb = pl.multiple_of(TR - 8 - base, 8)
            for r in range(8):
                tf, tb = base + r, baseb + 7 - r
                hf = af_ref[pl.ds(tf, 1), :] * hf + uf_ref[pl.ds(tf, 1), :]
                hf_ref[pl.ds(tf, 1), :] = hf
                hb = ab_ref[pl.ds(tb, 1), :] * hb + ub_ref[pl.ds(tb, 1), :]
                hb_ref[pl.ds(tb, 1), :] = hb
            return hf, hb

        hf, hb = _scan_rows(ng, step, (cf_ref[0:1, :], cb_ref[0:1, :]))
        cf_ref[0:1, :] = hf
        cb_ref[0:1, :] = hb

    fmap = lambda j, s: (jnp.where(s < nc, nx + s, s - nc), j)
    bmap = lambda j, s: (nm - 1 - s, j)
    fs, bs = pl.BlockSpec((TR, W), fmap), pl.BlockSpec((TR, W), bmap)
    return _call(body, name=name, grid=(D // W, nm), in_specs=[fs, fs, bs, bs], out_specs=[fs, bs],
                 out_shape=[_sds((R, D))] * 2, scratch=[pltpu.VMEM((8, W), F32), pltpu.VMEM((8, W), F32)])(af, uf, ab, ub)


def _lru_scan_bwd(af, ab, dhs, S, name):
    R, D = af.shape
    W = min(SCAN_W, D)
    nm, nx = R // TR, S // TR
    ng = TR // 8

    def body(af_ref, dhf_ref, ab_ref, dhb_ref, gf_ref, gb_ref, cf_ref, cb_ref):
        @pl.when(pl.program_id(1) == 0)
        def _():
            cf_ref[...] = jnp.zeros_like(cf_ref)
            cb_ref[...] = jnp.zeros_like(cb_ref)

        def step(base, carry):
            cf, cb = carry
            based = pl.multiple_of(TR - 8 - base, 8)
            for r in range(8):
                tf, tb = based + 7 - r, base + r
                g = dhf_ref[pl.ds(tf, 1), :] + cf
                gf_ref[pl.ds(tf, 1), :] = g
                cf = af_ref[pl.ds(tf, 1), :] * g
                g = dhb_ref[pl.ds(tb, 1), :] + cb
                gb_ref[pl.ds(tb, 1), :] = g
                cb = ab_ref[pl.ds(tb, 1), :] * g
            return cf, cb

        cf, cb = _scan_rows(ng, step, (cf_ref[0:1, :], cb_ref[0:1, :]))
        cf_ref[0:1, :] = cf
        cb_ref[0:1, :] = cb

    fmap = lambda j, s: (jnp.where(s < nx, nx - 1 - s, nm - 1 - (s - nx)), j)
    bmap = lambda j, s: (s, j)
    fs, bs = pl.BlockSpec((TR, W), fmap), pl.BlockSpec((TR, W), bmap)
    return _call(body, name=name, grid=(D // W, nm), in_specs=[fs, fs, bs, bs], out_specs=[fs, bs],
                 out_shape=[_sds((R, D))] * 2, scratch=[pltpu.VMEM((8, W), F32), pltpu.VMEM((8, W), F32)])(af, dhs, ab, dhs)


def _merge_fwd(P, hf, hb, O, S, D, col_lg, name):
    R = P.shape[0]
    CB = D // 2
    nx = S // TR

    def body(lg_ref, ga_ref, gl_ref, hf_ref, hb_ref, o_ref, mix_ref):
        ge, _ = _gelu_and_grad(lg_ref[...])
        lru = (hf_ref[...] + hb_ref[...]) * ge
        mix_ref[...] = (_sigmoid(ga_ref[...]) * o_ref[...] + _sigmoid(gl_ref[...]) * lru).astype(mix_ref.dtype)

    def col(c0):
        return pl.BlockSpec((TR, CB), lambda h, i: (i, c0 + h))

    return _call(body, name=name, grid=(2, R // TR),
                 in_specs=[col(col_lg), col(col_lg + 2), col(col_lg + 4), col(0), col(0),
                           pl.BlockSpec((TR, CB), lambda h, i: (jnp.minimum(i, nx - 1), h))],
                 out_specs=col(0), out_shape=_sds((R, D), _MXU))(P, P, P, hf, hb, O)


def _merge_bwd(dmix, P, hf, hb, O, S, D, col_lg, name, jobs=()):
    R = P.shape[0]
    CB = D // 2
    nx = S // TR

    def body(dm_ref, lg_ref, ga_ref, gl_ref, hf_ref, hb_ref, o_ref, do_ref, dhs_ref, dp_ref, stash, sems):
        h, i = pl.program_id(0), pl.program_id(1)
        dm = jnp.where(i < nx, dm_ref[...], 0.0)
        sa, sl = _sigmoid(ga_ref[...]), _sigmoid(gl_ref[...])
        ge, dge = _gelu_and_grad(lg_ref[...])
        hs = hf_ref[...] + hb_ref[...]
        dl = dm * sl
        do_ref[...] = (dm * sa).astype(do_ref.dtype)
        dhs_ref[...] = dl * ge
        stash[0] = (dl * hs * dge).astype(stash.dtype)
        stash[1] = (dm * o_ref[...] * sa * (1.0 - sa)).astype(stash.dtype)
        stash[2] = (dm * (hs * ge) * sl * (1.0 - sl)).astype(stash.dtype)
        rows = pl.ds(pl.multiple_of(i * TR, TR), TR)
        copies = [pltpu.make_async_copy(stash.at[sec], dp_ref.at[rows, pl.ds(pl.multiple_of((col_lg + 2 * sec + h) * CB, CB), CB)],
                                        sems.at[sec]) for sec in range(3)]
        for cp in copies:
            cp.start()
        for cp in copies:
            cp.wait()

    def col(c0):
        return pl.BlockSpec((TR, CB), lambda h, i: (i, c0 + h))

    xrow = pl.BlockSpec((TR, CB), lambda h, i: (jnp.minimum(i, nx - 1), h))
    return _call(body, jobs=jobs, name=name, grid=(2, R // TR),
                 in_specs=[xrow, col(col_lg), col(col_lg + 2), col(col_lg + 4), col(0), col(0), xrow],
                 out_specs=[col(0), col(0), pl.BlockSpec(memory_space=pl.ANY)],
                 out_shape=[_sds((R, D), _MXU), _sds((R, D)), _sds(P.shape, _MXU)],
                 scratch=[pltpu.VMEM((3, TR, CB), _MXU), pltpu.SemaphoreType.DMA((3,))])(dmix, P, P, P, hf, hb, O)


def _conv_bwd(dxc, P, conv_w, dP, S, D, col0, name, jobs=()):
    R = P.shape[0]
    CB = D // 2

    def body(dp_, d_ref, dn_, xp_ref, x_ref, xn_ref, cw_ref, dp_in, dpo_ref, dcw_ref, dcb_ref):
        i = pl.program_id(1)

        @pl.when(i == 0)
        def _():
            dcw_ref[...] = jnp.zeros_like(dcw_ref)
            dcb_ref[...] = jnp.zeros_like(dcb_ref)

        d = d_ref[...]
        catd = jnp.concatenate([dp_[...], d, dn_[...]], axis=0)
        catx = jnp.concatenate([xp_ref[...], x_ref[...], xn_ref[...]], axis=0)
        from_start, to_end = _seq_pos(i, S, R, CB)
        dxl = (cw_ref[2:3, :] * catd
               + cw_ref[0:1, :] * jnp.where(to_end >= 3, _shift(catd, -2), 0.0)
               + cw_ref[1:2, :] * jnp.where(to_end >= 2, _shift(catd, -1), 0.0)
               + cw_ref[3:4, :] * jnp.where(from_start >= 1, _shift(catd, 1), 0.0))
        dpo_ref[...] = dxl[8:8 + TR, :].astype(dpo_ref.dtype)
        taps = (jnp.where(from_start >= 2, _shift(catx, 2), 0.0), jnp.where(from_start >= 1, _shift(catx, 1), 0.0),
                catx, jnp.where(to_end >= 2, _shift(catx, -1), 0.0))
        for kk in range(4):
            dcw_ref[kk:kk + 1, :] += jnp.sum(d * taps[kk][8:8 + TR, :], axis=0, keepdims=True)
        dcb_ref[...] += jnp.sum(d, axis=0, keepdims=True)

    return _call(body, jobs=jobs, name=name, grid=(2, R // TR),
                 in_specs=_halo_specs(R, CB, 0) + _halo_specs(R, CB, col0)
                 + [pl.BlockSpec((4, CB), lambda h, i: (0, h)), pl.BlockSpec(memory_space=pl.ANY)],
                 out_specs=[pl.BlockSpec((TR, CB), lambda h, i: (i, col0 + h)), pl.BlockSpec((4, CB), lambda h, i: (0, h)),
                            pl.BlockSpec((1, CB), lambda h, i: (0, h))],
                 out_shape=[_sds(dP.shape, dP.dtype), _sds((4, D)), _sds((1, D))],
                 aliases={7: 0})(dxc, dxc, dxc, P, P, P, conv_w, dP)


def _rope_tables(S, C):
    t = jnp.arange(S, dtype=jnp.int32)
    row = (t // GRID_W).astype(F32)
    col = (t % GRID_W).astype(F32)
    axis_dims = HD // 2
    freqs = ROPE_THETA ** (-jnp.arange(0, axis_dims, 2, dtype=F32) / axis_dims)
    ang = jnp.concatenate([row[:, None] * freqs, col[:, None] * freqs], axis=-1)
    cos = jnp.repeat(jnp.cos(ang), 2, axis=-1)
    sin = jnp.repeat(jnp.sin(ang), 2, axis=-1) * jnp.tile(jnp.array([-1.0, 1.0], F32), HD // 2)
    return (jnp.concatenate([cos, jnp.ones((C, HD), F32)], axis=0),
            jnp.concatenate([sin, jnp.zeros((C, HD), F32)], axis=0))


def _local_step(x, ctx, target, modx, modc, ng, shards, qg, kg, conv_w, conv_b, WA, WX, ba, bx, lam, fg, idx, opt):
    S, D = x.shape
    C = ctx.shape[0]
    R = S + C
    nx = S // TR
    Nb = shards['w_in'].shape[-1]
    Fb = shards['wd0'].shape[1]
    KVW = (ND * Nb - 5 * D) // 2
    G = D // KVW
    CB = D // 2
    col_lx = (D + 2 * KVW) // CB
    assert S % TR == 0 and C % TR == 0 and (D + 2 * KVW) % CB == 0 and CB % HD == 0
    mod2 = jnp.stack([modx, modc])
    X0 = jnp.concatenate([x, ctx], axis=0)
    COS, SIN = _rope_tables(S, C)
    ng0, ng1, ng2 = ng[0:1], ng[1:2], ng[2:3]
    ag = lambda n: _ag_job(shards[n])
    sib = lambda Gp: _rs_sibling_job(Gp.reshape(ND, -1, Gp.shape[-1]))
    add = lambda Gp, bufA, tag: _rs_add(Gp.reshape(ND, -1, Gp.shape[-1]), bufA, idx, f"{tag}_rs_add")
    out = {}

    def tn_specs(rows):
        tm = _tm(rows)
        return pl.BlockSpec((None, tm, Fb), lambda d, r: (d, r, 0)), pl.BlockSpec((tm, D), lambda d, r: (r, 0))

    wd_spec = pl.BlockSpec((None, None, Fb, D), lambda d, r: (d, 0, 0, 0))

    ((WG0,),) = _comm_call([ag('wg0')], "ag_wg0")
    H1 = _norm_mod_fwd(X0, ng0, mod2, 0, R, nx, "ffn1_norm")
    G1, ((WU0,),) = _ffn_gate(H1, WG0, R, "ffn1_gate", jobs=[ag('wu0')])
    (U1, A1), ((WD0,),) = _ffn_up(H1, None, WU0, R, "ffn1_up", jobs=[ag('wd0')], G=G1)
    (Y1, X1, H2), ((WIN,),) = _ffn_down(A1, WD0, 0, X0, mod2, 0, R, S, "ffn1_down", jobs=[ag('w_in')], next_norm=(ng1, 1))
    P, ((WOUT,), (WG1,)) = _proj_in(H2, WIN, "proj_in", jobs=[ag('w_out'), ag('wg1')])
    q, k, v = _qkv_prep(P, qg, kg, COS, SIN, D, KVW, "qkv_prep")
    (O, LSE), ((WU1,), (WD1,)) = _attn_fwd(q, k, v, S, G, "attn_fwd", jobs=[ag('wu1'), ag('wd1')])
    WOUT = WOUT.reshape(D, D)
    xc, af, uf, ab, ub = _lru_gates_fwd(P, conv_w, conv_b, WA, WX, ba, bx, lam, S, D, col_lx, "lru_gates")
    hf, hb = _lru_scan_fwd(af, uf, ab, ub, S, "lru_scan")
    mixb = _merge_fwd(P, hf, hb, O, S, D, col_lx + 2, "merge")
    Z, X2, H3 = _proj_out(mixb, WOUT, X1, mod2, ng2, S, "proj_out")
    G3, U3, A3 = _ffn_up(H3, WG1, WU1, S, "ffn2_up")
    loss, dX3, dY3b, dg3, dfg = _ffn_down_loss(A3, WD1, X2, mod2, 2, fg, target, "ffn2_down_loss")
    dg3 = dg3[None]

    dG3, dU3 = _ffn_dact(dY3b, WD1, 0, G3, U3, S, "ffn2_dact")
    blk, row = tn_specs(S)
    dWD1 = _mm_tn(A3, blk, dY3b, row, (ND, 1, Fb, D), wd_spec, S, "ffn2_dwd")
    dWG1 = _mm_tn(dG3, blk, H3, row, (ND, 1, Fb, D), wd_spec, S, "ffn2_dwg")
    dWU1 = _mm_tn(dU3, blk, H3, row, (ND, 1, Fb, D), wd_spec, S, "ffn2_dwu")
    dH3, ((a_wd1,), (a_wg1,), (a_wu1,)) = _ffn_dh(dG3, dU3, WG1, WU1, S, "ffn2_dh", jobs=[sib(dWD1), sib(dWG1), sib(dWU1)])
    T_wd1, own_wd1 = add(dWD1, a_wd1, "wd1")
    T_wg1, own_wg1 = add(dWG1, a_wg1, "wg1")
    T_wu1, own_wu1 = add(dWU1, a_wu1, "wu1")
    dX2, dsh3, dsc3, dng2, dZb, dg2 = _norm_mod_bwd(X2, dH3, dX3, ng2, mod2, 2, S, nx, nx, "ffn2_dnorm", branch=(Z, 1, 1.0))

    dmix = _dproj_out(dZb, WOUT, "dproj_out")
    dWOUT = _mm_tn(mixb, pl.BlockSpec((_tm(S), D // ND), lambda d, r: (r, d)), dZb, pl.BlockSpec((_tm(S), D), lambda d, r: (r, 0)),
                   (ND, D // ND, D), pl.BlockSpec((None, D // ND, D), lambda d, r: (d, 0, 0)), S, "dw_out")
    (dOb, dhs, dP), ((a_wout,),) = _merge_bwd(dmix, P, hf, hb, O, S, D, col_lx + 2, "merge_bwd", jobs=[sib(dWOUT)])
    T_wout, own_wout = add(dWOUT, a_wout, "w_out")
    gf, gb = _lru_scan_bwd(af, ab, dhs, S, "lru_scan_bwd")
    dxc, dWA, dWX, dba, dbx, dlam = _lru_gates_bwd(xc, hf, hb, gf, gb, WA, WX, ba, bx, lam, S, "lru_gates_bwd")
    dLW = jnp.stack([dWA, dWX]).reshape(ND, -1, HD)
    (dP, dconv_w, dconv_b), ((a_lw,),) = _conv_bwd(dxc, P, conv_w, dP, S, D, col_lx, "conv_bwd", jobs=[sib(dLW)])
    T_lw, own_lw = add(dLW, a_lw, "lru_w")
    (dq, dk, dv), ((b_wd1,), (b_wg1,), (b_wu1,), (b_wout,), (b_lw,)) = _attn_bwd(
        q, k, v, O, LSE, dOb, S, G, "attn_bwd",
        jobs=[_rs_chips_job(T_wd1), _rs_chips_job(T_wg1), _rs_chips_job(T_wu1), _rs_chips_job(T_wout), _rs_chips_job(T_lw)])
    fin_wd = _rs_finish(own_wd1, b_wd1, "wd1_rs_finish", opt['ffn_wd'], 1, 2)
    fin_wg = _rs_finish(own_wg1, b_wg1, "wg1_rs_finish", opt['ffn_wg'], 1, 2)
    fin_wu = _rs_finish(own_wu1, b_wu1, "wu1_rs_finish", opt['ffn_wu'], 1, 2)
    out['w_out'] = _rs_finish(own_wout, b_wout, "w_out_rs_finish", opt['w_out'])
    (lw_sum,) = _rs_finish(own_lw, b_lw, "lru_w_rs_finish")
    dP, dqg, dkg = _qkv_bwd(P, dq, dk, dv, qg, kg, COS, SIN, dP, D, KVW, nx, "qkv_bwd")
    dH2, ((lw_full,),) = _dproj_in(dP, WIN, "dproj_in", jobs=[_ag_job(lw_sum)])
    dWIN = _mm_tn(H2, pl.BlockSpec((_tm(R), D), lambda d, r: (r, 0)), dP, pl.BlockSpec((_tm(R), Nb), lambda d, r: (r, d)),
                  (ND, D, Nb), pl.BlockSpec((None, D, Nb), lambda d, r: (d, 0, 0)), R, "dw_in")
    dX1, dsh2, dsc2, dng1, dY1b, dg1 = _norm_mod_bwd(X1, dH2, dX2, ng1, mod2, 1, R, nx, nx, "mix_dnorm", branch=(Y1, 0, 0.5))

    (dG1, dU1), ((a_win,),) = _ffn_dact(dY1b, WD0, 0, G1, U1, R, "ffn1_dact", jobs=[sib(dWIN)])
    T_win, own_win = add(dWIN, a_win, "w_in")
    blk, row = tn_specs(R)
    dWD0, ((b_win01,),) = _mm_tn(A1, blk, dY1b, row, (ND, 1, Fb, D), wd_spec, R, "ffn1_dwd", jobs=[_rs_chips_job(T_win, (0, 1))])
    dWG0, ((b_win2,), (a_wd0,)) = _mm_tn(dG1, blk, H1, row, (ND, 1, Fb, D), wd_spec, R, "ffn1_dwg",
                                         jobs=[_rs_chips_job(T_win, (2,)), sib(dWD0)])
    out['w_in'] = _rs_finish(own_win, [(b_win01, 0), (b_win01, 1), (b_win2, 0)], "w_in_rs_finish", opt['w_in'])
    T_wd0, own_wd0 = add(dWD0, a_wd0, "wd0")
    dWU0, ((a_wg0,), (b_wd0,)) = _mm_tn(dU1, blk, H1, row, (ND, 1, Fb, D), wd_spec, R, "ffn1_dwu",
                                        jobs=[sib(dWG0), _rs_chips_job(T_wd0)])
    T_wg0, own_wg0 = add(dWG0, a_wg0, "wg0")
    out['ffn_wd'] = _rs_finish(own_wd0, b_wd0, "wd0_rs_finish", opt['ffn_wd'], 0, 2, fin_wd)
    dH1, ((a_wu0,), (b_wg0,)) = _ffn_dh(dG1, dU1, WG0, WU0, R, "ffn1_dh", jobs=[sib(dWU0), _rs_chips_job(T_wg0)])
    T_wu0, own_wu0 = add(dWU0, a_wu0, "wu0")
    out['ffn_wg'] = _rs_finish(own_wg0, b_wg0, "wg0_rs_finish", opt['ffn_wg'], 0, 2, fin_wg)
    (dX0, dsh1, dsc1, dng0), ((b_wu0,),) = _norm_mod_bwd(
        X0, dH1, dX1, ng0, mod2, 0, R, nx, R // TR, "ffn1_dnorm", jobs=[_rs_chips_job(T_wu0, (0, 1))])
    out['wu0_pending'] = (T_wu0, own_wu0, b_wu0, fin_wu)
    out['lru_w'] = lw_full

    zero = jnp.zeros((1, D), F32)
    dmodx = jnp.concatenate([dsh1[0], dsc1[0], dg1[0], dsh2[0], dsc2[0], dg2[0], dsh3[0], dsc3[0], dg3[0]], axis=0)
    dmodc = jnp.concatenate([dsh1[1], dsc1[1], dg1[1], dsh2[1], dsc2[1], zero, zero, zero, zero], axis=0)
    out.update(loss=loss, grad_x=dX0[:S], dmodx=dmodx, dmodc=dmodc, norm_g=jnp.concatenate([dng0, dng1, dng2], axis=0),
               q_norm_g=dqg, k_norm_g=dkg, conv_w=dconv_w, conv_b=dconv_b, lru_ba=dba, lru_bx=dbx,
               lru_lambda=dlam, final_norm_g=dfg)
    return out


def _mesh_pos():
    return lax.axis_index("x"), lax.axis_index("y"), lax.axis_index("c")


def _all_gather(xb, name, in_vmem=False):
    space = pltpu.VMEM if in_vmem else pl.ANY

    def body(x_ref, out_ref, send_sems, recv_sems, local_sem):
        x, y, c = _mesh_pos()
        me, sibling = (x, y, c), (x, y, 1 - c)
        chips = [(1 - x, y), (x, 1 - y), (1 - x, 1 - y)]

        def slot(px, py, pc):
            return out_ref.at[4 * px + 2 * py + pc]

        def copy(k, block, to, src=None):
            return pltpu.make_async_remote_copy(
                src_ref=slot(*block) if src is None else src, dst_ref=slot(*block),
                send_sem=send_sems.at[k], recv_sem=recv_sems.at[k], device_id=to, device_id_type=MESH)

        mine = pltpu.make_async_copy(x_ref, slot(*me), local_sem)
        mine.start()
        first = [copy(0, me, sibling, src=x_ref)]
        first += [copy(1 + j, me, (*chip, c), src=x_ref) for j, chip in enumerate(chips)]
        for cp in first:
            cp.start()
        passed = [copy(4 + j, (*chip, c), sibling) for j, chip in enumerate(chips)]
        for j, chip in enumerate(chips):
            copy(1 + j, (*chip, c), me).wait_recv()
            passed[j].start()
        copy(0, sibling, me).wait_recv()
        for j, chip in enumerate(chips):
            copy(4 + j, (*chip, 1 - c), me).wait_recv()
        for cp in first + passed:
            cp.wait_send()
        mine.wait()

    return pl.pallas_call(
        body, name=name, out_shape=_sds((ND,) + xb.shape, xb.dtype),
        in_specs=[pl.BlockSpec(memory_space=space)], out_specs=pl.BlockSpec(memory_space=space),
        scratch_shapes=[pltpu.SemaphoreType.DMA((7,)), pltpu.SemaphoreType.DMA((7,)), pltpu.SemaphoreType.DMA(())])(xb)


def _ag_job(xb):
    def env(ins, outs, sems):
        x_ref, out_ref = ins[0], outs[0]
        send_sems, recv_sems, local_sem = sems
        x, y, c = _mesh_pos()

        def slot(px, py, pc):
            return out_ref.at[4 * px + 2 * py + pc]

        def copy(k, block, to, src=None):
            return pltpu.make_async_remote_copy(
                src_ref=slot(*block) if src is None else src, dst_ref=slot(*block),
                send_sem=send_sems.at[k], recv_sem=recv_sems.at[k], device_id=to, device_id_type=MESH)

        return x_ref, local_sem, slot, copy, (x, y, c)

    def start(ins, outs, sems):
        x_ref, local_sem, slot, copy, (x, y, c) = env(ins, outs, sems)
        me = (x, y, c)
        pltpu.make_async_copy(x_ref, slot(*me), local_sem).start()
        copy(0, me, (x, y, 1 - c), src=x_ref).start()
        copy(1, me, (1 - x, y, c), src=x_ref).start()
        copy(2, me, (x, 1 - y, c), src=x_ref).start()

    def relay(ins, outs, sems):
        _, _, _, copy, (x, y, c) = env(ins, outs, sems)
        xn, yn = (1 - x, y, c), (x, 1 - y, c)
        copy(1, xn, (x, y, c)).wait_recv()
        copy(2, yn, (x, y, c)).wait_recv()
        src = (jnp.where(c == 0, x, 1 - x), jnp.where(c == 0, 1 - y, y), c)
        dst = (jnp.where(c == 0, 1 - x, x), jnp.where(c == 0, y, 1 - y), c)
        copy(3, src, dst).start()
        copy(4, xn, (x, y, 1 - c)).start()
        copy(5, yn, (x, y, 1 - c)).start()

    def relay2(ins, outs, sems):
        _, _, _, copy, (x, y, c) = env(ins, outs, sems)
        dg = (1 - x, 1 - y, c)
        copy(3, dg, (x, y, c)).wait_recv()
        copy(6, dg, (x, y, 1 - c)).start()

    def finish(ins, outs, sems):
        x_ref, local_sem, slot, copy, (x, y, c) = env(ins, outs, sems)
        me, sib = (x, y, c), (x, y, 1 - c)
        copy(0, sib, me).wait_recv()
        for k, chip in ((4, (1 - x, y)), (5, (x, 1 - y)), (6, (1 - x, 1 - y))):
            copy(k, (*chip, 1 - c), me).wait_recv()
        for k in range(7):
            copy(k, me, me).wait_send()
        pltpu.make_async_copy(x_ref, slot(*me), local_sem).wait()

    return _Job([xb], [_sds((ND,) + xb.shape, xb.dtype)],
                [pltpu.SemaphoreType.DMA((7,)), pltpu.SemaphoreType.DMA((7,)), pltpu.SemaphoreType.DMA(())],
                start, finish, relay, relay2)


def _rs_sibling_job(Gp):
    def copies(ins, outs, sems):
        x, y, c = _mesh_pos()
        return [pltpu.make_async_remote_copy(
            src_ref=ins[0].at[2 * k + (1 - c)], dst_ref=outs[0].at[k], send_sem=sems[0].at[k], recv_sem=sems[1].at[k],
            device_id=(x, y, 1 - c), device_id_type=MESH) for k in range(4)]

    def start(ins, outs, sems):
        for cp in copies(ins, outs, sems):
            cp.start()

    def finish(ins, outs, sems):
        cps = copies(ins, outs, sems)
        for cp in cps:
            cp.wait_recv()
        for cp in cps:
            cp.wait_send()

    return _Job([Gp], [_sds((4,) + Gp.shape[1:], Gp.dtype)],
                [pltpu.SemaphoreType.DMA((4,)), pltpu.SemaphoreType.DMA((4,))], start, finish)


def _rs_chips_job(T, dests=(0, 1, 2)):
    def copies(ins, outs, sems):
        x, y, c = _mesh_pos()
        chips = [(1 - x, y), (x, 1 - y), (1 - x, 1 - y)]
        cps = []
        for i, j in enumerate(dests):
            px, py = chips[j]
            cps.append(pltpu.make_async_remote_copy(
                src_ref=ins[0].at[2 * px + py], dst_ref=outs[0].at[i], send_sem=sems[0].at[i], recv_sem=sems[1].at[i],
                device_id=(px, py, c), device_id_type=MESH))
        return cps

    def start(ins, outs, sems):
        for cp in copies(ins, outs, sems):
            cp.start()

    def finish(ins, outs, sems):
        cps = copies(ins, outs, sems)
        for cp in cps:
            cp.wait_recv()
        for cp in cps:
            cp.wait_send()

    n = len(dests)
    return _Job([T], [_sds((n,) + T.shape[1:], T.dtype)],
                [pltpu.SemaphoreType.DMA((n,)), pltpu.SemaphoreType.DMA((n,))], start, finish)


def _tile_rows(rows, cols):
    best = None
    for t in range(16, rows + 1, 16):
        if rows % t == 0 and t * cols * 4 <= (3 << 19):
            best = t
    return best if best is not None else rows


def _prefetch_call(body, *, name, grid, in_specs, out_specs, out_shape):
    return pl.pallas_call(
        body, name=name, out_shape=out_shape,
        grid_spec=pltpu.PrefetchScalarGridSpec(num_scalar_prefetch=1, grid=grid, in_specs=in_specs, out_specs=out_specs),
        compiler_params=pltpu.CompilerParams(dimension_semantics=("arbitrary",) * len(grid), vmem_limit_bytes=VMEM_LIMIT))


def _rs_add(Gp, bufA, idx, name):
    rows, cols = Gp.shape[1:]
    tr = _tile_rows(rows, cols)

    def body(i_ref, g_ref, a_ref, t_ref, own_ref):
        t = g_ref[...] + a_ref[...]
        t_ref[...] = t.astype(t_ref.dtype)

        @pl.when(pl.program_id(1) == i_ref[1])
        def _():
            own_ref[...] = t

    return _prefetch_call(
        body, name=name, grid=(rows // tr, 4),
        in_specs=[pl.BlockSpec((None, tr, cols), lambda r, k, i_ref: (2 * k + i_ref[0], r, 0)),
                  pl.BlockSpec((None, tr, cols), lambda r, k, i_ref: (k, r, 0))],
        out_specs=[pl.BlockSpec((None, tr, cols), lambda r, k, i_ref: (k, r, 0)),
                   pl.BlockSpec((tr, cols), lambda r, k, i_ref: (r, 0))],
        out_shape=[_sds((4, rows, cols), jnp.bfloat16), _sds((rows, cols))])(idx, Gp, bufA)


def _adam(w, g, m, v):
    m = ADAM_B1 * m + (1.0 - ADAM_B1) * g
    v = ADAM_B2 * v + (1.0 - ADAM_B2) * (g * g)
    m_hat = m / (1.0 - ADAM_B1 ** ADAM_STEP)
    v_hat = v / (1.0 - ADAM_B2 ** ADAM_STEP)
    return -ADAM_LR * (m_hat / (jnp.sqrt(v_hat) + ADAM_EPS) + ADAM_WD * w), m, v


def _rs_finish(Town, bufB, name, wmv=None, slab=0, n_slabs=1, prev=None):
    rows, cols = Town.shape
    tr = _tile_rows(rows, cols)
    nr = rows // tr
    n_in = 4 + (3 if wmv is not None else 0)
    n_out = 4 if wmv is not None else 1

    def body(*refs):
        ins, outs = refs[:n_in], refs[len(refs) - n_out:]
        g = ((ins[0][...] + ins[1][...].astype(F32)) + ins[2][...].astype(F32)) + ins[3][...].astype(F32)
        outs[0][...] = g
        if wmv is not None:
            d, m, v = _adam(ins[4][...], g, ins[5][...], ins[6][...])
            outs[1][...] = d
            outs[2][...] = m
            outs[3][...] = v

    plain = pl.BlockSpec((tr, cols), lambda r: (r, 0))
    slabbed = pl.BlockSpec((tr, cols), lambda r: (slab * nr + r, 0))
    pairs = bufB if isinstance(bufB, list) else [(bufB, j) for j in range(3)]
    in_specs = [plain] + [pl.BlockSpec((None, tr, cols), (lambda j: lambda r: (j, r, 0))(j)) for _, j in pairs]
    args = [Town] + [a for a, _ in pairs]
    if wmv is not None:
        in_specs += [slabbed] * 3
        args += list(wmv)
    aliases = None
    if prev is not None:
        in_specs += [pl.BlockSpec(memory_space=pl.ANY)] * n_out
        aliases = {len(args) + i: i for i in range(n_out)}
        args += list(prev)
    return _call(body, name=name, grid=(nr,), in_specs=in_specs, out_specs=[slabbed] * n_out,
                 out_shape=[_sds((n_slabs * rows, cols))] * n_out, aliases=aliases)(*args)


def _adamw_plain(w, g, m, v, name, jobs=()):
    rows, cols = w.shape
    tr = _tile_rows(rows, cols)

    def body(w_ref, g_ref, m_ref, v_ref, d_ref, mo_ref, vo_ref):
        d, m_, v_ = _adam(w_ref[...], g_ref[...], m_ref[...], v_ref[...])
        d_ref[...] = d
        mo_ref[...] = m_
        vo_ref[...] = v_

    spec = pl.BlockSpec((tr, cols), lambda r: (r, 0))
    return _call(body, jobs=jobs, name=name, grid=(rows // tr,), in_specs=[spec] * 4, out_specs=[spec] * 3,
                 out_shape=[_sds((rows, cols))] * 3)(w, g, m, v)


_MOD_TK = 512


def _mod_fwd(cc16, w_loc, b_loc, name):
    D, cols = w_loc.shape
    tk = min(_MOD_TK, D)
    nk = D // tk

    def body(c_ref, w_ref, b_ref, o_ref):
        kk = pl.program_id(0)

        @pl.when(kk == 0)
        def _():
            o_ref[...] = jnp.zeros_like(o_ref)

        cc = c_ref[...]
        o_ref[...] += _dot(cc * _sigmoid(cc), w_ref[...])

        @pl.when(kk == nk - 1)
        def _():
            o_ref[...] += b_ref[...]

    return _call(body, name=name, grid=(nk,),
                 in_specs=[pl.BlockSpec((16, tk), lambda kk: (0, kk)), pl.BlockSpec((tk, cols), lambda kk: (kk, 0)),
                           _full_spec((1, cols))],
                 out_specs=_full_spec((16, cols)), out_shape=_sds((16, cols)))(cc16, w_loc, b_loc)


def _mod_bwd(dm_loc, cc16, w_loc, name):
    D, cols = w_loc.shape

    def body(dm_ref, c_ref, w_ref, gw_ref, ds_ref):
        rows = [dm_ref[b, 0:1, :] for b in range(ND)]
        ctx = dm_ref[0, 1:2, :]
        for b in range(1, ND):
            ctx = ctx + dm_ref[b, 1:2, :]
        dm16 = jnp.concatenate(rows + [ctx, jnp.zeros((7, cols), F32)], axis=0)
        cc = c_ref[...]
        gw_ref[...] = _dot_tn(cc * _sigmoid(cc), dm16)
        ds_ref[...] = _dot_nt(dm16, w_ref[...])

    tk = min(_MOD_TK, D)
    return _call(body, name=name, grid=(D // tk,),
                 in_specs=[_full_spec((ND, 8, cols)), pl.BlockSpec((16, tk), lambda kk: (0, kk)),
                           pl.BlockSpec((tk, cols), lambda kk: (kk, 0))],
                 out_specs=[pl.BlockSpec((tk, cols), lambda kk: (kk, 0)), pl.BlockSpec((16, tk), lambda kk: (0, kk))],
                 out_shape=[_sds((D, cols)), _sds((16, D))])(dm_loc, cc16, w_loc)


def _bmod_grad(dm_all, name):
    n = dm_all.shape[-1]

    def body(dm_ref, o_ref):
        acc = dm_ref[0, 0:1, :] + dm_ref[0, 1:2, :]
        for b in range(1, ND):
            acc = (acc + dm_ref[b, 0:1, :]) + dm_ref[b, 1:2, :]
        o_ref[...] = acc

    return _call(body, name=name, grid=(1,), in_specs=[_full_spec((ND, 8, n))], out_specs=_full_spec((1, n)),
                 out_shape=_sds((1, n)))(dm_all)


_SMALL_ROWS = 24
_ROW_CCTX = 15


def _small_finish(parts, c_ctx, name):
    D = parts.shape[-1]

    def body(p_ref, c_ref, o_ref):
        acc = p_ref[0]
        for b in range(1, ND):
            acc = acc + p_ref[b]
        cc = c_ref[...]
        sg = _sigmoid(cc)
        dsilu = sg * (1.0 + cc * (1.0 - sg))
        row = lax.broadcasted_iota(jnp.int32, acc.shape, 0)
        o_ref[...] = jnp.where(row == _ROW_CCTX, acc * dsilu, acc)

    return _call(body, name=name, grid=(1,), in_specs=[_full_spec(parts.shape), _full_spec((1, D))],
                 out_specs=_full_spec((_SMALL_ROWS, D)), out_shape=_sds((_SMALL_ROWS, D)))(parts, c_ctx)


_WEIGHTS = ['c_ctx', 'w_mod', 'b_mod', 'norm_g', 'ffn_wg', 'ffn_wu', 'ffn_wd', 'w_in', 'w_out', 'q_norm_g', 'k_norm_g',
            'conv_w', 'conv_b', 'lru_wa', 'lru_ba', 'lru_wx', 'lru_bx', 'lru_lambda', 'final_norm_g']
_SMALL = ['c_ctx', 'b_mod', 'norm_g', 'q_norm_g', 'k_norm_g', 'conv_w', 'conv_b', 'lru_ba', 'lru_bx', 'lru_lambda',
          'final_norm_g']


def _pad_rows(a, rows):
    return jnp.pad(a, ((0, rows - a.shape[0]),) + ((0, 0),) * (a.ndim - 1))


def _step(w, m, v, x, c, ctx, loss_target):
    xi, yi, ci = _mesh_pos()
    me = 4 * xi + 2 * yi + ci
    idx = jnp.stack([ci, 2 * xi + yi]).astype(jnp.int32)
    S, D = x.shape[1:]
    Ds = D // ND
    cols = w['w_mod'].shape[-1]

    sp = jnp.concatenate([w['norm_g'][0], w['conv_w'][0], w['lru_ba'][0], w['lru_bx'][0], w['lru_lambda'][0]], axis=0)
    spg = _all_gather(_pad_rows(sp, 16), "ag_small_params", in_vmem=True)
    spf = jnp.transpose(spg, (1, 0, 2)).reshape(16, D)
    ng, conv_w, ba, bx, lam = spf[0:3], spf[3:7], spf[7:9], spf[9:11], spf[11:13]

    cg = _all_gather(_pad_rows(c, 8), "ag_cond", in_vmem=True)
    cc16 = _pad_rows(jnp.concatenate([cg[:, 0, :], w['c_ctx'][None, :]], axis=0), 16)
    b_loc = lax.dynamic_slice_in_dim(w['b_mod'], me * cols, cols, axis=1)
    mod_loc = _mod_fwd(cc16, w['w_mod'][0], b_loc, "mod_fwd")
    modg = _all_gather(mod_loc, "ag_mod", in_vmem=True)
    mod16 = jnp.transpose(modg, (1, 0, 2)).reshape(16, ND * cols)
    modx = lax.dynamic_index_in_dim(mod16, me, axis=0, keepdims=False).reshape(9, D)
    modc = mod16[8].reshape(9, D)

    shards = {'w_in': w['w_in'][0].astype(_MXU), 'w_out': w['w_out'][0].astype(_MXU)}
    for layer in range(2):
        shards[f'wg{layer}'] = w['ffn_wg'][0, layer].T.astype(_MXU)[None]
        shards[f'wu{layer}'] = w['ffn_wu'][0, layer].T.astype(_MXU)[None]
        shards[f'wd{layer}'] = w['ffn_wd'][0, layer].astype(_MXU)[None]
    WA, WX = w['lru_wa'][0].astype(_MXU), w['lru_wx'][0].astype(_MXU)
    big = ('ffn_wg', 'ffn_wu', 'ffn_wd', 'w_in', 'w_out')
    transposed = ('ffn_wg', 'ffn_wu')
    tr_view = lambda a: jnp.swapaxes(a, -1, -2)
    opt = {n: tuple((tr_view(a[n]) if n in transposed else a[n]) for a in (w, m, v)) for n in big}
    opt = {n: tuple(a.reshape(-1, a.shape[-1]) for a in t) for n, t in opt.items()}

    g = _local_step(x[0], ctx[0], loss_target[0], modx, modc, ng, shards, w['q_norm_g'], w['k_norm_g'],
                    conv_w, w['conv_b'], WA, WX, ba, bx, lam, w['final_norm_g'][None, :], idx, opt)

    grad, delta, new_m, new_v = {}, {}, {}, {}

    lfull = g['lru_w'].reshape((2,) + w['lru_wa'].shape[1:])
    for i, n in enumerate(('lru_wa', 'lru_wx')):
        shard = w[n].shape
        view = lambda a: a.reshape(-1, HD)
        grad[n] = lfull[i].reshape(shard)
        outs = _adamw_plain(view(w[n]), view(lfull[i]), view(m[n]), view(v[n]), f"adamw_{n}")
        delta[n], new_m[n], new_v[n] = [o.reshape(shard) for o in outs]

    dm = _pad_rows(jnp.stack([g['dmodx'].reshape(-1), g['dmodc'].reshape(-1)]), 8)
    dm_all = _all_gather(dm, "ag_dmod", in_vmem=True)
    dm_loc = lax.dynamic_slice_in_dim(dm_all, me * cols, cols, axis=2)
    gw_mod, dsil = _mod_bwd(dm_loc, cc16, w['w_mod'][0], "mod_bwd")
    grad['w_mod'] = gw_mod[None]
    T_wu0, own_wu0, b_wu0, fin_wu = g['wu0_pending']
    outs, ((b_wu0d,),) = _adamw_plain(w['w_mod'][0], gw_mod, m['w_mod'][0], v['w_mod'][0], "adamw_w_mod",
                                      jobs=[_rs_chips_job(T_wu0, (2,))])
    delta['w_mod'], new_m['w_mod'], new_v['w_mod'] = [o[None] for o in outs]
    g['ffn_wu'] = _rs_finish(own_wu0, [(b_wu0, 0), (b_wu0, 1), (b_wu0d, 0)], "wu0_rs_finish", opt['ffn_wu'], 0, 2, fin_wu)
    for n in big:
        if n in transposed:
            shape_t = w[n].shape[:-2] + (w[n].shape[-1], w[n].shape[-2])
            grad[n], delta[n], new_m[n], new_v[n] = [tr_view(o.reshape(shape_t)) for o in g[n]]
        else:
            grad[n], delta[n], new_m[n], new_v[n] = [o.reshape(w[n].shape) for o in g[n]]
    grad['b_mod'] = _bmod_grad(dm_all, "bmod_grad")

    pad_d = lambda a: jnp.concatenate([a, jnp.zeros((1, D - a.shape[1]), F32)], axis=1)
    small = jnp.concatenate([g['norm_g'], g['conv_w'], g['conv_b'], g['lru_ba'], g['lru_bx'], g['lru_lambda'],
                             g['final_norm_g'], dsil[8:9], pad_d(g['q_norm_g']), pad_d(g['k_norm_g'])], axis=0)
    parts = _all_gather(_pad_rows(small, _SMALL_ROWS), "ag_small_grads", in_vmem=True)
    ssum = _small_finish(parts, w['c_ctx'][None, :], "small_finish")
    mine = lambda rows: lax.dynamic_slice_in_dim(rows, me * Ds, Ds, axis=1)
    grad['norm_g'] = mine(ssum[0:3])[None]
    grad['conv_w'] = mine(ssum[3:7])[None]
    grad['conv_b'] = ssum[7:8]
    grad['lru_ba'] = mine(ssum[8:10])[None]
    grad['lru_bx'] = mine(ssum[10:12])[None]
    grad['lru_lambda'] = mine(ssum[12:14])[None]
    grad['final_norm_g'] = ssum[14]
    grad['c_ctx'] = ssum[_ROW_CCTX]
    grad['q_norm_g'] = ssum[16:17, :HD]
    grad['k_norm_g'] = ssum[17:18, :HD]

    def pack(d):
        flat = jnp.concatenate([d[n].reshape(-1) for n in _SMALL])
        padded = -(-flat.shape[0] // 1024) * 1024
        return jnp.concatenate([flat, jnp.zeros((padded - flat.shape[0],), F32)]).reshape(-1, HD)

    outs = _adamw_plain(pack(w), pack(grad), pack(m), pack(v), "adamw_small")
    off = 0
    for n in _SMALL:
        size = math.prod(w[n].shape)
        for dst, o in zip((delta, new_m, new_v), outs):
            dst[n] = o.reshape(-1)[off:off + size].reshape(w[n].shape)
        off += size

    loss = lax.psum(g['loss'][0, 0], ("x", "y", "c"))
    return (loss, g['grad_x'][None], *[grad[n] for n in _WEIGHTS], *[delta[n] for n in _WEIGHTS],
            *[new_m[n] for n in _WEIGHTS], *[new_v[n] for n in _WEIGHTS])


def kernel(x, c, ctx, c_ctx, w_mod, b_mod, norm_g, ffn_wg, ffn_wu, ffn_wd, w_in, w_out, q_norm_g, k_norm_g, conv_w, conv_b, lru_wa, lru_ba, lru_wx, lru_bx, lru_lambda, final_norm_g, loss_target, m_c_ctx, m_w_mod, m_b_mod, m_norm_g, m_ffn_wg, m_ffn_wu, m_ffn_wd, m_w_in, m_w_out, m_q_norm_g, m_k_norm_g, m_conv_w, m_conv_b, m_lru_wa, m_lru_ba, m_lru_wx, m_lru_bx, m_lru_lambda, m_final_norm_g, v_c_ctx, v_w_mod, v_b_mod, v_norm_g, v_ffn_wg, v_ffn_wu, v_ffn_wd, v_w_in, v_w_out, v_q_norm_g, v_k_norm_g, v_conv_w, v_conv_b, v_lru_wa, v_lru_ba, v_lru_wx, v_lru_bx, v_lru_lambda, v_final_norm_g):
    given = dict(locals())
    w = {n: given[n] for n in _WEIGHTS}
    m = {n: given["m_" + n] for n in _WEIGHTS}
    v = {n: given["v_" + n] for n in _WEIGHTS}
    return _step(w, m, v, x, c, ctx, loss_target)
```
